```python
import math
import jax, jax.numpy as jnp
from jax import lax
import numpy as np

D_MODEL = 1024
BATCH = 32
SEQ = 256
DEPTH = 2
DEC_BATCH = 8
DEC_SEQ = 2048
PAST_LEN = 512

GRID_W = 64
ROPE_BASE = 10000.0
NORM_EPS = 1e-6
N_MIXERS = 2
N_ATTN_LAYERS = (DEPTH + 1) // 2
N_RET_LAYERS = DEPTH // 2
DA_HEADS = 8
DA_HEAD_DIM = 64
DA_V_DIM = 2 * DA_HEAD_DIM
DA_QK_WIDTH = DA_HEADS * 2 * DA_HEAD_DIM
DA_QBLOCK = 128
RET_HEADS = 4
RET_DK = D_MODEL // RET_HEADS
RET_DV = 2 * RET_DK
RET_CHUNK = 128
N_EXPERTS = 16
N_GROUPS = 4
EXPERTS_PER_GROUP = N_EXPERTS // N_GROUPS
TOP_K = 2
D_EXPERT = D_MODEL // 2

kernel_name = 'hybrid_diffattn_retention_groupmoe_denoise_step'

F32 = jnp.float32


def rmsnorm(x, g):
    xf = x.astype(F32)
    y = xf * lax.rsqrt(jnp.mean(xf * xf, axis=-1, keepdims=True) + NORM_EPS) * g.astype(F32)
    return y.astype(x.dtype)


def head_groupnorm(o):
    mu = jnp.mean(o, axis=-1, keepdims=True)
    var = jnp.mean(jnp.square(o - mu), axis=-1, keepdims=True)
    return (o - mu) * lax.rsqrt(var + NORM_EPS)


def ada_modulation(cond, w, b):
    m = jax.nn.silu(cond) @ w + b
    return [t[:, None, :] for t in jnp.split(m, 6, axis=-1)]


def axial_rope(x):
    n, d = x.shape[1], x.shape[-1]
    n_rows = n // GRID_W
    row = jnp.repeat(jnp.arange(n_rows), GRID_W).astype(F32)
    col = jnp.tile(jnp.arange(GRID_W), n_rows).astype(F32)
    nf = d // 4
    freqs = jnp.power(ROPE_BASE, -jnp.arange(nf, dtype=F32) / nf)
    ang = jnp.concatenate([row[:, None] * freqs, col[:, None] * freqs], axis=-1)
    shape = (n,) + (1,) * (x.ndim - 3) + (d // 2,)
    cos = jnp.cos(ang).reshape(shape)
    sin = jnp.sin(ang).reshape(shape)
    xf = x.astype(F32)
    x1, x2 = xf[..., : d // 2], xf[..., d // 2:]
    return jnp.concatenate([x1 * cos - x2 * sin, x2 * cos + x1 * sin], axis=-1).astype(x.dtype)


def diff_attn_blocks(q, k, v, lam):
    B, N = q.shape[:2]
    nb = N // DA_QBLOCK
    qb = q.reshape(B, nb, DA_QBLOCK, DA_HEADS, 2, DA_HEAD_DIM).swapaxes(0, 1)
    scale = DA_HEAD_DIM ** -0.5

    def block(qi):
        s = jnp.einsum('bqhcd,bkhcd->bchqk', qi, k, preferred_element_type=F32) * scale
        p = jax.nn.softmax(s, axis=-1)
        a = p[:, 0] - lam * p[:, 1]
        return jnp.einsum('bhqk,bkhe->bqhe', a.astype(v.dtype), v)

    o = lax.map(block, qb)
    return o.swapaxes(0, 1).reshape(B, N, DA_HEADS, DA_V_DIM)


def diff_attention(h, w_qkv, lq1, lk1, lq2, lk2, subln_g, w_o, lam_init, cached):
    B, N, _ = h.shape
    qkv = h @ w_qkv
    q = qkv[..., :DA_QK_WIDTH].reshape(B, N, DA_HEADS, 2, DA_HEAD_DIM)
    k = qkv[..., DA_QK_WIDTH:2 * DA_QK_WIDTH].reshape(B, N, DA_HEADS, 2, DA_HEAD_DIM)
    v = qkv[..., 2 * DA_QK_WIDTH:].reshape(B, N, DA_HEADS, DA_V_DIM)
    if cached is None:
        keys, vals = k, v
    else:
        q = axial_rope(q)
        keys = jnp.concatenate([cached[0], axial_rope(k)], axis=1)
        vals = jnp.concatenate([cached[1], v], axis=1)
    lam = (jnp.exp(jnp.sum(lq1.astype(F32) * lk1.astype(F32)))
           - jnp.exp(jnp.sum(lq2.astype(F32) * lk2.astype(F32))) + lam_init)
    o = diff_attn_blocks(q, keys, vals, lam)
    o = rmsnorm(o, subln_g) * (1.0 - lam_init)
    return o.reshape(B, N, DA_HEADS * DA_V_DIM) @ w_o, (k, v)


def retention_scan(q, k, v, log_g, s0):
    B, N, H, _ = q.shape
    C = RET_CHUNK
    nc = N // C

    def chunks(t):
        return t.astype(F32).reshape(B, nc, C, H, t.shape[-1]).swapaxes(0, 1)

    qc, kc, vc = chunks(q), chunks(k), chunks(v)
    idx = jnp.arange(C, dtype=F32)
    rel = idx[:, None] - idx[None, :]
    intra = jnp.exp(jnp.where(rel[None] >= 0, rel[None] * log_g[:, None, None], -jnp.inf))
    q_decay = jnp.exp((idx[:, None] + 1.0) * log_g[None, :])[None, :, :, None]
    k_decay = jnp.exp((C - 1.0 - idx)[:, None] * log_g[None, :])[None, :, :, None]
    chunk_decay = jnp.exp(C * log_g)[None, :, None, None]

    def step(s, inp):
        qi, ki, vi = inp
        a = jnp.einsum('bihd,bjhd->bhij', qi, ki) * intra
        o = (jnp.einsum('bhij,bjhe->bihe', a, vi)
             + jnp.einsum('bihd,bhde->bihe', qi, s) * q_decay)
        s = chunk_decay * s + jnp.einsum('bjhd,bjhe->bhde', ki * k_decay, vi)
        return s, o

    s, o = lax.scan(step, s0.astype(F32), (qc, kc, vc))
    return o.swapaxes(0, 1).reshape(B, N, H, v.shape[-1]), s


def retention(h, w_qkv, w_g_f, w_g_b, decay_f, decay_b, w_o, cached):
    B, N, _ = h.shape
    kd = RET_HEADS * RET_DK
    qkv = h @ w_qkv
    q = qkv[..., :kd].reshape(B, N, RET_HEADS, RET_DK)
    k = qkv[..., kd:2 * kd].reshape(B, N, RET_HEADS, RET_DK) * (RET_DK ** -0.5)
    v = qkv[..., 2 * kd:].reshape(B, N, RET_HEADS, RET_DV)
    if cached is None:
        s0f = jnp.zeros((B, RET_HEADS, RET_DK, RET_DV), F32)
        s0b = s0f
    else:
        q, k = axial_rope(q), axial_rope(k)
        s0f, s0b = cached
    lg_f = jax.nn.log_sigmoid(decay_f.astype(F32))
    lg_b = jax.nn.log_sigmoid(decay_b.astype(F32))
    o_f, s_f = retention_scan(q, k, v, lg_f, s0f)
    o_b, s_b = retention_scan(q[:, ::-1], k[:, ::-1], v[:, ::-1], lg_b, s0b)
    o_b = o_b[:, ::-1]
    g_f = jax.nn.silu(h @ w_g_f).reshape(B, N, RET_HEADS, RET_DV).astype(F32)
    g_b = jax.nn.silu(h @ w_g_b).reshape(B, N, RET_HEADS, RET_DV).astype(F32)
    o = (head_groupnorm(o_f) * g_f + head_groupnorm(o_b) * g_b).astype(h.dtype)
    return o.reshape(B, N, RET_HEADS * RET_DV) @ w_o, (s_f.astype(h.dtype), s_b.astype(h.dtype))


def group_moe(h, w_router, router_bias, w_gate, w_up, w_down):
    B, N, D = h.shape
    t = h.reshape(B * N, D)
    s = jax.nn.sigmoid((t @ w_router).astype(F32))
    sel = s + router_bias.astype(F32)
    grp_score = lax.top_k(sel.reshape(-1, N_GROUPS, EXPERTS_PER_GROUP), 2)[0].sum(-1)
    best = jnp.argmax(grp_score, axis=-1)
    in_group = (jnp.arange(N_EXPERTS) // EXPERTS_PER_GROUP)[None, :] == best[:, None]
    _, idx = lax.top_k(jnp.where(in_group, sel, -jnp.inf), TOP_K)
    w = jnp.take_along_axis(s, idx, axis=-1)
    w = w / jnp.sum(w, axis=-1, keepdims=True)
    gates = jnp.sum(jax.nn.one_hot(idx, N_EXPERTS, dtype=F32) * w[..., None], axis=1)

    def expert(y, p):
        wg, wu, wd, ge = p
        a = jax.nn.silu(t @ wg) * (t @ wu) * ge[:, None].astype(t.dtype)
        return y + a @ wd, None

    y, _ = lax.scan(expert, jnp.zeros_like(t), (w_gate, w_up, w_down, gates.T))
    return y.reshape(B, N, D)


def trunk(x, cond, ctx_k, ctx_v, ctx_sf, ctx_sb, w_ada, b_ada, norm_mix_g, norm_ffn_g,
          final_norm_g, da_w_qkv, da_lambda_q1, da_lambda_k1, da_lambda_q2, da_lambda_k2,
          da_subln_g, da_w_o, ret_w_qkv, ret_w_gate_fwd, ret_w_gate_bwd, ret_decay_fwd,
          ret_decay_bwd, ret_w_o, w_router, router_bias, moe_w_gate, moe_w_up, moe_w_down):
    is_ctx = ctx_k is None
    new_k, new_v, new_sf, new_sb = [], [], [], []
    for i in range(DEPTH):
        sh1, sc1, g1, sh2, sc2, g2 = ada_modulation(cond, w_ada[i], b_ada[i])
        h = rmsnorm(x, norm_mix_g[i]) * (1 + sc1) + sh1
        j = i // N_MIXERS
        if i % N_MIXERS == 0:
            lam_init = 0.8 - 0.6 * math.exp(-0.3 * i)
            cached = None if is_ctx else (ctx_k[:, j], ctx_v[:, j])
            mix, (k, v) = diff_attention(h, da_w_qkv[j], da_lambda_q1[j], da_lambda_k1[j],
                                         da_lambda_q2[j], da_lambda_k2[j], da_subln_g[j],
                                         da_w_o[j], lam_init, cached)
            if is_ctx:
                new_k.append(k)
                new_v.append(v)
        else:
            cached = None if is_ctx else (ctx_sf[:, j], ctx_sb[:, j])
            mix, (sf, sb) = retention(h, ret_w_qkv[j], ret_w_gate_fwd[j], ret_w_gate_bwd[j],
                                      ret_decay_fwd[j], ret_decay_bwd[j], ret_w_o[j], cached)
            if is_ctx:
                new_sf.append(sf)
                new_sb.append(sb)
        x = x + g1 * mix
        h = rmsnorm(x, norm_ffn_g[i]) * (1 + sc2) + sh2
        x = x + g2 * group_moe(h, w_router, router_bias, moe_w_gate[i], moe_w_up[i], moe_w_down[i])
    return rmsnorm(x, final_norm_g), new_k, new_v, new_sf, new_sb


def setup_inputs(seed: int = 0) -> dict:
    key = jax.random.key(seed)
    ks = iter(jax.random.split(key, 40))

    def nrm(shape, scale):
        return jax.random.normal(next(ks), shape, F32) * scale

    D = D_MODEL
    dsd = D ** -0.5
    base_decay = jnp.log(2.0 ** (5.0 + jnp.arange(RET_HEADS, dtype=F32)) - 1.0)
    return {
        'x_prompt': nrm((BATCH, SEQ, D), 1.0),
        'x_sample': nrm((DEC_BATCH, DEC_SEQ, D), 1.0),
        'cache_attn_k': nrm((DEC_BATCH, N_ATTN_LAYERS, PAST_LEN, DA_HEADS, 2, DA_HEAD_DIM), 1.0),
        'cache_attn_v': nrm((DEC_BATCH, N_ATTN_LAYERS, PAST_LEN, DA_HEADS, DA_V_DIM), 1.0),
        'state_ret_fwd': nrm((DEC_BATCH, N_RET_LAYERS, RET_HEADS, RET_DK, RET_DV), 0.3),
        'state_ret_bwd': nrm((DEC_BATCH, N_RET_LAYERS, RET_HEADS, RET_DK, RET_DV), 0.3),
        'c': nrm((DEC_BATCH, D), 1.0),
        'c_ctx': nrm((D,), 1.0),
        'w_ada': nrm((DEPTH, D, 6 * D), 0.5 * dsd),
        'b_ada': nrm((DEPTH, 6 * D), 0.02),
        'norm_mix_g': 1.0 + nrm((DEPTH, D), 0.05),
        'norm_ffn_g': 1.0 + nrm((DEPTH, D), 0.05),
        'final_norm_g': 1.0 + nrm((D,), 0.05),
        'da_w_qkv': nrm((N_ATTN_LAYERS, D, 2 * DA_QK_WIDTH + DA_HEADS * DA_V_DIM), dsd),
        'da_lambda_q1': nrm((N_ATTN_LAYERS, DA_HEAD_DIM), 0.1),
        'da_lambda_k1': nrm((N_ATTN_LAYERS, DA_HEAD_DIM), 0.1),
        'da_lambda_q2': nrm((N_ATTN_LAYERS, DA_HEAD_DIM), 0.1),
        'da_lambda_k2': nrm((N_ATTN_LAYERS, DA_HEAD_DIM), 0.1),
        'da_subln_g': 1.0 + nrm((N_ATTN_LAYERS, DA_V_DIM), 0.05),
        'da_w_o': nrm((N_ATTN_LAYERS, DA_HEADS * DA_V_DIM, D), (DA_HEADS * DA_V_DIM) ** -0.5),
        'ret_w_qkv': nrm((N_RET_LAYERS, D, 2 * RET_HEADS * RET_DK + RET_HEADS * RET_DV), dsd),
        'ret_w_gate_fwd': nrm((N_RET_LAYERS, D, RET_HEADS * RET_DV), dsd),
        'ret_w_gate_bwd': nrm((N_RET_LAYERS, D, RET_HEADS * RET_DV), dsd),
        'ret_decay_fwd': base_decay[None, :] + nrm((N_RET_LAYERS, RET_HEADS), 0.01),
        'ret_decay_bwd': base_decay[None, :] + nrm((N_RET_LAYERS, RET_HEADS), 0.01),
        'ret_w_o': nrm((N_RET_LAYERS, RET_HEADS * RET_DV, D), (RET_HEADS * RET_DV) ** -0.5),
        'w_router': nrm((D, N_EXPERTS), dsd),
        'router_bias': nrm((N_EXPERTS,), 0.01),
        'moe_w_gate': nrm((DEPTH, N_EXPERTS, D, D_EXPERT), dsd),
        'moe_w_up': nrm((DEPTH, N_EXPERTS, D, D_EXPERT), dsd),
        'moe_w_down': nrm((DEPTH, N_EXPERTS, D_EXPERT, D), D_EXPERT ** -0.5),
    }


def reference(x_prompt, x_sample, cache_attn_k, cache_attn_v, state_ret_fwd, state_ret_bwd,
              c, c_ctx, w_ada, b_ada, norm_mix_g, norm_ffn_g, final_norm_g, da_w_qkv,
              da_lambda_q1, da_lambda_k1, da_lambda_q2, da_lambda_k2, da_subln_g, da_w_o,
              ret_w_qkv, ret_w_gate_fwd, ret_w_gate_bwd, ret_decay_fwd, ret_decay_bwd, ret_w_o,
              w_router, router_bias, moe_w_gate, moe_w_up, moe_w_down):
    weights = (w_ada, b_ada, norm_mix_g, norm_ffn_g, final_norm_g, da_w_qkv, da_lambda_q1,
               da_lambda_k1, da_lambda_q2, da_lambda_k2, da_subln_g, da_w_o, ret_w_qkv,
               ret_w_gate_fwd, ret_w_gate_bwd, ret_decay_fwd, ret_decay_bwd, ret_w_o,
               w_router, router_bias, moe_w_gate, moe_w_up, moe_w_down)
    y_prompt, ks, vs, sfs, sbs = trunk(x_prompt, c_ctx[None, :], None, None, None, None, *weights)
    y_sample, _, _, _, _ = trunk(x_sample, c, cache_attn_k, cache_attn_v, state_ret_fwd,
                                 state_ret_bwd, *weights)
    new_attn_k = jnp.stack(ks, axis=1)
    new_attn_v = jnp.stack(vs, axis=1)
    new_ret_fwd = jnp.stack(sfs, axis=1)
    new_ret_bwd = jnp.stack(sbs, axis=1)
    return (y_prompt, y_sample, new_attn_k, new_attn_v, new_ret_fwd, new_ret_bwd)
```

```python
import functools
import math

import jax
import jax.numpy as jnp
from jax import lax
from jax.experimental import pallas as pl
from jax.experimental.pallas import tpu as pltpu

F32 = jnp.float32
BF16 = jnp.bfloat16

GRID_W = 64
ROPE_BASE = 10000.0
NORM_EPS = 1e-6
DA_HEADS = 8
DA_HEAD_DIM = 64
DA_V_DIM = 2 * DA_HEAD_DIM
RET_HEADS = 4
RET_CHUNK = 128
N_EXPERTS = 16
N_GROUPS = 4
EXPERTS_PER_GROUP = N_EXPERTS // N_GROUPS
N_MOD = 6
MOD_ROWS = 16

V7X_VMEM_LIMIT = 56 * 1024 * 1024


def _cparams(*sem):
    return pltpu.CompilerParams(dimension_semantics=sem, vmem_limit_bytes=V7X_VMEM_LIMIT)


def _nt_dot(a, b):
    return lax.dot_general(a, b, (((1,), (1,)), ((), ())), preferred_element_type=F32)


def _tn_dot(a, b):
    return lax.dot_general(a, b, (((0,), (0,)), ((), ())), preferred_element_type=F32)


def _ada_kernel(c_ref, w_ref, b_ref, o_ref):
    s = jax.nn.silu(c_ref[...]).astype(BF16)
    acc = jnp.dot(s, w_ref[0].astype(BF16), preferred_element_type=F32)
    o_ref[0] = acc + b_ref[0]


def ada_modulation(cond, w_ada, b_ada):
    depth, d, n = w_ada.shape
    tn = 1536
    out = pl.pallas_call(
        _ada_kernel,
        out_shape=jax.ShapeDtypeStruct((depth, MOD_ROWS, n), F32),
        grid=(depth, n // tn),
        in_specs=[
            pl.BlockSpec((MOD_ROWS, d), lambda l, j: (0, 0)),
            pl.BlockSpec((1, d, tn), lambda l, j: (l, 0, j)),
            pl.BlockSpec((1, 1, tn), lambda l, j: (l, 0, j)),
        ],
        out_specs=pl.BlockSpec((1, MOD_ROWS, tn), lambda l, j: (l, 0, j)),
        compiler_params=_cparams("parallel", "parallel"),
        name="ada_modulation",
    )(cond, w_ada, b_ada.reshape(depth, 1, n))
    return out.reshape(depth, MOD_ROWS, N_MOD, d)


def _modulated_norm(x, g, mod_ref, shift_idx):
    ms = jnp.mean(x * x, axis=-1, keepdims=True)
    y = x * lax.rsqrt(ms + NORM_EPS) * g
    return y * (1.0 + mod_ref[0, shift_idx + 1:shift_idx + 2, :]) + mod_ref[0, shift_idx:shift_idx + 1, :]


def _rope64(a, cos, sin_signed, first_half):
    partner = jnp.where(first_half, pltpu.roll(a, 96, 1), pltpu.roll(a, 32, 1))
    return a * cos + partner * sin_signed


def _nmm_kernel(*refs, segs, rope, tn):
    n_out = len(segs)
    x_ref, g_ref, mod_ref, w_ref = refs[:4]
    pos = 4
    if rope:
        cos_ref, sin_ref = refs[4:6]
        pos = 6
    out_refs = refs[pos:pos + n_out]
    h_scr = refs[pos + n_out]
    j = pl.program_id(1)

    @pl.when(j == 0)
    def _():
        h_scr[...] = _modulated_norm(x_ref[...], g_ref[...], mod_ref, 0).astype(BF16)

    acc = jnp.dot(h_scr[...], w_ref[...], preferred_element_type=F32)
    tm = acc.shape[0]

    for o_ref, (lo, hi, kind, scale) in zip(out_refs, segs):

        @pl.when((j >= lo) & (j < hi))
        def _(o_ref=o_ref, kind=kind, scale=scale):
            a = acc if scale == 1.0 else acc * scale
            if kind == "plain":
                o_ref[...] = a.astype(o_ref.dtype)
            elif kind == "silu":
                o_ref[...] = jax.nn.silu(a).astype(o_ref.dtype)
            elif kind == "rope64":
                cos, sin = cos_ref[...], sin_ref[...]
                lane = lax.broadcasted_iota(jnp.int32, (tm, 128), 1)
                first_half = (lane & 32) == 0
                for c in range(tn // 128):
                    sl = slice(c * 128, (c + 1) * 128)
                    o_ref[:, sl] = _rope64(a[:, sl], cos, sin, first_half).astype(o_ref.dtype)
            elif kind == "rope256":
                cos, sin = cos_ref[...], sin_ref[...]
                for c in range(tn // 256):
                    s1 = slice(c * 256, c * 256 + 128)
                    s2 = slice(c * 256 + 128, (c + 1) * 256)
                    x1, x2 = a[:, s1], a[:, s2]
                    o_ref[:, s1] = (x1 * cos - x2 * sin).astype(o_ref.dtype)
                    o_ref[:, s2] = (x2 * cos + x1 * sin).astype(o_ref.dtype)
            else:
                raise ValueError(kind)


def norm_mod_matmul(x, g, mods, mod_base, rows_per_mod, w, segments, rope_tables=None, seq_len=None,
                    tm=512, tn=1024):
    t, d = x.shape
    tm = min(tm, rows_per_mod)
    n_total = w.shape[1]
    assert t % tm == 0 and n_total % tn == 0 and rows_per_mod % tm == 0
    segs, out_shapes, out_specs = [], [], []
    col = 0
    for width, kind, scale, dtype in segments:
        assert width % tn == 0 and col % tn == 0
        lo, n_blk = col // tn, width // tn
        segs.append((lo, lo + n_blk, kind, float(scale)))
        out_shapes.append(jax.ShapeDtypeStruct((t, width), dtype))
        out_specs.append(pl.BlockSpec(
            (tm, tn), lambda i, j, lo=lo, n_blk=n_blk: (i, jnp.clip(j - lo, 0, n_blk - 1))))
        col += width
    assert col == n_total
    rope = rope_tables is not None
    in_specs = [
        pl.BlockSpec((tm, d), lambda i, j: (i, 0)),
        pl.BlockSpec((1, d), lambda i, j: (0, 0)),
        pl.BlockSpec((1, N_MOD, d), lambda i, j: (mod_base + (i * tm) // rows_per_mod, 0, 0)),
        pl.BlockSpec((d, tn), lambda i, j: (0, j)),
    ]
    args = [x, g.reshape(1, d), mods, w]
    if rope:
        assert seq_len % tm == 0
        blocks_per_seq = seq_len // tm
        for tab in rope_tables:
            in_specs.append(pl.BlockSpec((tm, 128), lambda i, j: (i % blocks_per_seq, 0)))
            args.append(tab)
    return pl.pallas_call(
        functools.partial(_nmm_kernel, segs=tuple(segs), rope=rope, tn=tn),
        out_shape=out_shapes,
        grid=(t // tm, n_total // tn),
        in_specs=in_specs,
        out_specs=out_specs,
        scratch_shapes=[pltpu.VMEM((tm, d), BF16)],
        compiler_params=_cparams("parallel", "arbitrary"),
        name="norm_mod_matmul",
    )(*args)


def _diff_attn_kernel(*refs, lam_init, has_cache):
    lam_ref, q_ref, k_ref, v_ref = refs[:4]
    pos = 4
    if has_cache:
        ck_ref, cv_ref = refs[4:6]
        pos = 6
    g_ref, o_ref = refs[pos], refs[pos + 1]

    lp = lam_ref[...]
    lam = (jnp.exp(jnp.sum(lp[0:1] * lp[1:2], axis=-1, keepdims=True))
           - jnp.exp(jnp.sum(lp[2:3] * lp[3:4], axis=-1, keepdims=True)) + lam_init)

    q = q_ref[...].astype(F32) * (DA_HEAD_DIM ** -0.5)
    lane = lax.broadcasted_iota(jnp.int32, q.shape, 1)
    kn = k_ref[...].astype(BF16)
    vn = v_ref[...].astype(BF16)
    if has_cache:
        kc = ck_ref[0].astype(BF16)
        vc = cv_ref[0].astype(BF16)

    probs = []
    for comp in range(2):
        in_comp = (lane < DA_HEAD_DIM) if comp == 0 else (lane >= DA_HEAD_DIM)
        qc = jnp.where(in_comp, q, 0.0).astype(BF16)
        sn = _nt_dot(qc, kn)
        m = jnp.max(sn, axis=-1, keepdims=True)
        if has_cache:
            sc = _nt_dot(qc, kc)
            m = jnp.maximum(m, jnp.max(sc, axis=-1, keepdims=True))
        en = jnp.exp(sn - m)
        denom = jnp.sum(en, axis=-1, keepdims=True)
        if has_cache:
            ec = jnp.exp(sc - m)
            denom = denom + jnp.sum(ec, axis=-1, keepdims=True)
            probs.append((en / denom, ec / denom))
        else:
            probs.append((en / denom, None))

    o = jnp.dot((probs[0][0] - lam * probs[1][0]).astype(BF16), vn, preferred_element_type=F32)
    if has_cache:
        o = o + jnp.dot((probs[0][1] - lam * probs[1][1]).astype(BF16), vc, preferred_element_type=F32)
    ms = jnp.mean(o * o, axis=-1, keepdims=True)
    o_ref[...] = (o * lax.rsqrt(ms + NORM_EPS) * g_ref[...]) * (1.0 - lam_init)


def diff_attention(q, k, v, lam_params, subln_g, lam_init, batch, seq_len, cache=None, tq=256):
    t, width = q.shape
    tq = min(tq, seq_len)
    nq = seq_len // tq
    hw = 2 * DA_HEAD_DIM
    in_specs = [
        pl.BlockSpec((4, DA_HEAD_DIM), lambda b, h, i: (0, 0)),
        pl.BlockSpec((tq, hw), lambda b, h, i: (b * nq + i, h)),
        pl.BlockSpec((seq_len, hw), lambda b, h, i: (b, h)),
        pl.BlockSpec((seq_len, hw), lambda b, h, i: (b, h)),
    ]
    args = [lam_params, q, k, v]
    if cache is not None:
        ck, cv = cache
        past = ck.shape[1]
        in_specs += [pl.BlockSpec((1, past, hw), lambda b, h, i: (b, 0, h)),
                     pl.BlockSpec((1, past, hw), lambda b, h, i: (b, 0, h))]
        args += [ck, cv]
    in_specs.append(pl.BlockSpec((1, hw), lambda b, h, i: (0, 0)))
    args.append(subln_g.reshape(1, hw))
    return pl.pallas_call(
        functools.partial(_diff_attn_kernel, lam_init=lam_init, has_cache=cache is not None),
        out_shape=jax.ShapeDtypeStruct((t, width), F32),
        grid=(batch, DA_HEADS, nq),
        in_specs=in_specs,
        out_specs=pl.BlockSpec((tq, hw), lambda b, h, i: (b * nq + i, h)),
        compiler_params=_cparams("parallel", "parallel", "arbitrary"),
        name="diff_attention",
    )(*args)


def _retention_kernel(*refs, seq_len, has_state, emit_state):
    lg_ref, q_ref, k_ref, v_ref, gate_ref = refs[:5]
    pos = 5
    if has_state:
        s0f_ref, s0b_ref = refs[5:7]
        pos = 7
    o_ref = refs[pos]
    pos += 1
    if emit_state:
        sf_ref, sb_ref = refs[pos:pos + 2]
        pos += 2
    s_scr, o_scr = refs[pos], refs[pos + 1]

    h = pl.program_id(1)
    direction = pl.program_id(2)
    c_len = RET_CHUNK
    n_chunks = seq_len // c_len
    row = lax.broadcasted_iota(jnp.int32, (c_len, c_len), 0)
    colm = lax.broadcasted_iota(jnp.int32, (c_len, c_len), 1)
    rel = (row - colm).astype(F32)
    posv = lax.broadcasted_iota(jnp.int32, (c_len, 1), 0).astype(F32)

    def run(backward):
        lg = lg_ref[1 if backward else 0, h]
        if backward:
            intra = jnp.where(rel <= 0, jnp.exp(-rel * lg), 0.0)
            q_decay = jnp.exp((c_len - posv) * lg)
            k_decay = jnp.exp(posv * lg)
        else:
            intra = jnp.where(rel >= 0, jnp.exp(rel * lg), 0.0)
            q_decay = jnp.exp((posv + 1.0) * lg)
            k_decay = jnp.exp((c_len - 1.0 - posv) * lg)
        chunk_decay = jnp.exp(jnp.zeros((1, 1), F32) + c_len * lg)

        if has_state:
            s_scr[...] = (s0b_ref if backward else s0f_ref)[0, 0, 0]
        else:
            s_scr[...] = jnp.zeros_like(s_scr)

        def body(ci, carry):
            c = (n_chunks - 1 - ci) if backward else ci
            r0 = pl.multiple_of(c * c_len, c_len)
            rows = pl.ds(r0, c_len)
            qb = q_ref[rows, :].astype(BF16)
            kf = k_ref[rows, :].astype(F32)
            vb = v_ref[rows, :].astype(BF16)
            a = _nt_dot(qb, kf.astype(BF16)) * intra
            s = s_scr[...]
            o = (jnp.dot(a.astype(BF16), vb, preferred_element_type=F32)
                 + jnp.dot(qb, s.astype(BF16), preferred_element_type=F32) * q_decay)
            s_scr[...] = chunk_decay * s + _tn_dot((kf * k_decay).astype(BF16), vb)
            mu = jnp.mean(o, axis=-1, keepdims=True)
            oc = o - mu
            var = jnp.mean(oc * oc, axis=-1, keepdims=True)
            gated = oc * lax.rsqrt(var + NORM_EPS) * gate_ref[rows, :].astype(F32)
            if backward:
                o_ref[rows, :] = o_scr[rows, :] + gated
            else:
                o_scr[rows, :] = gated
            return carry

        lax.fori_loop(0, n_chunks, body, 0)
        if emit_state:
            (sb_ref if backward else sf_ref)[0, 0, 0] = s_scr[...]

    @pl.when(direction == 0)
    def _():
        run(False)

    @pl.when(direction == 1)
    def _():
        run(True)


def retention(q, k, v, gates, log_decay, batch, seq_len, state=None, emit_state=False):
    t = q.shape[0]
    dk = q.shape[1] // RET_HEADS
    dv = v.shape[1] // RET_HEADS
    in_specs = [
        pl.BlockSpec(memory_space=pltpu.SMEM),
        pl.BlockSpec((seq_len, dk), lambda b, h, r: (b, h)),
        pl.BlockSpec((seq_len, dk), lambda b, h, r: (b, h)),
        pl.BlockSpec((seq_len, dv), lambda b, h, r: (b, h)),
        pl.BlockSpec((seq_len, dv), lambda b, h, r: (b, r * RET_HEADS + h)),
    ]
    args = [log_decay, q, k, v, gates]
    state_spec = pl.BlockSpec((1, 1, 1, dk, dv), lambda b, h, r: (b, 0, h, 0, 0))
    if state is not None:
        in_specs += [state_spec, state_spec]
        args += list(state)
    out_shapes = [jax.ShapeDtypeStruct((t, RET_HEADS * dv), F32)]
    out_specs = [pl.BlockSpec((seq_len, dv), lambda b, h, r: (b, h))]
    if emit_state:
        out_shapes += [jax.ShapeDtypeStruct((batch, 1, RET_HEADS, dk, dv), F32)] * 2
        out_specs += [state_spec, state_spec]
    kern = functools.partial(_retention_kernel, seq_len=seq_len, has_state=state is not None,
                             emit_state=emit_state)

    return pl.pallas_call(
        kern,
        out_shape=out_shapes,
        grid=(batch, RET_HEADS, 2),
        in_specs=in_specs,
        out_specs=out_specs,
        scratch_shapes=[pltpu.VMEM((dk, dv), F32), pltpu.VMEM((seq_len, dv), F32)],
        compiler_params=_cparams("parallel", "parallel", "arbitrary"),
        name="retention",
    )(*args)


def _proj_residual_kernel(x_ref, a_ref, w_ref, mod_ref, o_ref, *, gate_idx):
    y = jnp.dot(a_ref[...].astype(BF16), w_ref[...], preferred_element_type=F32)
    o_ref[...] = x_ref[...] + mod_ref[0, gate_idx:gate_idx + 1, :] * y


def proj_residual(x, a, w, mods, mod_base, rows_per_mod, gate_idx, tm=512):
    t, d = x.shape
    kdim = a.shape[1]
    tm = min(tm, rows_per_mod)
    return pl.pallas_call(
        functools.partial(_proj_residual_kernel, gate_idx=gate_idx),
        out_shape=jax.ShapeDtypeStruct((t, d), F32),
        grid=(t // tm,),
        in_specs=[
            pl.BlockSpec((tm, d), lambda i: (i, 0)),
            pl.BlockSpec((tm, kdim), lambda i: (i, 0)),
            pl.BlockSpec((kdim, d), lambda i: (0, 0)),
            pl.BlockSpec((1, N_MOD, d), lambda i: (mod_base + (i * tm) // rows_per_mod, 0, 0)),
        ],
        out_specs=pl.BlockSpec((tm, d), lambda i: (i, 0)),
        compiler_params=_cparams("parallel"),
        name="proj_residual",
    )(x, a, w, mods)


def _first_max_onehot(vals):
    m = vals[0]
    for v in vals[1:]:
        m = jnp.maximum(m, v)
    onehot, taken = [], None
    for v in vals:
        hit = v == m
        if taken is None:
            onehot.append(hit)
            taken = hit
        else:
            onehot.append(hit & jnp.logical_not(taken))
            taken = taken | hit
    return m, onehot


def _pick(onehot, vals):
    out = vals[-1]
    for oh, v in zip(onehot[-2::-1], vals[-2::-1]):
        out = jnp.where(oh, v, out)
    return out


def _router_kernel(x_ref, g_ref, mod_ref, wr_ref, bias_ref, h_ref, gates_ref):
    h = _modulated_norm(x_ref[...], g_ref[...], mod_ref, 3).astype(BF16)
    h_ref[...] = h
    logits = _nt_dot(wr_ref[...], h)
    s = jax.nn.sigmoid(logits)
    sel = s + bias_ref[...]
    neg_inf = jnp.full_like(sel[0:1], -jnp.inf)
    sel_rows = [sel[e:e + 1] for e in range(N_EXPERTS)]
    s_rows = [s[e:e + 1] for e in range(N_EXPERTS)]

    def top2(vals):
        m1, oh1 = _first_max_onehot(vals)
        rest = [jnp.where(o, neg_inf, v) for o, v in zip(oh1, vals)]
        m2, oh2 = _first_max_onehot(rest)
        return m1, m2, oh1, oh2

    grp_scores = []
    for g in range(N_GROUPS):
        m1, m2, _, _ = top2(sel_rows[g * EXPERTS_PER_GROUP:(g + 1) * EXPERTS_PER_GROUP])
        grp_scores.append(m1 + m2)
    _, in_grp = _first_max_onehot(grp_scores)
    cand_sel = [_pick(in_grp, [sel_rows[g * EXPERTS_PER_GROUP + k] for g in range(N_GROUPS)])
                for k in range(EXPERTS_PER_GROUP)]
    cand_s = [_pick(in_grp, [s_rows[g * EXPERTS_PER_GROUP + k] for g in range(N_GROUPS)])
              for k in range(EXPERTS_PER_GROUP)]
    _, _, oh1, oh2 = top2(cand_sel)
    w1 = _pick(oh1, cand_s)
    w2 = _pick(oh2, cand_s)
    denom = w1 + w2
    w1, w2 = w1 / denom, w2 / denom
    zero = jnp.zeros_like(w1)
    for g in range(N_GROUPS):
        for k in range(EXPERTS_PER_GROUP):
            e = g * EXPERTS_PER_GROUP + k
            gate = jnp.where(oh1[k], w1, jnp.where(oh2[k], w2, zero))
            gates_ref[e:e + 1, :] = jnp.where(in_grp[g], gate, zero)


def route(x, g, mods, mod_base, rows_per_mod, w_router_t, router_bias, tm=512):
    t, d = x.shape
    tm = min(tm, rows_per_mod)
    return pl.pallas_call(
        _router_kernel,
        out_shape=[jax.ShapeDtypeStruct((t, d), BF16), jax.ShapeDtypeStruct((N_EXPERTS, t), F32)],
        grid=(t // tm,),
        in_specs=[
            pl.BlockSpec((tm, d), lambda i: (i, 0)),
            pl.BlockSpec((1, d), lambda i: (0, 0)),
            pl.BlockSpec((1, N_MOD, d), lambda i: (mod_base + (i * tm) // rows_per_mod, 0, 0)),
            pl.BlockSpec((N_EXPERTS, d), lambda i: (0, 0)),
            pl.BlockSpec((N_EXPERTS, 1), lambda i: (0, 0)),
        ],
        out_specs=[pl.BlockSpec((tm, d), lambda i: (i, 0)),
                   pl.BlockSpec((N_EXPERTS, tm), lambda i: (0, i))],
        compiler_params=_cparams("parallel"),
        name="router",
    )(x, g.reshape(1, d), mods, w_router_t, router_bias.reshape(N_EXPERTS, 1))


def _experts_kernel(h_ref, gate_ref, wg_ref, wu_ref, wd_ref, x_ref, mod_ref, fg_ref, o_ref, acc_scr, *,
                    final_norm):
    e = pl.program_id(1)

    @pl.when(e == 0)
    def _():
        acc_scr[...] = jnp.zeros_like(acc_scr)

    h = h_ref[...]
    a = (jax.nn.silu(jnp.dot(h, wg_ref[0], preferred_element_type=F32))
         * jnp.dot(h, wu_ref[0], preferred_element_type=F32) * gate_ref[0])
    acc_scr[...] += jnp.dot(a.astype(BF16), wd_ref[0], preferred_element_type=F32)

    @pl.when(e == pl.num_programs(1) - 1)
    def _():
        y = x_ref[...] + mod_ref[0, 5:6, :] * acc_scr[...]
        if final_norm:
            ms = jnp.mean(y * y, axis=-1, keepdims=True)
            y = y * lax.rsqrt(ms + NORM_EPS) * fg_ref[...]
        o_ref[...] = y


def experts_residual(x, h, gates_col, wg, wu, wd, mods, mod_base, rows_per_mod, final_g, final_norm, tm=512):
    t, d = x.shape
    n_e, _, de = wg.shape
    tm = min(tm, rows_per_mod)
    return pl.pallas_call(
        functools.partial(_experts_kernel, final_norm=final_norm),
        out_shape=jax.ShapeDtypeStruct((t, d), F32),
        grid=(t // tm, n_e),
        in_specs=[
            pl.BlockSpec((tm, d), lambda i, e: (i, 0)),
            pl.BlockSpec((1, tm, 1), lambda i, e: (e, i, 0)),
            pl.BlockSpec((1, d, de), lambda i, e: (e, 0, 0)),
            pl.BlockSpec((1, d, de), lambda i, e: (e, 0, 0)),
            pl.BlockSpec((1, de, d), lambda i, e: (e, 0, 0)),
            pl.BlockSpec((tm, d), lambda i, e: (i, 0)),
            pl.BlockSpec((1, N_MOD, d), lambda i, e: (mod_base + (i * tm) // rows_per_mod, 0, 0)),
            pl.BlockSpec((1, d), lambda i, e: (0, 0)),
        ],
        out_specs=pl.BlockSpec((tm, d), lambda i, e: (i, 0)),
        scratch_shapes=[pltpu.VMEM((tm, d), F32)],
        compiler_params=_cparams("parallel", "arbitrary"),
        name="experts_residual",
    )(h, gates_col, wg, wu, wd, x, mods, final_g.reshape(1, d))


def _rope_angles(n, d):
    n_rows = n // GRID_W
    row = jnp.repeat(jnp.arange(n_rows), GRID_W).astype(F32)
    col = jnp.tile(jnp.arange(GRID_W), n_rows).astype(F32)
    nf = d // 4
    freqs = jnp.power(ROPE_BASE, -jnp.arange(nf, dtype=F32) / nf)
    ang = jnp.concatenate([row[:, None] * freqs, col[:, None] * freqs], axis=-1)
    return jnp.cos(ang), jnp.sin(ang)


def _trunk(x, mods, mod_base, batch, seq_len, cache, p):
    t, d = x.shape
    is_ctx = cache is None
    rows_per_mod = t if is_ctx else seq_len
    depth = p["w_ada"].shape[0]
    act = F32 if is_ctx else BF16
    outs = {}
    for i in range(depth):
        m = mods[i]
        j = i // 2
        if i % 2 == 0:
            lam_init = 0.8 - 0.6 * math.exp(-0.3 * i)
            qkw = DA_HEADS * 2 * DA_HEAD_DIM
            rope = None
            if not is_ctx:
                cos, sin = _rope_angles(seq_len, DA_HEAD_DIM)
                rope = (jnp.tile(cos, (1, 4)), jnp.concatenate([-sin, sin, -sin, sin], axis=-1))
            kind = "plain" if is_ctx else "rope64"
            q, k, v = norm_mod_matmul(
                x, p["norm_mix_g"][i], m, mod_base, rows_per_mod, p["da_w_qkv"][j],
                [(qkw, kind, 1.0, BF16), (qkw, kind, 1.0, act), (DA_HEADS * DA_V_DIM, "plain", 1.0, act)],
                rope_tables=rope, seq_len=seq_len)
            lam_params = jnp.stack([p["da_lambda_q1"][j], p["da_lambda_k1"][j],
                                    p["da_lambda_q2"][j], p["da_lambda_k2"][j]])
            kv_cache = None if is_ctx else (cache[0][:, j], cache[1][:, j])
            mix = diff_attention(q, k, v, lam_params, p["da_subln_g"][j], lam_init, batch, seq_len,
                                 cache=kv_cache)
            x = proj_residual(x, mix, p["da_w_o"][j], m, mod_base, rows_per_mod, 2)
            if is_ctx:
                outs.setdefault("k", []).append(k)
                outs.setdefault("v", []).append(v)
        else:
            kd = p["ret_w_qkv"].shape[2] // 4
            dv = 2 * kd
            rope = None if is_ctx else _rope_angles(seq_len, kd // RET_HEADS)
            kind = "plain" if is_ctx else "rope256"
            k_scale = (kd // RET_HEADS) ** -0.5
            q, k, v, gates = norm_mod_matmul(
                x, p["norm_mix_g"][i], m, mod_base, rows_per_mod, p["ret_w_all"][j],
                [(kd, kind, 1.0, BF16), (kd, kind, k_scale, F32), (dv, "plain", 1.0, BF16),
                 (2 * dv, "silu", 1.0, F32)],
                rope_tables=rope, seq_len=seq_len)
            state = None if is_ctx else (cache[2][:, j:j + 1], cache[3][:, j:j + 1])
            res = retention(q, k, v, gates, p["ret_log_decay"][j], batch, seq_len, state=state,
                            emit_state=is_ctx)
            x = proj_residual(x, res[0], p["ret_w_o"][j], m, mod_base, rows_per_mod, 2)
            if is_ctx:
                outs.setdefault("sf", []).append(res[1])
                outs.setdefault("sb", []).append(res[2])
        h, gates = route(x, p["norm_ffn_g"][i], m, mod_base, rows_per_mod, p["w_router_t"], p["router_bias"])
        x = experts_residual(x, h, gates[:, :, None], p["moe_w_gate"][i], p["moe_w_up"][i], p["moe_w_down"][i],
                             m, mod_base, rows_per_mod, p["final_norm_g"], final_norm=(i == depth - 1))
    return x, outs


def kernel(x_prompt, x_sample, cache_attn_k, cache_attn_v, state_ret_fwd, state_ret_bwd, c, c_ctx, w_ada, b_ada, norm_mix_g, norm_ffn_g, final_norm_g, da_w_qkv, da_lambda_q1, da_lambda_k1, da_lambda_q2, da_lambda_k2, da_subln_g, da_w_o, ret_w_qkv, ret_w_gate_fwd, ret_w_gate_bwd, ret_decay_fwd, ret_decay_bwd, ret_w_o, w_router, router_bias, moe_w_gate, moe_w_up, moe_w_down):
    b_ctx, n_ctx, d = x_prompt.shape
    b_dec, n_dec, _ = x_sample.shape
    past = cache_attn_k.shape[2]
    assert b_dec + 1 <= MOD_ROWS

    cond = jnp.zeros((MOD_ROWS, d), F32).at[0].set(c_ctx).at[1:1 + b_dec].set(c)
    mods = ada_modulation(cond, w_ada, b_ada)

    p = {
        "w_ada": w_ada, "norm_mix_g": norm_mix_g, "norm_ffn_g": norm_ffn_g, "final_norm_g": final_norm_g,
        "da_w_qkv": da_w_qkv.astype(BF16), "da_w_o": da_w_o.astype(BF16),
        "da_lambda_q1": da_lambda_q1, "da_lambda_k1": da_lambda_k1,
        "da_lambda_q2": da_lambda_q2, "da_lambda_k2": da_lambda_k2, "da_subln_g": da_subln_g,
        "ret_w_qkv": ret_w_qkv,
        "ret_w_all": jnp.concatenate([ret_w_qkv, ret_w_gate_fwd, ret_w_gate_bwd], axis=-1).astype(BF16),
        "ret_w_o": ret_w_o.astype(BF16),
        "ret_log_decay": jnp.stack([jax.nn.log_sigmoid(ret_decay_fwd.astype(F32)),
                                    jax.nn.log_sigmoid(ret_decay_bwd.astype(F32))], axis=1),
        "w_router_t": w_router.T.astype(BF16), "router_bias": router_bias.astype(F32),
        "moe_w_gate": moe_w_gate.astype(BF16), "moe_w_up": moe_w_up.astype(BF16),
        "moe_w_down": moe_w_down.astype(BF16),
    }

    y_ctx, outs = _trunk(x_prompt.reshape(b_ctx * n_ctx, d), mods, 0, b_ctx, n_ctx, None, p)
    n_attn = cache_attn_k.shape[1]
    cache = (cache_attn_k.reshape(b_dec, n_attn, past, -1), cache_attn_v.reshape(b_dec, n_attn, past, -1),
             state_ret_fwd, state_ret_bwd)
    y_dec, _ = _trunk(x_sample.reshape(b_dec * n_dec, d), mods, 1, b_dec, n_dec, cache, p)

    new_k = jnp.stack([k.reshape(b_ctx, n_ctx, DA_HEADS, 2, DA_HEAD_DIM) for k in outs["k"]], axis=1)
    new_v = jnp.stack([v.reshape(b_ctx, n_ctx, DA_HEADS, DA_V_DIM) for v in outs["v"]], axis=1)
    new_sf = jnp.concatenate(outs["sf"], axis=1)
    new_sb = jnp.concatenate(outs["sb"], axis=1)
    return (y_ctx.reshape(b_ctx, n_ctx, d), y_dec.reshape(b_dec, n_dec, d), new_k, new_v, new_sf, new_sb)
```

```python
import functools
import math

import jax
import jax.numpy as jnp
from jax import lax
from jax.experimental import pallas as pl
from jax.experimental.pallas import tpu as pltpu
from jax.experimental.pallas import tpu_sc as plsc

F32 = jnp.float32
BF16 = jnp.bfloat16
I32 = jnp.int32

GRID_W = 64
ROPE_BASE = 10000.0
NORM_EPS = 1e-6
DA_HEADS = 8
DA_HEAD_DIM = 64
DA_V_DIM = 2 * DA_HEAD_DIM
RET_HEADS = 4
RET_CHUNK = 128
N_EXPERTS = 16
N_GROUPS = 4
EXPERTS_PER_GROUP = N_EXPERTS // N_GROUPS
TOP_K = 2
N_MOD = 6
MOD_ROWS = 16

V7X_VMEM_LIMIT = 56 * 1024 * 1024
V7X_SC_CORES = 2
V7X_SC_SUBCORES = 16
SC_WORKERS = V7X_SC_CORES * V7X_SC_SUBCORES
SC_CHUNK_ROWS = 32

ROW_TILE = 512
EXPERT_TILE = 512


def _cparams(*sem):
    return pltpu.CompilerParams(dimension_semantics=sem, vmem_limit_bytes=V7X_VMEM_LIMIT)


def _nt_dot(a, b):
    return lax.dot_general(a, b, (((1,), (1,)), ((), ())), preferred_element_type=F32)


def _tn_dot(a, b):
    return lax.dot_general(a, b, (((0,), (0,)), ((), ())), preferred_element_type=F32)


class Layout:
    def __init__(self, b_ctx, n_ctx, b_dec, n_dec):
        self.b_ctx, self.n_ctx, self.b_dec, self.n_dec = b_ctx, n_ctx, b_dec, n_dec
        self.t_ctx, self.t_dec = b_ctx * n_ctx, b_dec * n_dec
        self.t = self.t_ctx + self.t_dec
        self.tm = min(ROW_TILE, n_dec, self.t_ctx)
        assert self.t_ctx % self.tm == 0 and n_dec % self.tm == 0
        assert self.t_ctx % n_dec == 0 and self.t_ctx % n_ctx == 0

    def mod_row(self, i):
        r = i * self.tm
        return jnp.where(r < self.t_ctx, 0, 1 + (r - self.t_ctx) // self.n_dec)


def _ada_kernel(c_ref, w_ref, b_ref, o_ref):
    s = jax.nn.silu(c_ref[...]).astype(BF16)
    acc = jnp.dot(s, w_ref[0].astype(BF16), preferred_element_type=F32)
    o_ref[0] = acc + b_ref[0]


def ada_modulation(cond, w_ada, b_ada):
    depth, d, n = w_ada.shape
    tn = 1536
    out = pl.pallas_call(
        _ada_kernel,
        out_shape=jax.ShapeDtypeStruct((depth, MOD_ROWS, n), F32),
        grid=(depth, n // tn),
        in_specs=[
            pl.BlockSpec((MOD_ROWS, d), lambda l, j: (0, 0)),
            pl.BlockSpec((1, d, tn), lambda l, j: (l, 0, j)),
            pl.BlockSpec((1, 1, tn), lambda l, j: (l, 0, j)),
        ],
        out_specs=pl.BlockSpec((1, MOD_ROWS, tn), lambda l, j: (l, 0, j)),
        compiler_params=_cparams("parallel", "parallel"),
        name="ada_modulation",
    )(cond, w_ada, b_ada.reshape(depth, 1, n))
    return out.reshape(depth, MOD_ROWS, N_MOD, d)


def _modulated_norm(x, g, mod_ref, shift_idx):
    ms = jnp.mean(x * x, axis=-1, keepdims=True)
    y = x * lax.rsqrt(ms + NORM_EPS) * g
    return y * (1.0 + mod_ref[0, shift_idx + 1:shift_idx + 2, :]) + mod_ref[0, shift_idx:shift_idx + 1, :]


def _rope64(a, cos, sin_signed, first_half):
    partner = jnp.where(first_half, pltpu.roll(a, 96, 1), pltpu.roll(a, 32, 1))
    return a * cos + partner * sin_signed


def _nmm_kernel(x_ref, g_ref, mod_ref, w_ref, cos_ref, sin_ref, *rest, segs, tn, n_ctx_tiles):
    n_out = len(segs)
    out_refs = rest[:n_out]
    h_scr = rest[n_out]
    j = pl.program_id(1)
    is_dec = pl.program_id(0) >= n_ctx_tiles

    @pl.when(j == 0)
    def _():
        h_scr[...] = _modulated_norm(x_ref[...], g_ref[...], mod_ref, 0).astype(BF16)

    acc = jnp.dot(h_scr[...], w_ref[...], preferred_element_type=F32)
    tm = acc.shape[0]

    for o_ref, (lo, hi, kind, scale) in zip(out_refs, segs):
        in_seg = (j >= lo) & (j < hi)
        a_scaled = lambda scale=scale: acc if scale == 1.0 else acc * scale

        if kind in ("plain", "silu"):
            @pl.when(in_seg)
            def _(o_ref=o_ref, kind=kind, a_scaled=a_scaled):
                a = a_scaled()
                o_ref[...] = (jax.nn.silu(a) if kind == "silu" else a).astype(o_ref.dtype)
            continue

        @pl.when(in_seg & jnp.logical_not(is_dec))
        def _(o_ref=o_ref, a_scaled=a_scaled):
            o_ref[...] = a_scaled().astype(o_ref.dtype)

        @pl.when(in_seg & is_dec)
        def _(o_ref=o_ref, kind=kind, a_scaled=a_scaled):
            a = a_scaled()
            if kind == "rope64":
                cos, sin = cos_ref[...], sin_ref[...]
                lane = lax.broadcasted_iota(I32, (tm, 128), 1)
                first_half = (lane & 32) == 0
                for c in range(tn // 128):
                    sl = slice(c * 128, (c + 1) * 128)
                    o_ref[:, sl] = _rope64(a[:, sl], cos, sin, first_half).astype(o_ref.dtype)
            elif kind == "rope256":
                cos, sin = cos_ref[...], sin_ref[...]
                for c in range(tn // 256):
                    s1 = slice(c * 256, c * 256 + 128)
                    s2 = slice(c * 256 + 128, (c + 1) * 256)
                    x1, x2 = a[:, s1], a[:, s2]
                    o_ref[:, s1] = (x1 * cos - x2 * sin).astype(o_ref.dtype)
                    o_ref[:, s2] = (x2 * cos + x1 * sin).astype(o_ref.dtype)
            else:
                raise ValueError(kind)


def norm_mod_matmul(lay, x, g, mods, w, segments, rope_tables, tn=1024):
    t, d = x.shape
    tm = lay.tm
    n_total = w.shape[1]
    assert n_total % tn == 0
    segs, out_shapes, out_specs = [], [], []
    col = 0
    for width, kind, scale, dtype in segments:
        assert width % tn == 0 and col % tn == 0
        lo, n_blk = col // tn, width // tn
        segs.append((lo, lo + n_blk, kind, float(scale)))
        out_shapes.append(jax.ShapeDtypeStruct((t, width), dtype))
        out_specs.append(pl.BlockSpec(
            (tm, tn), lambda i, j, lo=lo, n_blk=n_blk: (i, jnp.clip(j - lo, 0, n_blk - 1))))
        col += width
    assert col == n_total
    n_ctx_tiles = lay.t_ctx // tm
    blocks_per_seq = lay.n_dec // tm
    rope_spec = pl.BlockSpec((tm, 128), lambda i, j: (jnp.maximum(i - n_ctx_tiles, 0) % blocks_per_seq, 0))
    return pl.pallas_call(
        functools.partial(_nmm_kernel, segs=tuple(segs), tn=tn, n_ctx_tiles=n_ctx_tiles),
        out_shape=out_shapes,
        grid=(t // tm, n_total // tn),
        in_specs=[
            pl.BlockSpec((tm, d), lambda i, j: (i, 0)),
            pl.BlockSpec((1, d), lambda i, j: (0, 0)),
            pl.BlockSpec((1, N_MOD, d), lambda i, j: (lay.mod_row(i), 0, 0)),
            pl.BlockSpec((d, tn), lambda i, j: (0, j)),
            rope_spec, rope_spec,
        ],
        out_specs=out_specs,
        scratch_shapes=[pltpu.VMEM((tm, d), BF16)],
        compiler_params=_cparams("parallel", "arbitrary"),
        name="norm_mod_matmul",
    )(x, g.reshape(1, d), mods, w, *rope_tables)


def _diff_attn_kernel(*refs, lam_init, has_cache):
    lam_ref, q_ref, k_ref, v_ref = refs[:4]
    pos = 4
    if has_cache:
        ck_ref, cv_ref = refs[4:6]
        pos = 6
    g_ref, o_ref = refs[pos], refs[pos + 1]

    lp = lam_ref[...]
    lam = (jnp.exp(jnp.sum(lp[0:1] * lp[1:2], axis=-1, keepdims=True))
           - jnp.exp(jnp.sum(lp[2:3] * lp[3:4], axis=-1, keepdims=True)) + lam_init)

    q = q_ref[...].astype(F32) * (DA_HEAD_DIM ** -0.5)
    lane = lax.broadcasted_iota(I32, q.shape, 1)
    kn = k_ref[...].astype(BF16)
    vn = v_ref[...].astype(BF16)
    if has_cache:
        kc = ck_ref[0].astype(BF16)
        vc = cv_ref[0].astype(BF16)

    probs = []
    for comp in range(2):
        in_comp = (lane < DA_HEAD_DIM) if comp == 0 else (lane >= DA_HEAD_DIM)
        qc = jnp.where(in_comp, q, 0.0).astype(BF16)
        sn = _nt_dot(qc, kn)
        m = jnp.max(sn, axis=-1, keepdims=True)
        if has_cache:
            sc = _nt_dot(qc, kc)
            m = jnp.maximum(m, jnp.max(sc, axis=-1, keepdims=True))
        en = jnp.exp(sn - m)
        denom = jnp.sum(en, axis=-1, keepdims=True)
        if has_cache:
            ec = jnp.exp(sc - m)
            denom = denom + jnp.sum(ec, axis=-1, keepdims=True)
            probs.append((en / denom, ec / denom))
        else:
            probs.append((en / denom, None))

    o = jnp.dot((probs[0][0] - lam * probs[1][0]).astype(BF16), vn, preferred_element_type=F32)
    if has_cache:
        o = o + jnp.dot((probs[0][1] - lam * probs[1][1]).astype(BF16), vc, preferred_element_type=F32)
    ms = jnp.mean(o * o, axis=-1, keepdims=True)
    o_ref[...] = (o * lax.rsqrt(ms + NORM_EPS) * g_ref[...]) * (1.0 - lam_init)


def diff_attention(q, k, v, row0, batch, seq_len, lam_params, subln_g, lam_init, cache=None, tq=256):
    width = q.shape[1]
    tq = min(tq, seq_len)
    nq = seq_len // tq
    hw = 2 * DA_HEAD_DIM
    assert row0 % seq_len == 0
    q0, s0 = row0 // tq, row0 // seq_len
    in_specs = [
        pl.BlockSpec((4, DA_HEAD_DIM), lambda b, h, i: (0, 0)),
        pl.BlockSpec((tq, hw), lambda b, h, i: (q0 + b * nq + i, h)),
        pl.BlockSpec((seq_len, hw), lambda b, h, i: (s0 + b, h)),
        pl.BlockSpec((seq_len, hw), lambda b, h, i: (s0 + b, h)),
    ]
    args = [lam_params, q, k, v]
    if cache is not None:
        ck, cv = cache
        past = ck.shape[1]
        in_specs += [pl.BlockSpec((1, past, hw), lambda b, h, i: (b, 0, h)),
                     pl.BlockSpec((1, past, hw), lambda b, h, i: (b, 0, h))]
        args += [ck, cv]
    in_specs.append(pl.BlockSpec((1, hw), lambda b, h, i: (0, 0)))
    args.append(subln_g.reshape(1, hw))
    return pl.pallas_call(
        functools.partial(_diff_attn_kernel, lam_init=lam_init, has_cache=cache is not None),
        out_shape=jax.ShapeDtypeStruct((batch * seq_len, width), F32),
        grid=(batch, DA_HEADS, nq),
        in_specs=in_specs,
        out_specs=pl.BlockSpec((tq, hw), lambda b, h, i: (b * nq + i, h)),
        compiler_params=_cparams("parallel", "parallel", "arbitrary"),
        name="diff_attention",
    )(*args)


def _retention_kernel(*refs, seq_len, has_state, emit_state):
    lg_ref, q_ref, k_ref, v_ref, gate_ref = refs[:5]
    pos = 5
    if has_state:
        s0f_ref, s0b_ref = refs[5:7]
        pos = 7
    o_ref = refs[pos]
    pos += 1
    if emit_state:
        sf_ref, sb_ref = refs[pos:pos + 2]
        pos += 2
    s_scr, o_scr = refs[pos], refs[pos + 1]

    h = pl.program_id(1)
    direction = pl.program_id(2)
    c_len = RET_CHUNK
    n_chunks = seq_len // c_len
    row = lax.broadcasted_iota(I32, (c_len, c_len), 0)
    colm = lax.broadcasted_iota(I32, (c_len, c_len), 1)
    rel = (row - colm).astype(F32)
    posv = lax.broadcasted_iota(I32, (c_len, 1), 0).astype(F32)

    def run(backward):
        lg = lg_ref[1 if backward else 0, h]
        if backward:
            intra = jnp.where(rel <= 0, jnp.exp(-rel * lg), 0.0)
            q_decay = jnp.exp((c_len - posv) * lg)
            k_decay = jnp.exp(posv * lg)
        else:
            intra = jnp.where(rel >= 0, jnp.exp(rel * lg), 0.0)
            q_decay = jnp.exp((posv + 1.0) * lg)
            k_decay = jnp.exp((c_len - 1.0 - posv) * lg)
        chunk_decay = jnp.exp(jnp.zeros((1, 1), F32) + c_len * lg)

        if has_state:
            s_scr[...] = (s0b_ref if backward else s0f_ref)[0, 0, 0]
        else:
            s_scr[...] = jnp.zeros_like(s_scr)

        def body(ci, carry):
            c = (n_chunks - 1 - ci) if backward else ci
            r0 = pl.multiple_of(c * c_len, c_len)
            rows = pl.ds(r0, c_len)
            qb = q_ref[rows, :].astype(BF16)
            kf = k_ref[rows, :].astype(F32)
            vb = v_ref[rows, :].astype(BF16)
            a = _nt_dot(qb, kf.astype(BF16)) * intra
            s = s_scr[...]
            o = (jnp.dot(a.astype(BF16), vb, preferred_element_type=F32)
                 + jnp.dot(qb, s.astype(BF16), preferred_element_type=F32) * q_decay)
            s_scr[...] = chunk_decay * s + _tn_dot((kf * k_decay).astype(BF16), vb)
            mu = jnp.mean(o, axis=-1, keepdims=True)
            oc = o - mu
            var = jnp.mean(oc * oc, axis=-1, keepdims=True)
            gated = oc * lax.rsqrt(var + NORM_EPS) * gate_ref[rows, :].astype(F32)
            if backward:
                o_ref[rows, :] = o_scr[rows, :] + gated
            else:
                o_scr[rows, :] = gated
            return carry

        lax.fori_loop(0, n_chunks, body, 0)
        if emit_state:
            (sb_ref if backward else sf_ref)[0, 0, 0] = s_scr[...]

    @pl.when(direction == 0)
    def _():
        run(False)

    @pl.when(direction == 1)
    def _():
        run(True)


def retention(q, k, v, gates, row0, batch, seq_len, log_decay, state=None, emit_state=False):
    dk = q.shape[1] // RET_HEADS
    dv = v.shape[1] // RET_HEADS
    assert row0 % seq_len == 0
    s0 = row0 // seq_len
    in_specs = [
        pl.BlockSpec(memory_space=pltpu.SMEM),
        pl.BlockSpec((seq_len, dk), lambda b, h, r: (s0 + b, h)),
        pl.BlockSpec((seq_len, dk), lambda b, h, r: (s0 + b, h)),
        pl.BlockSpec((seq_len, dv), lambda b, h, r: (s0 + b, h)),
        pl.BlockSpec((seq_len, dv), lambda b, h, r: (s0 + b, r * RET_HEADS + h)),
    ]
    args = [log_decay, q, k, v, gates]
    state_spec = pl.BlockSpec((1, 1, 1, dk, dv), lambda b, h, r: (b, 0, h, 0, 0))
    if state is not None:
        in_specs += [state_spec, state_spec]
        args += list(state)
    out_shapes = [jax.ShapeDtypeStruct((batch * seq_len, RET_HEADS * dv), F32)]
    out_specs = [pl.BlockSpec((seq_len, dv), lambda b, h, r: (b, h))]
    if emit_state:
        out_shapes += [jax.ShapeDtypeStruct((batch, 1, RET_HEADS, dk, dv), F32)] * 2
        out_specs += [state_spec, state_spec]
    return pl.pallas_call(
        functools.partial(_retention_kernel, seq_len=seq_len, has_state=state is not None,
                          emit_state=emit_state),
        out_shape=out_shapes,
        grid=(batch, RET_HEADS, 2),
        in_specs=in_specs,
        out_specs=out_specs,
        scratch_shapes=[pltpu.VMEM((dk, dv), F32), pltpu.VMEM((seq_len, dv), F32)],
        compiler_params=_cparams("parallel", "parallel", "arbitrary"),
        name="retention",
    )(*args)


def _proj_residual_kernel(x_ref, a_ref, w_ref, mod_ref, o_ref):
    y = jnp.dot(a_ref[...].astype(BF16), w_ref[...], preferred_element_type=F32)
    o_ref[...] = x_ref[...] + mod_ref[0, 2:3, :] * y


def proj_residual(lay, x, a, w, mods):
    t, d = x.shape
    kdim = a.shape[1]
    tm = lay.tm
    return pl.pallas_call(
        _proj_residual_kernel,
        out_shape=jax.ShapeDtypeStruct((t, d), F32),
        grid=(t // tm,),
        in_specs=[
            pl.BlockSpec((tm, d), lambda i: (i, 0)),
            pl.BlockSpec((tm, kdim), lambda i: (i, 0)),
            pl.BlockSpec((kdim, d), lambda i: (0, 0)),
            pl.BlockSpec((1, N_MOD, d), lambda i: (lay.mod_row(i), 0, 0)),
        ],
        out_specs=pl.BlockSpec((tm, d), lambda i: (i, 0)),
        compiler_params=_cparams("parallel"),
        name="proj_residual",
    )(x, a, w, mods)


def _first_max_onehot(vals):
    m = vals[0]
    for v in vals[1:]:
        m = jnp.maximum(m, v)
    onehot, taken = [], None
    for v in vals:
        hit = v == m
        if taken is None:
            onehot.append(hit)
            taken = hit
        else:
            onehot.append(hit & jnp.logical_not(taken))
            taken = taken | hit
    return m, onehot


def _pick(onehot, vals):
    out = vals[-1]
    for oh, v in zip(onehot[-2::-1], vals[-2::-1]):
        out = jnp.where(oh, v, out)
    return out


def _router_kernel(x_ref, g_ref, mod_ref, wr_ref, bias_ref, tri_ref, h_ref, idx_ref, w_ref, rank_ref, cnt_ref,
                   ind_scr):
    h = _modulated_norm(x_ref[...], g_ref[...], mod_ref, 3)
    h_ref[...] = h
    logits = _nt_dot(wr_ref[...], h.astype(BF16))
    s = jax.nn.sigmoid(logits)
    sel = s + bias_ref[...]
    neg_inf = jnp.full_like(sel[0:1], -jnp.inf)
    sel_rows = [sel[e:e + 1] for e in range(N_EXPERTS)]
    s_rows = [s[e:e + 1] for e in range(N_EXPERTS)]

    def top2(vals):
        m1, oh1 = _first_max_onehot(vals)
        rest = [jnp.where(o, neg_inf, v) for o, v in zip(oh1, vals)]
        m2, oh2 = _first_max_onehot(rest)
        return m1, m2, oh1, oh2

    grp_scores = []
    for g in range(N_GROUPS):
        m1, m2, _, _ = top2(sel_rows[g * EXPERTS_PER_GROUP:(g + 1) * EXPERTS_PER_GROUP])
        grp_scores.append(m1 + m2)
    _, in_grp = _first_max_onehot(grp_scores)
    cand_sel = [_pick(in_grp, [sel_rows[g * EXPERTS_PER_GROUP + k] for g in range(N_GROUPS)])
                for k in range(EXPERTS_PER_GROUP)]
    cand_s = [_pick(in_grp, [s_rows[g * EXPERTS_PER_GROUP + k] for g in range(N_GROUPS)])
              for k in range(EXPERTS_PER_GROUP)]
    _, _, oh1, oh2 = top2(cand_sel)
    w1 = _pick(oh1, cand_s)
    w2 = _pick(oh2, cand_s)
    denom = w1 + w2
    w_ref[0:1, :] = w1 / denom
    w_ref[1:2, :] = w2 / denom

    ints = [jnp.full(w1.shape, k, I32) for k in range(EXPERTS_PER_GROUP)]
    grp = _pick(in_grp, ints) * EXPERTS_PER_GROUP
    idx_ref[0:1, :] = grp + _pick(oh1, ints)
    idx_ref[1:2, :] = grp + _pick(oh2, ints)

    one, zero = jnp.ones_like(w1), jnp.zeros_like(w1)
    for g in range(N_GROUPS):
        for k in range(EXPERTS_PER_GROUP):
            e = g * EXPERTS_PER_GROUP + k
            ind_scr[e:e + 1, :] = jnp.where(in_grp[g] & (oh1[k] | oh2[k]), one, zero)
    ind = ind_scr[...].astype(BF16)
    ranks = jnp.dot(ind, tri_ref[...], preferred_element_type=F32)
    cnt_ref[0] = jnp.dot(ind, jnp.ones((ind.shape[1], 128), BF16), preferred_element_type=F32)
    for slot, oh in ((0, oh1), (1, oh2)):
        r = zero
        for g in range(N_GROUPS):
            for k in range(EXPERTS_PER_GROUP):
                e = g * EXPERTS_PER_GROUP + k
                r = r + jnp.where(in_grp[g] & oh[k], ranks[e:e + 1], zero)
        rank_ref[slot:slot + 1, :] = r.astype(I32)


def route(lay, x, g, mods, w_router_t, router_bias):
    t, d = x.shape
    tm = lay.tm
    n_tiles = t // tm
    tri = jnp.triu(jnp.ones((tm, tm), BF16), k=1)
    pair = lambda dt: jax.ShapeDtypeStruct((TOP_K, t), dt)
    pair_spec = pl.BlockSpec((TOP_K, tm), lambda i: (0, i))
    return pl.pallas_call(
        _router_kernel,
        out_shape=[jax.ShapeDtypeStruct((t, d), F32), pair(I32), pair(F32), pair(I32),
                   jax.ShapeDtypeStruct((n_tiles, N_EXPERTS, 128), F32)],
        grid=(n_tiles,),
        in_specs=[
            pl.BlockSpec((tm, d), lambda i: (i, 0)),
            pl.BlockSpec((1, d), lambda i: (0, 0)),
            pl.BlockSpec((1, N_MOD, d), lambda i: (lay.mod_row(i), 0, 0)),
            pl.BlockSpec((N_EXPERTS, d), lambda i: (0, 0)),
            pl.BlockSpec((N_EXPERTS, 1), lambda i: (0, 0)),
            pl.BlockSpec((tm, tm), lambda i: (0, 0)),
        ],
        out_specs=[pl.BlockSpec((tm, d), lambda i: (i, 0)), pair_spec, pair_spec, pair_spec,
                   pl.BlockSpec((1, N_EXPERTS, 128), lambda i: (i, 0, 0))],
        scratch_shapes=[pltpu.VMEM((N_EXPERTS, tm), F32)],
        compiler_params=_cparams("parallel"),
        name="router",
    )(x, g.reshape(1, d), mods, w_router_t, router_bias.reshape(N_EXPERTS, 1), tri)


def dispatch_plan(lay, idx, rank, cnt, n_sorted):
    tm = lay.tm
    cnt_tile = cnt[:, :, 0].astype(I32)
    total = jnp.sum(cnt_tile, axis=0)
    padded = ((total + EXPERT_TILE - 1) // EXPERT_TILE) * EXPERT_TILE
    end = jnp.cumsum(padded)
    start = end - padded
    base = start[None, :] + jnp.cumsum(cnt_tile, axis=0) - cnt_tile
    base_tok = jnp.repeat(base, tm, axis=0)
    onehot = idx[:, :, None] == jnp.arange(N_EXPERTS, dtype=I32)[None, None, :]
    pos = jnp.sum(jnp.where(onehot, base_tok[None], 0), axis=-1) + rank
    tile_row = jnp.arange(n_sorted // EXPERT_TILE, dtype=I32) * EXPERT_TILE
    tile_expert = jnp.minimum(jnp.sum(end[None, :] <= tile_row[:, None], axis=1), N_EXPERTS - 1).astype(I32)
    n_valid = (end[-1] // EXPERT_TILE).astype(I32).reshape(1)
    return pos.astype(I32), tile_expert, n_valid


def _sc_mesh():
    return plsc.VectorSubcoreMesh(core_axis_name="c", subcore_axis_name="s")


def _sc_worker_id():
    return lax.axis_index("s") * V7X_SC_CORES + lax.axis_index("c")


def sc_scatter_rows(src, pos, n_out_rows):
    t, d = src.shape
    n_idx = pos.shape[0]
    per_worker = n_idx // SC_WORKERS
    chunk = SC_CHUNK_ROWS
    assert n_idx % (SC_WORKERS * chunk) == 0 and t % chunk == 0

    @functools.partial(
        pl.kernel, mesh=_sc_mesh(), out_type=jax.ShapeDtypeStruct((n_out_rows, d), src.dtype),
        scratch_types=[pltpu.VMEM((chunk,), I32), pltpu.VMEM((chunk, d), src.dtype)],
        name="sc_scatter_rows")
    def scatter(src_hbm, pos_hbm, out_hbm, idx_v, rows_v):
        base = _sc_worker_id() * per_worker

        @pl.loop(0, per_worker // chunk)
        def _(i):
            a0 = base + i * chunk
            pltpu.sync_copy(pos_hbm.at[pl.ds(a0, chunk)], idx_v)
            pltpu.sync_copy(src_hbm.at[pl.ds(lax.rem(a0, t), chunk)], rows_v)
            pltpu.sync_copy(rows_v, out_hbm.at[idx_v])

    return scatter(src, pos)


def sc_gather_rows(table, idx):
    _, d = table.shape
    n_idx = idx.shape[0]
    per_worker = n_idx // SC_WORKERS
    chunk = SC_CHUNK_ROWS
    assert n_idx % (SC_WORKERS * chunk) == 0

    @functools.partial(
        pl.kernel, mesh=_sc_mesh(), out_type=jax.ShapeDtypeStruct((n_idx, d), table.dtype),
        scratch_types=[pltpu.VMEM((chunk,), I32), pltpu.VMEM((chunk, d), table.dtype)],
        name="sc_gather_rows")
    def gather(table_hbm, idx_hbm, out_hbm, idx_v, rows_v):
        base = _sc_worker_id() * per_worker

        @pl.loop(0, per_worker // chunk)
        def _(i):
            a0 = base + i * chunk
            pltpu.sync_copy(idx_hbm.at[pl.ds(a0, chunk)], idx_v)
            pltpu.sync_copy(table_hbm.at[idx_v], rows_v)
            pltpu.sync_copy(rows_v, out_hbm.at[pl.ds(a0, chunk)])

    return gather(table, idx)


def _experts_kernel(te_ref, nv_ref, x_ref, wg_ref, wu_ref, wd_ref, y_ref):
    @pl.when(pl.program_id(0) < nv_ref[0])
    def _():
        x = x_ref[...].astype(BF16)
        a = (jax.nn.silu(jnp.dot(x, wg_ref[0], preferred_element_type=F32))
             * jnp.dot(x, wu_ref[0], preferred_element_type=F32))
        y_ref[...] = jnp.dot(a.astype(BF16), wd_ref[0], preferred_element_type=F32)


def grouped_experts(xs, tile_expert, n_valid, wg, wu, wd):
    n_rows, d = xs.shape
    _, _, de = wg.shape
    tm = EXPERT_TILE
    row_map = lambda i, te, nv: (jnp.minimum(i, nv[0] - 1), 0)
    grid_spec = pltpu.PrefetchScalarGridSpec(
        num_scalar_prefetch=2,
        grid=(n_rows // tm,),
        in_specs=[
            pl.BlockSpec((tm, d), row_map),
            pl.BlockSpec((1, d, de), lambda i, te, nv: (te[i], 0, 0)),
            pl.BlockSpec((1, d, de), lambda i, te, nv: (te[i], 0, 0)),
            pl.BlockSpec((1, de, d), lambda i, te, nv: (te[i], 0, 0)),
        ],
        out_specs=pl.BlockSpec((tm, d), row_map),
    )
    return pl.pallas_call(
        _experts_kernel,
        out_shape=jax.ShapeDtypeStruct((n_rows, d), F32),
        grid_spec=grid_spec,
        compiler_params=_cparams("arbitrary"),
        name="grouped_experts",
    )(tile_expert, n_valid, xs, wg, wu, wd)


def _combine_kernel(x_ref, y_ref, w_ref, mod_ref, fg_ref, o_ref, *, final_norm):
    y = w_ref[0] * y_ref[0] + w_ref[1] * y_ref[1]
    out = x_ref[...] + mod_ref[0, 5:6, :] * y
    if final_norm:
        ms = jnp.mean(out * out, axis=-1, keepdims=True)
        out = out * lax.rsqrt(ms + NORM_EPS) * fg_ref[...]
    o_ref[...] = out


def combine_residual(lay, x, y_pair, w_col, mods, final_g, final_norm):
    t, d = x.shape
    tm = lay.tm
    return pl.pallas_call(
        functools.partial(_combine_kernel, final_norm=final_norm),
        out_shape=jax.ShapeDtypeStruct((t, d), F32),
        grid=(t // tm,),
        in_specs=[
            pl.BlockSpec((tm, d), lambda i: (i, 0)),
            pl.BlockSpec((TOP_K, tm, d), lambda i: (0, i, 0)),
            pl.BlockSpec((TOP_K, tm, 1), lambda i: (0, i, 0)),
            pl.BlockSpec((1, N_MOD, d), lambda i: (lay.mod_row(i), 0, 0)),
            pl.BlockSpec((1, d), lambda i: (0, 0)),
        ],
        out_specs=pl.BlockSpec((tm, d), lambda i: (i, 0)),
        compiler_params=_cparams("parallel"),
        name="combine_residual",
    )(x, y_pair, w_col, mods, final_g.reshape(1, d))


def group_moe_residual(lay, x, mods, p, layer, final_norm):
    t, d = x.shape
    h, idx, w, rank, cnt = route(lay, x, p["norm_ffn_g"][layer], mods, p["w_router_t"], p["router_bias"])
    n_assign = TOP_K * t
    row_quantum = SC_WORKERS * SC_CHUNK_ROWS
    n_sorted = n_assign + N_EXPERTS * (EXPERT_TILE - 1)
    n_sorted = -(-n_sorted // row_quantum) * row_quantum
    n_sorted = -(-n_sorted // EXPERT_TILE) * EXPERT_TILE
    pos, tile_expert, n_valid = dispatch_plan(lay, idx, rank, cnt, n_sorted)
    pos_flat = pos.reshape(n_assign)
    xs = sc_scatter_rows(h, pos_flat, n_sorted)
    ys = grouped_experts(xs, tile_expert, n_valid, p["moe_w_gate"][layer], p["moe_w_up"][layer],
                         p["moe_w_down"][layer])
    y_pair = sc_gather_rows(ys, pos_flat).reshape(TOP_K, t, d)
    return combine_residual(lay, x, y_pair, w[:, :, None], mods, p["final_norm_g"], final_norm)


def _rope_angles(n, d):
    n_rows = n // GRID_W
    row = jnp.repeat(jnp.arange(n_rows), GRID_W).astype(F32)
    col = jnp.tile(jnp.arange(GRID_W), n_rows).astype(F32)
    nf = d // 4
    freqs = jnp.power(ROPE_BASE, -jnp.arange(nf, dtype=F32) / nf)
    ang = jnp.concatenate([row[:, None] * freqs, col[:, None] * freqs], axis=-1)
    return jnp.cos(ang), jnp.sin(ang)


def kernel(x_prompt, x_sample, cache_attn_k, cache_attn_v, state_ret_fwd, state_ret_bwd, c, c_ctx, w_ada, b_ada, norm_mix_g, norm_ffn_g, final_norm_g, da_w_qkv, da_lambda_q1, da_lambda_k1, da_lambda_q2, da_lambda_k2, da_subln_g, da_w_o, ret_w_qkv, ret_w_gate_fwd, ret_w_gate_bwd, ret_decay_fwd, ret_decay_bwd, ret_w_o, w_router, router_bias, moe_w_gate, moe_w_up, moe_w_down):
    b_ctx, n_ctx, d = x_prompt.shape
    b_dec, n_dec, _ = x_sample.shape
    past = cache_attn_k.shape[2]
    n_attn = cache_attn_k.shape[1]
    depth = w_ada.shape[0]
    assert b_dec + 1 <= MOD_ROWS
    lay = Layout(b_ctx, n_ctx, b_dec, n_dec)

    cond = jnp.zeros((MOD_ROWS, d), F32).at[0].set(c_ctx).at[1:1 + b_dec].set(c)
    mods_all = ada_modulation(cond, w_ada, b_ada)

    p = {
        "norm_ffn_g": norm_ffn_g, "final_norm_g": final_norm_g,
        "w_router_t": w_router.T.astype(BF16), "router_bias": router_bias.astype(F32),
        "moe_w_gate": moe_w_gate.astype(BF16), "moe_w_up": moe_w_up.astype(BF16),
        "moe_w_down": moe_w_down.astype(BF16),
    }
    ret_log_decay = jnp.stack([jax.nn.log_sigmoid(ret_decay_fwd.astype(F32)),
                               jax.nn.log_sigmoid(ret_decay_bwd.astype(F32))], axis=1)
    ck_all = cache_attn_k.reshape(b_dec, n_attn, past, -1)
    cv_all = cache_attn_v.reshape(b_dec, n_attn, past, -1)

    x = jnp.concatenate([x_prompt.reshape(lay.t_ctx, d), x_sample.reshape(lay.t_dec, d)], axis=0)
    new_k, new_v, new_sf, new_sb = [], [], [], []
    for i in range(depth):
        mods = mods_all[i]
        j = i // 2
        if i % 2 == 0:
            lam_init = 0.8 - 0.6 * math.exp(-0.3 * i)
            qkw = DA_HEADS * 2 * DA_HEAD_DIM
            cos, sin = _rope_angles(n_dec, DA_HEAD_DIM)
            rope = (jnp.tile(cos, (1, 4)), jnp.concatenate([-sin, sin, -sin, sin], axis=-1))
            q, k, v = norm_mod_matmul(
                lay, x, norm_mix_g[i], mods, da_w_qkv[j].astype(BF16),
                [(qkw, "rope64", 1.0, BF16), (qkw, "rope64", 1.0, F32), (DA_HEADS * DA_V_DIM, "plain", 1.0, F32)],
                rope)
            lam_params = jnp.stack([da_lambda_q1[j], da_lambda_k1[j], da_lambda_q2[j], da_lambda_k2[j]])
            mix = jnp.concatenate([
                diff_attention(q, k, v, 0, b_ctx, n_ctx, lam_params, da_subln_g[j], lam_init),
                diff_attention(q, k, v, lay.t_ctx, b_dec, n_dec, lam_params, da_subln_g[j], lam_init,
                               cache=(ck_all[:, j], cv_all[:, j])),
            ], axis=0)
            x = proj_residual(lay, x, mix, da_w_o[j].astype(BF16), mods)
            new_k.append(k[:lay.t_ctx].reshape(b_ctx, n_ctx, DA_HEADS, 2, DA_HEAD_DIM))
            new_v.append(v[:lay.t_ctx].reshape(b_ctx, n_ctx, DA_HEADS, DA_V_DIM))
        else:
            kd = ret_w_qkv.shape[2] // 4
            dv = 2 * kd
            w_all = jnp.concatenate([ret_w_qkv[j], ret_w_gate_fwd[j], ret_w_gate_bwd[j]], axis=-1).astype(BF16)
            q, k, v, gates = norm_mod_matmul(
                lay, x, norm_mix_g[i], mods, w_all,
                [(kd, "rope256", 1.0, BF16), (kd, "rope256", (kd // RET_HEADS) ** -0.5, F32),
                 (dv, "plain", 1.0, BF16), (2 * dv, "silu", 1.0, F32)],
                _rope_angles(n_dec, kd // RET_HEADS))
            o_ctx, sf, sb = retention(q, k, v, gates, 0, b_ctx, n_ctx, ret_log_decay[j], emit_state=True)
            (o_dec,) = retention(q, k, v, gates, lay.t_ctx, b_dec, n_dec, ret_log_decay[j],
                                 state=(state_ret_fwd[:, j:j + 1], state_ret_bwd[:, j:j + 1]))
            x = proj_residual(lay, x, jnp.concatenate([o_ctx, o_dec], axis=0), ret_w_o[j].astype(BF16), mods)
            new_sf.append(sf)
            new_sb.append(sb)
        x = group_moe_residual(lay, x, mods, p, i, final_norm=(i == depth - 1))

    return (x[:lay.t_ctx].reshape(b_ctx, n_ctx, d), x[lay.t_ctx:].reshape(b_dec, n_dec, d),
            jnp.stack(new_k, axis=1), jnp.stack(new_v, axis=1),
            jnp.concatenate(new_sf, axis=1), jnp.concatenate(new_sb, axis=1))
```

```python
import functools
import math

import jax
import jax.numpy as jnp
from jax import lax
from jax.experimental import pallas as pl
from jax.experimental.pallas import tpu as pltpu
from jax.experimental.pallas import tpu_sc as plsc

F32 = jnp.float32
BF16 = jnp.bfloat16
I32 = jnp.int32

GRID_W = 64
ROPE_BASE = 10000.0
NORM_EPS = 1e-6
DA_HEADS = 8
DA_HEAD_DIM = 64
DA_V_DIM = 2 * DA_HEAD_DIM
RET_HEADS = 4
RET_CHUNK = 128
N_EXPERTS = 16
N_GROUPS = 4
EXPERTS_PER_GROUP = N_EXPERTS // N_GROUPS
TOP_K = 2
N_MOD = 6
MOD_ROWS = 16
LANES = 128

V7X_VMEM_LIMIT = 56 * 1024 * 1024
V7X_SC_CORES = 2
V7X_SC_SUBCORES = 16
SC_WORKERS = V7X_SC_CORES * V7X_SC_SUBCORES
SC_CHUNK_ROWS = 32

ROW_TILE = 512
EXPERT_TILE = 512


def _cparams(*sem):
    return pltpu.CompilerParams(dimension_semantics=sem, vmem_limit_bytes=V7X_VMEM_LIMIT)


def _nt_dot(a, b):
    return lax.dot_general(a, b, (((1,), (1,)), ((), ())), preferred_element_type=F32)


def _tn_dot(a, b):
    return lax.dot_general(a, b, (((0,), (0,)), ((), ())), preferred_element_type=F32)


class Layout:
    def __init__(self, b_ctx, n_ctx, b_dec, n_dec):
        self.b_ctx, self.n_ctx, self.b_dec, self.n_dec = b_ctx, n_ctx, b_dec, n_dec
        self.t_ctx, self.t_dec = b_ctx * n_ctx, b_dec * n_dec
        self.t = self.t_ctx + self.t_dec
        self.tm = min(ROW_TILE, n_dec, self.t_ctx)
        assert self.t_ctx % self.tm == 0 and n_dec % self.tm == 0
        assert self.t_ctx % n_dec == 0 and self.t_ctx % n_ctx == 0
        self.n_ctx_tiles = self.t_ctx // self.tm
        self.n_tiles = self.t // self.tm

    def mod_row(self, i):
        r = i * self.tm
        return jnp.where(r < self.t_ctx, 0, 1 + (r - self.t_ctx) // self.n_dec)

    def part_tile(self, part, i):
        if part == 0:
            return jnp.minimum(i, self.n_ctx_tiles - 1)
        return jnp.maximum(i - self.n_ctx_tiles, 0)

    def row_specs(self, arrays, width, extra_grid_dims=0):
        pad = (0,) * 0
        if len(arrays) == 1:
            return [pl.BlockSpec((self.tm, width), lambda i, *_: (i, 0))]
        return [pl.BlockSpec((self.tm, width), lambda i, *_, p=p: (self.part_tile(p, i), 0)) for p in (0, 1)]


def _ada_kernel(c_ref, w_ref, b_ref, o_ref):
    s = jax.nn.silu(c_ref[...]).astype(BF16)
    acc = jnp.dot(s, w_ref[0].astype(BF16), preferred_element_type=F32)
    o_ref[0] = acc + b_ref[0]


def ada_modulation(cond, w_ada, b_ada):
    depth, d, n = w_ada.shape
    tn = 1536
    out = pl.pallas_call(
        _ada_kernel,
        out_shape=jax.ShapeDtypeStruct((depth, MOD_ROWS, n), F32),
        grid=(depth, n // tn),
        in_specs=[
            pl.BlockSpec((MOD_ROWS, d), lambda l, j: (0, 0)),
            pl.BlockSpec((1, d, tn), lambda l, j: (l, 0, j)),
            pl.BlockSpec((1, 1, tn), lambda l, j: (l, 0, j)),
        ],
        out_specs=pl.BlockSpec((1, MOD_ROWS, tn), lambda l, j: (l, 0, j)),
        compiler_params=_cparams("parallel", "parallel"),
        name="ada_modulation",
    )(cond, w_ada, b_ada.reshape(depth, 1, n))
    return out.reshape(depth, MOD_ROWS, N_MOD, d)


def _modulated_norm(x, g, mod_ref, shift_idx):
    ms = jnp.mean(x * x, axis=-1, keepdims=True)
    y = x * lax.rsqrt(ms + NORM_EPS) * g
    return y * (1.0 + mod_ref[0, shift_idx + 1:shift_idx + 2, :]) + mod_ref[0, shift_idx:shift_idx + 1, :]


def _by_part(is_dec, refs, fn):
    if len(refs) == 1:
        fn(refs[0])
        return
    pl.when(jnp.logical_not(is_dec))(lambda: fn(refs[0]))
    pl.when(is_dec)(lambda: fn(refs[1]))


def _rope64(a, cos, sin_signed, first_half):
    partner = jnp.where(first_half, pltpu.roll(a, 96, 1), pltpu.roll(a, 32, 1))
    return a * cos + partner * sin_signed


def _nmm_kernel(*refs, n_x, outs, tn, n_ctx_tiles):
    x_refs = refs[:n_x]
    g_ref, mod_ref, w_ref, cos_ref, sin_ref = refs[n_x:n_x + 5]
    out_refs = refs[n_x + 5:n_x + 5 + len(outs)]
    h_scr = refs[n_x + 5 + len(outs)]
    j = pl.program_id(1)
    is_dec = pl.program_id(0) >= n_ctx_tiles
    is_ctx = jnp.logical_not(is_dec)

    @pl.when(j == 0)
    def _():
        def fill(x_ref):
            h_scr[...] = _modulated_norm(x_ref[...], g_ref[...], mod_ref, 0).astype(BF16)
        _by_part(is_dec, x_refs, fill)

    acc = jnp.dot(h_scr[...], w_ref[...], preferred_element_type=F32)
    tm = acc.shape[0]

    def rotated(a, kind):
        cos, sin = cos_ref[...], sin_ref[...]
        if kind == "rope64":
            lane = lax.broadcasted_iota(I32, (tm, LANES), 1)
            first_half = (lane & 32) == 0
            return [(slice(c * 128, (c + 1) * 128), _rope64(a[:, c * 128:(c + 1) * 128], cos, sin, first_half))
                    for c in range(tn // 128)]
        pieces = []
        for c in range(tn // 256):
            s1 = slice(c * 256, c * 256 + 128)
            s2 = slice(c * 256 + 128, (c + 1) * 256)
            x1, x2 = a[:, s1], a[:, s2]
            pieces += [(s1, x1 * cos - x2 * sin), (s2, x2 * cos + x1 * sin)]
        return pieces

    for o_ref, (lo, hi, kind, scale, rows) in zip(out_refs, outs):
        in_seg = (j >= lo) & (j < hi)

        def plain(o_ref=o_ref, kind=kind, scale=scale):
            a = acc if scale == 1.0 else acc * scale
            o_ref[...] = (jax.nn.silu(a) if kind == "silu" else a).astype(o_ref.dtype)

        def rope(o_ref=o_ref, kind=kind, scale=scale):
            a = acc if scale == 1.0 else acc * scale
            for sl, val in rotated(a, kind):
                o_ref[:, sl] = val.astype(o_ref.dtype)

        if rows == "ctx":
            pl.when(in_seg & is_ctx)(plain)
        elif kind in ("plain", "silu"):
            pl.when(in_seg)(plain)
        else:
            pl.when(in_seg & is_ctx)(plain)
            pl.when(in_seg & is_dec)(rope)


def norm_mod_matmul(lay, x_parts, g, mods, w, outputs, rope_tables, tn=1024):
    d = x_parts[0].shape[1]
    tm = lay.tm
    n_total = w.shape[1]
    assert n_total % tn == 0
    n_ctx_tiles = lay.n_ctx_tiles
    outs, out_shapes, out_specs = [], [], []
    for col0, width, kind, scale, dtype, rows in outputs:
        assert width % tn == 0 and col0 % tn == 0
        lo, n_blk = col0 // tn, width // tn
        outs.append((lo, lo + n_blk, kind, float(scale), rows))
        if rows == "ctx":
            out_shapes.append(jax.ShapeDtypeStruct((lay.t_ctx, width), dtype))
            out_specs.append(pl.BlockSpec(
                (tm, tn), lambda i, j, lo=lo, n_blk=n_blk: (
                    lay.part_tile(0, i), jnp.where(i < n_ctx_tiles, jnp.clip(j - lo, 0, n_blk - 1), n_blk - 1))))
        else:
            out_shapes.append(jax.ShapeDtypeStruct((lay.t, width), dtype))
            out_specs.append(pl.BlockSpec(
                (tm, tn), lambda i, j, lo=lo, n_blk=n_blk: (i, jnp.clip(j - lo, 0, n_blk - 1))))
    blocks_per_seq = lay.n_dec // tm
    rope_spec = pl.BlockSpec((tm, LANES), lambda i, j: (lay.part_tile(1, i) % blocks_per_seq, 0))
    return pl.pallas_call(
        functools.partial(_nmm_kernel, n_x=len(x_parts), outs=tuple(outs), tn=tn, n_ctx_tiles=n_ctx_tiles),
        out_shape=out_shapes,
        grid=(lay.n_tiles, n_total // tn),
        in_specs=lay.row_specs(x_parts, d) + [
            pl.BlockSpec((1, d), lambda i, j: (0, 0)),
            pl.BlockSpec((1, N_MOD, d), lambda i, j: (lay.mod_row(i), 0, 0)),
            pl.BlockSpec((d, tn), lambda i, j: (0, j)),
            rope_spec, rope_spec,
        ],
        out_specs=out_specs,
        scratch_shapes=[pltpu.VMEM((tm, d), BF16)],
        compiler_params=_cparams("arbitrary", "arbitrary"),
        name="norm_mod_matmul",
    )(*x_parts, g.reshape(1, d), mods, w, *rope_tables)


def _diff_attn_kernel(*refs, lam_init, has_cache):
    lam_ref, q_ref, k_ref, v_ref = refs[:4]
    pos = 4
    if has_cache:
        ck_ref, cv_ref = refs[4:6]
        pos = 6
    g_ref, o_ref = refs[pos], refs[pos + 1]

    lp = lam_ref[...]
    lam = (jnp.exp(jnp.sum(lp[0:1] * lp[1:2], axis=-1, keepdims=True))
           - jnp.exp(jnp.sum(lp[2:3] * lp[3:4], axis=-1, keepdims=True)) + lam_init)

    q = q_ref[...].astype(F32) * (DA_HEAD_DIM ** -0.5)
    lane = lax.broadcasted_iota(I32, q.shape, 1)
    kn = k_ref[...].astype(BF16)
    vn = v_ref[...].astype(BF16)
    if has_cache:
        kc = ck_ref[0].astype(BF16)
        vc = cv_ref[0].astype(BF16)

    probs = []
    for comp in range(2):
        in_comp = (lane < DA_HEAD_DIM) if comp == 0 else (lane >= DA_HEAD_DIM)
        qc = jnp.where(in_comp, q, 0.0).astype(BF16)
        sn = _nt_dot(qc, kn)
        m = jnp.max(sn, axis=-1, keepdims=True)
        if has_cache:
            sc = _nt_dot(qc, kc)
            m = jnp.maximum(m, jnp.max(sc, axis=-1, keepdims=True))
        en = jnp.exp(sn - m)
        denom = jnp.sum(en, axis=-1, keepdims=True)
        if has_cache:
            ec = jnp.exp(sc - m)
            denom = denom + jnp.sum(ec, axis=-1, keepdims=True)
            probs.append((en / denom, ec / denom))
        else:
            probs.append((en / denom, None))

    o = jnp.dot((probs[0][0] - lam * probs[1][0]).astype(BF16), vn, preferred_element_type=F32)
    if has_cache:
        o = o + jnp.dot((probs[0][1] - lam * probs[1][1]).astype(BF16), vc, preferred_element_type=F32)
    ms = jnp.mean(o * o, axis=-1, keepdims=True)
    o_ref[...] = (o * lax.rsqrt(ms + NORM_EPS) * g_ref[...]) * (1.0 - lam_init)


def diff_attention(q, k, v, row0, batch, seq_len, lam_params, subln_g, lam_init, cache=None, tq=256):
    width = q.shape[1]
    tq = min(tq, seq_len)
    nq = seq_len // tq
    hw = 2 * DA_HEAD_DIM
    assert row0 % seq_len == 0
    q0, s0 = row0 // tq, row0 // seq_len
    in_specs = [
        pl.BlockSpec((4, DA_HEAD_DIM), lambda b, h, i: (0, 0)),
        pl.BlockSpec((tq, hw), lambda b, h, i: (q0 + b * nq + i, h)),
        pl.BlockSpec((seq_len, hw), lambda b, h, i: (s0 + b, h)),
        pl.BlockSpec((seq_len, hw), lambda b, h, i: (s0 + b, h)),
    ]
    args = [lam_params, q, k, v]
    if cache is not None:
        ck, cv = cache
        past = ck.shape[1]
        in_specs += [pl.BlockSpec((1, past, hw), lambda b, h, i: (b, 0, h)),
                     pl.BlockSpec((1, past, hw), lambda b, h, i: (b, 0, h))]
        args += [ck, cv]
    in_specs.append(pl.BlockSpec((1, hw), lambda b, h, i: (0, 0)))
    args.append(subln_g.reshape(1, hw))
    return pl.pallas_call(
        functools.partial(_diff_attn_kernel, lam_init=lam_init, has_cache=cache is not None),
        out_shape=jax.ShapeDtypeStruct((batch * seq_len, width), F32),
        grid=(batch, DA_HEADS, nq),
        in_specs=in_specs,
        out_specs=pl.BlockSpec((tq, hw), lambda b, h, i: (b * nq + i, h)),
        compiler_params=_cparams("parallel", "parallel", "arbitrary"),
        name="diff_attention",
    )(*args)


def _retention_kernel(*refs, seq_len, has_state, emit_state):
    lg_ref, q_ref, k_ref, v_ref, gate_ref = refs[:5]
    pos = 5
    if has_state:
        s0f_ref, s0b_ref = refs[5:7]
        pos = 7
    o_ref = refs[pos]
    pos += 1
    if emit_state:
        sf_ref, sb_ref = refs[pos:pos + 2]
        pos += 2
    s_scr, o_scr = refs[pos], refs[pos + 1]

    h = pl.program_id(1)
    direction = pl.program_id(2)
    c_len = RET_CHUNK
    n_chunks = seq_len // c_len
    row = lax.broadcasted_iota(I32, (c_len, c_len), 0)
    colm = lax.broadcasted_iota(I32, (c_len, c_len), 1)
    rel = (row - colm).astype(F32)
    posv = lax.broadcasted_iota(I32, (c_len, 1), 0).astype(F32)

    def run(backward):
        lg = lg_ref[1 if backward else 0, h]
        if backward:
            intra = jnp.where(rel <= 0, jnp.exp(-rel * lg), 0.0)
            q_decay = jnp.exp((c_len - posv) * lg)
            k_decay = jnp.exp(posv * lg)
        else:
            intra = jnp.where(rel >= 0, jnp.exp(rel * lg), 0.0)
            q_decay = jnp.exp((posv + 1.0) * lg)
            k_decay = jnp.exp((c_len - 1.0 - posv) * lg)
        chunk_decay = jnp.exp(jnp.zeros((1, 1), F32) + c_len * lg)

        if has_state:
            s_scr[...] = (s0b_ref if backward else s0f_ref)[0, 0, 0]
        else:
            s_scr[...] = jnp.zeros_like(s_scr)

        def body(ci, carry):
            c = (n_chunks - 1 - ci) if backward else ci
            r0 = pl.multiple_of(c * c_len, c_len)
            rows = pl.ds(r0, c_len)
            qb = q_ref[rows, :].astype(BF16)
            kf = k_ref[rows, :].astype(F32)
            vb = v_ref[rows, :].astype(BF16)
            a = _nt_dot(qb, kf.astype(BF16)) * intra
            s = s_scr[...]
            o = (jnp.dot(a.astype(BF16), vb, preferred_element_type=F32)
                 + jnp.dot(qb, s.astype(BF16), preferred_element_type=F32) * q_decay)
            s_scr[...] = chunk_decay * s + _tn_dot((kf * k_decay).astype(BF16), vb)
            mu = jnp.mean(o, axis=-1, keepdims=True)
            oc = o - mu
            var = jnp.mean(oc * oc, axis=-1, keepdims=True)
            gated = oc * lax.rsqrt(var + NORM_EPS) * gate_ref[rows, :].astype(F32)
            if backward:
                o_ref[rows, :] = o_scr[rows, :] + gated
            else:
                o_scr[rows, :] = gated
            return carry

        lax.fori_loop(0, n_chunks, body, 0)
        if emit_state:
            (sb_ref if backward else sf_ref)[0, 0, 0] = s_scr[...]

    @pl.when(direction == 0)
    def _():
        run(False)

    @pl.when(direction == 1)
    def _():
        run(True)


def retention(q, k, v, gates, row0, batch, seq_len, log_decay, state=None, emit_state=False):
    dk = q.shape[1] // RET_HEADS
    dv = v.shape[1] // RET_HEADS
    assert row0 % seq_len == 0
    s0 = row0 // seq_len
    in_specs = [
        pl.BlockSpec(memory_space=pltpu.SMEM),
        pl.BlockSpec((seq_len, dk), lambda b, h, r: (s0 + b, h)),
        pl.BlockSpec((seq_len, dk), lambda b, h, r: (s0 + b, h)),
        pl.BlockSpec((seq_len, dv), lambda b, h, r: (s0 + b, h)),
        pl.BlockSpec((seq_len, dv), lambda b, h, r: (s0 + b, r * RET_HEADS + h)),
    ]
    args = [log_decay, q, k, v, gates]
    state_spec = pl.BlockSpec((1, 1, 1, dk, dv), lambda b, h, r: (b, 0, h, 0, 0))
    if state is not None:
        in_specs += [state_spec, state_spec]
        args += list(state)
    out_shapes = [jax.ShapeDtypeStruct((batch * seq_len, RET_HEADS * dv), F32)]
    out_specs = [pl.BlockSpec((seq_len, dv), lambda b, h, r: (b, h))]
    if emit_state:
        out_shapes += [jax.ShapeDtypeStruct((batch, 1, RET_HEADS, dk, dv), F32)] * 2
        out_specs += [state_spec, state_spec]
    return pl.pallas_call(
        functools.partial(_retention_kernel, seq_len=seq_len, has_state=state is not None,
                          emit_state=emit_state),
        out_shape=out_shapes,
        grid=(batch, RET_HEADS, 2),
        in_specs=in_specs,
        out_specs=out_specs,
        scratch_shapes=[pltpu.VMEM((dk, dv), F32), pltpu.VMEM((seq_len, dv), F32)],
        compiler_params=_cparams("parallel", "parallel", "arbitrary"),
        name="retention",
    )(*args)


def _proj_residual_kernel(*refs, n_x, n_a, n_ctx_tiles):
    x_refs, a_refs = refs[:n_x], refs[n_x:n_x + n_a]
    w_ref, mod_ref, o_ref, y_scr = refs[n_x + n_a:]
    is_dec = pl.program_id(0) >= n_ctx_tiles

    def project(a_ref):
        y_scr[...] = mod_ref[0, 2:3, :] * jnp.dot(a_ref[...].astype(BF16), w_ref[...],
                                                  preferred_element_type=F32)

    def add(x_ref):
        o_ref[...] = x_ref[...] + y_scr[...]

    _by_part(is_dec, a_refs, project)
    _by_part(is_dec, x_refs, add)


def proj_residual(lay, x_parts, a_parts, w, mods):
    d = x_parts[0].shape[1]
    kdim = a_parts[0].shape[1]
    tm = lay.tm
    return pl.pallas_call(
        functools.partial(_proj_residual_kernel, n_x=len(x_parts), n_a=len(a_parts),
                          n_ctx_tiles=lay.n_ctx_tiles),
        out_shape=jax.ShapeDtypeStruct((lay.t, d), F32),
        grid=(lay.n_tiles,),
        in_specs=lay.row_specs(x_parts, d) + lay.row_specs(a_parts, kdim) + [
            pl.BlockSpec((kdim, d), lambda i: (0, 0)),
            pl.BlockSpec((1, N_MOD, d), lambda i: (lay.mod_row(i), 0, 0)),
        ],
        out_specs=pl.BlockSpec((tm, d), lambda i: (i, 0)),
        scratch_shapes=[pltpu.VMEM((tm, d), F32)],
        compiler_params=_cparams("arbitrary"),
        name="proj_residual",
    )(*x_parts, *a_parts, w, mods)


def _first_max_onehot(vals):
    m = vals[0]
    for v in vals[1:]:
        m = jnp.maximum(m, v)
    onehot, taken = [], None
    for v in vals:
        hit = v == m
        if taken is None:
            onehot.append(hit)
            taken = hit
        else:
            onehot.append(hit & jnp.logical_not(taken))
            taken = taken | hit
    return m, onehot


def _pick(onehot, vals):
    out = vals[-1]
    for oh, v in zip(onehot[-2::-1], vals[-2::-1]):
        out = jnp.where(oh, v, out)
    return out


def _router_kernel(x_ref, g_ref, mod_ref, wr_ref, bias_ref, tri_ref, h_ref, idx_ref, wcol_ref, rank_ref,
                   cnt_ref, ind_scr, wrow_scr):
    h = _modulated_norm(x_ref[...], g_ref[...], mod_ref, 3)
    h_ref[...] = h
    logits = _nt_dot(wr_ref[...], h.astype(BF16))
    s = jax.nn.sigmoid(logits)
    sel = s + bias_ref[...]
    neg_inf = jnp.full_like(sel[0:1], -jnp.inf)
    sel_rows = [sel[e:e + 1] for e in range(N_EXPERTS)]
    s_rows = [s[e:e + 1] for e in range(N_EXPERTS)]

    def top2(vals):
        m1, oh1 = _first_max_onehot(vals)
        rest = [jnp.where(o, neg_inf, v) for o, v in zip(oh1, vals)]
        m2, oh2 = _first_max_onehot(rest)
        return m1, m2, oh1, oh2

    grp_scores = []
    for g in range(N_GROUPS):
        m1, m2, _, _ = top2(sel_rows[g * EXPERTS_PER_GROUP:(g + 1) * EXPERTS_PER_GROUP])
        grp_scores.append(m1 + m2)
    _, in_grp = _first_max_onehot(grp_scores)
    cand_sel = [_pick(in_grp, [sel_rows[g * EXPERTS_PER_GROUP + k] for g in range(N_GROUPS)])
                for k in range(EXPERTS_PER_GROUP)]
    cand_s = [_pick(in_grp, [s_rows[g * EXPERTS_PER_GROUP + k] for g in range(N_GROUPS)])
              for k in range(EXPERTS_PER_GROUP)]
    _, _, oh1, oh2 = top2(cand_sel)
    w1 = _pick(oh1, cand_s)
    w2 = _pick(oh2, cand_s)
    denom = w1 + w2
    wrow_scr[...] = jnp.zeros_like(wrow_scr)
    wrow_scr[0:1, :] = w1 / denom
    wrow_scr[1:2, :] = w2 / denom
    wcol_ref[...] = jnp.transpose(wrow_scr[...])

    ints = [jnp.full(w1.shape, k, I32) for k in range(EXPERTS_PER_GROUP)]
    grp = _pick(in_grp, ints) * EXPERTS_PER_GROUP
    idx_ref[0:1, :] = grp + _pick(oh1, ints)
    idx_ref[1:2, :] = grp + _pick(oh2, ints)

    one, zero = jnp.ones_like(w1), jnp.zeros_like(w1)
    for g in range(N_GROUPS):
        for k in range(EXPERTS_PER_GROUP):
            e = g * EXPERTS_PER_GROUP + k
            ind_scr[e:e + 1, :] = jnp.where(in_grp[g] & (oh1[k] | oh2[k]), one, zero)
    ind = ind_scr[...].astype(BF16)
    ranks = jnp.dot(ind, tri_ref[...], preferred_element_type=F32)
    cnt_ref[0] = jnp.dot(ind, jnp.ones((ind.shape[1], LANES), BF16), preferred_element_type=F32)
    for slot, oh in ((0, oh1), (1, oh2)):
        r = zero
        for g in range(N_GROUPS):
            for k in range(EXPERTS_PER_GROUP):
                e = g * EXPERTS_PER_GROUP + k
                r = r + jnp.where(in_grp[g] & oh[k], ranks[e:e + 1], zero)
        rank_ref[slot:slot + 1, :] = r.astype(I32)


def route(lay, x, g, mods, w_router_t, router_bias):
    t, d = x.shape
    tm = lay.tm
    tri = jnp.triu(jnp.ones((tm, tm), BF16), k=1)
    pair = jax.ShapeDtypeStruct((TOP_K, t), I32)
    pair_spec = pl.BlockSpec((TOP_K, tm), lambda i: (0, i))
    return pl.pallas_call(
        _router_kernel,
        out_shape=[jax.ShapeDtypeStruct((t, d), F32), pair, jax.ShapeDtypeStruct((t, LANES), F32), pair,
                   jax.ShapeDtypeStruct((lay.n_tiles, N_EXPERTS, LANES), F32)],
        grid=(lay.n_tiles,),
        in_specs=[
            pl.BlockSpec((tm, d), lambda i: (i, 0)),
            pl.BlockSpec((1, d), lambda i: (0, 0)),
            pl.BlockSpec((1, N_MOD, d), lambda i: (lay.mod_row(i), 0, 0)),
            pl.BlockSpec((N_EXPERTS, d), lambda i: (0, 0)),
            pl.BlockSpec((N_EXPERTS, 1), lambda i: (0, 0)),
            pl.BlockSpec((tm, tm), lambda i: (0, 0)),
        ],
        out_specs=[pl.BlockSpec((tm, d), lambda i: (i, 0)), pair_spec,
                   pl.BlockSpec((tm, LANES), lambda i: (i, 0)), pair_spec,
                   pl.BlockSpec((1, N_EXPERTS, LANES), lambda i: (i, 0, 0))],
        scratch_shapes=[pltpu.VMEM((N_EXPERTS, tm), F32), pltpu.VMEM((LANES, tm), F32)],
        compiler_params=_cparams("parallel"),
        name="router",
    )(x, g.reshape(1, d), mods, w_router_t, router_bias.reshape(N_EXPERTS, 1), tri)


def dispatch_plan(lay, idx, rank, cnt, n_sorted):
    tm = lay.tm
    cnt_tile = cnt[:, :, 0].astype(I32)
    total = jnp.sum(cnt_tile, axis=0)
    padded = ((total + EXPERT_TILE - 1) // EXPERT_TILE) * EXPERT_TILE
    end = jnp.cumsum(padded)
    start = end - padded
    base = start[None, :] + jnp.cumsum(cnt_tile, axis=0) - cnt_tile
    idx3 = idx.reshape(TOP_K, lay.n_tiles, tm)
    pos = rank.reshape(TOP_K, lay.n_tiles, tm)
    for e in range(N_EXPERTS):
        pos = pos + jnp.where(idx3 == e, base[None, :, e, None], 0)
    tile_row = jnp.arange(n_sorted // EXPERT_TILE, dtype=I32) * EXPERT_TILE
    tile_expert = jnp.minimum(jnp.sum(end[None, :] <= tile_row[:, None], axis=1), N_EXPERTS - 1).astype(I32)
    n_valid = (end[-1] // EXPERT_TILE).astype(I32).reshape(1)
    return pos.reshape(TOP_K * lay.t).astype(I32), tile_expert, n_valid


def _sc_mesh():
    return plsc.VectorSubcoreMesh(core_axis_name="c", subcore_axis_name="s")


def _sc_worker_id():
    return lax.axis_index("s") * V7X_SC_CORES + lax.axis_index("c")


def sc_scatter_rows(src, pos, n_out_rows):
    t, d = src.shape
    n_idx = pos.shape[0]
    per_worker = n_idx // SC_WORKERS
    chunk = SC_CHUNK_ROWS
    assert n_idx % (SC_WORKERS * chunk) == 0 and t % chunk == 0

    @functools.partial(
        pl.kernel, mesh=_sc_mesh(), out_type=jax.ShapeDtypeStruct((n_out_rows, d), src.dtype),
        scratch_types=[pltpu.VMEM((chunk,), I32), pltpu.VMEM((chunk, d), src.dtype)],
        name="sc_scatter_rows")
    def scatter(src_hbm, pos_hbm, out_hbm, idx_v, rows_v):
        base = _sc_worker_id() * per_worker

        @pl.loop(0, per_worker // chunk)
        def _(i):
            a0 = base + i * chunk
            pltpu.sync_copy(pos_hbm.at[pl.ds(a0, chunk)], idx_v)
            pltpu.sync_copy(src_hbm.at[pl.ds(lax.rem(a0, t), chunk)], rows_v)
            pltpu.sync_copy(rows_v, out_hbm.at[idx_v])

    return scatter(src, pos)


def sc_gather_rows(table, idx):
    _, d = table.shape
    n_idx = idx.shape[0]
    per_worker = n_idx // SC_WORKERS
    chunk = SC_CHUNK_ROWS
    assert n_idx % (SC_WORKERS * chunk) == 0

    @functools.partial(
        pl.kernel, mesh=_sc_mesh(), out_type=jax.ShapeDtypeStruct((n_idx, d), table.dtype),
        scratch_types=[pltpu.VMEM((chunk,), I32), pltpu.VMEM((chunk, d), table.dtype)],
        name="sc_gather_rows")
    def gather(table_hbm, idx_hbm, out_hbm, idx_v, rows_v):
        base = _sc_worker_id() * per_worker

        @pl.loop(0, per_worker // chunk)
        def _(i):
            a0 = base + i * chunk
            pltpu.sync_copy(idx_hbm.at[pl.ds(a0, chunk)], idx_v)
            pltpu.sync_copy(table_hbm.at[idx_v], rows_v)
            pltpu.sync_copy(rows_v, out_hbm.at[pl.ds(a0, chunk)])

    return gather(table, idx)


def _experts_kernel(te_ref, nv_ref, x_ref, wg_ref, wu_ref, wd_ref, y_ref, wg_scr, wu_scr, wd_scr):
    i = pl.program_id(0)

    @pl.when(i < nv_ref[0])
    def _():
        @pl.when((i == 0) | (te_ref[i] != te_ref[jnp.maximum(i - 1, 0)]))
        def _():
            wg_scr[...] = wg_ref[0].astype(BF16)
            wu_scr[...] = wu_ref[0].astype(BF16)
            wd_scr[...] = wd_ref[0].astype(BF16)

        x = x_ref[...].astype(BF16)
        a = (jax.nn.silu(jnp.dot(x, wg_scr[...], preferred_element_type=F32))
             * jnp.dot(x, wu_scr[...], preferred_element_type=F32))
        y_ref[...] = jnp.dot(a.astype(BF16), wd_scr[...], preferred_element_type=F32)


def grouped_experts(xs, tile_expert, n_valid, wg, wu, wd):
    n_rows, d = xs.shape
    _, _, de = wg.shape
    tm = EXPERT_TILE
    row_map = lambda i, te, nv: (jnp.minimum(i, nv[0] - 1), 0)
    grid_spec = pltpu.PrefetchScalarGridSpec(
        num_scalar_prefetch=2,
        grid=(n_rows // tm,),
        in_specs=[
            pl.BlockSpec((tm, d), row_map),
            pl.BlockSpec((1, d, de), lambda i, te, nv: (te[i], 0, 0)),
            pl.BlockSpec((1, d, de), lambda i, te, nv: (te[i], 0, 0)),
            pl.BlockSpec((1, de, d), lambda i, te, nv: (te[i], 0, 0)),
        ],
        out_specs=pl.BlockSpec((tm, d), row_map),
        scratch_shapes=[pltpu.VMEM((d, de), BF16), pltpu.VMEM((d, de), BF16), pltpu.VMEM((de, d), BF16)],
    )
    return pl.pallas_call(
        _experts_kernel,
        out_shape=jax.ShapeDtypeStruct((n_rows, d), F32),
        grid_spec=grid_spec,
        compiler_params=_cparams("arbitrary"),
        name="grouped_experts",
    )(tile_expert, n_valid, xs, wg, wu, wd)


def _combine_kernel(x_ref, y_ref, w_ref, mod_ref, fg_ref, *o_refs, final_norm, n_ctx_tiles):
    w = w_ref[...]
    y = w[:, 0:1] * y_ref[0] + w[:, 1:2] * y_ref[1]
    out = x_ref[...] + mod_ref[0, 5:6, :] * y
    if final_norm:
        ms = jnp.mean(out * out, axis=-1, keepdims=True)
        out = out * lax.rsqrt(ms + NORM_EPS) * fg_ref[...]

    def store(o_ref):
        o_ref[...] = out
    _by_part(pl.program_id(0) >= n_ctx_tiles, o_refs, store)


def combine_residual(lay, x, y_pair, w_col, mods, final_g, final_norm, split):
    t, d = x.shape
    tm = lay.tm
    if split:
        out_shape = [jax.ShapeDtypeStruct((lay.t_ctx, d), F32), jax.ShapeDtypeStruct((lay.t_dec, d), F32)]
        out_specs = lay.row_specs([None, None], d)
    else:
        out_shape = [jax.ShapeDtypeStruct((t, d), F32)]
        out_specs = lay.row_specs([None], d)
    return pl.pallas_call(
        functools.partial(_combine_kernel, final_norm=final_norm, n_ctx_tiles=lay.n_ctx_tiles),
        out_shape=out_shape,
        grid=(lay.n_tiles,),
        in_specs=[
            pl.BlockSpec((tm, d), lambda i: (i, 0)),
            pl.BlockSpec((TOP_K, tm, d), lambda i: (0, i, 0)),
            pl.BlockSpec((tm, LANES), lambda i: (i, 0)),
            pl.BlockSpec((1, N_MOD, d), lambda i: (lay.mod_row(i), 0, 0)),
            pl.BlockSpec((1, d), lambda i: (0, 0)),
        ],
        out_specs=out_specs,
        compiler_params=_cparams("arbitrary"),
        name="combine_residual",
    )(x, y_pair, w_col, mods, final_g.reshape(1, d))


def group_moe_residual(lay, x, mods, p, layer, last):
    t, d = x.shape
    h, idx, w_col, rank, cnt = route(lay, x, p["norm_ffn_g"][layer], mods, p["w_router_t"], p["router_bias"])
    n_assign = TOP_K * t
    row_quantum = SC_WORKERS * SC_CHUNK_ROWS
    n_sorted = n_assign + N_EXPERTS * (EXPERT_TILE - 1)
    n_sorted = -(-n_sorted // row_quantum) * row_quantum
    n_sorted = -(-n_sorted // EXPERT_TILE) * EXPERT_TILE
    pos, tile_expert, n_valid = dispatch_plan(lay, idx, rank, cnt, n_sorted)
    xs = sc_scatter_rows(h, pos, n_sorted)
    ys = grouped_experts(xs, tile_expert, n_valid, p["moe_w_gate"][layer], p["moe_w_up"][layer],
                         p["moe_w_down"][layer])
    y_pair = sc_gather_rows(ys, pos).reshape(TOP_K, t, d)
    return combine_residual(lay, x, y_pair, w_col, mods, p["final_norm_g"], final_norm=last, split=last)


def _rope_angles(n, d):
    n_rows = n // GRID_W
    row = jnp.repeat(jnp.arange(n_rows), GRID_W).astype(F32)
    col = jnp.tile(jnp.arange(GRID_W), n_rows).astype(F32)
    nf = d // 4
    freqs = jnp.power(ROPE_BASE, -jnp.arange(nf, dtype=F32) / nf)
    ang = jnp.concatenate([row[:, None] * freqs, col[:, None] * freqs], axis=-1)
    return jnp.cos(ang), jnp.sin(ang)


def kernel(x_prompt, x_sample, cache_attn_k, cache_attn_v, state_ret_fwd, state_ret_bwd, c, c_ctx, w_ada, b_ada, norm_mix_g, norm_ffn_g, final_norm_g, da_w_qkv, da_lambda_q1, da_lambda_k1, da_lambda_q2, da_lambda_k2, da_subln_g, da_w_o, ret_w_qkv, ret_w_gate_fwd, ret_w_gate_bwd, ret_decay_fwd, ret_decay_bwd, ret_w_o, w_router, router_bias, moe_w_gate, moe_w_up, moe_w_down):
    b_ctx, n_ctx, d = x_prompt.shape
    b_dec, n_dec, _ = x_sample.shape
    past = cache_attn_k.shape[2]
    n_attn = cache_attn_k.shape[1]
    depth = w_ada.shape[0]
    assert b_dec + 1 <= MOD_ROWS
    lay = Layout(b_ctx, n_ctx, b_dec, n_dec)

    cond = jnp.zeros((MOD_ROWS, d), F32).at[0].set(c_ctx).at[1:1 + b_dec].set(c)
    mods_all = ada_modulation(cond, w_ada, b_ada)

    p = {
        "norm_ffn_g": norm_ffn_g, "final_norm_g": final_norm_g,
        "w_router_t": w_router.T.astype(BF16), "router_bias": router_bias.astype(F32),
        "moe_w_gate": moe_w_gate, "moe_w_up": moe_w_up, "moe_w_down": moe_w_down,
    }
    ret_log_decay = jnp.stack([jax.nn.log_sigmoid(ret_decay_fwd.astype(F32)),
                               jax.nn.log_sigmoid(ret_decay_bwd.astype(F32))], axis=1)
    ck_all = cache_attn_k.reshape(b_dec, n_attn, past, -1)
    cv_all = cache_attn_v.reshape(b_dec, n_attn, past, -1)

    x_parts = [x_prompt.reshape(lay.t_ctx, d), x_sample.reshape(lay.t_dec, d)]
    new_k, new_v, new_sf, new_sb = [], [], [], []
    for i in range(depth):
        mods = mods_all[i]
        j = i // 2
        if i % 2 == 0:
            lam_init = 0.8 - 0.6 * math.exp(-0.3 * i)
            qkw = DA_HEADS * 2 * DA_HEAD_DIM
            vw = DA_HEADS * DA_V_DIM
            cos, sin = _rope_angles(n_dec, DA_HEAD_DIM)
            rope = (jnp.tile(cos, (1, 4)), jnp.concatenate([-sin, sin, -sin, sin], axis=-1))
            q, k, v, k_ctx, v_ctx = norm_mod_matmul(
                lay, x_parts, norm_mix_g[i], mods, da_w_qkv[j].astype(BF16),
                [(0, qkw, "rope64", 1.0, BF16, "all"), (qkw, qkw, "rope64", 1.0, BF16, "all"),
                 (2 * qkw, vw, "plain", 1.0, BF16, "all"),
                 (qkw, qkw, "plain", 1.0, F32, "ctx"), (2 * qkw, vw, "plain", 1.0, F32, "ctx")],
                rope)
            lam_params = jnp.stack([da_lambda_q1[j], da_lambda_k1[j], da_lambda_q2[j], da_lambda_k2[j]])
            mix = [diff_attention(q, k, v, 0, b_ctx, n_ctx, lam_params, da_subln_g[j], lam_init),
                   diff_attention(q, k, v, lay.t_ctx, b_dec, n_dec, lam_params, da_subln_g[j], lam_init,
                                  cache=(ck_all[:, j], cv_all[:, j]))]
            x = proj_residual(lay, x_parts, mix, da_w_o[j].astype(BF16), mods)
            new_k.append(k_ctx.reshape(b_ctx, n_ctx, DA_HEADS, 2, DA_HEAD_DIM))
            new_v.append(v_ctx.reshape(b_ctx, n_ctx, DA_HEADS, DA_V_DIM))
        else:
            kd = ret_w_qkv.shape[2] // 4
            dv = 2 * kd
            w_all = jnp.concatenate([ret_w_qkv[j], ret_w_gate_fwd[j], ret_w_gate_bwd[j]], axis=-1).astype(BF16)
            q, k, v, gates = norm_mod_matmul(
                lay, x_parts, norm_mix_g[i], mods, w_all,
                [(0, kd, "rope256", 1.0, BF16, "all"),
                 (kd, kd, "rope256", (kd // RET_HEADS) ** -0.5, F32, "all"),
                 (2 * kd, dv, "plain", 1.0, BF16, "all"), (2 * kd + dv, 2 * dv, "silu", 1.0, F32, "all")],
                _rope_angles(n_dec, kd // RET_HEADS))
            o_ctx, sf, sb = retention(q, k, v, gates, 0, b_ctx, n_ctx, ret_log_decay[j], emit_state=True)
            (o_dec,) = retention(q, k, v, gates, lay.t_ctx, b_dec, n_dec, ret_log_decay[j],
                                 state=(state_ret_fwd[:, j:j + 1], state_ret_bwd[:, j:j + 1]))
            x = proj_residual(lay, x_parts, [o_ctx, o_dec], ret_w_o[j].astype(BF16), mods)
            new_sf.append(sf)
            new_sb.append(sb)
        x_parts = group_moe_residual(lay, x, mods, p, i, last=(i == depth - 1))

    y_ctx, y_dec = x_parts
    return (y_ctx.reshape(b_ctx, n_ctx, d), y_dec.reshape(b_dec, n_dec, d),
            jnp.stack(new_k, axis=1), jnp.stack(new_v, axis=1),
            jnp.concatenate(new_sf, axis=1), jnp.concatenate(new_sb, axis=1))
```

```python
import functools
import math

import jax
import jax.numpy as jnp
from jax import lax
from jax.experimental import pallas as pl
from jax.experimental.pallas import tpu as pltpu
from jax.experimental.pallas import tpu_sc as plsc

F32 = jnp.float32
BF16 = jnp.bfloat16
I32 = jnp.int32

GRID_W = 64
ROPE_BASE = 10000.0
NORM_EPS = 1e-6
DA_HEADS = 8
DA_HEAD_DIM = 64
DA_V_DIM = 2 * DA_HEAD_DIM
RET_HEADS = 4
RET_CHUNK = 128
N_EXPERTS = 16
N_GROUPS = 4
EXPERTS_PER_GROUP = N_EXPERTS // N_GROUPS
TOP_K = 2
N_MOD = 6
MOD_ROWS = 16
LANES = 128

V7X_VMEM_LIMIT = 56 * 1024 * 1024
V7X_SC_CORES = 2
V7X_SC_SUBCORES = 16
SC_WORKERS = V7X_SC_CORES * V7X_SC_SUBCORES
SC_CHUNK_ROWS = 32

ROW_TILE = 512
EXPERT_TILE = 512


def _cparams(*sem):
    return pltpu.CompilerParams(dimension_semantics=sem, vmem_limit_bytes=V7X_VMEM_LIMIT)


def _nt_dot(a, b):
    return lax.dot_general(a, b, (((1,), (1,)), ((), ())), preferred_element_type=F32)


def _tn_dot(a, b):
    return lax.dot_general(a, b, (((0,), (0,)), ((), ())), preferred_element_type=F32)


class Layout:
    def __init__(self, b_ctx, n_ctx, b_dec, n_dec):
        self.b_ctx, self.n_ctx, self.b_dec, self.n_dec = b_ctx, n_ctx, b_dec, n_dec
        self.t_ctx, self.t_dec = b_ctx * n_ctx, b_dec * n_dec
        self.t = self.t_ctx + self.t_dec
        self.tm = min(ROW_TILE, n_dec, self.t_ctx)
        assert self.t_ctx % self.tm == 0 and n_dec % self.tm == 0
        assert self.t_ctx % n_dec == 0 and self.t_ctx % n_ctx == 0
        self.n_ctx_tiles = self.t_ctx // self.tm
        self.n_tiles = self.t // self.tm

    def mod_row(self, i):
        r = i * self.tm
        return jnp.where(r < self.t_ctx, 0, 1 + (r - self.t_ctx) // self.n_dec)

    def part_tile(self, part, i):
        if part == 0:
            return jnp.minimum(i, self.n_ctx_tiles - 1)
        return jnp.maximum(i - self.n_ctx_tiles, 0)

    def row_specs(self, arrays, width, extra_grid_dims=0):
        pad = (0,) * 0
        if len(arrays) == 1:
            return [pl.BlockSpec((self.tm, width), lambda i, *_: (i, 0))]
        return [pl.BlockSpec((self.tm, width), lambda i, *_, p=p: (self.part_tile(p, i), 0)) for p in (0, 1)]


def _ada_kernel(c_ref, w_ref, b_ref, o_ref):
    s = jax.nn.silu(c_ref[...]).astype(BF16)
    acc = jnp.dot(s, w_ref[0].astype(BF16), preferred_element_type=F32)
    o_ref[0] = acc + b_ref[0]


def ada_modulation(cond, w_ada, b_ada):
    depth, d, n = w_ada.shape
    tn = 1536
    out = pl.pallas_call(
        _ada_kernel,
        out_shape=jax.ShapeDtypeStruct((depth, MOD_ROWS, n), F32),
        grid=(depth, n // tn),
        in_specs=[
            pl.BlockSpec((MOD_ROWS, d), lambda l, j: (0, 0)),
            pl.BlockSpec((1, d, tn), lambda l, j: (l, 0, j)),
            pl.BlockSpec((1, 1, tn), lambda l, j: (l, 0, j)),
        ],
        out_specs=pl.BlockSpec((1, MOD_ROWS, tn), lambda l, j: (l, 0, j)),
        compiler_params=_cparams("parallel", "parallel"),
        name="ada_modulation",
    )(cond, w_ada, b_ada.reshape(depth, 1, n))
    return out.reshape(depth, MOD_ROWS, N_MOD, d)


def _modulated_norm(x, g, mod_ref, shift_idx):
    ms = jnp.mean(x * x, axis=-1, keepdims=True)
    y = x * lax.rsqrt(ms + NORM_EPS) * g
    return y * (1.0 + mod_ref[0, shift_idx + 1:shift_idx + 2, :]) + mod_ref[0, shift_idx:shift_idx + 1, :]


def _by_part(is_dec, refs, fn):
    if len(refs) == 1:
        fn(refs[0])
        return
    pl.when(jnp.logical_not(is_dec))(lambda: fn(refs[0]))
    pl.when(is_dec)(lambda: fn(refs[1]))


def _rope64(a, cos, sin_signed, first_half):
    partner = jnp.where(first_half, pltpu.roll(a, 96, 1), pltpu.roll(a, 32, 1))
    return a * cos + partner * sin_signed


def _nmm_kernel(*refs, n_x, outs, tn, n_ctx_tiles):
    x_refs = refs[:n_x]
    g_ref, mod_ref, w_ref, cos_ref, sin_ref = refs[n_x:n_x + 5]
    out_refs = refs[n_x + 5:n_x + 5 + len(outs)]
    h_scr = refs[n_x + 5 + len(outs)]
    j = pl.program_id(1)
    is_dec = pl.program_id(0) >= n_ctx_tiles
    is_ctx = jnp.logical_not(is_dec)

    @pl.when(j == 0)
    def _():
        def fill(x_ref):
            h_scr[...] = _modulated_norm(x_ref[...], g_ref[...], mod_ref, 0).astype(BF16)
        _by_part(is_dec, x_refs, fill)

    acc = jnp.dot(h_scr[...], w_ref[...], preferred_element_type=F32)
    tm = acc.shape[0]

    def rotated(a, kind):
        cos, sin = cos_ref[...], sin_ref[...]
        if kind == "rope64":
            lane = lax.broadcasted_iota(I32, (tm, LANES), 1)
            first_half = (lane & 32) == 0
            return [(slice(c * 128, (c + 1) * 128), _rope64(a[:, c * 128:(c + 1) * 128], cos, sin, first_half))
                    for c in range(tn // 128)]
        pieces = []
        for c in range(tn // 256):
            s1 = slice(c * 256, c * 256 + 128)
            s2 = slice(c * 256 + 128, (c + 1) * 256)
            x1, x2 = a[:, s1], a[:, s2]
            pieces += [(s1, x1 * cos - x2 * sin), (s2, x2 * cos + x1 * sin)]
        return pieces

    for o_ref, (lo, hi, kind, scale, rows) in zip(out_refs, outs):
        in_seg = (j >= lo) & (j < hi)

        def plain(o_ref=o_ref, kind=kind, scale=scale):
            a = acc if scale == 1.0 else acc * scale
            o_ref[...] = (jax.nn.silu(a) if kind == "silu" else a).astype(o_ref.dtype)

        def rope(o_ref=o_ref, kind=kind, scale=scale):
            a = acc if scale == 1.0 else acc * scale
            for sl, val in rotated(a, kind):
                o_ref[:, sl] = val.astype(o_ref.dtype)

        if rows == "ctx":
            pl.when(in_seg & is_ctx)(plain)
        elif kind in ("plain", "silu"):
            pl.when(in_seg)(plain)
        else:
            pl.when(in_seg & is_ctx)(plain)
            pl.when(in_seg & is_dec)(rope)


def norm_mod_matmul(lay, x_parts, g, mods, w, outputs, rope_tables, tn=1024):
    d = x_parts[0].shape[1]
    tm = lay.tm
    n_total = w.shape[1]
    assert n_total % tn == 0
    n_ctx_tiles = lay.n_ctx_tiles
    outs, out_shapes, out_specs = [], [], []
    for col0, width, kind, scale, dtype, rows in outputs:
        assert width % tn == 0 and col0 % tn == 0
        lo, n_blk = col0 // tn, width // tn
        outs.append((lo, lo + n_blk, kind, float(scale), rows))
        if rows == "ctx":
            out_shapes.append(jax.ShapeDtypeStruct((lay.t_ctx, width), dtype))
            out_specs.append(pl.BlockSpec(
                (tm, tn), lambda i, j, lo=lo, n_blk=n_blk: (
                    lay.part_tile(0, i), jnp.where(i < n_ctx_tiles, jnp.clip(j - lo, 0, n_blk - 1), n_blk - 1))))
        else:
            out_shapes.append(jax.ShapeDtypeStruct((lay.t, width), dtype))
            out_specs.append(pl.BlockSpec(
                (tm, tn), lambda i, j, lo=lo, n_blk=n_blk: (i, jnp.clip(j - lo, 0, n_blk - 1))))
    blocks_per_seq = lay.n_dec // tm
    rope_spec = pl.BlockSpec((tm, LANES), lambda i, j: (lay.part_tile(1, i) % blocks_per_seq, 0))
    return pl.pallas_call(
        functools.partial(_nmm_kernel, n_x=len(x_parts), outs=tuple(outs), tn=tn, n_ctx_tiles=n_ctx_tiles),
        out_shape=out_shapes,
        grid=(lay.n_tiles, n_total // tn),
        in_specs=lay.row_specs(x_parts, d) + [
            pl.BlockSpec((1, d), lambda i, j: (0, 0)),
            pl.BlockSpec((1, N_MOD, d), lambda i, j: (lay.mod_row(i), 0, 0)),
            pl.BlockSpec((d, tn), lambda i, j: (0, j)),
            rope_spec, rope_spec,
        ],
        out_specs=out_specs,
        scratch_shapes=[pltpu.VMEM((tm, d), BF16)],
        compiler_params=_cparams("arbitrary", "arbitrary"),
        name="norm_mod_matmul",
    )(*x_parts, g.reshape(1, d), mods, w, *rope_tables)


def _diff_attn_kernel(*refs, lam_init, has_cache, heads_per_step):
    lam_ref, q_ref, k_ref, v_ref = refs[:4]
    pos = 4
    if has_cache:
        ck_ref, cv_ref = refs[4:6]
        pos = 6
    g_ref, o_ref = refs[pos], refs[pos + 1]
    hw = 2 * DA_HEAD_DIM

    lp = lam_ref[...]
    lam = (jnp.exp(jnp.sum(lp[0:1] * lp[1:2], axis=-1, keepdims=True))
           - jnp.exp(jnp.sum(lp[2:3] * lp[3:4], axis=-1, keepdims=True)) + lam_init)
    lane = lax.broadcasted_iota(I32, (q_ref.shape[0], hw), 1)

    for hh in range(heads_per_step):
        cols = slice(hh * hw, (hh + 1) * hw)
        q = q_ref[:, cols]
        kn = k_ref[:, cols]
        vn = v_ref[:, cols]
        if has_cache:
            kc = ck_ref[0].astype(BF16)
            vc = cv_ref[0].astype(BF16)
        exps = []
        for comp in range(2):
            in_comp = (lane < DA_HEAD_DIM) if comp == 0 else (lane >= DA_HEAD_DIM)
            qc = jnp.where(in_comp, q, jnp.zeros_like(q))
            sn = _nt_dot(qc, kn)
            m = jnp.max(sn, axis=-1, keepdims=True)
            if has_cache:
                sc = _nt_dot(qc, kc)
                m = jnp.maximum(m, jnp.max(sc, axis=-1, keepdims=True))
            en = jnp.exp2(sn - m)
            denom = jnp.sum(en, axis=-1, keepdims=True)
            ec = None
            if has_cache:
                ec = jnp.exp2(sc - m)
                denom = denom + jnp.sum(ec, axis=-1, keepdims=True)
            exps.append((en, ec, 1.0 / denom))
        c0, c1 = exps[0][2], lam * exps[1][2]
        o = jnp.dot((exps[0][0] * c0 - exps[1][0] * c1).astype(BF16), vn, preferred_element_type=F32)
        if has_cache:
            o = o + jnp.dot((exps[0][1] * c0 - exps[1][1] * c1).astype(BF16), vc, preferred_element_type=F32)
        ms = jnp.mean(o * o, axis=-1, keepdims=True)
        o_ref[:, cols] = (o * lax.rsqrt(ms + NORM_EPS) * g_ref[...]) * (1.0 - lam_init)


def diff_attention(q, k, v, row0, batch, seq_len, lam_params, subln_g, lam_init, cache=None, tq=256,
                   heads_per_step=1):
    width = q.shape[1]
    tq = min(tq, seq_len)
    nq = seq_len // tq
    hw = 2 * DA_HEAD_DIM
    bw = heads_per_step * hw
    assert row0 % seq_len == 0 and DA_HEADS % heads_per_step == 0
    q0, s0 = row0 // tq, row0 // seq_len
    in_specs = [
        pl.BlockSpec((4, DA_HEAD_DIM), lambda b, h, i: (0, 0)),
        pl.BlockSpec((tq, bw), lambda b, h, i: (q0 + b * nq + i, h)),
        pl.BlockSpec((seq_len, bw), lambda b, h, i: (s0 + b, h)),
        pl.BlockSpec((seq_len, bw), lambda b, h, i: (s0 + b, h)),
    ]
    args = [lam_params, q, k, v]
    if cache is not None:
        assert heads_per_step == 1
        ck, cv = cache
        past = ck.shape[1]
        in_specs += [pl.BlockSpec((1, past, hw), lambda b, h, i: (b, 0, h)),
                     pl.BlockSpec((1, past, hw), lambda b, h, i: (b, 0, h))]
        args += [ck, cv]
    in_specs.append(pl.BlockSpec((1, hw), lambda b, h, i: (0, 0)))
    args.append(subln_g.reshape(1, hw))
    return pl.pallas_call(
        functools.partial(_diff_attn_kernel, lam_init=lam_init, has_cache=cache is not None,
                          heads_per_step=heads_per_step),
        out_shape=jax.ShapeDtypeStruct((batch * seq_len, width), F32),
        grid=(batch, DA_HEADS // heads_per_step, nq),
        in_specs=in_specs,
        out_specs=pl.BlockSpec((tq, bw), lambda b, h, i: (b * nq + i, h)),
        compiler_params=_cparams("parallel", "parallel", "arbitrary"),
        name="diff_attention",
    )(*args)


def _retention_kernel(*refs, seq_len, has_state, emit_state):
    lg_ref, q_ref, k_ref, v_ref, gate_ref = refs[:5]
    pos = 5
    if has_state:
        s0f_ref, s0b_ref = refs[5:7]
        pos = 7
    o_ref = refs[pos]
    pos += 1
    if emit_state:
        sf_ref, sb_ref = refs[pos:pos + 2]
        pos += 2
    s_scr, o_scr = refs[pos], refs[pos + 1]

    h = pl.program_id(1)
    direction = pl.program_id(2)
    c_len = RET_CHUNK
    n_chunks = seq_len // c_len
    row = lax.broadcasted_iota(I32, (c_len, c_len), 0)
    colm = lax.broadcasted_iota(I32, (c_len, c_len), 1)
    rel = (row - colm).astype(F32)
    posv = lax.broadcasted_iota(I32, (c_len, 1), 0).astype(F32)

    def run(backward):
        lg = lg_ref[1 if backward else 0, h]
        if backward:
            intra = jnp.where(rel <= 0, jnp.exp(-rel * lg), 0.0)
            q_decay = jnp.exp((c_len - posv) * lg)
            k_decay = jnp.exp(posv * lg)
        else:
            intra = jnp.where(rel >= 0, jnp.exp(rel * lg), 0.0)
            q_decay = jnp.exp((posv + 1.0) * lg)
            k_decay = jnp.exp((c_len - 1.0 - posv) * lg)
        chunk_decay = jnp.exp(jnp.zeros((1, 1), F32) + c_len * lg)

        if has_state:
            s_scr[...] = (s0b_ref if backward else s0f_ref)[0, 0, 0]
        else:
            s_scr[...] = jnp.zeros_like(s_scr)

        def body(ci, carry):
            c = (n_chunks - 1 - ci) if backward else ci
            r0 = pl.multiple_of(c * c_len, c_len)
            rows = pl.ds(r0, c_len)
            qb = q_ref[rows, :].astype(BF16)
            kf = k_ref[rows, :].astype(F32)
            vb = v_ref[rows, :].astype(BF16)
            a = _nt_dot(qb, kf.astype(BF16)) * intra
            s = s_scr[...]
            o = (jnp.dot(a.astype(BF16), vb, preferred_element_type=F32)
                 + jnp.dot(qb, s.astype(BF16), preferred_element_type=F32) * q_decay)
            s_scr[...] = chunk_decay * s + _tn_dot((kf * k_decay).astype(BF16), vb)
            mu = jnp.mean(o, axis=-1, keepdims=True)
            oc = o - mu
            var = jnp.mean(oc * oc, axis=-1, keepdims=True)
            gated = oc * lax.rsqrt(var + NORM_EPS) * gate_ref[rows, :].astype(F32)
            if backward:
                o_ref[rows, :] = o_scr[rows, :] + gated
            else:
                o_scr[rows, :] = gated
            return carry

        lax.fori_loop(0, n_chunks, body, 0)
        if emit_state:
            (sb_ref if backward else sf_ref)[0, 0, 0] = s_scr[...]

    @pl.when(direction == 0)
    def _():
        run(False)

    @pl.when(direction == 1)
    def _():
        run(True)


def retention(q, k, v, gates, row0, batch, seq_len, log_decay, state=None, emit_state=False):
    dk = q.shape[1] // RET_HEADS
    dv = v.shape[1] // RET_HEADS
    assert row0 % seq_len == 0
    s0 = row0 // seq_len
    in_specs = [
        pl.BlockSpec(memory_space=pltpu.SMEM),
        pl.BlockSpec((seq_len, dk), lambda b, h, r: (s0 + b, h)),
        pl.BlockSpec((seq_len, dk), lambda b, h, r: (s0 + b, h)),
        pl.BlockSpec((seq_len, dv), lambda b, h, r: (s0 + b, h)),
        pl.BlockSpec((seq_len, dv), lambda b, h, r: (s0 + b, r * RET_HEADS + h)),
    ]
    args = [log_decay, q, k, v, gates]
    state_spec = pl.BlockSpec((1, 1, 1, dk, dv), lambda b, h, r: (b, 0, h, 0, 0))
    if state is not None:
        in_specs += [state_spec, state_spec]
        args += list(state)
    out_shapes = [jax.ShapeDtypeStruct((batch * seq_len, RET_HEADS * dv), F32)]
    out_specs = [pl.BlockSpec((seq_len, dv), lambda b, h, r: (b, h))]
    if emit_state:
        out_shapes += [jax.ShapeDtypeStruct((batch, 1, RET_HEADS, dk, dv), F32)] * 2
        out_specs += [state_spec, state_spec]
    return pl.pallas_call(
        functools.partial(_retention_kernel, seq_len=seq_len, has_state=state is not None,
                          emit_state=emit_state),
        out_shape=out_shapes,
        grid=(batch, RET_HEADS, 2),
        in_specs=in_specs,
        out_specs=out_specs,
        scratch_shapes=[pltpu.VMEM((dk, dv), F32), pltpu.VMEM((seq_len, dv), F32)],
        compiler_params=_cparams("parallel", "parallel", "arbitrary"),
        name="retention",
    )(*args)


def _proj_residual_kernel(*refs, n_x, n_a, n_ctx_tiles):
    x_refs, a_refs = refs[:n_x], refs[n_x:n_x + n_a]
    w_ref, mod_ref, o_ref, y_scr = refs[n_x + n_a:]
    is_dec = pl.program_id(0) >= n_ctx_tiles

    def project(a_ref):
        y_scr[...] = mod_ref[0, 2:3, :] * jnp.dot(a_ref[...].astype(BF16), w_ref[...],
                                                  preferred_element_type=F32)

    def add(x_ref):
        o_ref[...] = x_ref[...] + y_scr[...]

    _by_part(is_dec, a_refs, project)
    _by_part(is_dec, x_refs, add)


def proj_residual(lay, x_parts, a_parts, w, mods):
    d = x_parts[0].shape[1]
    kdim = a_parts[0].shape[1]
    tm = lay.tm
    return pl.pallas_call(
        functools.partial(_proj_residual_kernel, n_x=len(x_parts), n_a=len(a_parts),
                          n_ctx_tiles=lay.n_ctx_tiles),
        out_shape=jax.ShapeDtypeStruct((lay.t, d), F32),
        grid=(lay.n_tiles,),
        in_specs=lay.row_specs(x_parts, d) + lay.row_specs(a_parts, kdim) + [
            pl.BlockSpec((kdim, d), lambda i: (0, 0)),
            pl.BlockSpec((1, N_MOD, d), lambda i: (lay.mod_row(i), 0, 0)),
        ],
        out_specs=pl.BlockSpec((tm, d), lambda i: (i, 0)),
        scratch_shapes=[pltpu.VMEM((tm, d), F32)],
        compiler_params=_cparams("arbitrary"),
        name="proj_residual",
    )(*x_parts, *a_parts, w, mods)


def _first_max_onehot(vals):
    m = vals[0]
    for v in vals[1:]:
        m = jnp.maximum(m, v)
    onehot, taken = [], None
    for v in vals:
        hit = v == m
        if taken is None:
            onehot.append(hit)
            taken = hit
        else:
            onehot.append(hit & jnp.logical_not(taken))
            taken = taken | hit
    return m, onehot


def _pick(onehot, vals):
    out = vals[-1]
    for oh, v in zip(onehot[-2::-1], vals[-2::-1]):
        out = jnp.where(oh, v, out)
    return out


def _router_kernel(x_ref, g_ref, mod_ref, wr_ref, bias_ref, tri_ref, h_ref, idx_ref, wcol_ref, rank_ref,
                   cnt_ref, ind_scr, wrow_scr):
    h = _modulated_norm(x_ref[...], g_ref[...], mod_ref, 3)
    h_ref[...] = h
    logits = _nt_dot(wr_ref[...], h.astype(BF16))
    s = jax.nn.sigmoid(logits)
    sel = s + bias_ref[...]
    neg_inf = jnp.full_like(sel[0:1], -jnp.inf)
    sel_rows = [sel[e:e + 1] for e in range(N_EXPERTS)]
    s_rows = [s[e:e + 1] for e in range(N_EXPERTS)]

    def top2(vals):
        m1, oh1 = _first_max_onehot(vals)
        rest = [jnp.where(o, neg_inf, v) for o, v in zip(oh1, vals)]
        m2, oh2 = _first_max_onehot(rest)
        return m1, m2, oh1, oh2

    grp_scores = []
    for g in range(N_GROUPS):
        m1, m2, _, _ = top2(sel_rows[g * EXPERTS_PER_GROUP:(g + 1) * EXPERTS_PER_GROUP])
        grp_scores.append(m1 + m2)
    _, in_grp = _first_max_onehot(grp_scores)
    cand_sel = [_pick(in_grp, [sel_rows[g * EXPERTS_PER_GROUP + k] for g in range(N_GROUPS)])
                for k in range(EXPERTS_PER_GROUP)]
    cand_s = [_pick(in_grp, [s_rows[g * EXPERTS_PER_GROUP + k] for g in range(N_GROUPS)])
              for k in range(EXPERTS_PER_GROUP)]
    _, _, oh1, oh2 = top2(cand_sel)
    w1 = _pick(oh1, cand_s)
    w2 = _pick(oh2, cand_s)
    denom = w1 + w2
    wrow_scr[...] = jnp.zeros_like(wrow_scr)
    wrow_scr[0:1, :] = w1 / denom
    wrow_scr[1:2, :] = w2 / denom
    wcol_ref[...] = jnp.transpose(wrow_scr[...])

    ints = [jnp.full(w1.shape, k, I32) for k in range(EXPERTS_PER_GROUP)]
    grp = _pick(in_grp, ints) * EXPERTS_PER_GROUP
    idx_ref[0:1, :] = grp + _pick(oh1, ints)
    idx_ref[1:2, :] = grp + _pick(oh2, ints)

    one, zero = jnp.ones_like(w1), jnp.zeros_like(w1)
    for g in range(N_GROUPS):
        for k in range(EXPERTS_PER_GROUP):
            e = g * EXPERTS_PER_GROUP + k
            ind_scr[e:e + 1, :] = jnp.where(in_grp[g] & (oh1[k] | oh2[k]), one, zero)
    ind = ind_scr[...].astype(BF16)
    ranks = jnp.dot(ind, tri_ref[...], preferred_element_type=F32)
    cnt_ref[0] = jnp.dot(ind, jnp.ones((ind.shape[1], LANES), BF16), preferred_element_type=F32)
    for slot, oh in ((0, oh1), (1, oh2)):
        r = zero
        for g in range(N_GROUPS):
            for k in range(EXPERTS_PER_GROUP):
                e = g * EXPERTS_PER_GROUP + k
                r = r + jnp.where(in_grp[g] & oh[k], ranks[e:e + 1], zero)
        rank_ref[slot:slot + 1, :] = r.astype(I32)


def route(lay, x, g, mods, w_router_t, router_bias):
    t, d = x.shape
    tm = lay.tm
    tri = jnp.triu(jnp.ones((tm, tm), BF16), k=1)
    pair = jax.ShapeDtypeStruct((TOP_K, t), I32)
    pair_spec = pl.BlockSpec((TOP_K, tm), lambda i: (0, i))
    return pl.pallas_call(
        _router_kernel,
        out_shape=[jax.ShapeDtypeStruct((t, d), F32), pair, jax.ShapeDtypeStruct((t, LANES), F32), pair,
                   jax.ShapeDtypeStruct((lay.n_tiles, N_EXPERTS, LANES), F32)],
        grid=(lay.n_tiles,),
        in_specs=[
            pl.BlockSpec((tm, d), lambda i: (i, 0)),
            pl.BlockSpec((1, d), lambda i: (0, 0)),
            pl.BlockSpec((1, N_MOD, d), lambda i: (lay.mod_row(i), 0, 0)),
            pl.BlockSpec((N_EXPERTS, d), lambda i: (0, 0)),
            pl.BlockSpec((N_EXPERTS, 1), lambda i: (0, 0)),
            pl.BlockSpec((tm, tm), lambda i: (0, 0)),
        ],
        out_specs=[pl.BlockSpec((tm, d), lambda i: (i, 0)), pair_spec,
                   pl.BlockSpec((tm, LANES), lambda i: (i, 0)), pair_spec,
                   pl.BlockSpec((1, N_EXPERTS, LANES), lambda i: (i, 0, 0))],
        scratch_shapes=[pltpu.VMEM((N_EXPERTS, tm), F32), pltpu.VMEM((LANES, tm), F32)],
        compiler_params=_cparams("parallel"),
        name="router",
    )(x, g.reshape(1, d), mods, w_router_t, router_bias.reshape(N_EXPERTS, 1), tri)


def dispatch_plan(lay, idx, rank, cnt, n_sorted):
    tm = lay.tm
    cnt_tile = cnt[:, :, 0].astype(I32)
    total = jnp.sum(cnt_tile, axis=0)
    padded = ((total + EXPERT_TILE - 1) // EXPERT_TILE) * EXPERT_TILE
    end = jnp.cumsum(padded)
    start = end - padded
    base = start[None, :] + jnp.cumsum(cnt_tile, axis=0) - cnt_tile
    idx3 = idx.reshape(TOP_K, lay.n_tiles, tm)
    pos = rank.reshape(TOP_K, lay.n_tiles, tm)
    for e in range(N_EXPERTS):
        pos = pos + jnp.where(idx3 == e, base[None, :, e, None], 0)
    tile_row = jnp.arange(n_sorted // EXPERT_TILE, dtype=I32) * EXPERT_TILE
    tile_expert = jnp.minimum(jnp.sum(end[None, :] <= tile_row[:, None], axis=1), N_EXPERTS - 1).astype(I32)
    n_valid = (end[-1] // EXPERT_TILE).astype(I32).reshape(1)
    return pos.reshape(TOP_K * lay.t).astype(I32), tile_expert, n_valid


def _sc_mesh():
    return plsc.VectorSubcoreMesh(core_axis_name="c", subcore_axis_name="s")


def _sc_worker_id():
    return lax.axis_index("s") * V7X_SC_CORES + lax.axis_index("c")


def sc_scatter_rows(src, pos, n_out_rows):
    t, d = src.shape
    n_idx = pos.shape[0]
    per_worker = n_idx // SC_WORKERS
    chunk = SC_CHUNK_ROWS
    assert n_idx % (SC_WORKERS * chunk) == 0 and t % chunk == 0

    @functools.partial(
        pl.kernel, mesh=_sc_mesh(), out_type=jax.ShapeDtypeStruct((n_out_rows, d), src.dtype),
        scratch_types=[pltpu.VMEM((chunk,), I32), pltpu.VMEM((chunk, d), src.dtype)],
        name="sc_scatter_rows")
    def scatter(src_hbm, pos_hbm, out_hbm, idx_v, rows_v):
        base = _sc_worker_id() * per_worker

        @pl.loop(0, per_worker // chunk)
        def _(i):
            a0 = base + i * chunk
            pltpu.sync_copy(pos_hbm.at[pl.ds(a0, chunk)], idx_v)
            pltpu.sync_copy(src_hbm.at[pl.ds(lax.rem(a0, t), chunk)], rows_v)
            pltpu.sync_copy(rows_v, out_hbm.at[idx_v])

    return scatter(src, pos)


def sc_gather_rows(table, idx):
    _, d = table.shape
    n_idx = idx.shape[0]
    per_worker = n_idx // SC_WORKERS
    chunk = SC_CHUNK_ROWS
    assert n_idx % (SC_WORKERS * chunk) == 0

    @functools.partial(
        pl.kernel, mesh=_sc_mesh(), out_type=jax.ShapeDtypeStruct((n_idx, d), table.dtype),
        scratch_types=[pltpu.VMEM((chunk,), I32), pltpu.VMEM((chunk, d), table.dtype)],
        name="sc_gather_rows")
    def gather(table_hbm, idx_hbm, out_hbm, idx_v, rows_v):
        base = _sc_worker_id() * per_worker

        @pl.loop(0, per_worker // chunk)
        def _(i):
            a0 = base + i * chunk
            pltpu.sync_copy(idx_hbm.at[pl.ds(a0, chunk)], idx_v)
            pltpu.sync_copy(table_hbm.at[idx_v], rows_v)
            pltpu.sync_copy(rows_v, out_hbm.at[pl.ds(a0, chunk)])

    return gather(table, idx)


def _experts_kernel(te_ref, nv_ref, x_ref, wg_ref, wu_ref, wd_ref, y_ref, wg_scr, wu_scr, wd_scr):
    i = pl.program_id(0)

    @pl.when(i < nv_ref[0])
    def _():
        @pl.when((i == 0) | (te_ref[i] != te_ref[jnp.maximum(i - 1, 0)]))
        def _():
            wg_scr[...] = wg_ref[0, 0].astype(BF16)
            wu_scr[...] = wu_ref[0, 0].astype(BF16)
            wd_scr[...] = wd_ref[0, 0].astype(BF16)

        x = x_ref[...].astype(BF16)
        a = (jax.nn.silu(jnp.dot(x, wg_scr[...], preferred_element_type=F32))
             * jnp.dot(x, wu_scr[...], preferred_element_type=F32))
        y_ref[...] = jnp.dot(a.astype(BF16), wd_scr[...], preferred_element_type=F32)


def grouped_experts(xs, tile_expert, n_valid, wg, wu, wd, layer):
    n_rows, d = xs.shape
    de = wg.shape[-1]
    tm = EXPERT_TILE
    row_map = lambda i, te, nv: (jnp.minimum(i, nv[0] - 1), 0)
    grid_spec = pltpu.PrefetchScalarGridSpec(
        num_scalar_prefetch=2,
        grid=(n_rows // tm,),
        in_specs=[
            pl.BlockSpec((tm, d), row_map),
            pl.BlockSpec((1, 1, d, de), lambda i, te, nv: (layer, te[i], 0, 0)),
            pl.BlockSpec((1, 1, d, de), lambda i, te, nv: (layer, te[i], 0, 0)),
            pl.BlockSpec((1, 1, de, d), lambda i, te, nv: (layer, te[i], 0, 0)),
        ],
        out_specs=pl.BlockSpec((tm, d), row_map),
        scratch_shapes=[pltpu.VMEM((d, de), BF16), pltpu.VMEM((d, de), BF16), pltpu.VMEM((de, d), BF16)],
    )
    return pl.pallas_call(
        _experts_kernel,
        out_shape=jax.ShapeDtypeStruct((n_rows, d), F32),
        grid_spec=grid_spec,
        compiler_params=_cparams("arbitrary"),
        name="grouped_experts",
    )(tile_expert, n_valid, xs, wg, wu, wd)


def _combine_kernel(x_ref, y_ref, w_ref, mod_ref, fg_ref, *o_refs, final_norm, n_ctx_tiles):
    w = w_ref[...]
    y = w[:, 0:1] * y_ref[0] + w[:, 1:2] * y_ref[1]
    out = x_ref[...] + mod_ref[0, 5:6, :] * y
    if final_norm:
        ms = jnp.mean(out * out, axis=-1, keepdims=True)
        out = out * lax.rsqrt(ms + NORM_EPS) * fg_ref[...]

    def store(o_ref):
        o_ref[...] = out
    _by_part(pl.program_id(0) >= n_ctx_tiles, o_refs, store)


def combine_residual(lay, x, y_pair, w_col, mods, final_g, final_norm, split):
    t, d = x.shape
    tm = lay.tm
    if split:
        out_shape = [jax.ShapeDtypeStruct((lay.t_ctx, d), F32), jax.ShapeDtypeStruct((lay.t_dec, d), F32)]
        out_specs = lay.row_specs([None, None], d)
    else:
        out_shape = [jax.ShapeDtypeStruct((t, d), F32)]
        out_specs = lay.row_specs([None], d)
    return pl.pallas_call(
        functools.partial(_combine_kernel, final_norm=final_norm, n_ctx_tiles=lay.n_ctx_tiles),
        out_shape=out_shape,
        grid=(lay.n_tiles,),
        in_specs=[
            pl.BlockSpec((tm, d), lambda i: (i, 0)),
            pl.BlockSpec((TOP_K, tm, d), lambda i: (0, i, 0)),
            pl.BlockSpec((tm, LANES), lambda i: (i, 0)),
            pl.BlockSpec((1, N_MOD, d), lambda i: (lay.mod_row(i), 0, 0)),
            pl.BlockSpec((1, d), lambda i: (0, 0)),
        ],
        out_specs=out_specs,
        compiler_params=_cparams("arbitrary"),
        name="combine_residual",
    )(x, y_pair, w_col, mods, final_g.reshape(1, d))


def group_moe_residual(lay, x, mods, p, layer, last):
    t, d = x.shape
    h, idx, w_col, rank, cnt = route(lay, x, p["norm_ffn_g"][layer], mods, p["w_router_t"], p["router_bias"])
    n_assign = TOP_K * t
    row_quantum = SC_WORKERS * SC_CHUNK_ROWS
    n_sorted = n_assign + N_EXPERTS * (EXPERT_TILE - 1)
    n_sorted = -(-n_sorted // row_quantum) * row_quantum
    n_sorted = -(-n_sorted // EXPERT_TILE) * EXPERT_TILE
    pos, tile_expert, n_valid = dispatch_plan(lay, idx, rank, cnt, n_sorted)
    xs = sc_scatter_rows(h, pos, n_sorted)
    ys = grouped_experts(xs, tile_expert, n_valid, p["moe_w_gate"], p["moe_w_up"], p["moe_w_down"], layer)
    y_pair = sc_gather_rows(ys, pos).reshape(TOP_K, t, d)
    return combine_residual(lay, x, y_pair, w_col, mods, p["final_norm_g"], final_norm=last, split=last)


def _rope_angles(n, d):
    n_rows = n // GRID_W
    row = jnp.repeat(jnp.arange(n_rows), GRID_W).astype(F32)
    col = jnp.tile(jnp.arange(GRID_W), n_rows).astype(F32)
    nf = d // 4
    freqs = jnp.power(ROPE_BASE, -jnp.arange(nf, dtype=F32) / nf)
    ang = jnp.concatenate([row[:, None] * freqs, col[:, None] * freqs], axis=-1)
    return jnp.cos(ang), jnp.sin(ang)


def kernel(x_prompt, x_sample, cache_attn_k, cache_attn_v, state_ret_fwd, state_ret_bwd, c, c_ctx, w_ada, b_ada, norm_mix_g, norm_ffn_g, final_norm_g, da_w_qkv, da_lambda_q1, da_lambda_k1, da_lambda_q2, da_lambda_k2, da_subln_g, da_w_o, ret_w_qkv, ret_w_gate_fwd, ret_w_gate_bwd, ret_decay_fwd, ret_decay_bwd, ret_w_o, w_router, router_bias, moe_w_gate, moe_w_up, moe_w_down):
    b_ctx, n_ctx, d = x_prompt.shape
    b_dec, n_dec, _ = x_sample.shape
    past = cache_attn_k.shape[2]
    n_attn = cache_attn_k.shape[1]
    depth = w_ada.shape[0]
    assert b_dec + 1 <= MOD_ROWS
    lay = Layout(b_ctx, n_ctx, b_dec, n_dec)

    cond = jnp.zeros((MOD_ROWS, d), F32).at[0].set(c_ctx).at[1:1 + b_dec].set(c)
    mods_all = ada_modulation(cond, w_ada, b_ada)

    p = {
        "norm_ffn_g": norm_ffn_g, "final_norm_g": final_norm_g,
        "w_router_t": w_router.T.astype(BF16), "router_bias": router_bias.astype(F32),
        "moe_w_gate": moe_w_gate, "moe_w_up": moe_w_up, "moe_w_down": moe_w_down,
    }
    ret_log_decay = jnp.stack([jax.nn.log_sigmoid(ret_decay_fwd.astype(F32)),
                               jax.nn.log_sigmoid(ret_decay_bwd.astype(F32))], axis=1)
    ck_all = cache_attn_k.reshape(b_dec, n_attn, past, -1)
    cv_all = cache_attn_v.reshape(b_dec, n_attn, past, -1)

    x_parts = [x_prompt.reshape(lay.t_ctx, d), x_sample.reshape(lay.t_dec, d)]
    new_k, new_v, new_sf, new_sb = [], [], [], []
    for i in range(depth):
        mods = mods_all[i]
        j = i // 2
        if i % 2 == 0:
            lam_init = 0.8 - 0.6 * math.exp(-0.3 * i)
            qkw = DA_HEADS * 2 * DA_HEAD_DIM
            vw = DA_HEADS * DA_V_DIM
            cos, sin = _rope_angles(n_dec, DA_HEAD_DIM)
            rope = (jnp.tile(cos, (1, 4)), jnp.concatenate([-sin, sin, -sin, sin], axis=-1))
            q, k, v, k_ctx, v_ctx = norm_mod_matmul(
                lay, x_parts, norm_mix_g[i], mods, da_w_qkv[j].astype(BF16),
                [(0, qkw, "rope64", math.log2(math.e) * DA_HEAD_DIM ** -0.5, BF16, "all"),
                 (qkw, qkw, "rope64", 1.0, BF16, "all"),
                 (2 * qkw, vw, "plain", 1.0, BF16, "all"),
                 (qkw, qkw, "plain", 1.0, F32, "ctx"), (2 * qkw, vw, "plain", 1.0, F32, "ctx")],
                rope)
            lam_params = jnp.stack([da_lambda_q1[j], da_lambda_k1[j], da_lambda_q2[j], da_lambda_k2[j]])
            mix = [diff_attention(q, k, v, 0, b_ctx, n_ctx, lam_params, da_subln_g[j], lam_init,
                                  heads_per_step=DA_HEADS),
                   diff_attention(q, k, v, lay.t_ctx, b_dec, n_dec, lam_params, da_subln_g[j], lam_init,
                                  cache=(ck_all[:, j], cv_all[:, j]))]
            x = proj_residual(lay, x_parts, mix, da_w_o[j].astype(BF16), mods)
            new_k.append(k_ctx.reshape(b_ctx, n_ctx, DA_HEADS, 2, DA_HEAD_DIM))
            new_v.append(v_ctx.reshape(b_ctx, n_ctx, DA_HEADS, DA_V_DIM))
        else:
            kd = ret_w_qkv.shape[2] // 4
            dv = 2 * kd
            w_all = jnp.concatenate([ret_w_qkv[j], ret_w_gate_fwd[j], ret_w_gate_bwd[j]], axis=-1).astype(BF16)
            q, k, v, gates = norm_mod_matmul(
                lay, x_parts, norm_mix_g[i], mods, w_all,
                [(0, kd, "rope256", 1.0, BF16, "all"),
                 (kd, kd, "rope256", (kd // RET_HEADS) ** -0.5, F32, "all"),
                 (2 * kd, dv, "plain", 1.0, BF16, "all"), (2 * kd + dv, 2 * dv, "silu", 1.0, F32, "all")],
                _rope_angles(n_dec, kd // RET_HEADS))
            o_ctx, sf, sb = retention(q, k, v, gates, 0, b_ctx, n_ctx, ret_log_decay[j], emit_state=True)
            (o_dec,) = retention(q, k, v, gates, lay.t_ctx, b_dec, n_dec, ret_log_decay[j],
                                 state=(state_ret_fwd[:, j:j + 1], state_ret_bwd[:, j:j + 1]))
            x = proj_residual(lay, x_parts, [o_ctx, o_dec], ret_w_o[j].astype(BF16), mods)
            new_sf.append(sf)
            new_sb.append(sb)
        x_parts = group_moe_residual(lay, x, mods, p, i, last=(i == depth - 1))

    y_ctx, y_dec = x_parts
    return (y_ctx.reshape(b_ctx, n_ctx, d), y_dec.reshape(b_dec, n_dec, d),
            jnp.stack(new_k, axis=1), jnp.stack(new_v, axis=1),
            jnp.concatenate(new_sf, axis=1), jnp.concatenate(new_sb, axis=1))
```

```python
import functools
import math

import jax
import jax.numpy as jnp
from jax import lax
from jax.experimental import pallas as pl
from jax.experimental.pallas import tpu as pltpu
from jax.experimental.pallas import tpu_sc as plsc

F32 = jnp.float32
BF16 = jnp.bfloat16
I32 = jnp.int32

GRID_W = 64
ROPE_BASE = 10000.0
NORM_EPS = 1e-6
DA_HEADS = 8
DA_HEAD_DIM = 64
DA_V_DIM = 2 * DA_HEAD_DIM
RET_HEADS = 4
RET_CHUNK = 128
N_EXPERTS = 16
N_GROUPS = 4
EXPERTS_PER_GROUP = N_EXPERTS // N_GROUPS
TOP_K = 2
N_MOD = 6
MOD_ROWS = 16
LANES = 128

V7X_VMEM_LIMIT = 56 * 1024 * 1024
V7X_SC_CORES = 2
V7X_SC_SUBCORES = 16
SC_WORKERS = V7X_SC_CORES * V7X_SC_SUBCORES
SC_CHUNK_ROWS = 32

ROW_TILE = 512
EXPERT_TILE = 512


def _cparams(*sem):
    return pltpu.CompilerParams(dimension_semantics=sem, vmem_limit_bytes=V7X_VMEM_LIMIT)


def _nt_dot(a, b):
    return lax.dot_general(a, b, (((1,), (1,)), ((), ())), preferred_element_type=F32)


def _tn_dot(a, b):
    return lax.dot_general(a, b, (((0,), (0,)), ((), ())), preferred_element_type=F32)


class Layout:
    def __init__(self, b_ctx, n_ctx, b_dec, n_dec):
        self.b_ctx, self.n_ctx, self.b_dec, self.n_dec = b_ctx, n_ctx, b_dec, n_dec
        self.t_ctx, self.t_dec = b_ctx * n_ctx, b_dec * n_dec
        self.t = self.t_ctx + self.t_dec
        self.tm = min(ROW_TILE, n_dec, self.t_ctx)
        assert self.t_ctx % self.tm == 0 and n_dec % self.tm == 0
        assert self.t_ctx % n_dec == 0 and self.t_ctx % n_ctx == 0
        self.n_ctx_tiles = self.t_ctx // self.tm
        self.n_tiles = self.t // self.tm

    def mod_row(self, i):
        r = i * self.tm
        return jnp.where(r < self.t_ctx, 0, 1 + (r - self.t_ctx) // self.n_dec)

    def part_tile(self, part, i):
        if part == 0:
            return jnp.minimum(i, self.n_ctx_tiles - 1)
        return jnp.maximum(i - self.n_ctx_tiles, 0)

    def row_specs(self, arrays, width, extra_grid_dims=0):
        pad = (0,) * 0
        if len(arrays) == 1:
            return [pl.BlockSpec((self.tm, width), lambda i, *_: (i, 0))]
        return [pl.BlockSpec((self.tm, width), lambda i, *_, p=p: (self.part_tile(p, i), 0)) for p in (0, 1)]


def _ada_kernel(c_ref, w_ref, b_ref, o_ref):
    s = jax.nn.silu(c_ref[...]).astype(BF16)
    acc = jnp.dot(s, w_ref[0].astype(BF16), preferred_element_type=F32)
    o_ref[0] = acc + b_ref[0]


def ada_modulation(cond, w_ada, b_ada):
    depth, d, n = w_ada.shape
    tn = 1536
    out = pl.pallas_call(
        _ada_kernel,
        out_shape=jax.ShapeDtypeStruct((depth, MOD_ROWS, n), F32),
        grid=(depth, n // tn),
        in_specs=[
            pl.BlockSpec((MOD_ROWS, d), lambda l, j: (0, 0)),
            pl.BlockSpec((1, d, tn), lambda l, j: (l, 0, j)),
            pl.BlockSpec((1, 1, tn), lambda l, j: (l, 0, j)),
        ],
        out_specs=pl.BlockSpec((1, MOD_ROWS, tn), lambda l, j: (l, 0, j)),
        compiler_params=_cparams("parallel", "parallel"),
        name="ada_modulation",
    )(cond, w_ada, b_ada.reshape(depth, 1, n))
    return out.reshape(depth, MOD_ROWS, N_MOD, d)


def _modulated_norm(x, g, mod_ref, shift_idx):
    ms = jnp.mean(x * x, axis=-1, keepdims=True)
    y = x * lax.rsqrt(ms + NORM_EPS) * g
    return y * (1.0 + mod_ref[0, shift_idx + 1:shift_idx + 2, :]) + mod_ref[0, shift_idx:shift_idx + 1, :]


def _by_part(is_dec, refs, fn):
    if len(refs) == 1:
        fn(refs[0])
        return
    pl.when(jnp.logical_not(is_dec))(lambda: fn(refs[0]))
    pl.when(is_dec)(lambda: fn(refs[1]))


def _rope64(a, cos, sin_signed, first_half):
    partner = jnp.where(first_half, pltpu.roll(a, 96, 1), pltpu.roll(a, 32, 1))
    return a * cos + partner * sin_signed


def _nmm_kernel(*refs, n_x, outs, tn, n_ctx_tiles):
    x_refs = refs[:n_x]
    g_ref, mod_ref, w_ref, cos_ref, sin_ref = refs[n_x:n_x + 5]
    out_refs = refs[n_x + 5:]
    is_dec = pl.program_id(0) >= n_ctx_tiles
    tm = x_refs[0].shape[0]
    segments = sorted({(col0, width) for col0, width, _, _, _ in outs})

    def rotated(a, kind):
        cos, sin = cos_ref[...], sin_ref[...]
        if kind == "rope64":
            lane = lax.broadcasted_iota(I32, (tm, LANES), 1)
            first_half = (lane & 32) == 0
            return [(c * 128, _rope64(a[:, c * 128:(c + 1) * 128], cos, sin, first_half))
                    for c in range(tn // 128)]
        pieces = []
        for c in range(tn // 256):
            x1, x2 = a[:, c * 256:c * 256 + 128], a[:, c * 256 + 128:(c + 1) * 256]
            pieces += [(c * 256, x1 * cos - x2 * sin), (c * 256 + 128, x2 * cos + x1 * sin)]
        return pieces

    def emit(dec):
        x_ref = x_refs[(1 if dec else 0) if n_x == 2 else 0]
        h = _modulated_norm(x_ref[...], g_ref[...], mod_ref, 0).astype(BF16)
        for col0, width in segments:
            sinks = [(o_ref, o) for o_ref, o in zip(out_refs, outs)
                     if (o[0], o[1]) == (col0, width) and not (dec and o[4] == "ctx")]
            for blk in range(width // tn):
                acc = jnp.dot(h, w_ref[:, col0 + blk * tn:col0 + (blk + 1) * tn], preferred_element_type=F32)
                for o_ref, (_, _, kind, scale, rows) in sinks:
                    a = acc if scale == 1.0 else acc * scale
                    if kind == "silu":
                        o_ref[:, blk * tn:(blk + 1) * tn] = jax.nn.silu(a).astype(o_ref.dtype)
                    elif kind == "plain" or rows == "ctx" or not dec:
                        o_ref[:, blk * tn:(blk + 1) * tn] = a.astype(o_ref.dtype)
                    else:
                        for off, val in rotated(a, kind):
                            o_ref[:, blk * tn + off:blk * tn + off + 128] = val.astype(o_ref.dtype)

    pl.when(jnp.logical_not(is_dec))(lambda: emit(False))
    pl.when(is_dec)(lambda: emit(True))


def norm_mod_matmul(lay, x_parts, g, mods, w, outputs, rope_tables, tn=512):
    d = x_parts[0].shape[1]
    tm = lay.tm
    n_total = w.shape[1]
    outs, out_shapes, out_specs = [], [], []
    for col0, width, kind, scale, dtype, rows in outputs:
        assert width % tn == 0 and col0 % LANES == 0
        outs.append((col0, width, kind, float(scale), rows))
        n_rows = lay.t_ctx if rows == "ctx" else lay.t
        out_shapes.append(jax.ShapeDtypeStruct((n_rows, width), dtype))
        out_specs.append(lay.row_specs([None, None], width)[0] if rows == "ctx" else lay.row_specs([None], width)[0])
    blocks_per_seq = lay.n_dec // tm
    rope_spec = pl.BlockSpec((tm, LANES), lambda i: (lay.part_tile(1, i) % blocks_per_seq, 0))
    return pl.pallas_call(
        functools.partial(_nmm_kernel, n_x=len(x_parts), outs=tuple(outs), tn=tn, n_ctx_tiles=lay.n_ctx_tiles),
        out_shape=out_shapes,
        grid=(lay.n_tiles,),
        in_specs=lay.row_specs(x_parts, d) + [
            pl.BlockSpec((1, d), lambda i: (0, 0)),
            pl.BlockSpec((1, N_MOD, d), lambda i: (lay.mod_row(i), 0, 0)),
            pl.BlockSpec((d, n_total), lambda i: (0, 0), pipeline_mode=pl.Buffered(1)),
            rope_spec, rope_spec,
        ],
        out_specs=out_specs,
        compiler_params=_cparams("arbitrary"),
        name="norm_mod_matmul",
    )(*x_parts, g.reshape(1, d), mods, w, *rope_tables)


def _diff_attn_kernel(*refs, lam_init, has_cache, heads_per_step):
    lam_ref, q_ref, k_ref, v_ref = refs[:4]
    pos = 4
    if has_cache:
        ck_ref, cv_ref = refs[4:6]
        pos = 6
    g_ref, o_ref = refs[pos], refs[pos + 1]
    hw = 2 * DA_HEAD_DIM

    lp = lam_ref[...]
    lam = (jnp.exp(jnp.sum(lp[0:1] * lp[1:2], axis=-1, keepdims=True))
           - jnp.exp(jnp.sum(lp[2:3] * lp[3:4], axis=-1, keepdims=True)) + lam_init)
    lane = lax.broadcasted_iota(I32, (q_ref.shape[0], hw), 1)

    for hh in range(heads_per_step):
        cols = slice(hh * hw, (hh + 1) * hw)
        q = q_ref[:, cols]
        kn = k_ref[:, cols]
        vn = v_ref[:, cols]
        if has_cache:
            kc = ck_ref[0].astype(BF16)
            vc = cv_ref[0].astype(BF16)
        exps = []
        for comp in range(2):
            in_comp = (lane < DA_HEAD_DIM) if comp == 0 else (lane >= DA_HEAD_DIM)
            qc = jnp.where(in_comp, q, jnp.zeros_like(q))
            sn = _nt_dot(qc, kn)
            m = jnp.max(sn, axis=-1, keepdims=True)
            if has_cache:
                sc = _nt_dot(qc, kc)
                m = jnp.maximum(m, jnp.max(sc, axis=-1, keepdims=True))
            en = jnp.exp2(sn - m)
            denom = jnp.sum(en, axis=-1, keepdims=True)
            ec = None
            if has_cache:
                ec = jnp.exp2(sc - m)
                denom = denom + jnp.sum(ec, axis=-1, keepdims=True)
            exps.append((en, ec, 1.0 / denom))
        c0, c1 = exps[0][2], lam * exps[1][2]
        o = jnp.dot((exps[0][0] * c0 - exps[1][0] * c1).astype(BF16), vn, preferred_element_type=F32)
        if has_cache:
            o = o + jnp.dot((exps[0][1] * c0 - exps[1][1] * c1).astype(BF16), vc, preferred_element_type=F32)
        ms = jnp.mean(o * o, axis=-1, keepdims=True)
        o_ref[:, cols] = (o * lax.rsqrt(ms + NORM_EPS) * g_ref[...]) * (1.0 - lam_init)


def diff_attention(q, k, v, row0, batch, seq_len, lam_params, subln_g, lam_init, cache=None, tq=256,
                   heads_per_step=1):
    width = q.shape[1]
    tq = min(tq, seq_len)
    nq = seq_len // tq
    hw = 2 * DA_HEAD_DIM
    bw = heads_per_step * hw
    assert row0 % seq_len == 0 and DA_HEADS % heads_per_step == 0
    q0, s0 = row0 // tq, row0 // seq_len
    in_specs = [
        pl.BlockSpec((4, DA_HEAD_DIM), lambda b, h, i: (0, 0)),
        pl.BlockSpec((tq, bw), lambda b, h, i: (q0 + b * nq + i, h)),
        pl.BlockSpec((seq_len, bw), lambda b, h, i: (s0 + b, h)),
        pl.BlockSpec((seq_len, bw), lambda b, h, i: (s0 + b, h)),
    ]
    args = [lam_params, q, k, v]
    if cache is not None:
        assert heads_per_step == 1
        ck, cv = cache
        past = ck.shape[1]
        in_specs += [pl.BlockSpec((1, past, hw), lambda b, h, i: (b, 0, h)),
                     pl.BlockSpec((1, past, hw), lambda b, h, i: (b, 0, h))]
        args += [ck, cv]
    in_specs.append(pl.BlockSpec((1, hw), lambda b, h, i: (0, 0)))
    args.append(subln_g.reshape(1, hw))
    return pl.pallas_call(
        functools.partial(_diff_attn_kernel, lam_init=lam_init, has_cache=cache is not None,
                          heads_per_step=heads_per_step),
        out_shape=jax.ShapeDtypeStruct((batch * seq_len, width), F32),
        grid=(batch, DA_HEADS // heads_per_step, nq),
        in_specs=in_specs,
        out_specs=pl.BlockSpec((tq, bw), lambda b, h, i: (b * nq + i, h)),
        compiler_params=_cparams("parallel", "parallel", "arbitrary"),
        name="diff_attention",
    )(*args)


def _retention_kernel(*refs, seq_len, has_state, emit_state):
    lg_ref, q_ref, k_ref, v_ref, gf_ref, gb_ref = refs[:6]
    pos = 6
    if has_state:
        s0_refs = refs[6:8]
        pos = 8
    o_ref = refs[pos]
    pos += 1
    if emit_state:
        s_out_refs = refs[pos:pos + 2]
        pos += 2
    s_scrs = refs[pos:pos + 2]

    h = pl.program_id(1)
    c_len = RET_CHUNK
    n_chunks = seq_len // c_len
    assert n_chunks % 2 == 0
    row = lax.broadcasted_iota(I32, (c_len, c_len), 0)
    colm = lax.broadcasted_iota(I32, (c_len, c_len), 1)
    rel = (row - colm).astype(F32)
    posv = lax.broadcasted_iota(I32, (c_len, 1), 0).astype(F32)

    consts = []
    for backward in (False, True):
        lg = lg_ref[1 if backward else 0, h]
        if backward:
            intra = jnp.where(rel <= 0, jnp.exp(-rel * lg), 0.0)
            q_decay = jnp.exp((c_len - posv) * lg)
            k_decay = jnp.exp(posv * lg)
        else:
            intra = jnp.where(rel >= 0, jnp.exp(rel * lg), 0.0)
            q_decay = jnp.exp((posv + 1.0) * lg)
            k_decay = jnp.exp((c_len - 1.0 - posv) * lg)
        consts.append((intra, q_decay, k_decay, jnp.exp(jnp.zeros((1, 1), F32) + c_len * lg)))
        s_scr = s_scrs[1 if backward else 0]
        if has_state:
            s_scr[...] = s0_refs[1 if backward else 0][0, 0, 0]
        else:
            s_scr[...] = jnp.zeros_like(s_scr)

    def chunk(c, backward, first_touch):
        intra, q_decay, k_decay, chunk_decay = consts[1 if backward else 0]
        s_scr = s_scrs[1 if backward else 0]
        gate_ref = gb_ref if backward else gf_ref
        rows = pl.ds(pl.multiple_of(c * c_len, c_len), c_len)
        qb = q_ref[rows, :].astype(BF16)
        kf = k_ref[rows, :].astype(F32)
        vb = v_ref[rows, :].astype(BF16)
        a = _nt_dot(qb, kf.astype(BF16)) * intra
        s = s_scr[...]
        o = (jnp.dot(a.astype(BF16), vb, preferred_element_type=F32)
             + jnp.dot(qb, s.astype(BF16), preferred_element_type=F32) * q_decay)
        s_scr[...] = chunk_decay * s + _tn_dot((kf * k_decay).astype(BF16), vb)
        mu = jnp.mean(o, axis=-1, keepdims=True)
        oc = o - mu
        var = jnp.mean(oc * oc, axis=-1, keepdims=True)
        gated = oc * lax.rsqrt(var + NORM_EPS) * gate_ref[rows, :].astype(F32)
        if first_touch:
            o_ref[rows, :] = gated
        else:
            o_ref[rows, :] = o_ref[rows, :] + gated

    def body(ci, first_touch):
        chunk(ci, False, first_touch)
        chunk(n_chunks - 1 - ci, True, first_touch)

    unroll = 4 if n_chunks % 8 == 0 else 1
    lax.fori_loop(0, n_chunks // 2, lambda ci, c: (body(ci, True), c)[1], 0, unroll=unroll)
    lax.fori_loop(n_chunks // 2, n_chunks, lambda ci, c: (body(ci, False), c)[1], 0, unroll=unroll)
    if emit_state:
        for d in range(2):
            s_out_refs[d][0, 0, 0] = s_scrs[d][...]


def retention(q, k, v, gates, row0, batch, seq_len, log_decay, state=None, emit_state=False):
    dk = q.shape[1] // RET_HEADS
    dv = v.shape[1] // RET_HEADS
    assert row0 % seq_len == 0
    s0 = row0 // seq_len
    in_specs = [
        pl.BlockSpec(memory_space=pltpu.SMEM),
        pl.BlockSpec((seq_len, dk), lambda b, h: (s0 + b, h)),
        pl.BlockSpec((seq_len, dk), lambda b, h: (s0 + b, h)),
        pl.BlockSpec((seq_len, dv), lambda b, h: (s0 + b, h)),
        pl.BlockSpec((seq_len, dv), lambda b, h: (s0 + b, h)),
        pl.BlockSpec((seq_len, dv), lambda b, h: (s0 + b, RET_HEADS + h)),
    ]
    args = [log_decay, q, k, v, gates, gates]
    state_spec = pl.BlockSpec((1, 1, 1, dk, dv), lambda b, h: (b, 0, h, 0, 0))
    if state is not None:
        in_specs += [state_spec, state_spec]
        args += list(state)
    out_shapes = [jax.ShapeDtypeStruct((batch * seq_len, RET_HEADS * dv), F32)]
    out_specs = [pl.BlockSpec((seq_len, dv), lambda b, h: (b, h))]
    if emit_state:
        out_shapes += [jax.ShapeDtypeStruct((batch, 1, RET_HEADS, dk, dv), F32)] * 2
        out_specs += [state_spec, state_spec]
    return pl.pallas_call(
        functools.partial(_retention_kernel, seq_len=seq_len, has_state=state is not None,
                          emit_state=emit_state),
        out_shape=out_shapes,
        grid=(batch, RET_HEADS),
        in_specs=in_specs,
        out_specs=out_specs,
        scratch_shapes=[pltpu.VMEM((dk, dv), F32), pltpu.VMEM((dk, dv), F32)],
        compiler_params=_cparams("parallel", "parallel"),
        name="retention",
    )(*args)


def _proj_residual_kernel(*refs, n_x, n_a, n_ctx_tiles):
    x_refs, a_refs = refs[:n_x], refs[n_x:n_x + n_a]
    w_ref, mod_ref, o_ref, y_scr = refs[n_x + n_a:]
    is_dec = pl.program_id(0) >= n_ctx_tiles

    def project(a_ref):
        y_scr[...] = mod_ref[0, 2:3, :] * jnp.dot(a_ref[...].astype(BF16), w_ref[...],
                                                  preferred_element_type=F32)

    def add(x_ref):
        o_ref[...] = x_ref[...] + y_scr[...]

    _by_part(is_dec, a_refs, project)
    _by_part(is_dec, x_refs, add)


def proj_residual(lay, x_parts, a_parts, w, mods):
    d = x_parts[0].shape[1]
    kdim = a_parts[0].shape[1]
    tm = lay.tm
    return pl.pallas_call(
        functools.partial(_proj_residual_kernel, n_x=len(x_parts), n_a=len(a_parts),
                          n_ctx_tiles=lay.n_ctx_tiles),
        out_shape=jax.ShapeDtypeStruct((lay.t, d), F32),
        grid=(lay.n_tiles,),
        in_specs=lay.row_specs(x_parts, d) + lay.row_specs(a_parts, kdim) + [
            pl.BlockSpec((kdim, d), lambda i: (0, 0)),
            pl.BlockSpec((1, N_MOD, d), lambda i: (lay.mod_row(i), 0, 0)),
        ],
        out_specs=pl.BlockSpec((tm, d), lambda i: (i, 0)),
        scratch_shapes=[pltpu.VMEM((tm, d), F32)],
        compiler_params=_cparams("arbitrary"),
        name="proj_residual",
    )(*x_parts, *a_parts, w, mods)


def _first_max_onehot(vals):
    m = vals[0]
    for v in vals[1:]:
        m = jnp.maximum(m, v)
    onehot, taken = [], None
    for v in vals:
        hit = v == m
        if taken is None:
            onehot.append(hit)
            taken = hit
        else:
            onehot.append(hit & jnp.logical_not(taken))
            taken = taken | hit
    return m, onehot


def _pick(onehot, vals):
    out = vals[-1]
    for oh, v in zip(onehot[-2::-1], vals[-2::-1]):
        out = jnp.where(oh, v, out)
    return out


def _router_kernel(x_ref, g_ref, mod_ref, wr_ref, bias_ref, tri_ref, h_ref, idx_ref, wcol_ref, rank_ref,
                   cnt_ref, ind_scr, wrow_scr):
    h = _modulated_norm(x_ref[...], g_ref[...], mod_ref, 3)
    h_ref[...] = h
    logits = _nt_dot(wr_ref[...], h.astype(BF16))
    s = jax.nn.sigmoid(logits)
    sel = s + bias_ref[...]
    neg_inf = jnp.full_like(sel[0:1], -jnp.inf)
    sel_rows = [sel[e:e + 1] for e in range(N_EXPERTS)]
    s_rows = [s[e:e + 1] for e in range(N_EXPERTS)]

    def top2(vals):
        m1, oh1 = _first_max_onehot(vals)
        rest = [jnp.where(o, neg_inf, v) for o, v in zip(oh1, vals)]
        m2, oh2 = _first_max_onehot(rest)
        return m1, m2, oh1, oh2

    grp_scores = []
    for g in range(N_GROUPS):
        m1, m2, _, _ = top2(sel_rows[g * EXPERTS_PER_GROUP:(g + 1) * EXPERTS_PER_GROUP])
        grp_scores.append(m1 + m2)
    _, in_grp = _first_max_onehot(grp_scores)
    cand_sel = [_pick(in_grp, [sel_rows[g * EXPERTS_PER_GROUP + k] for g in range(N_GROUPS)])
                for k in range(EXPERTS_PER_GROUP)]
    cand_s = [_pick(in_grp, [s_rows[g * EXPERTS_PER_GROUP + k] for g in range(N_GROUPS)])
              for k in range(EXPERTS_PER_GROUP)]
    _, _, oh1, oh2 = top2(cand_sel)
    w1 = _pick(oh1, cand_s)
    w2 = _pick(oh2, cand_s)
    denom = w1 + w2
    wrow_scr[...] = jnp.zeros_like(wrow_scr)
    wrow_scr[0:1, :] = w1 / denom
    wrow_scr[1:2, :] = w2 / denom
    wcol_ref[...] = jnp.transpose(wrow_scr[...])

    ints = [jnp.full(w1.shape, k, I32) for k in range(EXPERTS_PER_GROUP)]
    grp = _pick(in_grp, ints) * EXPERTS_PER_GROUP
    idx_ref[0:1, :] = grp + _pick(oh1, ints)
    idx_ref[1:2, :] = grp + _pick(oh2, ints)

    one, zero = jnp.ones_like(w1), jnp.zeros_like(w1)
    for g in range(N_GROUPS):
        for k in range(EXPERTS_PER_GROUP):
            e = g * EXPERTS_PER_GROUP + k
            ind_scr[e:e + 1, :] = jnp.where(in_grp[g] & (oh1[k] | oh2[k]), one, zero)
    ind = ind_scr[...].astype(BF16)
    ranks = jnp.dot(ind, tri_ref[...], preferred_element_type=F32)
    cnt_ref[0] = jnp.dot(ind, jnp.ones((ind.shape[1], LANES), BF16), preferred_element_type=F32)
    for slot, oh in ((0, oh1), (1, oh2)):
        r = zero
        for g in range(N_GROUPS):
            for k in range(EXPERTS_PER_GROUP):
                e = g * EXPERTS_PER_GROUP + k
                r = r + jnp.where(in_grp[g] & oh[k], ranks[e:e + 1], zero)
        rank_ref[slot:slot + 1, :] = r.astype(I32)


def route(lay, x, g, mods, w_router_t, router_bias):
    t, d = x.shape
    tm = lay.tm
    tri = jnp.triu(jnp.ones((tm, tm), BF16), k=1)
    pair = jax.ShapeDtypeStruct((TOP_K, t), I32)
    pair_spec = pl.BlockSpec((TOP_K, tm), lambda i: (0, i))
    return pl.pallas_call(
        _router_kernel,
        out_shape=[jax.ShapeDtypeStruct((t, d), F32), pair, jax.ShapeDtypeStruct((t, LANES), F32), pair,
                   jax.ShapeDtypeStruct((lay.n_tiles, N_EXPERTS, LANES), F32)],
        grid=(lay.n_tiles,),
        in_specs=[
            pl.BlockSpec((tm, d), lambda i: (i, 0)),
            pl.BlockSpec((1, d), lambda i: (0, 0)),
            pl.BlockSpec((1, N_MOD, d), lambda i: (lay.mod_row(i), 0, 0)),
            pl.BlockSpec((N_EXPERTS, d), lambda i: (0, 0)),
            pl.BlockSpec((N_EXPERTS, 1), lambda i: (0, 0)),
            pl.BlockSpec((tm, tm), lambda i: (0, 0)),
        ],
        out_specs=[pl.BlockSpec((tm, d), lambda i: (i, 0)), pair_spec,
                   pl.BlockSpec((tm, LANES), lambda i: (i, 0)), pair_spec,
                   pl.BlockSpec((1, N_EXPERTS, LANES), lambda i: (i, 0, 0))],
        scratch_shapes=[pltpu.VMEM((N_EXPERTS, tm), F32), pltpu.VMEM((LANES, tm), F32)],
        compiler_params=_cparams("parallel"),
        name="router",
    )(x, g.reshape(1, d), mods, w_router_t, router_bias.reshape(N_EXPERTS, 1), tri)


def dispatch_plan(lay, idx, rank, cnt, n_sorted):
    tm = lay.tm
    cnt_tile = cnt[:, :, 0].astype(I32)
    total = jnp.sum(cnt_tile, axis=0)
    padded = ((total + EXPERT_TILE - 1) // EXPERT_TILE) * EXPERT_TILE
    end = jnp.cumsum(padded)
    start = end - padded
    base = start[None, :] + jnp.cumsum(cnt_tile, axis=0) - cnt_tile
    idx3 = idx.reshape(TOP_K, lay.n_tiles, tm)
    pos = rank.reshape(TOP_K, lay.n_tiles, tm)
    for e in range(N_EXPERTS):
        pos = pos + jnp.where(idx3 == e, base[None, :, e, None], 0)
    tile_row = jnp.arange(n_sorted // EXPERT_TILE, dtype=I32) * EXPERT_TILE
    tile_expert = jnp.minimum(jnp.sum(end[None, :] <= tile_row[:, None], axis=1), N_EXPERTS - 1).astype(I32)
    n_valid = (end[-1] // EXPERT_TILE).astype(I32).reshape(1)
    return pos.reshape(TOP_K * lay.t).astype(I32), tile_expert, n_valid


def _sc_mesh():
    return plsc.VectorSubcoreMesh(core_axis_name="c", subcore_axis_name="s")


def _sc_worker_id():
    return lax.axis_index("s") * V7X_SC_CORES + lax.axis_index("c")


def sc_scatter_rows(src, pos, n_out_rows):
    t, d = src.shape
    n_idx = pos.shape[0]
    per_worker = n_idx // SC_WORKERS
    chunk = SC_CHUNK_ROWS
    assert n_idx % (SC_WORKERS * chunk) == 0 and t % chunk == 0

    @functools.partial(
        pl.kernel, mesh=_sc_mesh(), out_type=jax.ShapeDtypeStruct((n_out_rows, d), src.dtype),
        scratch_types=[pltpu.VMEM((chunk,), I32), pltpu.VMEM((chunk, d), src.dtype)],
        name="sc_scatter_rows")
    def scatter(src_hbm, pos_hbm, out_hbm, idx_v, rows_v):
        base = _sc_worker_id() * per_worker

        @pl.loop(0, per_worker // chunk)
        def _(i):
            a0 = base + i * chunk
            pltpu.sync_copy(pos_hbm.at[pl.ds(a0, chunk)], idx_v)
            pltpu.sync_copy(src_hbm.at[pl.ds(lax.rem(a0, t), chunk)], rows_v)
            pltpu.sync_copy(rows_v, out_hbm.at[idx_v])

    return scatter(src, pos)


def sc_gather_rows(table, idx):
    _, d = table.shape
    n_idx = idx.shape[0]
    per_worker = n_idx // SC_WORKERS
    chunk = SC_CHUNK_ROWS
    assert n_idx % (SC_WORKERS * chunk) == 0

    @functools.partial(
        pl.kernel, mesh=_sc_mesh(), out_type=jax.ShapeDtypeStruct((n_idx, d), table.dtype),
        scratch_types=[pltpu.VMEM((chunk,), I32), pltpu.VMEM((chunk, d), table.dtype)],
        name="sc_gather_rows")
    def gather(table_hbm, idx_hbm, out_hbm, idx_v, rows_v):
        base = _sc_worker_id() * per_worker

        @pl.loop(0, per_worker // chunk)
        def _(i):
            a0 = base + i * chunk
            pltpu.sync_copy(idx_hbm.at[pl.ds(a0, chunk)], idx_v)
            pltpu.sync_copy(table_hbm.at[idx_v], rows_v)
            pltpu.sync_copy(rows_v, out_hbm.at[pl.ds(a0, chunk)])

    return gather(table, idx)


def _experts_kernel(te_ref, nv_ref, x_ref, wg_ref, wu_ref, wd_ref, y_ref, wg_scr, wu_scr, wd_scr):
    i = pl.program_id(0)

    @pl.when(i < nv_ref[0])
    def _():
        @pl.when((i == 0) | (te_ref[i] != te_ref[jnp.maximum(i - 1, 0)]))
        def _():
            wg_scr[...] = wg_ref[0, 0].astype(BF16)
            wu_scr[...] = wu_ref[0, 0].astype(BF16)
            wd_scr[...] = wd_ref[0, 0].astype(BF16)

        x = x_ref[...].astype(BF16)
        a = (jax.nn.silu(jnp.dot(x, wg_scr[...], preferred_element_type=F32))
             * jnp.dot(x, wu_scr[...], preferred_element_type=F32))
        y_ref[...] = jnp.dot(a.astype(BF16), wd_scr[...], preferred_element_type=F32)


def grouped_experts(xs, tile_expert, n_valid, wg, wu, wd, layer):
    n_rows, d = xs.shape
    de = wg.shape[-1]
    tm = EXPERT_TILE
    row_map = lambda i, te, nv: (jnp.minimum(i, nv[0] - 1), 0)
    grid_spec = pltpu.PrefetchScalarGridSpec(
        num_scalar_prefetch=2,
        grid=(n_rows // tm,),
        in_specs=[
            pl.BlockSpec((tm, d), row_map),
            pl.BlockSpec((1, 1, d, de), lambda i, te, nv: (layer, te[i], 0, 0)),
            pl.BlockSpec((1, 1, d, de), lambda i, te, nv: (layer, te[i], 0, 0)),
            pl.BlockSpec((1, 1, de, d), lambda i, te, nv: (layer, te[i], 0, 0)),
        ],
        out_specs=pl.BlockSpec((tm, d), row_map),
        scratch_shapes=[pltpu.VMEM((d, de), BF16), pltpu.VMEM((d, de), BF16), pltpu.VMEM((de, d), BF16)],
    )
    return pl.pallas_call(
        _experts_kernel,
        out_shape=jax.ShapeDtypeStruct((n_rows, d), F32),
        grid_spec=grid_spec,
        compiler_params=_cparams("arbitrary"),
        name="grouped_experts",
    )(tile_expert, n_valid, xs, wg, wu, wd)


def _combine_kernel(x_ref, y_ref, w_ref, mod_ref, fg_ref, *o_refs, final_norm, n_ctx_tiles):
    w = w_ref[...]
    y = w[:, 0:1] * y_ref[0] + w[:, 1:2] * y_ref[1]
    out = x_ref[...] + mod_ref[0, 5:6, :] * y
    if final_norm:
        ms = jnp.mean(out * out, axis=-1, keepdims=True)
        out = out * lax.rsqrt(ms + NORM_EPS) * fg_ref[...]

    def store(o_ref):
        o_ref[...] = out
    _by_part(pl.program_id(0) >= n_ctx_tiles, o_refs, store)


def combine_residual(lay, x, y_pair, w_col, mods, final_g, final_norm, split):
    t, d = x.shape
    tm = lay.tm
    if split:
        out_shape = [jax.ShapeDtypeStruct((lay.t_ctx, d), F32), jax.ShapeDtypeStruct((lay.t_dec, d), F32)]
        out_specs = lay.row_specs([None, None], d)
    else:
        out_shape = [jax.ShapeDtypeStruct((t, d), F32)]
        out_specs = lay.row_specs([None], d)
    return pl.pallas_call(
        functools.partial(_combine_kernel, final_norm=final_norm, n_ctx_tiles=lay.n_ctx_tiles),
        out_shape=out_shape,
        grid=(lay.n_tiles,),
        in_specs=[
            pl.BlockSpec((tm, d), lambda i: (i, 0)),
            pl.BlockSpec((TOP_K, tm, d), lambda i: (0, i, 0)),
            pl.BlockSpec((tm, LANES), lambda i: (i, 0)),
            pl.BlockSpec((1, N_MOD, d), lambda i: (lay.mod_row(i), 0, 0)),
            pl.BlockSpec((1, d), lambda i: (0, 0)),
        ],
        out_specs=out_specs,
        compiler_params=_cparams("arbitrary"),
        name="combine_residual",
    )(x, y_pair, w_col, mods, final_g.reshape(1, d))


def group_moe_residual(lay, x, mods, p, layer, last):
    t, d = x.shape
    h, idx, w_col, rank, cnt = route(lay, x, p["norm_ffn_g"][layer], mods, p["w_router_t"], p["router_bias"])
    n_assign = TOP_K * t
    row_quantum = SC_WORKERS * SC_CHUNK_ROWS
    n_sorted = n_assign + N_EXPERTS * (EXPERT_TILE - 1)
    n_sorted = -(-n_sorted // row_quantum) * row_quantum
    n_sorted = -(-n_sorted // EXPERT_TILE) * EXPERT_TILE
    pos, tile_expert, n_valid = dispatch_plan(lay, idx, rank, cnt, n_sorted)
    xs = sc_scatter_rows(h, pos, n_sorted)
    ys = grouped_experts(xs, tile_expert, n_valid, p["moe_w_gate"], p["moe_w_up"], p["moe_w_down"], layer)
    y_pair = sc_gather_rows(ys, pos).reshape(TOP_K, t, d)
    return combine_residual(lay, x, y_pair, w_col, mods, p["final_norm_g"], final_norm=last, split=last)


def _rope_angles(n, d):
    n_rows = n // GRID_W
    row = jnp.repeat(jnp.arange(n_rows), GRID_W).astype(F32)
    col = jnp.tile(jnp.arange(GRID_W), n_rows).astype(F32)
    nf = d // 4
    freqs = jnp.power(ROPE_BASE, -jnp.arange(nf, dtype=F32) / nf)
    ang = jnp.concatenate([row[:, None] * freqs, col[:, None] * freqs], axis=-1)
    return jnp.cos(ang), jnp.sin(ang)


def kernel(x_prompt, x_sample, cache_attn_k, cache_attn_v, state_ret_fwd, state_ret_bwd, c, c_ctx, w_ada, b_ada, norm_mix_g, norm_ffn_g, final_norm_g, da_w_qkv, da_lambda_q1, da_lambda_k1, da_lambda_q2, da_lambda_k2, da_subln_g, da_w_o, ret_w_qkv, ret_w_gate_fwd, ret_w_gate_bwd, ret_decay_fwd, ret_decay_bwd, ret_w_o, w_router, router_bias, moe_w_gate, moe_w_up, moe_w_down):
    b_ctx, n_ctx, d = x_prompt.shape
    b_dec, n_dec, _ = x_sample.shape
    past = cache_attn_k.shape[2]
    n_attn = cache_attn_k.shape[1]
    depth = w_ada.shape[0]
    assert b_dec + 1 <= MOD_ROWS
    lay = Layout(b_ctx, n_ctx, b_dec, n_dec)

    cond = jnp.zeros((MOD_ROWS, d), F32).at[0].set(c_ctx).at[1:1 + b_dec].set(c)
    mods_all = ada_modulation(cond, w_ada, b_ada)

    p = {
        "norm_ffn_g": norm_ffn_g, "final_norm_g": final_norm_g,
        "w_router_t": w_router.T.astype(BF16), "router_bias": router_bias.astype(F32),
        "moe_w_gate": moe_w_gate, "moe_w_up": moe_w_up, "moe_w_down": moe_w_down,
    }
    ret_log_decay = jnp.stack([jax.nn.log_sigmoid(ret_decay_fwd.astype(F32)),
                               jax.nn.log_sigmoid(ret_decay_bwd.astype(F32))], axis=1)
    ck_all = cache_attn_k.reshape(b_dec, n_attn, past, -1)
    cv_all = cache_attn_v.reshape(b_dec, n_attn, past, -1)

    x_parts = [x_prompt.reshape(lay.t_ctx, d), x_sample.reshape(lay.t_dec, d)]
    new_k, new_v, new_sf, new_sb = [], [], [], []
    for i in range(depth):
        mods = mods_all[i]
        j = i // 2
        if i % 2 == 0:
            lam_init = 0.8 - 0.6 * math.exp(-0.3 * i)
            qkw = DA_HEADS * 2 * DA_HEAD_DIM
            vw = DA_HEADS * DA_V_DIM
            cos, sin = _rope_angles(n_dec, DA_HEAD_DIM)
            rope = (jnp.tile(cos, (1, 4)), jnp.concatenate([-sin, sin, -sin, sin], axis=-1))
            q, k, v, k_ctx, v_ctx = norm_mod_matmul(
                lay, x_parts, norm_mix_g[i], mods, da_w_qkv[j].astype(BF16),
                [(0, qkw, "rope64", math.log2(math.e) * DA_HEAD_DIM ** -0.5, BF16, "all"),
                 (qkw, qkw, "rope64", 1.0, BF16, "all"),
                 (2 * qkw, vw, "plain", 1.0, BF16, "all"),
                 (qkw, qkw, "plain", 1.0, F32, "ctx"), (2 * qkw, vw, "plain", 1.0, F32, "ctx")],
                rope)
            lam_params = jnp.stack([da_lambda_q1[j], da_lambda_k1[j], da_lambda_q2[j], da_lambda_k2[j]])
            mix = [diff_attention(q, k, v, 0, b_ctx, n_ctx, lam_params, da_subln_g[j], lam_init,
                                  heads_per_step=DA_HEADS),
                   diff_attention(q, k, v, lay.t_ctx, b_dec, n_dec, lam_params, da_subln_g[j], lam_init,
                                  cache=(ck_all[:, j], cv_all[:, j]))]
            x = proj_residual(lay, x_parts, mix, da_w_o[j].astype(BF16), mods)
            new_k.append(k_ctx.reshape(b_ctx, n_ctx, DA_HEADS, 2, DA_HEAD_DIM))
            new_v.append(v_ctx.reshape(b_ctx, n_ctx, DA_HEADS, DA_V_DIM))
        else:
            kd = ret_w_qkv.shape[2] // 4
            dv = 2 * kd
            w_all = jnp.concatenate([ret_w_qkv[j], ret_w_gate_fwd[j], ret_w_gate_bwd[j]], axis=-1).astype(BF16)
            q, k, v, gates = norm_mod_matmul(
                lay, x_parts, norm_mix_g[i], mods, w_all,
                [(0, kd, "rope256", 1.0, BF16, "all"),
                 (kd, kd, "rope256", (kd // RET_HEADS) ** -0.5, F32, "all"),
                 (2 * kd, dv, "plain", 1.0, BF16, "all"), (2 * kd + dv, 2 * dv, "silu", 1.0, BF16, "all")],
                _rope_angles(n_dec, kd // RET_HEADS))
            o_ctx, sf, sb = retention(q, k, v, gates, 0, b_ctx, n_ctx, ret_log_decay[j], emit_state=True)
            (o_dec,) = retention(q, k, v, gates, lay.t_ctx, b_dec, n_dec, ret_log_decay[j],
                                 state=(state_ret_fwd[:, j:j + 1], state_ret_bwd[:, j:j + 1]))
            x = proj_residual(lay, x_parts, [o_ctx, o_dec], ret_w_o[j].astype(BF16), mods)
            new_sf.append(sf)
            new_sb.append(sb)
        x_parts = group_moe_residual(lay, x, mods, p, i, last=(i == depth - 1))

    y_ctx, y_dec = x_parts
    return (y_ctx.reshape(b_ctx, n_ctx, d), y_dec.reshape(b_dec, n_dec, d),
            jnp.stack(new_k, axis=1), jnp.stack(new_v, axis=1),
            jnp.concatenate(new_sf, axis=1), jnp.concatenate(new_sb, axis=1))
```

```python
import functools
import math

import jax
import jax.numpy as jnp
from jax import lax
from jax.experimental import pallas as pl
from jax.experimental.pallas import tpu as pltpu
from jax.experimental.pallas import tpu_sc as plsc

F32 = jnp.float32
BF16 = jnp.bfloat16
I32 = jnp.int32

GRID_W = 64
ROPE_BASE = 10000.0
NORM_EPS = 1e-6
DA_HEADS = 8
DA_HEAD_DIM = 64
DA_V_DIM = 2 * DA_HEAD_DIM
RET_HEADS = 4
RET_CHUNK = 128
N_EXPERTS = 16
N_GROUPS = 4
EXPERTS_PER_GROUP = N_EXPERTS // N_GROUPS
TOP_K = 2
N_MOD = 6
MOD_ROWS = 16
LANES = 128

V7X_VMEM_LIMIT = 56 * 1024 * 1024
V7X_SC_CORES = 2
V7X_SC_SUBCORES = 16
SC_WORKERS = V7X_SC_CORES * V7X_SC_SUBCORES
SC_CHUNK_ROWS = 32

ROW_TILE = 512
EXPERT_TILE = 512


def _cparams(*sem):
    return pltpu.CompilerParams(dimension_semantics=sem, vmem_limit_bytes=V7X_VMEM_LIMIT)


def _nt_dot(a, b):
    return lax.dot_general(a, b, (((1,), (1,)), ((), ())), preferred_element_type=F32)


def _tn_dot(a, b):
    return lax.dot_general(a, b, (((0,), (0,)), ((), ())), preferred_element_type=F32)


class Layout:
    def __init__(self, b_ctx, n_ctx, b_dec, n_dec):
        self.b_ctx, self.n_ctx, self.b_dec, self.n_dec = b_ctx, n_ctx, b_dec, n_dec
        self.t_ctx, self.t_dec = b_ctx * n_ctx, b_dec * n_dec
        self.t = self.t_ctx + self.t_dec
        self.tm = min(ROW_TILE, n_dec, self.t_ctx)
        assert self.t_ctx % self.tm == 0 and n_dec % self.tm == 0
        assert self.t_ctx % n_dec == 0 and self.t_ctx % n_ctx == 0
        self.n_ctx_tiles = self.t_ctx // self.tm
        self.n_tiles = self.t // self.tm

    def mod_row(self, i):
        r = i * self.tm
        return jnp.where(r < self.t_ctx, 0, 1 + (r - self.t_ctx) // self.n_dec)

    def part_tile(self, part, i):
        if part == 0:
            return jnp.minimum(i, self.n_ctx_tiles - 1)
        return jnp.maximum(i - self.n_ctx_tiles, 0)

    def row_specs(self, arrays, width, extra_grid_dims=0):
        pad = (0,) * 0
        if len(arrays) == 1:
            return [pl.BlockSpec((self.tm, width), lambda i, *_: (i, 0))]
        return [pl.BlockSpec((self.tm, width), lambda i, *_, p=p: (self.part_tile(p, i), 0)) for p in (0, 1)]


def _ada_kernel(c_ref, w_ref, b_ref, o_ref):
    s = jax.nn.silu(c_ref[...]).astype(BF16)
    acc = jnp.dot(s, w_ref[0].astype(BF16), preferred_element_type=F32)
    o_ref[0] = acc + b_ref[0]


def ada_modulation(cond, w_ada, b_ada):
    depth, d, n = w_ada.shape
    tn = 1536
    out = pl.pallas_call(
        _ada_kernel,
        out_shape=jax.ShapeDtypeStruct((depth, MOD_ROWS, n), F32),
        grid=(depth, n // tn),
        in_specs=[
            pl.BlockSpec((MOD_ROWS, d), lambda l, j: (0, 0)),
            pl.BlockSpec((1, d, tn), lambda l, j: (l, 0, j)),
            pl.BlockSpec((1, 1, tn), lambda l, j: (l, 0, j)),
        ],
        out_specs=pl.BlockSpec((1, MOD_ROWS, tn), lambda l, j: (l, 0, j)),
        compiler_params=_cparams("parallel", "parallel"),
        name="ada_modulation",
    )(cond, w_ada, b_ada.reshape(depth, 1, n))
    return out.reshape(depth, MOD_ROWS, N_MOD, d)


def _modulated_norm(x, g, mod_ref, shift_idx):
    ms = jnp.mean(x * x, axis=-1, keepdims=True)
    y = x * lax.rsqrt(ms + NORM_EPS) * g
    return y * (1.0 + mod_ref[0, shift_idx + 1:shift_idx + 2, :]) + mod_ref[0, shift_idx:shift_idx + 1, :]


def _by_part(is_dec, refs, fn):
    if len(refs) == 1:
        fn(refs[0])
        return
    pl.when(jnp.logical_not(is_dec))(lambda: fn(refs[0]))
    pl.when(is_dec)(lambda: fn(refs[1]))


def _rope64(a, cos, sin_signed, first_half):
    partner = jnp.where(first_half, pltpu.roll(a, 96, 1), pltpu.roll(a, 32, 1))
    return a * cos + partner * sin_signed


def _nmm_kernel(*refs, n_x, outs, tn, n_ctx_tiles):
    x_refs = refs[:n_x]
    g_ref, mod_ref, w_ref, cos_ref, sin_ref = refs[n_x:n_x + 5]
    out_refs = refs[n_x + 5:]
    is_dec = pl.program_id(0) >= n_ctx_tiles
    tm = x_refs[0].shape[0]
    segments = sorted({(col0, width) for col0, width, _, _, _ in outs})

    def rotated(a, kind):
        cos, sin = cos_ref[...], sin_ref[...]
        if kind == "rope64":
            lane = lax.broadcasted_iota(I32, (tm, LANES), 1)
            first_half = (lane & 32) == 0
            return [(c * 128, _rope64(a[:, c * 128:(c + 1) * 128], cos, sin, first_half))
                    for c in range(tn // 128)]
        pieces = []
        for c in range(tn // 256):
            x1, x2 = a[:, c * 256:c * 256 + 128], a[:, c * 256 + 128:(c + 1) * 256]
            pieces += [(c * 256, x1 * cos - x2 * sin), (c * 256 + 128, x2 * cos + x1 * sin)]
        return pieces

    def emit(dec):
        x_ref = x_refs[(1 if dec else 0) if n_x == 2 else 0]
        h = _modulated_norm(x_ref[...], g_ref[...], mod_ref, 0).astype(BF16)
        for col0, width in segments:
            sinks = [(o_ref, o) for o_ref, o in zip(out_refs, outs)
                     if (o[0], o[1]) == (col0, width) and not (dec and o[4] == "ctx")]
            for blk in range(width // tn):
                acc = jnp.dot(h, w_ref[:, col0 + blk * tn:col0 + (blk + 1) * tn], preferred_element_type=F32)
                for o_ref, (_, _, kind, scale, rows) in sinks:
                    a = acc if scale == 1.0 else acc * scale
                    if kind == "silu":
                        o_ref[:, blk * tn:(blk + 1) * tn] = jax.nn.silu(a).astype(o_ref.dtype)
                    elif kind == "plain" or rows == "ctx" or not dec:
                        o_ref[:, blk * tn:(blk + 1) * tn] = a.astype(o_ref.dtype)
                    else:
                        for off, val in rotated(a, kind):
                            o_ref[:, blk * tn + off:blk * tn + off + 128] = val.astype(o_ref.dtype)

    pl.when(jnp.logical_not(is_dec))(lambda: emit(False))
    pl.when(is_dec)(lambda: emit(True))


def norm_mod_matmul(lay, x_parts, g, mods, w, outputs, rope_tables, tn=512):
    d = x_parts[0].shape[1]
    tm = lay.tm
    n_total = w.shape[1]
    outs, out_shapes, out_specs = [], [], []
    for col0, width, kind, scale, dtype, rows in outputs:
        assert width % tn == 0 and col0 % LANES == 0
        outs.append((col0, width, kind, float(scale), rows))
        n_rows = lay.t_ctx if rows == "ctx" else lay.t
        out_shapes.append(jax.ShapeDtypeStruct((n_rows, width), dtype))
        out_specs.append(lay.row_specs([None, None], width)[0] if rows == "ctx" else lay.row_specs([None], width)[0])
    blocks_per_seq = lay.n_dec // tm
    rope_spec = pl.BlockSpec((tm, LANES), lambda i: (lay.part_tile(1, i) % blocks_per_seq, 0))
    return pl.pallas_call(
        functools.partial(_nmm_kernel, n_x=len(x_parts), outs=tuple(outs), tn=tn, n_ctx_tiles=lay.n_ctx_tiles),
        out_shape=out_shapes,
        grid=(lay.n_tiles,),
        in_specs=lay.row_specs(x_parts, d) + [
            pl.BlockSpec((1, d), lambda i: (0, 0)),
            pl.BlockSpec((1, N_MOD, d), lambda i: (lay.mod_row(i), 0, 0)),
            pl.BlockSpec((d, n_total), lambda i: (0, 0), pipeline_mode=pl.Buffered(1)),
            rope_spec, rope_spec,
        ],
        out_specs=out_specs,
        compiler_params=_cparams("arbitrary"),
        name="norm_mod_matmul",
    )(*x_parts, g.reshape(1, d), mods, w, *rope_tables)


def _diff_lambda(lam_ref, lam_init):
    lp = lam_ref[...]
    return (jnp.exp(jnp.sum(lp[0:1] * lp[1:2], axis=-1, keepdims=True))
            - jnp.exp(jnp.sum(lp[2:3] * lp[3:4], axis=-1, keepdims=True)) + lam_init)


def _component_scores(q, k):
    lane = lax.broadcasted_iota(I32, q.shape, 1)
    zero = jnp.zeros_like(q)
    return [_nt_dot(jnp.where((lane < DA_HEAD_DIM) == (comp == 0), q, zero), k) for comp in range(2)]


def _diff_softmax_values(scores, lam, v, subln_g, lam_init):
    exps = []
    for s in scores:
        e = jnp.exp2(s - jnp.max(s, axis=-1, keepdims=True))
        exps.append((e, 1.0 / jnp.sum(e, axis=-1, keepdims=True)))
    c0, c1 = exps[0][1], lam * exps[1][1]
    o = jnp.dot((exps[0][0] * c0 - exps[1][0] * c1).astype(BF16), v, preferred_element_type=F32)
    ms = jnp.mean(o * o, axis=-1, keepdims=True)
    return (o * lax.rsqrt(ms + NORM_EPS) * subln_g) * (1.0 - lam_init)


def _diff_attn_kernel(lam_ref, q_ref, k_ref, v_ref, g_ref, o_ref, *, lam_init, heads_per_step):
    hw = 2 * DA_HEAD_DIM
    lam = _diff_lambda(lam_ref, lam_init)
    for hh in range(heads_per_step):
        cols = slice(hh * hw, (hh + 1) * hw)
        scores = _component_scores(q_ref[:, cols], k_ref[:, cols])
        o_ref[:, cols] = _diff_softmax_values(scores, lam, v_ref[:, cols], g_ref[...], lam_init)


def _diff_attn_cached_kernel(lam_ref, q_ref, k_ref, v_ref, ck_ref, cv_ref, g_ref, o_ref, k_scr, v_scr, s_scr, *,
                             lam_init, sub_rows):
    seq_len = q_ref.shape[0]
    n_sub = seq_len // sub_rows
    assert n_sub % 2 == 0
    lam = _diff_lambda(lam_ref, lam_init)
    k_scr[0:seq_len, :] = k_ref[...]
    k_scr[seq_len:, :] = ck_ref[0].astype(BF16)
    v_scr[0:seq_len, :] = v_ref[...]
    v_scr[seq_len:, :] = cv_ref[0].astype(BF16)

    def rows_of(t):
        return pl.ds(pl.multiple_of(t * sub_rows, sub_rows), sub_rows)

    def scores(t, slot):
        for comp, s in enumerate(_component_scores(q_ref[rows_of(t), :], k_scr[...])):
            s_scr[slot, comp] = s

    def outputs(t, slot):
        o_ref[rows_of(t), :] = _diff_softmax_values([s_scr[slot, 0], s_scr[slot, 1]], lam, v_scr[...], g_ref[...],
                                                    lam_init)

    scores(0, 0)

    def body(i2, carry):
        t = 2 * i2
        scores(t + 1, 1)
        outputs(t, 0)
        scores(jnp.minimum(t + 2, n_sub - 1), 0)
        outputs(t + 1, 1)
        return carry

    lax.fori_loop(0, n_sub // 2, body, 0)


def diff_attention(q, k, v, row0, batch, seq_len, lam_params, subln_g, lam_init, cache=None, heads_per_step=1,
                   sub_rows=128):
    width = q.shape[1]
    hw = 2 * DA_HEAD_DIM
    bw = heads_per_step * hw
    assert row0 % seq_len == 0 and DA_HEADS % heads_per_step == 0
    s0 = row0 // seq_len
    seq_spec = pl.BlockSpec((seq_len, bw), lambda b, h: (s0 + b, h))
    in_specs = [pl.BlockSpec((4, DA_HEAD_DIM), lambda b, h: (0, 0)), seq_spec, seq_spec, seq_spec]
    args = [lam_params, q, k, v]
    scratch = []
    if cache is None:
        body = functools.partial(_diff_attn_kernel, lam_init=lam_init, heads_per_step=heads_per_step)
    else:
        assert heads_per_step == 1
        past = cache[0].shape[1]
        cache_spec = pl.BlockSpec((1, past, hw), lambda b, h: (b, 0, h))
        in_specs += [cache_spec, cache_spec]
        args += list(cache)
        sub_rows = min(sub_rows, seq_len // 2)
        scratch = [pltpu.VMEM((seq_len + past, hw), BF16), pltpu.VMEM((seq_len + past, hw), BF16),
                   pltpu.VMEM((2, 2, sub_rows, seq_len + past), F32)]
        body = functools.partial(_diff_attn_cached_kernel, lam_init=lam_init, sub_rows=sub_rows)
    in_specs.append(pl.BlockSpec((1, hw), lambda b, h: (0, 0)))
    args.append(subln_g.reshape(1, hw))
    return pl.pallas_call(
        body,
        out_shape=jax.ShapeDtypeStruct((batch * seq_len, width), F32),
        grid=(batch, DA_HEADS // heads_per_step),
        in_specs=in_specs,
        out_specs=pl.BlockSpec((seq_len, bw), lambda b, h: (b, h)),
        scratch_shapes=scratch,
        compiler_params=_cparams("parallel", "parallel"),
        name="diff_attention",
    )(*args)


def _retention_kernel(*refs, seq_len, has_state, emit_state):
    lg_ref, q_ref, k_ref, v_ref, gf_ref, gb_ref = refs[:6]
    pos = 6
    if has_state:
        s0_refs = refs[6:8]
        pos = 8
    o_ref = refs[pos]
    pos += 1
    if emit_state:
        s_out_refs = refs[pos:pos + 2]
        pos += 2
    s_scrs = refs[pos:pos + 2]

    h = pl.program_id(1)
    c_len = RET_CHUNK
    n_chunks = seq_len // c_len
    assert n_chunks % 2 == 0
    row = lax.broadcasted_iota(I32, (c_len, c_len), 0)
    colm = lax.broadcasted_iota(I32, (c_len, c_len), 1)
    rel = (row - colm).astype(F32)
    posv = lax.broadcasted_iota(I32, (c_len, 1), 0).astype(F32)

    consts = []
    for backward in (False, True):
        lg = lg_ref[1 if backward else 0, h]
        if backward:
            intra = jnp.where(rel <= 0, jnp.exp(-rel * lg), 0.0)
            q_decay = jnp.exp((c_len - posv) * lg)
            k_decay = jnp.exp(posv * lg)
        else:
            intra = jnp.where(rel >= 0, jnp.exp(rel * lg), 0.0)
            q_decay = jnp.exp((posv + 1.0) * lg)
            k_decay = jnp.exp((c_len - 1.0 - posv) * lg)
        consts.append((intra, q_decay, k_decay, jnp.exp(jnp.zeros((1, 1), F32) + c_len * lg)))
        s_scr = s_scrs[1 if backward else 0]
        if has_state:
            s_scr[...] = s0_refs[1 if backward else 0][0, 0, 0]
        else:
            s_scr[...] = jnp.zeros_like(s_scr)

    def chunk(c, backward, first_touch):
        intra, q_decay, k_decay, chunk_decay = consts[1 if backward else 0]
        s_scr = s_scrs[1 if backward else 0]
        gate_ref = gb_ref if backward else gf_ref
        rows = pl.ds(pl.multiple_of(c * c_len, c_len), c_len)
        qb = q_ref[rows, :].astype(BF16)
        kf = k_ref[rows, :].astype(F32)
        vb = v_ref[rows, :].astype(BF16)
        a = _nt_dot(qb, kf.astype(BF16)) * intra
        s = s_scr[...]
        o = (jnp.dot(a.astype(BF16), vb, preferred_element_type=F32)
             + jnp.dot(qb, s.astype(BF16), preferred_element_type=F32) * q_decay)
        s_scr[...] = chunk_decay * s + _tn_dot((kf * k_decay).astype(BF16), vb)
        mu = jnp.mean(o, axis=-1, keepdims=True)
        oc = o - mu
        var = jnp.mean(oc * oc, axis=-1, keepdims=True)
        gated = oc * lax.rsqrt(var + NORM_EPS) * gate_ref[rows, :].astype(F32)
        if first_touch:
            o_ref[rows, :] = gated
        else:
            o_ref[rows, :] = o_ref[rows, :] + gated

    def body(ci, first_touch):
        chunk(ci, False, first_touch)
        chunk(n_chunks - 1 - ci, True, first_touch)

    unroll = 4 if n_chunks % 8 == 0 else 1
    lax.fori_loop(0, n_chunks // 2, lambda ci, c: (body(ci, True), c)[1], 0, unroll=unroll)
    lax.fori_loop(n_chunks // 2, n_chunks, lambda ci, c: (body(ci, False), c)[1], 0, unroll=unroll)
    if emit_state:
        for d in range(2):
            s_out_refs[d][0, 0, 0] = s_scrs[d][...]


def retention(q, k, v, gates, row0, batch, seq_len, log_decay, state=None, emit_state=False):
    dk = q.shape[1] // RET_HEADS
    dv = v.shape[1] // RET_HEADS
    assert row0 % seq_len == 0
    s0 = row0 // seq_len
    in_specs = [
        pl.BlockSpec(memory_space=pltpu.SMEM),
        pl.BlockSpec((seq_len, dk), lambda b, h: (s0 + b, h)),
        pl.BlockSpec((seq_len, dk), lambda b, h: (s0 + b, h)),
        pl.BlockSpec((seq_len, dv), lambda b, h: (s0 + b, h)),
        pl.BlockSpec((seq_len, dv), lambda b, h: (s0 + b, h)),
        pl.BlockSpec((seq_len, dv), lambda b, h: (s0 + b, RET_HEADS + h)),
    ]
    args = [log_decay, q, k, v, gates, gates]
    state_spec = pl.BlockSpec((1, 1, 1, dk, dv), lambda b, h: (b, 0, h, 0, 0))
    if state is not None:
        in_specs += [state_spec, state_spec]
        args += list(state)
    out_shapes = [jax.ShapeDtypeStruct((batch * seq_len, RET_HEADS * dv), F32)]
    out_specs = [pl.BlockSpec((seq_len, dv), lambda b, h: (b, h))]
    if emit_state:
        out_shapes += [jax.ShapeDtypeStruct((batch, 1, RET_HEADS, dk, dv), F32)] * 2
        out_specs += [state_spec, state_spec]
    return pl.pallas_call(
        functools.partial(_retention_kernel, seq_len=seq_len, has_state=state is not None,
                          emit_state=emit_state),
        out_shape=out_shapes,
        grid=(batch, RET_HEADS),
        in_specs=in_specs,
        out_specs=out_specs,
        scratch_shapes=[pltpu.VMEM((dk, dv), F32), pltpu.VMEM((dk, dv), F32)],
        compiler_params=_cparams("parallel", "parallel"),
        name="retention",
    )(*args)


def _proj_residual_kernel(*refs, n_x, n_a, n_ctx_tiles):
    x_refs, a_refs = refs[:n_x], refs[n_x:n_x + n_a]
    w_ref, mod_ref, o_ref, y_scr = refs[n_x + n_a:]
    is_dec = pl.program_id(0) >= n_ctx_tiles

    def project(a_ref):
        y_scr[...] = mod_ref[0, 2:3, :] * jnp.dot(a_ref[...].astype(BF16), w_ref[...],
                                                  preferred_element_type=F32)

    def add(x_ref):
        o_ref[...] = x_ref[...] + y_scr[...]

    _by_part(is_dec, a_refs, project)
    _by_part(is_dec, x_refs, add)


def proj_residual(lay, x_parts, a_parts, w, mods):
    d = x_parts[0].shape[1]
    kdim = a_parts[0].shape[1]
    tm = lay.tm
    return pl.pallas_call(
        functools.partial(_proj_residual_kernel, n_x=len(x_parts), n_a=len(a_parts),
                          n_ctx_tiles=lay.n_ctx_tiles),
        out_shape=jax.ShapeDtypeStruct((lay.t, d), F32),
        grid=(lay.n_tiles,),
        in_specs=lay.row_specs(x_parts, d) + lay.row_specs(a_parts, kdim) + [
            pl.BlockSpec((kdim, d), lambda i: (0, 0)),
            pl.BlockSpec((1, N_MOD, d), lambda i: (lay.mod_row(i), 0, 0)),
        ],
        out_specs=pl.BlockSpec((tm, d), lambda i: (i, 0)),
        scratch_shapes=[pltpu.VMEM((tm, d), F32)],
        compiler_params=_cparams("arbitrary"),
        name="proj_residual",
    )(*x_parts, *a_parts, w, mods)


def _first_max_onehot(vals):
    m = vals[0]
    for v in vals[1:]:
        m = jnp.maximum(m, v)
    onehot, taken = [], None
    for v in vals:
        hit = v == m
        if taken is None:
            onehot.append(hit)
            taken = hit
        else:
            onehot.append(hit & jnp.logical_not(taken))
            taken = taken | hit
    return m, onehot


def _pick(onehot, vals):
    out = vals[-1]
    for oh, v in zip(onehot[-2::-1], vals[-2::-1]):
        out = jnp.where(oh, v, out)
    return out


def _router_kernel(x_ref, g_ref, mod_ref, wr_ref, bias_ref, tri_ref, h_ref, idx_ref, wcol_ref, rank_ref,
                   cnt_ref, ind_scr, wrow_scr):
    h = _modulated_norm(x_ref[...], g_ref[...], mod_ref, 3)
    h_ref[...] = h
    logits = _nt_dot(wr_ref[...], h.astype(BF16))
    s = jax.nn.sigmoid(logits)
    sel = s + bias_ref[...]
    neg_inf = jnp.full_like(sel[0:1], -jnp.inf)
    sel_rows = [sel[e:e + 1] for e in range(N_EXPERTS)]
    s_rows = [s[e:e + 1] for e in range(N_EXPERTS)]

    def top2(vals):
        m1, oh1 = _first_max_onehot(vals)
        rest = [jnp.where(o, neg_inf, v) for o, v in zip(oh1, vals)]
        m2, oh2 = _first_max_onehot(rest)
        return m1, m2, oh1, oh2

    grp_scores = []
    for g in range(N_GROUPS):
        m1, m2, _, _ = top2(sel_rows[g * EXPERTS_PER_GROUP:(g + 1) * EXPERTS_PER_GROUP])
        grp_scores.append(m1 + m2)
    _, in_grp = _first_max_onehot(grp_scores)
    cand_sel = [_pick(in_grp, [sel_rows[g * EXPERTS_PER_GROUP + k] for g in range(N_GROUPS)])
                for k in range(EXPERTS_PER_GROUP)]
    cand_s = [_pick(in_grp, [s_rows[g * EXPERTS_PER_GROUP + k] for g in range(N_GROUPS)])
              for k in range(EXPERTS_PER_GROUP)]
    _, _, oh1, oh2 = top2(cand_sel)
    w1 = _pick(oh1, cand_s)
    w2 = _pick(oh2, cand_s)
    denom = w1 + w2
    wrow_scr[...] = jnp.zeros_like(wrow_scr)
    wrow_scr[0:1, :] = w1 / denom
    wrow_scr[1:2, :] = w2 / denom
    wcol_ref[...] = jnp.transpose(wrow_scr[...])

    ints = [jnp.full(w1.shape, k, I32) for k in range(EXPERTS_PER_GROUP)]
    grp = _pick(in_grp, ints) * EXPERTS_PER_GROUP
    idx_ref[0:1, :] = grp + _pick(oh1, ints)
    idx_ref[1:2, :] = grp + _pick(oh2, ints)

    one, zero = jnp.ones_like(w1), jnp.zeros_like(w1)
    for g in range(N_GROUPS):
        for k in range(EXPERTS_PER_GROUP):
            e = g * EXPERTS_PER_GROUP + k
            ind_scr[e:e + 1, :] = jnp.where(in_grp[g] & (oh1[k] | oh2[k]), one, zero)
    ind = ind_scr[...].astype(BF16)
    ranks = jnp.dot(ind, tri_ref[...], preferred_element_type=F32)
    cnt_ref[0] = jnp.dot(ind, jnp.ones((ind.shape[1], LANES), BF16), preferred_element_type=F32)
    for slot, oh in ((0, oh1), (1, oh2)):
        r = zero
        for g in range(N_GROUPS):
            for k in range(EXPERTS_PER_GROUP):
                e = g * EXPERTS_PER_GROUP + k
                r = r + jnp.where(in_grp[g] & oh[k], ranks[e:e + 1], zero)
        rank_ref[slot:slot + 1, :] = r.astype(I32)


def route(lay, x, g, mods, w_router_t, router_bias):
    t, d = x.shape
    tm = lay.tm
    tri = jnp.triu(jnp.ones((tm, tm), BF16), k=1)
    pair = jax.ShapeDtypeStruct((TOP_K, t), I32)
    pair_spec = pl.BlockSpec((TOP_K, tm), lambda i: (0, i))
    return pl.pallas_call(
        _router_kernel,
        out_shape=[jax.ShapeDtypeStruct((t, d), F32), pair, jax.ShapeDtypeStruct((t, LANES), F32), pair,
                   jax.ShapeDtypeStruct((lay.n_tiles, N_EXPERTS, LANES), F32)],
        grid=(lay.n_tiles,),
        in_specs=[
            pl.BlockSpec((tm, d), lambda i: (i, 0)),
            pl.BlockSpec((1, d), lambda i: (0, 0)),
            pl.BlockSpec((1, N_MOD, d), lambda i: (lay.mod_row(i), 0, 0)),
            pl.BlockSpec((N_EXPERTS, d), lambda i: (0, 0)),
            pl.BlockSpec((N_EXPERTS, 1), lambda i: (0, 0)),
            pl.BlockSpec((tm, tm), lambda i: (0, 0)),
        ],
        out_specs=[pl.BlockSpec((tm, d), lambda i: (i, 0)), pair_spec,
                   pl.BlockSpec((tm, LANES), lambda i: (i, 0)), pair_spec,
                   pl.BlockSpec((1, N_EXPERTS, LANES), lambda i: (i, 0, 0))],
        scratch_shapes=[pltpu.VMEM((N_EXPERTS, tm), F32), pltpu.VMEM((LANES, tm), F32)],
        compiler_params=_cparams("parallel"),
        name="router",
    )(x, g.reshape(1, d), mods, w_router_t, router_bias.reshape(N_EXPERTS, 1), tri)


def dispatch_plan(lay, idx, rank, cnt, n_sorted):
    tm = lay.tm
    cnt_tile = cnt[:, :, 0].astype(I32)
    total = jnp.sum(cnt_tile, axis=0)
    padded = ((total + EXPERT_TILE - 1) // EXPERT_TILE) * EXPERT_TILE
    end = jnp.cumsum(padded)
    start = end - padded
    base = start[None, :] + jnp.cumsum(cnt_tile, axis=0) - cnt_tile
    idx3 = idx.reshape(TOP_K, lay.n_tiles, tm)
    pos = rank.reshape(TOP_K, lay.n_tiles, tm)
    for e in range(N_EXPERTS):
        pos = pos + jnp.where(idx3 == e, base[None, :, e, None], 0)
    tile_row = jnp.arange(n_sorted // EXPERT_TILE, dtype=I32) * EXPERT_TILE
    tile_expert = jnp.minimum(jnp.sum(end[None, :] <= tile_row[:, None], axis=1), N_EXPERTS - 1).astype(I32)
    n_valid = (end[-1] // EXPERT_TILE).astype(I32).reshape(1)
    return pos.reshape(TOP_K * lay.t).astype(I32), tile_expert, n_valid


def _sc_mesh():
    return plsc.VectorSubcoreMesh(core_axis_name="c", subcore_axis_name="s")


def _sc_worker_id():
    return lax.axis_index("s") * V7X_SC_CORES + lax.axis_index("c")


def sc_scatter_rows(src, pos, n_out_rows):
    t, d = src.shape
    n_idx = pos.shape[0]
    per_worker = n_idx // SC_WORKERS
    chunk = SC_CHUNK_ROWS
    assert n_idx % (SC_WORKERS * chunk) == 0 and t % chunk == 0

    @functools.partial(
        pl.kernel, mesh=_sc_mesh(), out_type=jax.ShapeDtypeStruct((n_out_rows, d), src.dtype),
        scratch_types=[pltpu.VMEM((chunk,), I32), pltpu.VMEM((chunk, d), src.dtype)],
        name="sc_scatter_rows")
    def scatter(src_hbm, pos_hbm, out_hbm, idx_v, rows_v):
        base = _sc_worker_id() * per_worker

        @pl.loop(0, per_worker // chunk)
        def _(i):
            a0 = base + i * chunk
            pltpu.sync_copy(pos_hbm.at[pl.ds(a0, chunk)], idx_v)
            pltpu.sync_copy(src_hbm.at[pl.ds(lax.rem(a0, t), chunk)], rows_v)
            pltpu.sync_copy(rows_v, out_hbm.at[idx_v])

    return scatter(src, pos)


def sc_gather_rows(table, idx):
    _, d = table.shape
    n_idx = idx.shape[0]
    per_worker = n_idx // SC_WORKERS
    chunk = SC_CHUNK_ROWS
    assert n_idx % (SC_WORKERS * chunk) == 0

    @functools.partial(
        pl.kernel, mesh=_sc_mesh(), out_type=jax.ShapeDtypeStruct((n_idx, d), table.dtype),
        scratch_types=[pltpu.VMEM((chunk,), I32), pltpu.VMEM((chunk, d), table.dtype)],
        name="sc_gather_rows")
    def gather(table_hbm, idx_hbm, out_hbm, idx_v, rows_v):
        base = _sc_worker_id() * per_worker

        @pl.loop(0, per_worker // chunk)
        def _(i):
            a0 = base + i * chunk
            pltpu.sync_copy(idx_hbm.at[pl.ds(a0, chunk)], idx_v)
            pltpu.sync_copy(table_hbm.at[idx_v], rows_v)
            pltpu.sync_copy(rows_v, out_hbm.at[pl.ds(a0, chunk)])

    return gather(table, idx)


def _experts_kernel(te_ref, nv_ref, x_ref, wg_ref, wu_ref, wd_ref, y_ref, wg_scr, wu_scr, wd_scr):
    i = pl.program_id(0)

    @pl.when(i < nv_ref[0])
    def _():
        @pl.when((i == 0) | (te_ref[i] != te_ref[jnp.maximum(i - 1, 0)]))
        def _():
            wg_scr[...] = wg_ref[0, 0].astype(BF16)
            wu_scr[...] = wu_ref[0, 0].astype(BF16)
            wd_scr[...] = wd_ref[0, 0].astype(BF16)

        x = x_ref[...].astype(BF16)
        a = (jax.nn.silu(jnp.dot(x, wg_scr[...], preferred_element_type=F32))
             * jnp.dot(x, wu_scr[...], preferred_element_type=F32))
        y_ref[...] = jnp.dot(a.astype(BF16), wd_scr[...], preferred_element_type=F32)


def grouped_experts(xs, tile_expert, n_valid, wg, wu, wd, layer):
    n_rows, d = xs.shape
    de = wg.shape[-1]
    tm = EXPERT_TILE
    row_map = lambda i, te, nv: (jnp.minimum(i, nv[0] - 1), 0)
    grid_spec = pltpu.PrefetchScalarGridSpec(
        num_scalar_prefetch=2,
        grid=(n_rows // tm,),
        in_specs=[
            pl.BlockSpec((tm, d), row_map),
            pl.BlockSpec((1, 1, d, de), lambda i, te, nv: (layer, te[i], 0, 0)),
            pl.BlockSpec((1, 1, d, de), lambda i, te, nv: (layer, te[i], 0, 0)),
            pl.BlockSpec((1, 1, de, d), lambda i, te, nv: (layer, te[i], 0, 0)),
        ],
        out_specs=pl.BlockSpec((tm, d), row_map),
        scratch_shapes=[pltpu.VMEM((d, de), BF16), pltpu.VMEM((d, de), BF16), pltpu.VMEM((de, d), BF16)],
    )
    return pl.pallas_call(
        _experts_kernel,
        out_shape=jax.ShapeDtypeStruct((n_rows, d), F32),
        grid_spec=grid_spec,
        compiler_params=_cparams("arbitrary"),
        name="grouped_experts",
    )(tile_expert, n_valid, xs, wg, wu, wd)


def _combine_kernel(x_ref, y_ref, w_ref, mod_ref, fg_ref, *o_refs, final_norm, n_ctx_tiles):
    w = w_ref[...]
    y = w[:, 0:1] * y_ref[0] + w[:, 1:2] * y_ref[1]
    out = x_ref[...] + mod_ref[0, 5:6, :] * y
    if final_norm:
        ms = jnp.mean(out * out, axis=-1, keepdims=True)
        out = out * lax.rsqrt(ms + NORM_EPS) * fg_ref[...]

    def store(o_ref):
        o_ref[...] = out
    _by_part(pl.program_id(0) >= n_ctx_tiles, o_refs, store)


def combine_residual(lay, x, y_pair, w_col, mods, final_g, final_norm, split):
    t, d = x.shape
    tm = lay.tm
    if split:
        out_shape = [jax.ShapeDtypeStruct((lay.t_ctx, d), F32), jax.ShapeDtypeStruct((lay.t_dec, d), F32)]
        out_specs = lay.row_specs([None, None], d)
    else:
        out_shape = [jax.ShapeDtypeStruct((t, d), F32)]
        out_specs = lay.row_specs([None], d)
    return pl.pallas_call(
        functools.partial(_combine_kernel, final_norm=final_norm, n_ctx_tiles=lay.n_ctx_tiles),
        out_shape=out_shape,
        grid=(lay.n_tiles,),
        in_specs=[
            pl.BlockSpec((tm, d), lambda i: (i, 0)),
            pl.BlockSpec((TOP_K, tm, d), lambda i: (0, i, 0)),
            pl.BlockSpec((tm, LANES), lambda i: (i, 0)),
            pl.BlockSpec((1, N_MOD, d), lambda i: (lay.mod_row(i), 0, 0)),
            pl.BlockSpec((1, d), lambda i: (0, 0)),
        ],
        out_specs=out_specs,
        compiler_params=_cparams("arbitrary"),
        name="combine_residual",
    )(x, y_pair, w_col, mods, final_g.reshape(1, d))


def group_moe_residual(lay, x, mods, p, layer, last):
    t, d = x.shape
    h, idx, w_col, rank, cnt = route(lay, x, p["norm_ffn_g"][layer], mods, p["w_router_t"], p["router_bias"])
    n_assign = TOP_K * t
    row_quantum = SC_WORKERS * SC_CHUNK_ROWS
    n_sorted = n_assign + N_EXPERTS * (EXPERT_TILE - 1)
    n_sorted = -(-n_sorted // row_quantum) * row_quantum
    n_sorted = -(-n_sorted // EXPERT_TILE) * EXPERT_TILE
    pos, tile_expert, n_valid = dispatch_plan(lay, idx, rank, cnt, n_sorted)
    xs = sc_scatter_rows(h, pos, n_sorted)
    ys = grouped_experts(xs, tile_expert, n_valid, p["moe_w_gate"], p["moe_w_up"], p["moe_w_down"], layer)
    y_pair = sc_gather_rows(ys, pos).reshape(TOP_K, t, d)
    return combine_residual(lay, x, y_pair, w_col, mods, p["final_norm_g"], final_norm=last, split=last)


def _rope_angles(n, d):
    n_rows = n // GRID_W
    row = jnp.repeat(jnp.arange(n_rows), GRID_W).astype(F32)
    col = jnp.tile(jnp.arange(GRID_W), n_rows).astype(F32)
    nf = d // 4
    freqs = jnp.power(ROPE_BASE, -jnp.arange(nf, dtype=F32) / nf)
    ang = jnp.concatenate([row[:, None] * freqs, col[:, None] * freqs], axis=-1)
    return jnp.cos(ang), jnp.sin(ang)


def kernel(x_prompt, x_sample, cache_attn_k, cache_attn_v, state_ret_fwd, state_ret_bwd, c, c_ctx, w_ada, b_ada, norm_mix_g, norm_ffn_g, final_norm_g, da_w_qkv, da_lambda_q1, da_lambda_k1, da_lambda_q2, da_lambda_k2, da_subln_g, da_w_o, ret_w_qkv, ret_w_gate_fwd, ret_w_gate_bwd, ret_decay_fwd, ret_decay_bwd, ret_w_o, w_router, router_bias, moe_w_gate, moe_w_up, moe_w_down):
    b_ctx, n_ctx, d = x_prompt.shape
    b_dec, n_dec, _ = x_sample.shape
    past = cache_attn_k.shape[2]
    n_attn = cache_attn_k.shape[1]
    depth = w_ada.shape[0]
    assert b_dec + 1 <= MOD_ROWS
    lay = Layout(b_ctx, n_ctx, b_dec, n_dec)

    cond = jnp.zeros((MOD_ROWS, d), F32).at[0].set(c_ctx).at[1:1 + b_dec].set(c)
    mods_all = ada_modulation(cond, w_ada, b_ada)

    p = {
        "norm_ffn_g": norm_ffn_g, "final_norm_g": final_norm_g,
        "w_router_t": w_router.T.astype(BF16), "router_bias": router_bias.astype(F32),
        "moe_w_gate": moe_w_gate, "moe_w_up": moe_w_up, "moe_w_down": moe_w_down,
    }
    ret_log_decay = jnp.stack([jax.nn.log_sigmoid(ret_decay_fwd.astype(F32)),
                               jax.nn.log_sigmoid(ret_decay_bwd.astype(F32))], axis=1)
    ck_all = cache_attn_k.reshape(b_dec, n_attn, past, -1)
    cv_all = cache_attn_v.reshape(b_dec, n_attn, past, -1)

    x_parts = [x_prompt.reshape(lay.t_ctx, d), x_sample.reshape(lay.t_dec, d)]
    new_k, new_v, new_sf, new_sb = [], [], [], []
    for i in range(depth):
        mods = mods_all[i]
        j = i // 2
        if i % 2 == 0:
            lam_init = 0.8 - 0.6 * math.exp(-0.3 * i)
            qkw = DA_HEADS * 2 * DA_HEAD_DIM
            vw = DA_HEADS * DA_V_DIM
            cos, sin = _rope_angles(n_dec, DA_HEAD_DIM)
            rope = (jnp.tile(cos, (1, 4)), jnp.concatenate([-sin, sin, -sin, sin], axis=-1))
            q, k, v, k_ctx, v_ctx = norm_mod_matmul(
                lay, x_parts, norm_mix_g[i], mods, da_w_qkv[j].astype(BF16),
                [(0, qkw, "rope64", math.log2(math.e) * DA_HEAD_DIM ** -0.5, BF16, "all"),
                 (qkw, qkw, "rope64", 1.0, BF16, "all"),
                 (2 * qkw, vw, "plain", 1.0, BF16, "all"),
                 (qkw, qkw, "plain", 1.0, F32, "ctx"), (2 * qkw, vw, "plain", 1.0, F32, "ctx")],
                rope)
            lam_params = jnp.stack([da_lambda_q1[j], da_lambda_k1[j], da_lambda_q2[j], da_lambda_k2[j]])
            mix = [diff_attention(q, k, v, 0, b_ctx, n_ctx, lam_params, da_subln_g[j], lam_init,
                                  heads_per_step=DA_HEADS),
                   diff_attention(q, k, v, lay.t_ctx, b_dec, n_dec, lam_params, da_subln_g[j], lam_init,
                                  cache=(ck_all[:, j], cv_all[:, j]))]
            x = proj_residual(lay, x_parts, mix, da_w_o[j].astype(BF16), mods)
            new_k.append(k_ctx.reshape(b_ctx, n_ctx, DA_HEADS, 2, DA_HEAD_DIM))
            new_v.append(v_ctx.reshape(b_ctx, n_ctx, DA_HEADS, DA_V_DIM))
        else:
            kd = ret_w_qkv.shape[2] // 4
            dv = 2 * kd
            w_all = jnp.concatenate([ret_w_qkv[j], ret_w_gate_fwd[j], ret_w_gate_bwd[j]], axis=-1).astype(BF16)
            q, k, v, gates = norm_mod_matmul(
                lay, x_parts, norm_mix_g[i], mods, w_all,
                [(0, kd, "rope256", 1.0, BF16, "all"),
                 (kd, kd, "rope256", (kd // RET_HEADS) ** -0.5, F32, "all"),
                 (2 * kd, dv, "plain", 1.0, BF16, "all"), (2 * kd + dv, 2 * dv, "silu", 1.0, BF16, "all")],
                _rope_angles(n_dec, kd // RET_HEADS))
            o_ctx, sf, sb = retention(q, k, v, gates, 0, b_ctx, n_ctx, ret_log_decay[j], emit_state=True)
            (o_dec,) = retention(q, k, v, gates, lay.t_ctx, b_dec, n_dec, ret_log_decay[j],
                                 state=(state_ret_fwd[:, j:j + 1], state_ret_bwd[:, j:j + 1]))
            x = proj_residual(lay, x_parts, [o_ctx, o_dec], ret_w_o[j].astype(BF16), mods)
            new_sf.append(sf)
            new_sb.append(sb)
        x_parts = group_moe_residual(lay, x, mods, p, i, last=(i == depth - 1))

    y_ctx, y_dec = x_parts
    return (y_ctx.reshape(b_ctx, n_ctx, d), y_dec.reshape(b_dec, n_dec, d),
            jnp.stack(new_k, axis=1), jnp.stack(new_v, axis=1),
            jnp.concatenate(new_sf, axis=1), jnp.concatenate(new_sb, axis=1))
```

```python
import functools
import math

import jax
import jax.numpy as jnp
from jax import lax
from jax.experimental import pallas as pl
from jax.experimental.pallas import tpu as pltpu
from jax.experimental.pallas import tpu_sc as plsc

F32 = jnp.float32
BF16 = jnp.bfloat16
I32 = jnp.int32

GRID_W = 64
ROPE_BASE = 10000.0
NORM_EPS = 1e-6
DA_HEADS = 8
DA_HEAD_DIM = 64
DA_V_DIM = 2 * DA_HEAD_DIM
RET_HEADS = 4
RET_CHUNK = 128
N_EXPERTS = 16
N_GROUPS = 4
EXPERTS_PER_GROUP = N_EXPERTS // N_GROUPS
TOP_K = 2
N_MOD = 6
MOD_ROWS = 16
LANES = 128

V7X_VMEM_LIMIT = 56 * 1024 * 1024
V7X_SC_CORES = 2
V7X_SC_SUBCORES = 16
SC_WORKERS = V7X_SC_CORES * V7X_SC_SUBCORES
SC_CHUNK_ROWS = 32

ROW_TILE = 512
EXPERT_TILE = 512


def _cparams(*sem):
    return pltpu.CompilerParams(dimension_semantics=sem, vmem_limit_bytes=V7X_VMEM_LIMIT)


def _nt_dot(a, b):
    return lax.dot_general(a, b, (((1,), (1,)), ((), ())), preferred_element_type=F32)


def _tn_dot(a, b):
    return lax.dot_general(a, b, (((0,), (0,)), ((), ())), preferred_element_type=F32)


class Layout:
    def __init__(self, b_ctx, n_ctx, b_dec, n_dec):
        self.b_ctx, self.n_ctx, self.b_dec, self.n_dec = b_ctx, n_ctx, b_dec, n_dec
        self.t_ctx, self.t_dec = b_ctx * n_ctx, b_dec * n_dec
        self.t = self.t_ctx + self.t_dec
        self.tm = min(ROW_TILE, n_dec, self.t_ctx)
        assert self.t_ctx % self.tm == 0 and n_dec % self.tm == 0
        assert self.t_ctx % n_dec == 0 and self.t_ctx % n_ctx == 0
        self.n_ctx_tiles = self.t_ctx // self.tm
        self.n_tiles = self.t // self.tm

    def mod_row(self, i):
        r = i * self.tm
        return jnp.where(r < self.t_ctx, 0, 1 + (r - self.t_ctx) // self.n_dec)

    def part_tile(self, part, i):
        if part == 0:
            return jnp.minimum(i, self.n_ctx_tiles - 1)
        return jnp.maximum(i - self.n_ctx_tiles, 0)

    def row_specs(self, arrays, width, extra_grid_dims=0):
        pad = (0,) * 0
        if len(arrays) == 1:
            return [pl.BlockSpec((self.tm, width), lambda i, *_: (i, 0))]
        return [pl.BlockSpec((self.tm, width), lambda i, *_, p=p: (self.part_tile(p, i), 0)) for p in (0, 1)]


def _ada_kernel(c_ref, w_ref, b_ref, o_ref):
    s = jax.nn.silu(c_ref[...]).astype(BF16)
    acc = jnp.dot(s, w_ref[0].astype(BF16), preferred_element_type=F32)
    o_ref[0] = acc + b_ref[0]


def ada_modulation(cond, w_ada, b_ada):
    depth, d, n = w_ada.shape
    tn = 1536
    out = pl.pallas_call(
        _ada_kernel,
        out_shape=jax.ShapeDtypeStruct((depth, MOD_ROWS, n), F32),
        grid=(depth, n // tn),
        in_specs=[
            pl.BlockSpec((MOD_ROWS, d), lambda l, j: (0, 0)),
            pl.BlockSpec((1, d, tn), lambda l, j: (l, 0, j)),
            pl.BlockSpec((1, 1, tn), lambda l, j: (l, 0, j)),
        ],
        out_specs=pl.BlockSpec((1, MOD_ROWS, tn), lambda l, j: (l, 0, j)),
        compiler_params=_cparams("parallel", "parallel"),
        name="ada_modulation",
    )(cond, w_ada, b_ada.reshape(depth, 1, n))
    return out.reshape(depth, MOD_ROWS, N_MOD, d)


def _modulated_norm(x, g, mod_ref, shift_idx):
    ms = jnp.mean(x * x, axis=-1, keepdims=True)
    y = x * lax.rsqrt(ms + NORM_EPS) * g
    return y * (1.0 + mod_ref[0, shift_idx + 1:shift_idx + 2, :]) + mod_ref[0, shift_idx:shift_idx + 1, :]


def _by_part(is_dec, refs, fn):
    if len(refs) == 1:
        fn(refs[0])
        return
    pl.when(jnp.logical_not(is_dec))(lambda: fn(refs[0]))
    pl.when(is_dec)(lambda: fn(refs[1]))


def _rope64(a, cos, sin_signed, first_half):
    partner = jnp.where(first_half, pltpu.roll(a, 96, 1), pltpu.roll(a, 32, 1))
    return a * cos + partner * sin_signed


def _nmm_kernel(*refs, n_x, outs, tn, n_ctx_tiles):
    x_refs = refs[:n_x]
    g_ref, mod_ref, w_ref, cos_ref, sin_ref = refs[n_x:n_x + 5]
    out_refs = refs[n_x + 5:]
    is_dec = pl.program_id(0) >= n_ctx_tiles
    tm = x_refs[0].shape[0]
    segments = sorted({(col0, width) for col0, width, _, _, _ in outs})

    def rotated(a, kind):
        cos, sin = cos_ref[...], sin_ref[...]
        if kind == "rope64":
            lane = lax.broadcasted_iota(I32, (tm, LANES), 1)
            first_half = (lane & 32) == 0
            return [(c * 128, _rope64(a[:, c * 128:(c + 1) * 128], cos, sin, first_half))
                    for c in range(tn // 128)]
        pieces = []
        for c in range(tn // 256):
            x1, x2 = a[:, c * 256:c * 256 + 128], a[:, c * 256 + 128:(c + 1) * 256]
            pieces += [(c * 256, x1 * cos - x2 * sin), (c * 256 + 128, x2 * cos + x1 * sin)]
        return pieces

    def emit(dec):
        x_ref = x_refs[(1 if dec else 0) if n_x == 2 else 0]
        h = _modulated_norm(x_ref[...], g_ref[...], mod_ref, 0).astype(BF16)
        for col0, width in segments:
            sinks = [(o_ref, o) for o_ref, o in zip(out_refs, outs)
                     if (o[0], o[1]) == (col0, width) and not (dec and o[4] == "ctx")]
            for blk in range(width // tn):
                acc = jnp.dot(h, w_ref[:, col0 + blk * tn:col0 + (blk + 1) * tn], preferred_element_type=F32)
                for o_ref, (_, _, kind, scale, rows) in sinks:
                    a = acc if scale == 1.0 else acc * scale
                    if kind == "silu":
                        o_ref[:, blk * tn:(blk + 1) * tn] = jax.nn.silu(a).astype(o_ref.dtype)
                    elif kind == "plain" or rows == "ctx" or not dec:
                        o_ref[:, blk * tn:(blk + 1) * tn] = a.astype(o_ref.dtype)
                    else:
                        for off, val in rotated(a, kind):
                            o_ref[:, blk * tn + off:blk * tn + off + 128] = val.astype(o_ref.dtype)

    pl.when(jnp.logical_not(is_dec))(lambda: emit(False))
    pl.when(is_dec)(lambda: emit(True))


def norm_mod_matmul(lay, x_parts, g, mods, w, outputs, rope_tables, tn=512):
    d = x_parts[0].shape[1]
    tm = lay.tm
    n_total = w.shape[1]
    outs, out_shapes, out_specs = [], [], []
    for col0, width, kind, scale, dtype, rows in outputs:
        assert width % tn == 0 and col0 % LANES == 0
        outs.append((col0, width, kind, float(scale), rows))
        n_rows = lay.t_ctx if rows == "ctx" else lay.t
        out_shapes.append(jax.ShapeDtypeStruct((n_rows, width), dtype))
        out_specs.append(lay.row_specs([None, None], width)[0] if rows == "ctx" else lay.row_specs([None], width)[0])
    blocks_per_seq = lay.n_dec // tm
    rope_spec = pl.BlockSpec((tm, LANES), lambda i: (lay.part_tile(1, i) % blocks_per_seq, 0))
    return pl.pallas_call(
        functools.partial(_nmm_kernel, n_x=len(x_parts), outs=tuple(outs), tn=tn, n_ctx_tiles=lay.n_ctx_tiles),
        out_shape=out_shapes,
        grid=(lay.n_tiles,),
        in_specs=lay.row_specs(x_parts, d) + [
            pl.BlockSpec((1, d), lambda i: (0, 0)),
            pl.BlockSpec((1, N_MOD, d), lambda i: (lay.mod_row(i), 0, 0)),
            pl.BlockSpec((d, n_total), lambda i: (0, 0), pipeline_mode=pl.Buffered(1)),
            rope_spec, rope_spec,
        ],
        out_specs=out_specs,
        compiler_params=_cparams("arbitrary"),
        name="norm_mod_matmul",
    )(*x_parts, g.reshape(1, d), mods, w, *rope_tables)


def _diff_lambda(lam_ref, lam_init):
    lp = lam_ref[...]
    return (jnp.exp(jnp.sum(lp[0:1] * lp[1:2], axis=-1, keepdims=True))
            - jnp.exp(jnp.sum(lp[2:3] * lp[3:4], axis=-1, keepdims=True)) + lam_init)


def _component_scores(q, k):
    lane = lax.broadcasted_iota(I32, q.shape, 1)
    zero = jnp.zeros_like(q)
    return [_nt_dot(jnp.where((lane < DA_HEAD_DIM) == (comp == 0), q, zero), k) for comp in range(2)]


def _diff_softmax_values(scores, lam, v, subln_g, lam_init):
    exps = []
    for s in scores:
        e = jnp.exp2(s - jnp.max(s, axis=-1, keepdims=True))
        exps.append((e, 1.0 / jnp.sum(e, axis=-1, keepdims=True)))
    c0, c1 = exps[0][1], lam * exps[1][1]
    o = jnp.dot((exps[0][0] * c0 - exps[1][0] * c1).astype(BF16), v, preferred_element_type=F32)
    ms = jnp.mean(o * o, axis=-1, keepdims=True)
    return (o * lax.rsqrt(ms + NORM_EPS) * subln_g) * (1.0 - lam_init)


def _diff_attn_kernel(lam_ref, q_ref, k_ref, v_ref, g_ref, o_ref, *, lam_init, heads_per_step):
    hw = 2 * DA_HEAD_DIM
    lam = _diff_lambda(lam_ref, lam_init)
    for hh in range(heads_per_step):
        cols = slice(hh * hw, (hh + 1) * hw)
        scores = _component_scores(q_ref[:, cols], k_ref[:, cols])
        o_ref[:, cols] = _diff_softmax_values(scores, lam, v_ref[:, cols], g_ref[...], lam_init)


def _diff_attn_cached_kernel(lam_ref, q_ref, k_ref, v_ref, ck_ref, cv_ref, g_ref, o_ref, k_scr, vt_scr, s_scr, *,
                             lam_init, sub_rows):
    seq_len, hw = q_ref.shape
    n_sub = seq_len // sub_rows
    assert n_sub % 2 == 0
    lam = _diff_lambda(lam_ref, lam_init)
    k_scr[0:seq_len, :] = k_ref[...]
    k_scr[seq_len:, :] = ck_ref[0].astype(BF16)
    vt_scr[0:hw, 0:seq_len] = jnp.transpose(v_ref[...].astype(F32)).astype(BF16)
    vt_scr[0:hw, seq_len:] = jnp.transpose(cv_ref[0]).astype(BF16)
    vt_scr[hw:, :] = jnp.ones((vt_scr.shape[0] - hw, vt_scr.shape[1]), BF16)

    def rows_of(t):
        return pl.ds(pl.multiple_of(t * sub_rows, sub_rows), sub_rows)

    def scores(t, slot):
        q = q_ref[rows_of(t), :]
        lane = lax.broadcasted_iota(I32, q.shape, 1)
        zero = jnp.zeros_like(q)
        for comp in range(2):
            s_scr[slot, comp] = _nt_dot(k_scr[...], jnp.where((lane < DA_HEAD_DIM) == (comp == 0), q, zero))

    def outputs(t, slot):
        parts = []
        for comp in range(2):
            s = s_scr[slot, comp]
            e = jnp.exp2(s - jnp.max(s, axis=0, keepdims=True)).astype(BF16)
            acc = jnp.dot(vt_scr[...], e, preferred_element_type=F32)
            parts.append(acc[0:hw] / acc[hw:hw + 1])
        o = parts[0] - lam * parts[1]
        ms = jnp.mean(o * o, axis=0, keepdims=True)
        o_ref[rows_of(t), :] = jnp.transpose((o * lax.rsqrt(ms + NORM_EPS) * g_ref[...]) * (1.0 - lam_init))

    scores(0, 0)

    def body(i2, carry):
        t = 2 * i2
        scores(t + 1, 1)
        outputs(t, 0)
        scores(jnp.minimum(t + 2, n_sub - 1), 0)
        outputs(t + 1, 1)
        return carry

    lax.fori_loop(0, n_sub // 2, body, 0)


def diff_attention(q, k, v, row0, batch, seq_len, lam_params, subln_g, lam_init, cache=None, heads_per_step=1,
                   sub_rows=256):
    width = q.shape[1]
    hw = 2 * DA_HEAD_DIM
    bw = heads_per_step * hw
    assert row0 % seq_len == 0 and DA_HEADS % heads_per_step == 0
    s0 = row0 // seq_len
    seq_spec = pl.BlockSpec((seq_len, bw), lambda b, h: (s0 + b, h))
    in_specs = [pl.BlockSpec((4, DA_HEAD_DIM), lambda b, h: (0, 0)), seq_spec, seq_spec, seq_spec]
    args = [lam_params, q, k, v]
    scratch = []
    if cache is None:
        body = functools.partial(_diff_attn_kernel, lam_init=lam_init, heads_per_step=heads_per_step)
    else:
        assert heads_per_step == 1
        past = cache[0].shape[1]
        cache_spec = pl.BlockSpec((1, past, hw), lambda b, h: (b, 0, h))
        in_specs += [cache_spec, cache_spec]
        args += list(cache)
        sub_rows = min(sub_rows, seq_len // 2)
        ones_rows = 16
        scratch = [pltpu.VMEM((seq_len + past, hw), BF16), pltpu.VMEM((hw + ones_rows, seq_len + past), BF16),
                   pltpu.VMEM((2, 2, seq_len + past, sub_rows), F32)]
        body = functools.partial(_diff_attn_cached_kernel, lam_init=lam_init, sub_rows=sub_rows)
        in_specs.append(pl.BlockSpec((hw, 1), lambda b, h: (0, 0)))
        args.append(subln_g.reshape(hw, 1))
    if cache is None:
        in_specs.append(pl.BlockSpec((1, hw), lambda b, h: (0, 0)))
        args.append(subln_g.reshape(1, hw))
    return pl.pallas_call(
        body,
        out_shape=jax.ShapeDtypeStruct((batch * seq_len, width), F32),
        grid=(batch, DA_HEADS // heads_per_step),
        in_specs=in_specs,
        out_specs=pl.BlockSpec((seq_len, bw), lambda b, h: (b, h)),
        scratch_shapes=scratch,
        compiler_params=_cparams("parallel", "parallel"),
        name="diff_attention",
    )(*args)


def _retention_kernel(*refs, seq_len, has_state, emit_state):
    lg_ref, q_ref, k_ref, v_ref, gf_ref, gb_ref = refs[:6]
    pos = 6
    if has_state:
        s0_refs = refs[6:8]
        pos = 8
    o_ref = refs[pos]
    pos += 1
    if emit_state:
        s_out_refs = refs[pos:pos + 2]
        pos += 2
    s_scrs = refs[pos:pos + 2]

    h = pl.program_id(1)
    c_len = RET_CHUNK
    n_chunks = seq_len // c_len
    assert n_chunks % 2 == 0
    row = lax.broadcasted_iota(I32, (c_len, c_len), 0)
    colm = lax.broadcasted_iota(I32, (c_len, c_len), 1)
    rel = (row - colm).astype(F32)
    posv = lax.broadcasted_iota(I32, (c_len, 1), 0).astype(F32)

    consts = []
    for backward in (False, True):
        lg = lg_ref[1 if backward else 0, h]
        if backward:
            intra = jnp.where(rel <= 0, jnp.exp(-rel * lg), 0.0)
            q_decay = jnp.exp((c_len - posv) * lg)
            k_decay = jnp.exp(posv * lg)
        else:
            intra = jnp.where(rel >= 0, jnp.exp(rel * lg), 0.0)
            q_decay = jnp.exp((posv + 1.0) * lg)
            k_decay = jnp.exp((c_len - 1.0 - posv) * lg)
        consts.append((intra, q_decay, k_decay, jnp.exp(jnp.zeros((1, 1), F32) + c_len * lg)))
        s_scr = s_scrs[1 if backward else 0]
        if has_state:
            s_scr[...] = s0_refs[1 if backward else 0][0, 0, 0]
        else:
            s_scr[...] = jnp.zeros_like(s_scr)

    def chunk(c, backward, first_touch):
        intra, q_decay, k_decay, chunk_decay = consts[1 if backward else 0]
        s_scr = s_scrs[1 if backward else 0]
        gate_ref = gb_ref if backward else gf_ref
        rows = pl.ds(pl.multiple_of(c * c_len, c_len), c_len)
        qb = q_ref[rows, :].astype(BF16)
        kf = k_ref[rows, :].astype(F32)
        vb = v_ref[rows, :].astype(BF16)
        a = _nt_dot(qb, kf.astype(BF16)) * intra
        s = s_scr[...]
        o = (jnp.dot(a.astype(BF16), vb, preferred_element_type=F32)
             + jnp.dot(qb, s.astype(BF16), preferred_element_type=F32) * q_decay)
        s_scr[...] = chunk_decay * s + _tn_dot((kf * k_decay).astype(BF16), vb)
        mu = jnp.mean(o, axis=-1, keepdims=True)
        oc = o - mu
        var = jnp.mean(oc * oc, axis=-1, keepdims=True)
        gated = oc * lax.rsqrt(var + NORM_EPS) * gate_ref[rows, :].astype(F32)
        if first_touch:
            o_ref[rows, :] = gated
        else:
            o_ref[rows, :] = o_ref[rows, :] + gated

    def body(ci, first_touch):
        chunk(ci, False, first_touch)
        chunk(n_chunks - 1 - ci, True, first_touch)

    unroll = 4 if n_chunks % 8 == 0 else 1
    lax.fori_loop(0, n_chunks // 2, lambda ci, c: (body(ci, True), c)[1], 0, unroll=unroll)
    lax.fori_loop(n_chunks // 2, n_chunks, lambda ci, c: (body(ci, False), c)[1], 0, unroll=unroll)
    if emit_state:
        for d in range(2):
            s_out_refs[d][0, 0, 0] = s_scrs[d][...]


def retention(q, k, v, gates, row0, batch, seq_len, log_decay, state=None, emit_state=False):
    dk = q.shape[1] // RET_HEADS
    dv = v.shape[1] // RET_HEADS
    assert row0 % seq_len == 0
    s0 = row0 // seq_len
    in_specs = [
        pl.BlockSpec(memory_space=pltpu.SMEM),
        pl.BlockSpec((seq_len, dk), lambda b, h: (s0 + b, h)),
        pl.BlockSpec((seq_len, dk), lambda b, h: (s0 + b, h)),
        pl.BlockSpec((seq_len, dv), lambda b, h: (s0 + b, h)),
        pl.BlockSpec((seq_len, dv), lambda b, h: (s0 + b, h)),
        pl.BlockSpec((seq_len, dv), lambda b, h: (s0 + b, RET_HEADS + h)),
    ]
    args = [log_decay, q, k, v, gates, gates]
    state_spec = pl.BlockSpec((1, 1, 1, dk, dv), lambda b, h: (b, 0, h, 0, 0))
    if state is not None:
        in_specs += [state_spec, state_spec]
        args += list(state)
    out_shapes = [jax.ShapeDtypeStruct((batch * seq_len, RET_HEADS * dv), F32)]
    out_specs = [pl.BlockSpec((seq_len, dv), lambda b, h: (b, h))]
    if emit_state:
        out_shapes += [jax.ShapeDtypeStruct((batch, 1, RET_HEADS, dk, dv), F32)] * 2
        out_specs += [state_spec, state_spec]
    return pl.pallas_call(
        functools.partial(_retention_kernel, seq_len=seq_len, has_state=state is not None,
                          emit_state=emit_state),
        out_shape=out_shapes,
        grid=(batch, RET_HEADS),
        in_specs=in_specs,
        out_specs=out_specs,
        scratch_shapes=[pltpu.VMEM((dk, dv), F32), pltpu.VMEM((dk, dv), F32)],
        compiler_params=_cparams("parallel", "parallel"),
        name="retention",
    )(*args)


def _proj_residual_kernel(*refs, n_x, n_a, n_ctx_tiles):
    x_refs, a_refs = refs[:n_x], refs[n_x:n_x + n_a]
    w_ref, mod_ref, o_ref, y_scr = refs[n_x + n_a:]
    is_dec = pl.program_id(0) >= n_ctx_tiles

    def project(a_ref):
        y_scr[...] = mod_ref[0, 2:3, :] * jnp.dot(a_ref[...].astype(BF16), w_ref[...],
                                                  preferred_element_type=F32)

    def add(x_ref):
        o_ref[...] = x_ref[...] + y_scr[...]

    _by_part(is_dec, a_refs, project)
    _by_part(is_dec, x_refs, add)


def proj_residual(lay, x_parts, a_parts, w, mods):
    d = x_parts[0].shape[1]
    kdim = a_parts[0].shape[1]
    tm = lay.tm
    return pl.pallas_call(
        functools.partial(_proj_residual_kernel, n_x=len(x_parts), n_a=len(a_parts),
                          n_ctx_tiles=lay.n_ctx_tiles),
        out_shape=jax.ShapeDtypeStruct((lay.t, d), F32),
        grid=(lay.n_tiles,),
        in_specs=lay.row_specs(x_parts, d) + lay.row_specs(a_parts, kdim) + [
            pl.BlockSpec((kdim, d), lambda i: (0, 0)),
            pl.BlockSpec((1, N_MOD, d), lambda i: (lay.mod_row(i), 0, 0)),
        ],
        out_specs=pl.BlockSpec((tm, d), lambda i: (i, 0)),
        scratch_shapes=[pltpu.VMEM((tm, d), F32)],
        compiler_params=_cparams("arbitrary"),
        name="proj_residual",
    )(*x_parts, *a_parts, w, mods)


def _pack_bf16_halves(x):
    half = x.shape[1] // 2
    lo = lax.bitcast_convert_type(x[:, :half].astype(F32), jnp.uint32)
    hi = lax.bitcast_convert_type(x[:, half:].astype(F32), jnp.uint32)
    return (hi & jnp.uint32(0xFFFF0000)) | (lo >> 16)


def _unpack_bf16_halves(p):
    lo = lax.bitcast_convert_type(p << 16, F32).astype(BF16)
    hi = lax.bitcast_convert_type(p & jnp.uint32(0xFFFF0000), F32).astype(BF16)
    return lo, hi


def _first_max_onehot(vals):
    m = vals[0]
    for v in vals[1:]:
        m = jnp.maximum(m, v)
    onehot, taken = [], None
    for v in vals:
        hit = v == m
        if taken is None:
            onehot.append(hit)
            taken = hit
        else:
            onehot.append(hit & jnp.logical_not(taken))
            taken = taken | hit
    return m, onehot


def _pick(onehot, vals):
    out = vals[-1]
    for oh, v in zip(onehot[-2::-1], vals[-2::-1]):
        out = jnp.where(oh, v, out)
    return out


def _router_kernel(x_ref, g_ref, mod_ref, wr_ref, bias_ref, tri_ref, h_ref, idx_ref, wcol_ref, rank_ref,
                   cnt_ref, ind_scr, wrow_scr):
    hb = _modulated_norm(x_ref[...], g_ref[...], mod_ref, 3).astype(BF16)
    h_ref[...] = _pack_bf16_halves(hb)
    logits = _nt_dot(wr_ref[...], hb)
    s = jax.nn.sigmoid(logits)
    sel = s + bias_ref[...]
    neg_inf = jnp.full_like(sel[0:1], -jnp.inf)
    sel_rows = [sel[e:e + 1] for e in range(N_EXPERTS)]
    s_rows = [s[e:e + 1] for e in range(N_EXPERTS)]

    def top2(vals):
        m1, oh1 = _first_max_onehot(vals)
        rest = [jnp.where(o, neg_inf, v) for o, v in zip(oh1, vals)]
        m2, oh2 = _first_max_onehot(rest)
        return m1, m2, oh1, oh2

    grp_scores = []
    for g in range(N_GROUPS):
        m1, m2, _, _ = top2(sel_rows[g * EXPERTS_PER_GROUP:(g + 1) * EXPERTS_PER_GROUP])
        grp_scores.append(m1 + m2)
    _, in_grp = _first_max_onehot(grp_scores)
    cand_sel = [_pick(in_grp, [sel_rows[g * EXPERTS_PER_GROUP + k] for g in range(N_GROUPS)])
                for k in range(EXPERTS_PER_GROUP)]
    cand_s = [_pick(in_grp, [s_rows[g * EXPERTS_PER_GROUP + k] for g in range(N_GROUPS)])
              for k in range(EXPERTS_PER_GROUP)]
    _, _, oh1, oh2 = top2(cand_sel)
    w1 = _pick(oh1, cand_s)
    w2 = _pick(oh2, cand_s)
    denom = w1 + w2
    wrow_scr[...] = jnp.zeros_like(wrow_scr)
    wrow_scr[0:1, :] = w1 / denom
    wrow_scr[1:2, :] = w2 / denom
    wcol_ref[...] = jnp.transpose(wrow_scr[...])

    ints = [jnp.full(w1.shape, k, I32) for k in range(EXPERTS_PER_GROUP)]
    grp = _pick(in_grp, ints) * EXPERTS_PER_GROUP
    idx_ref[0:1, :] = grp + _pick(oh1, ints)
    idx_ref[1:2, :] = grp + _pick(oh2, ints)

    one, zero = jnp.ones_like(w1), jnp.zeros_like(w1)
    for g in range(N_GROUPS):
        for k in range(EXPERTS_PER_GROUP):
            e = g * EXPERTS_PER_GROUP + k
            ind_scr[e:e + 1, :] = jnp.where(in_grp[g] & (oh1[k] | oh2[k]), one, zero)
    ind = ind_scr[...].astype(BF16)
    ranks = jnp.dot(ind, tri_ref[...], preferred_element_type=F32)
    cnt_ref[0] = jnp.dot(ind, jnp.ones((ind.shape[1], LANES), BF16), preferred_element_type=F32)
    for slot, oh in ((0, oh1), (1, oh2)):
        r = zero
        for g in range(N_GROUPS):
            for k in range(EXPERTS_PER_GROUP):
                e = g * EXPERTS_PER_GROUP + k
                r = r + jnp.where(in_grp[g] & oh[k], ranks[e:e + 1], zero)
        rank_ref[slot:slot + 1, :] = r.astype(I32)


def route(lay, x, g, mods, w_router_t, router_bias):
    t, d = x.shape
    tm = lay.tm
    tri = jnp.triu(jnp.ones((tm, tm), BF16), k=1)
    pair = jax.ShapeDtypeStruct((TOP_K, t), I32)
    pair_spec = pl.BlockSpec((TOP_K, tm), lambda i: (0, i))
    return pl.pallas_call(
        _router_kernel,
        out_shape=[jax.ShapeDtypeStruct((t, d // 2), jnp.uint32), pair, jax.ShapeDtypeStruct((t, LANES), F32), pair,
                   jax.ShapeDtypeStruct((lay.n_tiles, N_EXPERTS, LANES), F32)],
        grid=(lay.n_tiles,),
        in_specs=[
            pl.BlockSpec((tm, d), lambda i: (i, 0)),
            pl.BlockSpec((1, d), lambda i: (0, 0)),
            pl.BlockSpec((1, N_MOD, d), lambda i: (lay.mod_row(i), 0, 0)),
            pl.BlockSpec((N_EXPERTS, d), lambda i: (0, 0)),
            pl.BlockSpec((N_EXPERTS, 1), lambda i: (0, 0)),
            pl.BlockSpec((tm, tm), lambda i: (0, 0)),
        ],
        out_specs=[pl.BlockSpec((tm, d // 2), lambda i: (i, 0)), pair_spec,
                   pl.BlockSpec((tm, LANES), lambda i: (i, 0)), pair_spec,
                   pl.BlockSpec((1, N_EXPERTS, LANES), lambda i: (i, 0, 0))],
        scratch_shapes=[pltpu.VMEM((N_EXPERTS, tm), F32), pltpu.VMEM((LANES, tm), F32)],
        compiler_params=_cparams("parallel"),
        name="router",
    )(x, g.reshape(1, d), mods, w_router_t, router_bias.reshape(N_EXPERTS, 1), tri)


def dispatch_plan(lay, idx, rank, cnt, n_sorted):
    tm = lay.tm
    cnt_tile = cnt[:, :, 0].astype(I32)
    total = jnp.sum(cnt_tile, axis=0)
    padded = ((total + EXPERT_TILE - 1) // EXPERT_TILE) * EXPERT_TILE
    end = jnp.cumsum(padded)
    start = end - padded
    base = start[None, :] + jnp.cumsum(cnt_tile, axis=0) - cnt_tile
    idx3 = idx.reshape(TOP_K, lay.n_tiles, tm)
    pos = rank.reshape(TOP_K, lay.n_tiles, tm)
    for e in range(N_EXPERTS):
        pos = pos + jnp.where(idx3 == e, base[None, :, e, None], 0)
    tile_row = jnp.arange(n_sorted // EXPERT_TILE, dtype=I32) * EXPERT_TILE
    tile_expert = jnp.minimum(jnp.sum(end[None, :] <= tile_row[:, None], axis=1), N_EXPERTS - 1).astype(I32)
    n_valid = (end[-1] // EXPERT_TILE).astype(I32).reshape(1)
    return pos.reshape(TOP_K * lay.t).astype(I32), tile_expert, n_valid


def _sc_mesh():
    return plsc.VectorSubcoreMesh(core_axis_name="c", subcore_axis_name="s")


def _sc_worker_id():
    return lax.axis_index("s") * V7X_SC_CORES + lax.axis_index("c")


def sc_scatter_rows(src, pos, n_out_rows):
    t, d = src.shape
    n_idx = pos.shape[0]
    per_worker = n_idx // SC_WORKERS
    chunk = SC_CHUNK_ROWS
    assert n_idx % (SC_WORKERS * chunk) == 0 and t % chunk == 0

    @functools.partial(
        pl.kernel, mesh=_sc_mesh(), out_type=jax.ShapeDtypeStruct((n_out_rows, d), src.dtype),
        scratch_types=[pltpu.VMEM((chunk,), I32), pltpu.VMEM((chunk, d), src.dtype)],
        name="sc_scatter_rows")
    def scatter(src_hbm, pos_hbm, out_hbm, idx_v, rows_v):
        base = _sc_worker_id() * per_worker

        @pl.loop(0, per_worker // chunk)
        def _(i):
            a0 = base + i * chunk
            pltpu.sync_copy(pos_hbm.at[pl.ds(a0, chunk)], idx_v)
            pltpu.sync_copy(src_hbm.at[pl.ds(lax.rem(a0, t), chunk)], rows_v)
            pltpu.sync_copy(rows_v, out_hbm.at[idx_v])

    return scatter(src, pos)


def sc_gather_rows(table, idx):
    _, d = table.shape
    n_idx = idx.shape[0]
    per_worker = n_idx // SC_WORKERS
    chunk = SC_CHUNK_ROWS
    assert n_idx % (SC_WORKERS * chunk) == 0

    @functools.partial(
        pl.kernel, mesh=_sc_mesh(), out_type=jax.ShapeDtypeStruct((n_idx, d), table.dtype),
        scratch_types=[pltpu.VMEM((chunk,), I32), pltpu.VMEM((chunk, d), table.dtype)],
        name="sc_gather_rows")
    def gather(table_hbm, idx_hbm, out_hbm, idx_v, rows_v):
        base = _sc_worker_id() * per_worker

        @pl.loop(0, per_worker // chunk)
        def _(i):
            a0 = base + i * chunk
            pltpu.sync_copy(idx_hbm.at[pl.ds(a0, chunk)], idx_v)
            pltpu.sync_copy(table_hbm.at[idx_v], rows_v)
            pltpu.sync_copy(rows_v, out_hbm.at[pl.ds(a0, chunk)])

    return gather(table, idx)


def _experts_kernel(te_ref, nv_ref, x_ref, wg_ref, wu_ref, wd_ref, y_ref, wg_scr, wu_scr, wd_scr):
    i = pl.program_id(0)

    @pl.when(i < nv_ref[0])
    def _():
        @pl.when((i == 0) | (te_ref[i] != te_ref[jnp.maximum(i - 1, 0)]))
        def _():
            wg_scr[...] = wg_ref[0, 0].astype(BF16)
            wu_scr[...] = wu_ref[0, 0].astype(BF16)
            wd_scr[...] = wd_ref[0, 0].astype(BF16)

        x_lo, x_hi = _unpack_bf16_halves(x_ref[...])
        half = x_lo.shape[1]

        def in_proj(w_scr):
            return (jnp.dot(x_lo, w_scr[0:half, :], preferred_element_type=F32)
                    + jnp.dot(x_hi, w_scr[half:, :], preferred_element_type=F32))

        a = jax.nn.silu(in_proj(wg_scr)) * in_proj(wu_scr)
        y_ref[...] = jnp.dot(a.astype(BF16), wd_scr[...], preferred_element_type=F32)


def grouped_experts(xs, tile_expert, n_valid, wg, wu, wd, layer):
    n_rows = xs.shape[0]
    d, de = wg.shape[-2:]
    tm = EXPERT_TILE
    row_map = lambda i, te, nv: (jnp.minimum(i, nv[0] - 1), 0)
    grid_spec = pltpu.PrefetchScalarGridSpec(
        num_scalar_prefetch=2,
        grid=(n_rows // tm,),
        in_specs=[
            pl.BlockSpec((tm, d // 2), row_map),
            pl.BlockSpec((1, 1, d, de), lambda i, te, nv: (layer, te[i], 0, 0)),
            pl.BlockSpec((1, 1, d, de), lambda i, te, nv: (layer, te[i], 0, 0)),
            pl.BlockSpec((1, 1, de, d), lambda i, te, nv: (layer, te[i], 0, 0)),
        ],
        out_specs=pl.BlockSpec((tm, d), row_map),
        scratch_shapes=[pltpu.VMEM((d, de), BF16), pltpu.VMEM((d, de), BF16), pltpu.VMEM((de, d), BF16)],
    )
    return pl.pallas_call(
        _experts_kernel,
        out_shape=jax.ShapeDtypeStruct((n_rows, d), F32),
        grid_spec=grid_spec,
        compiler_params=_cparams("arbitrary"),
        name="grouped_experts",
    )(tile_expert, n_valid, xs, wg, wu, wd)


def _combine_kernel(x_ref, y_ref, w_ref, mod_ref, fg_ref, *o_refs, final_norm, n_ctx_tiles):
    w = w_ref[...]
    y = w[:, 0:1] * y_ref[0] + w[:, 1:2] * y_ref[1]
    out = x_ref[...] + mod_ref[0, 5:6, :] * y
    if final_norm:
        ms = jnp.mean(out * out, axis=-1, keepdims=True)
        out = out * lax.rsqrt(ms + NORM_EPS) * fg_ref[...]

    def store(o_ref):
        o_ref[...] = out
    _by_part(pl.program_id(0) >= n_ctx_tiles, o_refs, store)


def combine_residual(lay, x, y_pair, w_col, mods, final_g, final_norm, split):
    t, d = x.shape
    tm = lay.tm
    if split:
        out_shape = [jax.ShapeDtypeStruct((lay.t_ctx, d), F32), jax.ShapeDtypeStruct((lay.t_dec, d), F32)]
        out_specs = lay.row_specs([None, None], d)
    else:
        out_shape = [jax.ShapeDtypeStruct((t, d), F32)]
        out_specs = lay.row_specs([None], d)
    return pl.pallas_call(
        functools.partial(_combine_kernel, final_norm=final_norm, n_ctx_tiles=lay.n_ctx_tiles),
        out_shape=out_shape,
        grid=(lay.n_tiles,),
        in_specs=[
            pl.BlockSpec((tm, d), lambda i: (i, 0)),
            pl.BlockSpec((TOP_K, tm, d), lambda i: (0, i, 0)),
            pl.BlockSpec((tm, LANES), lambda i: (i, 0)),
            pl.BlockSpec((1, N_MOD, d), lambda i: (lay.mod_row(i), 0, 0)),
            pl.BlockSpec((1, d), lambda i: (0, 0)),
        ],
        out_specs=out_specs,
        compiler_params=_cparams("arbitrary"),
        name="combine_residual",
    )(x, y_pair, w_col, mods, final_g.reshape(1, d))


def group_moe_residual(lay, x, mods, p, layer, last):
    t, d = x.shape
    h, idx, w_col, rank, cnt = route(lay, x, p["norm_ffn_g"][layer], mods, p["w_router_t"], p["router_bias"])
    n_assign = TOP_K * t
    row_quantum = SC_WORKERS * SC_CHUNK_ROWS
    n_sorted = n_assign + N_EXPERTS * (EXPERT_TILE - 1)
    n_sorted = -(-n_sorted // row_quantum) * row_quantum
    n_sorted = -(-n_sorted // EXPERT_TILE) * EXPERT_TILE
    pos, tile_expert, n_valid = dispatch_plan(lay, idx, rank, cnt, n_sorted)
    xs = sc_scatter_rows(h, pos, n_sorted)
    ys = grouped_experts(xs, tile_expert, n_valid, p["moe_w_gate"], p["moe_w_up"], p["moe_w_down"], layer)
    y_pair = sc_gather_rows(ys, pos).reshape(TOP_K, t, d)
    return combine_residual(lay, x, y_pair, w_col, mods, p["final_norm_g"], final_norm=last, split=last)


def _rope_angles(n, d):
    n_rows = n // GRID_W
    row = jnp.repeat(jnp.arange(n_rows), GRID_W).astype(F32)
    col = jnp.tile(jnp.arange(GRID_W), n_rows).astype(F32)
    nf = d // 4
    freqs = jnp.power(ROPE_BASE, -jnp.arange(nf, dtype=F32) / nf)
    ang = jnp.concatenate([row[:, None] * freqs, col[:, None] * freqs], axis=-1)
    return jnp.cos(ang), jnp.sin(ang)


def kernel(x_prompt, x_sample, cache_attn_k, cache_attn_v, state_ret_fwd, state_ret_bwd, c, c_ctx, w_ada, b_ada, norm_mix_g, norm_ffn_g, final_norm_g, da_w_qkv, da_lambda_q1, da_lambda_k1, da_lambda_q2, da_lambda_k2, da_subln_g, da_w_o, ret_w_qkv, ret_w_gate_fwd, ret_w_gate_bwd, ret_decay_fwd, ret_decay_bwd, ret_w_o, w_router, router_bias, moe_w_gate, moe_w_up, moe_w_down):
    b_ctx, n_ctx, d = x_prompt.shape
    b_dec, n_dec, _ = x_sample.shape
    past = cache_attn_k.shape[2]
    n_attn = cache_attn_k.shape[1]
    depth = w_ada.shape[0]
    assert b_dec + 1 <= MOD_ROWS
    lay = Layout(b_ctx, n_ctx, b_dec, n_dec)

    cond = jnp.zeros((MOD_ROWS, d), F32).at[0].set(c_ctx).at[1:1 + b_dec].set(c)
    mods_all = ada_modulation(cond, w_ada, b_ada)

    p = {
        "norm_ffn_g": norm_ffn_g, "final_norm_g": final_norm_g,
        "w_router_t": w_router.T.astype(BF16), "router_bias": router_bias.astype(F32),
        "moe_w_gate": moe_w_gate, "moe_w_up": moe_w_up, "moe_w_down": moe_w_down,
    }
    ret_log_decay = jnp.stack([jax.nn.log_sigmoid(ret_decay_fwd.astype(F32)),
                               jax.nn.log_sigmoid(ret_decay_bwd.astype(F32))], axis=1)
    ck_all = cache_attn_k.reshape(b_dec, n_attn, past, -1)
    cv_all = cache_attn_v.reshape(b_dec, n_attn, past, -1)

    x_parts = [x_prompt.reshape(lay.t_ctx, d), x_sample.reshape(lay.t_dec, d)]
    new_k, new_v, new_sf, new_sb = [], [], [], []
    for i in range(depth):
        mods = mods_all[i]
        j = i // 2
        if i % 2 == 0:
            lam_init = 0.8 - 0.6 * math.exp(-0.3 * i)
            qkw = DA_HEADS * 2 * DA_HEAD_DIM
            vw = DA_HEADS * DA_V_DIM
            cos, sin = _rope_angles(n_dec, DA_HEAD_DIM)
            rope = (jnp.tile(cos, (1, 4)), jnp.concatenate([-sin, sin, -sin, sin], axis=-1))
            q, k, v, k_ctx, v_ctx = norm_mod_matmul(
                lay, x_parts, norm_mix_g[i], mods, da_w_qkv[j].astype(BF16),
                [(0, qkw, "rope64", math.log2(math.e) * DA_HEAD_DIM ** -0.5, BF16, "all"),
                 (qkw, qkw, "rope64", 1.0, BF16, "all"),
                 (2 * qkw, vw, "plain", 1.0, BF16, "all"),
                 (qkw, qkw, "plain", 1.0, F32, "ctx"), (2 * qkw, vw, "plain", 1.0, F32, "ctx")],
                rope)
            lam_params = jnp.stack([da_lambda_q1[j], da_lambda_k1[j], da_lambda_q2[j], da_lambda_k2[j]])
            mix = [diff_attention(q, k, v, 0, b_ctx, n_ctx, lam_params, da_subln_g[j], lam_init,
                                  heads_per_step=DA_HEADS),
                   diff_attention(q, k, v, lay.t_ctx, b_dec, n_dec, lam_params, da_subln_g[j], lam_init,
                                  cache=(ck_all[:, j], cv_all[:, j]))]
            x = proj_residual(lay, x_parts, mix, da_w_o[j].astype(BF16), mods)
            new_k.append(k_ctx.reshape(b_ctx, n_ctx, DA_HEADS, 2, DA_HEAD_DIM))
            new_v.append(v_ctx.reshape(b_ctx, n_ctx, DA_HEADS, DA_V_DIM))
        else:
            kd = ret_w_qkv.shape[2] // 4
            dv = 2 * kd
            w_all = jnp.concatenate([ret_w_qkv[j], ret_w_gate_fwd[j], ret_w_gate_bwd[j]], axis=-1).astype(BF16)
            q, k, v, gates = norm_mod_matmul(
                lay, x_parts, norm_mix_g[i], mods, w_all,
                [(0, kd, "rope256", 1.0, BF16, "all"),
                 (kd, kd, "rope256", (kd // RET_HEADS) ** -0.5, F32, "all"),
                 (2 * kd, dv, "plain", 1.0, BF16, "all"), (2 * kd + dv, 2 * dv, "silu", 1.0, BF16, "all")],
                _rope_angles(n_dec, kd // RET_HEADS))
            o_ctx, sf, sb = retention(q, k, v, gates, 0, b_ctx, n_ctx, ret_log_decay[j], emit_state=True)
            (o_dec,) = retention(q, k, v, gates, lay.t_ctx, b_dec, n_dec, ret_log_decay[j],
                                 state=(state_ret_fwd[:, j:j + 1], state_ret_bwd[:, j:j + 1]))
            x = proj_residual(lay, x_parts, [o_ctx, o_dec], ret_w_o[j].astype(BF16), mods)
            new_sf.append(sf)
            new_sb.append(sb)
        x_parts = group_moe_residual(lay, x, mods, p, i, last=(i == depth - 1))

    y_ctx, y_dec = x_parts
    return (y_ctx.reshape(b_ctx, n_ctx, d), y_dec.reshape(b_dec, n_dec, d),
            jnp.stack(new_k, axis=1), jnp.stack(new_v, axis=1),
            jnp.concatenate(new_sf, axis=1), jnp.concatenate(new_sb, axis=1))
```

```python
import functools
import math

import jax
import jax.numpy as jnp
from jax import lax
from jax.experimental import pallas as pl
from jax.experimental.pallas import tpu as pltpu
from jax.experimental.pallas import tpu_sc as plsc

F32 = jnp.float32
BF16 = jnp.bfloat16
I32 = jnp.int32

GRID_W = 64
ROPE_BASE = 10000.0
NORM_EPS = 1e-6
DA_HEADS = 8
DA_HEAD_DIM = 64
DA_V_DIM = 2 * DA_HEAD_DIM
RET_HEADS = 4
RET_CHUNK = 128
N_EXPERTS = 16
N_GROUPS = 4
EXPERTS_PER_GROUP = N_EXPERTS // N_GROUPS
TOP_K = 2
N_MOD = 6
MOD_ROWS = 16
LANES = 128

V7X_VMEM_LIMIT = 56 * 1024 * 1024
V7X_SC_CORES = 2
V7X_SC_SUBCORES = 16
SC_WORKERS = V7X_SC_CORES * V7X_SC_SUBCORES
SC_CHUNK_ROWS = 32

ROW_TILE = 512
EXPERT_TILE = 512


def _cparams(*sem):
    return pltpu.CompilerParams(dimension_semantics=sem, vmem_limit_bytes=V7X_VMEM_LIMIT)


def _nt_dot(a, b):
    return lax.dot_general(a, b, (((1,), (1,)), ((), ())), preferred_element_type=F32)


def _tn_dot(a, b):
    return lax.dot_general(a, b, (((0,), (0,)), ((), ())), preferred_element_type=F32)


class Layout:
    def __init__(self, b_ctx, n_ctx, b_dec, n_dec):
        self.b_ctx, self.n_ctx, self.b_dec, self.n_dec = b_ctx, n_ctx, b_dec, n_dec
        self.t_ctx, self.t_dec = b_ctx * n_ctx, b_dec * n_dec
        self.t = self.t_ctx + self.t_dec
        self.tm = min(ROW_TILE, n_dec, self.t_ctx)
        assert self.t_ctx % self.tm == 0 and n_dec % self.tm == 0
        assert self.t_ctx % n_dec == 0 and self.t_ctx % n_ctx == 0
        self.n_ctx_tiles = self.t_ctx // self.tm
        self.n_tiles = self.t // self.tm

    def mod_row(self, i):
        r = i * self.tm
        return jnp.where(r < self.t_ctx, 0, 1 + (r - self.t_ctx) // self.n_dec)

    def part_tile(self, part, i):
        if part == 0:
            return jnp.minimum(i, self.n_ctx_tiles - 1)
        return jnp.maximum(i - self.n_ctx_tiles, 0)

    def row_specs(self, arrays, width, extra_grid_dims=0):
        pad = (0,) * 0
        if len(arrays) == 1:
            return [pl.BlockSpec((self.tm, width), lambda i, *_: (i, 0))]
        return [pl.BlockSpec((self.tm, width), lambda i, *_, p=p: (self.part_tile(p, i), 0)) for p in (0, 1)]


def _ada_kernel(c_ref, w_ref, b_ref, o_ref):
    s = jax.nn.silu(c_ref[...]).astype(BF16)
    acc = jnp.dot(s, w_ref[0].astype(BF16), preferred_element_type=F32)
    o_ref[0] = acc + b_ref[0]


def ada_modulation(cond, w_ada, b_ada):
    depth, d, n = w_ada.shape
    tn = 1536
    out = pl.pallas_call(
        _ada_kernel,
        out_shape=jax.ShapeDtypeStruct((depth, MOD_ROWS, n), F32),
        grid=(depth, n // tn),
        in_specs=[
            pl.BlockSpec((MOD_ROWS, d), lambda l, j: (0, 0)),
            pl.BlockSpec((1, d, tn), lambda l, j: (l, 0, j)),
            pl.BlockSpec((1, 1, tn), lambda l, j: (l, 0, j)),
        ],
        out_specs=pl.BlockSpec((1, MOD_ROWS, tn), lambda l, j: (l, 0, j)),
        compiler_params=_cparams("parallel", "parallel"),
        name="ada_modulation",
    )(cond, w_ada, b_ada.reshape(depth, 1, n))
    return out.reshape(depth, MOD_ROWS, N_MOD, d)


def _modulated_norm(x, g, mod_ref, shift_idx):
    ms = jnp.mean(x * x, axis=-1, keepdims=True)
    y = x * lax.rsqrt(ms + NORM_EPS) * g
    return y * (1.0 + mod_ref[0, shift_idx + 1:shift_idx + 2, :]) + mod_ref[0, shift_idx:shift_idx + 1, :]


def _by_part(is_dec, refs, fn):
    if len(refs) == 1:
        fn(refs[0])
        return
    pl.when(jnp.logical_not(is_dec))(lambda: fn(refs[0]))
    pl.when(is_dec)(lambda: fn(refs[1]))


def _rope64(a, cos, sin_signed, first_half):
    partner = jnp.where(first_half, pltpu.roll(a, 96, 1), pltpu.roll(a, 32, 1))
    return a * cos + partner * sin_signed


def _nmm_kernel(*refs, n_x, outs, tn, n_ctx_tiles):
    x_refs = refs[:n_x]
    g_ref, mod_ref, w_ref, cos_ref, sin_ref = refs[n_x:n_x + 5]
    out_refs = refs[n_x + 5:]
    is_dec = pl.program_id(0) >= n_ctx_tiles
    tm = x_refs[0].shape[0]
    segments = sorted({(col0, width) for col0, width, _, _, _ in outs})

    def rotated(a, kind):
        cos, sin = cos_ref[...], sin_ref[...]
        if kind == "rope64":
            lane = lax.broadcasted_iota(I32, (tm, LANES), 1)
            first_half = (lane & 32) == 0
            return [(c * 128, _rope64(a[:, c * 128:(c + 1) * 128], cos, sin, first_half))
                    for c in range(tn // 128)]
        pieces = []
        for c in range(tn // 256):
            x1, x2 = a[:, c * 256:c * 256 + 128], a[:, c * 256 + 128:(c + 1) * 256]
            pieces += [(c * 256, x1 * cos - x2 * sin), (c * 256 + 128, x2 * cos + x1 * sin)]
        return pieces

    def emit(dec):
        x_ref = x_refs[(1 if dec else 0) if n_x == 2 else 0]
        h = _modulated_norm(x_ref[...], g_ref[...], mod_ref, 0).astype(BF16)
        for col0, width in segments:
            sinks = [(o_ref, o) for o_ref, o in zip(out_refs, outs)
                     if (o[0], o[1]) == (col0, width) and not (dec and o[4] == "ctx")]
            for blk in range(width // tn):
                acc = jnp.dot(h, w_ref[:, col0 + blk * tn:col0 + (blk + 1) * tn], preferred_element_type=F32)
                for o_ref, (_, _, kind, scale, rows) in sinks:
                    a = acc if scale == 1.0 else acc * scale
                    if kind == "silu":
                        o_ref[:, blk * tn:(blk + 1) * tn] = jax.nn.silu(a).astype(o_ref.dtype)
                    elif kind == "plain" or rows == "ctx" or not dec:
                        o_ref[:, blk * tn:(blk + 1) * tn] = a.astype(o_ref.dtype)
                    else:
                        for off, val in rotated(a, kind):
                            o_ref[:, blk * tn + off:blk * tn + off + 128] = val.astype(o_ref.dtype)

    pl.when(jnp.logical_not(is_dec))(lambda: emit(False))
    pl.when(is_dec)(lambda: emit(True))


def norm_mod_matmul(lay, x_parts, g, mods, w, outputs, rope_tables, tn=512):
    d = x_parts[0].shape[1]
    tm = lay.tm
    n_total = w.shape[1]
    outs, out_shapes, out_specs = [], [], []
    for col0, width, kind, scale, dtype, rows in outputs:
        assert width % tn == 0 and col0 % LANES == 0
        outs.append((col0, width, kind, float(scale), rows))
        n_rows = lay.t_ctx if rows == "ctx" else lay.t
        out_shapes.append(jax.ShapeDtypeStruct((n_rows, width), dtype))
        out_specs.append(lay.row_specs([None, None], width)[0] if rows == "ctx" else lay.row_specs([None], width)[0])
    blocks_per_seq = lay.n_dec // tm
    rope_spec = pl.BlockSpec((tm, LANES), lambda i: (lay.part_tile(1, i) % blocks_per_seq, 0))
    return pl.pallas_call(
        functools.partial(_nmm_kernel, n_x=len(x_parts), outs=tuple(outs), tn=tn, n_ctx_tiles=lay.n_ctx_tiles),
        out_shape=out_shapes,
        grid=(lay.n_tiles,),
        in_specs=lay.row_specs(x_parts, d) + [
            pl.BlockSpec((1, d), lambda i: (0, 0)),
            pl.BlockSpec((1, N_MOD, d), lambda i: (lay.mod_row(i), 0, 0)),
            pl.BlockSpec((d, n_total), lambda i: (0, 0), pipeline_mode=pl.Buffered(1)),
            rope_spec, rope_spec,
        ],
        out_specs=out_specs,
        compiler_params=_cparams("arbitrary"),
        name="norm_mod_matmul",
    )(*x_parts, g.reshape(1, d), mods, w, *rope_tables)


def _diff_lambda(lam_ref, lam_init):
    lp = lam_ref[...]
    return (jnp.exp(jnp.sum(lp[0:1] * lp[1:2], axis=-1, keepdims=True))
            - jnp.exp(jnp.sum(lp[2:3] * lp[3:4], axis=-1, keepdims=True)) + lam_init)


def _component_scores(q, k):
    lane = lax.broadcasted_iota(I32, q.shape, 1)
    zero = jnp.zeros_like(q)
    return [_nt_dot(jnp.where((lane < DA_HEAD_DIM) == (comp == 0), q, zero), k) for comp in range(2)]


def _diff_softmax_values(scores, lam, v, subln_g, lam_init):
    exps = []
    for s in scores:
        e = jnp.exp2(s - jnp.max(s, axis=-1, keepdims=True))
        exps.append((e, 1.0 / jnp.sum(e, axis=-1, keepdims=True)))
    c0, c1 = exps[0][1], lam * exps[1][1]
    o = jnp.dot((exps[0][0] * c0 - exps[1][0] * c1).astype(BF16), v, preferred_element_type=F32)
    ms = jnp.mean(o * o, axis=-1, keepdims=True)
    return (o * lax.rsqrt(ms + NORM_EPS) * subln_g) * (1.0 - lam_init)


def _diff_attn_kernel(lam_ref, q_ref, k_ref, v_ref, g_ref, o_ref, *, lam_init, heads_per_step):
    hw = 2 * DA_HEAD_DIM
    lam = _diff_lambda(lam_ref, lam_init)
    for hh in range(heads_per_step):
        cols = slice(hh * hw, (hh + 1) * hw)
        scores = _component_scores(q_ref[:, cols], k_ref[:, cols])
        o_ref[:, cols] = _diff_softmax_values(scores, lam, v_ref[:, cols], g_ref[...], lam_init)


def _diff_attn_cached_kernel(lam_ref, q_ref, k_ref, v_ref, ck_ref, cv_ref, g_ref, o_ref, k_scr, vt_scr, s_scr, *,
                             lam_init, sub_rows):
    seq_len, hw = q_ref.shape
    n_sub = seq_len // sub_rows
    assert n_sub % 2 == 0
    lam = _diff_lambda(lam_ref, lam_init)
    k_scr[0:seq_len, :] = k_ref[...]
    k_scr[seq_len:, :] = ck_ref[0].astype(BF16)
    vt_scr[0:hw, 0:seq_len] = jnp.transpose(v_ref[...].astype(F32)).astype(BF16)
    vt_scr[0:hw, seq_len:] = jnp.transpose(cv_ref[0]).astype(BF16)
    vt_scr[hw:, :] = jnp.ones((vt_scr.shape[0] - hw, vt_scr.shape[1]), BF16)

    def rows_of(t):
        return pl.ds(pl.multiple_of(t * sub_rows, sub_rows), sub_rows)

    def scores(t, slot):
        q = q_ref[rows_of(t), :]
        lane = lax.broadcasted_iota(I32, q.shape, 1)
        zero = jnp.zeros_like(q)
        for comp in range(2):
            s_scr[slot, comp] = _nt_dot(k_scr[...], jnp.where((lane < DA_HEAD_DIM) == (comp == 0), q, zero))

    def outputs(t, slot):
        parts = []
        for comp in range(2):
            s = s_scr[slot, comp]
            e = jnp.exp2(s - jnp.max(s, axis=0, keepdims=True)).astype(BF16)
            acc = jnp.dot(vt_scr[...], e, preferred_element_type=F32)
            parts.append(acc[0:hw] / acc[hw:hw + 1])
        o = parts[0] - lam * parts[1]
        ms = jnp.mean(o * o, axis=0, keepdims=True)
        o_ref[rows_of(t), :] = jnp.transpose((o * lax.rsqrt(ms + NORM_EPS) * g_ref[...]) * (1.0 - lam_init))

    scores(0, 0)

    def body(i2, carry):
        t = 2 * i2
        scores(t + 1, 1)
        outputs(t, 0)
        scores(jnp.minimum(t + 2, n_sub - 1), 0)
        outputs(t + 1, 1)
        return carry

    lax.fori_loop(0, n_sub // 2, body, 0)


def diff_attention(q, k, v, row0, batch, seq_len, lam_params, subln_g, lam_init, cache=None, heads_per_step=1,
                   sub_rows=256):
    width = q.shape[1]
    hw = 2 * DA_HEAD_DIM
    bw = heads_per_step * hw
    assert row0 % seq_len == 0 and DA_HEADS % heads_per_step == 0
    s0 = row0 // seq_len
    seq_spec = pl.BlockSpec((seq_len, bw), lambda b, h: (s0 + b, h))
    in_specs = [pl.BlockSpec((4, DA_HEAD_DIM), lambda b, h: (0, 0)), seq_spec, seq_spec, seq_spec]
    args = [lam_params, q, k, v]
    scratch = []
    if cache is None:
        body = functools.partial(_diff_attn_kernel, lam_init=lam_init, heads_per_step=heads_per_step)
    else:
        assert heads_per_step == 1
        past = cache[0].shape[1]
        cache_spec = pl.BlockSpec((1, past, hw), lambda b, h: (b, 0, h))
        in_specs += [cache_spec, cache_spec]
        args += list(cache)
        sub_rows = min(sub_rows, seq_len // 2)
        ones_rows = 16
        scratch = [pltpu.VMEM((seq_len + past, hw), BF16), pltpu.VMEM((hw + ones_rows, seq_len + past), BF16),
                   pltpu.VMEM((2, 2, seq_len + past, sub_rows), F32)]
        body = functools.partial(_diff_attn_cached_kernel, lam_init=lam_init, sub_rows=sub_rows)
        in_specs.append(pl.BlockSpec((hw, 1), lambda b, h: (0, 0)))
        args.append(subln_g.reshape(hw, 1))
    if cache is None:
        in_specs.append(pl.BlockSpec((1, hw), lambda b, h: (0, 0)))
        args.append(subln_g.reshape(1, hw))
    return pl.pallas_call(
        body,
        out_shape=jax.ShapeDtypeStruct((batch * seq_len, width), F32),
        grid=(batch, DA_HEADS // heads_per_step),
        in_specs=in_specs,
        out_specs=pl.BlockSpec((seq_len, bw), lambda b, h: (b, h)),
        scratch_shapes=scratch,
        compiler_params=_cparams("parallel", "parallel"),
        name="diff_attention",
    )(*args)


def _retention_kernel(*refs, seq_len, seqs, has_state, emit_state):
    lg_ref, q_ref, k_ref, v_ref, gf_ref, gb_ref = refs[:6]
    pos = 6
    if has_state:
        s0_refs = refs[6:8]
        pos = 8
    o_ref = refs[pos]
    pos += 1
    if emit_state:
        s_out_refs = refs[pos:pos + 2]
        pos += 2
    s_scr = refs[pos]

    h = pl.program_id(1)
    c_len = RET_CHUNK
    n_chunks = seq_len // c_len
    assert n_chunks % 2 == 0
    row = lax.broadcasted_iota(I32, (c_len, c_len), 0)
    colm = lax.broadcasted_iota(I32, (c_len, c_len), 1)
    rel = (row - colm).astype(F32)
    posv = lax.broadcasted_iota(I32, (c_len, 1), 0).astype(F32)

    consts = []
    for backward in (False, True):
        lg = lg_ref[1 if backward else 0, h]
        if backward:
            intra = jnp.where(rel <= 0, jnp.exp(-rel * lg), 0.0)
            q_decay = jnp.exp((c_len - posv) * lg)
            k_decay = jnp.exp(posv * lg)
        else:
            intra = jnp.where(rel >= 0, jnp.exp(rel * lg), 0.0)
            q_decay = jnp.exp((posv + 1.0) * lg)
            k_decay = jnp.exp((c_len - 1.0 - posv) * lg)
        consts.append((intra, q_decay, k_decay, jnp.exp(jnp.zeros((1, 1), F32) + c_len * lg)))
        direction = 1 if backward else 0
        for sq in range(seqs):
            if has_state:
                s_scr[sq, direction] = s0_refs[direction][sq, 0, 0]
            else:
                s_scr[sq, direction] = jnp.zeros(s_scr.shape[2:], F32)

    def chunk(sq, c, backward, first_touch):
        direction = 1 if backward else 0
        intra, q_decay, k_decay, chunk_decay = consts[direction]
        gate_ref = gb_ref if backward else gf_ref
        rows = pl.ds(pl.multiple_of(sq * seq_len + c * c_len, c_len), c_len)
        qb = q_ref[rows, :].astype(BF16)
        kf = k_ref[rows, :].astype(F32)
        vb = v_ref[rows, :].astype(BF16)
        a = _nt_dot(qb, kf.astype(BF16)) * intra
        s = s_scr[sq, direction]
        o = (jnp.dot(a.astype(BF16), vb, preferred_element_type=F32)
             + jnp.dot(qb, s.astype(BF16), preferred_element_type=F32) * q_decay)
        s_scr[sq, direction] = chunk_decay * s + _tn_dot((kf * k_decay).astype(BF16), vb)
        mu = jnp.mean(o, axis=-1, keepdims=True)
        oc = o - mu
        var = jnp.mean(oc * oc, axis=-1, keepdims=True)
        gated = oc * lax.rsqrt(var + NORM_EPS) * gate_ref[rows, :].astype(F32)
        if first_touch:
            o_ref[rows, :] = gated
        else:
            o_ref[rows, :] = o_ref[rows, :] + gated

    def body(ci, first_touch):
        for sq in range(seqs):
            chunk(sq, ci, False, first_touch)
            chunk(sq, n_chunks - 1 - ci, True, first_touch)

    unroll = 4 if n_chunks % 8 == 0 else 1
    lax.fori_loop(0, n_chunks // 2, lambda ci, c: (body(ci, True), c)[1], 0, unroll=unroll)
    lax.fori_loop(n_chunks // 2, n_chunks, lambda ci, c: (body(ci, False), c)[1], 0, unroll=unroll)
    if emit_state:
        for d in range(2):
            for sq in range(seqs):
                s_out_refs[d][sq, 0, 0] = s_scr[sq, d]


def retention(q, k, v, gates, row0, batch, seq_len, log_decay, state=None, emit_state=False, seqs_per_step=1):
    dk = q.shape[1] // RET_HEADS
    dv = v.shape[1] // RET_HEADS
    seqs = seqs_per_step
    rows = seqs * seq_len
    assert row0 % rows == 0 and batch % seqs == 0
    s0 = row0 // rows
    in_specs = [
        pl.BlockSpec(memory_space=pltpu.SMEM),
        pl.BlockSpec((rows, dk), lambda b, h: (s0 + b, h)),
        pl.BlockSpec((rows, dk), lambda b, h: (s0 + b, h)),
        pl.BlockSpec((rows, dv), lambda b, h: (s0 + b, h)),
        pl.BlockSpec((rows, dv), lambda b, h: (s0 + b, h)),
        pl.BlockSpec((rows, dv), lambda b, h: (s0 + b, RET_HEADS + h)),
    ]
    args = [log_decay, q, k, v, gates, gates]
    state_spec = pl.BlockSpec((seqs, 1, 1, dk, dv), lambda b, h: (b, 0, h, 0, 0))
    if state is not None:
        in_specs += [state_spec, state_spec]
        args += list(state)
    out_shapes = [jax.ShapeDtypeStruct((batch * seq_len, RET_HEADS * dv), F32)]
    out_specs = [pl.BlockSpec((rows, dv), lambda b, h: (b, h))]
    if emit_state:
        out_shapes += [jax.ShapeDtypeStruct((batch, 1, RET_HEADS, dk, dv), F32)] * 2
        out_specs += [state_spec, state_spec]
    return pl.pallas_call(
        functools.partial(_retention_kernel, seq_len=seq_len, seqs=seqs, has_state=state is not None,
                          emit_state=emit_state),
        out_shape=out_shapes,
        grid=(batch // seqs, RET_HEADS),
        in_specs=in_specs,
        out_specs=out_specs,
        scratch_shapes=[pltpu.VMEM((seqs, 2, dk, dv), F32)],
        compiler_params=_cparams("parallel", "parallel"),
        name="retention",
    )(*args)


def _pack_bf16_halves(x):
    half = x.shape[1] // 2
    lo = lax.bitcast_convert_type(x[:, :half].astype(F32), jnp.uint32)
    hi = lax.bitcast_convert_type(x[:, half:].astype(F32), jnp.uint32)
    return (hi & jnp.uint32(0xFFFF0000)) | (lo >> 16)


def _unpack_bf16_halves(p):
    lo = lax.bitcast_convert_type(p << 16, F32).astype(BF16)
    hi = lax.bitcast_convert_type(p & jnp.uint32(0xFFFF0000), F32).astype(BF16)
    return lo, hi


def _first_max_onehot(vals):
    m = vals[0]
    for v in vals[1:]:
        m = jnp.maximum(m, v)
    onehot, taken = [], None
    for v in vals:
        hit = v == m
        if taken is None:
            onehot.append(hit)
            taken = hit
        else:
            onehot.append(hit & jnp.logical_not(taken))
            taken = taken | hit
    return m, onehot


def _pick(onehot, vals):
    out = vals[-1]
    for oh, v in zip(onehot[-2::-1], vals[-2::-1]):
        out = jnp.where(oh, v, out)
    return out


def _route_rows(x, g_ref, mod_ref, wr_ref, bias_ref, tri_ref, h_ref, idx_ref, wcol_ref, rank_ref, cnt_ref,
                ind_scr, wrow_scr):
    hb = _modulated_norm(x, g_ref[...], mod_ref, 3).astype(BF16)
    h_ref[...] = _pack_bf16_halves(hb)
    logits = _nt_dot(wr_ref[...], hb)
    s = jax.nn.sigmoid(logits)
    sel = s + bias_ref[...]
    neg_inf = jnp.full_like(sel[0:1], -jnp.inf)
    sel_rows = [sel[e:e + 1] for e in range(N_EXPERTS)]
    s_rows = [s[e:e + 1] for e in range(N_EXPERTS)]

    def top2(vals):
        m1, oh1 = _first_max_onehot(vals)
        rest = [jnp.where(o, neg_inf, v) for o, v in zip(oh1, vals)]
        m2, oh2 = _first_max_onehot(rest)
        return m1, m2, oh1, oh2

    grp_scores = []
    for g in range(N_GROUPS):
        m1, m2, _, _ = top2(sel_rows[g * EXPERTS_PER_GROUP:(g + 1) * EXPERTS_PER_GROUP])
        grp_scores.append(m1 + m2)
    _, in_grp = _first_max_onehot(grp_scores)
    cand_sel = [_pick(in_grp, [sel_rows[g * EXPERTS_PER_GROUP + k] for g in range(N_GROUPS)])
                for k in range(EXPERTS_PER_GROUP)]
    cand_s = [_pick(in_grp, [s_rows[g * EXPERTS_PER_GROUP + k] for g in range(N_GROUPS)])
              for k in range(EXPERTS_PER_GROUP)]
    _, _, oh1, oh2 = top2(cand_sel)
    w1 = _pick(oh1, cand_s)
    w2 = _pick(oh2, cand_s)
    denom = w1 + w2
    wrow_scr[...] = jnp.zeros_like(wrow_scr)
    wrow_scr[0:1, :] = w1 / denom
    wrow_scr[1:2, :] = w2 / denom
    wcol_ref[...] = jnp.transpose(wrow_scr[...])

    ints = [jnp.full(w1.shape, k, I32) for k in range(EXPERTS_PER_GROUP)]
    grp = _pick(in_grp, ints) * EXPERTS_PER_GROUP
    idx_ref[0:1, :] = grp + _pick(oh1, ints)
    idx_ref[1:2, :] = grp + _pick(oh2, ints)

    one, zero = jnp.ones_like(w1), jnp.zeros_like(w1)
    for g in range(N_GROUPS):
        for k in range(EXPERTS_PER_GROUP):
            e = g * EXPERTS_PER_GROUP + k
            ind_scr[e:e + 1, :] = jnp.where(in_grp[g] & (oh1[k] | oh2[k]), one, zero)
    ind = ind_scr[...].astype(BF16)
    ranks = jnp.dot(ind, tri_ref[...], preferred_element_type=F32)
    cnt_ref[0] = jnp.dot(ind, jnp.ones((ind.shape[1], LANES), BF16), preferred_element_type=F32)
    for slot, oh in ((0, oh1), (1, oh2)):
        r = zero
        for g in range(N_GROUPS):
            for k in range(EXPERTS_PER_GROUP):
                e = g * EXPERTS_PER_GROUP + k
                r = r + jnp.where(in_grp[g] & oh[k], ranks[e:e + 1], zero)
        rank_ref[slot:slot + 1, :] = r.astype(I32)


def _proj_route_kernel(*refs, n_x, n_a, n_ctx_tiles):
    x_refs, a_refs = refs[:n_x], refs[n_x:n_x + n_a]
    w_ref, mod_ref, g_ref, wr_ref, bias_ref, tri_ref, o_ref = refs[n_x + n_a:n_x + n_a + 7]
    route_out_refs = refs[n_x + n_a + 7:n_x + n_a + 12]
    y_scr, ind_scr, wrow_scr = refs[n_x + n_a + 12:]
    is_dec = pl.program_id(0) >= n_ctx_tiles

    def project(a_ref):
        y_scr[...] = mod_ref[0, 2:3, :] * jnp.dot(a_ref[...].astype(BF16), w_ref[...],
                                                  preferred_element_type=F32)

    def add(x_ref):
        o_ref[...] = x_ref[...] + y_scr[...]

    _by_part(is_dec, a_refs, project)
    _by_part(is_dec, x_refs, add)
    _route_rows(o_ref[...], g_ref, mod_ref, wr_ref, bias_ref, tri_ref, *route_out_refs, ind_scr, wrow_scr)


def proj_residual_route(lay, x_parts, a_parts, w, mods, ffn_g, w_router_t, router_bias):
    d = x_parts[0].shape[1]
    kdim = a_parts[0].shape[1]
    tm, t = lay.tm, lay.t
    tri = jnp.triu(jnp.ones((tm, tm), BF16), k=1)
    pair = jax.ShapeDtypeStruct((TOP_K, t), I32)
    pair_spec = pl.BlockSpec((TOP_K, tm), lambda i: (0, i))
    const = lambda shape: pl.BlockSpec(shape, lambda i: (0,) * len(shape))
    return pl.pallas_call(
        functools.partial(_proj_route_kernel, n_x=len(x_parts), n_a=len(a_parts), n_ctx_tiles=lay.n_ctx_tiles),
        out_shape=[jax.ShapeDtypeStruct((t, d), F32), jax.ShapeDtypeStruct((t, d // 2), jnp.uint32), pair,
                   jax.ShapeDtypeStruct((t, LANES), F32), pair,
                   jax.ShapeDtypeStruct((lay.n_tiles, N_EXPERTS, LANES), F32)],
        grid=(lay.n_tiles,),
        in_specs=lay.row_specs(x_parts, d) + lay.row_specs(a_parts, kdim) + [
            const((kdim, d)),
            pl.BlockSpec((1, N_MOD, d), lambda i: (lay.mod_row(i), 0, 0)),
            const((1, d)), const((N_EXPERTS, d)), const((N_EXPERTS, 1)), const((tm, tm)),
        ],
        out_specs=[pl.BlockSpec((tm, d), lambda i: (i, 0)), pl.BlockSpec((tm, d // 2), lambda i: (i, 0)), pair_spec,
                   pl.BlockSpec((tm, LANES), lambda i: (i, 0)), pair_spec,
                   pl.BlockSpec((1, N_EXPERTS, LANES), lambda i: (i, 0, 0))],
        scratch_shapes=[pltpu.VMEM((tm, d), F32), pltpu.VMEM((N_EXPERTS, tm), F32), pltpu.VMEM((LANES, tm), F32)],
        compiler_params=_cparams("arbitrary"),
        name="proj_residual_route",
    )(*x_parts, *a_parts, w, mods, ffn_g.reshape(1, d), w_router_t, router_bias.reshape(N_EXPERTS, 1), tri)


def dispatch_plan(lay, idx, rank, cnt, n_sorted):
    tm = lay.tm
    cnt_tile = cnt[:, :, 0].astype(I32)
    total = jnp.sum(cnt_tile, axis=0)
    padded = ((total + EXPERT_TILE - 1) // EXPERT_TILE) * EXPERT_TILE
    end = jnp.cumsum(padded)
    start = end - padded
    base = start[None, :] + jnp.cumsum(cnt_tile, axis=0) - cnt_tile
    idx3 = idx.reshape(TOP_K, lay.n_tiles, tm)
    pos = rank.reshape(TOP_K, lay.n_tiles, tm)
    for e in range(N_EXPERTS):
        pos = pos + jnp.where(idx3 == e, base[None, :, e, None], 0)
    tile_row = jnp.arange(n_sorted // EXPERT_TILE, dtype=I32) * EXPERT_TILE
    tile_expert = jnp.minimum(jnp.sum(end[None, :] <= tile_row[:, None], axis=1), N_EXPERTS - 1).astype(I32)
    n_valid = (end[-1] // EXPERT_TILE).astype(I32).reshape(1)
    return pos.reshape(TOP_K * lay.t).astype(I32), tile_expert, n_valid


def _sc_mesh():
    return plsc.VectorSubcoreMesh(core_axis_name="c", subcore_axis_name="s")


def _sc_worker_id():
    return lax.axis_index("s") * V7X_SC_CORES + lax.axis_index("c")


def sc_scatter_rows(src, pos, n_out_rows):
    t, d = src.shape
    n_idx = pos.shape[0]
    per_worker = n_idx // SC_WORKERS
    chunk = SC_CHUNK_ROWS
    assert n_idx % (SC_WORKERS * chunk) == 0 and t % chunk == 0

    @functools.partial(
        pl.kernel, mesh=_sc_mesh(), out_type=jax.ShapeDtypeStruct((n_out_rows, d), src.dtype),
        scratch_types=[pltpu.VMEM((chunk,), I32), pltpu.VMEM((chunk, d), src.dtype)],
        name="sc_scatter_rows")
    def scatter(src_hbm, pos_hbm, out_hbm, idx_v, rows_v):
        base = _sc_worker_id() * per_worker

        @pl.loop(0, per_worker // chunk)
        def _(i):
            a0 = base + i * chunk
            pltpu.sync_copy(pos_hbm.at[pl.ds(a0, chunk)], idx_v)
            pltpu.sync_copy(src_hbm.at[pl.ds(lax.rem(a0, t), chunk)], rows_v)
            pltpu.sync_copy(rows_v, out_hbm.at[idx_v])

    return scatter(src, pos)


def sc_gather_rows(table, idx):
    _, d = table.shape
    n_idx = idx.shape[0]
    per_worker = n_idx // SC_WORKERS
    chunk = SC_CHUNK_ROWS
    assert n_idx % (SC_WORKERS * chunk) == 0

    @functools.partial(
        pl.kernel, mesh=_sc_mesh(), out_type=jax.ShapeDtypeStruct((n_idx, d), table.dtype),
        scratch_types=[pltpu.VMEM((chunk,), I32), pltpu.VMEM((chunk, d), table.dtype)],
        name="sc_gather_rows")
    def gather(table_hbm, idx_hbm, out_hbm, idx_v, rows_v):
        base = _sc_worker_id() * per_worker

        @pl.loop(0, per_worker // chunk)
        def _(i):
            a0 = base + i * chunk
            pltpu.sync_copy(idx_hbm.at[pl.ds(a0, chunk)], idx_v)
            pltpu.sync_copy(table_hbm.at[idx_v], rows_v)
            pltpu.sync_copy(rows_v, out_hbm.at[pl.ds(a0, chunk)])

    return gather(table, idx)


def _experts_kernel(te_ref, nv_ref, x_ref, wg_ref, wu_ref, wd_ref, y_ref, wg_scr, wu_scr, wd_scr):
    i = pl.program_id(0)

    @pl.when(i < nv_ref[0])
    def _():
        @pl.when((i == 0) | (te_ref[i] != te_ref[jnp.maximum(i - 1, 0)]))
        def _():
            wg_scr[...] = wg_ref[0, 0].astype(BF16)
            wu_scr[...] = wu_ref[0, 0].astype(BF16)
            wd_scr[...] = wd_ref[0, 0].astype(BF16)

        x_lo, x_hi = _unpack_bf16_halves(x_ref[...])
        half = x_lo.shape[1]

        def in_proj(w_scr):
            return (jnp.dot(x_lo, w_scr[0:half, :], preferred_element_type=F32)
                    + jnp.dot(x_hi, w_scr[half:, :], preferred_element_type=F32))

        a = jax.nn.silu(in_proj(wg_scr)) * in_proj(wu_scr)
        y = jnp.dot(a.astype(BF16), wd_scr[...], preferred_element_type=F32)
        y_ref[...] = _pack_bf16_halves(y.astype(BF16))


def grouped_experts(xs, tile_expert, n_valid, wg, wu, wd, layer):
    n_rows = xs.shape[0]
    d, de = wg.shape[-2:]
    tm = EXPERT_TILE
    row_map = lambda i, te, nv: (jnp.minimum(i, nv[0] - 1), 0)
    grid_spec = pltpu.PrefetchScalarGridSpec(
        num_scalar_prefetch=2,
        grid=(n_rows // tm,),
        in_specs=[
            pl.BlockSpec((tm, d // 2), row_map),
            pl.BlockSpec((1, 1, d, de), lambda i, te, nv: (layer, te[i], 0, 0)),
            pl.BlockSpec((1, 1, d, de), lambda i, te, nv: (layer, te[i], 0, 0)),
            pl.BlockSpec((1, 1, de, d), lambda i, te, nv: (layer, te[i], 0, 0)),
        ],
        out_specs=pl.BlockSpec((tm, d // 2), row_map),
        scratch_shapes=[pltpu.VMEM((d, de), BF16), pltpu.VMEM((d, de), BF16), pltpu.VMEM((de, d), BF16)],
    )
    return pl.pallas_call(
        _experts_kernel,
        out_shape=jax.ShapeDtypeStruct((n_rows, d // 2), jnp.uint32),
        grid_spec=grid_spec,
        compiler_params=_cparams("arbitrary"),
        name="grouped_experts",
    )(tile_expert, n_valid, xs, wg, wu, wd)


def _combine_kernel(x_ref, y_ref, w_ref, mod_ref, fg_ref, *o_refs, final_norm, n_ctx_tiles):
    w = w_ref[...]

    def expert_out(slot):
        lo, hi = _unpack_bf16_halves(y_ref[slot])
        return jnp.concatenate([lo.astype(F32), hi.astype(F32)], axis=1)

    y = w[:, 0:1] * expert_out(0) + w[:, 1:2] * expert_out(1)
    out = x_ref[...] + mod_ref[0, 5:6, :] * y
    if final_norm:
        ms = jnp.mean(out * out, axis=-1, keepdims=True)
        out = out * lax.rsqrt(ms + NORM_EPS) * fg_ref[...]

    def store(o_ref):
        o_ref[...] = out
    _by_part(pl.program_id(0) >= n_ctx_tiles, o_refs, store)


def combine_residual(lay, x, y_pair, w_col, mods, final_g, final_norm, split):
    t, d = x.shape
    tm = lay.tm
    if split:
        out_shape = [jax.ShapeDtypeStruct((lay.t_ctx, d), F32), jax.ShapeDtypeStruct((lay.t_dec, d), F32)]
        out_specs = lay.row_specs([None, None], d)
    else:
        out_shape = [jax.ShapeDtypeStruct((t, d), F32)]
        out_specs = lay.row_specs([None], d)
    return pl.pallas_call(
        functools.partial(_combine_kernel, final_norm=final_norm, n_ctx_tiles=lay.n_ctx_tiles),
        out_shape=out_shape,
        grid=(lay.n_tiles,),
        in_specs=[
            pl.BlockSpec((tm, d), lambda i: (i, 0)),
            pl.BlockSpec((TOP_K, tm, d // 2), lambda i: (0, i, 0)),
            pl.BlockSpec((tm, LANES), lambda i: (i, 0)),
            pl.BlockSpec((1, N_MOD, d), lambda i: (lay.mod_row(i), 0, 0)),
            pl.BlockSpec((1, d), lambda i: (0, 0)),
        ],
        out_specs=out_specs,
        compiler_params=_cparams("arbitrary"),
        name="combine_residual",
    )(x, y_pair, w_col, mods, final_g.reshape(1, d))


def group_moe_residual(lay, x, routing, mods, p, layer, last):
    t, d = x.shape
    h, idx, w_col, rank, cnt = routing
    n_assign = TOP_K * t
    row_quantum = SC_WORKERS * SC_CHUNK_ROWS
    n_sorted = n_assign + N_EXPERTS * (EXPERT_TILE - 1)
    n_sorted = -(-n_sorted // row_quantum) * row_quantum
    n_sorted = -(-n_sorted // EXPERT_TILE) * EXPERT_TILE
    pos, tile_expert, n_valid = dispatch_plan(lay, idx, rank, cnt, n_sorted)
    xs = sc_scatter_rows(h, pos, n_sorted)
    ys = grouped_experts(xs, tile_expert, n_valid, p["moe_w_gate"], p["moe_w_up"], p["moe_w_down"], layer)
    y_pair = sc_gather_rows(ys, pos).reshape(TOP_K, t, d // 2)
    return combine_residual(lay, x, y_pair, w_col, mods, p["final_norm_g"], final_norm=last, split=last)


def _rope_angles(n, d):
    n_rows = n // GRID_W
    row = jnp.repeat(jnp.arange(n_rows), GRID_W).astype(F32)
    col = jnp.tile(jnp.arange(GRID_W), n_rows).astype(F32)
    nf = d // 4
    freqs = jnp.power(ROPE_BASE, -jnp.arange(nf, dtype=F32) / nf)
    ang = jnp.concatenate([row[:, None] * freqs, col[:, None] * freqs], axis=-1)
    return jnp.cos(ang), jnp.sin(ang)


def kernel(x_prompt, x_sample, cache_attn_k, cache_attn_v, state_ret_fwd, state_ret_bwd, c, c_ctx, w_ada, b_ada, norm_mix_g, norm_ffn_g, final_norm_g, da_w_qkv, da_lambda_q1, da_lambda_k1, da_lambda_q2, da_lambda_k2, da_subln_g, da_w_o, ret_w_qkv, ret_w_gate_fwd, ret_w_gate_bwd, ret_decay_fwd, ret_decay_bwd, ret_w_o, w_router, router_bias, moe_w_gate, moe_w_up, moe_w_down):
    b_ctx, n_ctx, d = x_prompt.shape
    b_dec, n_dec, _ = x_sample.shape
    past = cache_attn_k.shape[2]
    n_attn = cache_attn_k.shape[1]
    depth = w_ada.shape[0]
    assert b_dec + 1 <= MOD_ROWS
    lay = Layout(b_ctx, n_ctx, b_dec, n_dec)

    cond = jnp.zeros((MOD_ROWS, d), F32).at[0].set(c_ctx).at[1:1 + b_dec].set(c)
    mods_all = ada_modulation(cond, w_ada, b_ada)

    p = {
        "norm_ffn_g": norm_ffn_g, "final_norm_g": final_norm_g,
        "w_router_t": w_router.T.astype(BF16), "router_bias": router_bias.astype(F32),
        "moe_w_gate": moe_w_gate, "moe_w_up": moe_w_up, "moe_w_down": moe_w_down,
    }
    ret_log_decay = jnp.stack([jax.nn.log_sigmoid(ret_decay_fwd.astype(F32)),
                               jax.nn.log_sigmoid(ret_decay_bwd.astype(F32))], axis=1)
    ck_all = cache_attn_k.reshape(b_dec, n_attn, past, -1)
    cv_all = cache_attn_v.reshape(b_dec, n_attn, past, -1)

    x_parts = [x_prompt.reshape(lay.t_ctx, d), x_sample.reshape(lay.t_dec, d)]
    new_k, new_v, new_sf, new_sb = [], [], [], []
    for i in range(depth):
        mods = mods_all[i]
        j = i // 2
        if i % 2 == 0:
            lam_init = 0.8 - 0.6 * math.exp(-0.3 * i)
            qkw = DA_HEADS * 2 * DA_HEAD_DIM
            vw = DA_HEADS * DA_V_DIM
            cos, sin = _rope_angles(n_dec, DA_HEAD_DIM)
            rope = (jnp.tile(cos, (1, 4)), jnp.concatenate([-sin, sin, -sin, sin], axis=-1))
            q, k, v, k_ctx, v_ctx = norm_mod_matmul(
                lay, x_parts, norm_mix_g[i], mods, da_w_qkv[j].astype(BF16),
                [(0, qkw, "rope64", math.log2(math.e) * DA_HEAD_DIM ** -0.5, BF16, "all"),
                 (qkw, qkw, "rope64", 1.0, BF16, "all"),
                 (2 * qkw, vw, "plain", 1.0, BF16, "all"),
                 (qkw, qkw, "plain", 1.0, F32, "ctx"), (2 * qkw, vw, "plain", 1.0, F32, "ctx")],
                rope)
            lam_params = jnp.stack([da_lambda_q1[j], da_lambda_k1[j], da_lambda_q2[j], da_lambda_k2[j]])
            mix = [diff_attention(q, k, v, 0, b_ctx, n_ctx, lam_params, da_subln_g[j], lam_init,
                                  heads_per_step=DA_HEADS),
                   diff_attention(q, k, v, lay.t_ctx, b_dec, n_dec, lam_params, da_subln_g[j], lam_init,
                                  cache=(ck_all[:, j], cv_all[:, j]))]
            mix_w = da_w_o[j].astype(BF16)
            new_k.append(k_ctx.reshape(b_ctx, n_ctx, DA_HEADS, 2, DA_HEAD_DIM))
            new_v.append(v_ctx.reshape(b_ctx, n_ctx, DA_HEADS, DA_V_DIM))
        else:
            kd = ret_w_qkv.shape[2] // 4
            dv = 2 * kd
            w_all = jnp.concatenate([ret_w_qkv[j], ret_w_gate_fwd[j], ret_w_gate_bwd[j]], axis=-1).astype(BF16)
            q, k, v, gates = norm_mod_matmul(
                lay, x_parts, norm_mix_g[i], mods, w_all,
                [(0, kd, "rope256", 1.0, BF16, "all"),
                 (kd, kd, "rope256", (kd // RET_HEADS) ** -0.5, F32, "all"),
                 (2 * kd, dv, "plain", 1.0, BF16, "all"), (2 * kd + dv, 2 * dv, "silu", 1.0, BF16, "all")],
                _rope_angles(n_dec, kd // RET_HEADS))
            ctx_seqs = math.gcd(b_ctx, max(1, n_dec // n_ctx // 2))
            o_ctx, sf, sb = retention(q, k, v, gates, 0, b_ctx, n_ctx, ret_log_decay[j], emit_state=True,
                                      seqs_per_step=ctx_seqs)
            (o_dec,) = retention(q, k, v, gates, lay.t_ctx, b_dec, n_dec, ret_log_decay[j],
                                 state=(state_ret_fwd[:, j:j + 1], state_ret_bwd[:, j:j + 1]))
            mix, mix_w = [o_ctx, o_dec], ret_w_o[j].astype(BF16)
            new_sf.append(sf)
            new_sb.append(sb)
        x, *routing = proj_residual_route(lay, x_parts, mix, mix_w, mods, norm_ffn_g[i], p["w_router_t"],
                                          p["router_bias"])
        x_parts = group_moe_residual(lay, x, routing, mods, p, i, last=(i == depth - 1))

    y_ctx, y_dec = x_parts
    return (y_ctx.reshape(b_ctx, n_ctx, d), y_dec.reshape(b_dec, n_dec, d),
            jnp.stack(new_k, axis=1), jnp.stack(new_v, axis=1),
            jnp.concatenate(new_sf, axis=1), jnp.concatenate(new_sb, axis=1))
```

```python
import functools
import math

import jax
import jax.numpy as jnp
from jax import lax
from jax.experimental import pallas as pl
from jax.experimental.pallas import tpu as pltpu
from jax.experimental.pallas import tpu_sc as plsc

F32 = jnp.float32
BF16 = jnp.bfloat16
I32 = jnp.int32

GRID_W = 64
ROPE_BASE = 10000.0
NORM_EPS = 1e-6
DA_HEADS = 8
DA_HEAD_DIM = 64
DA_V_DIM = 2 * DA_HEAD_DIM
RET_HEADS = 4
RET_CHUNK = 128
N_EXPERTS = 16
N_GROUPS = 4
EXPERTS_PER_GROUP = N_EXPERTS // N_GROUPS
TOP_K = 2
N_MOD = 6
MOD_ROWS = 16
LANES = 128

V7X_VMEM_LIMIT = 56 * 1024 * 1024
V7X_SC_CORES = 2
V7X_SC_SUBCORES = 16
SC_WORKERS = V7X_SC_CORES * V7X_SC_SUBCORES
SC_CHUNK_ROWS = 32

ROW_TILE = 512
EXPERT_TILE = 512


def _cparams(*sem):
    return pltpu.CompilerParams(dimension_semantics=sem, vmem_limit_bytes=V7X_VMEM_LIMIT)


def _nt_dot(a, b):
    return lax.dot_general(a, b, (((1,), (1,)), ((), ())), preferred_element_type=F32)


def _tn_dot(a, b):
    return lax.dot_general(a, b, (((0,), (0,)), ((), ())), preferred_element_type=F32)


class Layout:
    def __init__(self, b_ctx, n_ctx, b_dec, n_dec):
        self.b_ctx, self.n_ctx, self.b_dec, self.n_dec = b_ctx, n_ctx, b_dec, n_dec
        self.t_ctx, self.t_dec = b_ctx * n_ctx, b_dec * n_dec
        self.t = self.t_ctx + self.t_dec
        self.tm = min(ROW_TILE, n_dec, self.t_ctx)
        assert self.t_ctx % self.tm == 0 and n_dec % self.tm == 0
        assert self.t_ctx % n_dec == 0 and self.t_ctx % n_ctx == 0
        self.n_ctx_tiles = self.t_ctx // self.tm
        self.n_tiles = self.t // self.tm

    def mod_row(self, i):
        r = i * self.tm
        return jnp.where(r < self.t_ctx, 0, 1 + (r - self.t_ctx) // self.n_dec)

    def part_tile(self, part, i):
        if part == 0:
            return jnp.minimum(i, self.n_ctx_tiles - 1)
        return jnp.maximum(i - self.n_ctx_tiles, 0)

    def row_specs(self, arrays, width, extra_grid_dims=0):
        pad = (0,) * 0
        if len(arrays) == 1:
            return [pl.BlockSpec((self.tm, width), lambda i, *_: (i, 0))]
        return [pl.BlockSpec((self.tm, width), lambda i, *_, p=p: (self.part_tile(p, i), 0)) for p in (0, 1)]


def _ada_kernel(c_ref, w_ref, b_ref, o_ref):
    s = jax.nn.silu(c_ref[...]).astype(BF16)
    acc = jnp.dot(s, w_ref[0].astype(BF16), preferred_element_type=F32)
    o_ref[0] = acc + b_ref[0]


def ada_modulation(cond, w_ada, b_ada):
    depth, d, n = w_ada.shape
    tn = 1536
    out = pl.pallas_call(
        _ada_kernel,
        out_shape=jax.ShapeDtypeStruct((depth, MOD_ROWS, n), F32),
        grid=(depth, n // tn),
        in_specs=[
            pl.BlockSpec((MOD_ROWS, d), lambda l, j: (0, 0)),
            pl.BlockSpec((1, d, tn), lambda l, j: (l, 0, j)),
            pl.BlockSpec((1, 1, tn), lambda l, j: (l, 0, j)),
        ],
        out_specs=pl.BlockSpec((1, MOD_ROWS, tn), lambda l, j: (l, 0, j)),
        compiler_params=_cparams("parallel", "parallel"),
        name="ada_modulation",
    )(cond, w_ada, b_ada.reshape(depth, 1, n))
    return out.reshape(depth, MOD_ROWS, N_MOD, d)


def _modulated_norm(x, g, mod_ref, shift_idx):
    ms = jnp.mean(x * x, axis=-1, keepdims=True)
    y = x * lax.rsqrt(ms + NORM_EPS) * g
    return y * (1.0 + mod_ref[0, shift_idx + 1:shift_idx + 2, :]) + mod_ref[0, shift_idx:shift_idx + 1, :]


def _by_part(is_dec, refs, fn):
    if len(refs) == 1:
        fn(refs[0])
        return
    pl.when(jnp.logical_not(is_dec))(lambda: fn(refs[0]))
    pl.when(is_dec)(lambda: fn(refs[1]))


def _rope64(a, cos, sin_signed, first_half):
    partner = jnp.where(first_half, pltpu.roll(a, 96, 1), pltpu.roll(a, 32, 1))
    return a * cos + partner * sin_signed


def _nmm_kernel(*refs, n_x, outs, tn, n_ctx_tiles):
    x_refs = refs[:n_x]
    g_ref, mod_ref, w_ref, cos_ref, sin_ref = refs[n_x:n_x + 5]
    out_refs = refs[n_x + 5:]
    is_dec = pl.program_id(0) >= n_ctx_tiles
    tm = x_refs[0].shape[0]
    segments = sorted({(col0, width) for col0, width, _, _, _ in outs})

    def rotated(a, kind):
        cos, sin = cos_ref[...], sin_ref[...]
        if kind == "rope64":
            lane = lax.broadcasted_iota(I32, (tm, LANES), 1)
            first_half = (lane & 32) == 0
            return [(c * 128, _rope64(a[:, c * 128:(c + 1) * 128], cos, sin, first_half))
                    for c in range(tn // 128)]
        pieces = []
        for c in range(tn // 256):
            x1, x2 = a[:, c * 256:c * 256 + 128], a[:, c * 256 + 128:(c + 1) * 256]
            pieces += [(c * 256, x1 * cos - x2 * sin), (c * 256 + 128, x2 * cos + x1 * sin)]
        return pieces

    def emit(dec):
        x_ref = x_refs[(1 if dec else 0) if n_x == 2 else 0]
        h = _modulated_norm(x_ref[...], g_ref[...], mod_ref, 0).astype(BF16)
        for col0, width in segments:
            sinks = [(o_ref, o) for o_ref, o in zip(out_refs, outs)
                     if (o[0], o[1]) == (col0, width) and not (dec and o[4] != "all")]
            for blk in range(width // tn):
                acc = jnp.dot(h, w_ref[:, col0 + blk * tn:col0 + (blk + 1) * tn], preferred_element_type=F32)
                for o_ref, (_, _, kind, scale, rows) in sinks:
                    a = acc if scale == 1.0 else acc * scale
                    if rows == "ctx64":
                        n64 = width // 64
                        for c in range(tn // 64):
                            o_ref[pl.ds(blk * tn // 64 + c, tm, stride=n64), :] = (
                                a[:, c * 64:(c + 1) * 64].astype(o_ref.dtype))
                    elif kind == "silu":
                        o_ref[:, blk * tn:(blk + 1) * tn] = jax.nn.silu(a).astype(o_ref.dtype)
                    elif kind == "plain" or rows == "ctx" or not dec:
                        o_ref[:, blk * tn:(blk + 1) * tn] = a.astype(o_ref.dtype)
                    else:
                        for off, val in rotated(a, kind):
                            o_ref[:, blk * tn + off:blk * tn + off + 128] = val.astype(o_ref.dtype)

    pl.when(jnp.logical_not(is_dec))(lambda: emit(False))
    pl.when(is_dec)(lambda: emit(True))


def norm_mod_matmul(lay, x_parts, g, mods, w, outputs, rope_tables, tn=512):
    d = x_parts[0].shape[1]
    tm = lay.tm
    n_total = w.shape[1]
    outs, out_shapes, out_specs = [], [], []
    for col0, width, kind, scale, dtype, rows in outputs:
        assert width % tn == 0 and col0 % LANES == 0
        outs.append((col0, width, kind, float(scale), rows))
        if rows == "ctx64":
            n64 = width // 64
            out_shapes.append(jax.ShapeDtypeStruct((lay.t_ctx * n64, 64), dtype))
            out_specs.append(pl.BlockSpec((tm * n64, 64), lambda i: (lay.part_tile(0, i), 0)))
            continue
        n_rows = lay.t_ctx if rows == "ctx" else lay.t
        out_shapes.append(jax.ShapeDtypeStruct((n_rows, width), dtype))
        out_specs.append(lay.row_specs([None, None], width)[0] if rows == "ctx" else lay.row_specs([None], width)[0])
    blocks_per_seq = lay.n_dec // tm
    rope_spec = pl.BlockSpec((tm, LANES), lambda i: (lay.part_tile(1, i) % blocks_per_seq, 0))
    return pl.pallas_call(
        functools.partial(_nmm_kernel, n_x=len(x_parts), outs=tuple(outs), tn=tn, n_ctx_tiles=lay.n_ctx_tiles),
        out_shape=out_shapes,
        grid=(lay.n_tiles,),
        in_specs=lay.row_specs(x_parts, d) + [
            pl.BlockSpec((1, d), lambda i: (0, 0)),
            pl.BlockSpec((1, N_MOD, d), lambda i: (lay.mod_row(i), 0, 0)),
            pl.BlockSpec((d, n_total), lambda i: (0, 0), pipeline_mode=pl.Buffered(1)),
            rope_spec, rope_spec,
        ],
        out_specs=out_specs,
        compiler_params=_cparams("arbitrary"),
        name="norm_mod_matmul",
    )(*x_parts, g.reshape(1, d), mods, w, *rope_tables)


def _diff_lambda(lam_ref, lam_init):
    lp = lam_ref[...]
    return (jnp.exp(jnp.sum(lp[0:1] * lp[1:2], axis=-1, keepdims=True))
            - jnp.exp(jnp.sum(lp[2:3] * lp[3:4], axis=-1, keepdims=True)) + lam_init)


def _component_scores(q, k):
    lane = lax.broadcasted_iota(I32, q.shape, 1)
    zero = jnp.zeros_like(q)
    return [_nt_dot(jnp.where((lane < DA_HEAD_DIM) == (comp == 0), q, zero), k) for comp in range(2)]


def _diff_softmax_values(scores, lam, v, subln_g, lam_init):
    exps = []
    for s in scores:
        e = jnp.exp2(s - jnp.max(s, axis=-1, keepdims=True))
        exps.append((e, 1.0 / jnp.sum(e, axis=-1, keepdims=True)))
    c0, c1 = exps[0][1], lam * exps[1][1]
    o = jnp.dot((exps[0][0] * c0 - exps[1][0] * c1).astype(BF16), v, preferred_element_type=F32)
    ms = jnp.mean(o * o, axis=-1, keepdims=True)
    return (o * lax.rsqrt(ms + NORM_EPS) * subln_g) * (1.0 - lam_init)


def _diff_attn_kernel(lam_ref, q_ref, k_ref, v_ref, g_ref, o_ref, *, lam_init, heads_per_step):
    hw = 2 * DA_HEAD_DIM
    lam = _diff_lambda(lam_ref, lam_init)
    for hh in range(heads_per_step):
        cols = slice(hh * hw, (hh + 1) * hw)
        scores = _component_scores(q_ref[:, cols], k_ref[:, cols])
        o_ref[:, cols] = _diff_softmax_values(scores, lam, v_ref[:, cols], g_ref[...], lam_init).astype(o_ref.dtype)


def _diff_attn_cached_kernel(lam_ref, q_ref, k_ref, v_ref, ck_ref, cv_ref, g_ref, o_ref, k_scr, vt_scr, s_scr, *,
                             lam_init, sub_rows):
    seq_len, hw = q_ref.shape
    n_sub = seq_len // sub_rows
    assert n_sub % 2 == 0
    lam = _diff_lambda(lam_ref, lam_init)
    k_scr[0:seq_len, :] = k_ref[...]
    k_scr[seq_len:, :] = ck_ref[0].astype(BF16)
    vt_scr[0:hw, 0:seq_len] = jnp.transpose(v_ref[...].astype(F32)).astype(BF16)
    vt_scr[0:hw, seq_len:] = jnp.transpose(cv_ref[0]).astype(BF16)
    vt_scr[hw:, :] = jnp.ones((vt_scr.shape[0] - hw, vt_scr.shape[1]), BF16)

    def rows_of(t):
        return pl.ds(pl.multiple_of(t * sub_rows, sub_rows), sub_rows)

    def scores(t, slot):
        q = q_ref[rows_of(t), :]
        lane = lax.broadcasted_iota(I32, q.shape, 1)
        zero = jnp.zeros_like(q)
        for comp in range(2):
            s_scr[slot, comp] = _nt_dot(k_scr[...], jnp.where((lane < DA_HEAD_DIM) == (comp == 0), q, zero))

    def outputs(t, slot):
        parts = []
        for comp in range(2):
            s = s_scr[slot, comp]
            e = jnp.exp2(s - jnp.max(s, axis=0, keepdims=True)).astype(BF16)
            acc = jnp.dot(vt_scr[...], e, preferred_element_type=F32)
            parts.append(acc[0:hw] / acc[hw:hw + 1])
        o = parts[0] - lam * parts[1]
        ms = jnp.mean(o * o, axis=0, keepdims=True)
        o = (o * lax.rsqrt(ms + NORM_EPS) * g_ref[...]) * (1.0 - lam_init)
        o_ref[rows_of(t), :] = jnp.transpose(o).astype(o_ref.dtype)

    scores(0, 0)

    def body(i2, carry):
        t = 2 * i2
        scores(t + 1, 1)
        outputs(t, 0)
        scores(jnp.minimum(t + 2, n_sub - 1), 0)
        outputs(t + 1, 1)
        return carry

    lax.fori_loop(0, n_sub // 2, body, 0)


def diff_attention(q, k, v, row0, batch, seq_len, lam_params, subln_g, lam_init, cache=None, heads_per_step=1,
                   sub_rows=256):
    width = q.shape[1]
    hw = 2 * DA_HEAD_DIM
    bw = heads_per_step * hw
    assert row0 % seq_len == 0 and DA_HEADS % heads_per_step == 0
    s0 = row0 // seq_len
    seq_spec = pl.BlockSpec((seq_len, bw), lambda b, h: (s0 + b, h))
    in_specs = [pl.BlockSpec((4, DA_HEAD_DIM), lambda b, h: (0, 0)), seq_spec, seq_spec, seq_spec]
    args = [lam_params, q, k, v]
    scratch = []
    if cache is None:
        body = functools.partial(_diff_attn_kernel, lam_init=lam_init, heads_per_step=heads_per_step)
    else:
        assert heads_per_step == 1
        past = cache[0].shape[1]
        cache_spec = pl.BlockSpec((1, past, hw), lambda b, h: (b, 0, h))
        in_specs += [cache_spec, cache_spec]
        args += list(cache)
        sub_rows = min(sub_rows, seq_len // 2)
        ones_rows = 16
        scratch = [pltpu.VMEM((seq_len + past, hw), BF16), pltpu.VMEM((hw + ones_rows, seq_len + past), BF16),
                   pltpu.VMEM((2, 2, seq_len + past, sub_rows), F32)]
        body = functools.partial(_diff_attn_cached_kernel, lam_init=lam_init, sub_rows=sub_rows)
        in_specs.append(pl.BlockSpec((hw, 1), lambda b, h: (0, 0)))
        args.append(subln_g.reshape(hw, 1))
    if cache is None:
        in_specs.append(pl.BlockSpec((1, hw), lambda b, h: (0, 0)))
        args.append(subln_g.reshape(1, hw))
    return pl.pallas_call(
        body,
        out_shape=jax.ShapeDtypeStruct((batch * seq_len, width), BF16),
        grid=(batch, DA_HEADS // heads_per_step),
        in_specs=in_specs,
        out_specs=pl.BlockSpec((seq_len, bw), lambda b, h: (b, h)),
        scratch_shapes=scratch,
        compiler_params=_cparams("parallel", "parallel"),
        name="diff_attention",
    )(*args)


def _retention_kernel(*refs, seq_len, seqs, has_state, emit_state):
    lg_ref, q_ref, k_ref, v_ref, gf_ref, gb_ref = refs[:6]
    pos = 6
    if has_state:
        s0_refs = refs[6:8]
        pos = 8
    o_ref = refs[pos]
    pos += 1
    if emit_state:
        s_out_refs = refs[pos:pos + 2]
        pos += 2
    s_scr, o_scr = refs[pos:pos + 2]

    h = pl.program_id(1)
    c_len = RET_CHUNK
    n_chunks = seq_len // c_len
    assert n_chunks % 2 == 0
    row = lax.broadcasted_iota(I32, (c_len, c_len), 0)
    colm = lax.broadcasted_iota(I32, (c_len, c_len), 1)
    rel = (row - colm).astype(F32)
    posv = lax.broadcasted_iota(I32, (c_len, 1), 0).astype(F32)

    consts = []
    for backward in (False, True):
        lg = lg_ref[1 if backward else 0, h]
        if backward:
            intra = jnp.where(rel <= 0, jnp.exp(-rel * lg), 0.0)
            q_decay = jnp.exp((c_len - posv) * lg)
            k_decay = jnp.exp(posv * lg)
        else:
            intra = jnp.where(rel >= 0, jnp.exp(rel * lg), 0.0)
            q_decay = jnp.exp((posv + 1.0) * lg)
            k_decay = jnp.exp((c_len - 1.0 - posv) * lg)
        consts.append((intra, q_decay, k_decay, jnp.exp(jnp.zeros((1, 1), F32) + c_len * lg)))
        direction = 1 if backward else 0
        for sq in range(seqs):
            if has_state:
                s_scr[sq, direction] = s0_refs[direction][sq, 0, 0]
            else:
                s_scr[sq, direction] = jnp.zeros(s_scr.shape[2:], F32)

    def chunk(sq, c, backward, first_touch):
        direction = 1 if backward else 0
        intra, q_decay, k_decay, chunk_decay = consts[direction]
        gate_ref = gb_ref if backward else gf_ref
        rows = pl.ds(pl.multiple_of(sq * seq_len + c * c_len, c_len), c_len)
        qb = q_ref[rows, :].astype(BF16)
        kf = k_ref[rows, :].astype(F32)
        vb = v_ref[rows, :].astype(BF16)
        a = _nt_dot(qb, kf.astype(BF16)) * intra
        s = s_scr[sq, direction]
        o = (jnp.dot(a.astype(BF16), vb, preferred_element_type=F32)
             + jnp.dot(qb, s.astype(BF16), preferred_element_type=F32) * q_decay)
        s_scr[sq, direction] = chunk_decay * s + _tn_dot((kf * k_decay).astype(BF16), vb)
        mu = jnp.mean(o, axis=-1, keepdims=True)
        oc = o - mu
        var = jnp.mean(oc * oc, axis=-1, keepdims=True)
        gated = oc * lax.rsqrt(var + NORM_EPS) * gate_ref[rows, :].astype(F32)
        if first_touch:
            o_scr[rows, :] = gated
        else:
            o_ref[rows, :] = (o_scr[rows, :] + gated).astype(o_ref.dtype)

    def body(ci, first_touch):
        for sq in range(seqs):
            chunk(sq, ci, False, first_touch)
            chunk(sq, n_chunks - 1 - ci, True, first_touch)

    unroll = 4 if n_chunks % 8 == 0 else 1
    lax.fori_loop(0, n_chunks // 2, lambda ci, c: (body(ci, True), c)[1], 0, unroll=unroll)
    lax.fori_loop(n_chunks // 2, n_chunks, lambda ci, c: (body(ci, False), c)[1], 0, unroll=unroll)
    if emit_state:
        for d in range(2):
            for sq in range(seqs):
                s_out_refs[d][sq, 0, 0] = s_scr[sq, d]


def retention(q, k, v, gates, row0, batch, seq_len, log_decay, state=None, emit_state=False, seqs_per_step=1):
    dk = q.shape[1] // RET_HEADS
    dv = v.shape[1] // RET_HEADS
    seqs = seqs_per_step
    rows = seqs * seq_len
    assert row0 % rows == 0 and batch % seqs == 0
    s0 = row0 // rows
    in_specs = [
        pl.BlockSpec(memory_space=pltpu.SMEM),
        pl.BlockSpec((rows, dk), lambda b, h: (s0 + b, h)),
        pl.BlockSpec((rows, dk), lambda b, h: (s0 + b, h)),
        pl.BlockSpec((rows, dv), lambda b, h: (s0 + b, h)),
        pl.BlockSpec((rows, dv), lambda b, h: (s0 + b, h)),
        pl.BlockSpec((rows, dv), lambda b, h: (s0 + b, RET_HEADS + h)),
    ]
    args = [log_decay, q, k, v, gates, gates]
    state_spec = pl.BlockSpec((seqs, 1, 1, dk, dv), lambda b, h: (b, 0, h, 0, 0))
    if state is not None:
        in_specs += [state_spec, state_spec]
        args += list(state)
    out_shapes = [jax.ShapeDtypeStruct((batch * seq_len, RET_HEADS * dv), BF16)]
    out_specs = [pl.BlockSpec((rows, dv), lambda b, h: (b, h))]
    if emit_state:
        out_shapes += [jax.ShapeDtypeStruct((batch, 1, RET_HEADS, dk, dv), F32)] * 2
        out_specs += [state_spec, state_spec]
    return pl.pallas_call(
        functools.partial(_retention_kernel, seq_len=seq_len, seqs=seqs, has_state=state is not None,
                          emit_state=emit_state),
        out_shape=out_shapes,
        grid=(batch // seqs, RET_HEADS),
        in_specs=in_specs,
        out_specs=out_specs,
        scratch_shapes=[pltpu.VMEM((seqs, 2, dk, dv), F32), pltpu.VMEM((rows, dv), F32)],
        compiler_params=_cparams("parallel", "parallel"),
        name="retention",
    )(*args)


def _pack_bf16_halves(x):
    half = x.shape[1] // 2
    lo = lax.bitcast_convert_type(x[:, :half].astype(F32), jnp.uint32)
    hi = lax.bitcast_convert_type(x[:, half:].astype(F32), jnp.uint32)
    return (hi & jnp.uint32(0xFFFF0000)) | (lo >> 16)


def _unpack_bf16_halves(p):
    lo = lax.bitcast_convert_type(p << 16, F32).astype(BF16)
    hi = lax.bitcast_convert_type(p & jnp.uint32(0xFFFF0000), F32).astype(BF16)
    return lo, hi


def _first_max_onehot(vals):
    m = vals[0]
    for v in vals[1:]:
        m = jnp.maximum(m, v)
    onehot, taken = [], None
    for v in vals:
        hit = v == m
        if taken is None:
            onehot.append(hit)
            taken = hit
        else:
            onehot.append(hit & jnp.logical_not(taken))
            taken = taken | hit
    return m, onehot


def _pick(onehot, vals):
    out = vals[-1]
    for oh, v in zip(onehot[-2::-1], vals[-2::-1]):
        out = jnp.where(oh, v, out)
    return out


def _route_rows(x, g_ref, mod_ref, wr_ref, bias_ref, tri_ref, h_ref, idx_ref, wcol_ref, rank_ref, cnt_ref,
                ind_scr, wrow_scr):
    hb = _modulated_norm(x, g_ref[...], mod_ref, 3).astype(BF16)
    h_ref[...] = _pack_bf16_halves(hb)
    logits = _nt_dot(wr_ref[...], hb)
    s = jax.nn.sigmoid(logits)
    sel = s + bias_ref[...]
    neg_inf = jnp.full_like(sel[0:1], -jnp.inf)
    sel_rows = [sel[e:e + 1] for e in range(N_EXPERTS)]
    s_rows = [s[e:e + 1] for e in range(N_EXPERTS)]

    def top2(vals):
        m1, oh1 = _first_max_onehot(vals)
        rest = [jnp.where(o, neg_inf, v) for o, v in zip(oh1, vals)]
        m2, oh2 = _first_max_onehot(rest)
        return m1, m2, oh1, oh2

    grp_scores = []
    for g in range(N_GROUPS):
        m1, m2, _, _ = top2(sel_rows[g * EXPERTS_PER_GROUP:(g + 1) * EXPERTS_PER_GROUP])
        grp_scores.append(m1 + m2)
    _, in_grp = _first_max_onehot(grp_scores)
    cand_sel = [_pick(in_grp, [sel_rows[g * EXPERTS_PER_GROUP + k] for g in range(N_GROUPS)])
                for k in range(EXPERTS_PER_GROUP)]
    cand_s = [_pick(in_grp, [s_rows[g * EXPERTS_PER_GROUP + k] for g in range(N_GROUPS)])
              for k in range(EXPERTS_PER_GROUP)]
    _, _, oh1, oh2 = top2(cand_sel)
    w1 = _pick(oh1, cand_s)
    w2 = _pick(oh2, cand_s)
    denom = w1 + w2
    wrow_scr[...] = jnp.zeros_like(wrow_scr)
    wrow_scr[0:1, :] = w1 / denom
    wrow_scr[1:2, :] = w2 / denom
    wcol_ref[...] = jnp.transpose(wrow_scr[...])

    ints = [jnp.full(w1.shape, k, I32) for k in range(EXPERTS_PER_GROUP)]
    grp = _pick(in_grp, ints) * EXPERTS_PER_GROUP
    idx_ref[0:1, :] = grp + _pick(oh1, ints)
    idx_ref[1:2, :] = grp + _pick(oh2, ints)

    one, zero = jnp.ones_like(w1), jnp.zeros_like(w1)
    for g in range(N_GROUPS):
        for k in range(EXPERTS_PER_GROUP):
            e = g * EXPERTS_PER_GROUP + k
            ind_scr[e:e + 1, :] = jnp.where(in_grp[g] & (oh1[k] | oh2[k]), one, zero)
    ind = ind_scr[...].astype(BF16)
    ranks = jnp.dot(ind, tri_ref[...], preferred_element_type=F32)
    cnt_ref[0] = jnp.dot(ind, jnp.ones((ind.shape[1], LANES), BF16), preferred_element_type=F32)
    for slot, oh in ((0, oh1), (1, oh2)):
        r = zero
        for g in range(N_GROUPS):
            for k in range(EXPERTS_PER_GROUP):
                e = g * EXPERTS_PER_GROUP + k
                r = r + jnp.where(in_grp[g] & oh[k], ranks[e:e + 1], zero)
        rank_ref[slot:slot + 1, :] = r.astype(I32)


def _proj_route_kernel(*refs, n_x, n_a, n_ctx_tiles):
    x_refs, a_refs = refs[:n_x], refs[n_x:n_x + n_a]
    w_ref, mod_ref, g_ref, wr_ref, bias_ref, tri_ref, o_ref = refs[n_x + n_a:n_x + n_a + 7]
    route_out_refs = refs[n_x + n_a + 7:n_x + n_a + 12]
    y_scr, ind_scr, wrow_scr = refs[n_x + n_a + 12:]
    is_dec = pl.program_id(0) >= n_ctx_tiles

    def project(a_ref):
        y_scr[...] = mod_ref[0, 2:3, :] * jnp.dot(a_ref[...].astype(BF16), w_ref[...],
                                                  preferred_element_type=F32)

    def add(x_ref):
        o_ref[...] = x_ref[...] + y_scr[...]

    _by_part(is_dec, a_refs, project)
    _by_part(is_dec, x_refs, add)
    _route_rows(o_ref[...], g_ref, mod_ref, wr_ref, bias_ref, tri_ref, *route_out_refs, ind_scr, wrow_scr)


def proj_residual_route(lay, x_parts, a_parts, w, mods, ffn_g, w_router_t, router_bias):
    d = x_parts[0].shape[1]
    kdim = a_parts[0].shape[1]
    tm, t = lay.tm, lay.t
    tri = jnp.triu(jnp.ones((tm, tm), BF16), k=1)
    pair = jax.ShapeDtypeStruct((TOP_K, t), I32)
    pair_spec = pl.BlockSpec((TOP_K, tm), lambda i: (0, i))
    const = lambda shape: pl.BlockSpec(shape, lambda i: (0,) * len(shape))
    return pl.pallas_call(
        functools.partial(_proj_route_kernel, n_x=len(x_parts), n_a=len(a_parts), n_ctx_tiles=lay.n_ctx_tiles),
        out_shape=[jax.ShapeDtypeStruct((t, d), F32), jax.ShapeDtypeStruct((t, d // 2), jnp.uint32), pair,
                   jax.ShapeDtypeStruct((t, LANES), F32), pair,
                   jax.ShapeDtypeStruct((lay.n_tiles, N_EXPERTS, LANES), F32)],
        grid=(lay.n_tiles,),
        in_specs=lay.row_specs(x_parts, d) + lay.row_specs(a_parts, kdim) + [
            const((kdim, d)),
            pl.BlockSpec((1, N_MOD, d), lambda i: (lay.mod_row(i), 0, 0)),
            const((1, d)), const((N_EXPERTS, d)), const((N_EXPERTS, 1)), const((tm, tm)),
        ],
        out_specs=[pl.BlockSpec((tm, d), lambda i: (i, 0)), pl.BlockSpec((tm, d // 2), lambda i: (i, 0)), pair_spec,
                   pl.BlockSpec((tm, LANES), lambda i: (i, 0)), pair_spec,
                   pl.BlockSpec((1, N_EXPERTS, LANES), lambda i: (i, 0, 0))],
        scratch_shapes=[pltpu.VMEM((tm, d), F32), pltpu.VMEM((N_EXPERTS, tm), F32), pltpu.VMEM((LANES, tm), F32)],
        compiler_params=_cparams("arbitrary"),
        name="proj_residual_route",
    )(*x_parts, *a_parts, w, mods, ffn_g.reshape(1, d), w_router_t, router_bias.reshape(N_EXPERTS, 1), tri)


def dispatch_plan(lay, idx, rank, cnt, n_sorted):
    tm = lay.tm
    cnt_tile = cnt[:, :, 0].astype(I32)
    total = jnp.sum(cnt_tile, axis=0)
    padded = ((total + EXPERT_TILE - 1) // EXPERT_TILE) * EXPERT_TILE
    end = jnp.cumsum(padded)
    start = end - padded
    base = start[None, :] + jnp.cumsum(cnt_tile, axis=0) - cnt_tile
    idx3 = idx.reshape(TOP_K, lay.n_tiles, tm)
    pos = rank.reshape(TOP_K, lay.n_tiles, tm)
    for e in range(N_EXPERTS):
        pos = pos + jnp.where(idx3 == e, base[None, :, e, None], 0)
    tile_row = jnp.arange(n_sorted // EXPERT_TILE, dtype=I32) * EXPERT_TILE
    tile_expert = jnp.minimum(jnp.sum(end[None, :] <= tile_row[:, None], axis=1), N_EXPERTS - 1).astype(I32)
    n_valid = (end[-1] // EXPERT_TILE).astype(I32).reshape(1)
    return pos.reshape(TOP_K * lay.t).astype(I32), tile_expert, n_valid


def _sc_mesh():
    return plsc.VectorSubcoreMesh(core_axis_name="c", subcore_axis_name="s")


def _sc_worker_id():
    return lax.axis_index("s") * V7X_SC_CORES + lax.axis_index("c")


def sc_scatter_rows(src, pos, n_out_rows):
    t, d = src.shape
    n_idx = pos.shape[0]
    per_worker = n_idx // SC_WORKERS
    chunk = SC_CHUNK_ROWS
    assert n_idx % (SC_WORKERS * chunk) == 0 and t % chunk == 0

    @functools.partial(
        pl.kernel, mesh=_sc_mesh(), out_type=jax.ShapeDtypeStruct((n_out_rows, d), src.dtype),
        scratch_types=[pltpu.VMEM((chunk,), I32), pltpu.VMEM((chunk, d), src.dtype)],
        name="sc_scatter_rows")
    def scatter(src_hbm, pos_hbm, out_hbm, idx_v, rows_v):
        base = _sc_worker_id() * per_worker

        @pl.loop(0, per_worker // chunk)
        def _(i):
            a0 = base + i * chunk
            pltpu.sync_copy(pos_hbm.at[pl.ds(a0, chunk)], idx_v)
            pltpu.sync_copy(src_hbm.at[pl.ds(lax.rem(a0, t), chunk)], rows_v)
            pltpu.sync_copy(rows_v, out_hbm.at[idx_v])

    return scatter(src, pos)


def sc_gather_rows(table, idx):
    _, d = table.shape
    n_idx = idx.shape[0]
    per_worker = n_idx // SC_WORKERS
    chunk = SC_CHUNK_ROWS
    assert n_idx % (SC_WORKERS * chunk) == 0

    @functools.partial(
        pl.kernel, mesh=_sc_mesh(), out_type=jax.ShapeDtypeStruct((n_idx, d), table.dtype),
        scratch_types=[pltpu.VMEM((chunk,), I32), pltpu.VMEM((chunk, d), table.dtype)],
        name="sc_gather_rows")
    def gather(table_hbm, idx_hbm, out_hbm, idx_v, rows_v):
        base = _sc_worker_id() * per_worker

        @pl.loop(0, per_worker // chunk)
        def _(i):
            a0 = base + i * chunk
            pltpu.sync_copy(idx_hbm.at[pl.ds(a0, chunk)], idx_v)
            pltpu.sync_copy(table_hbm.at[idx_v], rows_v)
            pltpu.sync_copy(rows_v, out_hbm.at[pl.ds(a0, chunk)])

    return gather(table, idx)


def _experts_kernel(te_ref, nv_ref, x_ref, wg_ref, wu_ref, wd_ref, y_ref, wg_scr, wu_scr, wd_scr):
    i = pl.program_id(0)

    @pl.when(i < nv_ref[0])
    def _():
        @pl.when((i == 0) | (te_ref[i] != te_ref[jnp.maximum(i - 1, 0)]))
        def _():
            wg_scr[...] = wg_ref[0, 0].astype(BF16)
            wu_scr[...] = wu_ref[0, 0].astype(BF16)
            wd_scr[...] = wd_ref[0, 0].astype(BF16)

        x_lo, x_hi = _unpack_bf16_halves(x_ref[...])
        half = x_lo.shape[1]

        def in_proj(w_scr):
            return (jnp.dot(x_lo, w_scr[0:half, :], preferred_element_type=F32)
                    + jnp.dot(x_hi, w_scr[half:, :], preferred_element_type=F32))

        a = jax.nn.silu(in_proj(wg_scr)) * in_proj(wu_scr)
        y = jnp.dot(a.astype(BF16), wd_scr[...], preferred_element_type=F32)
        y_ref[...] = _pack_bf16_halves(y.astype(BF16))


def grouped_experts(xs, tile_expert, n_valid, wg, wu, wd, layer):
    n_rows = xs.shape[0]
    d, de = wg.shape[-2:]
    tm = EXPERT_TILE
    row_map = lambda i, te, nv: (jnp.minimum(i, nv[0] - 1), 0)
    grid_spec = pltpu.PrefetchScalarGridSpec(
        num_scalar_prefetch=2,
        grid=(n_rows // tm,),
        in_specs=[
            pl.BlockSpec((tm, d // 2), row_map),
            pl.BlockSpec((1, 1, d, de), lambda i, te, nv: (layer, te[i], 0, 0)),
            pl.BlockSpec((1, 1, d, de), lambda i, te, nv: (layer, te[i], 0, 0)),
            pl.BlockSpec((1, 1, de, d), lambda i, te, nv: (layer, te[i], 0, 0)),
        ],
        out_specs=pl.BlockSpec((tm, d // 2), row_map),
        scratch_shapes=[pltpu.VMEM((d, de), BF16), pltpu.VMEM((d, de), BF16), pltpu.VMEM((de, d), BF16)],
    )
    return pl.pallas_call(
        _experts_kernel,
        out_shape=jax.ShapeDtypeStruct((n_rows, d // 2), jnp.uint32),
        grid_spec=grid_spec,
        compiler_params=_cparams("arbitrary"),
        name="grouped_experts",
    )(tile_expert, n_valid, xs, wg, wu, wd)


def _combine_kernel(x_ref, y_ref, w_ref, mod_ref, fg_ref, *o_refs, final_norm, n_ctx_tiles):
    w = w_ref[...]

    def expert_out(slot):
        lo, hi = _unpack_bf16_halves(y_ref[slot])
        return jnp.concatenate([lo.astype(F32), hi.astype(F32)], axis=1)

    y = w[:, 0:1] * expert_out(0) + w[:, 1:2] * expert_out(1)
    out = x_ref[...] + mod_ref[0, 5:6, :] * y
    if final_norm:
        ms = jnp.mean(out * out, axis=-1, keepdims=True)
        out = out * lax.rsqrt(ms + NORM_EPS) * fg_ref[...]

    def store(o_ref):
        o_ref[...] = out
    _by_part(pl.program_id(0) >= n_ctx_tiles, o_refs, store)


def combine_residual(lay, x, y_pair, w_col, mods, final_g, final_norm, split):
    t, d = x.shape
    tm = lay.tm
    if split:
        out_shape = [jax.ShapeDtypeStruct((lay.t_ctx, d), F32), jax.ShapeDtypeStruct((lay.t_dec, d), F32)]
        out_specs = lay.row_specs([None, None], d)
    else:
        out_shape = [jax.ShapeDtypeStruct((t, d), F32)]
        out_specs = lay.row_specs([None], d)
    return pl.pallas_call(
        functools.partial(_combine_kernel, final_norm=final_norm, n_ctx_tiles=lay.n_ctx_tiles),
        out_shape=out_shape,
        grid=(lay.n_tiles,),
        in_specs=[
            pl.BlockSpec((tm, d), lambda i: (i, 0)),
            pl.BlockSpec((TOP_K, tm, d // 2), lambda i: (0, i, 0)),
            pl.BlockSpec((tm, LANES), lambda i: (i, 0)),
            pl.BlockSpec((1, N_MOD, d), lambda i: (lay.mod_row(i), 0, 0)),
            pl.BlockSpec((1, d), lambda i: (0, 0)),
        ],
        out_specs=out_specs,
        compiler_params=_cparams("arbitrary"),
        name="combine_residual",
    )(x, y_pair, w_col, mods, final_g.reshape(1, d))


def group_moe_residual(lay, x, routing, mods, p, layer, last):
    t, d = x.shape
    h, idx, w_col, rank, cnt = routing
    n_assign = TOP_K * t
    row_quantum = SC_WORKERS * SC_CHUNK_ROWS
    n_sorted = n_assign + N_EXPERTS * (EXPERT_TILE - 1)
    n_sorted = -(-n_sorted // row_quantum) * row_quantum
    n_sorted = -(-n_sorted // EXPERT_TILE) * EXPERT_TILE
    pos, tile_expert, n_valid = dispatch_plan(lay, idx, rank, cnt, n_sorted)
    xs = sc_scatter_rows(h, pos, n_sorted)
    ys = grouped_experts(xs, tile_expert, n_valid, p["moe_w_gate"], p["moe_w_up"], p["moe_w_down"], layer)
    y_pair = sc_gather_rows(ys, pos).reshape(TOP_K, t, d // 2)
    return combine_residual(lay, x, y_pair, w_col, mods, p["final_norm_g"], final_norm=last, split=last)


def _rope_angles(n, d):
    n_rows = n // GRID_W
    row = jnp.repeat(jnp.arange(n_rows), GRID_W).astype(F32)
    col = jnp.tile(jnp.arange(GRID_W), n_rows).astype(F32)
    nf = d // 4
    freqs = jnp.power(ROPE_BASE, -jnp.arange(nf, dtype=F32) / nf)
    ang = jnp.concatenate([row[:, None] * freqs, col[:, None] * freqs], axis=-1)
    return jnp.cos(ang), jnp.sin(ang)


def kernel(x_prompt, x_sample, cache_attn_k, cache_attn_v, state_ret_fwd, state_ret_bwd, c, c_ctx, w_ada, b_ada, norm_mix_g, norm_ffn_g, final_norm_g, da_w_qkv, da_lambda_q1, da_lambda_k1, da_lambda_q2, da_lambda_k2, da_subln_g, da_w_o, ret_w_qkv, ret_w_gate_fwd, ret_w_gate_bwd, ret_decay_fwd, ret_decay_bwd, ret_w_o, w_router, router_bias, moe_w_gate, moe_w_up, moe_w_down):
    b_ctx, n_ctx, d = x_prompt.shape
    b_dec, n_dec, _ = x_sample.shape
    past = cache_attn_k.shape[2]
    n_attn = cache_attn_k.shape[1]
    depth = w_ada.shape[0]
    assert b_dec + 1 <= MOD_ROWS
    lay = Layout(b_ctx, n_ctx, b_dec, n_dec)

    cond = jnp.zeros((MOD_ROWS, d), F32).at[0].set(c_ctx).at[1:1 + b_dec].set(c)
    mods_all = ada_modulation(cond, w_ada, b_ada)

    p = {
        "norm_ffn_g": norm_ffn_g, "final_norm_g": final_norm_g,
        "w_router_t": w_router.T.astype(BF16), "router_bias": router_bias.astype(F32),
        "moe_w_gate": moe_w_gate, "moe_w_up": moe_w_up, "moe_w_down": moe_w_down,
    }
    ret_log_decay = jnp.stack([jax.nn.log_sigmoid(ret_decay_fwd.astype(F32)),
                               jax.nn.log_sigmoid(ret_decay_bwd.astype(F32))], axis=1)
    ck_all = cache_attn_k.reshape(b_dec, n_attn, past, -1)
    cv_all = cache_attn_v.reshape(b_dec, n_attn, past, -1)

    x_parts = [x_prompt.reshape(lay.t_ctx, d), x_sample.reshape(lay.t_dec, d)]
    new_k, new_v, new_sf, new_sb = [], [], [], []
    for i in range(depth):
        mods = mods_all[i]
        j = i // 2
        if i % 2 == 0:
            lam_init = 0.8 - 0.6 * math.exp(-0.3 * i)
            qkw = DA_HEADS * 2 * DA_HEAD_DIM
            vw = DA_HEADS * DA_V_DIM
            cos, sin = _rope_angles(n_dec, DA_HEAD_DIM)
            rope = (jnp.tile(cos, (1, 4)), jnp.concatenate([-sin, sin, -sin, sin], axis=-1))
            q, k, v, k_ctx, v_ctx = norm_mod_matmul(
                lay, x_parts, norm_mix_g[i], mods, da_w_qkv[j].astype(BF16),
                [(0, qkw, "rope64", math.log2(math.e) * DA_HEAD_DIM ** -0.5, BF16, "all"),
                 (qkw, qkw, "rope64", 1.0, BF16, "all"),
                 (2 * qkw, vw, "plain", 1.0, BF16, "all"),
                 (qkw, qkw, "plain", 1.0, F32, "ctx64"), (2 * qkw, vw, "plain", 1.0, F32, "ctx")],
                rope)
            lam_params = jnp.stack([da_lambda_q1[j], da_lambda_k1[j], da_lambda_q2[j], da_lambda_k2[j]])
            mix = [diff_attention(q, k, v, 0, b_ctx, n_ctx, lam_params, da_subln_g[j], lam_init,
                                  heads_per_step=DA_HEADS),
                   diff_attention(q, k, v, lay.t_ctx, b_dec, n_dec, lam_params, da_subln_g[j], lam_init,
                                  cache=(ck_all[:, j], cv_all[:, j]))]
            mix_w = da_w_o[j].astype(BF16)
            new_k.append(k_ctx.reshape(b_ctx, n_ctx, DA_HEADS, 2, DA_HEAD_DIM))
            new_v.append(v_ctx.reshape(b_ctx, n_ctx, DA_HEADS, DA_V_DIM))
        else:
            kd = ret_w_qkv.shape[2] // 4
            dv = 2 * kd
            w_all = jnp.concatenate([ret_w_qkv[j], ret_w_gate_fwd[j], ret_w_gate_bwd[j]], axis=-1).astype(BF16)
            q, k, v, gates = norm_mod_matmul(
                lay, x_parts, norm_mix_g[i], mods, w_all,
                [(0, kd, "rope256", 1.0, BF16, "all"),
                 (kd, kd, "rope256", (kd // RET_HEADS) ** -0.5, F32, "all"),
                 (2 * kd, dv, "plain", 1.0, BF16, "all"), (2 * kd + dv, 2 * dv, "silu", 1.0, BF16, "all")],
                _rope_angles(n_dec, kd // RET_HEADS))
            ctx_seqs = math.gcd(b_ctx, max(1, n_dec // n_ctx // 2))
            o_ctx, sf, sb = retention(q, k, v, gates, 0, b_ctx, n_ctx, ret_log_decay[j], emit_state=True,
                                      seqs_per_step=ctx_seqs)
            (o_dec,) = retention(q, k, v, gates, lay.t_ctx, b_dec, n_dec, ret_log_decay[j],
                                 state=(state_ret_fwd[:, j:j + 1], state_ret_bwd[:, j:j + 1]))
            mix, mix_w = [o_ctx, o_dec], ret_w_o[j].astype(BF16)
            new_sf.append(sf)
            new_sb.append(sb)
        x, *routing = proj_residual_route(lay, x_parts, mix, mix_w, mods, norm_ffn_g[i], p["w_router_t"],
                                          p["router_bias"])
        x_parts = group_moe_residual(lay, x, routing, mods, p, i, last=(i == depth - 1))

    y_ctx, y_dec = x_parts
    return (y_ctx.reshape(b_ctx, n_ctx, d), y_dec.reshape(b_dec, n_dec, d),
            jnp.stack(new_k, axis=1), jnp.stack(new_v, axis=1),
            jnp.concatenate(new_sf, axis=1), jnp.concatenate(new_sb, axis=1))
```

```python
import functools
import math

import jax
import jax.numpy as jnp
from jax import lax
from jax.experimental import pallas as pl
from jax.experimental.pallas import tpu as pltpu
from jax.experimental.pallas import tpu_sc as plsc

F32 = jnp.float32
BF16 = jnp.bfloat16
I32 = jnp.int32

GRID_W = 64
ROPE_BASE = 10000.0
NORM_EPS = 1e-6
DA_HEADS = 8
DA_HEAD_DIM = 64
DA_V_DIM = 2 * DA_HEAD_DIM
RET_HEADS = 4
RET_CHUNK = 128
N_EXPERTS = 16
N_GROUPS = 4
EXPERTS_PER_GROUP = N_EXPERTS // N_GROUPS
TOP_K = 2
N_MOD = 6
MOD_ROWS = 16
LANES = 128

V7X_VMEM_LIMIT = 56 * 1024 * 1024
V7X_SC_CORES = 2
V7X_SC_SUBCORES = 16
SC_WORKERS = V7X_SC_CORES * V7X_SC_SUBCORES
SC_CHUNK_ROWS = 128

ROW_TILE = 512
EXPERT_TILE = 512


def _cparams(*sem):
    return pltpu.CompilerParams(dimension_semantics=sem, vmem_limit_bytes=V7X_VMEM_LIMIT)


def _nt_dot(a, b):
    return lax.dot_general(a, b, (((1,), (1,)), ((), ())), preferred_element_type=F32)


def _tn_dot(a, b):
    return lax.dot_general(a, b, (((0,), (0,)), ((), ())), preferred_element_type=F32)


class Layout:
    def __init__(self, b_ctx, n_ctx, b_dec, n_dec):
        self.b_ctx, self.n_ctx, self.b_dec, self.n_dec = b_ctx, n_ctx, b_dec, n_dec
        self.t_ctx, self.t_dec = b_ctx * n_ctx, b_dec * n_dec
        self.t = self.t_ctx + self.t_dec
        self.tm = min(ROW_TILE, n_dec, self.t_ctx)
        assert self.t_ctx % self.tm == 0 and n_dec % self.tm == 0
        assert self.t_ctx % n_dec == 0 and self.t_ctx % n_ctx == 0
        self.n_ctx_tiles = self.t_ctx // self.tm
        self.n_tiles = self.t // self.tm

    def mod_row(self, i):
        r = i * self.tm
        return jnp.where(r < self.t_ctx, 0, 1 + (r - self.t_ctx) // self.n_dec)

    def part_tile(self, part, i):
        if part == 0:
            return jnp.minimum(i, self.n_ctx_tiles - 1)
        return jnp.maximum(i - self.n_ctx_tiles, 0)

    def row_specs(self, arrays, width, extra_grid_dims=0):
        pad = (0,) * 0
        if len(arrays) == 1:
            return [pl.BlockSpec((self.tm, width), lambda i, *_: (i, 0))]
        return [pl.BlockSpec((self.tm, width), lambda i, *_, p=p: (self.part_tile(p, i), 0)) for p in (0, 1)]


def _ada_kernel(c_ref, w_ref, b_ref, o_ref):
    s = jax.nn.silu(c_ref[...]).astype(BF16)
    acc = jnp.dot(s, w_ref[0].astype(BF16), preferred_element_type=F32)
    o_ref[0] = acc + b_ref[0]


def ada_modulation(cond, w_ada, b_ada):
    depth, d, n = w_ada.shape
    tn = 1536
    out = pl.pallas_call(
        _ada_kernel,
        out_shape=jax.ShapeDtypeStruct((depth, MOD_ROWS, n), F32),
        grid=(depth, n // tn),
        in_specs=[
            pl.BlockSpec((MOD_ROWS, d), lambda l, j: (0, 0)),
            pl.BlockSpec((1, d, tn), lambda l, j: (l, 0, j)),
            pl.BlockSpec((1, 1, tn), lambda l, j: (l, 0, j)),
        ],
        out_specs=pl.BlockSpec((1, MOD_ROWS, tn), lambda l, j: (l, 0, j)),
        compiler_params=_cparams("parallel", "parallel"),
        name="ada_modulation",
    )(cond, w_ada, b_ada.reshape(depth, 1, n))
    return out.reshape(depth, MOD_ROWS, N_MOD, d)


def _modulated_norm(x, g, mod_ref, shift_idx):
    ms = jnp.mean(x * x, axis=-1, keepdims=True)
    y = x * lax.rsqrt(ms + NORM_EPS) * g
    return y * (1.0 + mod_ref[0, shift_idx + 1:shift_idx + 2, :]) + mod_ref[0, shift_idx:shift_idx + 1, :]


def _by_part(is_dec, refs, fn):
    if len(refs) == 1:
        fn(refs[0])
        return
    pl.when(jnp.logical_not(is_dec))(lambda: fn(refs[0]))
    pl.when(is_dec)(lambda: fn(refs[1]))


def _rope64(a, cos, sin_signed, first_half):
    partner = jnp.where(first_half, pltpu.roll(a, 96, 1), pltpu.roll(a, 32, 1))
    return a * cos + partner * sin_signed


def _nmm_kernel(*refs, n_x, outs, tn, n_ctx_tiles):
    x_refs = refs[:n_x]
    g_ref, mod_ref, w_ref, cos_ref, sin_ref = refs[n_x:n_x + 5]
    out_refs = refs[n_x + 5:]
    is_dec = pl.program_id(0) >= n_ctx_tiles
    tm = x_refs[0].shape[0]
    segments = sorted({(col0, width) for col0, width, _, _, _ in outs})

    def rotated(a, kind):
        cos, sin = cos_ref[...], sin_ref[...]
        if kind == "rope64":
            lane = lax.broadcasted_iota(I32, (tm, LANES), 1)
            first_half = (lane & 32) == 0
            return [(c * 128, _rope64(a[:, c * 128:(c + 1) * 128], cos, sin, first_half))
                    for c in range(tn // 128)]
        pieces = []
        for c in range(tn // 256):
            x1, x2 = a[:, c * 256:c * 256 + 128], a[:, c * 256 + 128:(c + 1) * 256]
            pieces += [(c * 256, x1 * cos - x2 * sin), (c * 256 + 128, x2 * cos + x1 * sin)]
        return pieces

    def emit(dec):
        x_ref = x_refs[(1 if dec else 0) if n_x == 2 else 0]
        h = _modulated_norm(x_ref[...], g_ref[...], mod_ref, 0).astype(BF16)
        for col0, width in segments:
            sinks = [(o_ref, o) for o_ref, o in zip(out_refs, outs)
                     if (o[0], o[1]) == (col0, width) and not (dec and o[4] != "all")]
            for blk in range(width // tn):
                acc = jnp.dot(h, w_ref[:, col0 + blk * tn:col0 + (blk + 1) * tn], preferred_element_type=F32)
                for o_ref, (_, _, kind, scale, rows) in sinks:
                    a = acc if scale == 1.0 else acc * scale
                    if rows == "ctx64":
                        n64 = width // 64
                        for c in range(tn // 64):
                            o_ref[pl.ds(blk * tn // 64 + c, tm, stride=n64), :] = (
                                a[:, c * 64:(c + 1) * 64].astype(o_ref.dtype))
                    elif kind == "silu":
                        o_ref[:, blk * tn:(blk + 1) * tn] = jax.nn.silu(a).astype(o_ref.dtype)
                    elif kind == "plain" or rows == "ctx" or not dec:
                        o_ref[:, blk * tn:(blk + 1) * tn] = a.astype(o_ref.dtype)
                    else:
                        for off, val in rotated(a, kind):
                            o_ref[:, blk * tn + off:blk * tn + off + 128] = val.astype(o_ref.dtype)

    pl.when(jnp.logical_not(is_dec))(lambda: emit(False))
    pl.when(is_dec)(lambda: emit(True))


def norm_mod_matmul(lay, x_parts, g, mods, w, outputs, rope_tables, tn=512):
    d = x_parts[0].shape[1]
    tm = lay.tm
    n_total = w.shape[1]
    outs, out_shapes, out_specs = [], [], []
    for col0, width, kind, scale, dtype, rows in outputs:
        assert width % tn == 0 and col0 % LANES == 0
        outs.append((col0, width, kind, float(scale), rows))
        if rows == "ctx64":
            n64 = width // 64
            out_shapes.append(jax.ShapeDtypeStruct((lay.t_ctx * n64, 64), dtype))
            out_specs.append(pl.BlockSpec((tm * n64, 64), lambda i: (lay.part_tile(0, i), 0)))
            continue
        n_rows = lay.t_ctx if rows == "ctx" else lay.t
        out_shapes.append(jax.ShapeDtypeStruct((n_rows, width), dtype))
        out_specs.append(lay.row_specs([None, None], width)[0] if rows == "ctx" else lay.row_specs([None], width)[0])
    blocks_per_seq = lay.n_dec // tm
    rope_spec = pl.BlockSpec((tm, LANES), lambda i: (lay.part_tile(1, i) % blocks_per_seq, 0))
    return pl.pallas_call(
        functools.partial(_nmm_kernel, n_x=len(x_parts), outs=tuple(outs), tn=tn, n_ctx_tiles=lay.n_ctx_tiles),
        out_shape=out_shapes,
        grid=(lay.n_tiles,),
        in_specs=lay.row_specs(x_parts, d) + [
            pl.BlockSpec((1, d), lambda i: (0, 0)),
            pl.BlockSpec((1, N_MOD, d), lambda i: (lay.mod_row(i), 0, 0)),
            pl.BlockSpec((d, n_total), lambda i: (0, 0), pipeline_mode=pl.Buffered(1)),
            rope_spec, rope_spec,
        ],
        out_specs=out_specs,
        compiler_params=_cparams("arbitrary"),
        name="norm_mod_matmul",
    )(*x_parts, g.reshape(1, d), mods, w, *rope_tables)


def _diff_lambda(lam_ref, lam_init):
    lp = lam_ref[...]
    return (jnp.exp(jnp.sum(lp[0:1] * lp[1:2], axis=-1, keepdims=True))
            - jnp.exp(jnp.sum(lp[2:3] * lp[3:4], axis=-1, keepdims=True)) + lam_init)


def _component_scores(q, k):
    lane = lax.broadcasted_iota(I32, q.shape, 1)
    zero = jnp.zeros_like(q)
    return [_nt_dot(jnp.where((lane < DA_HEAD_DIM) == (comp == 0), q, zero), k) for comp in range(2)]


def _diff_softmax_values(scores, lam, v, subln_g, lam_init):
    exps = []
    for s in scores:
        e = jnp.exp2(s - jnp.max(s, axis=-1, keepdims=True))
        exps.append((e, 1.0 / jnp.sum(e, axis=-1, keepdims=True)))
    c0, c1 = exps[0][1], lam * exps[1][1]
    o = jnp.dot((exps[0][0] * c0 - exps[1][0] * c1).astype(BF16), v, preferred_element_type=F32)
    ms = jnp.mean(o * o, axis=-1, keepdims=True)
    return (o * lax.rsqrt(ms + NORM_EPS) * subln_g) * (1.0 - lam_init)


def _diff_attn_kernel(lam_ref, q_ref, k_ref, v_ref, g_ref, o_ref, *, lam_init, heads_per_step):
    hw = 2 * DA_HEAD_DIM
    lam = _diff_lambda(lam_ref, lam_init)
    for hh in range(heads_per_step):
        cols = slice(hh * hw, (hh + 1) * hw)
        scores = _component_scores(q_ref[:, cols], k_ref[:, cols])
        o_ref[:, cols] = _diff_softmax_values(scores, lam, v_ref[:, cols], g_ref[...], lam_init).astype(o_ref.dtype)


def _diff_attn_cached_kernel(lam_ref, q_ref, k_ref, v_ref, ck_ref, cv_ref, g_ref, o_ref, k_scr, vt_scr, s_scr, *,
                             lam_init, sub_rows):
    seq_len, hw = q_ref.shape
    n_sub = seq_len // sub_rows
    assert n_sub % 2 == 0
    lam = _diff_lambda(lam_ref, lam_init)
    k_scr[0:seq_len, :] = k_ref[...]
    k_scr[seq_len:, :] = ck_ref[0].astype(BF16)
    vt_scr[0:hw, 0:seq_len] = jnp.transpose(v_ref[...].astype(F32)).astype(BF16)
    vt_scr[0:hw, seq_len:] = jnp.transpose(cv_ref[0]).astype(BF16)
    vt_scr[hw:, :] = jnp.ones((vt_scr.shape[0] - hw, vt_scr.shape[1]), BF16)

    def rows_of(t):
        return pl.ds(pl.multiple_of(t * sub_rows, sub_rows), sub_rows)

    def scores(t, slot):
        q = q_ref[rows_of(t), :]
        lane = lax.broadcasted_iota(I32, q.shape, 1)
        zero = jnp.zeros_like(q)
        for comp in range(2):
            s_scr[slot, comp] = _nt_dot(k_scr[...], jnp.where((lane < DA_HEAD_DIM) == (comp == 0), q, zero))

    def outputs(t, slot):
        parts = []
        for comp in range(2):
            s = s_scr[slot, comp]
            e = jnp.exp2(s - jnp.max(s, axis=0, keepdims=True)).astype(BF16)
            acc = jnp.dot(vt_scr[...], e, preferred_element_type=F32)
            parts.append(acc[0:hw] / acc[hw:hw + 1])
        o = parts[0] - lam * parts[1]
        ms = jnp.mean(o * o, axis=0, keepdims=True)
        o = (o * lax.rsqrt(ms + NORM_EPS) * g_ref[...]) * (1.0 - lam_init)
        o_ref[rows_of(t), :] = jnp.transpose(o).astype(o_ref.dtype)

    scores(0, 0)

    def body(i2, carry):
        t = 2 * i2
        scores(t + 1, 1)
        outputs(t, 0)
        scores(jnp.minimum(t + 2, n_sub - 1), 0)
        outputs(t + 1, 1)
        return carry

    lax.fori_loop(0, n_sub // 2, body, 0)


def diff_attention(q, k, v, row0, batch, seq_len, lam_params, subln_g, lam_init, cache=None, heads_per_step=1,
                   sub_rows=256):
    width = q.shape[1]
    hw = 2 * DA_HEAD_DIM
    bw = heads_per_step * hw
    assert row0 % seq_len == 0 and DA_HEADS % heads_per_step == 0
    s0 = row0 // seq_len
    seq_spec = pl.BlockSpec((seq_len, bw), lambda b, h: (s0 + b, h))
    in_specs = [pl.BlockSpec((4, DA_HEAD_DIM), lambda b, h: (0, 0)), seq_spec, seq_spec, seq_spec]
    args = [lam_params, q, k, v]
    scratch = []
    if cache is None:
        body = functools.partial(_diff_attn_kernel, lam_init=lam_init, heads_per_step=heads_per_step)
    else:
        assert heads_per_step == 1
        past = cache[0].shape[1]
        cache_spec = pl.BlockSpec((1, past, hw), lambda b, h: (b, 0, h))
        in_specs += [cache_spec, cache_spec]
        args += list(cache)
        sub_rows = min(sub_rows, seq_len // 2)
        ones_rows = 16
        scratch = [pltpu.VMEM((seq_len + past, hw), BF16), pltpu.VMEM((hw + ones_rows, seq_len + past), BF16),
                   pltpu.VMEM((2, 2, seq_len + past, sub_rows), F32)]
        body = functools.partial(_diff_attn_cached_kernel, lam_init=lam_init, sub_rows=sub_rows)
        in_specs.append(pl.BlockSpec((hw, 1), lambda b, h: (0, 0)))
        args.append(subln_g.reshape(hw, 1))
    if cache is None:
        in_specs.append(pl.BlockSpec((1, hw), lambda b, h: (0, 0)))
        args.append(subln_g.reshape(1, hw))
    return pl.pallas_call(
        body,
        out_shape=jax.ShapeDtypeStruct((batch * seq_len, width), BF16),
        grid=(batch, DA_HEADS // heads_per_step),
        in_specs=in_specs,
        out_specs=pl.BlockSpec((seq_len, bw), lambda b, h: (b, h)),
        scratch_shapes=scratch,
        compiler_params=_cparams("parallel", "parallel"),
        name="diff_attention",
    )(*args)


def _retention_kernel(*refs, seq_len, seqs, has_state, emit_state):
    lg_ref, q_ref, k_ref, v_ref, gf_ref, gb_ref = refs[:6]
    pos = 6
    if has_state:
        s0_refs = refs[6:8]
        pos = 8
    o_ref = refs[pos]
    pos += 1
    if emit_state:
        s_out_refs = refs[pos:pos + 2]
        pos += 2
    s_scr, o_scr = refs[pos:pos + 2]

    h = pl.program_id(1)
    c_len = RET_CHUNK
    n_chunks = seq_len // c_len
    assert n_chunks % 2 == 0
    row = lax.broadcasted_iota(I32, (c_len, c_len), 0)
    colm = lax.broadcasted_iota(I32, (c_len, c_len), 1)
    rel = (row - colm).astype(F32)
    posv = lax.broadcasted_iota(I32, (c_len, 1), 0).astype(F32)

    consts = []
    for backward in (False, True):
        lg = lg_ref[1 if backward else 0, h]
        if backward:
            intra = jnp.where(rel <= 0, jnp.exp(-rel * lg), 0.0)
            q_decay = jnp.exp((c_len - posv) * lg)
            k_decay = jnp.exp(posv * lg)
        else:
            intra = jnp.where(rel >= 0, jnp.exp(rel * lg), 0.0)
            q_decay = jnp.exp((posv + 1.0) * lg)
            k_decay = jnp.exp((c_len - 1.0 - posv) * lg)
        consts.append((intra, q_decay, k_decay, jnp.exp(jnp.zeros((1, 1), F32) + c_len * lg)))
        direction = 1 if backward else 0
        for sq in range(seqs):
            if has_state:
                s_scr[sq, direction] = s0_refs[direction][sq, 0, 0]
            else:
                s_scr[sq, direction] = jnp.zeros(s_scr.shape[2:], F32)

    def chunk(sq, c, backward, first_touch):
        direction = 1 if backward else 0
        intra, q_decay, k_decay, chunk_decay = consts[direction]
        gate_ref = gb_ref if backward else gf_ref
        rows = pl.ds(pl.multiple_of(sq * seq_len + c * c_len, c_len), c_len)
        qb = q_ref[rows, :].astype(BF16)
        kf = k_ref[rows, :].astype(F32)
        vb = v_ref[rows, :].astype(BF16)
        a = _nt_dot(qb, kf.astype(BF16)) * intra
        s = s_scr[sq, direction]
        o = (jnp.dot(a.astype(BF16), vb, preferred_element_type=F32)
             + jnp.dot(qb, s.astype(BF16), preferred_element_type=F32) * q_decay)
        s_scr[sq, direction] = chunk_decay * s + _tn_dot((kf * k_decay).astype(BF16), vb)
        mu = jnp.mean(o, axis=-1, keepdims=True)
        oc = o - mu
        var = jnp.mean(oc * oc, axis=-1, keepdims=True)
        gated = oc * lax.rsqrt(var + NORM_EPS) * gate_ref[rows, :].astype(F32)
        if first_touch:
            o_scr[rows, :] = gated
        else:
            o_ref[rows, :] = (o_scr[rows, :] + gated).astype(o_ref.dtype)

    def body(ci, first_touch):
        for sq in range(seqs):
            chunk(sq, ci, False, first_touch)
            chunk(sq, n_chunks - 1 - ci, True, first_touch)

    unroll = 4 if n_chunks % 8 == 0 else 1
    lax.fori_loop(0, n_chunks // 2, lambda ci, c: (body(ci, True), c)[1], 0, unroll=unroll)
    lax.fori_loop(n_chunks // 2, n_chunks, lambda ci, c: (body(ci, False), c)[1], 0, unroll=unroll)
    if emit_state:
        for d in range(2):
            for sq in range(seqs):
                s_out_refs[d][sq, 0, 0] = s_scr[sq, d]


def retention(q, k, v, gates, row0, batch, seq_len, log_decay, state=None, emit_state=False, seqs_per_step=1):
    dk = q.shape[1] // RET_HEADS
    dv = v.shape[1] // RET_HEADS
    seqs = seqs_per_step
    rows = seqs * seq_len
    assert row0 % rows == 0 and batch % seqs == 0
    s0 = row0 // rows
    in_specs = [
        pl.BlockSpec(memory_space=pltpu.SMEM),
        pl.BlockSpec((rows, dk), lambda b, h: (s0 + b, h)),
        pl.BlockSpec((rows, dk), lambda b, h: (s0 + b, h)),
        pl.BlockSpec((rows, dv), lambda b, h: (s0 + b, h)),
        pl.BlockSpec((rows, dv), lambda b, h: (s0 + b, h)),
        pl.BlockSpec((rows, dv), lambda b, h: (s0 + b, RET_HEADS + h)),
    ]
    args = [log_decay, q, k, v, gates, gates]
    state_spec = pl.BlockSpec((seqs, 1, 1, dk, dv), lambda b, h: (b, 0, h, 0, 0))
    if state is not None:
        in_specs += [state_spec, state_spec]
        args += list(state)
    out_shapes = [jax.ShapeDtypeStruct((batch * seq_len, RET_HEADS * dv), BF16)]
    out_specs = [pl.BlockSpec((rows, dv), lambda b, h: (b, h))]
    if emit_state:
        out_shapes += [jax.ShapeDtypeStruct((batch, 1, RET_HEADS, dk, dv), F32)] * 2
        out_specs += [state_spec, state_spec]
    return pl.pallas_call(
        functools.partial(_retention_kernel, seq_len=seq_len, seqs=seqs, has_state=state is not None,
                          emit_state=emit_state),
        out_shape=out_shapes,
        grid=(batch // seqs, RET_HEADS),
        in_specs=in_specs,
        out_specs=out_specs,
        scratch_shapes=[pltpu.VMEM((seqs, 2, dk, dv), F32), pltpu.VMEM((rows, dv), F32)],
        compiler_params=_cparams("parallel", "parallel"),
        name="retention",
    )(*args)


def _pack_bf16_halves(x):
    half = x.shape[1] // 2
    lo = lax.bitcast_convert_type(x[:, :half].astype(F32), jnp.uint32)
    hi = lax.bitcast_convert_type(x[:, half:].astype(F32), jnp.uint32)
    return (hi & jnp.uint32(0xFFFF0000)) | (lo >> 16)


def _unpack_bf16_halves(p):
    lo = lax.bitcast_convert_type(p << 16, F32).astype(BF16)
    hi = lax.bitcast_convert_type(p & jnp.uint32(0xFFFF0000), F32).astype(BF16)
    return lo, hi


def _first_max_onehot(vals):
    m = vals[0]
    for v in vals[1:]:
        m = jnp.maximum(m, v)
    onehot, taken = [], None
    for v in vals:
        hit = v == m
        if taken is None:
            onehot.append(hit)
            taken = hit
        else:
            onehot.append(hit & jnp.logical_not(taken))
            taken = taken | hit
    return m, onehot


def _pick(onehot, vals):
    out = vals[-1]
    for oh, v in zip(onehot[-2::-1], vals[-2::-1]):
        out = jnp.where(oh, v, out)
    return out


def _route_rows(x, g_ref, mod_ref, wr_ref, bias_ref, tri_ref, h_ref, idx_ref, wcol_ref, rank_ref, cnt_ref,
                ind_scr, wrow_scr):
    hb = _modulated_norm(x, g_ref[...], mod_ref, 3).astype(BF16)
    h_ref[...] = _pack_bf16_halves(hb)
    logits = _nt_dot(wr_ref[...], hb)
    s = jax.nn.sigmoid(logits)
    sel = s + bias_ref[...]
    neg_inf = jnp.full_like(sel[0:1], -jnp.inf)
    sel_rows = [sel[e:e + 1] for e in range(N_EXPERTS)]
    s_rows = [s[e:e + 1] for e in range(N_EXPERTS)]

    def top2(vals):
        m1, oh1 = _first_max_onehot(vals)
        rest = [jnp.where(o, neg_inf, v) for o, v in zip(oh1, vals)]
        m2, oh2 = _first_max_onehot(rest)
        return m1, m2, oh1, oh2

    grp_scores = []
    for g in range(N_GROUPS):
        m1, m2, _, _ = top2(sel_rows[g * EXPERTS_PER_GROUP:(g + 1) * EXPERTS_PER_GROUP])
        grp_scores.append(m1 + m2)
    _, in_grp = _first_max_onehot(grp_scores)
    cand_sel = [_pick(in_grp, [sel_rows[g * EXPERTS_PER_GROUP + k] for g in range(N_GROUPS)])
                for k in range(EXPERTS_PER_GROUP)]
    cand_s = [_pick(in_grp, [s_rows[g * EXPERTS_PER_GROUP + k] for g in range(N_GROUPS)])
              for k in range(EXPERTS_PER_GROUP)]
    _, _, oh1, oh2 = top2(cand_sel)
    w1 = _pick(oh1, cand_s)
    w2 = _pick(oh2, cand_s)
    denom = w1 + w2
    wrow_scr[...] = jnp.zeros_like(wrow_scr)
    wrow_scr[0:1, :] = w1 / denom
    wrow_scr[1:2, :] = w2 / denom
    wcol_ref[...] = jnp.transpose(wrow_scr[...])

    ints = [jnp.full(w1.shape, k, I32) for k in range(EXPERTS_PER_GROUP)]
    grp = _pick(in_grp, ints) * EXPERTS_PER_GROUP
    idx_ref[0:1, :] = grp + _pick(oh1, ints)
    idx_ref[1:2, :] = grp + _pick(oh2, ints)

    one, zero = jnp.ones_like(w1), jnp.zeros_like(w1)
    for g in range(N_GROUPS):
        for k in range(EXPERTS_PER_GROUP):
            e = g * EXPERTS_PER_GROUP + k
            ind_scr[e:e + 1, :] = jnp.where(in_grp[g] & (oh1[k] | oh2[k]), one, zero)
    ind = ind_scr[...].astype(BF16)
    ranks = jnp.dot(ind, tri_ref[...], preferred_element_type=F32)
    cnt_ref[0] = jnp.dot(ind, jnp.ones((ind.shape[1], LANES), BF16), preferred_element_type=F32)
    for slot, oh in ((0, oh1), (1, oh2)):
        r = zero
        for g in range(N_GROUPS):
            for k in range(EXPERTS_PER_GROUP):
                e = g * EXPERTS_PER_GROUP + k
                r = r + jnp.where(in_grp[g] & oh[k], ranks[e:e + 1], zero)
        rank_ref[slot:slot + 1, :] = r.astype(I32)


def _proj_route_kernel(*refs, n_x, n_a, n_ctx_tiles):
    x_refs, a_refs = refs[:n_x], refs[n_x:n_x + n_a]
    w_ref, mod_ref, g_ref, wr_ref, bias_ref, tri_ref, o_ref = refs[n_x + n_a:n_x + n_a + 7]
    route_out_refs = refs[n_x + n_a + 7:n_x + n_a + 12]
    y_scr, ind_scr, wrow_scr = refs[n_x + n_a + 12:]
    is_dec = pl.program_id(0) >= n_ctx_tiles

    def project(a_ref):
        y_scr[...] = mod_ref[0, 2:3, :] * jnp.dot(a_ref[...].astype(BF16), w_ref[...],
                                                  preferred_element_type=F32)

    def add(x_ref):
        o_ref[...] = x_ref[...] + y_scr[...]

    _by_part(is_dec, a_refs, project)
    _by_part(is_dec, x_refs, add)
    _route_rows(o_ref[...], g_ref, mod_ref, wr_ref, bias_ref, tri_ref, *route_out_refs, ind_scr, wrow_scr)


def proj_residual_route(lay, x_parts, a_parts, w, mods, ffn_g, w_router_t, router_bias):
    d = x_parts[0].shape[1]
    kdim = a_parts[0].shape[1]
    tm, t = lay.tm, lay.t
    tri = jnp.triu(jnp.ones((tm, tm), BF16), k=1)
    pair = jax.ShapeDtypeStruct((TOP_K, t), I32)
    pair_spec = pl.BlockSpec((TOP_K, tm), lambda i: (0, i))
    const = lambda shape: pl.BlockSpec(shape, lambda i: (0,) * len(shape))
    return pl.pallas_call(
        functools.partial(_proj_route_kernel, n_x=len(x_parts), n_a=len(a_parts), n_ctx_tiles=lay.n_ctx_tiles),
        out_shape=[jax.ShapeDtypeStruct((t, d), F32), jax.ShapeDtypeStruct((t, d // 2), jnp.uint32), pair,
                   jax.ShapeDtypeStruct((t, LANES), F32), pair,
                   jax.ShapeDtypeStruct((lay.n_tiles, N_EXPERTS, LANES), F32)],
        grid=(lay.n_tiles,),
        in_specs=lay.row_specs(x_parts, d) + lay.row_specs(a_parts, kdim) + [
            const((kdim, d)),
            pl.BlockSpec((1, N_MOD, d), lambda i: (lay.mod_row(i), 0, 0)),
            const((1, d)), const((N_EXPERTS, d)), const((N_EXPERTS, 1)), const((tm, tm)),
        ],
        out_specs=[pl.BlockSpec((tm, d), lambda i: (i, 0)), pl.BlockSpec((tm, d // 2), lambda i: (i, 0)), pair_spec,
                   pl.BlockSpec((tm, LANES), lambda i: (i, 0)), pair_spec,
                   pl.BlockSpec((1, N_EXPERTS, LANES), lambda i: (i, 0, 0))],
        scratch_shapes=[pltpu.VMEM((tm, d), F32), pltpu.VMEM((N_EXPERTS, tm), F32), pltpu.VMEM((LANES, tm), F32)],
        compiler_params=_cparams("arbitrary"),
        name="proj_residual_route",
    )(*x_parts, *a_parts, w, mods, ffn_g.reshape(1, d), w_router_t, router_bias.reshape(N_EXPERTS, 1), tri)


def dispatch_plan(lay, idx, rank, cnt, n_sorted):
    tm = lay.tm
    cnt_tile = cnt[:, :, 0].astype(I32)
    total = jnp.sum(cnt_tile, axis=0)
    padded = ((total + EXPERT_TILE - 1) // EXPERT_TILE) * EXPERT_TILE
    end = jnp.cumsum(padded)
    start = end - padded
    base = start[None, :] + jnp.cumsum(cnt_tile, axis=0) - cnt_tile
    idx3 = idx.reshape(TOP_K, lay.n_tiles, tm)
    pos = rank.reshape(TOP_K, lay.n_tiles, tm)
    for e in range(N_EXPERTS):
        pos = pos + jnp.where(idx3 == e, base[None, :, e, None], 0)
    tile_row = jnp.arange(n_sorted // EXPERT_TILE, dtype=I32) * EXPERT_TILE
    tile_expert = jnp.minimum(jnp.sum(end[None, :] <= tile_row[:, None], axis=1), N_EXPERTS - 1).astype(I32)
    n_valid = (end[-1] // EXPERT_TILE).astype(I32).reshape(1)
    return pos.reshape(TOP_K * lay.t).astype(I32), tile_expert, n_valid


def _sc_mesh():
    return plsc.VectorSubcoreMesh(core_axis_name="c", subcore_axis_name="s")


def _sc_worker_id():
    return lax.axis_index("s") * V7X_SC_CORES + lax.axis_index("c")


def sc_scatter_rows(src, pos, n_out_rows):
    t, d = src.shape
    n_idx = pos.shape[0]
    per_worker = n_idx // SC_WORKERS
    chunk = SC_CHUNK_ROWS
    assert n_idx % (SC_WORKERS * chunk) == 0 and t % chunk == 0

    @functools.partial(
        pl.kernel, mesh=_sc_mesh(), out_type=jax.ShapeDtypeStruct((n_out_rows, d), src.dtype),
        scratch_types=[pltpu.VMEM((chunk,), I32), pltpu.VMEM((chunk, d), src.dtype)],
        name="sc_scatter_rows")
    def scatter(src_hbm, pos_hbm, out_hbm, idx_v, rows_v):
        base = _sc_worker_id() * per_worker

        @pl.loop(0, per_worker // chunk)
        def _(i):
            a0 = base + i * chunk
            pltpu.sync_copy(pos_hbm.at[pl.ds(a0, chunk)], idx_v)
            pltpu.sync_copy(src_hbm.at[pl.ds(lax.rem(a0, t), chunk)], rows_v)
            pltpu.sync_copy(rows_v, out_hbm.at[idx_v])

    return scatter(src, pos)


def sc_gather_rows(table, idx):
    _, d = table.shape
    n_idx = idx.shape[0]
    per_worker = n_idx // SC_WORKERS
    chunk = SC_CHUNK_ROWS
    assert n_idx % (SC_WORKERS * chunk) == 0

    @functools.partial(
        pl.kernel, mesh=_sc_mesh(), out_type=jax.ShapeDtypeStruct((n_idx, d), table.dtype),
        scratch_types=[pltpu.VMEM((chunk,), I32), pltpu.VMEM((chunk, d), table.dtype)],
        name="sc_gather_rows")
    def gather(table_hbm, idx_hbm, out_hbm, idx_v, rows_v):
        base = _sc_worker_id() * per_worker

        @pl.loop(0, per_worker // chunk)
        def _(i):
            a0 = base + i * chunk
            pltpu.sync_copy(idx_hbm.at[pl.ds(a0, chunk)], idx_v)
            pltpu.sync_copy(table_hbm.at[idx_v], rows_v)
            pltpu.sync_copy(rows_v, out_hbm.at[pl.ds(a0, chunk)])

    return gather(table, idx)


def _experts_kernel(te_ref, nv_ref, x_ref, wg_ref, wu_ref, wd_ref, y_ref, wg_scr, wu_scr, wd_scr):
    i = pl.program_id(0)

    @pl.when(i < nv_ref[0])
    def _():
        @pl.when((i == 0) | (te_ref[i] != te_ref[jnp.maximum(i - 1, 0)]))
        def _():
            wg_scr[...] = wg_ref[0, 0].astype(BF16)
            wu_scr[...] = wu_ref[0, 0].astype(BF16)
            wd_scr[...] = wd_ref[0, 0].astype(BF16)

        x_lo, x_hi = _unpack_bf16_halves(x_ref[...])
        half = x_lo.shape[1]

        def in_proj(w_scr):
            return (jnp.dot(x_lo, w_scr[0:half, :], preferred_element_type=F32)
                    + jnp.dot(x_hi, w_scr[half:, :], preferred_element_type=F32))

        a = jax.nn.silu(in_proj(wg_scr)) * in_proj(wu_scr)
        y = jnp.dot(a.astype(BF16), wd_scr[...], preferred_element_type=F32)
        y_ref[...] = _pack_bf16_halves(y.astype(BF16))


def grouped_experts(xs, tile_expert, n_valid, wg, wu, wd, layer):
    n_rows = xs.shape[0]
    d, de = wg.shape[-2:]
    tm = EXPERT_TILE
    row_map = lambda i, te, nv: (jnp.minimum(i, nv[0] - 1), 0)
    grid_spec = pltpu.PrefetchScalarGridSpec(
        num_scalar_prefetch=2,
        grid=(n_rows // tm,),
        in_specs=[
            pl.BlockSpec((tm, d // 2), row_map),
            pl.BlockSpec((1, 1, d, de), lambda i, te, nv: (layer, te[i], 0, 0)),
            pl.BlockSpec((1, 1, d, de), lambda i, te, nv: (layer, te[i], 0, 0)),
            pl.BlockSpec((1, 1, de, d), lambda i, te, nv: (layer, te[i], 0, 0)),
        ],
        out_specs=pl.BlockSpec((tm, d // 2), row_map),
        scratch_shapes=[pltpu.VMEM((d, de), BF16), pltpu.VMEM((d, de), BF16), pltpu.VMEM((de, d), BF16)],
    )
    return pl.pallas_call(
        _experts_kernel,
        out_shape=jax.ShapeDtypeStruct((n_rows, d // 2), jnp.uint32),
        grid_spec=grid_spec,
        compiler_params=_cparams("arbitrary"),
        name="grouped_experts",
    )(tile_expert, n_valid, xs, wg, wu, wd)


def _combine_kernel(x_ref, y_ref, w_ref, mod_ref, fg_ref, *o_refs, final_norm, n_ctx_tiles):
    w = w_ref[...]

    def expert_out(slot):
        lo, hi = _unpack_bf16_halves(y_ref[slot])
        return jnp.concatenate([lo.astype(F32), hi.astype(F32)], axis=1)

    y = w[:, 0:1] * expert_out(0) + w[:, 1:2] * expert_out(1)
    out = x_ref[...] + mod_ref[0, 5:6, :] * y
    if final_norm:
        ms = jnp.mean(out * out, axis=-1, keepdims=True)
        out = out * lax.rsqrt(ms + NORM_EPS) * fg_ref[...]

    def store(o_ref):
        o_ref[...] = out
    _by_part(pl.program_id(0) >= n_ctx_tiles, o_refs, store)


def combine_residual(lay, x, y_pair, w_col, mods, final_g, final_norm, split):
    t, d = x.shape
    tm = lay.tm
    if split:
        out_shape = [jax.ShapeDtypeStruct((lay.t_ctx, d), F32), jax.ShapeDtypeStruct((lay.t_dec, d), F32)]
        out_specs = lay.row_specs([None, None], d)
    else:
        out_shape = [jax.ShapeDtypeStruct((t, d), F32)]
        out_specs = lay.row_specs([None], d)
    return pl.pallas_call(
        functools.partial(_combine_kernel, final_norm=final_norm, n_ctx_tiles=lay.n_ctx_tiles),
        out_shape=out_shape,
        grid=(lay.n_tiles,),
        in_specs=[
            pl.BlockSpec((tm, d), lambda i: (i, 0)),
            pl.BlockSpec((TOP_K, tm, d // 2), lambda i: (0, i, 0)),
            pl.BlockSpec((tm, LANES), lambda i: (i, 0)),
            pl.BlockSpec((1, N_MOD, d), lambda i: (lay.mod_row(i), 0, 0)),
            pl.BlockSpec((1, d), lambda i: (0, 0)),
        ],
        out_specs=out_specs,
        compiler_params=_cparams("arbitrary"),
        name="combine_residual",
    )(x, y_pair, w_col, mods, final_g.reshape(1, d))


def group_moe_residual(lay, x, routing, mods, p, layer, last):
    t, d = x.shape
    h, idx, w_col, rank, cnt = routing
    n_assign = TOP_K * t
    row_quantum = SC_WORKERS * SC_CHUNK_ROWS
    n_sorted = n_assign + N_EXPERTS * (EXPERT_TILE - 1)
    n_sorted = -(-n_sorted // row_quantum) * row_quantum
    n_sorted = -(-n_sorted // EXPERT_TILE) * EXPERT_TILE
    pos, tile_expert, n_valid = dispatch_plan(lay, idx, rank, cnt, n_sorted)
    xs = sc_scatter_rows(h, pos, n_sorted)
    ys = grouped_experts(xs, tile_expert, n_valid, p["moe_w_gate"], p["moe_w_up"], p["moe_w_down"], layer)
    y_pair = sc_gather_rows(ys, pos).reshape(TOP_K, t, d // 2)
    return combine_residual(lay, x, y_pair, w_col, mods, p["final_norm_g"], final_norm=last, split=last)


def _rope_angles(n, d):
    n_rows = n // GRID_W
    row = jnp.repeat(jnp.arange(n_rows), GRID_W).astype(F32)
    col = jnp.tile(jnp.arange(GRID_W), n_rows).astype(F32)
    nf = d // 4
    freqs = jnp.power(ROPE_BASE, -jnp.arange(nf, dtype=F32) / nf)
    ang = jnp.concatenate([row[:, None] * freqs, col[:, None] * freqs], axis=-1)
    return jnp.cos(ang), jnp.sin(ang)


def kernel(x_prompt, x_sample, cache_attn_k, cache_attn_v, state_ret_fwd, state_ret_bwd, c, c_ctx, w_ada, b_ada, norm_mix_g, norm_ffn_g, final_norm_g, da_w_qkv, da_lambda_q1, da_lambda_k1, da_lambda_q2, da_lambda_k2, da_subln_g, da_w_o, ret_w_qkv, ret_w_gate_fwd, ret_w_gate_bwd, ret_decay_fwd, ret_decay_bwd, ret_w_o, w_router, router_bias, moe_w_gate, moe_w_up, moe_w_down):
    b_ctx, n_ctx, d = x_prompt.shape
    b_dec, n_dec, _ = x_sample.shape
    past = cache_attn_k.shape[2]
    n_attn = cache_attn_k.shape[1]
    depth = w_ada.shape[0]
    assert b_dec + 1 <= MOD_ROWS
    lay = Layout(b_ctx, n_ctx, b_dec, n_dec)

    cond = jnp.zeros((MOD_ROWS, d), F32).at[0].set(c_ctx).at[1:1 + b_dec].set(c)
    mods_all = ada_modulation(cond, w_ada, b_ada)

    p = {
        "norm_ffn_g": norm_ffn_g, "final_norm_g": final_norm_g,
        "w_router_t": w_router.T.astype(BF16), "router_bias": router_bias.astype(F32),
        "moe_w_gate": moe_w_gate, "moe_w_up": moe_w_up, "moe_w_down": moe_w_down,
    }
    ret_log_decay = jnp.stack([jax.nn.log_sigmoid(ret_decay_fwd.astype(F32)),
                               jax.nn.log_sigmoid(ret_decay_bwd.astype(F32))], axis=1)
    ck_all = cache_attn_k.reshape(b_dec, n_attn, past, -1)
    cv_all = cache_attn_v.reshape(b_dec, n_attn, past, -1)

    x_parts = [x_prompt.reshape(lay.t_ctx, d), x_sample.reshape(lay.t_dec, d)]
    new_k, new_v, new_sf, new_sb = [], [], [], []
    for i in range(depth):
        mods = mods_all[i]
        j = i // 2
        if i % 2 == 0:
            lam_init = 0.8 - 0.6 * math.exp(-0.3 * i)
            qkw = DA_HEADS * 2 * DA_HEAD_DIM
            vw = DA_HEADS * DA_V_DIM
            cos, sin = _rope_angles(n_dec, DA_HEAD_DIM)
            rope = (jnp.tile(cos, (1, 4)), jnp.concatenate([-sin, sin, -sin, sin], axis=-1))
            q, k, v, k_ctx, v_ctx = norm_mod_matmul(
                lay, x_parts, norm_mix_g[i], mods, da_w_qkv[j].astype(BF16),
                [(0, qkw, "rope64", math.log2(math.e) * DA_HEAD_DIM ** -0.5, BF16, "all"),
                 (qkw, qkw, "rope64", 1.0, BF16, "all"),
                 (2 * qkw, vw, "plain", 1.0, BF16, "all"),
                 (qkw, qkw, "plain", 1.0, F32, "ctx64"), (2 * qkw, vw, "plain", 1.0, F32, "ctx")],
                rope)
            lam_params = jnp.stack([da_lambda_q1[j], da_lambda_k1[j], da_lambda_q2[j], da_lambda_k2[j]])
            mix = [diff_attention(q, k, v, 0, b_ctx, n_ctx, lam_params, da_subln_g[j], lam_init,
                                  heads_per_step=DA_HEADS),
                   diff_attention(q, k, v, lay.t_ctx, b_dec, n_dec, lam_params, da_subln_g[j], lam_init,
                                  cache=(ck_all[:, j], cv_all[:, j]))]
            mix_w = da_w_o[j].astype(BF16)
            new_k.append(k_ctx.reshape(b_ctx, n_ctx, DA_HEADS, 2, DA_HEAD_DIM))
            new_v.append(v_ctx.reshape(b_ctx, n_ctx, DA_HEADS, DA_V_DIM))
        else:
            kd = ret_w_qkv.shape[2] // 4
            dv = 2 * kd
            w_all = jnp.concatenate([ret_w_qkv[j], ret_w_gate_fwd[j], ret_w_gate_bwd[j]], axis=-1).astype(BF16)
            q, k, v, gates = norm_mod_matmul(
                lay, x_parts, norm_mix_g[i], mods, w_all,
                [(0, kd, "rope256", 1.0, BF16, "all"),
                 (kd, kd, "rope256", (kd // RET_HEADS) ** -0.5, F32, "all"),
                 (2 * kd, dv, "plain", 1.0, BF16, "all"), (2 * kd + dv, 2 * dv, "silu", 1.0, BF16, "all")],
                _rope_angles(n_dec, kd // RET_HEADS))
            ctx_seqs = math.gcd(b_ctx, max(1, n_dec // n_ctx // 2))
            o_ctx, sf, sb = retention(q, k, v, gates, 0, b_ctx, n_ctx, ret_log_decay[j], emit_state=True,
                                      seqs_per_step=ctx_seqs)
            (o_dec,) = retention(q, k, v, gates, lay.t_ctx, b_dec, n_dec, ret_log_decay[j],
                                 state=(state_ret_fwd[:, j:j + 1], state_ret_bwd[:, j:j + 1]))
            mix, mix_w = [o_ctx, o_dec], ret_w_o[j].astype(BF16)
            new_sf.append(sf)
            new_sb.append(sb)
        x, *routing = proj_residual_route(lay, x_parts, mix, mix_w, mods, norm_ffn_g[i], p["w_router_t"],
                                          p["router_bias"])
        x_parts = group_moe_residual(lay, x, routing, mods, p, i, last=(i == depth - 1))

    y_ctx, y_dec = x_parts
    return (y_ctx.reshape(b_ctx, n_ctx, d), y_dec.reshape(b_dec, n_dec, d),
            jnp.stack(new_k, axis=1), jnp.stack(new_v, axis=1),
            jnp.concatenate(new_sf, axis=1), jnp.concatenate(new_sb, axis=1))
```

```python
import functools
import math

import jax
import jax.numpy as jnp
from jax import lax
from jax.experimental import pallas as pl
from jax.experimental.pallas import tpu as pltpu
from jax.experimental.pallas import tpu_sc as plsc

F32 = jnp.float32
BF16 = jnp.bfloat16
I32 = jnp.int32

GRID_W = 64
ROPE_BASE = 10000.0
NORM_EPS = 1e-6
DA_HEADS = 8
DA_HEAD_DIM = 64
DA_V_DIM = 2 * DA_HEAD_DIM
RET_HEADS = 4
RET_CHUNK = 128
N_EXPERTS = 16
N_GROUPS = 4
EXPERTS_PER_GROUP = N_EXPERTS // N_GROUPS
TOP_K = 2
N_MOD = 6
MOD_ROWS = 16
LANES = 128

V7X_VMEM_LIMIT = 56 * 1024 * 1024
V7X_SC_CORES = 2
V7X_SC_SUBCORES = 16
SC_WORKERS = V7X_SC_CORES * V7X_SC_SUBCORES
SC_CHUNK_ROWS = 64

ROW_TILE = 512
EXPERT_TILE = 512


def _cparams(*sem):
    return pltpu.CompilerParams(dimension_semantics=sem, vmem_limit_bytes=V7X_VMEM_LIMIT)


def _nt_dot(a, b):
    return lax.dot_general(a, b, (((1,), (1,)), ((), ())), preferred_element_type=F32)


def _tn_dot(a, b):
    return lax.dot_general(a, b, (((0,), (0,)), ((), ())), preferred_element_type=F32)


class Layout:
    def __init__(self, b_ctx, n_ctx, b_dec, n_dec):
        self.b_ctx, self.n_ctx, self.b_dec, self.n_dec = b_ctx, n_ctx, b_dec, n_dec
        self.t_ctx, self.t_dec = b_ctx * n_ctx, b_dec * n_dec
        self.t = self.t_ctx + self.t_dec
        self.tm = min(ROW_TILE, n_dec, self.t_ctx)
        assert self.t_ctx % self.tm == 0 and n_dec % self.tm == 0
        assert self.t_ctx % n_dec == 0 and self.t_ctx % n_ctx == 0
        self.n_ctx_tiles = self.t_ctx // self.tm
        self.n_tiles = self.t // self.tm

    def mod_row(self, i):
        r = i * self.tm
        return jnp.where(r < self.t_ctx, 0, 1 + (r - self.t_ctx) // self.n_dec)

    def part_tile(self, part, i):
        if part == 0:
            return jnp.minimum(i, self.n_ctx_tiles - 1)
        return jnp.maximum(i - self.n_ctx_tiles, 0)

    def row_specs(self, arrays, width, extra_grid_dims=0):
        pad = (0,) * 0
        if len(arrays) == 1:
            return [pl.BlockSpec((self.tm, width), lambda i, *_: (i, 0))]
        return [pl.BlockSpec((self.tm, width), lambda i, *_, p=p: (self.part_tile(p, i), 0)) for p in (0, 1)]


def _ada_kernel(c_ref, w_ref, b_ref, o_ref):
    s = jax.nn.silu(c_ref[...]).astype(BF16)
    acc = jnp.dot(s, w_ref[0].astype(BF16), preferred_element_type=F32)
    o_ref[0] = acc + b_ref[0]


def ada_modulation(cond, w_ada, b_ada):
    depth, d, n = w_ada.shape
    tn = 1536
    out = pl.pallas_call(
        _ada_kernel,
        out_shape=jax.ShapeDtypeStruct((depth, MOD_ROWS, n), F32),
        grid=(depth, n // tn),
        in_specs=[
            pl.BlockSpec((MOD_ROWS, d), lambda l, j: (0, 0)),
            pl.BlockSpec((1, d, tn), lambda l, j: (l, 0, j)),
            pl.BlockSpec((1, 1, tn), lambda l, j: (l, 0, j)),
        ],
        out_specs=pl.BlockSpec((1, MOD_ROWS, tn), lambda l, j: (l, 0, j)),
        compiler_params=_cparams("parallel", "parallel"),
        name="ada_modulation",
    )(cond, w_ada, b_ada.reshape(depth, 1, n))
    return out.reshape(depth, MOD_ROWS, N_MOD, d)


def _modulated_norm(x, g, mod_ref, shift_idx):
    ms = jnp.mean(x * x, axis=-1, keepdims=True)
    y = x * lax.rsqrt(ms + NORM_EPS) * g
    return y * (1.0 + mod_ref[0, shift_idx + 1:shift_idx + 2, :]) + mod_ref[0, shift_idx:shift_idx + 1, :]


def _by_part(is_dec, refs, fn):
    if len(refs) == 1:
        fn(refs[0])
        return
    pl.when(jnp.logical_not(is_dec))(lambda: fn(refs[0]))
    pl.when(is_dec)(lambda: fn(refs[1]))


def _rope64(a, cos, sin_signed, first_half):
    partner = jnp.where(first_half, pltpu.roll(a, 96, 1), pltpu.roll(a, 32, 1))
    return a * cos + partner * sin_signed


def _nmm_kernel(*refs, n_x, outs, tn, n_ctx_tiles):
    x_refs = refs[:n_x]
    g_ref, mod_ref, w_ref, cos_ref, sin_ref = refs[n_x:n_x + 5]
    out_refs = refs[n_x + 5:]
    is_dec = pl.program_id(0) >= n_ctx_tiles
    tm = x_refs[0].shape[0]
    segments = sorted({(col0, width) for col0, width, _, _, _ in outs})

    def rotated(a, kind):
        cos, sin = cos_ref[...], sin_ref[...]
        if kind == "rope64":
            lane = lax.broadcasted_iota(I32, (tm, LANES), 1)
            first_half = (lane & 32) == 0
            return [(c * 128, _rope64(a[:, c * 128:(c + 1) * 128], cos, sin, first_half))
                    for c in range(tn // 128)]
        pieces = []
        for c in range(tn // 256):
            x1, x2 = a[:, c * 256:c * 256 + 128], a[:, c * 256 + 128:(c + 1) * 256]
            pieces += [(c * 256, x1 * cos - x2 * sin), (c * 256 + 128, x2 * cos + x1 * sin)]
        return pieces

    def emit(dec):
        x_ref = x_refs[(1 if dec else 0) if n_x == 2 else 0]
        h = _modulated_norm(x_ref[...], g_ref[...], mod_ref, 0).astype(BF16)
        for col0, width in segments:
            sinks = [(o_ref, o) for o_ref, o in zip(out_refs, outs)
                     if (o[0], o[1]) == (col0, width) and not (dec and o[4] != "all")]
            for blk in range(width // tn):
                acc = jnp.dot(h, w_ref[:, col0 + blk * tn:col0 + (blk + 1) * tn], preferred_element_type=F32)
                for o_ref, (_, _, kind, scale, rows) in sinks:
                    a = acc if scale == 1.0 else acc * scale
                    if rows == "ctx64":
                        n64 = width // 64
                        for c in range(tn // 64):
                            o_ref[pl.ds(blk * tn // 64 + c, tm, stride=n64), :] = (
                                a[:, c * 64:(c + 1) * 64].astype(o_ref.dtype))
                    elif kind == "silu":
                        o_ref[:, blk * tn:(blk + 1) * tn] = jax.nn.silu(a).astype(o_ref.dtype)
                    elif kind == "plain" or rows == "ctx" or not dec:
                        o_ref[:, blk * tn:(blk + 1) * tn] = a.astype(o_ref.dtype)
                    else:
                        for off, val in rotated(a, kind):
                            o_ref[:, blk * tn + off:blk * tn + off + 128] = val.astype(o_ref.dtype)

    pl.when(jnp.logical_not(is_dec))(lambda: emit(False))
    pl.when(is_dec)(lambda: emit(True))


def norm_mod_matmul(lay, x_parts, g, mods, w, outputs, rope_tables, tn=512):
    d = x_parts[0].shape[1]
    tm = lay.tm
    n_total = w.shape[1]
    outs, out_shapes, out_specs = [], [], []
    for col0, width, kind, scale, dtype, rows in outputs:
        assert width % tn == 0 and col0 % LANES == 0
        outs.append((col0, width, kind, float(scale), rows))
        if rows == "ctx64":
            n64 = width // 64
            out_shapes.append(jax.ShapeDtypeStruct((lay.t_ctx * n64, 64), dtype))
            out_specs.append(pl.BlockSpec((tm * n64, 64), lambda i: (lay.part_tile(0, i), 0)))
            continue
        n_rows = lay.t_ctx if rows == "ctx" else lay.t
        out_shapes.append(jax.ShapeDtypeStruct((n_rows, width), dtype))
        out_specs.append(lay.row_specs([None, None], width)[0] if rows == "ctx" else lay.row_specs([None], width)[0])
    blocks_per_seq = lay.n_dec // tm
    rope_spec = pl.BlockSpec((tm, LANES), lambda i: (lay.part_tile(1, i) % blocks_per_seq, 0))
    return pl.pallas_call(
        functools.partial(_nmm_kernel, n_x=len(x_parts), outs=tuple(outs), tn=tn, n_ctx_tiles=lay.n_ctx_tiles),
        out_shape=out_shapes,
        grid=(lay.n_tiles,),
        in_specs=lay.row_specs(x_parts, d) + [
            pl.BlockSpec((1, d), lambda i: (0, 0)),
            pl.BlockSpec((1, N_MOD, d), lambda i: (lay.mod_row(i), 0, 0)),
            pl.BlockSpec((d, n_total), lambda i: (0, 0), pipeline_mode=pl.Buffered(1)),
            rope_spec, rope_spec,
        ],
        out_specs=out_specs,
        compiler_params=_cparams("arbitrary"),
        name="norm_mod_matmul",
    )(*x_parts, g.reshape(1, d), mods, w, *rope_tables)


def _diff_lambda(lam_ref, lam_init):
    lp = lam_ref[...]
    return (jnp.exp(jnp.sum(lp[0:1] * lp[1:2], axis=-1, keepdims=True))
            - jnp.exp(jnp.sum(lp[2:3] * lp[3:4], axis=-1, keepdims=True)) + lam_init)


def _component_scores(q, k):
    lane = lax.broadcasted_iota(I32, q.shape, 1)
    zero = jnp.zeros_like(q)
    return [_nt_dot(jnp.where((lane < DA_HEAD_DIM) == (comp == 0), q, zero), k) for comp in range(2)]


def _diff_softmax_values(scores, lam, v, subln_g, lam_init):
    exps = []
    for s in scores:
        e = jnp.exp2(s - jnp.max(s, axis=-1, keepdims=True))
        exps.append((e, 1.0 / jnp.sum(e, axis=-1, keepdims=True)))
    c0, c1 = exps[0][1], lam * exps[1][1]
    o = jnp.dot((exps[0][0] * c0 - exps[1][0] * c1).astype(BF16), v, preferred_element_type=F32)
    ms = jnp.mean(o * o, axis=-1, keepdims=True)
    return (o * lax.rsqrt(ms + NORM_EPS) * subln_g) * (1.0 - lam_init)


def _diff_attn_kernel(lam_ref, q_ref, k_ref, v_ref, g_ref, o_ref, *, lam_init, heads_per_step):
    hw = 2 * DA_HEAD_DIM
    lam = _diff_lambda(lam_ref, lam_init)
    for hh in range(heads_per_step):
        cols = slice(hh * hw, (hh + 1) * hw)
        scores = _component_scores(q_ref[:, cols], k_ref[:, cols])
        o_ref[:, cols] = _diff_softmax_values(scores, lam, v_ref[:, cols], g_ref[...], lam_init).astype(o_ref.dtype)


def _diff_attn_cached_kernel(lam_ref, q_ref, k_ref, v_ref, ck_ref, cv_ref, g_ref, o_ref, k_scr, vt_scr, s_scr, *,
                             lam_init, sub_rows):
    seq_len, hw = q_ref.shape
    n_sub = seq_len // sub_rows
    assert n_sub % 2 == 0
    lam = _diff_lambda(lam_ref, lam_init)
    k_scr[0:seq_len, :] = k_ref[...]
    k_scr[seq_len:, :] = ck_ref[0].astype(BF16)
    vt_scr[0:hw, 0:seq_len] = jnp.transpose(v_ref[...].astype(F32)).astype(BF16)
    vt_scr[0:hw, seq_len:] = jnp.transpose(cv_ref[0]).astype(BF16)
    vt_scr[hw:, :] = jnp.ones((vt_scr.shape[0] - hw, vt_scr.shape[1]), BF16)

    def rows_of(t):
        return pl.ds(pl.multiple_of(t * sub_rows, sub_rows), sub_rows)

    def scores(t, slot):
        q = q_ref[rows_of(t), :]
        lane = lax.broadcasted_iota(I32, q.shape, 1)
        zero = jnp.zeros_like(q)
        for comp in range(2):
            s_scr[slot, comp] = _nt_dot(k_scr[...], jnp.where((lane < DA_HEAD_DIM) == (comp == 0), q, zero))

    def outputs(t, slot):
        parts = []
        for comp in range(2):
            s = s_scr[slot, comp]
            e = jnp.exp2(s - jnp.max(s, axis=0, keepdims=True)).astype(BF16)
            acc = jnp.dot(vt_scr[...], e, preferred_element_type=F32)
            parts.append(acc[0:hw] / acc[hw:hw + 1])
        o = parts[0] - lam * parts[1]
        ms = jnp.mean(o * o, axis=0, keepdims=True)
        o = (o * lax.rsqrt(ms + NORM_EPS) * g_ref[...]) * (1.0 - lam_init)
        o_ref[rows_of(t), :] = jnp.transpose(o).astype(o_ref.dtype)

    scores(0, 0)

    def body(i2, carry):
        t = 2 * i2
        scores(t + 1, 1)
        outputs(t, 0)
        scores(jnp.minimum(t + 2, n_sub - 1), 0)
        outputs(t + 1, 1)
        return carry

    lax.fori_loop(0, n_sub // 2, body, 0)


def diff_attention(q, k, v, row0, batch, seq_len, lam_params, subln_g, lam_init, cache=None, heads_per_step=1,
                   sub_rows=256):
    width = q.shape[1]
    hw = 2 * DA_HEAD_DIM
    bw = heads_per_step * hw
    assert row0 % seq_len == 0 and DA_HEADS % heads_per_step == 0
    s0 = row0 // seq_len
    seq_spec = pl.BlockSpec((seq_len, bw), lambda b, h: (s0 + b, h))
    in_specs = [pl.BlockSpec((4, DA_HEAD_DIM), lambda b, h: (0, 0)), seq_spec, seq_spec, seq_spec]
    args = [lam_params, q, k, v]
    scratch = []
    if cache is None:
        body = functools.partial(_diff_attn_kernel, lam_init=lam_init, heads_per_step=heads_per_step)
    else:
        assert heads_per_step == 1
        past = cache[0].shape[1]
        cache_spec = pl.BlockSpec((1, past, hw), lambda b, h: (b, 0, h))
        in_specs += [cache_spec, cache_spec]
        args += list(cache)
        sub_rows = min(sub_rows, seq_len // 2)
        ones_rows = 16
        scratch = [pltpu.VMEM((seq_len + past, hw), BF16), pltpu.VMEM((hw + ones_rows, seq_len + past), BF16),
                   pltpu.VMEM((2, 2, seq_len + past, sub_rows), F32)]
        body = functools.partial(_diff_attn_cached_kernel, lam_init=lam_init, sub_rows=sub_rows)
        in_specs.append(pl.BlockSpec((hw, 1), lambda b, h: (0, 0)))
        args.append(subln_g.reshape(hw, 1))
    if cache is None:
        in_specs.append(pl.BlockSpec((1, hw), lambda b, h: (0, 0)))
        args.append(subln_g.reshape(1, hw))
    return pl.pallas_call(
        body,
        out_shape=jax.ShapeDtypeStruct((batch * seq_len, width), BF16),
        grid=(batch, DA_HEADS // heads_per_step),
        in_specs=in_specs,
        out_specs=pl.BlockSpec((seq_len, bw), lambda b, h: (b, h)),
        scratch_shapes=scratch,
        compiler_params=_cparams("parallel", "parallel"),
        name="diff_attention",
    )(*args)


def _retention_kernel(*refs, seq_len, seqs, has_state, emit_state):
    lg_ref, q_ref, k_ref, v_ref, gf_ref, gb_ref = refs[:6]
    pos = 6
    if has_state:
        s0_refs = refs[6:8]
        pos = 8
    o_ref = refs[pos]
    pos += 1
    if emit_state:
        s_out_refs = refs[pos:pos + 2]
        pos += 2
    s_scr, o_scr = refs[pos:pos + 2]

    h = pl.program_id(1)
    c_len = RET_CHUNK
    n_chunks = seq_len // c_len
    assert n_chunks % 2 == 0
    row = lax.broadcasted_iota(I32, (c_len, c_len), 0)
    colm = lax.broadcasted_iota(I32, (c_len, c_len), 1)
    rel = (row - colm).astype(F32)
    posv = lax.broadcasted_iota(I32, (c_len, 1), 0).astype(F32)

    consts = []
    for backward in (False, True):
        lg = lg_ref[1 if backward else 0, h]
        if backward:
            intra = jnp.where(rel <= 0, jnp.exp(-rel * lg), 0.0)
            q_decay = jnp.exp((c_len - posv) * lg)
            k_decay = jnp.exp(posv * lg)
        else:
            intra = jnp.where(rel >= 0, jnp.exp(rel * lg), 0.0)
            q_decay = jnp.exp((posv + 1.0) * lg)
            k_decay = jnp.exp((c_len - 1.0 - posv) * lg)
        consts.append((intra, q_decay, k_decay, jnp.exp(jnp.zeros((1, 1), F32) + c_len * lg)))
        direction = 1 if backward else 0
        for sq in range(seqs):
            if has_state:
                s_scr[sq, direction] = s0_refs[direction][sq, 0, 0]
            else:
                s_scr[sq, direction] = jnp.zeros(s_scr.shape[2:], F32)

    def chunk(sq, c, backward, first_touch):
        direction = 1 if backward else 0
        intra, q_decay, k_decay, chunk_decay = consts[direction]
        gate_ref = gb_ref if backward else gf_ref
        rows = pl.ds(pl.multiple_of(sq * seq_len + c * c_len, c_len), c_len)
        qb = q_ref[rows, :].astype(BF16)
        kf = k_ref[rows, :].astype(F32)
        vb = v_ref[rows, :].astype(BF16)
        a = _nt_dot(qb, kf.astype(BF16)) * intra
        s = s_scr[sq, direction]
        o = (jnp.dot(a.astype(BF16), vb, preferred_element_type=F32)
             + jnp.dot(qb, s.astype(BF16), preferred_element_type=F32) * q_decay)
        s_scr[sq, direction] = chunk_decay * s + _tn_dot((kf * k_decay).astype(BF16), vb)
        mu = jnp.mean(o, axis=-1, keepdims=True)
        oc = o - mu
        var = jnp.mean(oc * oc, axis=-1, keepdims=True)
        gated = oc * lax.rsqrt(var + NORM_EPS) * gate_ref[rows, :].astype(F32)
        if first_touch:
            o_scr[rows, :] = gated
        else:
            o_ref[rows, :] = (o_scr[rows, :] + gated).astype(o_ref.dtype)

    def body(ci, first_touch):
        for sq in range(seqs):
            chunk(sq, ci, False, first_touch)
            chunk(sq, n_chunks - 1 - ci, True, first_touch)

    unroll = 4 if n_chunks % 8 == 0 else 1
    lax.fori_loop(0, n_chunks // 2, lambda ci, c: (body(ci, True), c)[1], 0, unroll=unroll)
    lax.fori_loop(n_chunks // 2, n_chunks, lambda ci, c: (body(ci, False), c)[1], 0, unroll=unroll)
    if emit_state:
        for d in range(2):
            for sq in range(seqs):
                s_out_refs[d][sq, 0, 0] = s_scr[sq, d]


def retention(q, k, v, gates, row0, batch, seq_len, log_decay, state=None, emit_state=False, seqs_per_step=1):
    dk = q.shape[1] // RET_HEADS
    dv = v.shape[1] // RET_HEADS
    seqs = seqs_per_step
    rows = seqs * seq_len
    assert row0 % rows == 0 and batch % seqs == 0
    s0 = row0 // rows
    in_specs = [
        pl.BlockSpec(memory_space=pltpu.SMEM),
        pl.BlockSpec((rows, dk), lambda b, h: (s0 + b, h)),
        pl.BlockSpec((rows, dk), lambda b, h: (s0 + b, h)),
        pl.BlockSpec((rows, dv), lambda b, h: (s0 + b, h)),
        pl.BlockSpec((rows, dv), lambda b, h: (s0 + b, h)),
        pl.BlockSpec((rows, dv), lambda b, h: (s0 + b, RET_HEADS + h)),
    ]
    args = [log_decay, q, k, v, gates, gates]
    state_spec = pl.BlockSpec((seqs, 1, 1, dk, dv), lambda b, h: (b, 0, h, 0, 0))
    if state is not None:
        in_specs += [state_spec, state_spec]
        args += list(state)
    out_shapes = [jax.ShapeDtypeStruct((batch * seq_len, RET_HEADS * dv), BF16)]
    out_specs = [pl.BlockSpec((rows, dv), lambda b, h: (b, h))]
    if emit_state:
        out_shapes += [jax.ShapeDtypeStruct((batch, 1, RET_HEADS, dk, dv), F32)] * 2
        out_specs += [state_spec, state_spec]
    return pl.pallas_call(
        functools.partial(_retention_kernel, seq_len=seq_len, seqs=seqs, has_state=state is not None,
                          emit_state=emit_state),
        out_shape=out_shapes,
        grid=(batch // seqs, RET_HEADS),
        in_specs=in_specs,
        out_specs=out_specs,
        scratch_shapes=[pltpu.VMEM((seqs, 2, dk, dv), F32), pltpu.VMEM((rows, dv), F32)],
        compiler_params=_cparams("parallel", "parallel"),
        name="retention",
    )(*args)


def _pack_bf16_halves(x):
    half = x.shape[1] // 2
    lo = lax.bitcast_convert_type(x[:, :half].astype(F32), jnp.uint32)
    hi = lax.bitcast_convert_type(x[:, half:].astype(F32), jnp.uint32)
    return (hi & jnp.uint32(0xFFFF0000)) | (lo >> 16)


def _unpack_bf16_halves(p):
    lo = lax.bitcast_convert_type(p << 16, F32).astype(BF16)
    hi = lax.bitcast_convert_type(p & jnp.uint32(0xFFFF0000), F32).astype(BF16)
    return lo, hi


def _first_max_onehot(vals):
    m = vals[0]
    for v in vals[1:]:
        m = jnp.maximum(m, v)
    onehot, taken = [], None
    for v in vals:
        hit = v == m
        if taken is None:
            onehot.append(hit)
            taken = hit
        else:
            onehot.append(hit & jnp.logical_not(taken))
            taken = taken | hit
    return m, onehot


def _pick(onehot, vals):
    out = vals[-1]
    for oh, v in zip(onehot[-2::-1], vals[-2::-1]):
        out = jnp.where(oh, v, out)
    return out


def _route_rows(x, g_ref, mod_ref, wr_ref, bias_ref, tri_ref, h_ref, idx_ref, wcol_ref, rank_ref, cnt_ref,
                ind_scr, wrow_scr):
    hb = _modulated_norm(x, g_ref[...], mod_ref, 3).astype(BF16)
    h_ref[...] = _pack_bf16_halves(hb)
    logits = _nt_dot(wr_ref[...], hb)
    s = jax.nn.sigmoid(logits)
    sel = s + bias_ref[...]
    neg_inf = jnp.full_like(sel[0:1], -jnp.inf)
    sel_rows = [sel[e:e + 1] for e in range(N_EXPERTS)]
    s_rows = [s[e:e + 1] for e in range(N_EXPERTS)]

    def top2(vals):
        m1, oh1 = _first_max_onehot(vals)
        rest = [jnp.where(o, neg_inf, v) for o, v in zip(oh1, vals)]
        m2, oh2 = _first_max_onehot(rest)
        return m1, m2, oh1, oh2

    grp_scores = []
    for g in range(N_GROUPS):
        m1, m2, _, _ = top2(sel_rows[g * EXPERTS_PER_GROUP:(g + 1) * EXPERTS_PER_GROUP])
        grp_scores.append(m1 + m2)
    _, in_grp = _first_max_onehot(grp_scores)
    cand_sel = [_pick(in_grp, [sel_rows[g * EXPERTS_PER_GROUP + k] for g in range(N_GROUPS)])
                for k in range(EXPERTS_PER_GROUP)]
    cand_s = [_pick(in_grp, [s_rows[g * EXPERTS_PER_GROUP + k] for g in range(N_GROUPS)])
              for k in range(EXPERTS_PER_GROUP)]
    _, _, oh1, oh2 = top2(cand_sel)
    w1 = _pick(oh1, cand_s)
    w2 = _pick(oh2, cand_s)
    denom = w1 + w2
    wrow_scr[...] = jnp.zeros_like(wrow_scr)
    wrow_scr[0:1, :] = w1 / denom
    wrow_scr[1:2, :] = w2 / denom
    wcol_ref[...] = jnp.transpose(wrow_scr[...])

    ints = [jnp.full(w1.shape, k, I32) for k in range(EXPERTS_PER_GROUP)]
    grp = _pick(in_grp, ints) * EXPERTS_PER_GROUP
    idx_ref[0:1, :] = grp + _pick(oh1, ints)
    idx_ref[1:2, :] = grp + _pick(oh2, ints)

    one, zero = jnp.ones_like(w1), jnp.zeros_like(w1)
    for g in range(N_GROUPS):
        for k in range(EXPERTS_PER_GROUP):
            e = g * EXPERTS_PER_GROUP + k
            ind_scr[e:e + 1, :] = jnp.where(in_grp[g] & (oh1[k] | oh2[k]), one, zero)
    ind = ind_scr[...].astype(BF16)
    ranks = jnp.dot(ind, tri_ref[...], preferred_element_type=F32)
    cnt_ref[0] = jnp.dot(ind, jnp.ones((ind.shape[1], LANES), BF16), preferred_element_type=F32)
    for slot, oh in ((0, oh1), (1, oh2)):
        r = zero
        for g in range(N_GROUPS):
            for k in range(EXPERTS_PER_GROUP):
                e = g * EXPERTS_PER_GROUP + k
                r = r + jnp.where(in_grp[g] & oh[k], ranks[e:e + 1], zero)
        rank_ref[slot:slot + 1, :] = r.astype(I32)


def _proj_route_kernel(*refs, n_x, n_a, n_ctx_tiles):
    x_refs, a_refs = refs[:n_x], refs[n_x:n_x + n_a]
    w_ref, mod_ref, g_ref, wr_ref, bias_ref, tri_ref, o_ref = refs[n_x + n_a:n_x + n_a + 7]
    route_out_refs = refs[n_x + n_a + 7:n_x + n_a + 12]
    y_scr, ind_scr, wrow_scr = refs[n_x + n_a + 12:]
    is_dec = pl.program_id(0) >= n_ctx_tiles

    def project(a_ref):
        y_scr[...] = mod_ref[0, 2:3, :] * jnp.dot(a_ref[...].astype(BF16), w_ref[...],
                                                  preferred_element_type=F32)

    def add(x_ref):
        o_ref[...] = x_ref[...] + y_scr[...]

    _by_part(is_dec, a_refs, project)
    _by_part(is_dec, x_refs, add)
    _route_rows(o_ref[...], g_ref, mod_ref, wr_ref, bias_ref, tri_ref, *route_out_refs, ind_scr, wrow_scr)


def proj_residual_route(lay, x_parts, a_parts, w, mods, ffn_g, w_router_t, router_bias):
    d = x_parts[0].shape[1]
    kdim = a_parts[0].shape[1]
    tm, t = lay.tm, lay.t
    tri = jnp.triu(jnp.ones((tm, tm), BF16), k=1)
    pair = jax.ShapeDtypeStruct((TOP_K, t), I32)
    pair_spec = pl.BlockSpec((TOP_K, tm), lambda i: (0, i))
    const = lambda shape: pl.BlockSpec(shape, lambda i: (0,) * len(shape))
    return pl.pallas_call(
        functools.partial(_proj_route_kernel, n_x=len(x_parts), n_a=len(a_parts), n_ctx_tiles=lay.n_ctx_tiles),
        out_shape=[jax.ShapeDtypeStruct((t, d), F32), jax.ShapeDtypeStruct((t, d // 2), jnp.uint32), pair,
                   jax.ShapeDtypeStruct((t, LANES), F32), pair,
                   jax.ShapeDtypeStruct((lay.n_tiles, N_EXPERTS, LANES), F32)],
        grid=(lay.n_tiles,),
        in_specs=lay.row_specs(x_parts, d) + lay.row_specs(a_parts, kdim) + [
            const((kdim, d)),
            pl.BlockSpec((1, N_MOD, d), lambda i: (lay.mod_row(i), 0, 0)),
            const((1, d)), const((N_EXPERTS, d)), const((N_EXPERTS, 1)), const((tm, tm)),
        ],
        out_specs=[pl.BlockSpec((tm, d), lambda i: (i, 0)), pl.BlockSpec((tm, d // 2), lambda i: (i, 0)), pair_spec,
                   pl.BlockSpec((tm, LANES), lambda i: (i, 0)), pair_spec,
                   pl.BlockSpec((1, N_EXPERTS, LANES), lambda i: (i, 0, 0))],
        scratch_shapes=[pltpu.VMEM((tm, d), F32), pltpu.VMEM((N_EXPERTS, tm), F32), pltpu.VMEM((LANES, tm), F32)],
        compiler_params=_cparams("arbitrary"),
        name="proj_residual_route",
    )(*x_parts, *a_parts, w, mods, ffn_g.reshape(1, d), w_router_t, router_bias.reshape(N_EXPERTS, 1), tri)


def dispatch_plan(lay, idx, rank, cnt, n_sorted):
    tm = lay.tm
    cnt_tile = cnt[:, :, 0].astype(I32)
    total = jnp.sum(cnt_tile, axis=0)
    padded = ((total + EXPERT_TILE - 1) // EXPERT_TILE) * EXPERT_TILE
    end = jnp.cumsum(padded)
    start = end - padded
    base = start[None, :] + jnp.cumsum(cnt_tile, axis=0) - cnt_tile
    idx3 = idx.reshape(TOP_K, lay.n_tiles, tm)
    pos = rank.reshape(TOP_K, lay.n_tiles, tm)
    for e in range(N_EXPERTS):
        pos = pos + jnp.where(idx3 == e, base[None, :, e, None], 0)
    tile_row = jnp.arange(n_sorted // EXPERT_TILE, dtype=I32) * EXPERT_TILE
    tile_expert = jnp.minimum(jnp.sum(end[None, :] <= tile_row[:, None], axis=1), N_EXPERTS - 1).astype(I32)
    n_valid = (end[-1] // EXPERT_TILE).astype(I32).reshape(1)
    return pos.reshape(TOP_K * lay.t).astype(I32), tile_expert, n_valid


def _sc_mesh():
    return plsc.VectorSubcoreMesh(core_axis_name="c", subcore_axis_name="s")


def _sc_worker_id():
    return lax.axis_index("s") * V7X_SC_CORES + lax.axis_index("c")


def _sc_two_stage_pipeline(n_chunks, fetch_idx, load, store):
    assert n_chunks % 2 == 0
    fetch_idx(0, 0)
    load(0, 0).start()

    @pl.loop(0, n_chunks // 2)
    def _(p):
        j = 2 * p

        @pl.when(p > 0)
        def _():
            store(j - 1, 1).wait()

        fetch_idx(j + 1, 1)
        load(j + 1, 1).start()
        load(j, 0).wait()
        store(j, 0).start()
        store(j, 0).wait()

        @pl.when(j + 2 < n_chunks)
        def _():
            fetch_idx(j + 2, 0)
            load(j + 2, 0).start()

        load(j + 1, 1).wait()
        store(j + 1, 1).start()

    store(n_chunks - 1, 1).wait()


_SC_SCRATCH = lambda chunk, d, dtype: [
    pltpu.VMEM((chunk,), I32), pltpu.VMEM((chunk,), I32),
    pltpu.VMEM((chunk, d), dtype), pltpu.VMEM((chunk, d), dtype),
    pltpu.SemaphoreType.DMA, pltpu.SemaphoreType.DMA, pltpu.SemaphoreType.DMA, pltpu.SemaphoreType.DMA]


def sc_scatter_rows(src, pos, n_out_rows):
    t, d = src.shape
    n_idx = pos.shape[0]
    per_worker = n_idx // SC_WORKERS
    chunk = SC_CHUNK_ROWS
    assert n_idx % (SC_WORKERS * chunk * 2) == 0 and t % chunk == 0

    @functools.partial(
        pl.kernel, mesh=_sc_mesh(), out_type=jax.ShapeDtypeStruct((n_out_rows, d), src.dtype),
        scratch_types=_SC_SCRATCH(chunk, d, src.dtype), name="sc_scatter_rows")
    def scatter(src_hbm, pos_hbm, out_hbm, idx_a, idx_b, rows_a, rows_b, lsem_a, lsem_b, ssem_a, ssem_b):
        base = _sc_worker_id() * per_worker
        idx, rows, lsem, ssem = (idx_a, idx_b), (rows_a, rows_b), (lsem_a, lsem_b), (ssem_a, ssem_b)

        def fetch_idx(j, b):
            pltpu.sync_copy(pos_hbm.at[pl.ds(base + j * chunk, chunk)], idx[b])

        def load(j, b):
            return pltpu.make_async_copy(src_hbm.at[pl.ds(lax.rem(base + j * chunk, t), chunk)], rows[b], lsem[b])

        def store(j, b):
            return pltpu.make_async_copy(rows[b], out_hbm.at[idx[b]], ssem[b])

        _sc_two_stage_pipeline(per_worker // chunk, fetch_idx, load, store)

    return scatter(src, pos)


def sc_gather_rows(table, idx):
    _, d = table.shape
    n_idx = idx.shape[0]
    per_worker = n_idx // SC_WORKERS
    chunk = SC_CHUNK_ROWS
    assert n_idx % (SC_WORKERS * chunk * 2) == 0

    @functools.partial(
        pl.kernel, mesh=_sc_mesh(), out_type=jax.ShapeDtypeStruct((n_idx, d), table.dtype),
        scratch_types=_SC_SCRATCH(chunk, d, table.dtype), name="sc_gather_rows")
    def gather(table_hbm, idx_hbm, out_hbm, idx_a, idx_b, rows_a, rows_b, lsem_a, lsem_b, ssem_a, ssem_b):
        base = _sc_worker_id() * per_worker
        idx, rows, lsem, ssem = (idx_a, idx_b), (rows_a, rows_b), (lsem_a, lsem_b), (ssem_a, ssem_b)

        def fetch_idx(j, b):
            pltpu.sync_copy(idx_hbm.at[pl.ds(base + j * chunk, chunk)], idx[b])

        def load(j, b):
            return pltpu.make_async_copy(table_hbm.at[idx[b]], rows[b], lsem[b])

        def store(j, b):
            return pltpu.make_async_copy(rows[b], out_hbm.at[pl.ds(base + j * chunk, chunk)], ssem[b])

        _sc_two_stage_pipeline(per_worker // chunk, fetch_idx, load, store)

    return gather(table, idx)


def _experts_kernel(te_ref, nv_ref, x_ref, wg_ref, wu_ref, wd_ref, y_ref, wg_scr, wu_scr, wd_scr):
    i = pl.program_id(0)

    @pl.when(i < nv_ref[0])
    def _():
        @pl.when((i == 0) | (te_ref[i] != te_ref[jnp.maximum(i - 1, 0)]))
        def _():
            wg_scr[...] = wg_ref[0, 0].astype(BF16)
            wu_scr[...] = wu_ref[0, 0].astype(BF16)
            wd_scr[...] = wd_ref[0, 0].astype(BF16)

        x_lo, x_hi = _unpack_bf16_halves(x_ref[...])
        half = x_lo.shape[1]

        def in_proj(w_scr):
            return (jnp.dot(x_lo, w_scr[0:half, :], preferred_element_type=F32)
                    + jnp.dot(x_hi, w_scr[half:, :], preferred_element_type=F32))

        a = jax.nn.silu(in_proj(wg_scr)) * in_proj(wu_scr)
        y = jnp.dot(a.astype(BF16), wd_scr[...], preferred_element_type=F32)
        y_ref[...] = _pack_bf16_halves(y.astype(BF16))


def grouped_experts(xs, tile_expert, n_valid, wg, wu, wd, layer):
    n_rows = xs.shape[0]
    d, de = wg.shape[-2:]
    tm = EXPERT_TILE
    row_map = lambda i, te, nv: (jnp.minimum(i, nv[0] - 1), 0)
    grid_spec = pltpu.PrefetchScalarGridSpec(
        num_scalar_prefetch=2,
        grid=(n_rows // tm,),
        in_specs=[
            pl.BlockSpec((tm, d // 2), row_map),
            pl.BlockSpec((1, 1, d, de), lambda i, te, nv: (layer, te[i], 0, 0)),
            pl.BlockSpec((1, 1, d, de), lambda i, te, nv: (layer, te[i], 0, 0)),
            pl.BlockSpec((1, 1, de, d), lambda i, te, nv: (layer, te[i], 0, 0)),
        ],
        out_specs=pl.BlockSpec((tm, d // 2), row_map),
        scratch_shapes=[pltpu.VMEM((d, de), BF16), pltpu.VMEM((d, de), BF16), pltpu.VMEM((de, d), BF16)],
    )
    return pl.pallas_call(
        _experts_kernel,
        out_shape=jax.ShapeDtypeStruct((n_rows, d // 2), jnp.uint32),
        grid_spec=grid_spec,
        compiler_params=_cparams("arbitrary"),
        name="grouped_experts",
    )(tile_expert, n_valid, xs, wg, wu, wd)


def _combine_kernel(x_ref, y_ref, w_ref, mod_ref, fg_ref, *o_refs, final_norm, n_ctx_tiles):
    w = w_ref[...]

    def expert_out(slot):
        lo, hi = _unpack_bf16_halves(y_ref[slot])
        return jnp.concatenate([lo.astype(F32), hi.astype(F32)], axis=1)

    y = w[:, 0:1] * expert_out(0) + w[:, 1:2] * expert_out(1)
    out = x_ref[...] + mod_ref[0, 5:6, :] * y
    if final_norm:
        ms = jnp.mean(out * out, axis=-1, keepdims=True)
        out = out * lax.rsqrt(ms + NORM_EPS) * fg_ref[...]

    def store(o_ref):
        o_ref[...] = out
    _by_part(pl.program_id(0) >= n_ctx_tiles, o_refs, store)


def combine_residual(lay, x, y_pair, w_col, mods, final_g, final_norm, split):
    t, d = x.shape
    tm = lay.tm
    if split:
        out_shape = [jax.ShapeDtypeStruct((lay.t_ctx, d), F32), jax.ShapeDtypeStruct((lay.t_dec, d), F32)]
        out_specs = lay.row_specs([None, None], d)
    else:
        out_shape = [jax.ShapeDtypeStruct((t, d), F32)]
        out_specs = lay.row_specs([None], d)
    return pl.pallas_call(
        functools.partial(_combine_kernel, final_norm=final_norm, n_ctx_tiles=lay.n_ctx_tiles),
        out_shape=out_shape,
        grid=(lay.n_tiles,),
        in_specs=[
            pl.BlockSpec((tm, d), lambda i: (i, 0)),
            pl.BlockSpec((TOP_K, tm, d // 2), lambda i: (0, i, 0)),
            pl.BlockSpec((tm, LANES), lambda i: (i, 0)),
            pl.BlockSpec((1, N_MOD, d), lambda i: (lay.mod_row(i), 0, 0)),
            pl.BlockSpec((1, d), lambda i: (0, 0)),
        ],
        out_specs=out_specs,
        compiler_params=_cparams("arbitrary"),
        name="combine_residual",
    )(x, y_pair, w_col, mods, final_g.reshape(1, d))


def group_moe_residual(lay, x, routing, mods, p, layer, last):
    t, d = x.shape
    h, idx, w_col, rank, cnt = routing
    n_assign = TOP_K * t
    row_quantum = SC_WORKERS * SC_CHUNK_ROWS
    n_sorted = n_assign + N_EXPERTS * (EXPERT_TILE - 1)
    n_sorted = -(-n_sorted // row_quantum) * row_quantum
    n_sorted = -(-n_sorted // EXPERT_TILE) * EXPERT_TILE
    pos, tile_expert, n_valid = dispatch_plan(lay, idx, rank, cnt, n_sorted)
    xs = sc_scatter_rows(h, pos, n_sorted)
    ys = grouped_experts(xs, tile_expert, n_valid, p["moe_w_gate"], p["moe_w_up"], p["moe_w_down"], layer)
    y_pair = sc_gather_rows(ys, pos).reshape(TOP_K, t, d // 2)
    return combine_residual(lay, x, y_pair, w_col, mods, p["final_norm_g"], final_norm=last, split=last)


def _rope_angles(n, d):
    n_rows = n // GRID_W
    row = jnp.repeat(jnp.arange(n_rows), GRID_W).astype(F32)
    col = jnp.tile(jnp.arange(GRID_W), n_rows).astype(F32)
    nf = d // 4
    freqs = jnp.power(ROPE_BASE, -jnp.arange(nf, dtype=F32) / nf)
    ang = jnp.concatenate([row[:, None] * freqs, col[:, None] * freqs], axis=-1)
    return jnp.cos(ang), jnp.sin(ang)


def kernel(x_prompt, x_sample, cache_attn_k, cache_attn_v, state_ret_fwd, state_ret_bwd, c, c_ctx, w_ada, b_ada, norm_mix_g, norm_ffn_g, final_norm_g, da_w_qkv, da_lambda_q1, da_lambda_k1, da_lambda_q2, da_lambda_k2, da_subln_g, da_w_o, ret_w_qkv, ret_w_gate_fwd, ret_w_gate_bwd, ret_decay_fwd, ret_decay_bwd, ret_w_o, w_router, router_bias, moe_w_gate, moe_w_up, moe_w_down):
    b_ctx, n_ctx, d = x_prompt.shape
    b_dec, n_dec, _ = x_sample.shape
    past = cache_attn_k.shape[2]
    n_attn = cache_attn_k.shape[1]
    depth = w_ada.shape[0]
    assert b_dec + 1 <= MOD_ROWS
    lay = Layout(b_ctx, n_ctx, b_dec, n_dec)

    cond = jnp.zeros((MOD_ROWS, d), F32).at[0].set(c_ctx).at[1:1 + b_dec].set(c)
    mods_all = ada_modulation(cond, w_ada, b_ada)

    p = {
        "norm_ffn_g": norm_ffn_g, "final_norm_g": final_norm_g,
        "w_router_t": w_router.T.astype(BF16), "router_bias": router_bias.astype(F32),
        "moe_w_gate": moe_w_gate, "moe_w_up": moe_w_up, "moe_w_down": moe_w_down,
    }
    ret_log_decay = jnp.stack([jax.nn.log_sigmoid(ret_decay_fwd.astype(F32)),
                               jax.nn.log_sigmoid(ret_decay_bwd.astype(F32))], axis=1)
    ck_all = cache_attn_k.reshape(b_dec, n_attn, past, -1)
    cv_all = cache_attn_v.reshape(b_dec, n_attn, past, -1)

    x_parts = [x_prompt.reshape(lay.t_ctx, d), x_sample.reshape(lay.t_dec, d)]
    new_k, new_v, new_sf, new_sb = [], [], [], []
    for i in range(depth):
        mods = mods_all[i]
        j = i // 2
        if i % 2 == 0:
            lam_init = 0.8 - 0.6 * math.exp(-0.3 * i)
            qkw = DA_HEADS * 2 * DA_HEAD_DIM
            vw = DA_HEADS * DA_V_DIM
            cos, sin = _rope_angles(n_dec, DA_HEAD_DIM)
            rope = (jnp.tile(cos, (1, 4)), jnp.concatenate([-sin, sin, -sin, sin], axis=-1))
            q, k, v, k_ctx, v_ctx = norm_mod_matmul(
                lay, x_parts, norm_mix_g[i], mods, da_w_qkv[j].astype(BF16),
                [(0, qkw, "rope64", math.log2(math.e) * DA_HEAD_DIM ** -0.5, BF16, "all"),
                 (qkw, qkw, "rope64", 1.0, BF16, "all"),
                 (2 * qkw, vw, "plain", 1.0, BF16, "all"),
                 (qkw, qkw, "plain", 1.0, F32, "ctx64"), (2 * qkw, vw, "plain", 1.0, F32, "ctx")],
                rope)
            lam_params = jnp.stack([da_lambda_q1[j], da_lambda_k1[j], da_lambda_q2[j], da_lambda_k2[j]])
            mix = [diff_attention(q, k, v, 0, b_ctx, n_ctx, lam_params, da_subln_g[j], lam_init,
                                  heads_per_step=DA_HEADS),
                   diff_attention(q, k, v, lay.t_ctx, b_dec, n_dec, lam_params, da_subln_g[j], lam_init,
                                  cache=(ck_all[:, j], cv_all[:, j]))]
            mix_w = da_w_o[j].astype(BF16)
            new_k.append(k_ctx.reshape(b_ctx, n_ctx, DA_HEADS, 2, DA_HEAD_DIM))
            new_v.append(v_ctx.reshape(b_ctx, n_ctx, DA_HEADS, DA_V_DIM))
        else:
            kd = ret_w_qkv.shape[2] // 4
            dv = 2 * kd
            w_all = jnp.concatenate([ret_w_qkv[j], ret_w_gate_fwd[j], ret_w_gate_bwd[j]], axis=-1).astype(BF16)
            q, k, v, gates = norm_mod_matmul(
                lay, x_parts, norm_mix_g[i], mods, w_all,
                [(0, kd, "rope256", 1.0, BF16, "all"),
                 (kd, kd, "rope256", (kd // RET_HEADS) ** -0.5, F32, "all"),
                 (2 * kd, dv, "plain", 1.0, BF16, "all"), (2 * kd + dv, 2 * dv, "silu", 1.0, BF16, "all")],
                _rope_angles(n_dec, kd // RET_HEADS))
            ctx_seqs = math.gcd(b_ctx, max(1, n_dec // n_ctx // 2))
            o_ctx, sf, sb = retention(q, k, v, gates, 0, b_ctx, n_ctx, ret_log_decay[j], emit_state=True,
                                      seqs_per_step=ctx_seqs)
            (o_dec,) = retention(q, k, v, gates, lay.t_ctx, b_dec, n_dec, ret_log_decay[j],
                                 state=(state_ret_fwd[:, j:j + 1], state_ret_bwd[:, j:j + 1]))
            mix, mix_w = [o_ctx, o_dec], ret_w_o[j].astype(BF16)
            new_sf.append(sf)
            new_sb.append(sb)
        x, *routing = proj_residual_route(lay, x_parts, mix, mix_w, mods, norm_ffn_g[i], p["w_router_t"],
                                          p["router_bias"])
        x_parts = group_moe_residual(lay, x, routing, mods, p, i, last=(i == depth - 1))

    y_ctx, y_dec = x_parts
    return (y_ctx.reshape(b_ctx, n_ctx, d), y_dec.reshape(b_dec, n_dec, d),
            jnp.stack(new_k, axis=1), jnp.stack(new_v, axis=1),
            jnp.concatenate(new_sf, axis=1), jnp.concatenate(new_sb, axis=1))
```

```python
import functools
import math

import jax
import jax.numpy as jnp
from jax import lax
from jax.experimental import pallas as pl
from jax.experimental.pallas import tpu as pltpu
from jax.experimental.pallas import tpu_sc as plsc

F32 = jnp.float32
BF16 = jnp.bfloat16
I32 = jnp.int32

GRID_W = 64
ROPE_BASE = 10000.0
NORM_EPS = 1e-6
DA_HEADS = 8
DA_HEAD_DIM = 64
DA_V_DIM = 2 * DA_HEAD_DIM
RET_HEADS = 4
RET_CHUNK = 128
N_EXPERTS = 16
N_GROUPS = 4
EXPERTS_PER_GROUP = N_EXPERTS // N_GROUPS
TOP_K = 2
N_MOD = 6
MOD_ROWS = 16
LANES = 128

V7X_VMEM_LIMIT = 56 * 1024 * 1024
V7X_SC_CORES = 2
V7X_SC_SUBCORES = 16
SC_WORKERS = V7X_SC_CORES * V7X_SC_SUBCORES
SC_CHUNK_ROWS = 64

ROW_TILE = 512
EXPERT_TILE = 512


def _cparams(*sem):
    return pltpu.CompilerParams(dimension_semantics=sem, vmem_limit_bytes=V7X_VMEM_LIMIT)


def _nt_dot(a, b):
    return lax.dot_general(a, b, (((1,), (1,)), ((), ())), preferred_element_type=F32)


def _tn_dot(a, b):
    return lax.dot_general(a, b, (((0,), (0,)), ((), ())), preferred_element_type=F32)


class Layout:
    def __init__(self, b_ctx, n_ctx, b_dec, n_dec):
        self.b_ctx, self.n_ctx, self.b_dec, self.n_dec = b_ctx, n_ctx, b_dec, n_dec
        self.t_ctx, self.t_dec = b_ctx * n_ctx, b_dec * n_dec
        self.t = self.t_ctx + self.t_dec
        self.tm = min(ROW_TILE, n_dec, self.t_ctx)
        assert self.t_ctx % self.tm == 0 and n_dec % self.tm == 0
        assert self.t_ctx % n_dec == 0 and self.t_ctx % n_ctx == 0
        self.n_ctx_tiles = self.t_ctx // self.tm
        self.n_tiles = self.t // self.tm

    def mod_row(self, i):
        r = i * self.tm
        return jnp.where(r < self.t_ctx, 0, 1 + (r - self.t_ctx) // self.n_dec)

    def part_tile(self, part, i):
        if part == 0:
            return jnp.minimum(i, self.n_ctx_tiles - 1)
        return jnp.maximum(i - self.n_ctx_tiles, 0)

    def row_specs(self, arrays, width, extra_grid_dims=0):
        pad = (0,) * 0
        if len(arrays) == 1:
            return [pl.BlockSpec((self.tm, width), lambda i, *_: (i, 0))]
        return [pl.BlockSpec((self.tm, width), lambda i, *_, p=p: (self.part_tile(p, i), 0)) for p in (0, 1)]


def _ada_kernel(c_ref, w_ref, b_ref, o_ref):
    s = jax.nn.silu(c_ref[...]).astype(BF16)
    acc = jnp.dot(s, w_ref[0].astype(BF16), preferred_element_type=F32)
    o_ref[0] = acc + b_ref[0]


def ada_modulation(cond, w_ada, b_ada):
    depth, d, n = w_ada.shape
    tn = 1536
    out = pl.pallas_call(
        _ada_kernel,
        out_shape=jax.ShapeDtypeStruct((depth, MOD_ROWS, n), F32),
        grid=(depth, n // tn),
        in_specs=[
            pl.BlockSpec((MOD_ROWS, d), lambda l, j: (0, 0)),
            pl.BlockSpec((1, d, tn), lambda l, j: (l, 0, j)),
            pl.BlockSpec((1, 1, tn), lambda l, j: (l, 0, j)),
        ],
        out_specs=pl.BlockSpec((1, MOD_ROWS, tn), lambda l, j: (l, 0, j)),
        compiler_params=_cparams("parallel", "parallel"),
        name="ada_modulation",
    )(cond, w_ada, b_ada.reshape(depth, 1, n))
    return out.reshape(depth, MOD_ROWS, N_MOD, d)


def _modulated_norm(x, g, mod_ref, shift_idx):
    ms = jnp.mean(x * x, axis=-1, keepdims=True)
    y = x * lax.rsqrt(ms + NORM_EPS) * g
    return y * (1.0 + mod_ref[0, shift_idx + 1:shift_idx + 2, :]) + mod_ref[0, shift_idx:shift_idx + 1, :]


def _by_part(is_dec, refs, fn):
    if len(refs) == 1:
        fn(refs[0])
        return
    pl.when(jnp.logical_not(is_dec))(lambda: fn(refs[0]))
    pl.when(is_dec)(lambda: fn(refs[1]))


def _rope64(a, cos, sin_signed, first_half):
    partner = jnp.where(first_half, pltpu.roll(a, 96, 1), pltpu.roll(a, 32, 1))
    return a * cos + partner * sin_signed


def _nmm_kernel(*refs, n_x, outs, tn, n_ctx_tiles):
    x_refs = refs[:n_x]
    g_ref, mod_ref, w_ref, cos_ref, sin_ref = refs[n_x:n_x + 5]
    out_refs = refs[n_x + 5:]
    is_dec = pl.program_id(0) >= n_ctx_tiles
    tm = x_refs[0].shape[0]
    segments = sorted({(col0, width) for col0, width, _, _, _ in outs})

    def rotated(a, kind):
        cos, sin = cos_ref[...], sin_ref[...]
        if kind == "rope64":
            lane = lax.broadcasted_iota(I32, (tm, LANES), 1)
            first_half = (lane & 32) == 0
            return [(c * 128, _rope64(a[:, c * 128:(c + 1) * 128], cos, sin, first_half))
                    for c in range(tn // 128)]
        pieces = []
        for c in range(tn // 256):
            x1, x2 = a[:, c * 256:c * 256 + 128], a[:, c * 256 + 128:(c + 1) * 256]
            pieces += [(c * 256, x1 * cos - x2 * sin), (c * 256 + 128, x2 * cos + x1 * sin)]
        return pieces

    def emit(dec):
        x_ref = x_refs[(1 if dec else 0) if n_x == 2 else 0]
        h = _modulated_norm(x_ref[...], g_ref[...], mod_ref, 0).astype(BF16)
        for col0, width in segments:
            sinks = [(o_ref, o) for o_ref, o in zip(out_refs, outs)
                     if (o[0], o[1]) == (col0, width) and not (dec and o[4] != "all")]
            for blk in range(width // tn):
                acc = jnp.dot(h, w_ref[:, col0 + blk * tn:col0 + (blk + 1) * tn], preferred_element_type=F32)
                for o_ref, (_, _, kind, scale, rows) in sinks:
                    a = acc if scale == 1.0 else acc * scale
                    if rows == "ctx64":
                        n64 = width // 64
                        for c in range(tn // 64):
                            o_ref[pl.ds(blk * tn // 64 + c, tm, stride=n64), :] = (
                                a[:, c * 64:(c + 1) * 64].astype(o_ref.dtype))
                    elif kind == "silu":
                        o_ref[:, blk * tn:(blk + 1) * tn] = jax.nn.silu(a).astype(o_ref.dtype)
                    elif kind == "plain" or rows == "ctx" or not dec:
                        o_ref[:, blk * tn:(blk + 1) * tn] = a.astype(o_ref.dtype)
                    else:
                        for off, val in rotated(a, kind):
                            o_ref[:, blk * tn + off:blk * tn + off + 128] = val.astype(o_ref.dtype)

    pl.when(jnp.logical_not(is_dec))(lambda: emit(False))
    pl.when(is_dec)(lambda: emit(True))


def norm_mod_matmul(lay, x_parts, g, mods, w, outputs, rope_tables, tn=512):
    d = x_parts[0].shape[1]
    tm = lay.tm
    n_total = w.shape[1]
    outs, out_shapes, out_specs = [], [], []
    for col0, width, kind, scale, dtype, rows in outputs:
        assert width % tn == 0 and col0 % LANES == 0
        outs.append((col0, width, kind, float(scale), rows))
        if rows == "ctx64":
            n64 = width // 64
            out_shapes.append(jax.ShapeDtypeStruct((lay.t_ctx * n64, 64), dtype))
            out_specs.append(pl.BlockSpec((tm * n64, 64), lambda i: (lay.part_tile(0, i), 0)))
            continue
        n_rows = lay.t_ctx if rows == "ctx" else lay.t
        out_shapes.append(jax.ShapeDtypeStruct((n_rows, width), dtype))
        out_specs.append(lay.row_specs([None, None], width)[0] if rows == "ctx" else lay.row_specs([None], width)[0])
    blocks_per_seq = lay.n_dec // tm
    rope_spec = pl.BlockSpec((tm, LANES), lambda i: (lay.part_tile(1, i) % blocks_per_seq, 0))
    return pl.pallas_call(
        functools.partial(_nmm_kernel, n_x=len(x_parts), outs=tuple(outs), tn=tn, n_ctx_tiles=lay.n_ctx_tiles),
        out_shape=out_shapes,
        grid=(lay.n_tiles,),
        in_specs=lay.row_specs(x_parts, d) + [
            pl.BlockSpec((1, d), lambda i: (0, 0)),
            pl.BlockSpec((1, N_MOD, d), lambda i: (lay.mod_row(i), 0, 0)),
            pl.BlockSpec((d, n_total), lambda i: (0, 0), pipeline_mode=pl.Buffered(1)),
            rope_spec, rope_spec,
        ],
        out_specs=out_specs,
        compiler_params=_cparams("arbitrary"),
        name="norm_mod_matmul",
    )(*x_parts, g.reshape(1, d), mods, w, *rope_tables)


def _diff_lambda(lam_ref, lam_init):
    lp = lam_ref[...]
    return (jnp.exp(jnp.sum(lp[0:1] * lp[1:2], axis=-1, keepdims=True))
            - jnp.exp(jnp.sum(lp[2:3] * lp[3:4], axis=-1, keepdims=True)) + lam_init)


ONES_ROWS = 16


def _scores_t(k, q, comp):
    lane = lax.broadcasted_iota(I32, q.shape, 1)
    return _nt_dot(k, jnp.where((lane < DA_HEAD_DIM) == (comp == 0), q, jnp.zeros_like(q)))


def _softmax_values_t(scores, vts):
    dv = vts[0].shape[0] - ONES_ROWS
    maxes = [jnp.max(s, axis=0, keepdims=True) for s in scores]
    exps = [jnp.exp2(s - m).astype(BF16) for s, m in zip(scores, maxes)]
    accs = [jnp.dot(vt, e, preferred_element_type=F32) for vt, e in zip(vts, exps)]
    return [acc[0:dv] / acc[dv:dv + 1] for acc in accs]


def _diff_finish_t(parts, lam, g_col, lam_init):
    o = parts[0] - lam * parts[1]
    ms = jnp.mean(o * o, axis=0, keepdims=True)
    return jnp.transpose((o * lax.rsqrt(ms + NORM_EPS) * g_col) * (1.0 - lam_init))


def _diff_attn_kernel(lam_ref, q_ref, k_ref, v_ref, g_ref, o_ref, *, lam_init, heads_per_step):
    hw = 2 * DA_HEAD_DIM
    lam = _diff_lambda(lam_ref, lam_init)
    ones = jnp.ones((ONES_ROWS, k_ref.shape[0]), BF16)
    heads = [slice(hh * hw, (hh + 1) * hw) for hh in range(heads_per_step)]
    scores = [[_scores_t(k_ref[:, cols], q_ref[:, cols], comp) for comp in range(2)] for cols in heads]
    vts = [jnp.concatenate([jnp.transpose(v_ref[:, cols].astype(F32)).astype(BF16), ones], axis=0) for cols in heads]
    parts = _softmax_values_t([s for pair in scores for s in pair], [vt for vt in vts for _ in range(2)])
    for hh, cols in enumerate(heads):
        o_ref[:, cols] = _diff_finish_t(parts[2 * hh:2 * hh + 2], lam, g_ref[...], lam_init).astype(o_ref.dtype)


def _diff_attn_cached_kernel(lam_ref, q_ref, k_ref, v_ref, ck_ref, cv_ref, g_ref, o_ref, k_scr, vt_scr, s_scr, *,
                             lam_init, sub_rows):
    seq_len, hw = q_ref.shape
    n_sub = seq_len // sub_rows
    assert n_sub % 2 == 0
    lam = _diff_lambda(lam_ref, lam_init)
    k_scr[0:seq_len, :] = k_ref[...]
    k_scr[seq_len:, :] = ck_ref[0].astype(BF16)
    vt_scr[0:hw, 0:seq_len] = jnp.transpose(v_ref[...].astype(F32)).astype(BF16)
    vt_scr[0:hw, seq_len:] = jnp.transpose(cv_ref[0]).astype(BF16)
    vt_scr[hw:, :] = jnp.ones((ONES_ROWS, vt_scr.shape[1]), BF16)

    def rows_of(t):
        return pl.ds(pl.multiple_of(t * sub_rows, sub_rows), sub_rows)

    def scores(t, slot, comp):
        s_scr[slot, comp] = _scores_t(k_scr[...], q_ref[rows_of(t), :], comp)

    def stage(t_next, slot_next, t, slot):
        for comp in range(2):
            scores(t_next, slot_next, comp)
        parts = _softmax_values_t([s_scr[slot, 0], s_scr[slot, 1]], [vt_scr[...]] * 2)
        o_ref[rows_of(t), :] = _diff_finish_t(parts, lam, g_ref[...], lam_init).astype(o_ref.dtype)

    scores(0, 0, 0)
    scores(0, 0, 1)

    def body(i2, carry):
        t = 2 * i2
        stage(t + 1, 1, t, 0)
        stage(jnp.minimum(t + 2, n_sub - 1), 0, t + 1, 1)
        return carry

    lax.fori_loop(0, n_sub // 2, body, 0)


def diff_attention(q, k, v, row0, batch, seq_len, lam_params, subln_g, lam_init, cache=None, heads_per_step=1,
                   sub_rows=256):
    width = q.shape[1]
    hw = 2 * DA_HEAD_DIM
    bw = heads_per_step * hw
    assert row0 % seq_len == 0 and DA_HEADS % heads_per_step == 0
    s0 = row0 // seq_len
    seq_spec = pl.BlockSpec((seq_len, bw), lambda b, h: (s0 + b, h))
    in_specs = [pl.BlockSpec((4, DA_HEAD_DIM), lambda b, h: (0, 0)), seq_spec, seq_spec, seq_spec]
    args = [lam_params, q, k, v]
    scratch = []
    if cache is None:
        body = functools.partial(_diff_attn_kernel, lam_init=lam_init, heads_per_step=heads_per_step)
    else:
        assert heads_per_step == 1
        past = cache[0].shape[1]
        cache_spec = pl.BlockSpec((1, past, hw), lambda b, h: (b, 0, h))
        in_specs += [cache_spec, cache_spec]
        args += list(cache)
        sub_rows = min(sub_rows, seq_len // 2)
        scratch = [pltpu.VMEM((seq_len + past, hw), BF16), pltpu.VMEM((hw + ONES_ROWS, seq_len + past), BF16),
                   pltpu.VMEM((2, 2, seq_len + past, sub_rows), F32)]
        body = functools.partial(_diff_attn_cached_kernel, lam_init=lam_init, sub_rows=sub_rows)
    in_specs.append(pl.BlockSpec((hw, 1), lambda b, h: (0, 0)))
    args.append(subln_g.reshape(hw, 1))
    return pl.pallas_call(
        body,
        out_shape=jax.ShapeDtypeStruct((batch * seq_len, width), BF16),
        grid=(batch, DA_HEADS // heads_per_step),
        in_specs=in_specs,
        out_specs=pl.BlockSpec((seq_len, bw), lambda b, h: (b, h)),
        scratch_shapes=scratch,
        compiler_params=_cparams("parallel", "parallel"),
        name="diff_attention",
    )(*args)


def _retention_kernel(*refs, seq_len, seqs, has_state, emit_state):
    lg_ref, q_ref, k_ref, v_ref, gf_ref, gb_ref = refs[:6]
    pos = 6
    if has_state:
        s0_refs = refs[6:8]
        pos = 8
    o_ref = refs[pos]
    pos += 1
    if emit_state:
        s_out_refs = refs[pos:pos + 2]
        pos += 2
    s_scr, o_scr = refs[pos:pos + 2]

    h = pl.program_id(1)
    c_len = RET_CHUNK
    n_chunks = seq_len // c_len
    assert n_chunks % 2 == 0
    row = lax.broadcasted_iota(I32, (c_len, c_len), 0)
    colm = lax.broadcasted_iota(I32, (c_len, c_len), 1)
    rel = (row - colm).astype(F32)
    posv = lax.broadcasted_iota(I32, (c_len, 1), 0).astype(F32)

    consts = []
    for backward in (False, True):
        lg = lg_ref[1 if backward else 0, h]
        if backward:
            intra = jnp.where(rel <= 0, jnp.exp(-rel * lg), 0.0)
            q_decay = jnp.exp((c_len - posv) * lg)
            k_decay = jnp.exp(posv * lg)
        else:
            intra = jnp.where(rel >= 0, jnp.exp(rel * lg), 0.0)
            q_decay = jnp.exp((posv + 1.0) * lg)
            k_decay = jnp.exp((c_len - 1.0 - posv) * lg)
        consts.append((intra, q_decay, k_decay, jnp.exp(jnp.zeros((1, 1), F32) + c_len * lg)))
        direction = 1 if backward else 0
        for sq in range(seqs):
            if has_state:
                s_scr[sq, direction] = s0_refs[direction][sq, 0, 0]
            else:
                s_scr[sq, direction] = jnp.zeros(s_scr.shape[2:], F32)

    def body(ci, first_touch):
        chains = [(sq, direction, pl.ds(pl.multiple_of(sq * seq_len + c * c_len, c_len), c_len))
                  for sq in range(seqs) for direction, c in ((0, ci), (1, n_chunks - 1 - ci))]
        qs = [q_ref[rows, :].astype(BF16) for _, _, rows in chains]
        ks = [k_ref[rows, :].astype(F32) for _, _, rows in chains]
        vs = [v_ref[rows, :].astype(BF16) for _, _, rows in chains]
        intras = [(_nt_dot(qb, kf.astype(BF16)) * consts[d][0]).astype(BF16)
                  for (_, d, _), qb, kf in zip(chains, qs, ks)]
        states = [s_scr[sq, d] for sq, d, _ in chains]
        outs = [jnp.dot(a, vb, preferred_element_type=F32)
                + jnp.dot(qb, s.astype(BF16), preferred_element_type=F32) * consts[d][1]
                for (_, d, _), a, qb, vb, s in zip(chains, intras, qs, vs, states)]
        for (sq, d, _), kf, vb, s in zip(chains, ks, vs, states):
            s_scr[sq, d] = consts[d][3] * s + _tn_dot((kf * consts[d][2]).astype(BF16), vb)
        centred = [o - jnp.mean(o, axis=-1, keepdims=True) for o in outs]
        scales = [lax.rsqrt(jnp.mean(oc * oc, axis=-1, keepdims=True) + NORM_EPS) for oc in centred]
        for (_, d, rows), oc, scale in zip(chains, centred, scales):
            gated = oc * scale * (gb_ref if d else gf_ref)[rows, :].astype(F32)
            if first_touch:
                o_scr[rows, :] = gated
            else:
                o_ref[rows, :] = (o_scr[rows, :] + gated).astype(o_ref.dtype)

    unroll = 4 if n_chunks % 8 == 0 else 1
    lax.fori_loop(0, n_chunks // 2, lambda ci, c: (body(ci, True), c)[1], 0, unroll=unroll)
    lax.fori_loop(n_chunks // 2, n_chunks, lambda ci, c: (body(ci, False), c)[1], 0, unroll=unroll)
    if emit_state:
        for d in range(2):
            for sq in range(seqs):
                s_out_refs[d][sq, 0, 0] = s_scr[sq, d]


def retention(q, k, v, gates, row0, batch, seq_len, log_decay, state=None, emit_state=False, seqs_per_step=1):
    dk = q.shape[1] // RET_HEADS
    dv = v.shape[1] // RET_HEADS
    seqs = seqs_per_step
    rows = seqs * seq_len
    assert row0 % rows == 0 and batch % seqs == 0
    s0 = row0 // rows
    in_specs = [
        pl.BlockSpec(memory_space=pltpu.SMEM),
        pl.BlockSpec((rows, dk), lambda b, h: (s0 + b, h)),
        pl.BlockSpec((rows, dk), lambda b, h: (s0 + b, h)),
        pl.BlockSpec((rows, dv), lambda b, h: (s0 + b, h)),
        pl.BlockSpec((rows, dv), lambda b, h: (s0 + b, h)),
        pl.BlockSpec((rows, dv), lambda b, h: (s0 + b, RET_HEADS + h)),
    ]
    args = [log_decay, q, k, v, gates, gates]
    state_spec = pl.BlockSpec((seqs, 1, 1, dk, dv), lambda b, h: (b, 0, h, 0, 0))
    if state is not None:
        in_specs += [state_spec, state_spec]
        args += list(state)
    out_shapes = [jax.ShapeDtypeStruct((batch * seq_len, RET_HEADS * dv), BF16)]
    out_specs = [pl.BlockSpec((rows, dv), lambda b, h: (b, h))]
    if emit_state:
        out_shapes += [jax.ShapeDtypeStruct((batch, 1, RET_HEADS, dk, dv), F32)] * 2
        out_specs += [state_spec, state_spec]
    return pl.pallas_call(
        functools.partial(_retention_kernel, seq_len=seq_len, seqs=seqs, has_state=state is not None,
                          emit_state=emit_state),
        out_shape=out_shapes,
        grid=(batch // seqs, RET_HEADS),
        in_specs=in_specs,
        out_specs=out_specs,
        scratch_shapes=[pltpu.VMEM((seqs, 2, dk, dv), F32), pltpu.VMEM((rows, dv), F32)],
        compiler_params=_cparams("parallel", "parallel"),
        name="retention",
    )(*args)


def _pack_bf16_halves(x):
    half = x.shape[1] // 2
    lo = lax.bitcast_convert_type(x[:, :half].astype(F32), jnp.uint32)
    hi = lax.bitcast_convert_type(x[:, half:].astype(F32), jnp.uint32)
    return (hi & jnp.uint32(0xFFFF0000)) | (lo >> 16)


def _unpack_bf16_halves(p):
    lo = lax.bitcast_convert_type(p << 16, F32).astype(BF16)
    hi = lax.bitcast_convert_type(p & jnp.uint32(0xFFFF0000), F32).astype(BF16)
    return lo, hi


def _first_max_onehot(vals):
    m = vals[0]
    for v in vals[1:]:
        m = jnp.maximum(m, v)
    onehot, taken = [], None
    for v in vals:
        hit = v == m
        if taken is None:
            onehot.append(hit)
            taken = hit
        else:
            onehot.append(hit & jnp.logical_not(taken))
            taken = taken | hit
    return m, onehot


def _pick(onehot, vals):
    out = vals[-1]
    for oh, v in zip(onehot[-2::-1], vals[-2::-1]):
        out = jnp.where(oh, v, out)
    return out


def _route_rows(x, g_ref, mod_ref, wr_ref, bias_ref, tri_ref, h_ref, idx_ref, wcol_ref, rank_ref, cnt_ref,
                ind_scr, wrow_scr):
    hb = _modulated_norm(x, g_ref[...], mod_ref, 3).astype(BF16)
    h_ref[...] = _pack_bf16_halves(hb)
    logits = _nt_dot(wr_ref[...], hb)
    s = jax.nn.sigmoid(logits)
    sel = s + bias_ref[...]
    neg_inf = jnp.full_like(sel[0:1], -jnp.inf)
    sel_rows = [sel[e:e + 1] for e in range(N_EXPERTS)]
    s_rows = [s[e:e + 1] for e in range(N_EXPERTS)]

    def top2(vals):
        m1, oh1 = _first_max_onehot(vals)
        rest = [jnp.where(o, neg_inf, v) for o, v in zip(oh1, vals)]
        m2, oh2 = _first_max_onehot(rest)
        return m1, m2, oh1, oh2

    grp_scores = []
    for g in range(N_GROUPS):
        m1, m2, _, _ = top2(sel_rows[g * EXPERTS_PER_GROUP:(g + 1) * EXPERTS_PER_GROUP])
        grp_scores.append(m1 + m2)
    _, in_grp = _first_max_onehot(grp_scores)
    cand_sel = [_pick(in_grp, [sel_rows[g * EXPERTS_PER_GROUP + k] for g in range(N_GROUPS)])
                for k in range(EXPERTS_PER_GROUP)]
    cand_s = [_pick(in_grp, [s_rows[g * EXPERTS_PER_GROUP + k] for g in range(N_GROUPS)])
              for k in range(EXPERTS_PER_GROUP)]
    _, _, oh1, oh2 = top2(cand_sel)
    w1 = _pick(oh1, cand_s)
    w2 = _pick(oh2, cand_s)
    denom = w1 + w2
    wrow_scr[...] = jnp.zeros_like(wrow_scr)
    wrow_scr[0:1, :] = w1 / denom
    wrow_scr[1:2, :] = w2 / denom
    wcol_ref[...] = jnp.transpose(wrow_scr[...])

    ints = [jnp.full(w1.shape, k, I32) for k in range(EXPERTS_PER_GROUP)]
    grp = _pick(in_grp, ints) * EXPERTS_PER_GROUP
    idx_ref[0:1, :] = grp + _pick(oh1, ints)
    idx_ref[1:2, :] = grp + _pick(oh2, ints)

    one, zero = jnp.ones_like(w1), jnp.zeros_like(w1)
    for g in range(N_GROUPS):
        for k in range(EXPERTS_PER_GROUP):
            e = g * EXPERTS_PER_GROUP + k
            ind_scr[e:e + 1, :] = jnp.where(in_grp[g] & (oh1[k] | oh2[k]), one, zero)
    ind = ind_scr[...].astype(BF16)
    ranks = jnp.dot(ind, tri_ref[...], preferred_element_type=F32)
    cnt_ref[0] = jnp.dot(ind, jnp.ones((ind.shape[1], LANES), BF16), preferred_element_type=F32)
    for slot, oh in ((0, oh1), (1, oh2)):
        r = zero
        for g in range(N_GROUPS):
            for k in range(EXPERTS_PER_GROUP):
                e = g * EXPERTS_PER_GROUP + k
                r = r + jnp.where(in_grp[g] & oh[k], ranks[e:e + 1], zero)
        rank_ref[slot:slot + 1, :] = r.astype(I32)


def _proj_route_kernel(*refs, n_x, n_a, n_ctx_tiles):
    x_refs, a_refs = refs[:n_x], refs[n_x:n_x + n_a]
    w_ref, mod_ref, g_ref, wr_ref, bias_ref, tri_ref, o_ref = refs[n_x + n_a:n_x + n_a + 7]
    route_out_refs = refs[n_x + n_a + 7:n_x + n_a + 12]
    y_scr, ind_scr, wrow_scr = refs[n_x + n_a + 12:]
    is_dec = pl.program_id(0) >= n_ctx_tiles

    def project(a_ref):
        y_scr[...] = mod_ref[0, 2:3, :] * jnp.dot(a_ref[...].astype(BF16), w_ref[...],
                                                  preferred_element_type=F32)

    def add(x_ref):
        o_ref[...] = x_ref[...] + y_scr[...]

    _by_part(is_dec, a_refs, project)
    _by_part(is_dec, x_refs, add)
    _route_rows(o_ref[...], g_ref, mod_ref, wr_ref, bias_ref, tri_ref, *route_out_refs, ind_scr, wrow_scr)


def proj_residual_route(lay, x_parts, a_parts, w, mods, ffn_g, w_router_t, router_bias):
    d = x_parts[0].shape[1]
    kdim = a_parts[0].shape[1]
    tm, t = lay.tm, lay.t
    tri = jnp.triu(jnp.ones((tm, tm), BF16), k=1)
    pair = jax.ShapeDtypeStruct((TOP_K, t), I32)
    pair_spec = pl.BlockSpec((TOP_K, tm), lambda i: (0, i))
    const = lambda shape: pl.BlockSpec(shape, lambda i: (0,) * len(shape))
    return pl.pallas_call(
        functools.partial(_proj_route_kernel, n_x=len(x_parts), n_a=len(a_parts), n_ctx_tiles=lay.n_ctx_tiles),
        out_shape=[jax.ShapeDtypeStruct((t, d), F32), jax.ShapeDtypeStruct((t, d // 2), jnp.uint32), pair,
                   jax.ShapeDtypeStruct((t, LANES), F32), pair,
                   jax.ShapeDtypeStruct((lay.n_tiles, N_EXPERTS, LANES), F32)],
        grid=(lay.n_tiles,),
        in_specs=lay.row_specs(x_parts, d) + lay.row_specs(a_parts, kdim) + [
            const((kdim, d)),
            pl.BlockSpec((1, N_MOD, d), lambda i: (lay.mod_row(i), 0, 0)),
            const((1, d)), const((N_EXPERTS, d)), const((N_EXPERTS, 1)), const((tm, tm)),
        ],
        out_specs=[pl.BlockSpec((tm, d), lambda i: (i, 0)), pl.BlockSpec((tm, d // 2), lambda i: (i, 0)), pair_spec,
                   pl.BlockSpec((tm, LANES), lambda i: (i, 0)), pair_spec,
                   pl.BlockSpec((1, N_EXPERTS, LANES), lambda i: (i, 0, 0))],
        scratch_shapes=[pltpu.VMEM((tm, d), F32), pltpu.VMEM((N_EXPERTS, tm), F32), pltpu.VMEM((LANES, tm), F32)],
        compiler_params=_cparams("arbitrary"),
        name="proj_residual_route",
    )(*x_parts, *a_parts, w, mods, ffn_g.reshape(1, d), w_router_t, router_bias.reshape(N_EXPERTS, 1), tri)


def dispatch_plan(lay, idx, rank, cnt, n_sorted):
    tm = lay.tm
    cnt_tile = cnt[:, :, 0].astype(I32)
    total = jnp.sum(cnt_tile, axis=0)
    padded = ((total + EXPERT_TILE - 1) // EXPERT_TILE) * EXPERT_TILE
    end = jnp.cumsum(padded)
    start = end - padded
    base = start[None, :] + jnp.cumsum(cnt_tile, axis=0) - cnt_tile
    idx3 = idx.reshape(TOP_K, lay.n_tiles, tm)
    pos = rank.reshape(TOP_K, lay.n_tiles, tm)
    for e in range(N_EXPERTS):
        pos = pos + jnp.where(idx3 == e, base[None, :, e, None], 0)
    tile_row = jnp.arange(n_sorted // EXPERT_TILE, dtype=I32) * EXPERT_TILE
    tile_expert = jnp.minimum(jnp.sum(end[None, :] <= tile_row[:, None], axis=1), N_EXPERTS - 1).astype(I32)
    n_valid = (end[-1] // EXPERT_TILE).astype(I32).reshape(1)
    return pos.reshape(TOP_K * lay.t).astype(I32), tile_expert, n_valid


def _sc_mesh():
    return plsc.VectorSubcoreMesh(core_axis_name="c", subcore_axis_name="s")


def _sc_worker_id():
    return lax.axis_index("s") * V7X_SC_CORES + lax.axis_index("c")


def _sc_two_stage_pipeline(n_chunks, fetch_idx, load, store):
    assert n_chunks % 2 == 0
    fetch_idx(0, 0)
    load(0, 0).start()

    @pl.loop(0, n_chunks // 2)
    def _(p):
        j = 2 * p

        @pl.when(p > 0)
        def _():
            store(j - 1, 1).wait()

        fetch_idx(j + 1, 1)
        load(j + 1, 1).start()
        load(j, 0).wait()
        store(j, 0).start()
        store(j, 0).wait()

        @pl.when(j + 2 < n_chunks)
        def _():
            fetch_idx(j + 2, 0)
            load(j + 2, 0).start()

        load(j + 1, 1).wait()
        store(j + 1, 1).start()

    store(n_chunks - 1, 1).wait()


_SC_SCRATCH = lambda chunk, d, dtype: [
    pltpu.VMEM((chunk,), I32), pltpu.VMEM((chunk,), I32),
    pltpu.VMEM((chunk, d), dtype), pltpu.VMEM((chunk, d), dtype),
    pltpu.SemaphoreType.DMA, pltpu.SemaphoreType.DMA, pltpu.SemaphoreType.DMA, pltpu.SemaphoreType.DMA]


def sc_scatter_rows(src, pos, n_out_rows):
    t, d = src.shape
    n_idx = pos.shape[0]
    per_worker = n_idx // SC_WORKERS
    chunk = SC_CHUNK_ROWS
    assert n_idx % (SC_WORKERS * chunk * 2) == 0 and t % chunk == 0

    @functools.partial(
        pl.kernel, mesh=_sc_mesh(), out_type=jax.ShapeDtypeStruct((n_out_rows, d), src.dtype),
        scratch_types=_SC_SCRATCH(chunk, d, src.dtype), name="sc_scatter_rows")
    def scatter(src_hbm, pos_hbm, out_hbm, idx_a, idx_b, rows_a, rows_b, lsem_a, lsem_b, ssem_a, ssem_b):
        base = _sc_worker_id() * per_worker
        idx, rows, lsem, ssem = (idx_a, idx_b), (rows_a, rows_b), (lsem_a, lsem_b), (ssem_a, ssem_b)

        def fetch_idx(j, b):
            pltpu.sync_copy(pos_hbm.at[pl.ds(base + j * chunk, chunk)], idx[b])

        def load(j, b):
            return pltpu.make_async_copy(src_hbm.at[pl.ds(lax.rem(base + j * chunk, t), chunk)], rows[b], lsem[b])

        def store(j, b):
            return pltpu.make_async_copy(rows[b], out_hbm.at[idx[b]], ssem[b])

        _sc_two_stage_pipeline(per_worker // chunk, fetch_idx, load, store)

    return scatter(src, pos)


def sc_gather_rows(table, idx):
    _, d = table.shape
    n_idx = idx.shape[0]
    per_worker = n_idx // SC_WORKERS
    chunk = SC_CHUNK_ROWS
    assert n_idx % (SC_WORKERS * chunk * 2) == 0

    @functools.partial(
        pl.kernel, mesh=_sc_mesh(), out_type=jax.ShapeDtypeStruct((n_idx, d), table.dtype),
        scratch_types=_SC_SCRATCH(chunk, d, table.dtype), name="sc_gather_rows")
    def gather(table_hbm, idx_hbm, out_hbm, idx_a, idx_b, rows_a, rows_b, lsem_a, lsem_b, ssem_a, ssem_b):
        base = _sc_worker_id() * per_worker
        idx, rows, lsem, ssem = (idx_a, idx_b), (rows_a, rows_b), (lsem_a, lsem_b), (ssem_a, ssem_b)

        def fetch_idx(j, b):
            pltpu.sync_copy(idx_hbm.at[pl.ds(base + j * chunk, chunk)], idx[b])

        def load(j, b):
            return pltpu.make_async_copy(table_hbm.at[idx[b]], rows[b], lsem[b])

        def store(j, b):
            return pltpu.make_async_copy(rows[b], out_hbm.at[pl.ds(base + j * chunk, chunk)], ssem[b])

        _sc_two_stage_pipeline(per_worker // chunk, fetch_idx, load, store)

    return gather(table, idx)


def _experts_kernel(te_ref, nv_ref, x_ref, wg_ref, wu_ref, wd_ref, y_ref, wg_scr, wu_scr, wd_scr):
    i = pl.program_id(0)

    @pl.when(i < nv_ref[0])
    def _():
        @pl.when((i == 0) | (te_ref[i] != te_ref[jnp.maximum(i - 1, 0)]))
        def _():
            wg_scr[...] = wg_ref[0, 0].astype(BF16)
            wu_scr[...] = wu_ref[0, 0].astype(BF16)
            wd_scr[...] = wd_ref[0, 0].astype(BF16)

        x_lo, x_hi = _unpack_bf16_halves(x_ref[...])
        half = x_lo.shape[1]

        def in_proj(w_scr):
            return (jnp.dot(x_lo, w_scr[0:half, :], preferred_element_type=F32)
                    + jnp.dot(x_hi, w_scr[half:, :], preferred_element_type=F32))

        a = jax.nn.silu(in_proj(wg_scr)) * in_proj(wu_scr)
        y = jnp.dot(a.astype(BF16), wd_scr[...], preferred_element_type=F32)
        y_ref[...] = _pack_bf16_halves(y.astype(BF16))


def grouped_experts(xs, tile_expert, n_valid, wg, wu, wd, layer):
    n_rows = xs.shape[0]
    d, de = wg.shape[-2:]
    tm = EXPERT_TILE
    row_map = lambda i, te, nv: (jnp.minimum(i, nv[0] - 1), 0)
    grid_spec = pltpu.PrefetchScalarGridSpec(
        num_scalar_prefetch=2,
        grid=(n_rows // tm,),
        in_specs=[
            pl.BlockSpec((tm, d // 2), row_map),
            pl.BlockSpec((1, 1, d, de), lambda i, te, nv: (layer, te[i], 0, 0)),
            pl.BlockSpec((1, 1, d, de), lambda i, te, nv: (layer, te[i], 0, 0)),
            pl.BlockSpec((1, 1, de, d), lambda i, te, nv: (layer, te[i], 0, 0)),
        ],
        out_specs=pl.BlockSpec((tm, d // 2), row_map),
        scratch_shapes=[pltpu.VMEM((d, de), BF16), pltpu.VMEM((d, de), BF16), pltpu.VMEM((de, d), BF16)],
    )
    return pl.pallas_call(
        _experts_kernel,
        out_shape=jax.ShapeDtypeStruct((n_rows, d // 2), jnp.uint32),
        grid_spec=grid_spec,
        compiler_params=_cparams("arbitrary"),
        name="grouped_experts",
    )(tile_expert, n_valid, xs, wg, wu, wd)


def _combine_kernel(x_ref, y_ref, w_ref, mod_ref, fg_ref, *o_refs, final_norm, n_ctx_tiles):
    w = w_ref[...]

    def expert_out(slot):
        lo, hi = _unpack_bf16_halves(y_ref[slot])
        return jnp.concatenate([lo.astype(F32), hi.astype(F32)], axis=1)

    y = w[:, 0:1] * expert_out(0) + w[:, 1:2] * expert_out(1)
    out = x_ref[...] + mod_ref[0, 5:6, :] * y
    if final_norm:
        ms = jnp.mean(out * out, axis=-1, keepdims=True)
        out = out * lax.rsqrt(ms + NORM_EPS) * fg_ref[...]

    def store(o_ref):
        o_ref[...] = out
    _by_part(pl.program_id(0) >= n_ctx_tiles, o_refs, store)


def combine_residual(lay, x, y_pair, w_col, mods, final_g, final_norm, split):
    t, d = x.shape
    tm = lay.tm
    if split:
        out_shape = [jax.ShapeDtypeStruct((lay.t_ctx, d), F32), jax.ShapeDtypeStruct((lay.t_dec, d), F32)]
        out_specs = lay.row_specs([None, None], d)
    else:
        out_shape = [jax.ShapeDtypeStruct((t, d), F32)]
        out_specs = lay.row_specs([None], d)
    return pl.pallas_call(
        functools.partial(_combine_kernel, final_norm=final_norm, n_ctx_tiles=lay.n_ctx_tiles),
        out_shape=out_shape,
        grid=(lay.n_tiles,),
        in_specs=[
            pl.BlockSpec((tm, d), lambda i: (i, 0)),
            pl.BlockSpec((TOP_K, tm, d // 2), lambda i: (0, i, 0)),
            pl.BlockSpec((tm, LANES), lambda i: (i, 0)),
            pl.BlockSpec((1, N_MOD, d), lambda i: (lay.mod_row(i), 0, 0)),
            pl.BlockSpec((1, d), lambda i: (0, 0)),
        ],
        out_specs=out_specs,
        compiler_params=_cparams("arbitrary"),
        name="combine_residual",
    )(x, y_pair, w_col, mods, final_g.reshape(1, d))


def group_moe_residual(lay, x, routing, mods, p, layer, last):
    t, d = x.shape
    h, idx, w_col, rank, cnt = routing
    n_assign = TOP_K * t
    row_quantum = SC_WORKERS * SC_CHUNK_ROWS
    n_sorted = n_assign + N_EXPERTS * (EXPERT_TILE - 1)
    n_sorted = -(-n_sorted // row_quantum) * row_quantum
    n_sorted = -(-n_sorted // EXPERT_TILE) * EXPERT_TILE
    pos, tile_expert, n_valid = dispatch_plan(lay, idx, rank, cnt, n_sorted)
    xs = sc_scatter_rows(h, pos, n_sorted)
    ys = grouped_experts(xs, tile_expert, n_valid, p["moe_w_gate"], p["moe_w_up"], p["moe_w_down"], layer)
    y_pair = sc_gather_rows(ys, pos).reshape(TOP_K, t, d // 2)
    return combine_residual(lay, x, y_pair, w_col, mods, p["final_norm_g"], final_norm=last, split=last)


def _rope_angles(n, d):
    n_rows = n // GRID_W
    row = jnp.repeat(jnp.arange(n_rows), GRID_W).astype(F32)
    col = jnp.tile(jnp.arange(GRID_W), n_rows).astype(F32)
    nf = d // 4
    freqs = jnp.power(ROPE_BASE, -jnp.arange(nf, dtype=F32) / nf)
    ang = jnp.concatenate([row[:, None] * freqs, col[:, None] * freqs], axis=-1)
    return jnp.cos(ang), jnp.sin(ang)


def kernel(x_prompt, x_sample, cache_attn_k, cache_attn_v, state_ret_fwd, state_ret_bwd, c, c_ctx, w_ada, b_ada, norm_mix_g, norm_ffn_g, final_norm_g, da_w_qkv, da_lambda_q1, da_lambda_k1, da_lambda_q2, da_lambda_k2, da_subln_g, da_w_o, ret_w_qkv, ret_w_gate_fwd, ret_w_gate_bwd, ret_decay_fwd, ret_decay_bwd, ret_w_o, w_router, router_bias, moe_w_gate, moe_w_up, moe_w_down):
    b_ctx, n_ctx, d = x_prompt.shape
    b_dec, n_dec, _ = x_sample.shape
    past = cache_attn_k.shape[2]
    n_attn = cache_attn_k.shape[1]
    depth = w_ada.shape[0]
    assert b_dec + 1 <= MOD_ROWS
    lay = Layout(b_ctx, n_ctx, b_dec, n_dec)

    cond = jnp.zeros((MOD_ROWS, d), F32).at[0].set(c_ctx).at[1:1 + b_dec].set(c)
    mods_all = ada_modulation(cond, w_ada, b_ada)

    p = {
        "norm_ffn_g": norm_ffn_g, "final_norm_g": final_norm_g,
        "w_router_t": w_router.T.astype(BF16), "router_bias": router_bias.astype(F32),
        "moe_w_gate": moe_w_gate, "moe_w_up": moe_w_up, "moe_w_down": moe_w_down,
    }
    ret_log_decay = jnp.stack([jax.nn.log_sigmoid(ret_decay_fwd.astype(F32)),
                               jax.nn.log_sigmoid(ret_decay_bwd.astype(F32))], axis=1)
    ck_all = cache_attn_k.reshape(b_dec, n_attn, past, -1)
    cv_all = cache_attn_v.reshape(b_dec, n_attn, past, -1)

    x_parts = [x_prompt.reshape(lay.t_ctx, d), x_sample.reshape(lay.t_dec, d)]
    new_k, new_v, new_sf, new_sb = [], [], [], []
    for i in range(depth):
        mods = mods_all[i]
        j = i // 2
        if i % 2 == 0:
            lam_init = 0.8 - 0.6 * math.exp(-0.3 * i)
            qkw = DA_HEADS * 2 * DA_HEAD_DIM
            vw = DA_HEADS * DA_V_DIM
            cos, sin = _rope_angles(n_dec, DA_HEAD_DIM)
            rope = (jnp.tile(cos, (1, 4)), jnp.concatenate([-sin, sin, -sin, sin], axis=-1))
            q, k, v, k_ctx, v_ctx = norm_mod_matmul(
                lay, x_parts, norm_mix_g[i], mods, da_w_qkv[j].astype(BF16),
                [(0, qkw, "rope64", math.log2(math.e) * DA_HEAD_DIM ** -0.5, BF16, "all"),
                 (qkw, qkw, "rope64", 1.0, BF16, "all"),
                 (2 * qkw, vw, "plain", 1.0, BF16, "all"),
                 (qkw, qkw, "plain", 1.0, F32, "ctx64"), (2 * qkw, vw, "plain", 1.0, F32, "ctx")],
                rope)
            lam_params = jnp.stack([da_lambda_q1[j], da_lambda_k1[j], da_lambda_q2[j], da_lambda_k2[j]])
            mix = [diff_attention(q, k, v, 0, b_ctx, n_ctx, lam_params, da_subln_g[j], lam_init,
                                  heads_per_step=DA_HEADS),
                   diff_attention(q, k, v, lay.t_ctx, b_dec, n_dec, lam_params, da_subln_g[j], lam_init,
                                  cache=(ck_all[:, j], cv_all[:, j]))]
            mix_w = da_w_o[j].astype(BF16)
            new_k.append(k_ctx.reshape(b_ctx, n_ctx, DA_HEADS, 2, DA_HEAD_DIM))
            new_v.append(v_ctx.reshape(b_ctx, n_ctx, DA_HEADS, DA_V_DIM))
        else:
            kd = ret_w_qkv.shape[2] // 4
            dv = 2 * kd
            w_all = jnp.concatenate([ret_w_qkv[j], ret_w_gate_fwd[j], ret_w_gate_bwd[j]], axis=-1).astype(BF16)
            q, k, v, gates = norm_mod_matmul(
                lay, x_parts, norm_mix_g[i], mods, w_all,
                [(0, kd, "rope256", 1.0, BF16, "all"),
                 (kd, kd, "rope256", (kd // RET_HEADS) ** -0.5, F32, "all"),
                 (2 * kd, dv, "plain", 1.0, BF16, "all"), (2 * kd + dv, 2 * dv, "silu", 1.0, BF16, "all")],
                _rope_angles(n_dec, kd // RET_HEADS))
            ctx_seqs = math.gcd(b_ctx, max(1, n_dec // n_ctx // 2))
            o_ctx, sf, sb = retention(q, k, v, gates, 0, b_ctx, n_ctx, ret_log_decay[j], emit_state=True,
                                      seqs_per_step=ctx_seqs)
            (o_dec,) = retention(q, k, v, gates, lay.t_ctx, b_dec, n_dec, ret_log_decay[j],
                                 state=(state_ret_fwd[:, j:j + 1], state_ret_bwd[:, j:j + 1]))
            mix, mix_w = [o_ctx, o_dec], ret_w_o[j].astype(BF16)
            new_sf.append(sf)
            new_sb.append(sb)
        x, *routing = proj_residual_route(lay, x_parts, mix, mix_w, mods, norm_ffn_g[i], p["w_router_t"],
                                          p["router_bias"])
        x_parts = group_moe_residual(lay, x, routing, mods, p, i, last=(i == depth - 1))

    y_ctx, y_dec = x_parts
    return (y_ctx.reshape(b_ctx, n_ctx, d), y_dec.reshape(b_dec, n_dec, d),
            jnp.stack(new_k, axis=1), jnp.stack(new_v, axis=1),
            jnp.concatenate(new_sf, axis=1), jnp.concatenate(new_sb, axis=1))
```

```python
import functools
import math

import jax
import jax.numpy as jnp
from jax import lax
from jax.experimental import pallas as pl
from jax.experimental.pallas import tpu as pltpu
from jax.experimental.pallas import tpu_sc as plsc

F32 = jnp.float32
BF16 = jnp.bfloat16
I32 = jnp.int32

GRID_W = 64
ROPE_BASE = 10000.0
NORM_EPS = 1e-6
DA_HEADS = 8
DA_HEAD_DIM = 64
DA_V_DIM = 2 * DA_HEAD_DIM
RET_HEADS = 4
RET_CHUNK = 128
N_EXPERTS = 16
N_GROUPS = 4
EXPERTS_PER_GROUP = N_EXPERTS // N_GROUPS
TOP_K = 2
N_MOD = 6
MOD_ROWS = 16
LANES = 128

V7X_VMEM_LIMIT = 56 * 1024 * 1024
V7X_SC_CORES = 2
V7X_SC_SUBCORES = 16
SC_WORKERS = V7X_SC_CORES * V7X_SC_SUBCORES
SC_CHUNK_ROWS = 64

ROW_TILE = 512
EXPERT_TILE = 512


def _cparams(*sem):
    return pltpu.CompilerParams(dimension_semantics=sem, vmem_limit_bytes=V7X_VMEM_LIMIT)


def _nt_dot(a, b):
    return lax.dot_general(a, b, (((1,), (1,)), ((), ())), preferred_element_type=F32)


def _tn_dot(a, b):
    return lax.dot_general(a, b, (((0,), (0,)), ((), ())), preferred_element_type=F32)


class Layout:
    def __init__(self, b_ctx, n_ctx, b_dec, n_dec):
        self.b_ctx, self.n_ctx, self.b_dec, self.n_dec = b_ctx, n_ctx, b_dec, n_dec
        self.t_ctx, self.t_dec = b_ctx * n_ctx, b_dec * n_dec
        self.t = self.t_ctx + self.t_dec
        self.tm = min(ROW_TILE, n_dec, self.t_ctx)
        assert self.t_ctx % self.tm == 0 and n_dec % self.tm == 0
        assert self.t_ctx % n_dec == 0 and self.t_ctx % n_ctx == 0
        self.n_ctx_tiles = self.t_ctx // self.tm
        self.n_tiles = self.t // self.tm

    def mod_row(self, i):
        r = i * self.tm
        return jnp.where(r < self.t_ctx, 0, 1 + (r - self.t_ctx) // self.n_dec)

    def part_tile(self, part, i):
        if part == 0:
            return jnp.minimum(i, self.n_ctx_tiles - 1)
        return jnp.maximum(i - self.n_ctx_tiles, 0)

    def row_specs(self, arrays, width, extra_grid_dims=0):
        pad = (0,) * 0
        if len(arrays) == 1:
            return [pl.BlockSpec((self.tm, width), lambda i, *_: (i, 0))]
        return [pl.BlockSpec((self.tm, width), lambda i, *_, p=p: (self.part_tile(p, i), 0)) for p in (0, 1)]


def _ada_kernel(c_ref, w_ref, b_ref, o_ref):
    s = jax.nn.silu(c_ref[...]).astype(BF16)
    acc = jnp.dot(s, w_ref[0].astype(BF16), preferred_element_type=F32)
    o_ref[0] = acc + b_ref[0]


def ada_modulation(cond, w_ada, b_ada):
    depth, d, n = w_ada.shape
    tn = 1536
    out = pl.pallas_call(
        _ada_kernel,
        out_shape=jax.ShapeDtypeStruct((depth, MOD_ROWS, n), F32),
        grid=(depth, n // tn),
        in_specs=[
            pl.BlockSpec((MOD_ROWS, d), lambda l, j: (0, 0)),
            pl.BlockSpec((1, d, tn), lambda l, j: (l, 0, j)),
            pl.BlockSpec((1, 1, tn), lambda l, j: (l, 0, j)),
        ],
        out_specs=pl.BlockSpec((1, MOD_ROWS, tn), lambda l, j: (l, 0, j)),
        compiler_params=_cparams("parallel", "parallel"),
        name="ada_modulation",
    )(cond, w_ada, b_ada.reshape(depth, 1, n))
    return out.reshape(depth, MOD_ROWS, N_MOD, d)


def _modulated_norm(x, g, mod_ref, shift_idx):
    ms = jnp.mean(x * x, axis=-1, keepdims=True)
    y = x * lax.rsqrt(ms + NORM_EPS) * g
    return y * (1.0 + mod_ref[0, shift_idx + 1:shift_idx + 2, :]) + mod_ref[0, shift_idx:shift_idx + 1, :]


def _by_part(is_dec, refs, fn):
    if len(refs) == 1:
        fn(refs[0])
        return
    pl.when(jnp.logical_not(is_dec))(lambda: fn(refs[0]))
    pl.when(is_dec)(lambda: fn(refs[1]))


def _rope64(a, cos, sin_signed, first_half):
    partner = jnp.where(first_half, pltpu.roll(a, 96, 1), pltpu.roll(a, 32, 1))
    return a * cos + partner * sin_signed


def _moe_combine(x, y_ref, w_ref, mod_ref):
    w = w_ref[...]

    def expert_out(slot):
        lo, hi = _unpack_bf16_halves(y_ref[slot])
        return jnp.concatenate([lo.astype(F32), hi.astype(F32)], axis=1)

    return x + mod_ref[0, 5:6, :] * (w[:, 0:1] * expert_out(0) + w[:, 1:2] * expert_out(1))


def _nmm_kernel(*refs, n_x, pending_moe, outs, tn, n_ctx_tiles):
    x_refs = refs[:n_x]
    if pending_moe:
        y_ref, wcol_ref, prev_mod_ref = refs[n_x:n_x + 3]
        refs = refs[:n_x] + refs[n_x + 3:]
    g_ref, mod_ref, w_ref, cos_ref, sin_ref = refs[n_x:n_x + 5]
    out_refs = refs[n_x + 5:]
    is_dec = pl.program_id(0) >= n_ctx_tiles
    tm = x_refs[0].shape[0]
    segments = sorted({(col0, width) for col0, width, _, _, _ in outs})

    def rotated(a, kind):
        cos, sin = cos_ref[...], sin_ref[...]
        if kind == "rope64":
            lane = lax.broadcasted_iota(I32, (tm, LANES), 1)
            first_half = (lane & 32) == 0
            return [(c * 128, _rope64(a[:, c * 128:(c + 1) * 128], cos, sin, first_half))
                    for c in range(tn // 128)]
        pieces = []
        for c in range(tn // 256):
            x1, x2 = a[:, c * 256:c * 256 + 128], a[:, c * 256 + 128:(c + 1) * 256]
            pieces += [(c * 256, x1 * cos - x2 * sin), (c * 256 + 128, x2 * cos + x1 * sin)]
        return pieces

    def emit(dec):
        x = x_refs[(1 if dec else 0) if n_x == 2 else 0][...]
        if pending_moe:
            x = _moe_combine(x, y_ref, wcol_ref, prev_mod_ref)
            out_refs[-1][...] = x
        h = _modulated_norm(x, g_ref[...], mod_ref, 0).astype(BF16)
        for col0, width in segments:
            sinks = [(o_ref, o) for o_ref, o in zip(out_refs, outs)
                     if (o[0], o[1]) == (col0, width) and not (dec and o[4] != "all")]
            for blk in range(width // tn):
                acc = jnp.dot(h, w_ref[:, col0 + blk * tn:col0 + (blk + 1) * tn], preferred_element_type=F32)
                for o_ref, (_, _, kind, scale, rows) in sinks:
                    a = acc if scale == 1.0 else acc * scale
                    if rows == "ctx64":
                        n64 = width // 64
                        for c in range(tn // 64):
                            o_ref[pl.ds(blk * tn // 64 + c, tm, stride=n64), :] = (
                                a[:, c * 64:(c + 1) * 64].astype(o_ref.dtype))
                    elif kind == "silu":
                        o_ref[:, blk * tn:(blk + 1) * tn] = jax.nn.silu(a).astype(o_ref.dtype)
                    elif kind == "plain" or rows == "ctx" or not dec:
                        o_ref[:, blk * tn:(blk + 1) * tn] = a.astype(o_ref.dtype)
                    else:
                        for off, val in rotated(a, kind):
                            o_ref[:, blk * tn + off:blk * tn + off + 128] = val.astype(o_ref.dtype)

    pl.when(jnp.logical_not(is_dec))(lambda: emit(False))
    pl.when(is_dec)(lambda: emit(True))


def norm_mod_matmul(lay, x_parts, g, mods, w, outputs, rope_tables, pending_moe=None, tn=512):
    d = x_parts[0].shape[1]
    tm = lay.tm
    n_total = w.shape[1]
    outs, out_shapes, out_specs = [], [], []
    for col0, width, kind, scale, dtype, rows in outputs:
        assert width % tn == 0 and col0 % LANES == 0
        outs.append((col0, width, kind, float(scale), rows))
        if rows == "ctx64":
            n64 = width // 64
            out_shapes.append(jax.ShapeDtypeStruct((lay.t_ctx * n64, 64), dtype))
            out_specs.append(pl.BlockSpec((tm * n64, 64), lambda i: (lay.part_tile(0, i), 0)))
            continue
        n_rows = lay.t_ctx if rows == "ctx" else lay.t
        out_shapes.append(jax.ShapeDtypeStruct((n_rows, width), dtype))
        out_specs.append(lay.row_specs([None, None], width)[0] if rows == "ctx" else lay.row_specs([None], width)[0])
    blocks_per_seq = lay.n_dec // tm
    rope_spec = pl.BlockSpec((tm, LANES), lambda i: (lay.part_tile(1, i) % blocks_per_seq, 0))
    mod_spec = pl.BlockSpec((1, N_MOD, d), lambda i: (lay.mod_row(i), 0, 0))
    moe_specs, moe_args = [], []
    if pending_moe is not None:
        assert len(x_parts) == 1
        moe_specs = [pl.BlockSpec((TOP_K, tm, d // 2), lambda i: (0, i, 0)),
                     pl.BlockSpec((tm, LANES), lambda i: (i, 0)), mod_spec]
        moe_args = list(pending_moe)
        out_shapes.append(jax.ShapeDtypeStruct((lay.t, d), F32))
        out_specs.append(pl.BlockSpec((tm, d), lambda i: (i, 0)))
    return pl.pallas_call(
        functools.partial(_nmm_kernel, n_x=len(x_parts), pending_moe=pending_moe is not None, outs=tuple(outs),
                          tn=tn, n_ctx_tiles=lay.n_ctx_tiles),
        out_shape=out_shapes,
        grid=(lay.n_tiles,),
        in_specs=lay.row_specs(x_parts, d) + moe_specs + [
            pl.BlockSpec((1, d), lambda i: (0, 0)),
            mod_spec,
            pl.BlockSpec((d, n_total), lambda i: (0, 0), pipeline_mode=pl.Buffered(1)),
            rope_spec, rope_spec,
        ],
        out_specs=out_specs,
        compiler_params=_cparams("arbitrary"),
        name="norm_mod_matmul",
    )(*x_parts, *moe_args, g.reshape(1, d), mods, w, *rope_tables)


def _diff_lambda(lam_ref, lam_init):
    lp = lam_ref[...]
    return (jnp.exp(jnp.sum(lp[0:1] * lp[1:2], axis=-1, keepdims=True))
            - jnp.exp(jnp.sum(lp[2:3] * lp[3:4], axis=-1, keepdims=True)) + lam_init)


ONES_ROWS = 16


def _scores_t(k, q, comp):
    lane = lax.broadcasted_iota(I32, q.shape, 1)
    return _nt_dot(k, jnp.where((lane < DA_HEAD_DIM) == (comp == 0), q, jnp.zeros_like(q)))


def _softmax_values_t(scores, vts):
    dv = vts[0].shape[0] - ONES_ROWS
    maxes = [jnp.max(s, axis=0, keepdims=True) for s in scores]
    exps = [jnp.exp2(s - m).astype(BF16) for s, m in zip(scores, maxes)]
    accs = [jnp.dot(vt, e, preferred_element_type=F32) for vt, e in zip(vts, exps)]
    return [acc[0:dv] / acc[dv:dv + 1] for acc in accs]


def _diff_finish_t(parts, lam, g_col, lam_init):
    o = parts[0] - lam * parts[1]
    ms = jnp.mean(o * o, axis=0, keepdims=True)
    return jnp.transpose((o * lax.rsqrt(ms + NORM_EPS) * g_col) * (1.0 - lam_init))


def _diff_attn_kernel(lam_ref, q_ref, k_ref, v_ref, g_ref, o_ref, *, lam_init, heads_per_step):
    hw = 2 * DA_HEAD_DIM
    lam = _diff_lambda(lam_ref, lam_init)
    ones = jnp.ones((ONES_ROWS, k_ref.shape[0]), BF16)
    heads = [slice(hh * hw, (hh + 1) * hw) for hh in range(heads_per_step)]
    scores = [[_scores_t(k_ref[:, cols], q_ref[:, cols], comp) for comp in range(2)] for cols in heads]
    vts = [jnp.concatenate([jnp.transpose(v_ref[:, cols].astype(F32)).astype(BF16), ones], axis=0) for cols in heads]
    parts = _softmax_values_t([s for pair in scores for s in pair], [vt for vt in vts for _ in range(2)])
    for hh, cols in enumerate(heads):
        o_ref[:, cols] = _diff_finish_t(parts[2 * hh:2 * hh + 2], lam, g_ref[...], lam_init).astype(o_ref.dtype)


def _diff_attn_cached_kernel(lam_ref, q_ref, k_ref, v_ref, ck_ref, cv_ref, g_ref, o_ref, k_scr, vt_scr, s_scr, *,
                             lam_init, sub_rows):
    seq_len, hw = q_ref.shape
    n_sub = seq_len // sub_rows
    assert n_sub % 2 == 0
    lam = _diff_lambda(lam_ref, lam_init)
    k_scr[0:seq_len, :] = k_ref[...]
    k_scr[seq_len:, :] = ck_ref[0].astype(BF16)
    vt_scr[0:hw, 0:seq_len] = jnp.transpose(v_ref[...].astype(F32)).astype(BF16)
    vt_scr[0:hw, seq_len:] = jnp.transpose(cv_ref[0]).astype(BF16)
    vt_scr[hw:, :] = jnp.ones((ONES_ROWS, vt_scr.shape[1]), BF16)

    def rows_of(t):
        return pl.ds(pl.multiple_of(t * sub_rows, sub_rows), sub_rows)

    def scores(t, slot, comp):
        s_scr[slot, comp] = _scores_t(k_scr[...], q_ref[rows_of(t), :], comp)

    def stage(t_next, slot_next, t, slot):
        for comp in range(2):
            scores(t_next, slot_next, comp)
        parts = _softmax_values_t([s_scr[slot, 0], s_scr[slot, 1]], [vt_scr[...]] * 2)
        o_ref[rows_of(t), :] = _diff_finish_t(parts, lam, g_ref[...], lam_init).astype(o_ref.dtype)

    scores(0, 0, 0)
    scores(0, 0, 1)

    def body(i2, carry):
        t = 2 * i2
        stage(t + 1, 1, t, 0)
        stage(jnp.minimum(t + 2, n_sub - 1), 0, t + 1, 1)
        return carry

    lax.fori_loop(0, n_sub // 2, body, 0, unroll=2 if n_sub % 4 == 0 else 1)


def diff_attention(q, k, v, row0, batch, seq_len, lam_params, subln_g, lam_init, cache=None, heads_per_step=1,
                   sub_rows=256):
    width = q.shape[1]
    hw = 2 * DA_HEAD_DIM
    bw = heads_per_step * hw
    assert row0 % seq_len == 0 and DA_HEADS % heads_per_step == 0
    s0 = row0 // seq_len
    seq_spec = pl.BlockSpec((seq_len, bw), lambda b, h: (s0 + b, h))
    in_specs = [pl.BlockSpec((4, DA_HEAD_DIM), lambda b, h: (0, 0)), seq_spec, seq_spec, seq_spec]
    args = [lam_params, q, k, v]
    scratch = []
    if cache is None:
        body = functools.partial(_diff_attn_kernel, lam_init=lam_init, heads_per_step=heads_per_step)
    else:
        assert heads_per_step == 1
        past = cache[0].shape[1]
        cache_spec = pl.BlockSpec((1, past, hw), lambda b, h: (b, 0, h))
        in_specs += [cache_spec, cache_spec]
        args += list(cache)
        sub_rows = min(sub_rows, seq_len // 2)
        scratch = [pltpu.VMEM((seq_len + past, hw), BF16), pltpu.VMEM((hw + ONES_ROWS, seq_len + past), BF16),
                   pltpu.VMEM((2, 2, seq_len + past, sub_rows), F32)]
        body = functools.partial(_diff_attn_cached_kernel, lam_init=lam_init, sub_rows=sub_rows)
    in_specs.append(pl.BlockSpec((hw, 1), lambda b, h: (0, 0)))
    args.append(subln_g.reshape(hw, 1))
    return pl.pallas_call(
        body,
        out_shape=jax.ShapeDtypeStruct((batch * seq_len, width), BF16),
        grid=(batch, DA_HEADS // heads_per_step),
        in_specs=in_specs,
        out_specs=pl.BlockSpec((seq_len, bw), lambda b, h: (b, h)),
        scratch_shapes=scratch,
        compiler_params=_cparams("parallel", "parallel"),
        name="diff_attention",
    )(*args)


def _retention_kernel(*refs, seq_len, seqs, has_state, emit_state):
    lg_ref, q_ref, k_ref, v_ref, gf_ref, gb_ref = refs[:6]
    pos = 6
    if has_state:
        s0_refs = refs[6:8]
        pos = 8
    o_ref = refs[pos]
    pos += 1
    if emit_state:
        s_out_refs = refs[pos:pos + 2]
        pos += 2
    s_scr, o_scr = refs[pos:pos + 2]

    h = pl.program_id(1)
    c_len = RET_CHUNK
    n_chunks = seq_len // c_len
    assert n_chunks % 2 == 0
    row = lax.broadcasted_iota(I32, (c_len, c_len), 0)
    colm = lax.broadcasted_iota(I32, (c_len, c_len), 1)
    rel = (row - colm).astype(F32)
    posv = lax.broadcasted_iota(I32, (c_len, 1), 0).astype(F32)

    consts = []
    for backward in (False, True):
        lg = lg_ref[1 if backward else 0, h]
        if backward:
            intra = jnp.where(rel <= 0, jnp.exp(-rel * lg), 0.0)
            q_decay = jnp.exp((c_len - posv) * lg)
            k_decay = jnp.exp(posv * lg)
        else:
            intra = jnp.where(rel >= 0, jnp.exp(rel * lg), 0.0)
            q_decay = jnp.exp((posv + 1.0) * lg)
            k_decay = jnp.exp((c_len - 1.0 - posv) * lg)
        consts.append((intra, q_decay, k_decay, jnp.exp(jnp.zeros((1, 1), F32) + c_len * lg)))
        direction = 1 if backward else 0
        for sq in range(seqs):
            if has_state:
                s_scr[sq, direction] = s0_refs[direction][sq, 0, 0]
            else:
                s_scr[sq, direction] = jnp.zeros(s_scr.shape[2:], F32)

    def body(ci, first_touch):
        chains = [(sq, direction, pl.ds(pl.multiple_of(sq * seq_len + c * c_len, c_len), c_len))
                  for sq in range(seqs) for direction, c in ((0, ci), (1, n_chunks - 1 - ci))]
        qs = [q_ref[rows, :].astype(BF16) for _, _, rows in chains]
        ks = [k_ref[rows, :].astype(F32) for _, _, rows in chains]
        vs = [v_ref[rows, :].astype(BF16) for _, _, rows in chains]
        intras = [(_nt_dot(qb, kf.astype(BF16)) * consts[d][0]).astype(BF16)
                  for (_, d, _), qb, kf in zip(chains, qs, ks)]
        states = [s_scr[sq, d] for sq, d, _ in chains]
        outs = [jnp.dot(a, vb, preferred_element_type=F32)
                + jnp.dot(qb, s.astype(BF16), preferred_element_type=F32) * consts[d][1]
                for (_, d, _), a, qb, vb, s in zip(chains, intras, qs, vs, states)]
        for (sq, d, _), kf, vb, s in zip(chains, ks, vs, states):
            s_scr[sq, d] = consts[d][3] * s + _tn_dot((kf * consts[d][2]).astype(BF16), vb)
        centred = [o - jnp.mean(o, axis=-1, keepdims=True) for o in outs]
        scales = [lax.rsqrt(jnp.mean(oc * oc, axis=-1, keepdims=True) + NORM_EPS) for oc in centred]
        for (_, d, rows), oc, scale in zip(chains, centred, scales):
            gated = oc * scale * (gb_ref if d else gf_ref)[rows, :].astype(F32)
            if first_touch:
                o_scr[rows, :] = gated
            else:
                o_ref[rows, :] = (o_scr[rows, :] + gated).astype(o_ref.dtype)

    unroll = 4 if n_chunks % 8 == 0 else 1
    lax.fori_loop(0, n_chunks // 2, lambda ci, c: (body(ci, True), c)[1], 0, unroll=unroll)
    lax.fori_loop(n_chunks // 2, n_chunks, lambda ci, c: (body(ci, False), c)[1], 0, unroll=unroll)
    if emit_state:
        for d in range(2):
            for sq in range(seqs):
                s_out_refs[d][sq, 0, 0] = s_scr[sq, d]


def retention(q, k, v, gates, row0, batch, seq_len, log_decay, state=None, emit_state=False, seqs_per_step=1):
    dk = q.shape[1] // RET_HEADS
    dv = v.shape[1] // RET_HEADS
    seqs = seqs_per_step
    rows = seqs * seq_len
    assert row0 % rows == 0 and batch % seqs == 0
    s0 = row0 // rows
    in_specs = [
        pl.BlockSpec(memory_space=pltpu.SMEM),
        pl.BlockSpec((rows, dk), lambda b, h: (s0 + b, h)),
        pl.BlockSpec((rows, dk), lambda b, h: (s0 + b, h)),
        pl.BlockSpec((rows, dv), lambda b, h: (s0 + b, h)),
        pl.BlockSpec((rows, dv), lambda b, h: (s0 + b, h)),
        pl.BlockSpec((rows, dv), lambda b, h: (s0 + b, RET_HEADS + h)),
    ]
    args = [log_decay, q, k, v, gates, gates]
    state_spec = pl.BlockSpec((seqs, 1, 1, dk, dv), lambda b, h: (b, 0, h, 0, 0))
    if state is not None:
        in_specs += [state_spec, state_spec]
        args += list(state)
    out_shapes = [jax.ShapeDtypeStruct((batch * seq_len, RET_HEADS * dv), BF16)]
    out_specs = [pl.BlockSpec((rows, dv), lambda b, h: (b, h))]
    if emit_state:
        out_shapes += [jax.ShapeDtypeStruct((batch, 1, RET_HEADS, dk, dv), F32)] * 2
        out_specs += [state_spec, state_spec]
    return pl.pallas_call(
        functools.partial(_retention_kernel, seq_len=seq_len, seqs=seqs, has_state=state is not None,
                          emit_state=emit_state),
        out_shape=out_shapes,
        grid=(batch // seqs, RET_HEADS),
        in_specs=in_specs,
        out_specs=out_specs,
        scratch_shapes=[pltpu.VMEM((seqs, 2, dk, dv), F32), pltpu.VMEM((rows, dv), F32)],
        compiler_params=_cparams("parallel", "parallel"),
        name="retention",
    )(*args)


def _pack_bf16_halves(x):
    half = x.shape[1] // 2
    lo = lax.bitcast_convert_type(x[:, :half].astype(F32), jnp.uint32)
    hi = lax.bitcast_convert_type(x[:, half:].astype(F32), jnp.uint32)
    return (hi & jnp.uint32(0xFFFF0000)) | (lo >> 16)


def _unpack_bf16_halves(p):
    lo = lax.bitcast_convert_type(p << 16, F32).astype(BF16)
    hi = lax.bitcast_convert_type(p & jnp.uint32(0xFFFF0000), F32).astype(BF16)
    return lo, hi


def _first_max_onehot(vals):
    m = vals[0]
    for v in vals[1:]:
        m = jnp.maximum(m, v)
    onehot, taken = [], None
    for v in vals:
        hit = v == m
        if taken is None:
            onehot.append(hit)
            taken = hit
        else:
            onehot.append(hit & jnp.logical_not(taken))
            taken = taken | hit
    return m, onehot


def _pick(onehot, vals):
    out = vals[-1]
    for oh, v in zip(onehot[-2::-1], vals[-2::-1]):
        out = jnp.where(oh, v, out)
    return out


def _route_rows(x, g_ref, mod_ref, wr_ref, bias_ref, tri_ref, h_ref, idx_ref, wcol_ref, rank_ref, cnt_ref,
                ind_scr, wrow_scr):
    hb = _modulated_norm(x, g_ref[...], mod_ref, 3).astype(BF16)
    h_ref[...] = _pack_bf16_halves(hb)
    logits = _nt_dot(wr_ref[...], hb)
    s = jax.nn.sigmoid(logits)
    sel = s + bias_ref[...]
    neg_inf = jnp.full_like(sel[0:1], -jnp.inf)
    sel_rows = [sel[e:e + 1] for e in range(N_EXPERTS)]
    s_rows = [s[e:e + 1] for e in range(N_EXPERTS)]

    def top2(vals):
        m1, oh1 = _first_max_onehot(vals)
        rest = [jnp.where(o, neg_inf, v) for o, v in zip(oh1, vals)]
        m2, oh2 = _first_max_onehot(rest)
        return m1, m2, oh1, oh2

    grp_scores = []
    for g in range(N_GROUPS):
        m1, m2, _, _ = top2(sel_rows[g * EXPERTS_PER_GROUP:(g + 1) * EXPERTS_PER_GROUP])
        grp_scores.append(m1 + m2)
    _, in_grp = _first_max_onehot(grp_scores)
    cand_sel = [_pick(in_grp, [sel_rows[g * EXPERTS_PER_GROUP + k] for g in range(N_GROUPS)])
                for k in range(EXPERTS_PER_GROUP)]
    cand_s = [_pick(in_grp, [s_rows[g * EXPERTS_PER_GROUP + k] for g in range(N_GROUPS)])
              for k in range(EXPERTS_PER_GROUP)]
    _, _, oh1, oh2 = top2(cand_sel)
    w1 = _pick(oh1, cand_s)
    w2 = _pick(oh2, cand_s)
    denom = w1 + w2
    wrow_scr[...] = jnp.zeros_like(wrow_scr)
    wrow_scr[0:1, :] = w1 / denom
    wrow_scr[1:2, :] = w2 / denom
    wcol_ref[...] = jnp.transpose(wrow_scr[...])

    ints = [jnp.full(w1.shape, k, I32) for k in range(EXPERTS_PER_GROUP)]
    grp = _pick(in_grp, ints) * EXPERTS_PER_GROUP
    idx_ref[0:1, :] = grp + _pick(oh1, ints)
    idx_ref[1:2, :] = grp + _pick(oh2, ints)

    one, zero = jnp.ones_like(w1), jnp.zeros_like(w1)
    for g in range(N_GROUPS):
        for k in range(EXPERTS_PER_GROUP):
            e = g * EXPERTS_PER_GROUP + k
            ind_scr[e:e + 1, :] = jnp.where(in_grp[g] & (oh1[k] | oh2[k]), one, zero)
    ind = ind_scr[...].astype(BF16)
    ranks = jnp.dot(ind, tri_ref[...], preferred_element_type=F32)
    cnt_ref[0] = jnp.dot(ind, jnp.ones((ind.shape[1], LANES), BF16), preferred_element_type=F32)
    for slot, oh in ((0, oh1), (1, oh2)):
        r = zero
        for g in range(N_GROUPS):
            for k in range(EXPERTS_PER_GROUP):
                e = g * EXPERTS_PER_GROUP + k
                r = r + jnp.where(in_grp[g] & oh[k], ranks[e:e + 1], zero)
        rank_ref[slot:slot + 1, :] = r.astype(I32)


def _proj_route_kernel(*refs, n_x, n_a, n_ctx_tiles):
    x_refs, a_refs = refs[:n_x], refs[n_x:n_x + n_a]
    w_ref, mod_ref, g_ref, wr_ref, bias_ref, tri_ref, o_ref = refs[n_x + n_a:n_x + n_a + 7]
    route_out_refs = refs[n_x + n_a + 7:n_x + n_a + 12]
    y_scr, ind_scr, wrow_scr = refs[n_x + n_a + 12:]
    is_dec = pl.program_id(0) >= n_ctx_tiles

    def project(a_ref):
        y_scr[...] = mod_ref[0, 2:3, :] * jnp.dot(a_ref[...].astype(BF16), w_ref[...],
                                                  preferred_element_type=F32)

    def add(x_ref):
        o_ref[...] = x_ref[...] + y_scr[...]

    _by_part(is_dec, a_refs, project)
    _by_part(is_dec, x_refs, add)
    _route_rows(o_ref[...], g_ref, mod_ref, wr_ref, bias_ref, tri_ref, *route_out_refs, ind_scr, wrow_scr)


def proj_residual_route(lay, x_parts, a_parts, w, mods, ffn_g, w_router_t, router_bias):
    d = x_parts[0].shape[1]
    kdim = a_parts[0].shape[1]
    tm, t = lay.tm, lay.t
    tri = jnp.triu(jnp.ones((tm, tm), BF16), k=1)
    pair = jax.ShapeDtypeStruct((TOP_K, t), I32)
    pair_spec = pl.BlockSpec((TOP_K, tm), lambda i: (0, i))
    const = lambda shape: pl.BlockSpec(shape, lambda i: (0,) * len(shape))
    return pl.pallas_call(
        functools.partial(_proj_route_kernel, n_x=len(x_parts), n_a=len(a_parts), n_ctx_tiles=lay.n_ctx_tiles),
        out_shape=[jax.ShapeDtypeStruct((t, d), F32), jax.ShapeDtypeStruct((t, d // 2), jnp.uint32), pair,
                   jax.ShapeDtypeStruct((t, LANES), F32), pair,
                   jax.ShapeDtypeStruct((lay.n_tiles, N_EXPERTS, LANES), F32)],
        grid=(lay.n_tiles,),
        in_specs=lay.row_specs(x_parts, d) + lay.row_specs(a_parts, kdim) + [
            const((kdim, d)),
            pl.BlockSpec((1, N_MOD, d), lambda i: (lay.mod_row(i), 0, 0)),
            const((1, d)), const((N_EXPERTS, d)), const((N_EXPERTS, 1)), const((tm, tm)),
        ],
        out_specs=[pl.BlockSpec((tm, d), lambda i: (i, 0)), pl.BlockSpec((tm, d // 2), lambda i: (i, 0)), pair_spec,
                   pl.BlockSpec((tm, LANES), lambda i: (i, 0)), pair_spec,
                   pl.BlockSpec((1, N_EXPERTS, LANES), lambda i: (i, 0, 0))],
        scratch_shapes=[pltpu.VMEM((tm, d), F32), pltpu.VMEM((N_EXPERTS, tm), F32), pltpu.VMEM((LANES, tm), F32)],
        compiler_params=_cparams("arbitrary"),
        name="proj_residual_route",
    )(*x_parts, *a_parts, w, mods, ffn_g.reshape(1, d), w_router_t, router_bias.reshape(N_EXPERTS, 1), tri)


def dispatch_plan(lay, idx, rank, cnt, n_sorted):
    tm = lay.tm
    cnt_tile = cnt[:, :, 0].astype(I32)
    total = jnp.sum(cnt_tile, axis=0)
    padded = ((total + EXPERT_TILE - 1) // EXPERT_TILE) * EXPERT_TILE
    end = jnp.cumsum(padded)
    start = end - padded
    base = start[None, :] + jnp.cumsum(cnt_tile, axis=0) - cnt_tile
    idx3 = idx.reshape(TOP_K, lay.n_tiles, tm)
    pos = rank.reshape(TOP_K, lay.n_tiles, tm)
    for e in range(N_EXPERTS):
        pos = pos + jnp.where(idx3 == e, base[None, :, e, None], 0)
    tile_row = jnp.arange(n_sorted // EXPERT_TILE, dtype=I32) * EXPERT_TILE
    tile_expert = jnp.minimum(jnp.sum(end[None, :] <= tile_row[:, None], axis=1), N_EXPERTS - 1).astype(I32)
    n_valid = (end[-1] // EXPERT_TILE).astype(I32).reshape(1)
    return pos.reshape(TOP_K * lay.t).astype(I32), tile_expert, n_valid


def _sc_mesh():
    return plsc.VectorSubcoreMesh(core_axis_name="c", subcore_axis_name="s")


def _sc_worker_id():
    return lax.axis_index("s") * V7X_SC_CORES + lax.axis_index("c")


def _sc_two_stage_pipeline(n_chunks, fetch_idx, load, store):
    assert n_chunks % 2 == 0
    fetch_idx(0, 0)
    load(0, 0).start()

    @pl.loop(0, n_chunks // 2)
    def _(p):
        j = 2 * p

        @pl.when(p > 0)
        def _():
            store(j - 1, 1).wait()

        fetch_idx(j + 1, 1)
        load(j + 1, 1).start()
        load(j, 0).wait()
        store(j, 0).start()
        store(j, 0).wait()

        @pl.when(j + 2 < n_chunks)
        def _():
            fetch_idx(j + 2, 0)
            load(j + 2, 0).start()

        load(j + 1, 1).wait()
        store(j + 1, 1).start()

    store(n_chunks - 1, 1).wait()


_SC_SCRATCH = lambda chunk, d, dtype: [
    pltpu.VMEM((chunk,), I32), pltpu.VMEM((chunk,), I32),
    pltpu.VMEM((chunk, d), dtype), pltpu.VMEM((chunk, d), dtype),
    pltpu.SemaphoreType.DMA, pltpu.SemaphoreType.DMA, pltpu.SemaphoreType.DMA, pltpu.SemaphoreType.DMA]


def sc_scatter_rows(src, pos, n_out_rows):
    t, d = src.shape
    n_idx = pos.shape[0]
    per_worker = n_idx // SC_WORKERS
    chunk = SC_CHUNK_ROWS
    assert n_idx % (SC_WORKERS * chunk * 2) == 0 and t % chunk == 0

    @functools.partial(
        pl.kernel, mesh=_sc_mesh(), out_type=jax.ShapeDtypeStruct((n_out_rows, d), src.dtype),
        scratch_types=_SC_SCRATCH(chunk, d, src.dtype), name="sc_scatter_rows")
    def scatter(src_hbm, pos_hbm, out_hbm, idx_a, idx_b, rows_a, rows_b, lsem_a, lsem_b, ssem_a, ssem_b):
        base = _sc_worker_id() * per_worker
        idx, rows, lsem, ssem = (idx_a, idx_b), (rows_a, rows_b), (lsem_a, lsem_b), (ssem_a, ssem_b)

        def fetch_idx(j, b):
            pltpu.sync_copy(pos_hbm.at[pl.ds(base + j * chunk, chunk)], idx[b])

        def load(j, b):
            return pltpu.make_async_copy(src_hbm.at[pl.ds(lax.rem(base + j * chunk, t), chunk)], rows[b], lsem[b])

        def store(j, b):
            return pltpu.make_async_copy(rows[b], out_hbm.at[idx[b]], ssem[b])

        _sc_two_stage_pipeline(per_worker // chunk, fetch_idx, load, store)

    return scatter(src, pos)


def sc_gather_rows(table, idx):
    _, d = table.shape
    n_idx = idx.shape[0]
    per_worker = n_idx // SC_WORKERS
    chunk = SC_CHUNK_ROWS
    assert n_idx % (SC_WORKERS * chunk * 2) == 0

    @functools.partial(
        pl.kernel, mesh=_sc_mesh(), out_type=jax.ShapeDtypeStruct((n_idx, d), table.dtype),
        scratch_types=_SC_SCRATCH(chunk, d, table.dtype), name="sc_gather_rows")
    def gather(table_hbm, idx_hbm, out_hbm, idx_a, idx_b, rows_a, rows_b, lsem_a, lsem_b, ssem_a, ssem_b):
        base = _sc_worker_id() * per_worker
        idx, rows, lsem, ssem = (idx_a, idx_b), (rows_a, rows_b), (lsem_a, lsem_b), (ssem_a, ssem_b)

        def fetch_idx(j, b):
            pltpu.sync_copy(idx_hbm.at[pl.ds(base + j * chunk, chunk)], idx[b])

        def load(j, b):
            return pltpu.make_async_copy(table_hbm.at[idx[b]], rows[b], lsem[b])

        def store(j, b):
            return pltpu.make_async_copy(rows[b], out_hbm.at[pl.ds(base + j * chunk, chunk)], ssem[b])

        _sc_two_stage_pipeline(per_worker // chunk, fetch_idx, load, store)

    return gather(table, idx)


def _experts_kernel(te_ref, nv_ref, x_ref, wg_ref, wu_ref, wd_ref, y_ref, wg_scr, wu_scr, wd_scr):
    i = pl.program_id(0)

    @pl.when(i < nv_ref[0])
    def _():
        @pl.when((i == 0) | (te_ref[i] != te_ref[jnp.maximum(i - 1, 0)]))
        def _():
            wg_scr[...] = wg_ref[0, 0].astype(BF16)
            wu_scr[...] = wu_ref[0, 0].astype(BF16)
            wd_scr[...] = wd_ref[0, 0].astype(BF16)

        x_lo, x_hi = _unpack_bf16_halves(x_ref[...])
        half = x_lo.shape[1]

        def in_proj(w_scr):
            return (jnp.dot(x_lo, w_scr[0:half, :], preferred_element_type=F32)
                    + jnp.dot(x_hi, w_scr[half:, :], preferred_element_type=F32))

        a = jax.nn.silu(in_proj(wg_scr)) * in_proj(wu_scr)
        y = jnp.dot(a.astype(BF16), wd_scr[...], preferred_element_type=F32)
        y_ref[...] = _pack_bf16_halves(y.astype(BF16))


def grouped_experts(xs, tile_expert, n_valid, wg, wu, wd, layer):
    n_rows = xs.shape[0]
    d, de = wg.shape[-2:]
    tm = EXPERT_TILE
    row_map = lambda i, te, nv: (jnp.minimum(i, nv[0] - 1), 0)
    grid_spec = pltpu.PrefetchScalarGridSpec(
        num_scalar_prefetch=2,
        grid=(n_rows // tm,),
        in_specs=[
            pl.BlockSpec((tm, d // 2), row_map),
            pl.BlockSpec((1, 1, d, de), lambda i, te, nv: (layer, te[i], 0, 0)),
            pl.BlockSpec((1, 1, d, de), lambda i, te, nv: (layer, te[i], 0, 0)),
            pl.BlockSpec((1, 1, de, d), lambda i, te, nv: (layer, te[i], 0, 0)),
        ],
        out_specs=pl.BlockSpec((tm, d // 2), row_map),
        scratch_shapes=[pltpu.VMEM((d, de), BF16), pltpu.VMEM((d, de), BF16), pltpu.VMEM((de, d), BF16)],
    )
    return pl.pallas_call(
        _experts_kernel,
        out_shape=jax.ShapeDtypeStruct((n_rows, d // 2), jnp.uint32),
        grid_spec=grid_spec,
        compiler_params=_cparams("arbitrary"),
        name="grouped_experts",
    )(tile_expert, n_valid, xs, wg, wu, wd)


def _combine_kernel(x_ref, y_ref, w_ref, mod_ref, fg_ref, *o_refs, final_norm, n_ctx_tiles):
    out = _moe_combine(x_ref[...], y_ref, w_ref, mod_ref)
    if final_norm:
        ms = jnp.mean(out * out, axis=-1, keepdims=True)
        out = out * lax.rsqrt(ms + NORM_EPS) * fg_ref[...]

    def store(o_ref):
        o_ref[...] = out
    _by_part(pl.program_id(0) >= n_ctx_tiles, o_refs, store)


def combine_residual(lay, x, y_pair, w_col, mods, final_g, final_norm, split):
    t, d = x.shape
    tm = lay.tm
    if split:
        out_shape = [jax.ShapeDtypeStruct((lay.t_ctx, d), F32), jax.ShapeDtypeStruct((lay.t_dec, d), F32)]
        out_specs = lay.row_specs([None, None], d)
    else:
        out_shape = [jax.ShapeDtypeStruct((t, d), F32)]
        out_specs = lay.row_specs([None], d)
    return pl.pallas_call(
        functools.partial(_combine_kernel, final_norm=final_norm, n_ctx_tiles=lay.n_ctx_tiles),
        out_shape=out_shape,
        grid=(lay.n_tiles,),
        in_specs=[
            pl.BlockSpec((tm, d), lambda i: (i, 0)),
            pl.BlockSpec((TOP_K, tm, d // 2), lambda i: (0, i, 0)),
            pl.BlockSpec((tm, LANES), lambda i: (i, 0)),
            pl.BlockSpec((1, N_MOD, d), lambda i: (lay.mod_row(i), 0, 0)),
            pl.BlockSpec((1, d), lambda i: (0, 0)),
        ],
        out_specs=out_specs,
        compiler_params=_cparams("arbitrary"),
        name="combine_residual",
    )(x, y_pair, w_col, mods, final_g.reshape(1, d))


def group_moe(lay, x, routing, p, layer):
    t, d = x.shape
    h, idx, w_col, rank, cnt = routing
    n_assign = TOP_K * t
    row_quantum = SC_WORKERS * SC_CHUNK_ROWS
    n_sorted = n_assign + N_EXPERTS * (EXPERT_TILE - 1)
    n_sorted = -(-n_sorted // row_quantum) * row_quantum
    n_sorted = -(-n_sorted // EXPERT_TILE) * EXPERT_TILE
    pos, tile_expert, n_valid = dispatch_plan(lay, idx, rank, cnt, n_sorted)
    xs = sc_scatter_rows(h, pos, n_sorted)
    ys = grouped_experts(xs, tile_expert, n_valid, p["moe_w_gate"], p["moe_w_up"], p["moe_w_down"], layer)
    return sc_gather_rows(ys, pos).reshape(TOP_K, t, d // 2), w_col


def _rope_angles(n, d):
    n_rows = n // GRID_W
    row = jnp.repeat(jnp.arange(n_rows), GRID_W).astype(F32)
    col = jnp.tile(jnp.arange(GRID_W), n_rows).astype(F32)
    nf = d // 4
    freqs = jnp.power(ROPE_BASE, -jnp.arange(nf, dtype=F32) / nf)
    ang = jnp.concatenate([row[:, None] * freqs, col[:, None] * freqs], axis=-1)
    return jnp.cos(ang), jnp.sin(ang)


def kernel(x_prompt, x_sample, cache_attn_k, cache_attn_v, state_ret_fwd, state_ret_bwd, c, c_ctx, w_ada, b_ada, norm_mix_g, norm_ffn_g, final_norm_g, da_w_qkv, da_lambda_q1, da_lambda_k1, da_lambda_q2, da_lambda_k2, da_subln_g, da_w_o, ret_w_qkv, ret_w_gate_fwd, ret_w_gate_bwd, ret_decay_fwd, ret_decay_bwd, ret_w_o, w_router, router_bias, moe_w_gate, moe_w_up, moe_w_down):
    b_ctx, n_ctx, d = x_prompt.shape
    b_dec, n_dec, _ = x_sample.shape
    past = cache_attn_k.shape[2]
    n_attn = cache_attn_k.shape[1]
    depth = w_ada.shape[0]
    assert b_dec + 1 <= MOD_ROWS
    lay = Layout(b_ctx, n_ctx, b_dec, n_dec)

    cond = jnp.zeros((MOD_ROWS, d), F32).at[0].set(c_ctx).at[1:1 + b_dec].set(c)
    mods_all = ada_modulation(cond, w_ada, b_ada)

    p = {
        "norm_ffn_g": norm_ffn_g, "final_norm_g": final_norm_g,
        "w_router_t": w_router.T.astype(BF16), "router_bias": router_bias.astype(F32),
        "moe_w_gate": moe_w_gate, "moe_w_up": moe_w_up, "moe_w_down": moe_w_down,
    }
    ret_log_decay = jnp.stack([jax.nn.log_sigmoid(ret_decay_fwd.astype(F32)),
                               jax.nn.log_sigmoid(ret_decay_bwd.astype(F32))], axis=1)
    ck_all = cache_attn_k.reshape(b_dec, n_attn, past, -1)
    cv_all = cache_attn_v.reshape(b_dec, n_attn, past, -1)

    x_parts = [x_prompt.reshape(lay.t_ctx, d), x_sample.reshape(lay.t_dec, d)]
    new_k, new_v, new_sf, new_sb = [], [], [], []
    pending_moe = None

    def first_kernel(*args):
        nonlocal x_parts, pending_moe
        outs = norm_mod_matmul(lay, x_parts, *args, pending_moe=pending_moe)
        if pending_moe is not None:
            *outs, x_joint = outs
            x_parts, pending_moe = [x_joint], None
        return outs

    for i in range(depth):
        mods = mods_all[i]
        j = i // 2
        if i % 2 == 0:
            lam_init = 0.8 - 0.6 * math.exp(-0.3 * i)
            qkw = DA_HEADS * 2 * DA_HEAD_DIM
            vw = DA_HEADS * DA_V_DIM
            cos, sin = _rope_angles(n_dec, DA_HEAD_DIM)
            rope = (jnp.tile(cos, (1, 4)), jnp.concatenate([-sin, sin, -sin, sin], axis=-1))
            q, k, v, k_ctx, v_ctx = first_kernel(
                norm_mix_g[i], mods, da_w_qkv[j].astype(BF16),
                [(0, qkw, "rope64", math.log2(math.e) * DA_HEAD_DIM ** -0.5, BF16, "all"),
                 (qkw, qkw, "rope64", 1.0, BF16, "all"),
                 (2 * qkw, vw, "plain", 1.0, BF16, "all"),
                 (qkw, qkw, "plain", 1.0, F32, "ctx64"), (2 * qkw, vw, "plain", 1.0, F32, "ctx")],
                rope)
            lam_params = jnp.stack([da_lambda_q1[j], da_lambda_k1[j], da_lambda_q2[j], da_lambda_k2[j]])
            mix = [diff_attention(q, k, v, 0, b_ctx, n_ctx, lam_params, da_subln_g[j], lam_init,
                                  heads_per_step=DA_HEADS),
                   diff_attention(q, k, v, lay.t_ctx, b_dec, n_dec, lam_params, da_subln_g[j], lam_init,
                                  cache=(ck_all[:, j], cv_all[:, j]))]
            mix_w = da_w_o[j].astype(BF16)
            new_k.append(k_ctx.reshape(b_ctx, n_ctx, DA_HEADS, 2, DA_HEAD_DIM))
            new_v.append(v_ctx.reshape(b_ctx, n_ctx, DA_HEADS, DA_V_DIM))
        else:
            kd = ret_w_qkv.shape[2] // 4
            dv = 2 * kd
            w_all = jnp.concatenate([ret_w_qkv[j], ret_w_gate_fwd[j], ret_w_gate_bwd[j]], axis=-1).astype(BF16)
            q, k, v, gates = first_kernel(
                norm_mix_g[i], mods, w_all,
                [(0, kd, "rope256", 1.0, BF16, "all"),
                 (kd, kd, "rope256", (kd // RET_HEADS) ** -0.5, F32, "all"),
                 (2 * kd, dv, "plain", 1.0, BF16, "all"), (2 * kd + dv, 2 * dv, "silu", 1.0, BF16, "all")],
                _rope_angles(n_dec, kd // RET_HEADS))
            ctx_seqs = math.gcd(b_ctx, max(1, n_dec // n_ctx // 2))
            o_ctx, sf, sb = retention(q, k, v, gates, 0, b_ctx, n_ctx, ret_log_decay[j], emit_state=True,
                                      seqs_per_step=ctx_seqs)
            (o_dec,) = retention(q, k, v, gates, lay.t_ctx, b_dec, n_dec, ret_log_decay[j],
                                 state=(state_ret_fwd[:, j:j + 1], state_ret_bwd[:, j:j + 1]))
            mix, mix_w = [o_ctx, o_dec], ret_w_o[j].astype(BF16)
            new_sf.append(sf)
            new_sb.append(sb)
        x, *routing = proj_residual_route(lay, x_parts, mix, mix_w, mods, norm_ffn_g[i], p["w_router_t"],
                                          p["router_bias"])
        y_pair, w_col = group_moe(lay, x, routing, p, i)
        if i == depth - 1:
            x_parts = combine_residual(lay, x, y_pair, w_col, mods, final_norm_g, final_norm=True, split=True)
        else:
            x_parts, pending_moe = [x], (y_pair, w_col, mods)

    y_ctx, y_dec = x_parts
    return (y_ctx.reshape(b_ctx, n_ctx, d), y_dec.reshape(b_dec, n_dec, d),
            jnp.stack(new_k, axis=1), jnp.stack(new_v, axis=1),
            jnp.concatenate(new_sf, axis=1), jnp.concatenate(new_sb, axis=1))
```

```python
import functools
import math

import jax
import jax.numpy as jnp
from jax import lax
from jax.experimental import pallas as pl
from jax.experimental.pallas import tpu as pltpu
from jax.experimental.pallas import tpu_sc as plsc

F32 = jnp.float32
BF16 = jnp.bfloat16
I32 = jnp.int32

GRID_W = 64
ROPE_BASE = 10000.0
NORM_EPS = 1e-6
DA_HEADS = 8
DA_HEAD_DIM = 64
DA_V_DIM = 2 * DA_HEAD_DIM
RET_HEADS = 4
RET_CHUNK = 128
N_EXPERTS = 16
N_GROUPS = 4
EXPERTS_PER_GROUP = N_EXPERTS // N_GROUPS
TOP_K = 2
N_MOD = 6
MOD_ROWS = 16
LANES = 128

V7X_VMEM_LIMIT = 56 * 1024 * 1024
V7X_SC_CORES = 2
V7X_SC_SUBCORES = 16
SC_WORKERS = V7X_SC_CORES * V7X_SC_SUBCORES
SC_CHUNK_ROWS = 64

ROW_TILE = 512
EXPERT_TILE = 512
ROUTE_SPLIT = 2


def _cparams(*sem):
    return pltpu.CompilerParams(dimension_semantics=sem, vmem_limit_bytes=V7X_VMEM_LIMIT)


def _nt_dot(a, b):
    return lax.dot_general(a, b, (((1,), (1,)), ((), ())), preferred_element_type=F32)


def _tn_dot(a, b):
    return lax.dot_general(a, b, (((0,), (0,)), ((), ())), preferred_element_type=F32)


class Layout:
    def __init__(self, b_ctx, n_ctx, b_dec, n_dec):
        self.b_ctx, self.n_ctx, self.b_dec, self.n_dec = b_ctx, n_ctx, b_dec, n_dec
        self.t_ctx, self.t_dec = b_ctx * n_ctx, b_dec * n_dec
        self.t = self.t_ctx + self.t_dec
        self.tm = min(ROW_TILE, n_dec, self.t_ctx)
        assert self.t_ctx % self.tm == 0 and n_dec % self.tm == 0
        assert self.t_ctx % n_dec == 0 and self.t_ctx % n_ctx == 0
        self.n_ctx_tiles = self.t_ctx // self.tm
        self.n_tiles = self.t // self.tm

    def mod_row(self, i):
        r = i * self.tm
        return jnp.where(r < self.t_ctx, 0, 1 + (r - self.t_ctx) // self.n_dec)

    def part_tile(self, part, i):
        if part == 0:
            return jnp.minimum(i, self.n_ctx_tiles - 1)
        return jnp.maximum(i - self.n_ctx_tiles, 0)

    def row_specs(self, arrays, width, extra_grid_dims=0):
        pad = (0,) * 0
        if len(arrays) == 1:
            return [pl.BlockSpec((self.tm, width), lambda i, *_: (i, 0))]
        return [pl.BlockSpec((self.tm, width), lambda i, *_, p=p: (self.part_tile(p, i), 0)) for p in (0, 1)]


def _ada_kernel(c_ref, w_ref, b_ref, o_ref):
    s = jax.nn.silu(c_ref[...]).astype(BF16)
    acc = jnp.dot(s, w_ref[0].astype(BF16), preferred_element_type=F32)
    o_ref[0] = acc + b_ref[0]


def ada_modulation(cond, w_ada, b_ada):
    depth, d, n = w_ada.shape
    tn = 1536
    out = pl.pallas_call(
        _ada_kernel,
        out_shape=jax.ShapeDtypeStruct((depth, MOD_ROWS, n), F32),
        grid=(depth, n // tn),
        in_specs=[
            pl.BlockSpec((MOD_ROWS, d), lambda l, j: (0, 0)),
            pl.BlockSpec((1, d, tn), lambda l, j: (l, 0, j)),
            pl.BlockSpec((1, 1, tn), lambda l, j: (l, 0, j)),
        ],
        out_specs=pl.BlockSpec((1, MOD_ROWS, tn), lambda l, j: (l, 0, j)),
        compiler_params=_cparams("parallel", "parallel"),
        name="ada_modulation",
    )(cond, w_ada, b_ada.reshape(depth, 1, n))
    return out.reshape(depth, MOD_ROWS, N_MOD, d)


def _modulated_norm(x, g, mod_ref, shift_idx):
    ms = jnp.mean(x * x, axis=-1, keepdims=True)
    y = x * lax.rsqrt(ms + NORM_EPS) * g
    return y * (1.0 + mod_ref[0, shift_idx + 1:shift_idx + 2, :]) + mod_ref[0, shift_idx:shift_idx + 1, :]


def _by_part(is_dec, refs, fn):
    if len(refs) == 1:
        fn(refs[0])
        return
    pl.when(jnp.logical_not(is_dec))(lambda: fn(refs[0]))
    pl.when(is_dec)(lambda: fn(refs[1]))


def _rope64(a, cos, sin_signed, first_half):
    partner = jnp.where(first_half, pltpu.roll(a, 96, 1), pltpu.roll(a, 32, 1))
    return a * cos + partner * sin_signed


def _moe_combine(x, y_ref, w_ref, mod_ref):
    w = w_ref[...]

    def expert_out(slot):
        lo, hi = _unpack_bf16_halves(y_ref[slot])
        return jnp.concatenate([lo.astype(F32), hi.astype(F32)], axis=1)

    return x + mod_ref[0, 5:6, :] * (w[:, 0:1] * expert_out(0) + w[:, 1:2] * expert_out(1))


def _nmm_kernel(*refs, n_x, pending_moe, outs, tn, n_ctx_tiles):
    x_refs = refs[:n_x]
    if pending_moe:
        y_ref, wcol_ref, prev_mod_ref = refs[n_x:n_x + 3]
        refs = refs[:n_x] + refs[n_x + 3:]
    g_ref, mod_ref, w_ref, cos_ref, sin_ref = refs[n_x:n_x + 5]
    out_refs = refs[n_x + 5:]
    is_dec = pl.program_id(0) >= n_ctx_tiles
    tm = x_refs[0].shape[0]
    segments = sorted({(col0, width) for col0, width, _, _, _ in outs})

    def rotated(a, kind):
        cos, sin = cos_ref[...], sin_ref[...]
        if kind == "rope64":
            lane = lax.broadcasted_iota(I32, (tm, LANES), 1)
            first_half = (lane & 32) == 0
            return [(c * 128, _rope64(a[:, c * 128:(c + 1) * 128], cos, sin, first_half))
                    for c in range(tn // 128)]
        pieces = []
        for c in range(tn // 256):
            x1, x2 = a[:, c * 256:c * 256 + 128], a[:, c * 256 + 128:(c + 1) * 256]
            pieces += [(c * 256, x1 * cos - x2 * sin), (c * 256 + 128, x2 * cos + x1 * sin)]
        return pieces

    def emit(dec):
        x = x_refs[(1 if dec else 0) if n_x == 2 else 0][...]
        if pending_moe:
            x = _moe_combine(x, y_ref, wcol_ref, prev_mod_ref)
            out_refs[-1][...] = x
        h = _modulated_norm(x, g_ref[...], mod_ref, 0).astype(BF16)
        for col0, width in segments:
            sinks = [(o_ref, o) for o_ref, o in zip(out_refs, outs)
                     if (o[0], o[1]) == (col0, width) and not (dec and o[4] != "all")]
            for blk in range(width // tn):
                acc = jnp.dot(h, w_ref[:, col0 + blk * tn:col0 + (blk + 1) * tn], preferred_element_type=F32)
                for o_ref, (_, _, kind, scale, rows) in sinks:
                    a = acc if scale == 1.0 else acc * scale
                    if rows == "ctx64":
                        n64 = width // 64
                        for c in range(tn // 64):
                            o_ref[pl.ds(blk * tn // 64 + c, tm, stride=n64), :] = (
                                a[:, c * 64:(c + 1) * 64].astype(o_ref.dtype))
                    elif kind == "silu":
                        o_ref[:, blk * tn:(blk + 1) * tn] = jax.nn.silu(a).astype(o_ref.dtype)
                    elif kind == "plain" or rows == "ctx" or not dec:
                        o_ref[:, blk * tn:(blk + 1) * tn] = a.astype(o_ref.dtype)
                    else:
                        for off, val in rotated(a, kind):
                            o_ref[:, blk * tn + off:blk * tn + off + 128] = val.astype(o_ref.dtype)

    pl.when(jnp.logical_not(is_dec))(lambda: emit(False))
    pl.when(is_dec)(lambda: emit(True))


def norm_mod_matmul(lay, x_parts, g, mods, w, outputs, rope_tables, pending_moe=None, tn=512):
    d = x_parts[0].shape[1]
    tm = lay.tm
    n_total = w.shape[1]
    outs, out_shapes, out_specs = [], [], []
    for col0, width, kind, scale, dtype, rows in outputs:
        assert width % tn == 0 and col0 % LANES == 0
        outs.append((col0, width, kind, float(scale), rows))
        if rows == "ctx64":
            n64 = width // 64
            out_shapes.append(jax.ShapeDtypeStruct((lay.t_ctx * n64, 64), dtype))
            out_specs.append(pl.BlockSpec((tm * n64, 64), lambda i: (lay.part_tile(0, i), 0)))
            continue
        n_rows = lay.t_ctx if rows == "ctx" else lay.t
        out_shapes.append(jax.ShapeDtypeStruct((n_rows, width), dtype))
        out_specs.append(lay.row_specs([None, None], width)[0] if rows == "ctx" else lay.row_specs([None], width)[0])
    blocks_per_seq = lay.n_dec // tm
    rope_spec = pl.BlockSpec((tm, LANES), lambda i: (lay.part_tile(1, i) % blocks_per_seq, 0))
    mod_spec = pl.BlockSpec((1, N_MOD, d), lambda i: (lay.mod_row(i), 0, 0))
    moe_specs, moe_args = [], []
    if pending_moe is not None:
        assert len(x_parts) == 1
        moe_specs = [pl.BlockSpec((TOP_K, tm, d // 2), lambda i: (0, i, 0)),
                     pl.BlockSpec((tm, LANES), lambda i: (i, 0)), mod_spec]
        moe_args = list(pending_moe)
        out_shapes.append(jax.ShapeDtypeStruct((lay.t, d), F32))
        out_specs.append(pl.BlockSpec((tm, d), lambda i: (i, 0)))
    return pl.pallas_call(
        functools.partial(_nmm_kernel, n_x=len(x_parts), pending_moe=pending_moe is not None, outs=tuple(outs),
                          tn=tn, n_ctx_tiles=lay.n_ctx_tiles),
        out_shape=out_shapes,
        grid=(lay.n_tiles,),
        in_specs=lay.row_specs(x_parts, d) + moe_specs + [
            pl.BlockSpec((1, d), lambda i: (0, 0)),
            mod_spec,
            pl.BlockSpec((d, n_total), lambda i: (0, 0), pipeline_mode=pl.Buffered(1)),
            rope_spec, rope_spec,
        ],
        out_specs=out_specs,
        compiler_params=_cparams("arbitrary"),
        name="norm_mod_matmul",
    )(*x_parts, *moe_args, g.reshape(1, d), mods, w, *rope_tables)


def _diff_lambda(lam_ref, lam_init):
    lp = lam_ref[...]
    return (jnp.exp(jnp.sum(lp[0:1] * lp[1:2], axis=-1, keepdims=True))
            - jnp.exp(jnp.sum(lp[2:3] * lp[3:4], axis=-1, keepdims=True)) + lam_init)


ONES_ROWS = 16


def _scores_t(k, q, comp):
    lane = lax.broadcasted_iota(I32, q.shape, 1)
    return _nt_dot(k, jnp.where((lane < DA_HEAD_DIM) == (comp == 0), q, jnp.zeros_like(q)))


def _softmax_values_t(scores, vts):
    dv = vts[0].shape[0] - ONES_ROWS
    maxes = [jnp.max(s, axis=0, keepdims=True) for s in scores]
    exps = [jnp.exp2(s - m).astype(BF16) for s, m in zip(scores, maxes)]
    accs = [jnp.dot(vt, e, preferred_element_type=F32) for vt, e in zip(vts, exps)]
    return [acc[0:dv] / acc[dv:dv + 1] for acc in accs]


def _diff_finish_t(parts, lam, g_col, lam_init):
    o = parts[0] - lam * parts[1]
    ms = jnp.mean(o * o, axis=0, keepdims=True)
    return jnp.transpose((o * lax.rsqrt(ms + NORM_EPS) * g_col) * (1.0 - lam_init))


def _diff_attn_kernel(lam_ref, q_ref, k_ref, v_ref, g_ref, o_ref, *, lam_init, heads_per_step):
    hw = 2 * DA_HEAD_DIM
    lam = _diff_lambda(lam_ref, lam_init)
    ones = jnp.ones((ONES_ROWS, k_ref.shape[0]), BF16)
    heads = [slice(hh * hw, (hh + 1) * hw) for hh in range(heads_per_step)]
    scores = [[_scores_t(k_ref[:, cols], q_ref[:, cols], comp) for comp in range(2)] for cols in heads]
    vts = [jnp.concatenate([jnp.transpose(v_ref[:, cols].astype(F32)).astype(BF16), ones], axis=0) for cols in heads]
    parts = _softmax_values_t([s for pair in scores for s in pair], [vt for vt in vts for _ in range(2)])
    for hh, cols in enumerate(heads):
        o_ref[:, cols] = _diff_finish_t(parts[2 * hh:2 * hh + 2], lam, g_ref[...], lam_init).astype(o_ref.dtype)


def _diff_attn_cached_kernel(lam_ref, q_ref, k_ref, v_ref, ck_ref, cv_ref, g_ref, o_ref, k_scr, vt_scr, s_scr, *,
                             lam_init, sub_rows):
    seq_len, hw = q_ref.shape
    n_sub = seq_len // sub_rows
    assert n_sub % 2 == 0
    lam = _diff_lambda(lam_ref, lam_init)
    k_scr[0:seq_len, :] = k_ref[...]
    k_scr[seq_len:, :] = ck_ref[0].astype(BF16)
    vt_scr[0:hw, 0:seq_len] = jnp.transpose(v_ref[...])
    vt_scr[0:hw, seq_len:] = jnp.transpose(cv_ref[0]).astype(BF16)
    vt_scr[hw:, :] = jnp.ones((ONES_ROWS, vt_scr.shape[1]), BF16)

    def rows_of(t):
        return pl.ds(pl.multiple_of(t * sub_rows, sub_rows), sub_rows)

    def scores(t, slot, comp):
        s_scr[slot, comp] = _scores_t(k_scr[...], q_ref[rows_of(t), :], comp)

    def stage(t_next, slot_next, t, slot):
        for comp in range(2):
            scores(t_next, slot_next, comp)
        parts = _softmax_values_t([s_scr[slot, 0], s_scr[slot, 1]], [vt_scr[...]] * 2)
        o_ref[rows_of(t), :] = _diff_finish_t(parts, lam, g_ref[...], lam_init).astype(o_ref.dtype)

    scores(0, 0, 0)
    scores(0, 0, 1)

    def body(i2, carry):
        t = 2 * i2
        stage(t + 1, 1, t, 0)
        stage(jnp.minimum(t + 2, n_sub - 1), 0, t + 1, 1)
        return carry

    lax.fori_loop(0, n_sub // 2, body, 0, unroll=2 if n_sub % 4 == 0 else 1)


def diff_attention(q, k, v, row0, batch, seq_len, lam_params, subln_g, lam_init, cache=None, heads_per_step=1,
                   sub_rows=256):
    width = q.shape[1]
    hw = 2 * DA_HEAD_DIM
    bw = heads_per_step * hw
    assert row0 % seq_len == 0 and DA_HEADS % heads_per_step == 0
    s0 = row0 // seq_len
    seq_spec = pl.BlockSpec((seq_len, bw), lambda b, h: (s0 + b, h))
    in_specs = [pl.BlockSpec((4, DA_HEAD_DIM), lambda b, h: (0, 0)), seq_spec, seq_spec, seq_spec]
    args = [lam_params, q, k, v]
    scratch = []
    if cache is None:
        body = functools.partial(_diff_attn_kernel, lam_init=lam_init, heads_per_step=heads_per_step)
    else:
        assert heads_per_step == 1
        past = cache[0].shape[1]
        cache_spec = pl.BlockSpec((1, past, hw), lambda b, h: (b, 0, h))
        in_specs += [cache_spec, cache_spec]
        args += list(cache)
        sub_rows = min(sub_rows, seq_len // 2)
        scratch = [pltpu.VMEM((seq_len + past, hw), BF16), pltpu.VMEM((hw + ONES_ROWS, seq_len + past), BF16),
                   pltpu.VMEM((2, 2, seq_len + past, sub_rows), F32)]
        body = functools.partial(_diff_attn_cached_kernel, lam_init=lam_init, sub_rows=sub_rows)
    in_specs.append(pl.BlockSpec((hw, 1), lambda b, h: (0, 0)))
    args.append(subln_g.reshape(hw, 1))
    return pl.pallas_call(
        body,
        out_shape=jax.ShapeDtypeStruct((batch * seq_len, width), BF16),
        grid=(batch, DA_HEADS // heads_per_step),
        in_specs=in_specs,
        out_specs=pl.BlockSpec((seq_len, bw), lambda b, h: (b, h)),
        scratch_shapes=scratch,
        compiler_params=_cparams("parallel", "parallel"),
        name="diff_attention",
    )(*args)


def _retention_kernel(*refs, seq_len, seqs, has_state, emit_state):
    lg_ref, q_ref, k_ref, v_ref, gf_ref, gb_ref = refs[:6]
    pos = 6
    if has_state:
        s0_refs = refs[6:8]
        pos = 8
    o_ref = refs[pos]
    pos += 1
    if emit_state:
        s_out_refs = refs[pos:pos + 2]
        pos += 2
    s_scr, o_scr = refs[pos:pos + 2]

    h = pl.program_id(1)
    c_len = RET_CHUNK
    n_chunks = seq_len // c_len
    assert n_chunks % 2 == 0
    row = lax.broadcasted_iota(I32, (c_len, c_len), 0)
    colm = lax.broadcasted_iota(I32, (c_len, c_len), 1)
    rel = (row - colm).astype(F32)
    posv = lax.broadcasted_iota(I32, (c_len, 1), 0).astype(F32)

    consts = []
    for backward in (False, True):
        lg = lg_ref[1 if backward else 0, h]
        if backward:
            intra = jnp.where(rel <= 0, jnp.exp(-rel * lg), 0.0)
            q_decay = jnp.exp((c_len - posv) * lg)
            k_decay = jnp.exp(posv * lg)
        else:
            intra = jnp.where(rel >= 0, jnp.exp(rel * lg), 0.0)
            q_decay = jnp.exp((posv + 1.0) * lg)
            k_decay = jnp.exp((c_len - 1.0 - posv) * lg)
        consts.append((intra, q_decay, k_decay, jnp.exp(jnp.zeros((1, 1), F32) + c_len * lg)))
        direction = 1 if backward else 0
        for sq in range(seqs):
            if has_state:
                s_scr[sq, direction] = s0_refs[direction][sq, 0, 0]
            else:
                s_scr[sq, direction] = jnp.zeros(s_scr.shape[2:], F32)

    def body(ci, first_touch):
        chains = [(sq, direction, pl.ds(pl.multiple_of(sq * seq_len + c * c_len, c_len), c_len))
                  for sq in range(seqs) for direction, c in ((0, ci), (1, n_chunks - 1 - ci))]
        qs = [q_ref[rows, :].astype(BF16) for _, _, rows in chains]
        ks = [k_ref[rows, :].astype(F32) for _, _, rows in chains]
        vs = [v_ref[rows, :].astype(BF16) for _, _, rows in chains]
        intras = [(_nt_dot(qb, kf.astype(BF16)) * consts[d][0]).astype(BF16)
                  for (_, d, _), qb, kf in zip(chains, qs, ks)]
        states = [s_scr[sq, d] for sq, d, _ in chains]
        outs = [jnp.dot(a, vb, preferred_element_type=F32)
                + jnp.dot(qb, s.astype(BF16), preferred_element_type=F32) * consts[d][1]
                for (_, d, _), a, qb, vb, s in zip(chains, intras, qs, vs, states)]
        for (sq, d, _), kf, vb, s in zip(chains, ks, vs, states):
            s_scr[sq, d] = consts[d][3] * s + _tn_dot((kf * consts[d][2]).astype(BF16), vb)
        centred = [o - jnp.mean(o, axis=-1, keepdims=True) for o in outs]
        scales = [lax.rsqrt(jnp.mean(oc * oc, axis=-1, keepdims=True) + NORM_EPS) for oc in centred]
        for (_, d, rows), oc, scale in zip(chains, centred, scales):
            gated = oc * scale * (gb_ref if d else gf_ref)[rows, :].astype(F32)
            if first_touch:
                o_scr[rows, :] = gated
            else:
                o_ref[rows, :] = (o_scr[rows, :] + gated).astype(o_ref.dtype)

    unroll = 4 if n_chunks % 8 == 0 else 1
    lax.fori_loop(0, n_chunks // 2, lambda ci, c: (body(ci, True), c)[1], 0, unroll=unroll)
    lax.fori_loop(n_chunks // 2, n_chunks, lambda ci, c: (body(ci, False), c)[1], 0, unroll=unroll)
    if emit_state:
        for d in range(2):
            for sq in range(seqs):
                s_out_refs[d][sq, 0, 0] = s_scr[sq, d]


def retention(q, k, v, gates, row0, batch, seq_len, log_decay, state=None, emit_state=False, seqs_per_step=1):
    dk = q.shape[1] // RET_HEADS
    dv = v.shape[1] // RET_HEADS
    seqs = seqs_per_step
    rows = seqs * seq_len
    assert row0 % rows == 0 and batch % seqs == 0
    s0 = row0 // rows
    in_specs = [
        pl.BlockSpec(memory_space=pltpu.SMEM),
        pl.BlockSpec((rows, dk), lambda b, h: (s0 + b, h)),
        pl.BlockSpec((rows, dk), lambda b, h: (s0 + b, h)),
        pl.BlockSpec((rows, dv), lambda b, h: (s0 + b, h)),
        pl.BlockSpec((rows, dv), lambda b, h: (s0 + b, h)),
        pl.BlockSpec((rows, dv), lambda b, h: (s0 + b, RET_HEADS + h)),
    ]
    args = [log_decay, q, k, v, gates, gates]
    state_spec = pl.BlockSpec((seqs, 1, 1, dk, dv), lambda b, h: (b, 0, h, 0, 0))
    if state is not None:
        in_specs += [state_spec, state_spec]
        args += list(state)
    out_shapes = [jax.ShapeDtypeStruct((batch * seq_len, RET_HEADS * dv), BF16)]
    out_specs = [pl.BlockSpec((rows, dv), lambda b, h: (b, h))]
    if emit_state:
        out_shapes += [jax.ShapeDtypeStruct((batch, 1, RET_HEADS, dk, dv), F32)] * 2
        out_specs += [state_spec, state_spec]
    return pl.pallas_call(
        functools.partial(_retention_kernel, seq_len=seq_len, seqs=seqs, has_state=state is not None,
                          emit_state=emit_state),
        out_shape=out_shapes,
        grid=(batch // seqs, RET_HEADS),
        in_specs=in_specs,
        out_specs=out_specs,
        scratch_shapes=[pltpu.VMEM((seqs, 2, dk, dv), F32), pltpu.VMEM((rows, dv), F32)],
        compiler_params=_cparams("parallel", "parallel"),
        name="retention",
    )(*args)


def _pack_bf16_halves(x):
    half = x.shape[1] // 2
    lo = lax.bitcast_convert_type(x[:, :half].astype(F32), jnp.uint32)
    hi = lax.bitcast_convert_type(x[:, half:].astype(F32), jnp.uint32)
    return (hi & jnp.uint32(0xFFFF0000)) | (lo >> 16)


def _unpack_bf16_halves(p):
    lo = lax.bitcast_convert_type(p << 16, F32).astype(BF16)
    hi = lax.bitcast_convert_type(p & jnp.uint32(0xFFFF0000), F32).astype(BF16)
    return lo, hi


def _first_max_onehot(vals):
    m = vals[0]
    for v in vals[1:]:
        m = jnp.maximum(m, v)
    onehot, taken = [], None
    for v in vals:
        hit = v == m
        if taken is None:
            onehot.append(hit)
            taken = hit
        else:
            onehot.append(hit & jnp.logical_not(taken))
            taken = taken | hit
    return m, onehot


def _pick(onehot, vals):
    out = vals[-1]
    for oh, v in zip(onehot[-2::-1], vals[-2::-1]):
        out = jnp.where(oh, v, out)
    return out


def _route_rows(x, g_ref, mod_ref, wr_ref, bias_ref, tri_ref, h_ref, idx_ref, wcol_ref, rank_ref, cnt_ref,
                ind_scr, wrow_scr):
    hb = _modulated_norm(x, g_ref[...], mod_ref, 3).astype(BF16)
    h_ref[...] = _pack_bf16_halves(hb)
    logits = _nt_dot(wr_ref[...], hb)
    s = jax.nn.sigmoid(logits)
    sel = s + bias_ref[...]
    neg_inf = jnp.full_like(sel[0:1], -jnp.inf)
    sel_rows = [sel[e:e + 1] for e in range(N_EXPERTS)]
    s_rows = [s[e:e + 1] for e in range(N_EXPERTS)]

    def top2(vals):
        m1, oh1 = _first_max_onehot(vals)
        rest = [jnp.where(o, neg_inf, v) for o, v in zip(oh1, vals)]
        m2, oh2 = _first_max_onehot(rest)
        return m1, m2, oh1, oh2

    grp_scores = []
    for g in range(N_GROUPS):
        m1, m2, _, _ = top2(sel_rows[g * EXPERTS_PER_GROUP:(g + 1) * EXPERTS_PER_GROUP])
        grp_scores.append(m1 + m2)
    _, in_grp = _first_max_onehot(grp_scores)
    cand_sel = [_pick(in_grp, [sel_rows[g * EXPERTS_PER_GROUP + k] for g in range(N_GROUPS)])
                for k in range(EXPERTS_PER_GROUP)]
    cand_s = [_pick(in_grp, [s_rows[g * EXPERTS_PER_GROUP + k] for g in range(N_GROUPS)])
              for k in range(EXPERTS_PER_GROUP)]
    _, _, oh1, oh2 = top2(cand_sel)
    w1 = _pick(oh1, cand_s)
    w2 = _pick(oh2, cand_s)
    denom = w1 + w2
    wrow_scr[...] = jnp.zeros_like(wrow_scr)
    wrow_scr[0:1, :] = w1 / denom
    wrow_scr[1:2, :] = w2 / denom
    wcol_ref[...] = jnp.transpose(wrow_scr[...])

    ints = [jnp.full(w1.shape, k, I32) for k in range(EXPERTS_PER_GROUP)]
    grp = _pick(in_grp, ints) * EXPERTS_PER_GROUP
    idx_ref[0:1, :] = grp + _pick(oh1, ints)
    idx_ref[1:2, :] = grp + _pick(oh2, ints)

    one, zero = jnp.ones_like(w1), jnp.zeros_like(w1)
    for g in range(N_GROUPS):
        for k in range(EXPERTS_PER_GROUP):
            e = g * EXPERTS_PER_GROUP + k
            ind_scr[e:e + 1, :] = jnp.where(in_grp[g] & (oh1[k] | oh2[k]), one, zero)
    ind = ind_scr[...].astype(BF16)
    ranks = jnp.dot(ind, tri_ref[...], preferred_element_type=F32)
    cnt_ref[0] = jnp.dot(ind, jnp.ones((ind.shape[1], LANES), BF16), preferred_element_type=F32)
    for slot, oh in ((0, oh1), (1, oh2)):
        r = zero
        for g in range(N_GROUPS):
            for k in range(EXPERTS_PER_GROUP):
                e = g * EXPERTS_PER_GROUP + k
                r = r + jnp.where(in_grp[g] & oh[k], ranks[e:e + 1], zero)
        rank_ref[slot:slot + 1, :] = r.astype(I32)


def _proj_route_kernel(*refs, n_x, n_a, n_ctx_tiles):
    x_refs, a_refs = refs[:n_x], refs[n_x:n_x + n_a]
    w_ref, mod_ref, g_ref, wr_ref, bias_ref, tri_ref, o_ref = refs[n_x + n_a:n_x + n_a + 7]
    route_out_refs = refs[n_x + n_a + 7:n_x + n_a + 12]
    ind_scr, wrow_scr = refs[n_x + n_a + 12:]
    is_dec = pl.program_id(0) >= n_ctx_tiles
    h_ref, idx_ref, wcol_ref, rank_ref, cnt_ref = route_out_refs
    sub = o_ref.shape[0] // ROUTE_SPLIT

    def body(dec):
        x_ref = x_refs[(1 if dec else 0) if n_x == 2 else 0]
        a_ref = a_refs[(1 if dec else 0) if n_a == 2 else 0]
        parts = [slice(part * sub, (part + 1) * sub) for part in range(ROUTE_SPLIT)]
        for rows in parts:
            y = jnp.dot(a_ref[rows, :].astype(BF16), w_ref[...], preferred_element_type=F32)
            o_ref[rows, :] = x_ref[rows, :] + mod_ref[0, 2:3, :] * y
        for part, rows in enumerate(parts):
            _route_rows(o_ref[rows, :], g_ref, mod_ref, wr_ref, bias_ref, tri_ref, h_ref.at[rows, :],
                        idx_ref.at[:, rows], wcol_ref.at[rows, :], rank_ref.at[:, rows], cnt_ref.at[part:part + 1],
                        ind_scr, wrow_scr)

    if n_x == 1 and n_a == 1:
        body(False)
    else:
        pl.when(jnp.logical_not(is_dec))(lambda: body(False))
        pl.when(is_dec)(lambda: body(True))


def proj_residual_route(lay, x_parts, a_parts, w, mods, ffn_g, w_router_t, router_bias):
    d = x_parts[0].shape[1]
    kdim = a_parts[0].shape[1]
    tm, t = lay.tm, lay.t
    sub = tm // ROUTE_SPLIT
    tri = jnp.triu(jnp.ones((sub, sub), BF16), k=1)
    pair = jax.ShapeDtypeStruct((TOP_K, t), I32)
    pair_spec = pl.BlockSpec((TOP_K, tm), lambda i: (0, i))
    const = lambda shape: pl.BlockSpec(shape, lambda i: (0,) * len(shape))
    return pl.pallas_call(
        functools.partial(_proj_route_kernel, n_x=len(x_parts), n_a=len(a_parts), n_ctx_tiles=lay.n_ctx_tiles),
        out_shape=[jax.ShapeDtypeStruct((t, d), F32), jax.ShapeDtypeStruct((t, d // 2), jnp.uint32), pair,
                   jax.ShapeDtypeStruct((t, LANES), F32), pair,
                   jax.ShapeDtypeStruct((lay.n_tiles * ROUTE_SPLIT, N_EXPERTS, LANES), F32)],
        grid=(lay.n_tiles,),
        in_specs=lay.row_specs(x_parts, d) + lay.row_specs(a_parts, kdim) + [
            const((kdim, d)),
            pl.BlockSpec((1, N_MOD, d), lambda i: (lay.mod_row(i), 0, 0)),
            const((1, d)), const((N_EXPERTS, d)), const((N_EXPERTS, 1)), const((sub, sub)),
        ],
        out_specs=[pl.BlockSpec((tm, d), lambda i: (i, 0)), pl.BlockSpec((tm, d // 2), lambda i: (i, 0)), pair_spec,
                   pl.BlockSpec((tm, LANES), lambda i: (i, 0)), pair_spec,
                   pl.BlockSpec((ROUTE_SPLIT, N_EXPERTS, LANES), lambda i: (i, 0, 0))],
        scratch_shapes=[pltpu.VMEM((N_EXPERTS, sub), F32), pltpu.VMEM((LANES, sub), F32)],
        compiler_params=_cparams("arbitrary"),
        name="proj_residual_route",
    )(*x_parts, *a_parts, w, mods, ffn_g.reshape(1, d), w_router_t, router_bias.reshape(N_EXPERTS, 1), tri)


def dispatch_plan(lay, idx, rank, cnt, n_sorted):
    n_tiles, tm = cnt.shape[0], lay.t // cnt.shape[0]
    cnt_tile = cnt[:, :, 0].astype(I32)
    total = jnp.sum(cnt_tile, axis=0)
    padded = ((total + EXPERT_TILE - 1) // EXPERT_TILE) * EXPERT_TILE
    end = jnp.cumsum(padded)
    start = end - padded
    base = start[None, :] + jnp.cumsum(cnt_tile, axis=0) - cnt_tile
    idx3 = idx.reshape(TOP_K, n_tiles, tm)
    pos = rank.reshape(TOP_K, n_tiles, tm)
    for e in range(N_EXPERTS):
        pos = pos + jnp.where(idx3 == e, base[None, :, e, None], 0)
    tile_row = jnp.arange(n_sorted // EXPERT_TILE, dtype=I32) * EXPERT_TILE
    tile_expert = jnp.minimum(jnp.sum(end[None, :] <= tile_row[:, None], axis=1), N_EXPERTS - 1).astype(I32)
    n_valid = (end[-1] // EXPERT_TILE).astype(I32).reshape(1)
    return pos.reshape(TOP_K * lay.t).astype(I32), tile_expert, n_valid


def _sc_mesh():
    return plsc.VectorSubcoreMesh(core_axis_name="c", subcore_axis_name="s")


def _sc_worker_id():
    return lax.axis_index("s") * V7X_SC_CORES + lax.axis_index("c")


def _sc_two_stage_pipeline(n_chunks, fetch_idx, load, store):
    assert n_chunks % 2 == 0
    fetch_idx(0, 0)
    load(0, 0).start()

    @pl.loop(0, n_chunks // 2)
    def _(p):
        j = 2 * p

        @pl.when(p > 0)
        def _():
            store(j - 1, 1).wait()

        fetch_idx(j + 1, 1)
        load(j + 1, 1).start()
        load(j, 0).wait()
        store(j, 0).start()
        store(j, 0).wait()

        @pl.when(j + 2 < n_chunks)
        def _():
            fetch_idx(j + 2, 0)
            load(j + 2, 0).start()

        load(j + 1, 1).wait()
        store(j + 1, 1).start()

    store(n_chunks - 1, 1).wait()


_SC_SCRATCH = lambda chunk, d, dtype: [
    pltpu.VMEM((chunk,), I32), pltpu.VMEM((chunk,), I32),
    pltpu.VMEM((chunk, d), dtype), pltpu.VMEM((chunk, d), dtype),
    pltpu.SemaphoreType.DMA, pltpu.SemaphoreType.DMA, pltpu.SemaphoreType.DMA, pltpu.SemaphoreType.DMA]


def sc_scatter_rows(src, pos, n_out_rows):
    t, d = src.shape
    n_idx = pos.shape[0]
    per_worker = n_idx // SC_WORKERS
    chunk = SC_CHUNK_ROWS
    assert n_idx % (SC_WORKERS * chunk * 2) == 0 and t % chunk == 0

    @functools.partial(
        pl.kernel, mesh=_sc_mesh(), out_type=jax.ShapeDtypeStruct((n_out_rows, d), src.dtype),
        scratch_types=_SC_SCRATCH(chunk, d, src.dtype), name="sc_scatter_rows")
    def scatter(src_hbm, pos_hbm, out_hbm, idx_a, idx_b, rows_a, rows_b, lsem_a, lsem_b, ssem_a, ssem_b):
        base = _sc_worker_id() * per_worker
        idx, rows, lsem, ssem = (idx_a, idx_b), (rows_a, rows_b), (lsem_a, lsem_b), (ssem_a, ssem_b)

        def fetch_idx(j, b):
            pltpu.sync_copy(pos_hbm.at[pl.ds(base + j * chunk, chunk)], idx[b])

        def load(j, b):
            return pltpu.make_async_copy(src_hbm.at[pl.ds(lax.rem(base + j * chunk, t), chunk)], rows[b], lsem[b])

        def store(j, b):
            return pltpu.make_async_copy(rows[b], out_hbm.at[idx[b]], ssem[b])

        _sc_two_stage_pipeline(per_worker // chunk, fetch_idx, load, store)

    return scatter(src, pos)


def sc_gather_rows(table, idx):
    _, d = table.shape
    n_idx = idx.shape[0]
    per_worker = n_idx // SC_WORKERS
    chunk = SC_CHUNK_ROWS
    assert n_idx % (SC_WORKERS * chunk * 2) == 0

    @functools.partial(
        pl.kernel, mesh=_sc_mesh(), out_type=jax.ShapeDtypeStruct((n_idx, d), table.dtype),
        scratch_types=_SC_SCRATCH(chunk, d, table.dtype), name="sc_gather_rows")
    def gather(table_hbm, idx_hbm, out_hbm, idx_a, idx_b, rows_a, rows_b, lsem_a, lsem_b, ssem_a, ssem_b):
        base = _sc_worker_id() * per_worker
        idx, rows, lsem, ssem = (idx_a, idx_b), (rows_a, rows_b), (lsem_a, lsem_b), (ssem_a, ssem_b)

        def fetch_idx(j, b):
            pltpu.sync_copy(idx_hbm.at[pl.ds(base + j * chunk, chunk)], idx[b])

        def load(j, b):
            return pltpu.make_async_copy(table_hbm.at[idx[b]], rows[b], lsem[b])

        def store(j, b):
            return pltpu.make_async_copy(rows[b], out_hbm.at[pl.ds(base + j * chunk, chunk)], ssem[b])

        _sc_two_stage_pipeline(per_worker // chunk, fetch_idx, load, store)

    return gather(table, idx)


def _experts_kernel(te_ref, nv_ref, x_ref, wg_ref, wu_ref, wd_ref, y_ref, wg_scr, wu_scr, wd_scr):
    i = pl.program_id(0)

    @pl.when(i < nv_ref[0])
    def _():
        @pl.when((i == 0) | (te_ref[i] != te_ref[jnp.maximum(i - 1, 0)]))
        def _():
            wg_scr[...] = wg_ref[0, 0].astype(BF16)
            wu_scr[...] = wu_ref[0, 0].astype(BF16)
            wd_scr[...] = wd_ref[0, 0].astype(BF16)

        x_lo, x_hi = _unpack_bf16_halves(x_ref[...])
        half = x_lo.shape[1]

        def in_proj(w_scr):
            return (jnp.dot(x_lo, w_scr[0:half, :], preferred_element_type=F32)
                    + jnp.dot(x_hi, w_scr[half:, :], preferred_element_type=F32))

        a = jax.nn.silu(in_proj(wg_scr)) * in_proj(wu_scr)
        y = jnp.dot(a.astype(BF16), wd_scr[...], preferred_element_type=F32)
        y_ref[...] = _pack_bf16_halves(y.astype(BF16))


def grouped_experts(xs, tile_expert, n_valid, wg, wu, wd, layer):
    n_rows = xs.shape[0]
    d, de = wg.shape[-2:]
    tm = EXPERT_TILE
    row_map = lambda i, te, nv: (jnp.minimum(i, nv[0] - 1), 0)
    grid_spec = pltpu.PrefetchScalarGridSpec(
        num_scalar_prefetch=2,
        grid=(n_rows // tm,),
        in_specs=[
            pl.BlockSpec((tm, d // 2), row_map),
            pl.BlockSpec((1, 1, d, de), lambda i, te, nv: (layer, te[i], 0, 0)),
            pl.BlockSpec((1, 1, d, de), lambda i, te, nv: (layer, te[i], 0, 0)),
            pl.BlockSpec((1, 1, de, d), lambda i, te, nv: (layer, te[i], 0, 0)),
        ],
        out_specs=pl.BlockSpec((tm, d // 2), row_map),
        scratch_shapes=[pltpu.VMEM((d, de), BF16), pltpu.VMEM((d, de), BF16), pltpu.VMEM((de, d), BF16)],
    )
    return pl.pallas_call(
        _experts_kernel,
        out_shape=jax.ShapeDtypeStruct((n_rows, d // 2), jnp.uint32),
        grid_spec=grid_spec,
        compiler_params=_cparams("arbitrary"),
        name="grouped_experts",
    )(tile_expert, n_valid, xs, wg, wu, wd)


def _combine_kernel(x_ref, y_ref, w_ref, mod_ref, fg_ref, *o_refs, final_norm, n_ctx_tiles):
    out = _moe_combine(x_ref[...], y_ref, w_ref, mod_ref)
    if final_norm:
        ms = jnp.mean(out * out, axis=-1, keepdims=True)
        out = out * lax.rsqrt(ms + NORM_EPS) * fg_ref[...]

    def store(o_ref):
        o_ref[...] = out
    _by_part(pl.program_id(0) >= n_ctx_tiles, o_refs, store)


def combine_residual(lay, x, y_pair, w_col, mods, final_g, final_norm, split):
    t, d = x.shape
    tm = lay.tm
    if split:
        out_shape = [jax.ShapeDtypeStruct((lay.t_ctx, d), F32), jax.ShapeDtypeStruct((lay.t_dec, d), F32)]
        out_specs = lay.row_specs([None, None], d)
    else:
        out_shape = [jax.ShapeDtypeStruct((t, d), F32)]
        out_specs = lay.row_specs([None], d)
    return pl.pallas_call(
        functools.partial(_combine_kernel, final_norm=final_norm, n_ctx_tiles=lay.n_ctx_tiles),
        out_shape=out_shape,
        grid=(lay.n_tiles,),
        in_specs=[
            pl.BlockSpec((tm, d), lambda i: (i, 0)),
            pl.BlockSpec((TOP_K, tm, d // 2), lambda i: (0, i, 0)),
            pl.BlockSpec((tm, LANES), lambda i: (i, 0)),
            pl.BlockSpec((1, N_MOD, d), lambda i: (lay.mod_row(i), 0, 0)),
            pl.BlockSpec((1, d), lambda i: (0, 0)),
        ],
        out_specs=out_specs,
        compiler_params=_cparams("arbitrary"),
        name="combine_residual",
    )(x, y_pair, w_col, mods, final_g.reshape(1, d))


def group_moe(lay, x, routing, p, layer):
    t, d = x.shape
    h, idx, w_col, rank, cnt = routing
    n_assign = TOP_K * t
    row_quantum = SC_WORKERS * SC_CHUNK_ROWS
    n_sorted = n_assign + N_EXPERTS * (EXPERT_TILE - 1)
    n_sorted = -(-n_sorted // row_quantum) * row_quantum
    n_sorted = -(-n_sorted // EXPERT_TILE) * EXPERT_TILE
    pos, tile_expert, n_valid = dispatch_plan(lay, idx, rank, cnt, n_sorted)
    xs = sc_scatter_rows(h, pos, n_sorted)
    ys = grouped_experts(xs, tile_expert, n_valid, p["moe_w_gate"], p["moe_w_up"], p["moe_w_down"], layer)
    return sc_gather_rows(ys, pos).reshape(TOP_K, t, d // 2), w_col


def _rope_angles(n, d):
    n_rows = n // GRID_W
    row = jnp.repeat(jnp.arange(n_rows), GRID_W).astype(F32)
    col = jnp.tile(jnp.arange(GRID_W), n_rows).astype(F32)
    nf = d // 4
    freqs = jnp.power(ROPE_BASE, -jnp.arange(nf, dtype=F32) / nf)
    ang = jnp.concatenate([row[:, None] * freqs, col[:, None] * freqs], axis=-1)
    return jnp.cos(ang), jnp.sin(ang)


def kernel(x_prompt, x_sample, cache_attn_k, cache_attn_v, state_ret_fwd, state_ret_bwd, c, c_ctx, w_ada, b_ada, norm_mix_g, norm_ffn_g, final_norm_g, da_w_qkv, da_lambda_q1, da_lambda_k1, da_lambda_q2, da_lambda_k2, da_subln_g, da_w_o, ret_w_qkv, ret_w_gate_fwd, ret_w_gate_bwd, ret_decay_fwd, ret_decay_bwd, ret_w_o, w_router, router_bias, moe_w_gate, moe_w_up, moe_w_down):
    b_ctx, n_ctx, d = x_prompt.shape
    b_dec, n_dec, _ = x_sample.shape
    past = cache_attn_k.shape[2]
    n_attn = cache_attn_k.shape[1]
    depth = w_ada.shape[0]
    assert b_dec + 1 <= MOD_ROWS
    lay = Layout(b_ctx, n_ctx, b_dec, n_dec)

    cond = jnp.zeros((MOD_ROWS, d), F32).at[0].set(c_ctx).at[1:1 + b_dec].set(c)
    mods_all = ada_modulation(cond, w_ada, b_ada)

    p = {
        "norm_ffn_g": norm_ffn_g, "final_norm_g": final_norm_g,
        "w_router_t": w_router.T.astype(BF16), "router_bias": router_bias.astype(F32),
        "moe_w_gate": moe_w_gate, "moe_w_up": moe_w_up, "moe_w_down": moe_w_down,
    }
    ret_log_decay = jnp.stack([jax.nn.log_sigmoid(ret_decay_fwd.astype(F32)),
                               jax.nn.log_sigmoid(ret_decay_bwd.astype(F32))], axis=1)
    ck_all = cache_attn_k.reshape(b_dec, n_attn, past, -1)
    cv_all = cache_attn_v.reshape(b_dec, n_attn, past, -1)

    x_parts = [x_prompt.reshape(lay.t_ctx, d), x_sample.reshape(lay.t_dec, d)]
    new_k, new_v, new_sf, new_sb = [], [], [], []
    pending_moe = None

    def first_kernel(*args):
        nonlocal x_parts, pending_moe
        outs = norm_mod_matmul(lay, x_parts, *args, pending_moe=pending_moe)
        if pending_moe is not None:
            *outs, x_joint = outs
            x_parts, pending_moe = [x_joint], None
        return outs

    for i in range(depth):
        mods = mods_all[i]
        j = i // 2
        if i % 2 == 0:
            lam_init = 0.8 - 0.6 * math.exp(-0.3 * i)
            qkw = DA_HEADS * 2 * DA_HEAD_DIM
            vw = DA_HEADS * DA_V_DIM
            cos, sin = _rope_angles(n_dec, DA_HEAD_DIM)
            rope = (jnp.tile(cos, (1, 4)), jnp.concatenate([-sin, sin, -sin, sin], axis=-1))
            q, k, v, k_ctx, v_ctx = first_kernel(
                norm_mix_g[i], mods, da_w_qkv[j].astype(BF16),
                [(0, qkw, "rope64", math.log2(math.e) * DA_HEAD_DIM ** -0.5, BF16, "all"),
                 (qkw, qkw, "rope64", 1.0, BF16, "all"),
                 (2 * qkw, vw, "plain", 1.0, BF16, "all"),
                 (qkw, qkw, "plain", 1.0, F32, "ctx64"), (2 * qkw, vw, "plain", 1.0, F32, "ctx")],
                rope)
            lam_params = jnp.stack([da_lambda_q1[j], da_lambda_k1[j], da_lambda_q2[j], da_lambda_k2[j]])
            mix = [diff_attention(q, k, v, 0, b_ctx, n_ctx, lam_params, da_subln_g[j], lam_init,
                                  heads_per_step=DA_HEADS),
                   diff_attention(q, k, v, lay.t_ctx, b_dec, n_dec, lam_params, da_subln_g[j], lam_init,
                                  cache=(ck_all[:, j], cv_all[:, j]))]
            mix_w = da_w_o[j].astype(BF16)
            new_k.append(k_ctx.reshape(b_ctx, n_ctx, DA_HEADS, 2, DA_HEAD_DIM))
            new_v.append(v_ctx.reshape(b_ctx, n_ctx, DA_HEADS, DA_V_DIM))
        else:
            kd = ret_w_qkv.shape[2] // 4
            dv = 2 * kd
            w_all = jnp.concatenate([ret_w_qkv[j], ret_w_gate_fwd[j], ret_w_gate_bwd[j]], axis=-1).astype(BF16)
            q, k, v, gates = first_kernel(
                norm_mix_g[i], mods, w_all,
                [(0, kd, "rope256", 1.0, BF16, "all"),
                 (kd, kd, "rope256", (kd // RET_HEADS) ** -0.5, F32, "all"),
                 (2 * kd, dv, "plain", 1.0, BF16, "all"), (2 * kd + dv, 2 * dv, "silu", 1.0, BF16, "all")],
                _rope_angles(n_dec, kd // RET_HEADS))
            ctx_seqs = math.gcd(b_ctx, max(1, n_dec // n_ctx // 2))
            o_ctx, sf, sb = retention(q, k, v, gates, 0, b_ctx, n_ctx, ret_log_decay[j], emit_state=True,
                                      seqs_per_step=ctx_seqs)
            (o_dec,) = retention(q, k, v, gates, lay.t_ctx, b_dec, n_dec, ret_log_decay[j],
                                 state=(state_ret_fwd[:, j:j + 1], state_ret_bwd[:, j:j + 1]))
            mix, mix_w = [o_ctx, o_dec], ret_w_o[j].astype(BF16)
            new_sf.append(sf)
            new_sb.append(sb)
        x, *routing = proj_residual_route(lay, x_parts, mix, mix_w, mods, norm_ffn_g[i], p["w_router_t"],
                                          p["router_bias"])
        y_pair, w_col = group_moe(lay, x, routing, p, i)
        if i == depth - 1:
            x_parts = combine_residual(lay, x, y_pair, w_col, mods, final_norm_g, final_norm=True, split=True)
        else:
            x_parts, pending_moe = [x], (y_pair, w_col, mods)

    y_ctx, y_dec = x_parts
    return (y_ctx.reshape(b_ctx, n_ctx, d), y_dec.reshape(b_dec, n_dec, d),
            jnp.stack(new_k, axis=1), jnp.stack(new_v, axis=1),
            jnp.concatenate(new_sf, axis=1), jnp.concatenate(new_sb, axis=1))
```

```python
import functools
import math

import jax
import jax.numpy as jnp
from jax import lax
from jax.experimental import pallas as pl
from jax.experimental.pallas import tpu as pltpu
from jax.experimental.pallas import tpu_sc as plsc

F32 = jnp.float32
BF16 = jnp.bfloat16
I32 = jnp.int32

GRID_W = 64
ROPE_BASE = 10000.0
NORM_EPS = 1e-6
DA_HEADS = 8
DA_HEAD_DIM = 64
DA_V_DIM = 2 * DA_HEAD_DIM
RET_HEADS = 4
RET_CHUNK = 128
N_EXPERTS = 16
N_GROUPS = 4
EXPERTS_PER_GROUP = N_EXPERTS // N_GROUPS
TOP_K = 2
N_MOD = 6
MOD_ROWS = 16
LANES = 128

V7X_VMEM_LIMIT = 56 * 1024 * 1024
V7X_SC_CORES = 2
V7X_SC_SUBCORES = 16
SC_WORKERS = V7X_SC_CORES * V7X_SC_SUBCORES
SC_CHUNK_ROWS = 64

ROW_TILE = 512
EXPERT_TILE = 512
ROUTE_SPLIT = 2


def _cparams(*sem):
    return pltpu.CompilerParams(dimension_semantics=sem, vmem_limit_bytes=V7X_VMEM_LIMIT)


def _nt_dot(a, b):
    return lax.dot_general(a, b, (((1,), (1,)), ((), ())), preferred_element_type=F32)


def _tn_dot(a, b):
    return lax.dot_general(a, b, (((0,), (0,)), ((), ())), preferred_element_type=F32)


class Layout:
    def __init__(self, b_ctx, n_ctx, b_dec, n_dec):
        self.b_ctx, self.n_ctx, self.b_dec, self.n_dec = b_ctx, n_ctx, b_dec, n_dec
        self.t_ctx, self.t_dec = b_ctx * n_ctx, b_dec * n_dec
        self.t = self.t_ctx + self.t_dec
        self.tm = min(ROW_TILE, n_dec, self.t_ctx)
        assert self.t_ctx % self.tm == 0 and n_dec % self.tm == 0
        assert self.t_ctx % n_dec == 0 and self.t_ctx % n_ctx == 0
        self.n_ctx_tiles = self.t_ctx // self.tm
        self.n_tiles = self.t // self.tm

    def mod_row(self, i):
        r = i * self.tm
        return jnp.where(r < self.t_ctx, 0, 1 + (r - self.t_ctx) // self.n_dec)

    def part_tile(self, part, i):
        if part == 0:
            return jnp.minimum(i, self.n_ctx_tiles - 1)
        return jnp.maximum(i - self.n_ctx_tiles, 0)

    def expert_out_specs(self, d):
        return [pl.BlockSpec((TOP_K, self.tm, d // 2), lambda i, p=p: (0, self.part_tile(p, i), 0)) for p in (0, 1)]

    def row_specs(self, arrays, width):
        if len(arrays) == 1:
            return [pl.BlockSpec((self.tm, width), lambda i, *_: (i, 0))]
        return [pl.BlockSpec((self.tm, width), lambda i, *_, p=p: (self.part_tile(p, i), 0)) for p in (0, 1)]


def _ada_kernel(c_ref, w_ref, b_ref, o_ref):
    s = jax.nn.silu(c_ref[...]).astype(BF16)
    acc = jnp.dot(s, w_ref[0].astype(BF16), preferred_element_type=F32)
    o_ref[0] = acc + b_ref[0]


def ada_modulation(cond, w_ada, b_ada):
    depth, d, n = w_ada.shape
    tn = 1536
    out = pl.pallas_call(
        _ada_kernel,
        out_shape=jax.ShapeDtypeStruct((depth, MOD_ROWS, n), F32),
        grid=(depth, n // tn),
        in_specs=[
            pl.BlockSpec((MOD_ROWS, d), lambda l, j: (0, 0)),
            pl.BlockSpec((1, d, tn), lambda l, j: (l, 0, j)),
            pl.BlockSpec((1, 1, tn), lambda l, j: (l, 0, j)),
        ],
        out_specs=pl.BlockSpec((1, MOD_ROWS, tn), lambda l, j: (l, 0, j)),
        compiler_params=_cparams("parallel", "parallel"),
        name="ada_modulation",
    )(cond, w_ada, b_ada.reshape(depth, 1, n))
    return out.reshape(depth, MOD_ROWS, N_MOD, d)


def _modulated_norm(x, g, mod_ref, shift_idx):
    ms = jnp.mean(x * x, axis=-1, keepdims=True)
    y = x * lax.rsqrt(ms + NORM_EPS) * g
    return y * (1.0 + mod_ref[0, shift_idx + 1:shift_idx + 2, :]) + mod_ref[0, shift_idx:shift_idx + 1, :]


def _by_part(is_dec, refs, fn):
    if len(refs) == 1:
        fn(refs[0])
        return
    pl.when(jnp.logical_not(is_dec))(lambda: fn(refs[0]))
    pl.when(is_dec)(lambda: fn(refs[1]))


def _rope64(a, cos, sin_signed, first_half):
    partner = jnp.where(first_half, pltpu.roll(a, 96, 1), pltpu.roll(a, 32, 1))
    return a * cos + partner * sin_signed


def _moe_combine(x, y_ref, w_ref, mod_ref):
    w = w_ref[...]

    def expert_out(slot):
        lo, hi = _unpack_bf16_halves(y_ref[slot])
        return jnp.concatenate([lo.astype(F32), hi.astype(F32)], axis=1)

    return x + mod_ref[0, 5:6, :] * (w[:, 0:1] * expert_out(0) + w[:, 1:2] * expert_out(1))


def _nmm_kernel(*refs, n_x, pending_moe, outs, tn, n_ctx_tiles):
    x_refs = refs[:n_x]
    if pending_moe:
        y_refs, (wcol_ref, prev_mod_ref) = refs[n_x:n_x + 2], refs[n_x + 2:n_x + 4]
        refs = refs[:n_x] + refs[n_x + 4:]
    g_ref, mod_ref, w_ref, cos_ref, sin_ref = refs[n_x:n_x + 5]
    out_refs = refs[n_x + 5:]
    is_dec = pl.program_id(0) >= n_ctx_tiles
    tm = x_refs[0].shape[0]
    segments = sorted({(col0, width) for col0, width, _, _, _ in outs})

    def rotated(a, kind):
        cos, sin = cos_ref[...], sin_ref[...]
        if kind == "rope64":
            lane = lax.broadcasted_iota(I32, (tm, LANES), 1)
            first_half = (lane & 32) == 0
            return [(c * 128, _rope64(a[:, c * 128:(c + 1) * 128], cos, sin, first_half))
                    for c in range(tn // 128)]
        pieces = []
        for c in range(tn // 256):
            x1, x2 = a[:, c * 256:c * 256 + 128], a[:, c * 256 + 128:(c + 1) * 256]
            pieces += [(c * 256, x1 * cos - x2 * sin), (c * 256 + 128, x2 * cos + x1 * sin)]
        return pieces

    def emit(dec):
        x = x_refs[(1 if dec else 0) if n_x == 2 else 0][...]
        if pending_moe:
            x = _moe_combine(x, y_refs[1 if dec else 0], wcol_ref, prev_mod_ref)
            out_refs[-1][...] = x
        h = _modulated_norm(x, g_ref[...], mod_ref, 0).astype(BF16)
        for col0, width in segments:
            sinks = [(o_ref, o) for o_ref, o in zip(out_refs, outs)
                     if (o[0], o[1]) == (col0, width) and not (dec and o[4] != "all")]
            for blk in range(width // tn):
                acc = jnp.dot(h, w_ref[:, col0 + blk * tn:col0 + (blk + 1) * tn], preferred_element_type=F32)
                for o_ref, (_, _, kind, scale, rows) in sinks:
                    a = acc if scale == 1.0 else acc * scale
                    if rows == "ctx64":
                        n64 = width // 64
                        for c in range(tn // 64):
                            o_ref[pl.ds(blk * tn // 64 + c, tm, stride=n64), :] = (
                                a[:, c * 64:(c + 1) * 64].astype(o_ref.dtype))
                    elif kind == "silu":
                        o_ref[:, blk * tn:(blk + 1) * tn] = jax.nn.silu(a).astype(o_ref.dtype)
                    elif kind == "plain" or rows == "ctx" or not dec:
                        o_ref[:, blk * tn:(blk + 1) * tn] = a.astype(o_ref.dtype)
                    else:
                        for off, val in rotated(a, kind):
                            o_ref[:, blk * tn + off:blk * tn + off + 128] = val.astype(o_ref.dtype)

    pl.when(jnp.logical_not(is_dec))(lambda: emit(False))
    pl.when(is_dec)(lambda: emit(True))


def norm_mod_matmul(lay, x_parts, g, mods, w, outputs, rope_tables, pending_moe=None, tn=512):
    d = x_parts[0].shape[1]
    tm = lay.tm
    n_total = w.shape[1]
    outs, out_shapes, out_specs = [], [], []
    for col0, width, kind, scale, dtype, rows in outputs:
        assert width % tn == 0 and col0 % LANES == 0
        outs.append((col0, width, kind, float(scale), rows))
        if rows == "ctx64":
            n64 = width // 64
            out_shapes.append(jax.ShapeDtypeStruct((lay.t_ctx * n64, 64), dtype))
            out_specs.append(pl.BlockSpec((tm * n64, 64), lambda i: (lay.part_tile(0, i), 0)))
            continue
        n_rows = lay.t_ctx if rows == "ctx" else lay.t
        out_shapes.append(jax.ShapeDtypeStruct((n_rows, width), dtype))
        out_specs.append(lay.row_specs([None, None], width)[0] if rows == "ctx" else lay.row_specs([None], width)[0])
    blocks_per_seq = lay.n_dec // tm
    rope_spec = pl.BlockSpec((tm, LANES), lambda i: (lay.part_tile(1, i) % blocks_per_seq, 0))
    mod_spec = pl.BlockSpec((1, N_MOD, d), lambda i: (lay.mod_row(i), 0, 0))
    moe_specs, moe_args = [], []
    if pending_moe is not None:
        assert len(x_parts) == 1
        y_parts, w_col, prev_mods = pending_moe
        moe_specs = lay.expert_out_specs(d) + [pl.BlockSpec((tm, LANES), lambda i: (i, 0)), mod_spec]
        moe_args = [*y_parts, w_col, prev_mods]
        out_shapes.append(jax.ShapeDtypeStruct((lay.t, d), F32))
        out_specs.append(pl.BlockSpec((tm, d), lambda i: (i, 0)))
    return pl.pallas_call(
        functools.partial(_nmm_kernel, n_x=len(x_parts), pending_moe=pending_moe is not None, outs=tuple(outs),
                          tn=tn, n_ctx_tiles=lay.n_ctx_tiles),
        out_shape=out_shapes,
        grid=(lay.n_tiles,),
        in_specs=lay.row_specs(x_parts, d) + moe_specs + [
            pl.BlockSpec((1, d), lambda i: (0, 0)),
            mod_spec,
            pl.BlockSpec((d, n_total), lambda i: (0, 0), pipeline_mode=pl.Buffered(1)),
            rope_spec, rope_spec,
        ],
        out_specs=out_specs,
        compiler_params=_cparams("arbitrary"),
        name="norm_mod_matmul",
    )(*x_parts, *moe_args, g.reshape(1, d), mods, w, *rope_tables)


def _diff_lambda(lam_ref, lam_init):
    lp = lam_ref[...]
    return (jnp.exp(jnp.sum(lp[0:1] * lp[1:2], axis=-1, keepdims=True))
            - jnp.exp(jnp.sum(lp[2:3] * lp[3:4], axis=-1, keepdims=True)) + lam_init)


ONES_ROWS = 16


def _scores_t(k, q, comp):
    lane = lax.broadcasted_iota(I32, q.shape, 1)
    return _nt_dot(k, jnp.where((lane < DA_HEAD_DIM) == (comp == 0), q, jnp.zeros_like(q)))


def _softmax_values_t(scores, vts):
    dv = vts[0].shape[0] - ONES_ROWS
    maxes = [jnp.max(s, axis=0, keepdims=True) for s in scores]
    exps = [jnp.exp2(s - m).astype(BF16) for s, m in zip(scores, maxes)]
    accs = [jnp.dot(vt, e, preferred_element_type=F32) for vt, e in zip(vts, exps)]
    return [acc[0:dv] / acc[dv:dv + 1] for acc in accs]


def _diff_finish_t(parts, lam, g_col, lam_init):
    o = parts[0] - lam * parts[1]
    ms = jnp.mean(o * o, axis=0, keepdims=True)
    return jnp.transpose((o * lax.rsqrt(ms + NORM_EPS) * g_col) * (1.0 - lam_init))


def _diff_attn_kernel(lam_ref, q_ref, k_ref, v_ref, g_ref, o_ref, *, lam_init, heads_per_step):
    hw = 2 * DA_HEAD_DIM
    lam = _diff_lambda(lam_ref, lam_init)
    ones = jnp.ones((ONES_ROWS, k_ref.shape[0]), BF16)
    heads = [slice(hh * hw, (hh + 1) * hw) for hh in range(heads_per_step)]
    scores = [[_scores_t(k_ref[:, cols], q_ref[:, cols], comp) for comp in range(2)] for cols in heads]
    vts = [jnp.concatenate([jnp.transpose(v_ref[:, cols].astype(F32)).astype(BF16), ones], axis=0) for cols in heads]
    parts = _softmax_values_t([s for pair in scores for s in pair], [vt for vt in vts for _ in range(2)])
    for hh, cols in enumerate(heads):
        o_ref[:, cols] = _diff_finish_t(parts[2 * hh:2 * hh + 2], lam, g_ref[...], lam_init).astype(o_ref.dtype)


def _diff_attn_cached_kernel(lam_ref, q_ref, k_ref, v_ref, ck_ref, cv_ref, g_ref, o_ref, k_scr, vt_scr, s_scr, *,
                             lam_init, sub_rows):
    seq_len, hw = q_ref.shape
    n_sub = seq_len // sub_rows
    assert n_sub % 2 == 0
    lam = _diff_lambda(lam_ref, lam_init)
    k_scr[0:seq_len, :] = k_ref[...]
    k_scr[seq_len:, :] = ck_ref[0].astype(BF16)
    vt_scr[0:hw, 0:seq_len] = jnp.transpose(v_ref[...])
    vt_scr[0:hw, seq_len:] = jnp.transpose(cv_ref[0]).astype(BF16)
    vt_scr[hw:, :] = jnp.ones((ONES_ROWS, vt_scr.shape[1]), BF16)

    def rows_of(t):
        return pl.ds(pl.multiple_of(t * sub_rows, sub_rows), sub_rows)

    def scores(t, slot, comp):
        s_scr[slot, comp] = _scores_t(k_scr[...], q_ref[rows_of(t), :], comp)

    def stage(t_next, slot_next, t, slot):
        for comp in range(2):
            scores(t_next, slot_next, comp)
        parts = _softmax_values_t([s_scr[slot, 0], s_scr[slot, 1]], [vt_scr[...]] * 2)
        o_ref[rows_of(t), :] = _diff_finish_t(parts, lam, g_ref[...], lam_init).astype(o_ref.dtype)

    scores(0, 0, 0)
    scores(0, 0, 1)

    def body(i2, carry):
        t = 2 * i2
        stage(t + 1, 1, t, 0)
        stage(jnp.minimum(t + 2, n_sub - 1), 0, t + 1, 1)
        return carry

    lax.fori_loop(0, n_sub // 2, body, 0, unroll=2 if n_sub % 4 == 0 else 1)


def diff_attention(q, k, v, row0, batch, seq_len, lam_params, subln_g, lam_init, cache=None, heads_per_step=1,
                   sub_rows=256):
    width = q.shape[1]
    hw = 2 * DA_HEAD_DIM
    bw = heads_per_step * hw
    assert row0 % seq_len == 0 and DA_HEADS % heads_per_step == 0
    s0 = row0 // seq_len
    seq_spec = pl.BlockSpec((seq_len, bw), lambda b, h: (s0 + b, h))
    in_specs = [pl.BlockSpec((4, DA_HEAD_DIM), lambda b, h: (0, 0)), seq_spec, seq_spec, seq_spec]
    args = [lam_params, q, k, v]
    scratch = []
    if cache is None:
        body = functools.partial(_diff_attn_kernel, lam_init=lam_init, heads_per_step=heads_per_step)
    else:
        assert heads_per_step == 1
        past = cache[0].shape[1]
        cache_spec = pl.BlockSpec((1, past, hw), lambda b, h: (b, 0, h))
        in_specs += [cache_spec, cache_spec]
        args += list(cache)
        sub_rows = min(sub_rows, seq_len // 2)
        scratch = [pltpu.VMEM((seq_len + past, hw), BF16), pltpu.VMEM((hw + ONES_ROWS, seq_len + past), BF16),
                   pltpu.VMEM((2, 2, seq_len + past, sub_rows), F32)]
        body = functools.partial(_diff_attn_cached_kernel, lam_init=lam_init, sub_rows=sub_rows)
    in_specs.append(pl.BlockSpec((hw, 1), lambda b, h: (0, 0)))
    args.append(subln_g.reshape(hw, 1))
    return pl.pallas_call(
        body,
        out_shape=jax.ShapeDtypeStruct((batch * seq_len, width), BF16),
        grid=(batch, DA_HEADS // heads_per_step),
        in_specs=in_specs,
        out_specs=pl.BlockSpec((seq_len, bw), lambda b, h: (b, h)),
        scratch_shapes=scratch,
        compiler_params=_cparams("parallel", "parallel"),
        name="diff_attention",
    )(*args)


def _retention_kernel(*refs, seq_len, seqs, has_state, emit_state):
    lg_ref, q_ref, k_ref, v_ref, gf_ref, gb_ref = refs[:6]
    pos = 6
    if has_state:
        s0_refs = refs[6:8]
        pos = 8
    o_ref = refs[pos]
    pos += 1
    if emit_state:
        s_out_refs = refs[pos:pos + 2]
        pos += 2
    s_scr, o_scr = refs[pos:pos + 2]

    h = pl.program_id(1)
    c_len = RET_CHUNK
    n_chunks = seq_len // c_len
    assert n_chunks % 2 == 0
    row = lax.broadcasted_iota(I32, (c_len, c_len), 0)
    colm = lax.broadcasted_iota(I32, (c_len, c_len), 1)
    rel = (row - colm).astype(F32)
    posv = lax.broadcasted_iota(I32, (c_len, 1), 0).astype(F32)

    consts = []
    for backward in (False, True):
        lg = lg_ref[1 if backward else 0, h]
        if backward:
            intra = jnp.where(rel <= 0, jnp.exp(-rel * lg), 0.0)
            q_decay = jnp.exp((c_len - posv) * lg)
            k_decay = jnp.exp(posv * lg)
        else:
            intra = jnp.where(rel >= 0, jnp.exp(rel * lg), 0.0)
            q_decay = jnp.exp((posv + 1.0) * lg)
            k_decay = jnp.exp((c_len - 1.0 - posv) * lg)
        consts.append((intra, q_decay, k_decay, jnp.exp(jnp.zeros((1, 1), F32) + c_len * lg)))
        direction = 1 if backward else 0
        for sq in range(seqs):
            if has_state:
                s_scr[sq, direction] = s0_refs[direction][sq, 0, 0]
            else:
                s_scr[sq, direction] = jnp.zeros(s_scr.shape[2:], F32)

    def body(ci, first_touch):
        chains = [(sq, direction, pl.ds(pl.multiple_of(sq * seq_len + c * c_len, c_len), c_len))
                  for sq in range(seqs) for direction, c in ((0, ci), (1, n_chunks - 1 - ci))]
        qs = [q_ref[rows, :].astype(BF16) for _, _, rows in chains]
        ks = [k_ref[rows, :].astype(F32) for _, _, rows in chains]
        vs = [v_ref[rows, :].astype(BF16) for _, _, rows in chains]
        intras = [(_nt_dot(qb, kf.astype(BF16)) * consts[d][0]).astype(BF16)
                  for (_, d, _), qb, kf in zip(chains, qs, ks)]
        states = [s_scr[sq, d] for sq, d, _ in chains]
        outs = [jnp.dot(a, vb, preferred_element_type=F32)
                + jnp.dot(qb, s.astype(BF16), preferred_element_type=F32) * consts[d][1]
                for (_, d, _), a, qb, vb, s in zip(chains, intras, qs, vs, states)]
        for (sq, d, _), kf, vb, s in zip(chains, ks, vs, states):
            s_scr[sq, d] = consts[d][3] * s + _tn_dot((kf * consts[d][2]).astype(BF16), vb)
        centred = [o - jnp.mean(o, axis=-1, keepdims=True) for o in outs]
        scales = [lax.rsqrt(jnp.mean(oc * oc, axis=-1, keepdims=True) + NORM_EPS) for oc in centred]
        for (_, d, rows), oc, scale in zip(chains, centred, scales):
            gated = oc * scale * (gb_ref if d else gf_ref)[rows, :].astype(F32)
            if first_touch:
                o_scr[rows, :] = gated
            else:
                o_ref[rows, :] = (o_scr[rows, :] + gated).astype(o_ref.dtype)

    unroll = 8 if n_chunks % 16 == 0 else 1
    lax.fori_loop(0, n_chunks // 2, lambda ci, c: (body(ci, True), c)[1], 0, unroll=unroll)
    lax.fori_loop(n_chunks // 2, n_chunks, lambda ci, c: (body(ci, False), c)[1], 0, unroll=unroll)
    if emit_state:
        for d in range(2):
            for sq in range(seqs):
                s_out_refs[d][sq, 0, 0] = s_scr[sq, d]


def retention(q, k, v, gates, row0, batch, seq_len, log_decay, state=None, emit_state=False, seqs_per_step=1):
    dk = q.shape[1] // RET_HEADS
    dv = v.shape[1] // RET_HEADS
    seqs = seqs_per_step
    rows = seqs * seq_len
    assert row0 % rows == 0 and batch % seqs == 0
    s0 = row0 // rows
    in_specs = [
        pl.BlockSpec(memory_space=pltpu.SMEM),
        pl.BlockSpec((rows, dk), lambda b, h: (s0 + b, h)),
        pl.BlockSpec((rows, dk), lambda b, h: (s0 + b, h)),
        pl.BlockSpec((rows, dv), lambda b, h: (s0 + b, h)),
        pl.BlockSpec((rows, dv), lambda b, h: (s0 + b, h)),
        pl.BlockSpec((rows, dv), lambda b, h: (s0 + b, RET_HEADS + h)),
    ]
    args = [log_decay, q, k, v, gates, gates]
    state_spec = pl.BlockSpec((seqs, 1, 1, dk, dv), lambda b, h: (b, 0, h, 0, 0))
    if state is not None:
        in_specs += [state_spec, state_spec]
        args += list(state)
    out_shapes = [jax.ShapeDtypeStruct((batch * seq_len, RET_HEADS * dv), BF16)]
    out_specs = [pl.BlockSpec((rows, dv), lambda b, h: (b, h))]
    if emit_state:
        out_shapes += [jax.ShapeDtypeStruct((batch, 1, RET_HEADS, dk, dv), F32)] * 2
        out_specs += [state_spec, state_spec]
    return pl.pallas_call(
        functools.partial(_retention_kernel, seq_len=seq_len, seqs=seqs, has_state=state is not None,
                          emit_state=emit_state),
        out_shape=out_shapes,
        grid=(batch // seqs, RET_HEADS),
        in_specs=in_specs,
        out_specs=out_specs,
        scratch_shapes=[pltpu.VMEM((seqs, 2, dk, dv), F32), pltpu.VMEM((rows, dv), F32)],
        compiler_params=_cparams("parallel", "parallel"),
        name="retention",
    )(*args)


def _pack_bf16_halves(x):
    half = x.shape[1] // 2
    lo = lax.bitcast_convert_type(x[:, :half].astype(F32), jnp.uint32)
    hi = lax.bitcast_convert_type(x[:, half:].astype(F32), jnp.uint32)
    return (hi & jnp.uint32(0xFFFF0000)) | (lo >> 16)


def _unpack_bf16_halves(p):
    lo = lax.bitcast_convert_type(p << 16, F32).astype(BF16)
    hi = lax.bitcast_convert_type(p & jnp.uint32(0xFFFF0000), F32).astype(BF16)
    return lo, hi


def _first_max_onehot(vals):
    m = vals[0]
    for v in vals[1:]:
        m = jnp.maximum(m, v)
    onehot, taken = [], None
    for v in vals:
        hit = v == m
        if taken is None:
            onehot.append(hit)
            taken = hit
        else:
            onehot.append(hit & jnp.logical_not(taken))
            taken = taken | hit
    return m, onehot


def _pick(onehot, vals):
    out = vals[-1]
    for oh, v in zip(onehot[-2::-1], vals[-2::-1]):
        out = jnp.where(oh, v, out)
    return out


def _route_rows(x, g_ref, mod_ref, wr_ref, bias_ref, tri_ref, h_ref, idx_ref, wcol_ref, rank_ref, cnt_ref,
                ind_scr, wrow_scr):
    hb = _modulated_norm(x, g_ref[...], mod_ref, 3).astype(BF16)
    h_ref[...] = _pack_bf16_halves(hb)
    logits = _nt_dot(wr_ref[...], hb)
    s = jax.nn.sigmoid(logits)
    sel = s + bias_ref[...]
    neg_inf = jnp.full_like(sel[0:1], -jnp.inf)
    sel_rows = [sel[e:e + 1] for e in range(N_EXPERTS)]
    s_rows = [s[e:e + 1] for e in range(N_EXPERTS)]

    def top2(vals):
        m1, oh1 = _first_max_onehot(vals)
        rest = [jnp.where(o, neg_inf, v) for o, v in zip(oh1, vals)]
        m2, oh2 = _first_max_onehot(rest)
        return m1, m2, oh1, oh2

    grp_scores = []
    for g in range(N_GROUPS):
        m1, m2, _, _ = top2(sel_rows[g * EXPERTS_PER_GROUP:(g + 1) * EXPERTS_PER_GROUP])
        grp_scores.append(m1 + m2)
    _, in_grp = _first_max_onehot(grp_scores)
    cand_sel = [_pick(in_grp, [sel_rows[g * EXPERTS_PER_GROUP + k] for g in range(N_GROUPS)])
                for k in range(EXPERTS_PER_GROUP)]
    cand_s = [_pick(in_grp, [s_rows[g * EXPERTS_PER_GROUP + k] for g in range(N_GROUPS)])
              for k in range(EXPERTS_PER_GROUP)]
    _, _, oh1, oh2 = top2(cand_sel)
    w1 = _pick(oh1, cand_s)
    w2 = _pick(oh2, cand_s)
    denom = w1 + w2
    wrow_scr[...] = jnp.zeros_like(wrow_scr)
    wrow_scr[0:1, :] = w1 / denom
    wrow_scr[1:2, :] = w2 / denom
    wcol_ref[...] = jnp.transpose(wrow_scr[...])

    ints = [jnp.full(w1.shape, k, I32) for k in range(EXPERTS_PER_GROUP)]
    grp = _pick(in_grp, ints) * EXPERTS_PER_GROUP
    idx_ref[0:1, :] = grp + _pick(oh1, ints)
    idx_ref[1:2, :] = grp + _pick(oh2, ints)

    one, zero = jnp.ones_like(w1), jnp.zeros_like(w1)
    for g in range(N_GROUPS):
        for k in range(EXPERTS_PER_GROUP):
            e = g * EXPERTS_PER_GROUP + k
            ind_scr[e:e + 1, :] = jnp.where(in_grp[g] & (oh1[k] | oh2[k]), one, zero)
    ind = ind_scr[...].astype(BF16)
    ranks = jnp.dot(ind, tri_ref[...], preferred_element_type=F32)
    cnt_ref[0] = jnp.dot(ind, jnp.ones((ind.shape[1], LANES), BF16), preferred_element_type=F32)
    for slot, oh in ((0, oh1), (1, oh2)):
        r = zero
        for g in range(N_GROUPS):
            for k in range(EXPERTS_PER_GROUP):
                e = g * EXPERTS_PER_GROUP + k
                r = r + jnp.where(in_grp[g] & oh[k], ranks[e:e + 1], zero)
        rank_ref[slot:slot + 1, :] = r.astype(I32)


def _proj_route_kernel(*refs, n_x, n_a, n_ctx_tiles):
    x_refs, a_refs = refs[:n_x], refs[n_x:n_x + n_a]
    w_ref, mod_ref, g_ref, wr_ref, bias_ref, tri_ref, o_ref = refs[n_x + n_a:n_x + n_a + 7]
    route_out_refs = refs[n_x + n_a + 7:n_x + n_a + 12]
    ind_scr, wrow_scr = refs[n_x + n_a + 12:]
    is_dec = pl.program_id(0) >= n_ctx_tiles
    h_ref, idx_ref, wcol_ref, rank_ref, cnt_ref = route_out_refs
    sub = o_ref.shape[0] // ROUTE_SPLIT

    def body(dec):
        x_ref = x_refs[(1 if dec else 0) if n_x == 2 else 0]
        a_ref = a_refs[(1 if dec else 0) if n_a == 2 else 0]
        parts = [slice(part * sub, (part + 1) * sub) for part in range(ROUTE_SPLIT)]
        for rows in parts:
            y = jnp.dot(a_ref[rows, :].astype(BF16), w_ref[...], preferred_element_type=F32)
            o_ref[rows, :] = x_ref[rows, :] + mod_ref[0, 2:3, :] * y
        for part, rows in enumerate(parts):
            _route_rows(o_ref[rows, :], g_ref, mod_ref, wr_ref, bias_ref, tri_ref, h_ref.at[rows, :],
                        idx_ref.at[:, rows], wcol_ref.at[rows, :], rank_ref.at[:, rows], cnt_ref.at[part:part + 1],
                        ind_scr, wrow_scr)

    if n_x == 1 and n_a == 1:
        body(False)
    else:
        pl.when(jnp.logical_not(is_dec))(lambda: body(False))
        pl.when(is_dec)(lambda: body(True))


def proj_residual_route(lay, x_parts, a_parts, w, mods, ffn_g, w_router_t, router_bias):
    d = x_parts[0].shape[1]
    kdim = a_parts[0].shape[1]
    tm, t = lay.tm, lay.t
    sub = tm // ROUTE_SPLIT
    tri = jnp.triu(jnp.ones((sub, sub), BF16), k=1)
    pair = jax.ShapeDtypeStruct((TOP_K, t), I32)
    pair_spec = pl.BlockSpec((TOP_K, tm), lambda i: (0, i))
    const = lambda shape: pl.BlockSpec(shape, lambda i: (0,) * len(shape))
    return pl.pallas_call(
        functools.partial(_proj_route_kernel, n_x=len(x_parts), n_a=len(a_parts), n_ctx_tiles=lay.n_ctx_tiles),
        out_shape=[jax.ShapeDtypeStruct((t, d), F32), jax.ShapeDtypeStruct((t, d // 2), jnp.uint32), pair,
                   jax.ShapeDtypeStruct((t, LANES), F32), pair,
                   jax.ShapeDtypeStruct((lay.n_tiles * ROUTE_SPLIT, N_EXPERTS, LANES), F32)],
        grid=(lay.n_tiles,),
        in_specs=lay.row_specs(x_parts, d) + lay.row_specs(a_parts, kdim) + [
            const((kdim, d)),
            pl.BlockSpec((1, N_MOD, d), lambda i: (lay.mod_row(i), 0, 0)),
            const((1, d)), const((N_EXPERTS, d)), const((N_EXPERTS, 1)), const((sub, sub)),
        ],
        out_specs=[pl.BlockSpec((tm, d), lambda i: (i, 0)), pl.BlockSpec((tm, d // 2), lambda i: (i, 0)), pair_spec,
                   pl.BlockSpec((tm, LANES), lambda i: (i, 0)), pair_spec,
                   pl.BlockSpec((ROUTE_SPLIT, N_EXPERTS, LANES), lambda i: (i, 0, 0))],
        scratch_shapes=[pltpu.VMEM((N_EXPERTS, sub), F32), pltpu.VMEM((LANES, sub), F32)],
        compiler_params=_cparams("arbitrary"),
        name="proj_residual_route",
    )(*x_parts, *a_parts, w, mods, ffn_g.reshape(1, d), w_router_t, router_bias.reshape(N_EXPERTS, 1), tri)


def dispatch_plan(idx, rank, cnt, n_sorted):
    t = idx.shape[1]
    n_tiles, tm = cnt.shape[0], t // cnt.shape[0]
    cnt_tile = cnt[:, :, 0].astype(I32)
    total = jnp.sum(cnt_tile, axis=0)
    padded = ((total + EXPERT_TILE - 1) // EXPERT_TILE) * EXPERT_TILE
    end = jnp.cumsum(padded)
    start = end - padded
    base = start[None, :] + jnp.cumsum(cnt_tile, axis=0) - cnt_tile
    idx3 = idx.reshape(TOP_K, n_tiles, tm)
    pos = rank.reshape(TOP_K, n_tiles, tm)
    for e in range(N_EXPERTS):
        pos = pos + jnp.where(idx3 == e, base[None, :, e, None], 0)
    tile_row = jnp.arange(n_sorted // EXPERT_TILE, dtype=I32) * EXPERT_TILE
    tile_expert = jnp.minimum(jnp.sum(end[None, :] <= tile_row[:, None], axis=1), N_EXPERTS - 1).astype(I32)
    n_valid = (end[-1] // EXPERT_TILE).astype(I32).reshape(1)
    return pos.reshape(TOP_K * t).astype(I32), tile_expert, n_valid


def _sc_mesh():
    return plsc.VectorSubcoreMesh(core_axis_name="c", subcore_axis_name="s")


def _sc_worker_id():
    return lax.axis_index("s") * V7X_SC_CORES + lax.axis_index("c")


def _sc_two_stage_pipeline(n_chunks, fetch_idx, load, store):
    assert n_chunks % 2 == 0
    fetch_idx(0, 0)
    load(0, 0).start()

    @pl.loop(0, n_chunks // 2)
    def _(p):
        j = 2 * p

        @pl.when(p > 0)
        def _():
            store(j - 1, 1).wait()

        fetch_idx(j + 1, 1)
        load(j + 1, 1).start()
        load(j, 0).wait()
        store(j, 0).start()
        store(j, 0).wait()

        @pl.when(j + 2 < n_chunks)
        def _():
            fetch_idx(j + 2, 0)
            load(j + 2, 0).start()

        load(j + 1, 1).wait()
        store(j + 1, 1).start()

    store(n_chunks - 1, 1).wait()


_SC_SCRATCH = lambda chunk, d, dtype: [
    pltpu.VMEM((chunk,), I32), pltpu.VMEM((chunk,), I32),
    pltpu.VMEM((chunk, d), dtype), pltpu.VMEM((chunk, d), dtype),
    pltpu.SemaphoreType.DMA, pltpu.SemaphoreType.DMA, pltpu.SemaphoreType.DMA, pltpu.SemaphoreType.DMA]


def sc_scatter_rows(src, pos, n_out_rows, src_row0, t):
    d = src.shape[1]
    n_idx = pos.shape[0]
    per_worker = n_idx // SC_WORKERS
    chunk = SC_CHUNK_ROWS
    assert n_idx % (SC_WORKERS * chunk * 2) == 0 and t % chunk == 0

    @functools.partial(
        pl.kernel, mesh=_sc_mesh(), out_type=jax.ShapeDtypeStruct((n_out_rows, d), src.dtype),
        scratch_types=_SC_SCRATCH(chunk, d, src.dtype), name="sc_scatter_rows")
    def scatter(src_hbm, pos_hbm, out_hbm, idx_a, idx_b, rows_a, rows_b, lsem_a, lsem_b, ssem_a, ssem_b):
        base = _sc_worker_id() * per_worker
        idx, rows, lsem, ssem = (idx_a, idx_b), (rows_a, rows_b), (lsem_a, lsem_b), (ssem_a, ssem_b)

        def fetch_idx(j, b):
            pltpu.sync_copy(pos_hbm.at[pl.ds(base + j * chunk, chunk)], idx[b])

        def load(j, b):
            first = src_row0 + lax.rem(base + j * chunk, t)
            return pltpu.make_async_copy(src_hbm.at[pl.ds(first, chunk)], rows[b], lsem[b])

        def store(j, b):
            return pltpu.make_async_copy(rows[b], out_hbm.at[idx[b]], ssem[b])

        _sc_two_stage_pipeline(per_worker // chunk, fetch_idx, load, store)

    return scatter(src, pos)


def sc_gather_rows(table, idx):
    _, d = table.shape
    n_idx = idx.shape[0]
    per_worker = n_idx // SC_WORKERS
    chunk = SC_CHUNK_ROWS
    assert n_idx % (SC_WORKERS * chunk * 2) == 0

    @functools.partial(
        pl.kernel, mesh=_sc_mesh(), out_type=jax.ShapeDtypeStruct((n_idx, d), table.dtype),
        scratch_types=_SC_SCRATCH(chunk, d, table.dtype), name="sc_gather_rows")
    def gather(table_hbm, idx_hbm, out_hbm, idx_a, idx_b, rows_a, rows_b, lsem_a, lsem_b, ssem_a, ssem_b):
        base = _sc_worker_id() * per_worker
        idx, rows, lsem, ssem = (idx_a, idx_b), (rows_a, rows_b), (lsem_a, lsem_b), (ssem_a, ssem_b)

        def fetch_idx(j, b):
            pltpu.sync_copy(idx_hbm.at[pl.ds(base + j * chunk, chunk)], idx[b])

        def load(j, b):
            return pltpu.make_async_copy(table_hbm.at[idx[b]], rows[b], lsem[b])

        def store(j, b):
            return pltpu.make_async_copy(rows[b], out_hbm.at[pl.ds(base + j * chunk, chunk)], ssem[b])

        _sc_two_stage_pipeline(per_worker // chunk, fetch_idx, load, store)

    return gather(table, idx)


def _experts_kernel(te_ref, nv_ref, x_ref, wg_ref, wu_ref, wd_ref, y_ref, wg_scr, wu_scr, wd_scr):
    i = pl.program_id(0)

    @pl.when(i < nv_ref[0])
    def _():
        @pl.when((i == 0) | (te_ref[i] != te_ref[jnp.maximum(i - 1, 0)]))
        def _():
            wg_scr[...] = wg_ref[0, 0].astype(BF16)
            wu_scr[...] = wu_ref[0, 0].astype(BF16)
            wd_scr[...] = wd_ref[0, 0].astype(BF16)

        x_lo, x_hi = _unpack_bf16_halves(x_ref[...])
        half = x_lo.shape[1]

        def in_proj(w_scr):
            return (jnp.dot(x_lo, w_scr[0:half, :], preferred_element_type=F32)
                    + jnp.dot(x_hi, w_scr[half:, :], preferred_element_type=F32))

        a = jax.nn.silu(in_proj(wg_scr)) * in_proj(wu_scr)
        y = jnp.dot(a.astype(BF16), wd_scr[...], preferred_element_type=F32)
        y_ref[...] = _pack_bf16_halves(y.astype(BF16))


def grouped_experts(xs, tile_expert, n_valid, wg, wu, wd, layer):
    n_rows = xs.shape[0]
    d, de = wg.shape[-2:]
    tm = EXPERT_TILE
    row_map = lambda i, te, nv: (jnp.minimum(i, nv[0] - 1), 0)
    grid_spec = pltpu.PrefetchScalarGridSpec(
        num_scalar_prefetch=2,
        grid=(n_rows // tm,),
        in_specs=[
            pl.BlockSpec((tm, d // 2), row_map),
            pl.BlockSpec((1, 1, d, de), lambda i, te, nv: (layer, te[i], 0, 0)),
            pl.BlockSpec((1, 1, d, de), lambda i, te, nv: (layer, te[i], 0, 0)),
            pl.BlockSpec((1, 1, de, d), lambda i, te, nv: (layer, te[i], 0, 0)),
        ],
        out_specs=pl.BlockSpec((tm, d // 2), row_map),
        scratch_shapes=[pltpu.VMEM((d, de), BF16), pltpu.VMEM((d, de), BF16), pltpu.VMEM((de, d), BF16)],
    )
    return pl.pallas_call(
        _experts_kernel,
        out_shape=jax.ShapeDtypeStruct((n_rows, d // 2), jnp.uint32),
        grid_spec=grid_spec,
        compiler_params=_cparams("arbitrary"),
        name="grouped_experts",
    )(tile_expert, n_valid, xs, wg, wu, wd)


def _final_combine_kernel(x_ref, y_ctx_ref, y_dec_ref, w_ref, mod_ref, fg_ref, o_ctx_ref, o_dec_ref, *, n_ctx_tiles):
    is_dec = pl.program_id(0) >= n_ctx_tiles

    def body(y_ref, o_ref):
        out = _moe_combine(x_ref[...], y_ref, w_ref, mod_ref)
        ms = jnp.mean(out * out, axis=-1, keepdims=True)
        o_ref[...] = out * lax.rsqrt(ms + NORM_EPS) * fg_ref[...]

    pl.when(jnp.logical_not(is_dec))(lambda: body(y_ctx_ref, o_ctx_ref))
    pl.when(is_dec)(lambda: body(y_dec_ref, o_dec_ref))


def final_combine(lay, x, y_parts, w_col, mods, final_g):
    d = x.shape[1]
    tm = lay.tm
    return pl.pallas_call(
        functools.partial(_final_combine_kernel, n_ctx_tiles=lay.n_ctx_tiles),
        out_shape=[jax.ShapeDtypeStruct((lay.t_ctx, d), F32), jax.ShapeDtypeStruct((lay.t_dec, d), F32)],
        grid=(lay.n_tiles,),
        in_specs=[pl.BlockSpec((tm, d), lambda i: (i, 0))] + lay.expert_out_specs(d) + [
            pl.BlockSpec((tm, LANES), lambda i: (i, 0)),
            pl.BlockSpec((1, N_MOD, d), lambda i: (lay.mod_row(i), 0, 0)),
            pl.BlockSpec((1, d), lambda i: (0, 0)),
        ],
        out_specs=lay.row_specs([None, None], d),
        compiler_params=_cparams("arbitrary"),
        name="final_combine",
    )(x, *y_parts, w_col, mods, final_g.reshape(1, d))


def group_moe(lay, x, routing, p, layer):
    d = x.shape[1]
    h, idx, w_col, rank, cnt = routing
    routing_tile = lay.t // cnt.shape[0]
    y_parts = []
    for r0, r1 in ((0, lay.t_ctx), (lay.t_ctx, lay.t)):
        n_sorted = TOP_K * (r1 - r0) + N_EXPERTS * (EXPERT_TILE - 1)
        n_sorted = -(-n_sorted // EXPERT_TILE) * EXPERT_TILE
        pos, tile_expert, n_valid = dispatch_plan(idx[:, r0:r1], rank[:, r0:r1],
                                                  cnt[r0 // routing_tile:r1 // routing_tile], n_sorted)
        xs = sc_scatter_rows(h, pos, n_sorted, r0, r1 - r0)
        ys = grouped_experts(xs, tile_expert, n_valid, p["moe_w_gate"], p["moe_w_up"], p["moe_w_down"], layer)
        y_parts.append(sc_gather_rows(ys, pos).reshape(TOP_K, r1 - r0, d // 2))
    return y_parts, w_col


def _rope_angles(n, d):
    n_rows = n // GRID_W
    row = jnp.repeat(jnp.arange(n_rows), GRID_W).astype(F32)
    col = jnp.tile(jnp.arange(GRID_W), n_rows).astype(F32)
    nf = d // 4
    freqs = jnp.power(ROPE_BASE, -jnp.arange(nf, dtype=F32) / nf)
    ang = jnp.concatenate([row[:, None] * freqs, col[:, None] * freqs], axis=-1)
    return jnp.cos(ang), jnp.sin(ang)


def kernel(x_prompt, x_sample, cache_attn_k, cache_attn_v, state_ret_fwd, state_ret_bwd, c, c_ctx, w_ada, b_ada, norm_mix_g, norm_ffn_g, final_norm_g, da_w_qkv, da_lambda_q1, da_lambda_k1, da_lambda_q2, da_lambda_k2, da_subln_g, da_w_o, ret_w_qkv, ret_w_gate_fwd, ret_w_gate_bwd, ret_decay_fwd, ret_decay_bwd, ret_w_o, w_router, router_bias, moe_w_gate, moe_w_up, moe_w_down):
    b_ctx, n_ctx, d = x_prompt.shape
    b_dec, n_dec, _ = x_sample.shape
    past = cache_attn_k.shape[2]
    n_attn = cache_attn_k.shape[1]
    depth = w_ada.shape[0]
    assert b_dec + 1 <= MOD_ROWS
    lay = Layout(b_ctx, n_ctx, b_dec, n_dec)

    cond = jnp.zeros((MOD_ROWS, d), F32).at[0].set(c_ctx).at[1:1 + b_dec].set(c)
    mods_all = ada_modulation(cond, w_ada, b_ada)

    p = {
        "norm_ffn_g": norm_ffn_g, "final_norm_g": final_norm_g,
        "w_router_t": w_router.T.astype(BF16), "router_bias": router_bias.astype(F32),
        "moe_w_gate": moe_w_gate, "moe_w_up": moe_w_up, "moe_w_down": moe_w_down,
    }
    ret_log_decay = jnp.stack([jax.nn.log_sigmoid(ret_decay_fwd.astype(F32)),
                               jax.nn.log_sigmoid(ret_decay_bwd.astype(F32))], axis=1)
    ck_all = cache_attn_k.reshape(b_dec, n_attn, past, -1)
    cv_all = cache_attn_v.reshape(b_dec, n_attn, past, -1)

    x_parts = [x_prompt.reshape(lay.t_ctx, d), x_sample.reshape(lay.t_dec, d)]
    new_k, new_v, new_sf, new_sb = [], [], [], []
    pending_moe = None

    def first_kernel(*args):
        nonlocal x_parts, pending_moe
        outs = norm_mod_matmul(lay, x_parts, *args, pending_moe=pending_moe)
        if pending_moe is not None:
            *outs, x_joint = outs
            x_parts, pending_moe = [x_joint], None
        return outs

    for i in range(depth):
        mods = mods_all[i]
        j = i // 2
        if i % 2 == 0:
            lam_init = 0.8 - 0.6 * math.exp(-0.3 * i)
            qkw = DA_HEADS * 2 * DA_HEAD_DIM
            vw = DA_HEADS * DA_V_DIM
            cos, sin = _rope_angles(n_dec, DA_HEAD_DIM)
            rope = (jnp.tile(cos, (1, 4)), jnp.concatenate([-sin, sin, -sin, sin], axis=-1))
            q, k, v, k_ctx, v_ctx = first_kernel(
                norm_mix_g[i], mods, da_w_qkv[j].astype(BF16),
                [(0, qkw, "rope64", math.log2(math.e) * DA_HEAD_DIM ** -0.5, BF16, "all"),
                 (qkw, qkw, "rope64", 1.0, BF16, "all"),
                 (2 * qkw, vw, "plain", 1.0, BF16, "all"),
                 (qkw, qkw, "plain", 1.0, F32, "ctx64"), (2 * qkw, vw, "plain", 1.0, F32, "ctx")],
                rope)
            lam_params = jnp.stack([da_lambda_q1[j], da_lambda_k1[j], da_lambda_q2[j], da_lambda_k2[j]])
            mix = [diff_attention(q, k, v, 0, b_ctx, n_ctx, lam_params, da_subln_g[j], lam_init,
                                  heads_per_step=DA_HEADS),
                   diff_attention(q, k, v, lay.t_ctx, b_dec, n_dec, lam_params, da_subln_g[j], lam_init,
                                  cache=(ck_all[:, j], cv_all[:, j]))]
            mix_w = da_w_o[j].astype(BF16)
            new_k.append(k_ctx.reshape(b_ctx, n_ctx, DA_HEADS, 2, DA_HEAD_DIM))
            new_v.append(v_ctx.reshape(b_ctx, n_ctx, DA_HEADS, DA_V_DIM))
        else:
            kd = ret_w_qkv.shape[2] // 4
            dv = 2 * kd
            w_all = jnp.concatenate([ret_w_qkv[j], ret_w_gate_fwd[j], ret_w_gate_bwd[j]], axis=-1).astype(BF16)
            q, k, v, gates = first_kernel(
                norm_mix_g[i], mods, w_all,
                [(0, kd, "rope256", 1.0, BF16, "all"),
                 (kd, kd, "rope256", (kd // RET_HEADS) ** -0.5, F32, "all"),
                 (2 * kd, dv, "plain", 1.0, BF16, "all"), (2 * kd + dv, 2 * dv, "silu", 1.0, BF16, "all")],
                _rope_angles(n_dec, kd // RET_HEADS))
            ctx_seqs = math.gcd(b_ctx, max(1, n_dec // n_ctx // 2))
            o_ctx, sf, sb = retention(q, k, v, gates, 0, b_ctx, n_ctx, ret_log_decay[j], emit_state=True,
                                      seqs_per_step=ctx_seqs)
            (o_dec,) = retention(q, k, v, gates, lay.t_ctx, b_dec, n_dec, ret_log_decay[j],
                                 state=(state_ret_fwd[:, j:j + 1], state_ret_bwd[:, j:j + 1]))
            mix, mix_w = [o_ctx, o_dec], ret_w_o[j].astype(BF16)
            new_sf.append(sf)
            new_sb.append(sb)
        x, *routing = proj_residual_route(lay, x_parts, mix, mix_w, mods, norm_ffn_g[i], p["w_router_t"],
                                          p["router_bias"])
        y_parts, w_col = group_moe(lay, x, routing, p, i)
        if i == depth - 1:
            x_parts = final_combine(lay, x, y_parts, w_col, mods, final_norm_g)
        else:
            x_parts, pending_moe = [x], (y_parts, w_col, mods)

    y_ctx, y_dec = x_parts
    return (y_ctx.reshape(b_ctx, n_ctx, d), y_dec.reshape(b_dec, n_dec, d),
            jnp.stack(new_k, axis=1), jnp.stack(new_v, axis=1),
            jnp.concatenate(new_sf, axis=1), jnp.concatenate(new_sb, axis=1))
```

```python
import functools
import math

import jax
import jax.numpy as jnp
from jax import lax
from jax.experimental import pallas as pl
from jax.experimental.pallas import tpu as pltpu
from jax.experimental.pallas import tpu_sc as plsc

F32 = jnp.float32
BF16 = jnp.bfloat16
I32 = jnp.int32

GRID_W = 64
ROPE_BASE = 10000.0
NORM_EPS = 1e-6
DA_HEADS = 8
DA_HEAD_DIM = 64
DA_V_DIM = 2 * DA_HEAD_DIM
RET_HEADS = 4
RET_CHUNK = 128
N_EXPERTS = 16
N_GROUPS = 4
EXPERTS_PER_GROUP = N_EXPERTS // N_GROUPS
TOP_K = 2
N_MOD = 6
MOD_ROWS = 16
LANES = 128

V7X_VMEM_LIMIT = 56 * 1024 * 1024
V7X_SC_CORES = 2
V7X_SC_SUBCORES = 16
SC_WORKERS = V7X_SC_CORES * V7X_SC_SUBCORES
SC_CHUNK_ROWS = 64

ROW_TILE = 512
EXPERT_TILE = 256
ROUTE_SPLIT = 2


def _cparams(*sem):
    return pltpu.CompilerParams(dimension_semantics=sem, vmem_limit_bytes=V7X_VMEM_LIMIT)


def _nt_dot(a, b):
    return lax.dot_general(a, b, (((1,), (1,)), ((), ())), preferred_element_type=F32)


def _tn_dot(a, b):
    return lax.dot_general(a, b, (((0,), (0,)), ((), ())), preferred_element_type=F32)


class Layout:
    def __init__(self, b_ctx, n_ctx, b_dec, n_dec):
        self.b_ctx, self.n_ctx, self.b_dec, self.n_dec = b_ctx, n_ctx, b_dec, n_dec
        self.t_ctx, self.t_dec = b_ctx * n_ctx, b_dec * n_dec
        self.t = self.t_ctx + self.t_dec
        self.tm = min(ROW_TILE, n_dec, self.t_ctx)
        assert self.t_ctx % self.tm == 0 and n_dec % self.tm == 0
        assert self.t_ctx % n_dec == 0 and self.t_ctx % n_ctx == 0
        self.n_ctx_tiles = self.t_ctx // self.tm
        self.n_tiles = self.t // self.tm

    def mod_row(self, i):
        r = i * self.tm
        return jnp.where(r < self.t_ctx, 0, 1 + (r - self.t_ctx) // self.n_dec)

    def part_tile(self, part, i):
        if part == 0:
            return jnp.minimum(i, self.n_ctx_tiles - 1)
        return jnp.maximum(i - self.n_ctx_tiles, 0)

    def expert_out_specs(self, d):
        return [pl.BlockSpec((TOP_K, self.tm, d // 2), lambda i, p=p: (0, self.part_tile(p, i), 0)) for p in (0, 1)]

    def row_specs(self, arrays, width):
        if len(arrays) == 1:
            return [pl.BlockSpec((self.tm, width), lambda i, *_: (i, 0))]
        return [pl.BlockSpec((self.tm, width), lambda i, *_, p=p: (self.part_tile(p, i), 0)) for p in (0, 1)]


def _ada_kernel(c_ref, w_ref, b_ref, o_ref):
    s = jax.nn.silu(c_ref[...]).astype(BF16)
    acc = jnp.dot(s, w_ref[0].astype(BF16), preferred_element_type=F32)
    o_ref[0] = acc + b_ref[0]


def ada_modulation(cond, w_ada, b_ada):
    depth, d, n = w_ada.shape
    tn = 1536
    out = pl.pallas_call(
        _ada_kernel,
        out_shape=jax.ShapeDtypeStruct((depth, MOD_ROWS, n), F32),
        grid=(depth, n // tn),
        in_specs=[
            pl.BlockSpec((MOD_ROWS, d), lambda l, j: (0, 0)),
            pl.BlockSpec((1, d, tn), lambda l, j: (l, 0, j)),
            pl.BlockSpec((1, 1, tn), lambda l, j: (l, 0, j)),
        ],
        out_specs=pl.BlockSpec((1, MOD_ROWS, tn), lambda l, j: (l, 0, j)),
        compiler_params=_cparams("parallel", "parallel"),
        name="ada_modulation",
    )(cond, w_ada, b_ada.reshape(depth, 1, n))
    return out.reshape(depth, MOD_ROWS, N_MOD, d)


def _modulated_norm(x, g, mod_ref, shift_idx):
    ms = jnp.mean(x * x, axis=-1, keepdims=True)
    y = x * lax.rsqrt(ms + NORM_EPS) * g
    return y * (1.0 + mod_ref[0, shift_idx + 1:shift_idx + 2, :]) + mod_ref[0, shift_idx:shift_idx + 1, :]


def _by_part(is_dec, refs, fn):
    if len(refs) == 1:
        fn(refs[0])
        return
    pl.when(jnp.logical_not(is_dec))(lambda: fn(refs[0]))
    pl.when(is_dec)(lambda: fn(refs[1]))


def _rope64(a, cos, sin_signed, first_half):
    partner = jnp.where(first_half, pltpu.roll(a, 96, 1), pltpu.roll(a, 32, 1))
    return a * cos + partner * sin_signed


def _moe_combine(x, y_ref, w_ref, mod_ref):
    w = w_ref[...]

    def expert_out(slot):
        lo, hi = _unpack_bf16_halves(y_ref[slot])
        return jnp.concatenate([lo.astype(F32), hi.astype(F32)], axis=1)

    return x + mod_ref[0, 5:6, :] * (w[:, 0:1] * expert_out(0) + w[:, 1:2] * expert_out(1))


def _nmm_kernel(*refs, n_x, pending_moe, outs, tn, n_ctx_tiles):
    x_refs = refs[:n_x]
    if pending_moe:
        y_refs, (wcol_ref, prev_mod_ref) = refs[n_x:n_x + 2], refs[n_x + 2:n_x + 4]
        refs = refs[:n_x] + refs[n_x + 4:]
    g_ref, mod_ref, w_ref, cos_ref, sin_ref = refs[n_x:n_x + 5]
    out_refs = refs[n_x + 5:]
    is_dec = pl.program_id(0) >= n_ctx_tiles
    tm = x_refs[0].shape[0]
    segments = sorted({(col0, width) for col0, width, _, _, _ in outs})

    def rotated(a, kind):
        cos, sin = cos_ref[...], sin_ref[...]
        if kind == "rope64":
            lane = lax.broadcasted_iota(I32, (tm, LANES), 1)
            first_half = (lane & 32) == 0
            return [(c * 128, _rope64(a[:, c * 128:(c + 1) * 128], cos, sin, first_half))
                    for c in range(tn // 128)]
        pieces = []
        for c in range(tn // 256):
            x1, x2 = a[:, c * 256:c * 256 + 128], a[:, c * 256 + 128:(c + 1) * 256]
            pieces += [(c * 256, x1 * cos - x2 * sin), (c * 256 + 128, x2 * cos + x1 * sin)]
        return pieces

    def emit(dec):
        x = x_refs[(1 if dec else 0) if n_x == 2 else 0][...]
        if pending_moe:
            x = _moe_combine(x, y_refs[1 if dec else 0], wcol_ref, prev_mod_ref)
            out_refs[-1][...] = x
        h = _modulated_norm(x, g_ref[...], mod_ref, 0).astype(BF16)
        for col0, width in segments:
            sinks = [(o_ref, o) for o_ref, o in zip(out_refs, outs)
                     if (o[0], o[1]) == (col0, width) and not (dec and o[4] != "all")]
            for blk in range(width // tn):
                acc = jnp.dot(h, w_ref[:, col0 + blk * tn:col0 + (blk + 1) * tn], preferred_element_type=F32)
                for o_ref, (_, _, kind, scale, rows) in sinks:
                    a = acc if scale == 1.0 else acc * scale
                    if rows == "ctx64":
                        n64 = width // 64
                        for c in range(tn // 64):
                            o_ref[pl.ds(blk * tn // 64 + c, tm, stride=n64), :] = (
                                a[:, c * 64:(c + 1) * 64].astype(o_ref.dtype))
                    elif kind == "silu":
                        o_ref[:, blk * tn:(blk + 1) * tn] = jax.nn.silu(a).astype(o_ref.dtype)
                    elif kind == "plain" or rows == "ctx" or not dec:
                        o_ref[:, blk * tn:(blk + 1) * tn] = a.astype(o_ref.dtype)
                    else:
                        for off, val in rotated(a, kind):
                            o_ref[:, blk * tn + off:blk * tn + off + 128] = val.astype(o_ref.dtype)

    pl.when(jnp.logical_not(is_dec))(lambda: emit(False))
    pl.when(is_dec)(lambda: emit(True))


def norm_mod_matmul(lay, x_parts, g, mods, w, outputs, rope_tables, pending_moe=None, tn=512):
    d = x_parts[0].shape[1]
    tm = lay.tm
    n_total = w.shape[1]
    outs, out_shapes, out_specs = [], [], []
    for col0, width, kind, scale, dtype, rows in outputs:
        assert width % tn == 0 and col0 % LANES == 0
        outs.append((col0, width, kind, float(scale), rows))
        if rows == "ctx64":
            n64 = width // 64
            out_shapes.append(jax.ShapeDtypeStruct((lay.t_ctx * n64, 64), dtype))
            out_specs.append(pl.BlockSpec((tm * n64, 64), lambda i: (lay.part_tile(0, i), 0)))
            continue
        n_rows = lay.t_ctx if rows == "ctx" else lay.t
        out_shapes.append(jax.ShapeDtypeStruct((n_rows, width), dtype))
        out_specs.append(lay.row_specs([None, None], width)[0] if rows == "ctx" else lay.row_specs([None], width)[0])
    blocks_per_seq = lay.n_dec // tm
    rope_spec = pl.BlockSpec((tm, LANES), lambda i: (lay.part_tile(1, i) % blocks_per_seq, 0))
    mod_spec = pl.BlockSpec((1, N_MOD, d), lambda i: (lay.mod_row(i), 0, 0))
    moe_specs, moe_args = [], []
    if pending_moe is not None:
        assert len(x_parts) == 1
        y_parts, w_col, prev_mods = pending_moe
        moe_specs = lay.expert_out_specs(d) + [pl.BlockSpec((tm, LANES), lambda i: (i, 0)), mod_spec]
        moe_args = [*y_parts, w_col, prev_mods]
        out_shapes.append(jax.ShapeDtypeStruct((lay.t, d), F32))
        out_specs.append(pl.BlockSpec((tm, d), lambda i: (i, 0)))
    return pl.pallas_call(
        functools.partial(_nmm_kernel, n_x=len(x_parts), pending_moe=pending_moe is not None, outs=tuple(outs),
                          tn=tn, n_ctx_tiles=lay.n_ctx_tiles),
        out_shape=out_shapes,
        grid=(lay.n_tiles,),
        in_specs=lay.row_specs(x_parts, d) + moe_specs + [
            pl.BlockSpec((1, d), lambda i: (0, 0)),
            mod_spec,
            pl.BlockSpec((d, n_total), lambda i: (0, 0), pipeline_mode=pl.Buffered(1)),
            rope_spec, rope_spec,
        ],
        out_specs=out_specs,
        compiler_params=_cparams("arbitrary"),
        name="norm_mod_matmul",
    )(*x_parts, *moe_args, g.reshape(1, d), mods, w, *rope_tables)


def _diff_lambda(lam_ref, lam_init):
    lp = lam_ref[...]
    return (jnp.exp(jnp.sum(lp[0:1] * lp[1:2], axis=-1, keepdims=True))
            - jnp.exp(jnp.sum(lp[2:3] * lp[3:4], axis=-1, keepdims=True)) + lam_init)


ONES_ROWS = 16


def _scores_t(k, q, comp):
    lane = lax.broadcasted_iota(I32, q.shape, 1)
    return _nt_dot(k, jnp.where((lane < DA_HEAD_DIM) == (comp == 0), q, jnp.zeros_like(q)))


def _softmax_values_t(scores, vts):
    dv = vts[0].shape[0] - ONES_ROWS
    maxes = [jnp.max(s, axis=0, keepdims=True) for s in scores]
    exps = [jnp.exp2(s - m).astype(BF16) for s, m in zip(scores, maxes)]
    accs = [jnp.dot(vt, e, preferred_element_type=F32) for vt, e in zip(vts, exps)]
    return [acc[0:dv] / acc[dv:dv + 1] for acc in accs]


def _diff_finish_t(parts, lam, g_col, lam_init):
    o = parts[0] - lam * parts[1]
    ms = jnp.mean(o * o, axis=0, keepdims=True)
    return jnp.transpose((o * lax.rsqrt(ms + NORM_EPS) * g_col) * (1.0 - lam_init))


def _diff_attn_kernel(lam_ref, q_ref, k_ref, v_ref, g_ref, o_ref, *, lam_init, heads_per_step):
    hw = 2 * DA_HEAD_DIM
    lam = _diff_lambda(lam_ref, lam_init)
    ones = jnp.ones((ONES_ROWS, k_ref.shape[0]), BF16)
    heads = [slice(hh * hw, (hh + 1) * hw) for hh in range(heads_per_step)]
    scores = [[_scores_t(k_ref[:, cols], q_ref[:, cols], comp) for comp in range(2)] for cols in heads]
    vts = [jnp.concatenate([jnp.transpose(v_ref[:, cols].astype(F32)).astype(BF16), ones], axis=0) for cols in heads]
    parts = _softmax_values_t([s for pair in scores for s in pair], [vt for vt in vts for _ in range(2)])
    for hh, cols in enumerate(heads):
        o_ref[:, cols] = _diff_finish_t(parts[2 * hh:2 * hh + 2], lam, g_ref[...], lam_init).astype(o_ref.dtype)


def _diff_attn_cached_kernel(lam_ref, q_ref, k_ref, v_ref, ck_ref, cv_ref, g_ref, o_ref, k_scr, vt_scr, s_scr, *,
                             lam_init, sub_rows):
    seq_len, hw = q_ref.shape
    n_sub = seq_len // sub_rows
    assert n_sub % 2 == 0
    lam = _diff_lambda(lam_ref, lam_init)
    k_scr[0:seq_len, :] = k_ref[...]
    k_scr[seq_len:, :] = ck_ref[0].astype(BF16)
    vt_scr[0:hw, 0:seq_len] = jnp.transpose(v_ref[...])
    vt_scr[0:hw, seq_len:] = jnp.transpose(cv_ref[0]).astype(BF16)
    vt_scr[hw:, :] = jnp.ones((ONES_ROWS, vt_scr.shape[1]), BF16)

    def rows_of(t):
        return pl.ds(pl.multiple_of(t * sub_rows, sub_rows), sub_rows)

    def scores(t, slot, comp):
        s_scr[slot, comp] = _scores_t(k_scr[...], q_ref[rows_of(t), :], comp)

    def stage(t_next, slot_next, t, slot):
        for comp in range(2):
            scores(t_next, slot_next, comp)
        parts = _softmax_values_t([s_scr[slot, 0], s_scr[slot, 1]], [vt_scr[...]] * 2)
        o_ref[rows_of(t), :] = _diff_finish_t(parts, lam, g_ref[...], lam_init).astype(o_ref.dtype)

    scores(0, 0, 0)
    scores(0, 0, 1)

    def body(i2, carry):
        t = 2 * i2
        stage(t + 1, 1, t, 0)
        stage(jnp.minimum(t + 2, n_sub - 1), 0, t + 1, 1)
        return carry

    lax.fori_loop(0, n_sub // 2, body, 0, unroll=2 if n_sub % 4 == 0 else 1)


def diff_attention(q, k, v, row0, batch, seq_len, lam_params, subln_g, lam_init, cache=None, heads_per_step=1,
                   sub_rows=256):
    width = q.shape[1]
    hw = 2 * DA_HEAD_DIM
    bw = heads_per_step * hw
    assert row0 % seq_len == 0 and DA_HEADS % heads_per_step == 0
    s0 = row0 // seq_len
    seq_spec = pl.BlockSpec((seq_len, bw), lambda b, h: (s0 + b, h))
    in_specs = [pl.BlockSpec((4, DA_HEAD_DIM), lambda b, h: (0, 0)), seq_spec, seq_spec, seq_spec]
    args = [lam_params, q, k, v]
    scratch = []
    if cache is None:
        body = functools.partial(_diff_attn_kernel, lam_init=lam_init, heads_per_step=heads_per_step)
    else:
        assert heads_per_step == 1
        past = cache[0].shape[1]
        cache_spec = pl.BlockSpec((1, past, hw), lambda b, h: (b, 0, h))
        in_specs += [cache_spec, cache_spec]
        args += list(cache)
        sub_rows = min(sub_rows, seq_len // 2)
        scratch = [pltpu.VMEM((seq_len + past, hw), BF16), pltpu.VMEM((hw + ONES_ROWS, seq_len + past), BF16),
                   pltpu.VMEM((2, 2, seq_len + past, sub_rows), F32)]
        body = functools.partial(_diff_attn_cached_kernel, lam_init=lam_init, sub_rows=sub_rows)
    in_specs.append(pl.BlockSpec((hw, 1), lambda b, h: (0, 0)))
    args.append(subln_g.reshape(hw, 1))
    return pl.pallas_call(
        body,
        out_shape=jax.ShapeDtypeStruct((batch * seq_len, width), BF16),
        grid=(batch, DA_HEADS // heads_per_step),
        in_specs=in_specs,
        out_specs=pl.BlockSpec((seq_len, bw), lambda b, h: (b, h)),
        scratch_shapes=scratch,
        compiler_params=_cparams("parallel", "parallel"),
        name="diff_attention",
    )(*args)


def _retention_kernel(*refs, seq_len, seqs, has_state, emit_state):
    lg_ref, q_ref, k_ref, v_ref, gf_ref, gb_ref = refs[:6]
    pos = 6
    if has_state:
        s0_refs = refs[6:8]
        pos = 8
    o_ref = refs[pos]
    pos += 1
    if emit_state:
        s_out_refs = refs[pos:pos + 2]
        pos += 2
    s_scr, o_scr = refs[pos:pos + 2]

    h = pl.program_id(1)
    c_len = RET_CHUNK
    n_chunks = seq_len // c_len
    assert n_chunks % 2 == 0
    row = lax.broadcasted_iota(I32, (c_len, c_len), 0)
    colm = lax.broadcasted_iota(I32, (c_len, c_len), 1)
    rel = (row - colm).astype(F32)
    posv = lax.broadcasted_iota(I32, (c_len, 1), 0).astype(F32)

    consts = []
    for backward in (False, True):
        lg = lg_ref[1 if backward else 0, h]
        if backward:
            intra = jnp.where(rel <= 0, jnp.exp(-rel * lg), 0.0)
            q_decay = jnp.exp((c_len - posv) * lg)
            k_decay = jnp.exp(posv * lg)
        else:
            intra = jnp.where(rel >= 0, jnp.exp(rel * lg), 0.0)
            q_decay = jnp.exp((posv + 1.0) * lg)
            k_decay = jnp.exp((c_len - 1.0 - posv) * lg)
        consts.append((intra, q_decay, k_decay, jnp.exp(jnp.zeros((1, 1), F32) + c_len * lg)))
        direction = 1 if backward else 0
        for sq in range(seqs):
            if has_state:
                s_scr[sq, direction] = s0_refs[direction][sq, 0, 0]
            else:
                s_scr[sq, direction] = jnp.zeros(s_scr.shape[2:], F32)

    def body(ci, first_touch):
        chains = [(sq, direction, pl.ds(pl.multiple_of(sq * seq_len + c * c_len, c_len), c_len))
                  for sq in range(seqs) for direction, c in ((0, ci), (1, n_chunks - 1 - ci))]
        qs = [q_ref[rows, :].astype(BF16) for _, _, rows in chains]
        ks = [k_ref[rows, :].astype(F32) for _, _, rows in chains]
        vs = [v_ref[rows, :].astype(BF16) for _, _, rows in chains]
        intras = [(_nt_dot(qb, kf.astype(BF16)) * consts[d][0]).astype(BF16)
                  for (_, d, _), qb, kf in zip(chains, qs, ks)]
        states = [s_scr[sq, d] for sq, d, _ in chains]
        outs = [jnp.dot(a, vb, preferred_element_type=F32)
                + jnp.dot(qb, s.astype(BF16), preferred_element_type=F32) * consts[d][1]
                for (_, d, _), a, qb, vb, s in zip(chains, intras, qs, vs, states)]
        for (sq, d, _), kf, vb, s in zip(chains, ks, vs, states):
            s_scr[sq, d] = consts[d][3] * s + _tn_dot((kf * consts[d][2]).astype(BF16), vb)
        centred = [o - jnp.mean(o, axis=-1, keepdims=True) for o in outs]
        scales = [lax.rsqrt(jnp.mean(oc * oc, axis=-1, keepdims=True) + NORM_EPS) for oc in centred]
        for (_, d, rows), oc, scale in zip(chains, centred, scales):
            gated = oc * scale * (gb_ref if d else gf_ref)[rows, :].astype(F32)
            if first_touch:
                o_scr[rows, :] = gated
            else:
                o_ref[rows, :] = (o_scr[rows, :] + gated).astype(o_ref.dtype)

    unroll = 8 if n_chunks % 16 == 0 else 1
    lax.fori_loop(0, n_chunks // 2, lambda ci, c: (body(ci, True), c)[1], 0, unroll=unroll)
    lax.fori_loop(n_chunks // 2, n_chunks, lambda ci, c: (body(ci, False), c)[1], 0, unroll=unroll)
    if emit_state:
        for d in range(2):
            for sq in range(seqs):
                s_out_refs[d][sq, 0, 0] = s_scr[sq, d]


def retention(q, k, v, gates, row0, batch, seq_len, log_decay, state=None, emit_state=False, seqs_per_step=1):
    dk = q.shape[1] // RET_HEADS
    dv = v.shape[1] // RET_HEADS
    seqs = seqs_per_step
    rows = seqs * seq_len
    assert row0 % rows == 0 and batch % seqs == 0
    s0 = row0 // rows
    in_specs = [
        pl.BlockSpec(memory_space=pltpu.SMEM),
        pl.BlockSpec((rows, dk), lambda b, h: (s0 + b, h)),
        pl.BlockSpec((rows, dk), lambda b, h: (s0 + b, h)),
        pl.BlockSpec((rows, dv), lambda b, h: (s0 + b, h)),
        pl.BlockSpec((rows, dv), lambda b, h: (s0 + b, h)),
        pl.BlockSpec((rows, dv), lambda b, h: (s0 + b, RET_HEADS + h)),
    ]
    args = [log_decay, q, k, v, gates, gates]
    state_spec = pl.BlockSpec((seqs, 1, 1, dk, dv), lambda b, h: (b, 0, h, 0, 0))
    if state is not None:
        in_specs += [state_spec, state_spec]
        args += list(state)
    out_shapes = [jax.ShapeDtypeStruct((batch * seq_len, RET_HEADS * dv), BF16)]
    out_specs = [pl.BlockSpec((rows, dv), lambda b, h: (b, h))]
    if emit_state:
        out_shapes += [jax.ShapeDtypeStruct((batch, 1, RET_HEADS, dk, dv), F32)] * 2
        out_specs += [state_spec, state_spec]
    return pl.pallas_call(
        functools.partial(_retention_kernel, seq_len=seq_len, seqs=seqs, has_state=state is not None,
                          emit_state=emit_state),
        out_shape=out_shapes,
        grid=(batch // seqs, RET_HEADS),
        in_specs=in_specs,
        out_specs=out_specs,
        scratch_shapes=[pltpu.VMEM((seqs, 2, dk, dv), F32), pltpu.VMEM((rows, dv), F32)],
        compiler_params=_cparams("parallel", "parallel"),
        name="retention",
    )(*args)


def _pack_bf16_halves(x):
    half = x.shape[1] // 2
    lo = lax.bitcast_convert_type(x[:, :half].astype(F32), jnp.uint32)
    hi = lax.bitcast_convert_type(x[:, half:].astype(F32), jnp.uint32)
    return (hi & jnp.uint32(0xFFFF0000)) | (lo >> 16)


def _unpack_bf16_halves(p):
    lo = lax.bitcast_convert_type(p << 16, F32).astype(BF16)
    hi = lax.bitcast_convert_type(p & jnp.uint32(0xFFFF0000), F32).astype(BF16)
    return lo, hi


def _first_max_onehot(vals):
    m = vals[0]
    for v in vals[1:]:
        m = jnp.maximum(m, v)
    onehot, taken = [], None
    for v in vals:
        hit = v == m
        if taken is None:
            onehot.append(hit)
            taken = hit
        else:
            onehot.append(hit & jnp.logical_not(taken))
            taken = taken | hit
    return m, onehot


def _pick(onehot, vals):
    out = vals[-1]
    for oh, v in zip(onehot[-2::-1], vals[-2::-1]):
        out = jnp.where(oh, v, out)
    return out


def _route_rows(x, g_ref, mod_ref, wr_ref, bias_ref, tri_ref, h_ref, idx_ref, wcol_ref, rank_ref, cnt_ref,
                ind_scr, wrow_scr):
    hb = _modulated_norm(x, g_ref[...], mod_ref, 3).astype(BF16)
    h_ref[...] = _pack_bf16_halves(hb)
    logits = _nt_dot(wr_ref[...], hb)
    s = jax.nn.sigmoid(logits)
    sel = s + bias_ref[...]
    neg_inf = jnp.full_like(sel[0:1], -jnp.inf)
    sel_rows = [sel[e:e + 1] for e in range(N_EXPERTS)]
    s_rows = [s[e:e + 1] for e in range(N_EXPERTS)]

    def top2(vals):
        m1, oh1 = _first_max_onehot(vals)
        rest = [jnp.where(o, neg_inf, v) for o, v in zip(oh1, vals)]
        m2, oh2 = _first_max_onehot(rest)
        return m1, m2, oh1, oh2

    grp_scores = []
    for g in range(N_GROUPS):
        m1, m2, _, _ = top2(sel_rows[g * EXPERTS_PER_GROUP:(g + 1) * EXPERTS_PER_GROUP])
        grp_scores.append(m1 + m2)
    _, in_grp = _first_max_onehot(grp_scores)
    cand_sel = [_pick(in_grp, [sel_rows[g * EXPERTS_PER_GROUP + k] for g in range(N_GROUPS)])
                for k in range(EXPERTS_PER_GROUP)]
    cand_s = [_pick(in_grp, [s_rows[g * EXPERTS_PER_GROUP + k] for g in range(N_GROUPS)])
              for k in range(EXPERTS_PER_GROUP)]
    _, _, oh1, oh2 = top2(cand_sel)
    w1 = _pick(oh1, cand_s)
    w2 = _pick(oh2, cand_s)
    denom = w1 + w2
    wrow_scr[...] = jnp.zeros_like(wrow_scr)
    wrow_scr[0:1, :] = w1 / denom
    wrow_scr[1:2, :] = w2 / denom
    wcol_ref[...] = jnp.transpose(wrow_scr[...])

    ints = [jnp.full(w1.shape, k, I32) for k in range(EXPERTS_PER_GROUP)]
    grp = _pick(in_grp, ints) * EXPERTS_PER_GROUP
    idx_ref[0:1, :] = grp + _pick(oh1, ints)
    idx_ref[1:2, :] = grp + _pick(oh2, ints)

    one, zero = jnp.ones_like(w1), jnp.zeros_like(w1)
    for g in range(N_GROUPS):
        for k in range(EXPERTS_PER_GROUP):
            e = g * EXPERTS_PER_GROUP + k
            ind_scr[e:e + 1, :] = jnp.where(in_grp[g] & (oh1[k] | oh2[k]), one, zero)
    ind = ind_scr[...].astype(BF16)
    ranks = jnp.dot(ind, tri_ref[...], preferred_element_type=F32)
    cnt_ref[0] = jnp.dot(ind, jnp.ones((ind.shape[1], LANES), BF16), preferred_element_type=F32)
    for slot, oh in ((0, oh1), (1, oh2)):
        r = zero
        for g in range(N_GROUPS):
            for k in range(EXPERTS_PER_GROUP):
                e = g * EXPERTS_PER_GROUP + k
                r = r + jnp.where(in_grp[g] & oh[k], ranks[e:e + 1], zero)
        rank_ref[slot:slot + 1, :] = r.astype(I32)


def _proj_route_kernel(*refs, n_x, n_a, n_ctx_tiles):
    x_refs, a_refs = refs[:n_x], refs[n_x:n_x + n_a]
    w_ref, mod_ref, g_ref, wr_ref, bias_ref, tri_ref, o_ref = refs[n_x + n_a:n_x + n_a + 7]
    route_out_refs = refs[n_x + n_a + 7:n_x + n_a + 12]
    ind_scr, wrow_scr = refs[n_x + n_a + 12:]
    is_dec = pl.program_id(0) >= n_ctx_tiles
    h_ref, idx_ref, wcol_ref, rank_ref, cnt_ref = route_out_refs
    sub = o_ref.shape[0] // ROUTE_SPLIT

    def body(dec):
        x_ref = x_refs[(1 if dec else 0) if n_x == 2 else 0]
        a_ref = a_refs[(1 if dec else 0) if n_a == 2 else 0]
        parts = [slice(part * sub, (part + 1) * sub) for part in range(ROUTE_SPLIT)]
        for rows in parts:
            y = jnp.dot(a_ref[rows, :].astype(BF16), w_ref[...], preferred_element_type=F32)
            o_ref[rows, :] = x_ref[rows, :] + mod_ref[0, 2:3, :] * y
        for part, rows in enumerate(parts):
            _route_rows(o_ref[rows, :], g_ref, mod_ref, wr_ref, bias_ref, tri_ref, h_ref.at[rows, :],
                        idx_ref.at[:, rows], wcol_ref.at[rows, :], rank_ref.at[:, rows], cnt_ref.at[part:part + 1],
                        ind_scr, wrow_scr)

    if n_x == 1 and n_a == 1:
        body(False)
    else:
        pl.when(jnp.logical_not(is_dec))(lambda: body(False))
        pl.when(is_dec)(lambda: body(True))


def proj_residual_route(lay, x_parts, a_parts, w, mods, ffn_g, w_router_t, router_bias):
    d = x_parts[0].shape[1]
    kdim = a_parts[0].shape[1]
    tm, t = lay.tm, lay.t
    sub = tm // ROUTE_SPLIT
    tri = jnp.triu(jnp.ones((sub, sub), BF16), k=1)
    pair = jax.ShapeDtypeStruct((TOP_K, t), I32)
    pair_spec = pl.BlockSpec((TOP_K, tm), lambda i: (0, i))
    const = lambda shape: pl.BlockSpec(shape, lambda i: (0,) * len(shape))
    return pl.pallas_call(
        functools.partial(_proj_route_kernel, n_x=len(x_parts), n_a=len(a_parts), n_ctx_tiles=lay.n_ctx_tiles),
        out_shape=[jax.ShapeDtypeStruct((t, d), F32), jax.ShapeDtypeStruct((t, d // 2), jnp.uint32), pair,
                   jax.ShapeDtypeStruct((t, LANES), F32), pair,
                   jax.ShapeDtypeStruct((lay.n_tiles * ROUTE_SPLIT, N_EXPERTS, LANES), F32)],
        grid=(lay.n_tiles,),
        in_specs=lay.row_specs(x_parts, d) + lay.row_specs(a_parts, kdim) + [
            const((kdim, d)),
            pl.BlockSpec((1, N_MOD, d), lambda i: (lay.mod_row(i), 0, 0)),
            const((1, d)), const((N_EXPERTS, d)), const((N_EXPERTS, 1)), const((sub, sub)),
        ],
        out_specs=[pl.BlockSpec((tm, d), lambda i: (i, 0)), pl.BlockSpec((tm, d // 2), lambda i: (i, 0)), pair_spec,
                   pl.BlockSpec((tm, LANES), lambda i: (i, 0)), pair_spec,
                   pl.BlockSpec((ROUTE_SPLIT, N_EXPERTS, LANES), lambda i: (i, 0, 0))],
        scratch_shapes=[pltpu.VMEM((N_EXPERTS, sub), F32), pltpu.VMEM((LANES, sub), F32)],
        compiler_params=_cparams("arbitrary"),
        name="proj_residual_route",
    )(*x_parts, *a_parts, w, mods, ffn_g.reshape(1, d), w_router_t, router_bias.reshape(N_EXPERTS, 1), tri)


def dispatch_plan(idx, rank, cnt, n_sorted):
    t = idx.shape[1]
    n_tiles, tm = cnt.shape[0], t // cnt.shape[0]
    cnt_tile = cnt[:, :, 0].astype(I32)
    total = jnp.sum(cnt_tile, axis=0)
    padded = ((total + EXPERT_TILE - 1) // EXPERT_TILE) * EXPERT_TILE
    end = jnp.cumsum(padded)
    start = end - padded
    base = start[None, :] + jnp.cumsum(cnt_tile, axis=0) - cnt_tile
    idx3 = idx.reshape(TOP_K, n_tiles, tm)
    pos = rank.reshape(TOP_K, n_tiles, tm)
    for e in range(N_EXPERTS):
        pos = pos + jnp.where(idx3 == e, base[None, :, e, None], 0)
    tile_row = jnp.arange(n_sorted // EXPERT_TILE, dtype=I32) * EXPERT_TILE
    tile_expert = jnp.minimum(jnp.sum(end[None, :] <= tile_row[:, None], axis=1), N_EXPERTS - 1).astype(I32)
    n_valid = (end[-1] // EXPERT_TILE).astype(I32).reshape(1)
    return pos.reshape(TOP_K * t).astype(I32), tile_expert, n_valid


def _sc_mesh():
    return plsc.VectorSubcoreMesh(core_axis_name="c", subcore_axis_name="s")


def _sc_worker_id():
    return lax.axis_index("s") * V7X_SC_CORES + lax.axis_index("c")


def _sc_two_stage_pipeline(n_chunks, fetch_idx, load, store):
    assert n_chunks % 2 == 0
    fetch_idx(0, 0)
    load(0, 0).start()

    @pl.loop(0, n_chunks // 2)
    def _(p):
        j = 2 * p

        @pl.when(p > 0)
        def _():
            store(j - 1, 1).wait()

        fetch_idx(j + 1, 1)
        load(j + 1, 1).start()
        load(j, 0).wait()
        store(j, 0).start()
        store(j, 0).wait()

        @pl.when(j + 2 < n_chunks)
        def _():
            fetch_idx(j + 2, 0)
            load(j + 2, 0).start()

        load(j + 1, 1).wait()
        store(j + 1, 1).start()

    store(n_chunks - 1, 1).wait()


_SC_SCRATCH = lambda chunk, d, dtype: [
    pltpu.VMEM((chunk,), I32), pltpu.VMEM((chunk,), I32),
    pltpu.VMEM((chunk, d), dtype), pltpu.VMEM((chunk, d), dtype),
    pltpu.SemaphoreType.DMA, pltpu.SemaphoreType.DMA, pltpu.SemaphoreType.DMA, pltpu.SemaphoreType.DMA]


def sc_scatter_rows(src, pos, n_out_rows, src_row0, t):
    d = src.shape[1]
    n_idx = pos.shape[0]
    per_worker = n_idx // SC_WORKERS
    chunk = SC_CHUNK_ROWS
    assert n_idx % (SC_WORKERS * chunk * 2) == 0 and t % chunk == 0

    @functools.partial(
        pl.kernel, mesh=_sc_mesh(), out_type=jax.ShapeDtypeStruct((n_out_rows, d), src.dtype),
        scratch_types=_SC_SCRATCH(chunk, d, src.dtype), name="sc_scatter_rows")
    def scatter(src_hbm, pos_hbm, out_hbm, idx_a, idx_b, rows_a, rows_b, lsem_a, lsem_b, ssem_a, ssem_b):
        base = _sc_worker_id() * per_worker
        idx, rows, lsem, ssem = (idx_a, idx_b), (rows_a, rows_b), (lsem_a, lsem_b), (ssem_a, ssem_b)

        def fetch_idx(j, b):
            pltpu.sync_copy(pos_hbm.at[pl.ds(base + j * chunk, chunk)], idx[b])

        def load(j, b):
            first = src_row0 + lax.rem(base + j * chunk, t)
            return pltpu.make_async_copy(src_hbm.at[pl.ds(first, chunk)], rows[b], lsem[b])

        def store(j, b):
            return pltpu.make_async_copy(rows[b], out_hbm.at[idx[b]], ssem[b])

        _sc_two_stage_pipeline(per_worker // chunk, fetch_idx, load, store)

    return scatter(src, pos)


def sc_gather_rows(table, idx):
    _, d = table.shape
    n_idx = idx.shape[0]
    per_worker = n_idx // SC_WORKERS
    chunk = SC_CHUNK_ROWS
    assert n_idx % (SC_WORKERS * chunk * 2) == 0

    @functools.partial(
        pl.kernel, mesh=_sc_mesh(), out_type=jax.ShapeDtypeStruct((n_idx, d), table.dtype),
        scratch_types=_SC_SCRATCH(chunk, d, table.dtype), name="sc_gather_rows")
    def gather(table_hbm, idx_hbm, out_hbm, idx_a, idx_b, rows_a, rows_b, lsem_a, lsem_b, ssem_a, ssem_b):
        base = _sc_worker_id() * per_worker
        idx, rows, lsem, ssem = (idx_a, idx_b), (rows_a, rows_b), (lsem_a, lsem_b), (ssem_a, ssem_b)

        def fetch_idx(j, b):
            pltpu.sync_copy(idx_hbm.at[pl.ds(base + j * chunk, chunk)], idx[b])

        def load(j, b):
            return pltpu.make_async_copy(table_hbm.at[idx[b]], rows[b], lsem[b])

        def store(j, b):
            return pltpu.make_async_copy(rows[b], out_hbm.at[pl.ds(base + j * chunk, chunk)], ssem[b])

        _sc_two_stage_pipeline(per_worker // chunk, fetch_idx, load, store)

    return gather(table, idx)


def _experts_kernel(te_ref, nv_ref, x_ref, wg_ref, wu_ref, wd_ref, y_ref, wg_scr, wu_scr, wd_scr):
    i = pl.program_id(0)

    @pl.when(i < nv_ref[0])
    def _():
        @pl.when((i == 0) | (te_ref[i] != te_ref[jnp.maximum(i - 1, 0)]))
        def _():
            wg_scr[...] = wg_ref[0, 0].astype(BF16)
            wu_scr[...] = wu_ref[0, 0].astype(BF16)
            wd_scr[...] = wd_ref[0, 0].astype(BF16)

        x_lo, x_hi = _unpack_bf16_halves(x_ref[...])
        half = x_lo.shape[1]

        def in_proj(w_scr):
            return (jnp.dot(x_lo, w_scr[0:half, :], preferred_element_type=F32)
                    + jnp.dot(x_hi, w_scr[half:, :], preferred_element_type=F32))

        a = jax.nn.silu(in_proj(wg_scr)) * in_proj(wu_scr)
        y = jnp.dot(a.astype(BF16), wd_scr[...], preferred_element_type=F32)
        y_ref[...] = _pack_bf16_halves(y.astype(BF16))


def grouped_experts(xs, tile_expert, n_valid, wg, wu, wd, layer):
    n_rows = xs.shape[0]
    d, de = wg.shape[-2:]
    tm = EXPERT_TILE
    row_map = lambda i, te, nv: (jnp.minimum(i, nv[0] - 1), 0)
    grid_spec = pltpu.PrefetchScalarGridSpec(
        num_scalar_prefetch=2,
        grid=(n_rows // tm,),
        in_specs=[
            pl.BlockSpec((tm, d // 2), row_map),
            pl.BlockSpec((1, 1, d, de), lambda i, te, nv: (layer, te[i], 0, 0)),
            pl.BlockSpec((1, 1, d, de), lambda i, te, nv: (layer, te[i], 0, 0)),
            pl.BlockSpec((1, 1, de, d), lambda i, te, nv: (layer, te[i], 0, 0)),
        ],
        out_specs=pl.BlockSpec((tm, d // 2), row_map),
        scratch_shapes=[pltpu.VMEM((d, de), BF16), pltpu.VMEM((d, de), BF16), pltpu.VMEM((de, d), BF16)],
    )
    return pl.pallas_call(
        _experts_kernel,
        out_shape=jax.ShapeDtypeStruct((n_rows, d // 2), jnp.uint32),
        grid_spec=grid_spec,
        compiler_params=_cparams("arbitrary"),
        name="grouped_experts",
    )(tile_expert, n_valid, xs, wg, wu, wd)


def _final_combine_kernel(x_ref, y_ctx_ref, y_dec_ref, w_ref, mod_ref, fg_ref, o_ctx_ref, o_dec_ref, *, n_ctx_tiles):
    is_dec = pl.program_id(0) >= n_ctx_tiles

    def body(y_ref, o_ref):
        out = _moe_combine(x_ref[...], y_ref, w_ref, mod_ref)
        ms = jnp.mean(out * out, axis=-1, keepdims=True)
        o_ref[...] = out * lax.rsqrt(ms + NORM_EPS) * fg_ref[...]

    pl.when(jnp.logical_not(is_dec))(lambda: body(y_ctx_ref, o_ctx_ref))
    pl.when(is_dec)(lambda: body(y_dec_ref, o_dec_ref))


def final_combine(lay, x, y_parts, w_col, mods, final_g):
    d = x.shape[1]
    tm = lay.tm
    return pl.pallas_call(
        functools.partial(_final_combine_kernel, n_ctx_tiles=lay.n_ctx_tiles),
        out_shape=[jax.ShapeDtypeStruct((lay.t_ctx, d), F32), jax.ShapeDtypeStruct((lay.t_dec, d), F32)],
        grid=(lay.n_tiles,),
        in_specs=[pl.BlockSpec((tm, d), lambda i: (i, 0))] + lay.expert_out_specs(d) + [
            pl.BlockSpec((tm, LANES), lambda i: (i, 0)),
            pl.BlockSpec((1, N_MOD, d), lambda i: (lay.mod_row(i), 0, 0)),
            pl.BlockSpec((1, d), lambda i: (0, 0)),
        ],
        out_specs=lay.row_specs([None, None], d),
        compiler_params=_cparams("arbitrary"),
        name="final_combine",
    )(x, *y_parts, w_col, mods, final_g.reshape(1, d))


def group_moe(lay, x, routing, p, layer):
    d = x.shape[1]
    h, idx, w_col, rank, cnt = routing
    routing_tile = lay.t // cnt.shape[0]
    y_parts = []
    for r0, r1 in ((0, lay.t_ctx), (lay.t_ctx, lay.t)):
        n_sorted = TOP_K * (r1 - r0) + N_EXPERTS * (EXPERT_TILE - 1)
        n_sorted = -(-n_sorted // EXPERT_TILE) * EXPERT_TILE
        pos, tile_expert, n_valid = dispatch_plan(idx[:, r0:r1], rank[:, r0:r1],
                                                  cnt[r0 // routing_tile:r1 // routing_tile], n_sorted)
        xs = sc_scatter_rows(h, pos, n_sorted, r0, r1 - r0)
        ys = grouped_experts(xs, tile_expert, n_valid, p["moe_w_gate"], p["moe_w_up"], p["moe_w_down"], layer)
        y_parts.append(sc_gather_rows(ys, pos).reshape(TOP_K, r1 - r0, d // 2))
    return y_parts, w_col


def _rope_angles(n, d):
    n_rows = n // GRID_W
    row = jnp.repeat(jnp.arange(n_rows), GRID_W).astype(F32)
    col = jnp.tile(jnp.arange(GRID_W), n_rows).astype(F32)
    nf = d // 4
    freqs = jnp.power(ROPE_BASE, -jnp.arange(nf, dtype=F32) / nf)
    ang = jnp.concatenate([row[:, None] * freqs, col[:, None] * freqs], axis=-1)
    return jnp.cos(ang), jnp.sin(ang)


def kernel(x_prompt, x_sample, cache_attn_k, cache_attn_v, state_ret_fwd, state_ret_bwd, c, c_ctx, w_ada, b_ada, norm_mix_g, norm_ffn_g, final_norm_g, da_w_qkv, da_lambda_q1, da_lambda_k1, da_lambda_q2, da_lambda_k2, da_subln_g, da_w_o, ret_w_qkv, ret_w_gate_fwd, ret_w_gate_bwd, ret_decay_fwd, ret_decay_bwd, ret_w_o, w_router, router_bias, moe_w_gate, moe_w_up, moe_w_down):
    b_ctx, n_ctx, d = x_prompt.shape
    b_dec, n_dec, _ = x_sample.shape
    past = cache_attn_k.shape[2]
    n_attn = cache_attn_k.shape[1]
    depth = w_ada.shape[0]
    assert b_dec + 1 <= MOD_ROWS
    lay = Layout(b_ctx, n_ctx, b_dec, n_dec)

    cond = jnp.zeros((MOD_ROWS, d), F32).at[0].set(c_ctx).at[1:1 + b_dec].set(c)
    mods_all = ada_modulation(cond, w_ada, b_ada)

    p = {
        "norm_ffn_g": norm_ffn_g, "final_norm_g": final_norm_g,
        "w_router_t": w_router.T.astype(BF16), "router_bias": router_bias.astype(F32),
        "moe_w_gate": moe_w_gate, "moe_w_up": moe_w_up, "moe_w_down": moe_w_down,
    }
    ret_log_decay = jnp.stack([jax.nn.log_sigmoid(ret_decay_fwd.astype(F32)),
                               jax.nn.log_sigmoid(ret_decay_bwd.astype(F32))], axis=1)
    ck_all = cache_attn_k.reshape(b_dec, n_attn, past, -1)
    cv_all = cache_attn_v.reshape(b_dec, n_attn, past, -1)

    x_parts = [x_prompt.reshape(lay.t_ctx, d), x_sample.reshape(lay.t_dec, d)]
    new_k, new_v, new_sf, new_sb = [], [], [], []
    pending_moe = None

    def first_kernel(*args):
        nonlocal x_parts, pending_moe
        outs = norm_mod_matmul(lay, x_parts, *args, pending_moe=pending_moe)
        if pending_moe is not None:
            *outs, x_joint = outs
            x_parts, pending_moe = [x_joint], None
        return outs

    for i in range(depth):
        mods = mods_all[i]
        j = i // 2
        if i % 2 == 0:
            lam_init = 0.8 - 0.6 * math.exp(-0.3 * i)
            qkw = DA_HEADS * 2 * DA_HEAD_DIM
            vw = DA_HEADS * DA_V_DIM
            cos, sin = _rope_angles(n_dec, DA_HEAD_DIM)
            rope = (jnp.tile(cos, (1, 4)), jnp.concatenate([-sin, sin, -sin, sin], axis=-1))
            q, k, v, k_ctx, v_ctx = first_kernel(
                norm_mix_g[i], mods, da_w_qkv[j].astype(BF16),
                [(0, qkw, "rope64", math.log2(math.e) * DA_HEAD_DIM ** -0.5, BF16, "all"),
                 (qkw, qkw, "rope64", 1.0, BF16, "all"),
                 (2 * qkw, vw, "plain", 1.0, BF16, "all"),
                 (qkw, qkw, "plain", 1.0, F32, "ctx64"), (2 * qkw, vw, "plain", 1.0, F32, "ctx")],
                rope)
            lam_params = jnp.stack([da_lambda_q1[j], da_lambda_k1[j], da_lambda_q2[j], da_lambda_k2[j]])
            mix = [diff_attention(q, k, v, 0, b_ctx, n_ctx, lam_params, da_subln_g[j], lam_init,
                                  heads_per_step=DA_HEADS),
                   diff_attention(q, k, v, lay.t_ctx, b_dec, n_dec, lam_params, da_subln_g[j], lam_init,
                                  cache=(ck_all[:, j], cv_all[:, j]))]
            mix_w = da_w_o[j].astype(BF16)
            new_k.append(k_ctx.reshape(b_ctx, n_ctx, DA_HEADS, 2, DA_HEAD_DIM))
            new_v.append(v_ctx.reshape(b_ctx, n_ctx, DA_HEADS, DA_V_DIM))
        else:
            kd = ret_w_qkv.shape[2] // 4
            dv = 2 * kd
            w_all = jnp.concatenate([ret_w_qkv[j], ret_w_gate_fwd[j], ret_w_gate_bwd[j]], axis=-1).astype(BF16)
            q, k, v, gates = first_kernel(
                norm_mix_g[i], mods, w_all,
                [(0, kd, "rope256", 1.0, BF16, "all"),
                 (kd, kd, "rope256", (kd // RET_HEADS) ** -0.5, F32, "all"),
                 (2 * kd, dv, "plain", 1.0, BF16, "all"), (2 * kd + dv, 2 * dv, "silu", 1.0, BF16, "all")],
                _rope_angles(n_dec, kd // RET_HEADS))
            ctx_seqs = math.gcd(b_ctx, max(1, n_dec // n_ctx // 2))
            o_ctx, sf, sb = retention(q, k, v, gates, 0, b_ctx, n_ctx, ret_log_decay[j], emit_state=True,
                                      seqs_per_step=ctx_seqs)
            (o_dec,) = retention(q, k, v, gates, lay.t_ctx, b_dec, n_dec, ret_log_decay[j],
                                 state=(state_ret_fwd[:, j:j + 1], state_ret_bwd[:, j:j + 1]))
            mix, mix_w = [o_ctx, o_dec], ret_w_o[j].astype(BF16)
            new_sf.append(sf)
            new_sb.append(sb)
        x, *routing = proj_residual_route(lay, x_parts, mix, mix_w, mods, norm_ffn_g[i], p["w_router_t"],
                                          p["router_bias"])
        y_parts, w_col = group_moe(lay, x, routing, p, i)
        if i == depth - 1:
            x_parts = final_combine(lay, x, y_parts, w_col, mods, final_norm_g)
        else:
            x_parts, pending_moe = [x], (y_parts, w_col, mods)

    y_ctx, y_dec = x_parts
    return (y_ctx.reshape(b_ctx, n_ctx, d), y_dec.reshape(b_dec, n_dec, d),
            jnp.stack(new_k, axis=1), jnp.stack(new_v, axis=1),
            jnp.concatenate(new_sf, axis=1), jnp.concatenate(new_sb, axis=1))
```

```python
import functools
import math

import jax
import jax.numpy as jnp
from jax import lax
from jax.experimental import pallas as pl
from jax.experimental.pallas import tpu as pltpu
from jax.experimental.pallas import tpu_sc as plsc

F32 = jnp.float32
BF16 = jnp.bfloat16
I32 = jnp.int32

GRID_W = 64
ROPE_BASE = 10000.0
NORM_EPS = 1e-6
DA_HEADS = 8
DA_HEAD_DIM = 64
DA_V_DIM = 2 * DA_HEAD_DIM
RET_HEADS = 4
RET_CHUNK = 128
N_EXPERTS = 16
N_GROUPS = 4
EXPERTS_PER_GROUP = N_EXPERTS // N_GROUPS
TOP_K = 2
N_MOD = 6
MOD_ROWS = 16
LANES = 128

V7X_VMEM_LIMIT = 56 * 1024 * 1024
V7X_SC_CORES = 2
V7X_SC_SUBCORES = 16
SC_WORKERS = V7X_SC_CORES * V7X_SC_SUBCORES
SC_CHUNK_ROWS = 64

ROW_TILE = 512
EXPERT_TILE = 512
ROUTE_SPLIT = 2


def _cparams(*sem):
    return pltpu.CompilerParams(dimension_semantics=sem, vmem_limit_bytes=V7X_VMEM_LIMIT)


def _nt_dot(a, b):
    return lax.dot_general(a, b, (((1,), (1,)), ((), ())), preferred_element_type=F32)


def _tn_dot(a, b):
    return lax.dot_general(a, b, (((0,), (0,)), ((), ())), preferred_element_type=F32)


class Layout:
    def __init__(self, b_ctx, n_ctx, b_dec, n_dec):
        self.b_ctx, self.n_ctx, self.b_dec, self.n_dec = b_ctx, n_ctx, b_dec, n_dec
        self.t_ctx, self.t_dec = b_ctx * n_ctx, b_dec * n_dec
        self.t = self.t_ctx + self.t_dec
        self.tm = min(ROW_TILE, n_dec, self.t_ctx)
        assert self.t_ctx % self.tm == 0 and n_dec % self.tm == 0
        assert self.t_ctx % n_dec == 0 and self.t_ctx % n_ctx == 0
        self.n_ctx_tiles = self.t_ctx // self.tm
        self.n_tiles = self.t // self.tm

    def mod_row(self, i):
        r = i * self.tm
        return jnp.where(r < self.t_ctx, 0, 1 + (r - self.t_ctx) // self.n_dec)

    def part_tile(self, part, i):
        if part == 0:
            return jnp.minimum(i, self.n_ctx_tiles - 1)
        return jnp.maximum(i - self.n_ctx_tiles, 0)

    def row_specs(self, arrays, width):
        if len(arrays) == 1:
            return [pl.BlockSpec((self.tm, width), lambda i, *_: (i, 0))]
        return [pl.BlockSpec((self.tm, width), lambda i, *_, p=p: (self.part_tile(p, i), 0)) for p in (0, 1)]


def _ada_kernel(c_ref, w_ref, b_ref, o_ref):
    s = jax.nn.silu(c_ref[...]).astype(BF16)
    acc = jnp.dot(s, w_ref[0].astype(BF16), preferred_element_type=F32)
    o_ref[0] = acc + b_ref[0]


def ada_modulation(cond, w_ada, b_ada):
    depth, d, n = w_ada.shape
    tn = 1536
    out = pl.pallas_call(
        _ada_kernel,
        out_shape=jax.ShapeDtypeStruct((depth, MOD_ROWS, n), F32),
        grid=(depth, n // tn),
        in_specs=[
            pl.BlockSpec((MOD_ROWS, d), lambda l, j: (0, 0)),
            pl.BlockSpec((1, d, tn), lambda l, j: (l, 0, j)),
            pl.BlockSpec((1, 1, tn), lambda l, j: (l, 0, j)),
        ],
        out_specs=pl.BlockSpec((1, MOD_ROWS, tn), lambda l, j: (l, 0, j)),
        compiler_params=_cparams("parallel", "parallel"),
        name="ada_modulation",
    )(cond, w_ada, b_ada.reshape(depth, 1, n))
    return out.reshape(depth, MOD_ROWS, N_MOD, d)


def _modulated_norm(x, g, mod_ref, shift_idx):
    ms = jnp.mean(x * x, axis=-1, keepdims=True)
    y = x * lax.rsqrt(ms + NORM_EPS) * g
    return y * (1.0 + mod_ref[0, shift_idx + 1:shift_idx + 2, :]) + mod_ref[0, shift_idx:shift_idx + 1, :]


def _by_part(is_dec, refs, fn):
    if len(refs) == 1:
        fn(refs[0])
        return
    pl.when(jnp.logical_not(is_dec))(lambda: fn(refs[0]))
    pl.when(is_dec)(lambda: fn(refs[1]))


def _rope64(a, cos, sin_signed, first_half):
    partner = jnp.where(first_half, pltpu.roll(a, 96, 1), pltpu.roll(a, 32, 1))
    return a * cos + partner * sin_signed


def _moe_combine(x, y_ref, w_ref, mod_ref):
    w = w_ref[...]

    def expert_out(slot):
        lo, hi = _unpack_bf16_halves(y_ref[slot])
        return jnp.concatenate([lo.astype(F32), hi.astype(F32)], axis=1)

    return x + mod_ref[0, 5:6, :] * (w[:, 0:1] * expert_out(0) + w[:, 1:2] * expert_out(1))


def _nmm_kernel(*refs, n_x, pending_moe, outs, tn, n_ctx_tiles):
    x_refs = refs[:n_x]
    if pending_moe:
        y_ref, wcol_ref, prev_mod_ref = refs[n_x:n_x + 3]
        refs = refs[:n_x] + refs[n_x + 3:]
    g_ref, mod_ref, w_ref, cos_ref, sin_ref = refs[n_x:n_x + 5]
    out_refs = refs[n_x + 5:]
    is_dec = pl.program_id(0) >= n_ctx_tiles
    tm = x_refs[0].shape[0]
    segments = sorted({(col0, width) for col0, width, _, _, _ in outs})

    def rotated(a, kind):
        cos, sin = cos_ref[...], sin_ref[...]
        if kind == "rope64":
            lane = lax.broadcasted_iota(I32, (tm, LANES), 1)
            first_half = (lane & 32) == 0
            return [(c * 128, _rope64(a[:, c * 128:(c + 1) * 128], cos, sin, first_half))
                    for c in range(tn // 128)]
        pieces = []
        for c in range(tn // 256):
            x1, x2 = a[:, c * 256:c * 256 + 128], a[:, c * 256 + 128:(c + 1) * 256]
            pieces += [(c * 256, x1 * cos - x2 * sin), (c * 256 + 128, x2 * cos + x1 * sin)]
        return pieces

    def emit(dec):
        x = x_refs[(1 if dec else 0) if n_x == 2 else 0][...]
        if pending_moe:
            x = _moe_combine(x, y_ref, wcol_ref, prev_mod_ref)
            out_refs[-1][...] = x
        h = _modulated_norm(x, g_ref[...], mod_ref, 0).astype(BF16)
        for col0, width in segments:
            sinks = [(o_ref, o) for o_ref, o in zip(out_refs, outs)
                     if (o[0], o[1]) == (col0, width) and not (dec and o[4] != "all")]
            for blk in range(width // tn):
                acc = jnp.dot(h, w_ref[:, col0 + blk * tn:col0 + (blk + 1) * tn], preferred_element_type=F32)
                for o_ref, (_, _, kind, scale, rows) in sinks:
                    a = acc if scale == 1.0 else acc * scale
                    if rows == "ctx64":
                        n64 = width // 64
                        for c in range(tn // 64):
                            o_ref[pl.ds(blk * tn // 64 + c, tm, stride=n64), :] = (
                                a[:, c * 64:(c + 1) * 64].astype(o_ref.dtype))
                    elif kind == "silu":
                        o_ref[:, blk * tn:(blk + 1) * tn] = jax.nn.silu(a).astype(o_ref.dtype)
                    elif kind == "plain" or rows == "ctx" or not dec:
                        o_ref[:, blk * tn:(blk + 1) * tn] = a.astype(o_ref.dtype)
                    else:
                        for off, val in rotated(a, kind):
                            o_ref[:, blk * tn + off:blk * tn + off + 128] = val.astype(o_ref.dtype)

    pl.when(jnp.logical_not(is_dec))(lambda: emit(False))
    pl.when(is_dec)(lambda: emit(True))


def norm_mod_matmul(lay, x_parts, g, mods, w, outputs, rope_tables, pending_moe=None, tn=512):
    d = x_parts[0].shape[1]
    tm = lay.tm
    n_total = w.shape[1]
    outs, out_shapes, out_specs = [], [], []
    for col0, width, kind, scale, dtype, rows in outputs:
        assert width % tn == 0 and col0 % LANES == 0
        outs.append((col0, width, kind, float(scale), rows))
        if rows == "ctx64":
            n64 = width // 64
            out_shapes.append(jax.ShapeDtypeStruct((lay.t_ctx * n64, 64), dtype))
            out_specs.append(pl.BlockSpec((tm * n64, 64), lambda i: (lay.part_tile(0, i), 0)))
            continue
        n_rows = lay.t_ctx if rows == "ctx" else lay.t
        out_shapes.append(jax.ShapeDtypeStruct((n_rows, width), dtype))
        out_specs.append(lay.row_specs([None, None], width)[0] if rows == "ctx" else lay.row_specs([None], width)[0])
    blocks_per_seq = lay.n_dec // tm
    rope_spec = pl.BlockSpec((tm, LANES), lambda i: (lay.part_tile(1, i) % blocks_per_seq, 0))
    mod_spec = pl.BlockSpec((1, N_MOD, d), lambda i: (lay.mod_row(i), 0, 0))
    moe_specs, moe_args = [], []
    if pending_moe is not None:
        assert len(x_parts) == 1
        moe_specs = [pl.BlockSpec((TOP_K, tm, d // 2), lambda i: (0, i, 0)),
                     pl.BlockSpec((tm, LANES), lambda i: (i, 0)), mod_spec]
        moe_args = list(pending_moe)
        out_shapes.append(jax.ShapeDtypeStruct((lay.t, d), F32))
        out_specs.append(pl.BlockSpec((tm, d), lambda i: (i, 0)))
    return pl.pallas_call(
        functools.partial(_nmm_kernel, n_x=len(x_parts), pending_moe=pending_moe is not None, outs=tuple(outs),
                          tn=tn, n_ctx_tiles=lay.n_ctx_tiles),
        out_shape=out_shapes,
        grid=(lay.n_tiles,),
        in_specs=lay.row_specs(x_parts, d) + moe_specs + [
            pl.BlockSpec((1, d), lambda i: (0, 0)),
            mod_spec,
            pl.BlockSpec((d, n_total), lambda i: (0, 0), pipeline_mode=pl.Buffered(1)),
            rope_spec, rope_spec,
        ],
        out_specs=out_specs,
        compiler_params=_cparams("arbitrary"),
        name="norm_mod_matmul",
    )(*x_parts, *moe_args, g.reshape(1, d), mods, w, *rope_tables)


def _diff_lambda(lam_ref, lam_init):
    lp = lam_ref[...]
    return (jnp.exp(jnp.sum(lp[0:1] * lp[1:2], axis=-1, keepdims=True))
            - jnp.exp(jnp.sum(lp[2:3] * lp[3:4], axis=-1, keepdims=True)) + lam_init)


ONES_ROWS = 16


def _scores_t(k, q, comp):
    lane = lax.broadcasted_iota(I32, q.shape, 1)
    return _nt_dot(k, jnp.where((lane < DA_HEAD_DIM) == (comp == 0), q, jnp.zeros_like(q)))


def _softmax_values_t(scores, vts):
    dv = vts[0].shape[0] - ONES_ROWS
    maxes = [jnp.max(s, axis=0, keepdims=True) for s in scores]
    exps = [jnp.exp2(s - m).astype(BF16) for s, m in zip(scores, maxes)]
    accs = [jnp.dot(vt, e, preferred_element_type=F32) for vt, e in zip(vts, exps)]
    return [acc[0:dv] / acc[dv:dv + 1] for acc in accs]


def _diff_finish_t(parts, lam, g_col, lam_init):
    o = parts[0] - lam * parts[1]
    ms = jnp.mean(o * o, axis=0, keepdims=True)
    return jnp.transpose((o * lax.rsqrt(ms + NORM_EPS) * g_col) * (1.0 - lam_init))


def _diff_attn_kernel(lam_ref, q_ref, k_ref, v_ref, g_ref, o_ref, *, lam_init, heads_per_step):
    hw = 2 * DA_HEAD_DIM
    lam = _diff_lambda(lam_ref, lam_init)
    ones = jnp.ones((ONES_ROWS, k_ref.shape[0]), BF16)
    heads = [slice(hh * hw, (hh + 1) * hw) for hh in range(heads_per_step)]
    scores = [[_scores_t(k_ref[:, cols], q_ref[:, cols], comp) for comp in range(2)] for cols in heads]
    vts = [jnp.concatenate([jnp.transpose(v_ref[:, cols].astype(F32)).astype(BF16), ones], axis=0) for cols in heads]
    parts = _softmax_values_t([s for pair in scores for s in pair], [vt for vt in vts for _ in range(2)])
    for hh, cols in enumerate(heads):
        o_ref[:, cols] = _diff_finish_t(parts[2 * hh:2 * hh + 2], lam, g_ref[...], lam_init).astype(o_ref.dtype)


def _diff_attn_cached_kernel(lam_ref, q_ref, k_ref, v_ref, ck_ref, cv_ref, g_ref, o_ref, k_scr, vt_scr, s_scr, *,
                             lam_init, sub_rows):
    seq_len, hw = q_ref.shape
    n_sub = seq_len // sub_rows
    assert n_sub % 2 == 0
    lam = _diff_lambda(lam_ref, lam_init)
    k_scr[0:seq_len, :] = k_ref[...]
    k_scr[seq_len:, :] = ck_ref[0].astype(BF16)
    vt_scr[0:hw, 0:seq_len] = jnp.transpose(v_ref[...])
    vt_scr[0:hw, seq_len:] = jnp.transpose(cv_ref[0]).astype(BF16)
    vt_scr[hw:, :] = jnp.ones((ONES_ROWS, vt_scr.shape[1]), BF16)

    def rows_of(t):
        return pl.ds(pl.multiple_of(t * sub_rows, sub_rows), sub_rows)

    def scores(t, slot, comp):
        s_scr[slot, comp] = _scores_t(k_scr[...], q_ref[rows_of(t), :], comp)

    def stage(t_next, slot_next, t, slot):
        for comp in range(2):
            scores(t_next, slot_next, comp)
        parts = _softmax_values_t([s_scr[slot, 0], s_scr[slot, 1]], [vt_scr[...]] * 2)
        o_ref[rows_of(t), :] = _diff_finish_t(parts, lam, g_ref[...], lam_init).astype(o_ref.dtype)

    scores(0, 0, 0)
    scores(0, 0, 1)

    def body(i2, carry):
        t = 2 * i2
        stage(t + 1, 1, t, 0)
        stage(jnp.minimum(t + 2, n_sub - 1), 0, t + 1, 1)
        return carry

    lax.fori_loop(0, n_sub // 2, body, 0, unroll=2 if n_sub % 4 == 0 else 1)


def diff_attention(q, k, v, row0, batch, seq_len, lam_params, subln_g, lam_init, cache=None, heads_per_step=1,
                   sub_rows=256):
    width = q.shape[1]
    hw = 2 * DA_HEAD_DIM
    bw = heads_per_step * hw
    assert row0 % seq_len == 0 and DA_HEADS % heads_per_step == 0
    s0 = row0 // seq_len
    seq_spec = pl.BlockSpec((seq_len, bw), lambda b, h: (s0 + b, h))
    in_specs = [pl.BlockSpec((4, DA_HEAD_DIM), lambda b, h: (0, 0)), seq_spec, seq_spec, seq_spec]
    args = [lam_params, q, k, v]
    scratch = []
    if cache is None:
        body = functools.partial(_diff_attn_kernel, lam_init=lam_init, heads_per_step=heads_per_step)
    else:
        assert heads_per_step == 1
        past = cache[0].shape[1]
        cache_spec = pl.BlockSpec((1, past, hw), lambda b, h: (b, 0, h))
        in_specs += [cache_spec, cache_spec]
        args += list(cache)
        sub_rows = min(sub_rows, seq_len // 2)
        scratch = [pltpu.VMEM((seq_len + past, hw), BF16), pltpu.VMEM((hw + ONES_ROWS, seq_len + past), BF16),
                   pltpu.VMEM((2, 2, seq_len + past, sub_rows), F32)]
        body = functools.partial(_diff_attn_cached_kernel, lam_init=lam_init, sub_rows=sub_rows)
    in_specs.append(pl.BlockSpec((hw, 1), lambda b, h: (0, 0)))
    args.append(subln_g.reshape(hw, 1))
    return pl.pallas_call(
        body,
        out_shape=jax.ShapeDtypeStruct((batch * seq_len, width), BF16),
        grid=(batch, DA_HEADS // heads_per_step),
        in_specs=in_specs,
        out_specs=pl.BlockSpec((seq_len, bw), lambda b, h: (b, h)),
        scratch_shapes=scratch,
        compiler_params=_cparams("parallel", "parallel"),
        name="diff_attention",
    )(*args)


def _retention_kernel(*refs, seq_len, seqs, has_state, emit_state):
    lg_ref, q_ref, k_ref, v_ref, gf_ref, gb_ref = refs[:6]
    pos = 6
    if has_state:
        s0_refs = refs[6:8]
        pos = 8
    o_ref = refs[pos]
    pos += 1
    if emit_state:
        s_out_refs = refs[pos:pos + 2]
        pos += 2
    s_scr, o_scr = refs[pos:pos + 2]

    h = pl.program_id(1)
    c_len = RET_CHUNK
    n_chunks = seq_len // c_len
    assert n_chunks % 2 == 0
    row = lax.broadcasted_iota(I32, (c_len, c_len), 0)
    colm = lax.broadcasted_iota(I32, (c_len, c_len), 1)
    rel = (row - colm).astype(F32)
    posv = lax.broadcasted_iota(I32, (c_len, 1), 0).astype(F32)

    consts = []
    for backward in (False, True):
        lg = lg_ref[1 if backward else 0, h]
        if backward:
            intra = jnp.where(rel <= 0, jnp.exp(-rel * lg), 0.0)
            q_decay = jnp.exp((c_len - posv) * lg)
            k_decay = jnp.exp(posv * lg)
        else:
            intra = jnp.where(rel >= 0, jnp.exp(rel * lg), 0.0)
            q_decay = jnp.exp((posv + 1.0) * lg)
            k_decay = jnp.exp((c_len - 1.0 - posv) * lg)
        consts.append((intra, q_decay, k_decay, jnp.exp(jnp.zeros((1, 1), F32) + c_len * lg)))
        direction = 1 if backward else 0
        for sq in range(seqs):
            if has_state:
                s_scr[sq, direction] = s0_refs[direction][sq, 0, 0]
            else:
                s_scr[sq, direction] = jnp.zeros(s_scr.shape[2:], F32)

    def body(ci, first_touch):
        chains = [(sq, direction, pl.ds(pl.multiple_of(sq * seq_len + c * c_len, c_len), c_len))
                  for sq in range(seqs) for direction, c in ((0, ci), (1, n_chunks - 1 - ci))]
        qs = [q_ref[rows, :].astype(BF16) for _, _, rows in chains]
        ks = [k_ref[rows, :].astype(F32) for _, _, rows in chains]
        vs = [v_ref[rows, :].astype(BF16) for _, _, rows in chains]
        intras = [(_nt_dot(qb, kf.astype(BF16)) * consts[d][0]).astype(BF16)
                  for (_, d, _), qb, kf in zip(chains, qs, ks)]
        states = [s_scr[sq, d] for sq, d, _ in chains]
        outs = [jnp.dot(a, vb, preferred_element_type=F32)
                + jnp.dot(qb, s.astype(BF16), preferred_element_type=F32) * consts[d][1]
                for (_, d, _), a, qb, vb, s in zip(chains, intras, qs, vs, states)]
        for (sq, d, _), kf, vb, s in zip(chains, ks, vs, states):
            s_scr[sq, d] = consts[d][3] * s + _tn_dot((kf * consts[d][2]).astype(BF16), vb)
        centred = [o - jnp.mean(o, axis=-1, keepdims=True) for o in outs]
        scales = [lax.rsqrt(jnp.mean(oc * oc, axis=-1, keepdims=True) + NORM_EPS) for oc in centred]
        for (_, d, rows), oc, scale in zip(chains, centred, scales):
            gated = oc * scale * (gb_ref if d else gf_ref)[rows, :].astype(F32)
            if first_touch:
                o_scr[rows, :] = gated
            else:
                o_ref[rows, :] = (o_scr[rows, :] + gated).astype(o_ref.dtype)

    unroll = 8 if n_chunks % 16 == 0 else 1
    lax.fori_loop(0, n_chunks // 2, lambda ci, c: (body(ci, True), c)[1], 0, unroll=unroll)
    lax.fori_loop(n_chunks // 2, n_chunks, lambda ci, c: (body(ci, False), c)[1], 0, unroll=unroll)
    if emit_state:
        for d in range(2):
            for sq in range(seqs):
                s_out_refs[d][sq, 0, 0] = s_scr[sq, d]


def retention(q, k, v, gates, row0, batch, seq_len, log_decay, state=None, emit_state=False, seqs_per_step=1):
    dk = q.shape[1] // RET_HEADS
    dv = v.shape[1] // RET_HEADS
    seqs = seqs_per_step
    rows = seqs * seq_len
    assert row0 % rows == 0 and batch % seqs == 0
    s0 = row0 // rows
    in_specs = [
        pl.BlockSpec(memory_space=pltpu.SMEM),
        pl.BlockSpec((rows, dk), lambda b, h: (s0 + b, h)),
        pl.BlockSpec((rows, dk), lambda b, h: (s0 + b, h)),
        pl.BlockSpec((rows, dv), lambda b, h: (s0 + b, h)),
        pl.BlockSpec((rows, dv), lambda b, h: (s0 + b, h)),
        pl.BlockSpec((rows, dv), lambda b, h: (s0 + b, RET_HEADS + h)),
    ]
    args = [log_decay, q, k, v, gates, gates]
    state_spec = pl.BlockSpec((seqs, 1, 1, dk, dv), lambda b, h: (b, 0, h, 0, 0))
    if state is not None:
        in_specs += [state_spec, state_spec]
        args += list(state)
    out_shapes = [jax.ShapeDtypeStruct((batch * seq_len, RET_HEADS * dv), BF16)]
    out_specs = [pl.BlockSpec((rows, dv), lambda b, h: (b, h))]
    if emit_state:
        out_shapes += [jax.ShapeDtypeStruct((batch, 1, RET_HEADS, dk, dv), F32)] * 2
        out_specs += [state_spec, state_spec]
    return pl.pallas_call(
        functools.partial(_retention_kernel, seq_len=seq_len, seqs=seqs, has_state=state is not None,
                          emit_state=emit_state),
        out_shape=out_shapes,
        grid=(batch // seqs, RET_HEADS),
        in_specs=in_specs,
        out_specs=out_specs,
        scratch_shapes=[pltpu.VMEM((seqs, 2, dk, dv), F32), pltpu.VMEM((rows, dv), F32)],
        compiler_params=_cparams("parallel", "parallel"),
        name="retention",
    )(*args)


def _pack_bf16_halves(x):
    half = x.shape[1] // 2
    lo = lax.bitcast_convert_type(x[:, :half].astype(F32), jnp.uint32)
    hi = lax.bitcast_convert_type(x[:, half:].astype(F32), jnp.uint32)
    return (hi & jnp.uint32(0xFFFF0000)) | (lo >> 16)


def _unpack_bf16_halves(p):
    lo = lax.bitcast_convert_type(p << 16, F32).astype(BF16)
    hi = lax.bitcast_convert_type(p & jnp.uint32(0xFFFF0000), F32).astype(BF16)
    return lo, hi


def _first_max_onehot(vals):
    m = vals[0]
    for v in vals[1:]:
        m = jnp.maximum(m, v)
    onehot, taken = [], None
    for v in vals:
        hit = v == m
        if taken is None:
            onehot.append(hit)
            taken = hit
        else:
            onehot.append(hit & jnp.logical_not(taken))
            taken = taken | hit
    return m, onehot


def _pick(onehot, vals):
    out = vals[-1]
    for oh, v in zip(onehot[-2::-1], vals[-2::-1]):
        out = jnp.where(oh, v, out)
    return out


def _route_rows(x, g_ref, mod_ref, wr_ref, bias_ref, tri_ref, h_ref, idx_ref, wcol_ref, rank_ref, cnt_ref,
                ind_scr, wrow_scr):
    hb = _modulated_norm(x, g_ref[...], mod_ref, 3).astype(BF16)
    h_ref[...] = _pack_bf16_halves(hb)
    logits = _nt_dot(wr_ref[...], hb)
    s = jax.nn.sigmoid(logits)
    sel = s + bias_ref[...]
    neg_inf = jnp.full_like(sel[0:1], -jnp.inf)
    sel_rows = [sel[e:e + 1] for e in range(N_EXPERTS)]
    s_rows = [s[e:e + 1] for e in range(N_EXPERTS)]

    def top2(vals):
        m1, oh1 = _first_max_onehot(vals)
        rest = [jnp.where(o, neg_inf, v) for o, v in zip(oh1, vals)]
        m2, oh2 = _first_max_onehot(rest)
        return m1, m2, oh1, oh2

    grp_scores = []
    for g in range(N_GROUPS):
        m1, m2, _, _ = top2(sel_rows[g * EXPERTS_PER_GROUP:(g + 1) * EXPERTS_PER_GROUP])
        grp_scores.append(m1 + m2)
    _, in_grp = _first_max_onehot(grp_scores)
    cand_sel = [_pick(in_grp, [sel_rows[g * EXPERTS_PER_GROUP + k] for g in range(N_GROUPS)])
                for k in range(EXPERTS_PER_GROUP)]
    cand_s = [_pick(in_grp, [s_rows[g * EXPERTS_PER_GROUP + k] for g in range(N_GROUPS)])
              for k in range(EXPERTS_PER_GROUP)]
    _, _, oh1, oh2 = top2(cand_sel)
    w1 = _pick(oh1, cand_s)
    w2 = _pick(oh2, cand_s)
    denom = w1 + w2
    wrow_scr[...] = jnp.zeros_like(wrow_scr)
    wrow_scr[0:1, :] = w1 / denom
    wrow_scr[1:2, :] = w2 / denom
    wcol_ref[...] = jnp.transpose(wrow_scr[...])

    ints = [jnp.full(w1.shape, k, I32) for k in range(EXPERTS_PER_GROUP)]
    grp = _pick(in_grp, ints) * EXPERTS_PER_GROUP
    idx_ref[0:1, :] = grp + _pick(oh1, ints)
    idx_ref[1:2, :] = grp + _pick(oh2, ints)

    one, zero = jnp.ones_like(w1), jnp.zeros_like(w1)
    for g in range(N_GROUPS):
        for k in range(EXPERTS_PER_GROUP):
            e = g * EXPERTS_PER_GROUP + k
            ind_scr[e:e + 1, :] = jnp.where(in_grp[g] & (oh1[k] | oh2[k]), one, zero)
    ind = ind_scr[...].astype(BF16)
    ranks = jnp.dot(ind, tri_ref[...], preferred_element_type=F32)
    cnt_ref[0] = jnp.dot(ind, jnp.ones((ind.shape[1], LANES), BF16), preferred_element_type=F32)
    for slot, oh in ((0, oh1), (1, oh2)):
        r = zero
        for g in range(N_GROUPS):
            for k in range(EXPERTS_PER_GROUP):
                e = g * EXPERTS_PER_GROUP + k
                r = r + jnp.where(in_grp[g] & oh[k], ranks[e:e + 1], zero)
        rank_ref[slot:slot + 1, :] = r.astype(I32)


def _proj_route_kernel(*refs, n_x, n_a, n_ctx_tiles):
    x_refs, a_refs = refs[:n_x], refs[n_x:n_x + n_a]
    w_ref, mod_ref, g_ref, wr_ref, bias_ref, tri_ref, o_ref = refs[n_x + n_a:n_x + n_a + 7]
    route_out_refs = refs[n_x + n_a + 7:n_x + n_a + 12]
    ind_scr, wrow_scr = refs[n_x + n_a + 12:]
    is_dec = pl.program_id(0) >= n_ctx_tiles
    h_ref, idx_ref, wcol_ref, rank_ref, cnt_ref = route_out_refs
    sub = o_ref.shape[0] // ROUTE_SPLIT

    def body(dec):
        x_ref = x_refs[(1 if dec else 0) if n_x == 2 else 0]
        a_ref = a_refs[(1 if dec else 0) if n_a == 2 else 0]
        parts = [slice(part * sub, (part + 1) * sub) for part in range(ROUTE_SPLIT)]
        for rows in parts:
            y = jnp.dot(a_ref[rows, :].astype(BF16), w_ref[...], preferred_element_type=F32)
            o_ref[rows, :] = x_ref[rows, :] + mod_ref[0, 2:3, :] * y
        for part, rows in enumerate(parts):
            _route_rows(o_ref[rows, :], g_ref, mod_ref, wr_ref, bias_ref, tri_ref, h_ref.at[rows, :],
                        idx_ref.at[:, rows], wcol_ref.at[rows, :], rank_ref.at[:, rows], cnt_ref.at[part:part + 1],
                        ind_scr, wrow_scr)

    if n_x == 1 and n_a == 1:
        body(False)
    else:
        pl.when(jnp.logical_not(is_dec))(lambda: body(False))
        pl.when(is_dec)(lambda: body(True))


def proj_residual_route(lay, x_parts, a_parts, w, mods, ffn_g, w_router_t, router_bias):
    d = x_parts[0].shape[1]
    kdim = a_parts[0].shape[1]
    tm, t = lay.tm, lay.t
    sub = tm // ROUTE_SPLIT
    tri = jnp.triu(jnp.ones((sub, sub), BF16), k=1)
    pair = jax.ShapeDtypeStruct((TOP_K, t), I32)
    pair_spec = pl.BlockSpec((TOP_K, tm), lambda i: (0, i))
    const = lambda shape: pl.BlockSpec(shape, lambda i: (0,) * len(shape))
    return pl.pallas_call(
        functools.partial(_proj_route_kernel, n_x=len(x_parts), n_a=len(a_parts), n_ctx_tiles=lay.n_ctx_tiles),
        out_shape=[jax.ShapeDtypeStruct((t, d), F32), jax.ShapeDtypeStruct((t, d // 2), jnp.uint32), pair,
                   jax.ShapeDtypeStruct((t, LANES), F32), pair,
                   jax.ShapeDtypeStruct((lay.n_tiles * ROUTE_SPLIT, N_EXPERTS, LANES), F32)],
        grid=(lay.n_tiles,),
        in_specs=lay.row_specs(x_parts, d) + lay.row_specs(a_parts, kdim) + [
            const((kdim, d)),
            pl.BlockSpec((1, N_MOD, d), lambda i: (lay.mod_row(i), 0, 0)),
            const((1, d)), const((N_EXPERTS, d)), const((N_EXPERTS, 1)), const((sub, sub)),
        ],
        out_specs=[pl.BlockSpec((tm, d), lambda i: (i, 0)), pl.BlockSpec((tm, d // 2), lambda i: (i, 0)), pair_spec,
                   pl.BlockSpec((tm, LANES), lambda i: (i, 0)), pair_spec,
                   pl.BlockSpec((ROUTE_SPLIT, N_EXPERTS, LANES), lambda i: (i, 0, 0))],
        scratch_shapes=[pltpu.VMEM((N_EXPERTS, sub), F32), pltpu.VMEM((LANES, sub), F32)],
        compiler_params=_cparams("arbitrary"),
        name="proj_residual_route",
    )(*x_parts, *a_parts, w, mods, ffn_g.reshape(1, d), w_router_t, router_bias.reshape(N_EXPERTS, 1), tri)


def dispatch_plan(idx, rank, cnt, n_sorted):
    t = idx.shape[1]
    n_tiles, tm = cnt.shape[0], t // cnt.shape[0]
    cnt_tile = cnt[:, :, 0].astype(I32)
    total = jnp.sum(cnt_tile, axis=0)
    padded = ((total + EXPERT_TILE - 1) // EXPERT_TILE) * EXPERT_TILE
    end = jnp.cumsum(padded)
    start = end - padded
    base = start[None, :] + jnp.cumsum(cnt_tile, axis=0) - cnt_tile
    idx3 = idx.reshape(TOP_K, n_tiles, tm)
    pos = rank.reshape(TOP_K, n_tiles, tm)
    for e in range(N_EXPERTS):
        pos = pos + jnp.where(idx3 == e, base[None, :, e, None], 0)
    tile_row = jnp.arange(n_sorted // EXPERT_TILE, dtype=I32) * EXPERT_TILE
    tile_expert = jnp.minimum(jnp.sum(end[None, :] <= tile_row[:, None], axis=1), N_EXPERTS - 1).astype(I32)
    n_valid = (end[-1] // EXPERT_TILE).astype(I32).reshape(1)
    return pos.reshape(TOP_K * t).astype(I32), tile_expert, n_valid


def _sc_mesh():
    return plsc.VectorSubcoreMesh(core_axis_name="c", subcore_axis_name="s")


def _sc_worker_id():
    return lax.axis_index("s") * V7X_SC_CORES + lax.axis_index("c")


def _sc_two_stage_pipeline(n_chunks, fetch_idx, load, store):
    assert n_chunks % 2 == 0
    fetch_idx(0, 0)
    load(0, 0).start()

    @pl.loop(0, n_chunks // 2)
    def _(p):
        j = 2 * p

        @pl.when(p > 0)
        def _():
            store(j - 1, 1).wait()

        fetch_idx(j + 1, 1)
        load(j + 1, 1).start()
        load(j, 0).wait()
        store(j, 0).start()
        store(j, 0).wait()

        @pl.when(j + 2 < n_chunks)
        def _():
            fetch_idx(j + 2, 0)
            load(j + 2, 0).start()

        load(j + 1, 1).wait()
        store(j + 1, 1).start()

    store(n_chunks - 1, 1).wait()


_SC_SCRATCH = lambda chunk, d, dtype: [
    pltpu.VMEM((chunk,), I32), pltpu.VMEM((chunk,), I32),
    pltpu.VMEM((chunk, d), dtype), pltpu.VMEM((chunk, d), dtype),
    pltpu.SemaphoreType.DMA, pltpu.SemaphoreType.DMA, pltpu.SemaphoreType.DMA, pltpu.SemaphoreType.DMA]


def sc_scatter_rows(src, pos, n_out_rows):
    t, d = src.shape
    n_idx = pos.shape[0]
    per_worker = n_idx // SC_WORKERS
    chunk = SC_CHUNK_ROWS
    assert n_idx % (SC_WORKERS * chunk * 2) == 0 and t % chunk == 0

    @functools.partial(
        pl.kernel, mesh=_sc_mesh(), out_type=jax.ShapeDtypeStruct((n_out_rows, d), src.dtype),
        scratch_types=_SC_SCRATCH(chunk, d, src.dtype), name="sc_scatter_rows")
    def scatter(src_hbm, pos_hbm, out_hbm, idx_a, idx_b, rows_a, rows_b, lsem_a, lsem_b, ssem_a, ssem_b):
        base = _sc_worker_id() * per_worker
        idx, rows, lsem, ssem = (idx_a, idx_b), (rows_a, rows_b), (lsem_a, lsem_b), (ssem_a, ssem_b)

        def fetch_idx(j, b):
            pltpu.sync_copy(pos_hbm.at[pl.ds(base + j * chunk, chunk)], idx[b])

        def load(j, b):
            return pltpu.make_async_copy(src_hbm.at[pl.ds(lax.rem(base + j * chunk, t), chunk)], rows[b], lsem[b])

        def store(j, b):
            return pltpu.make_async_copy(rows[b], out_hbm.at[idx[b]], ssem[b])

        _sc_two_stage_pipeline(per_worker // chunk, fetch_idx, load, store)

    return scatter(src, pos)


def sc_gather_rows(table, idx):
    _, d = table.shape
    n_idx = idx.shape[0]
    per_worker = n_idx // SC_WORKERS
    chunk = SC_CHUNK_ROWS
    assert n_idx % (SC_WORKERS * chunk * 2) == 0

    @functools.partial(
        pl.kernel, mesh=_sc_mesh(), out_type=jax.ShapeDtypeStruct((n_idx, d), table.dtype),
        scratch_types=_SC_SCRATCH(chunk, d, table.dtype), name="sc_gather_rows")
    def gather(table_hbm, idx_hbm, out_hbm, idx_a, idx_b, rows_a, rows_b, lsem_a, lsem_b, ssem_a, ssem_b):
        base = _sc_worker_id() * per_worker
        idx, rows, lsem, ssem = (idx_a, idx_b), (rows_a, rows_b), (lsem_a, lsem_b), (ssem_a, ssem_b)

        def fetch_idx(j, b):
            pltpu.sync_copy(idx_hbm.at[pl.ds(base + j * chunk, chunk)], idx[b])

        def load(j, b):
            return pltpu.make_async_copy(table_hbm.at[idx[b]], rows[b], lsem[b])

        def store(j, b):
            return pltpu.make_async_copy(rows[b], out_hbm.at[pl.ds(base + j * chunk, chunk)], ssem[b])

        _sc_two_stage_pipeline(per_worker // chunk, fetch_idx, load, store)

    return gather(table, idx)


def _experts_kernel(te_ref, nv_ref, x_ref, wg_ref, wu_ref, wd_ref, y_ref, wg_scr, wu_scr, wd_scr):
    i = pl.program_id(0)

    @pl.when(i < nv_ref[0])
    def _():
        @pl.when((i == 0) | (te_ref[i] != te_ref[jnp.maximum(i - 1, 0)]))
        def _():
            wg_scr[...] = wg_ref[0, 0].astype(BF16)
            wu_scr[...] = wu_ref[0, 0].astype(BF16)
            wd_scr[...] = wd_ref[0, 0].astype(BF16)

        x_lo, x_hi = _unpack_bf16_halves(x_ref[...])
        half = x_lo.shape[1]

        def in_proj(w_scr):
            return (jnp.dot(x_lo, w_scr[0:half, :], preferred_element_type=F32)
                    + jnp.dot(x_hi, w_scr[half:, :], preferred_element_type=F32))

        a = jax.nn.silu(in_proj(wg_scr)) * in_proj(wu_scr)
        y = jnp.dot(a.astype(BF16), wd_scr[...], preferred_element_type=F32)
        y_ref[...] = _pack_bf16_halves(y.astype(BF16))


def grouped_experts(xs, tile_expert, n_valid, wg, wu, wd, layer):
    n_rows = xs.shape[0]
    d, de = wg.shape[-2:]
    tm = EXPERT_TILE
    row_map = lambda i, te, nv: (jnp.minimum(i, nv[0] - 1), 0)
    grid_spec = pltpu.PrefetchScalarGridSpec(
        num_scalar_prefetch=2,
        grid=(n_rows // tm,),
        in_specs=[
            pl.BlockSpec((tm, d // 2), row_map),
            pl.BlockSpec((1, 1, d, de), lambda i, te, nv: (layer, te[i], 0, 0)),
            pl.BlockSpec((1, 1, d, de), lambda i, te, nv: (layer, te[i], 0, 0)),
            pl.BlockSpec((1, 1, de, d), lambda i, te, nv: (layer, te[i], 0, 0)),
        ],
        out_specs=pl.BlockSpec((tm, d // 2), row_map),
        scratch_shapes=[pltpu.VMEM((d, de), BF16), pltpu.VMEM((d, de), BF16), pltpu.VMEM((de, d), BF16)],
    )
    return pl.pallas_call(
        _experts_kernel,
        out_shape=jax.ShapeDtypeStruct((n_rows, d // 2), jnp.uint32),
        grid_spec=grid_spec,
        compiler_params=_cparams("arbitrary"),
        name="grouped_experts",
    )(tile_expert, n_valid, xs, wg, wu, wd)


def _final_combine_kernel(x_ref, y_ref, w_ref, mod_ref, fg_ref, o_ctx_ref, o_dec_ref, *, n_ctx_tiles):
    out = _moe_combine(x_ref[...], y_ref, w_ref, mod_ref)
    ms = jnp.mean(out * out, axis=-1, keepdims=True)
    out = out * lax.rsqrt(ms + NORM_EPS) * fg_ref[...]

    def store(o_ref):
        o_ref[...] = out
    _by_part(pl.program_id(0) >= n_ctx_tiles, (o_ctx_ref, o_dec_ref), store)


def final_combine(lay, x, y_pair, w_col, mods, final_g):
    d = x.shape[1]
    tm = lay.tm
    return pl.pallas_call(
        functools.partial(_final_combine_kernel, n_ctx_tiles=lay.n_ctx_tiles),
        out_shape=[jax.ShapeDtypeStruct((lay.t_ctx, d), F32), jax.ShapeDtypeStruct((lay.t_dec, d), F32)],
        grid=(lay.n_tiles,),
        in_specs=[
            pl.BlockSpec((tm, d), lambda i: (i, 0)),
            pl.BlockSpec((TOP_K, tm, d // 2), lambda i: (0, i, 0)),
            pl.BlockSpec((tm, LANES), lambda i: (i, 0)),
            pl.BlockSpec((1, N_MOD, d), lambda i: (lay.mod_row(i), 0, 0)),
            pl.BlockSpec((1, d), lambda i: (0, 0)),
        ],
        out_specs=lay.row_specs([None, None], d),
        compiler_params=_cparams("arbitrary"),
        name="final_combine",
    )(x, y_pair, w_col, mods, final_g.reshape(1, d))


def group_moe(lay, x, routing, p, layer):
    t, d = x.shape
    h, idx, w_col, rank, cnt = routing
    n_sorted = TOP_K * t + N_EXPERTS * (EXPERT_TILE - 1)
    n_sorted = -(-n_sorted // EXPERT_TILE) * EXPERT_TILE
    pos, tile_expert, n_valid = dispatch_plan(idx, rank, cnt, n_sorted)
    xs = sc_scatter_rows(h, pos, n_sorted)
    ys = grouped_experts(xs, tile_expert, n_valid, p["moe_w_gate"], p["moe_w_up"], p["moe_w_down"], layer)
    return sc_gather_rows(ys, pos).reshape(TOP_K, t, d // 2), w_col


def _rope_angles(n, d):
    n_rows = n // GRID_W
    row = jnp.repeat(jnp.arange(n_rows), GRID_W).astype(F32)
    col = jnp.tile(jnp.arange(GRID_W), n_rows).astype(F32)
    nf = d // 4
    freqs = jnp.power(ROPE_BASE, -jnp.arange(nf, dtype=F32) / nf)
    ang = jnp.concatenate([row[:, None] * freqs, col[:, None] * freqs], axis=-1)
    return jnp.cos(ang), jnp.sin(ang)


def kernel(x_prompt, x_sample, cache_attn_k, cache_attn_v, state_ret_fwd, state_ret_bwd, c, c_ctx, w_ada, b_ada, norm_mix_g, norm_ffn_g, final_norm_g, da_w_qkv, da_lambda_q1, da_lambda_k1, da_lambda_q2, da_lambda_k2, da_subln_g, da_w_o, ret_w_qkv, ret_w_gate_fwd, ret_w_gate_bwd, ret_decay_fwd, ret_decay_bwd, ret_w_o, w_router, router_bias, moe_w_gate, moe_w_up, moe_w_down):
    b_ctx, n_ctx, d = x_prompt.shape
    b_dec, n_dec, _ = x_sample.shape
    past = cache_attn_k.shape[2]
    n_attn = cache_attn_k.shape[1]
    depth = w_ada.shape[0]
    assert b_dec + 1 <= MOD_ROWS
    lay = Layout(b_ctx, n_ctx, b_dec, n_dec)

    cond = jnp.zeros((MOD_ROWS, d), F32).at[0].set(c_ctx).at[1:1 + b_dec].set(c)
    mods_all = ada_modulation(cond, w_ada, b_ada)

    p = {
        "norm_ffn_g": norm_ffn_g, "final_norm_g": final_norm_g,
        "w_router_t": w_router.T.astype(BF16), "router_bias": router_bias.astype(F32),
        "moe_w_gate": moe_w_gate, "moe_w_up": moe_w_up, "moe_w_down": moe_w_down,
    }
    ret_log_decay = jnp.stack([jax.nn.log_sigmoid(ret_decay_fwd.astype(F32)),
                               jax.nn.log_sigmoid(ret_decay_bwd.astype(F32))], axis=1)
    ck_all = cache_attn_k.reshape(b_dec, n_attn, past, -1)
    cv_all = cache_attn_v.reshape(b_dec, n_attn, past, -1)

    x_parts = [x_prompt.reshape(lay.t_ctx, d), x_sample.reshape(lay.t_dec, d)]
    new_k, new_v, new_sf, new_sb = [], [], [], []
    pending_moe = None

    def first_kernel(*args):
        nonlocal x_parts, pending_moe
        outs = norm_mod_matmul(lay, x_parts, *args, pending_moe=pending_moe)
        if pending_moe is not None:
            *outs, x_joint = outs
            x_parts, pending_moe = [x_joint], None
        return outs

    for i in range(depth):
        mods = mods_all[i]
        j = i // 2
        if i % 2 == 0:
            lam_init = 0.8 - 0.6 * math.exp(-0.3 * i)
            qkw = DA_HEADS * 2 * DA_HEAD_DIM
            vw = DA_HEADS * DA_V_DIM
            cos, sin = _rope_angles(n_dec, DA_HEAD_DIM)
            rope = (jnp.tile(cos, (1, 4)), jnp.concatenate([-sin, sin, -sin, sin], axis=-1))
            q, k, v, k_ctx, v_ctx = first_kernel(
                norm_mix_g[i], mods, da_w_qkv[j].astype(BF16),
                [(0, qkw, "rope64", math.log2(math.e) * DA_HEAD_DIM ** -0.5, BF16, "all"),
                 (qkw, qkw, "rope64", 1.0, BF16, "all"),
                 (2 * qkw, vw, "plain", 1.0, BF16, "all"),
                 (qkw, qkw, "plain", 1.0, F32, "ctx64"), (2 * qkw, vw, "plain", 1.0, F32, "ctx")],
                rope)
            lam_params = jnp.stack([da_lambda_q1[j], da_lambda_k1[j], da_lambda_q2[j], da_lambda_k2[j]])
            mix = [diff_attention(q, k, v, 0, b_ctx, n_ctx, lam_params, da_subln_g[j], lam_init,
                                  heads_per_step=DA_HEADS),
                   diff_attention(q, k, v, lay.t_ctx, b_dec, n_dec, lam_params, da_subln_g[j], lam_init,
                                  cache=(ck_all[:, j], cv_all[:, j]))]
            mix_w = da_w_o[j].astype(BF16)
            new_k.append(k_ctx.reshape(b_ctx, n_ctx, DA_HEADS, 2, DA_HEAD_DIM))
            new_v.append(v_ctx.reshape(b_ctx, n_ctx, DA_HEADS, DA_V_DIM))
        else:
            kd = ret_w_qkv.shape[2] // 4
            dv = 2 * kd
            w_all = jnp.concatenate([ret_w_qkv[j], ret_w_gate_fwd[j], ret_w_gate_bwd[j]], axis=-1).astype(BF16)
            q, k, v, gates = first_kernel(
                norm_mix_g[i], mods, w_all,
                [(0, kd, "rope256", 1.0, BF16, "all"),
                 (kd, kd, "rope256", (kd // RET_HEADS) ** -0.5, F32, "all"),
                 (2 * kd, dv, "plain", 1.0, BF16, "all"), (2 * kd + dv, 2 * dv, "silu", 1.0, BF16, "all")],
                _rope_angles(n_dec, kd // RET_HEADS))
            ctx_seqs = math.gcd(b_ctx, max(1, n_dec // n_ctx // 2))
            o_ctx, sf, sb = retention(q, k, v, gates, 0, b_ctx, n_ctx, ret_log_decay[j], emit_state=True,
                                      seqs_per_step=ctx_seqs)
            (o_dec,) = retention(q, k, v, gates, lay.t_ctx, b_dec, n_dec, ret_log_decay[j],
                                 state=(state_ret_fwd[:, j:j + 1], state_ret_bwd[:, j:j + 1]))
            mix, mix_w = [o_ctx, o_dec], ret_w_o[j].astype(BF16)
            new_sf.append(sf)
            new_sb.append(sb)
        x, *routing = proj_residual_route(lay, x_parts, mix, mix_w, mods, norm_ffn_g[i], p["w_router_t"],
                                          p["router_bias"])
        y_pair, w_col = group_moe(lay, x, routing, p, i)
        if i == depth - 1:
            x_parts = final_combine(lay, x, y_pair, w_col, mods, final_norm_g)
        else:
            x_parts, pending_moe = [x], (y_pair, w_col, mods)

    y_ctx, y_dec = x_parts
    return (y_ctx.reshape(b_ctx, n_ctx, d), y_dec.reshape(b_dec, n_dec, d),
            jnp.stack(new_k, axis=1), jnp.stack(new_v, axis=1),
            jnp.concatenate(new_sf, axis=1), jnp.concatenate(new_sb, axis=1))
```

```python
import functools
import math

import jax
import jax.numpy as jnp
from jax import lax
from jax.experimental import pallas as pl
from jax.experimental.pallas import tpu as pltpu
from jax.experimental.pallas import tpu_sc as plsc

F32 = jnp.float32
BF16 = jnp.bfloat16
I32 = jnp.int32

GRID_W = 64
ROPE_BASE = 10000.0
NORM_EPS = 1e-6
DA_HEADS = 8
DA_HEAD_DIM = 64
DA_V_DIM = 2 * DA_HEAD_DIM
RET_HEADS = 4
RET_CHUNK = 128
N_EXPERTS = 16
N_GROUPS = 4
EXPERTS_PER_GROUP = N_EXPERTS // N_GROUPS
TOP_K = 2
N_MOD = 6
MOD_ROWS = 16
LANES = 128

V7X_VMEM_LIMIT = 56 * 1024 * 1024
V7X_SC_CORES = 2
V7X_SC_SUBCORES = 16
SC_WORKERS = V7X_SC_CORES * V7X_SC_SUBCORES
SC_CHUNK_ROWS = 64

ROW_TILE = 512
EXPERT_TILE = 512
ROUTE_SPLIT = 2


def _cparams(*sem):
    return pltpu.CompilerParams(dimension_semantics=sem, vmem_limit_bytes=V7X_VMEM_LIMIT)


def _nt_dot(a, b):
    return lax.dot_general(a, b, (((1,), (1,)), ((), ())), preferred_element_type=F32)


def _tn_dot(a, b):
    return lax.dot_general(a, b, (((0,), (0,)), ((), ())), preferred_element_type=F32)


class Layout:
    def __init__(self, b_ctx, n_ctx, b_dec, n_dec):
        self.b_ctx, self.n_ctx, self.b_dec, self.n_dec = b_ctx, n_ctx, b_dec, n_dec
        self.t_ctx, self.t_dec = b_ctx * n_ctx, b_dec * n_dec
        self.t = self.t_ctx + self.t_dec
        self.tm = min(ROW_TILE, n_dec, self.t_ctx)
        assert self.t_ctx % self.tm == 0 and n_dec % self.tm == 0
        assert self.t_ctx % n_dec == 0 and self.t_ctx % n_ctx == 0
        self.n_ctx_tiles = self.t_ctx // self.tm
        self.n_tiles = self.t // self.tm

    def mod_row(self, i):
        r = i * self.tm
        return jnp.where(r < self.t_ctx, 0, 1 + (r - self.t_ctx) // self.n_dec)

    def part_tile(self, part, i):
        if part == 0:
            return jnp.minimum(i, self.n_ctx_tiles - 1)
        return jnp.maximum(i - self.n_ctx_tiles, 0)

    def row_specs(self, arrays, width):
        if len(arrays) == 1:
            return [pl.BlockSpec((self.tm, width), lambda i, *_: (i, 0))]
        return [pl.BlockSpec((self.tm, width), lambda i, *_, p=p: (self.part_tile(p, i), 0)) for p in (0, 1)]


def _ada_kernel(c_ref, w_ref, b_ref, o_ref):
    s = jax.nn.silu(c_ref[...]).astype(BF16)
    acc = jnp.dot(s, w_ref[0].astype(BF16), preferred_element_type=F32)
    o_ref[0] = acc + b_ref[0]


def ada_modulation(cond, w_ada, b_ada):
    depth, d, n = w_ada.shape
    tn = 1536
    out = pl.pallas_call(
        _ada_kernel,
        out_shape=jax.ShapeDtypeStruct((depth, MOD_ROWS, n), F32),
        grid=(depth, n // tn),
        in_specs=[
            pl.BlockSpec((MOD_ROWS, d), lambda l, j: (0, 0)),
            pl.BlockSpec((1, d, tn), lambda l, j: (l, 0, j)),
            pl.BlockSpec((1, 1, tn), lambda l, j: (l, 0, j)),
        ],
        out_specs=pl.BlockSpec((1, MOD_ROWS, tn), lambda l, j: (l, 0, j)),
        compiler_params=_cparams("parallel", "parallel"),
        name="ada_modulation",
    )(cond, w_ada, b_ada.reshape(depth, 1, n))
    return out.reshape(depth, MOD_ROWS, N_MOD, d)


def _modulated_norm(x, g, mod_ref, shift_idx):
    ms = jnp.mean(x * x, axis=-1, keepdims=True)
    y = x * lax.rsqrt(ms + NORM_EPS) * g
    return y * (1.0 + mod_ref[0, shift_idx + 1:shift_idx + 2, :]) + mod_ref[0, shift_idx:shift_idx + 1, :]


def _by_part(is_dec, refs, fn):
    if len(refs) == 1:
        fn(refs[0])
        return
    pl.when(jnp.logical_not(is_dec))(lambda: fn(refs[0]))
    pl.when(is_dec)(lambda: fn(refs[1]))


def _rope64(a, cos, sin_signed, first_half):
    partner = jnp.where(first_half, pltpu.roll(a, 96, 1), pltpu.roll(a, 32, 1))
    return a * cos + partner * sin_signed


def _moe_combine(x, y_ref, w_ref, mod_ref):
    w = w_ref[...]

    def expert_out(slot):
        lo, hi = _unpack_bf16_halves(y_ref[slot])
        return jnp.concatenate([lo.astype(F32), hi.astype(F32)], axis=1)

    return x + mod_ref[0, 5:6, :] * (w[:, 0:1] * expert_out(0) + w[:, 1:2] * expert_out(1))


def _nmm_kernel(*refs, n_x, pending_moe, cast_w, outs, tn, n_ctx_tiles):
    x_refs = refs[:n_x]
    if pending_moe:
        y_ref, wcol_ref, prev_mod_ref = refs[n_x:n_x + 3]
        refs = refs[:n_x] + refs[n_x + 3:]
    g_ref, mod_ref, w_ref, cos_ref, sin_ref = refs[n_x:n_x + 5]
    n_out = len(outs) + (1 if pending_moe else 0)
    out_refs = refs[n_x + 5:n_x + 5 + n_out]
    if cast_w:
        w_f32_ref, w_ref = w_ref, refs[n_x + 5 + n_out]

        @pl.when(pl.program_id(0) == 0)
        def _():
            w_ref[...] = w_f32_ref[...].astype(BF16)
    is_dec = pl.program_id(0) >= n_ctx_tiles
    tm = x_refs[0].shape[0]
    segments = sorted({(col0, width) for col0, width, _, _, _ in outs})

    def rotated(a, kind):
        cos, sin = cos_ref[...], sin_ref[...]
        if kind == "rope64":
            lane = lax.broadcasted_iota(I32, (tm, LANES), 1)
            first_half = (lane & 32) == 0
            return [(c * 128, _rope64(a[:, c * 128:(c + 1) * 128], cos, sin, first_half))
                    for c in range(tn // 128)]
        pieces = []
        for c in range(tn // 256):
            x1, x2 = a[:, c * 256:c * 256 + 128], a[:, c * 256 + 128:(c + 1) * 256]
            pieces += [(c * 256, x1 * cos - x2 * sin), (c * 256 + 128, x2 * cos + x1 * sin)]
        return pieces

    def emit(dec):
        x = x_refs[(1 if dec else 0) if n_x == 2 else 0][...]
        if pending_moe:
            x = _moe_combine(x, y_ref, wcol_ref, prev_mod_ref)
            out_refs[-1][...] = x
        h = _modulated_norm(x, g_ref[...], mod_ref, 0).astype(BF16)
        for col0, width in segments:
            sinks = [(o_ref, o) for o_ref, o in zip(out_refs, outs)
                     if (o[0], o[1]) == (col0, width) and not (dec and o[4] != "all")]
            for blk in range(width // tn):
                acc = jnp.dot(h, w_ref[:, col0 + blk * tn:col0 + (blk + 1) * tn], preferred_element_type=F32)
                for o_ref, (_, _, kind, scale, rows) in sinks:
                    a = acc if scale == 1.0 else acc * scale
                    if rows == "ctx64":
                        n64 = width // 64
                        for c in range(tn // 64):
                            o_ref[pl.ds(blk * tn // 64 + c, tm, stride=n64), :] = (
                                a[:, c * 64:(c + 1) * 64].astype(o_ref.dtype))
                    elif kind == "silu":
                        o_ref[:, blk * tn:(blk + 1) * tn] = jax.nn.silu(a).astype(o_ref.dtype)
                    elif kind == "plain" or rows == "ctx" or not dec:
                        o_ref[:, blk * tn:(blk + 1) * tn] = a.astype(o_ref.dtype)
                    else:
                        for off, val in rotated(a, kind):
                            o_ref[:, blk * tn + off:blk * tn + off + 128] = val.astype(o_ref.dtype)

    pl.when(jnp.logical_not(is_dec))(lambda: emit(False))
    pl.when(is_dec)(lambda: emit(True))


def norm_mod_matmul(lay, x_parts, g, mods, w, outputs, rope_tables, pending_moe=None, tn=512):
    d = x_parts[0].shape[1]
    tm = lay.tm
    n_total = w.shape[1]
    outs, out_shapes, out_specs = [], [], []
    for col0, width, kind, scale, dtype, rows in outputs:
        assert width % tn == 0 and col0 % LANES == 0
        outs.append((col0, width, kind, float(scale), rows))
        if rows == "ctx64":
            n64 = width // 64
            out_shapes.append(jax.ShapeDtypeStruct((lay.t_ctx * n64, 64), dtype))
            out_specs.append(pl.BlockSpec((tm * n64, 64), lambda i: (lay.part_tile(0, i), 0)))
            continue
        n_rows = lay.t_ctx if rows == "ctx" else lay.t
        out_shapes.append(jax.ShapeDtypeStruct((n_rows, width), dtype))
        out_specs.append(lay.row_specs([None, None], width)[0] if rows == "ctx" else lay.row_specs([None], width)[0])
    blocks_per_seq = lay.n_dec // tm
    rope_spec = pl.BlockSpec((tm, LANES), lambda i: (lay.part_tile(1, i) % blocks_per_seq, 0))
    mod_spec = pl.BlockSpec((1, N_MOD, d), lambda i: (lay.mod_row(i), 0, 0))
    moe_specs, moe_args = [], []
    if pending_moe is not None:
        assert len(x_parts) == 1
        moe_specs = [pl.BlockSpec((TOP_K, tm, d // 2), lambda i: (0, i, 0)),
                     pl.BlockSpec((tm, LANES), lambda i: (i, 0)), mod_spec]
        moe_args = list(pending_moe)
        out_shapes.append(jax.ShapeDtypeStruct((lay.t, d), F32))
        out_specs.append(pl.BlockSpec((tm, d), lambda i: (i, 0)))
    return pl.pallas_call(
        functools.partial(_nmm_kernel, n_x=len(x_parts), pending_moe=pending_moe is not None,
                          cast_w=w.dtype != BF16, outs=tuple(outs), tn=tn, n_ctx_tiles=lay.n_ctx_tiles),
        out_shape=out_shapes,
        grid=(lay.n_tiles,),
        in_specs=lay.row_specs(x_parts, d) + moe_specs + [
            pl.BlockSpec((1, d), lambda i: (0, 0)),
            mod_spec,
            pl.BlockSpec((d, n_total), lambda i: (0, 0), pipeline_mode=pl.Buffered(1)),
            rope_spec, rope_spec,
        ],
        out_specs=out_specs,
        scratch_shapes=[pltpu.VMEM(w.shape, BF16)] if w.dtype != BF16 else [],
        compiler_params=_cparams("arbitrary"),
        name="norm_mod_matmul",
    )(*x_parts, *moe_args, g.reshape(1, d), mods, w, *rope_tables)


def _diff_lambda(lam_ref, lam_init):
    lp = lam_ref[...]
    return (jnp.exp(jnp.sum(lp[0:1] * lp[1:2], axis=-1, keepdims=True))
            - jnp.exp(jnp.sum(lp[2:3] * lp[3:4], axis=-1, keepdims=True)) + lam_init)


ONES_ROWS = 16


def _scores_t(k, q, comp):
    lane = lax.broadcasted_iota(I32, q.shape, 1)
    return _nt_dot(k, jnp.where((lane < DA_HEAD_DIM) == (comp == 0), q, jnp.zeros_like(q)))


def _softmax_values_t(scores, vts):
    dv = vts[0].shape[0] - ONES_ROWS
    maxes = [jnp.max(s, axis=0, keepdims=True) for s in scores]
    exps = [jnp.exp2(s - m).astype(BF16) for s, m in zip(scores, maxes)]
    accs = [jnp.dot(vt, e, preferred_element_type=F32) for vt, e in zip(vts, exps)]
    return [acc[0:dv] / acc[dv:dv + 1] for acc in accs]


def _diff_finish_t(parts, lam, g_col, lam_init):
    o = parts[0] - lam * parts[1]
    ms = jnp.mean(o * o, axis=0, keepdims=True)
    return jnp.transpose((o * lax.rsqrt(ms + NORM_EPS) * g_col) * (1.0 - lam_init))


def _diff_attn_kernel(lam_ref, q_ref, k_ref, v_ref, g_ref, o_ref, *, lam_init, heads_per_step):
    hw = 2 * DA_HEAD_DIM
    lam = _diff_lambda(lam_ref, lam_init)
    ones = jnp.ones((ONES_ROWS, k_ref.shape[0]), BF16)
    heads = [slice(hh * hw, (hh + 1) * hw) for hh in range(heads_per_step)]
    scores = [[_scores_t(k_ref[:, cols], q_ref[:, cols], comp) for comp in range(2)] for cols in heads]
    vts = [jnp.concatenate([jnp.transpose(v_ref[:, cols].astype(F32)).astype(BF16), ones], axis=0) for cols in heads]
    parts = _softmax_values_t([s for pair in scores for s in pair], [vt for vt in vts for _ in range(2)])
    for hh, cols in enumerate(heads):
        o_ref[:, cols] = _diff_finish_t(parts[2 * hh:2 * hh + 2], lam, g_ref[...], lam_init).astype(o_ref.dtype)


def _diff_attn_cached_kernel(lam_ref, q_ref, k_ref, v_ref, ck_ref, cv_ref, g_ref, o_ref, k_scr, vt_scr, s_scr, *,
                             lam_init, sub_rows):
    seq_len, hw = q_ref.shape
    n_sub = seq_len // sub_rows
    assert n_sub % 2 == 0
    lam = _diff_lambda(lam_ref, lam_init)
    k_scr[0:seq_len, :] = k_ref[...]
    k_scr[seq_len:, :] = ck_ref[0].astype(BF16)
    vt_scr[0:hw, 0:seq_len] = jnp.transpose(v_ref[...])
    vt_scr[0:hw, seq_len:] = jnp.transpose(cv_ref[0]).astype(BF16)
    vt_scr[hw:, :] = jnp.ones((ONES_ROWS, vt_scr.shape[1]), BF16)

    def rows_of(t):
        return pl.ds(pl.multiple_of(t * sub_rows, sub_rows), sub_rows)

    def scores(t, slot, comp):
        s_scr[slot, comp] = _scores_t(k_scr[...], q_ref[rows_of(t), :], comp)

    def stage(t_next, slot_next, t, slot):
        for comp in range(2):
            scores(t_next, slot_next, comp)
        parts = _softmax_values_t([s_scr[slot, 0], s_scr[slot, 1]], [vt_scr[...]] * 2)
        o_ref[rows_of(t), :] = _diff_finish_t(parts, lam, g_ref[...], lam_init).astype(o_ref.dtype)

    scores(0, 0, 0)
    scores(0, 0, 1)

    def body(i2, carry):
        t = 2 * i2
        stage(t + 1, 1, t, 0)
        stage(jnp.minimum(t + 2, n_sub - 1), 0, t + 1, 1)
        return carry

    lax.fori_loop(0, n_sub // 2, body, 0, unroll=2 if n_sub % 4 == 0 else 1)


def diff_attention(q, k, v, row0, batch, seq_len, lam_params, subln_g, lam_init, cache=None, heads_per_step=1,
                   sub_rows=256):
    width = q.shape[1]
    hw = 2 * DA_HEAD_DIM
    bw = heads_per_step * hw
    assert row0 % seq_len == 0 and DA_HEADS % heads_per_step == 0
    s0 = row0 // seq_len
    seq_spec = pl.BlockSpec((seq_len, bw), lambda b, h: (s0 + b, h))
    in_specs = [pl.BlockSpec((4, DA_HEAD_DIM), lambda b, h: (0, 0)), seq_spec, seq_spec, seq_spec]
    args = [lam_params, q, k, v]
    scratch = []
    if cache is None:
        body = functools.partial(_diff_attn_kernel, lam_init=lam_init, heads_per_step=heads_per_step)
    else:
        assert heads_per_step == 1
        past = cache[0].shape[1]
        cache_spec = pl.BlockSpec((1, past, hw), lambda b, h: (b, 0, h))
        in_specs += [cache_spec, cache_spec]
        args += list(cache)
        sub_rows = min(sub_rows, seq_len // 2)
        scratch = [pltpu.VMEM((seq_len + past, hw), BF16), pltpu.VMEM((hw + ONES_ROWS, seq_len + past), BF16),
                   pltpu.VMEM((2, 2, seq_len + past, sub_rows), F32)]
        body = functools.partial(_diff_attn_cached_kernel, lam_init=lam_init, sub_rows=sub_rows)
    in_specs.append(pl.BlockSpec((hw, 1), lambda b, h: (0, 0)))
    args.append(subln_g.reshape(hw, 1))
    return pl.pallas_call(
        body,
        out_shape=jax.ShapeDtypeStruct((batch * seq_len, width), BF16),
        grid=(batch, DA_HEADS // heads_per_step),
        in_specs=in_specs,
        out_specs=pl.BlockSpec((seq_len, bw), lambda b, h: (b, h)),
        scratch_shapes=scratch,
        compiler_params=_cparams("parallel", "parallel"),
        name="diff_attention",
    )(*args)


def _retention_kernel(*refs, seq_len, seqs, has_state, emit_state):
    lg_ref, q_ref, k_ref, v_ref, gf_ref, gb_ref = refs[:6]
    pos = 6
    if has_state:
        s0_refs = refs[6:8]
        pos = 8
    o_ref = refs[pos]
    pos += 1
    if emit_state:
        s_out_refs = refs[pos:pos + 2]
        pos += 2
    s_scr, o_scr = refs[pos:pos + 2]

    h = pl.program_id(1)
    c_len = RET_CHUNK
    n_chunks = seq_len // c_len
    assert n_chunks % 2 == 0
    row = lax.broadcasted_iota(I32, (c_len, c_len), 0)
    colm = lax.broadcasted_iota(I32, (c_len, c_len), 1)
    rel = (row - colm).astype(F32)
    posv = lax.broadcasted_iota(I32, (c_len, 1), 0).astype(F32)

    consts = []
    for backward in (False, True):
        lg = lg_ref[1 if backward else 0, h]
        if backward:
            intra = jnp.where(rel <= 0, jnp.exp(-rel * lg), 0.0)
            q_decay = jnp.exp((c_len - posv) * lg)
            k_decay = jnp.exp(posv * lg)
        else:
            intra = jnp.where(rel >= 0, jnp.exp(rel * lg), 0.0)
            q_decay = jnp.exp((posv + 1.0) * lg)
            k_decay = jnp.exp((c_len - 1.0 - posv) * lg)
        consts.append((intra, q_decay, k_decay, jnp.exp(jnp.zeros((1, 1), F32) + c_len * lg)))
        direction = 1 if backward else 0
        for sq in range(seqs):
            if has_state:
                s_scr[sq, direction] = s0_refs[direction][sq, 0, 0]
            else:
                s_scr[sq, direction] = jnp.zeros(s_scr.shape[2:], F32)

    def body(ci, first_touch):
        chains = [(sq, direction, pl.ds(pl.multiple_of(sq * seq_len + c * c_len, c_len), c_len))
                  for sq in range(seqs) for direction, c in ((0, ci), (1, n_chunks - 1 - ci))]
        qs = [q_ref[rows, :].astype(BF16) for _, _, rows in chains]
        ks = [k_ref[rows, :].astype(F32) for _, _, rows in chains]
        vs = [v_ref[rows, :].astype(BF16) for _, _, rows in chains]
        intras = [(_nt_dot(qb, kf.astype(BF16)) * consts[d][0]).astype(BF16)
                  for (_, d, _), qb, kf in zip(chains, qs, ks)]
        states = [s_scr[sq, d] for sq, d, _ in chains]
        outs = [jnp.dot(a, vb, preferred_element_type=F32)
                + jnp.dot(qb, s.astype(BF16), preferred_element_type=F32) * consts[d][1]
                for (_, d, _), a, qb, vb, s in zip(chains, intras, qs, vs, states)]
        for (sq, d, _), kf, vb, s in zip(chains, ks, vs, states):
            s_scr[sq, d] = consts[d][3] * s + _tn_dot((kf * consts[d][2]).astype(BF16), vb)
        centred = [o - jnp.mean(o, axis=-1, keepdims=True) for o in outs]
        scales = [lax.rsqrt(jnp.mean(oc * oc, axis=-1, keepdims=True) + NORM_EPS) for oc in centred]
        for (_, d, rows), oc, scale in zip(chains, centred, scales):
            gated = oc * scale * (gb_ref if d else gf_ref)[rows, :].astype(F32)
            if first_touch:
                o_scr[rows, :] = gated
            else:
                o_ref[rows, :] = (o_scr[rows, :] + gated).astype(o_ref.dtype)

    unroll = 8 if n_chunks % 16 == 0 else 1
    lax.fori_loop(0, n_chunks // 2, lambda ci, c: (body(ci, True), c)[1], 0, unroll=unroll)
    lax.fori_loop(n_chunks // 2, n_chunks, lambda ci, c: (body(ci, False), c)[1], 0, unroll=unroll)
    if emit_state:
        for d in range(2):
            for sq in range(seqs):
                s_out_refs[d][sq, 0, 0] = s_scr[sq, d]


def retention(q, k, v, gates, row0, batch, seq_len, log_decay, state=None, emit_state=False, seqs_per_step=1):
    dk = q.shape[1] // RET_HEADS
    dv = v.shape[1] // RET_HEADS
    seqs = seqs_per_step
    rows = seqs * seq_len
    assert row0 % rows == 0 and batch % seqs == 0
    s0 = row0 // rows
    in_specs = [
        pl.BlockSpec(memory_space=pltpu.SMEM),
        pl.BlockSpec((rows, dk), lambda b, h: (s0 + b, h)),
        pl.BlockSpec((rows, dk), lambda b, h: (s0 + b, h)),
        pl.BlockSpec((rows, dv), lambda b, h: (s0 + b, h)),
        pl.BlockSpec((rows, dv), lambda b, h: (s0 + b, h)),
        pl.BlockSpec((rows, dv), lambda b, h: (s0 + b, RET_HEADS + h)),
    ]
    args = [log_decay, q, k, v, gates, gates]
    state_spec = pl.BlockSpec((seqs, 1, 1, dk, dv), lambda b, h: (b, 0, h, 0, 0))
    if state is not None:
        in_specs += [state_spec, state_spec]
        args += list(state)
    out_shapes = [jax.ShapeDtypeStruct((batch * seq_len, RET_HEADS * dv), BF16)]
    out_specs = [pl.BlockSpec((rows, dv), lambda b, h: (b, h))]
    if emit_state:
        out_shapes += [jax.ShapeDtypeStruct((batch, 1, RET_HEADS, dk, dv), F32)] * 2
        out_specs += [state_spec, state_spec]
    return pl.pallas_call(
        functools.partial(_retention_kernel, seq_len=seq_len, seqs=seqs, has_state=state is not None,
                          emit_state=emit_state),
        out_shape=out_shapes,
        grid=(batch // seqs, RET_HEADS),
        in_specs=in_specs,
        out_specs=out_specs,
        scratch_shapes=[pltpu.VMEM((seqs, 2, dk, dv), F32), pltpu.VMEM((rows, dv), F32)],
        compiler_params=_cparams("parallel", "parallel"),
        name="retention",
    )(*args)


def _pack_bf16_halves(x):
    half = x.shape[1] // 2
    lo = lax.bitcast_convert_type(x[:, :half].astype(F32), jnp.uint32)
    hi = lax.bitcast_convert_type(x[:, half:].astype(F32), jnp.uint32)
    return (hi & jnp.uint32(0xFFFF0000)) | (lo >> 16)


def _unpack_bf16_halves(p):
    lo = lax.bitcast_convert_type(p << 16, F32).astype(BF16)
    hi = lax.bitcast_convert_type(p & jnp.uint32(0xFFFF0000), F32).astype(BF16)
    return lo, hi


def _first_max_onehot(vals):
    m = vals[0]
    for v in vals[1:]:
        m = jnp.maximum(m, v)
    onehot, taken = [], None
    for v in vals:
        hit = v == m
        if taken is None:
            onehot.append(hit)
            taken = hit
        else:
            onehot.append(hit & jnp.logical_not(taken))
            taken = taken | hit
    return m, onehot


def _pick(onehot, vals):
    out = vals[-1]
    for oh, v in zip(onehot[-2::-1], vals[-2::-1]):
        out = jnp.where(oh, v, out)
    return out


def _route_rows(x, g_ref, mod_ref, wr_ref, bias_ref, tri_ref, h_ref, idx_ref, wcol_ref, rank_ref, cnt_ref,
                ind_scr, wrow_scr):
    hb = _modulated_norm(x, g_ref[...], mod_ref, 3).astype(BF16)
    h_ref[...] = _pack_bf16_halves(hb)
    logits = _nt_dot(wr_ref[...], hb)
    s = jax.nn.sigmoid(logits)
    sel = s + bias_ref[...]
    neg_inf = jnp.full_like(sel[0:1], -jnp.inf)
    sel_rows = [sel[e:e + 1] for e in range(N_EXPERTS)]
    s_rows = [s[e:e + 1] for e in range(N_EXPERTS)]

    def top2(vals):
        m1, oh1 = _first_max_onehot(vals)
        rest = [jnp.where(o, neg_inf, v) for o, v in zip(oh1, vals)]
        m2, oh2 = _first_max_onehot(rest)
        return m1, m2, oh1, oh2

    grp_scores = []
    for g in range(N_GROUPS):
        m1, m2, _, _ = top2(sel_rows[g * EXPERTS_PER_GROUP:(g + 1) * EXPERTS_PER_GROUP])
        grp_scores.append(m1 + m2)
    _, in_grp = _first_max_onehot(grp_scores)
    cand_sel = [_pick(in_grp, [sel_rows[g * EXPERTS_PER_GROUP + k] for g in range(N_GROUPS)])
                for k in range(EXPERTS_PER_GROUP)]
    cand_s = [_pick(in_grp, [s_rows[g * EXPERTS_PER_GROUP + k] for g in range(N_GROUPS)])
              for k in range(EXPERTS_PER_GROUP)]
    _, _, oh1, oh2 = top2(cand_sel)
    w1 = _pick(oh1, cand_s)
    w2 = _pick(oh2, cand_s)
    denom = w1 + w2
    wrow_scr[...] = jnp.zeros_like(wrow_scr)
    wrow_scr[0:1, :] = w1 / denom
    wrow_scr[1:2, :] = w2 / denom
    wcol_ref[...] = jnp.transpose(wrow_scr[...])

    ints = [jnp.full(w1.shape, k, I32) for k in range(EXPERTS_PER_GROUP)]
    grp = _pick(in_grp, ints) * EXPERTS_PER_GROUP
    idx_ref[0:1, :] = grp + _pick(oh1, ints)
    idx_ref[1:2, :] = grp + _pick(oh2, ints)

    one, zero = jnp.ones_like(w1), jnp.zeros_like(w1)
    for g in range(N_GROUPS):
        for k in range(EXPERTS_PER_GROUP):
            e = g * EXPERTS_PER_GROUP + k
            ind_scr[e:e + 1, :] = jnp.where(in_grp[g] & (oh1[k] | oh2[k]), one, zero)
    ind = ind_scr[...].astype(BF16)
    ranks = jnp.dot(ind, tri_ref[...], preferred_element_type=F32)
    cnt_ref[0] = jnp.dot(ind, jnp.ones((ind.shape[1], LANES), BF16), preferred_element_type=F32)
    for slot, oh in ((0, oh1), (1, oh2)):
        r = zero
        for g in range(N_GROUPS):
            for k in range(EXPERTS_PER_GROUP):
                e = g * EXPERTS_PER_GROUP + k
                r = r + jnp.where(in_grp[g] & oh[k], ranks[e:e + 1], zero)
        rank_ref[slot:slot + 1, :] = r.astype(I32)


def _proj_route_kernel(*refs, n_x, n_a, n_ctx_tiles):
    x_refs, a_refs = refs[:n_x], refs[n_x:n_x + n_a]
    w_ref, mod_ref, g_ref, wr_ref, bias_ref, tri_ref, o_ref = refs[n_x + n_a:n_x + n_a + 7]
    route_out_refs = refs[n_x + n_a + 7:n_x + n_a + 12]
    ind_scr, wrow_scr, w_scr = refs[n_x + n_a + 12:]
    is_dec = pl.program_id(0) >= n_ctx_tiles

    @pl.when(pl.program_id(0) == 0)
    def _():
        w_scr[...] = w_ref[...].astype(BF16)
    h_ref, idx_ref, wcol_ref, rank_ref, cnt_ref = route_out_refs
    sub = o_ref.shape[0] // ROUTE_SPLIT

    def body(dec):
        x_ref = x_refs[(1 if dec else 0) if n_x == 2 else 0]
        a_ref = a_refs[(1 if dec else 0) if n_a == 2 else 0]
        parts = [slice(part * sub, (part + 1) * sub) for part in range(ROUTE_SPLIT)]
        for rows in parts:
            y = jnp.dot(a_ref[rows, :].astype(BF16), w_scr[...], preferred_element_type=F32)
            o_ref[rows, :] = x_ref[rows, :] + mod_ref[0, 2:3, :] * y
        for part, rows in enumerate(parts):
            _route_rows(o_ref[rows, :], g_ref, mod_ref, wr_ref, bias_ref, tri_ref, h_ref.at[rows, :],
                        idx_ref.at[:, rows], wcol_ref.at[rows, :], rank_ref.at[:, rows], cnt_ref.at[part:part + 1],
                        ind_scr, wrow_scr)

    if n_x == 1 and n_a == 1:
        body(False)
    else:
        pl.when(jnp.logical_not(is_dec))(lambda: body(False))
        pl.when(is_dec)(lambda: body(True))


def proj_residual_route(lay, x_parts, a_parts, w, mods, ffn_g, w_router_t, router_bias):
    d = x_parts[0].shape[1]
    kdim = a_parts[0].shape[1]
    tm, t = lay.tm, lay.t
    sub = tm // ROUTE_SPLIT
    tri = jnp.triu(jnp.ones((sub, sub), BF16), k=1)
    pair = jax.ShapeDtypeStruct((TOP_K, t), I32)
    pair_spec = pl.BlockSpec((TOP_K, tm), lambda i: (0, i))
    const = lambda shape: pl.BlockSpec(shape, lambda i: (0,) * len(shape))
    return pl.pallas_call(
        functools.partial(_proj_route_kernel, n_x=len(x_parts), n_a=len(a_parts), n_ctx_tiles=lay.n_ctx_tiles),
        out_shape=[jax.ShapeDtypeStruct((t, d), F32), jax.ShapeDtypeStruct((t, d // 2), jnp.uint32), pair,
                   jax.ShapeDtypeStruct((t, LANES), F32), pair,
                   jax.ShapeDtypeStruct((lay.n_tiles * ROUTE_SPLIT, N_EXPERTS, LANES), F32)],
        grid=(lay.n_tiles,),
        in_specs=lay.row_specs(x_parts, d) + lay.row_specs(a_parts, kdim) + [
            pl.BlockSpec((kdim, d), lambda i: (0, 0), pipeline_mode=pl.Buffered(1)),
            pl.BlockSpec((1, N_MOD, d), lambda i: (lay.mod_row(i), 0, 0)),
            const((1, d)), const((N_EXPERTS, d)), const((N_EXPERTS, 1)), const((sub, sub)),
        ],
        out_specs=[pl.BlockSpec((tm, d), lambda i: (i, 0)), pl.BlockSpec((tm, d // 2), lambda i: (i, 0)), pair_spec,
                   pl.BlockSpec((tm, LANES), lambda i: (i, 0)), pair_spec,
                   pl.BlockSpec((ROUTE_SPLIT, N_EXPERTS, LANES), lambda i: (i, 0, 0))],
        scratch_shapes=[pltpu.VMEM((N_EXPERTS, sub), F32), pltpu.VMEM((LANES, sub), F32),
                        pltpu.VMEM((kdim, d), BF16)],
        compiler_params=_cparams("arbitrary"),
        name="proj_residual_route",
    )(*x_parts, *a_parts, w, mods, ffn_g.reshape(1, d), w_router_t, router_bias.reshape(N_EXPERTS, 1), tri)


def dispatch_plan(idx, rank, cnt, n_sorted):
    t = idx.shape[1]
    n_tiles, tm = cnt.shape[0], t // cnt.shape[0]
    cnt_tile = cnt[:, :, 0].astype(I32)
    total = jnp.sum(cnt_tile, axis=0)
    padded = ((total + EXPERT_TILE - 1) // EXPERT_TILE) * EXPERT_TILE
    end = jnp.cumsum(padded)
    start = end - padded
    base = start[None, :] + jnp.cumsum(cnt_tile, axis=0) - cnt_tile
    idx3 = idx.reshape(TOP_K, n_tiles, tm)
    pos = rank.reshape(TOP_K, n_tiles, tm)
    for e in range(N_EXPERTS):
        pos = pos + jnp.where(idx3 == e, base[None, :, e, None], 0)
    tile_row = jnp.arange(n_sorted // EXPERT_TILE, dtype=I32) * EXPERT_TILE
    tile_expert = jnp.minimum(jnp.sum(end[None, :] <= tile_row[:, None], axis=1), N_EXPERTS - 1).astype(I32)
    n_valid = (end[-1] // EXPERT_TILE).astype(I32).reshape(1)
    return pos.reshape(TOP_K * t).astype(I32), tile_expert, n_valid


def _sc_mesh():
    return plsc.VectorSubcoreMesh(core_axis_name="c", subcore_axis_name="s")


def _sc_worker_id():
    return lax.axis_index("s") * V7X_SC_CORES + lax.axis_index("c")


def _sc_two_stage_pipeline(n_chunks, fetch_idx, load, store):
    assert n_chunks % 2 == 0
    fetch_idx(0, 0)
    load(0, 0).start()

    @pl.loop(0, n_chunks // 2)
    def _(p):
        j = 2 * p

        @pl.when(p > 0)
        def _():
            store(j - 1, 1).wait()

        fetch_idx(j + 1, 1)
        load(j + 1, 1).start()
        load(j, 0).wait()
        store(j, 0).start()
        store(j, 0).wait()

        @pl.when(j + 2 < n_chunks)
        def _():
            fetch_idx(j + 2, 0)
            load(j + 2, 0).start()

        load(j + 1, 1).wait()
        store(j + 1, 1).start()

    store(n_chunks - 1, 1).wait()


_SC_SCRATCH = lambda chunk, d, dtype: [
    pltpu.VMEM((chunk,), I32), pltpu.VMEM((chunk,), I32),
    pltpu.VMEM((chunk, d), dtype), pltpu.VMEM((chunk, d), dtype),
    pltpu.SemaphoreType.DMA, pltpu.SemaphoreType.DMA, pltpu.SemaphoreType.DMA, pltpu.SemaphoreType.DMA]


def sc_scatter_rows(src, pos, n_out_rows):
    t, d = src.shape
    n_idx = pos.shape[0]
    per_worker = n_idx // SC_WORKERS
    chunk = SC_CHUNK_ROWS
    assert n_idx % (SC_WORKERS * chunk * 2) == 0 and t % chunk == 0

    @functools.partial(
        pl.kernel, mesh=_sc_mesh(), out_type=jax.ShapeDtypeStruct((n_out_rows, d), src.dtype),
        scratch_types=_SC_SCRATCH(chunk, d, src.dtype), name="sc_scatter_rows")
    def scatter(src_hbm, pos_hbm, out_hbm, idx_a, idx_b, rows_a, rows_b, lsem_a, lsem_b, ssem_a, ssem_b):
        base = _sc_worker_id() * per_worker
        idx, rows, lsem, ssem = (idx_a, idx_b), (rows_a, rows_b), (lsem_a, lsem_b), (ssem_a, ssem_b)

        def fetch_idx(j, b):
            pltpu.sync_copy(pos_hbm.at[pl.ds(base + j * chunk, chunk)], idx[b])

        def load(j, b):
            return pltpu.make_async_copy(src_hbm.at[pl.ds(lax.rem(base + j * chunk, t), chunk)], rows[b], lsem[b])

        def store(j, b):
            return pltpu.make_async_copy(rows[b], out_hbm.at[idx[b]], ssem[b])

        _sc_two_stage_pipeline(per_worker // chunk, fetch_idx, load, store)

    return scatter(src, pos)


def sc_gather_rows(table, idx):
    _, d = table.shape
    n_idx = idx.shape[0]
    per_worker = n_idx // SC_WORKERS
    chunk = SC_CHUNK_ROWS
    assert n_idx % (SC_WORKERS * chunk * 2) == 0

    @functools.partial(
        pl.kernel, mesh=_sc_mesh(), out_type=jax.ShapeDtypeStruct((n_idx, d), table.dtype),
        scratch_types=_SC_SCRATCH(chunk, d, table.dtype), name="sc_gather_rows")
    def gather(table_hbm, idx_hbm, out_hbm, idx_a, idx_b, rows_a, rows_b, lsem_a, lsem_b, ssem_a, ssem_b):
        base = _sc_worker_id() * per_worker
        idx, rows, lsem, ssem = (idx_a, idx_b), (rows_a, rows_b), (lsem_a, lsem_b), (ssem_a, ssem_b)

        def fetch_idx(j, b):
            pltpu.sync_copy(idx_hbm.at[pl.ds(base + j * chunk, chunk)], idx[b])

        def load(j, b):
            return pltpu.make_async_copy(table_hbm.at[idx[b]], rows[b], lsem[b])

        def store(j, b):
            return pltpu.make_async_copy(rows[b], out_hbm.at[pl.ds(base + j * chunk, chunk)], ssem[b])

        _sc_two_stage_pipeline(per_worker // chunk, fetch_idx, load, store)

    return gather(table, idx)


def _experts_kernel(te_ref, nv_ref, x_ref, wg_ref, wu_ref, wd_ref, y_ref, wg_scr, wu_scr, wd_scr):
    i = pl.program_id(0)

    @pl.when(i < nv_ref[0])
    def _():
        @pl.when((i == 0) | (te_ref[i] != te_ref[jnp.maximum(i - 1, 0)]))
        def _():
            wg_scr[...] = wg_ref[0, 0].astype(BF16)
            wu_scr[...] = wu_ref[0, 0].astype(BF16)
            wd_scr[...] = wd_ref[0, 0].astype(BF16)

        x_lo, x_hi = _unpack_bf16_halves(x_ref[...])
        half = x_lo.shape[1]

        def in_proj(w_scr):
            return (jnp.dot(x_lo, w_scr[0:half, :], preferred_element_type=F32)
                    + jnp.dot(x_hi, w_scr[half:, :], preferred_element_type=F32))

        a = jax.nn.silu(in_proj(wg_scr)) * in_proj(wu_scr)
        y = jnp.dot(a.astype(BF16), wd_scr[...], preferred_element_type=F32)
        y_ref[...] = _pack_bf16_halves(y.astype(BF16))


def grouped_experts(xs, tile_expert, n_valid, wg, wu, wd, layer):
    n_rows = xs.shape[0]
    d, de = wg.shape[-2:]
    tm = EXPERT_TILE
    row_map = lambda i, te, nv: (jnp.minimum(i, nv[0] - 1), 0)
    grid_spec = pltpu.PrefetchScalarGridSpec(
        num_scalar_prefetch=2,
        grid=(n_rows // tm,),
        in_specs=[
            pl.BlockSpec((tm, d // 2), row_map),
            pl.BlockSpec((1, 1, d, de), lambda i, te, nv: (layer, te[i], 0, 0)),
            pl.BlockSpec((1, 1, d, de), lambda i, te, nv: (layer, te[i], 0, 0)),
            pl.BlockSpec((1, 1, de, d), lambda i, te, nv: (layer, te[i], 0, 0)),
        ],
        out_specs=pl.BlockSpec((tm, d // 2), row_map),
        scratch_shapes=[pltpu.VMEM((d, de), BF16), pltpu.VMEM((d, de), BF16), pltpu.VMEM((de, d), BF16)],
    )
    return pl.pallas_call(
        _experts_kernel,
        out_shape=jax.ShapeDtypeStruct((n_rows, d // 2), jnp.uint32),
        grid_spec=grid_spec,
        compiler_params=_cparams("arbitrary"),
        name="grouped_experts",
    )(tile_expert, n_valid, xs, wg, wu, wd)


def _final_combine_kernel(x_ref, y_ref, w_ref, mod_ref, fg_ref, o_ctx_ref, o_dec_ref, *, n_ctx_tiles):
    out = _moe_combine(x_ref[...], y_ref, w_ref, mod_ref)
    ms = jnp.mean(out * out, axis=-1, keepdims=True)
    out = out * lax.rsqrt(ms + NORM_EPS) * fg_ref[...]

    def store(o_ref):
        o_ref[...] = out
    _by_part(pl.program_id(0) >= n_ctx_tiles, (o_ctx_ref, o_dec_ref), store)


def final_combine(lay, x, y_pair, w_col, mods, final_g):
    d = x.shape[1]
    tm = lay.tm
    return pl.pallas_call(
        functools.partial(_final_combine_kernel, n_ctx_tiles=lay.n_ctx_tiles),
        out_shape=[jax.ShapeDtypeStruct((lay.t_ctx, d), F32), jax.ShapeDtypeStruct((lay.t_dec, d), F32)],
        grid=(lay.n_tiles,),
        in_specs=[
            pl.BlockSpec((tm, d), lambda i: (i, 0)),
            pl.BlockSpec((TOP_K, tm, d // 2), lambda i: (0, i, 0)),
            pl.BlockSpec((tm, LANES), lambda i: (i, 0)),
            pl.BlockSpec((1, N_MOD, d), lambda i: (lay.mod_row(i), 0, 0)),
            pl.BlockSpec((1, d), lambda i: (0, 0)),
        ],
        out_specs=lay.row_specs([None, None], d),
        compiler_params=_cparams("arbitrary"),
        name="final_combine",
    )(x, y_pair, w_col, mods, final_g.reshape(1, d))


def group_moe(lay, x, routing, p, layer):
    t, d = x.shape
    h, idx, w_col, rank, cnt = routing
    n_sorted = TOP_K * t + N_EXPERTS * (EXPERT_TILE - 1)
    n_sorted = -(-n_sorted // EXPERT_TILE) * EXPERT_TILE
    pos, tile_expert, n_valid = dispatch_plan(idx, rank, cnt, n_sorted)
    xs = sc_scatter_rows(h, pos, n_sorted)
    ys = grouped_experts(xs, tile_expert, n_valid, p["moe_w_gate"], p["moe_w_up"], p["moe_w_down"], layer)
    return sc_gather_rows(ys, pos).reshape(TOP_K, t, d // 2), w_col


def _rope_angles(n, d):
    n_rows = n // GRID_W
    row = jnp.repeat(jnp.arange(n_rows), GRID_W).astype(F32)
    col = jnp.tile(jnp.arange(GRID_W), n_rows).astype(F32)
    nf = d // 4
    freqs = jnp.power(ROPE_BASE, -jnp.arange(nf, dtype=F32) / nf)
    ang = jnp.concatenate([row[:, None] * freqs, col[:, None] * freqs], axis=-1)
    return jnp.cos(ang), jnp.sin(ang)


def kernel(x_prompt, x_sample, cache_attn_k, cache_attn_v, state_ret_fwd, state_ret_bwd, c, c_ctx, w_ada, b_ada, norm_mix_g, norm_ffn_g, final_norm_g, da_w_qkv, da_lambda_q1, da_lambda_k1, da_lambda_q2, da_lambda_k2, da_subln_g, da_w_o, ret_w_qkv, ret_w_gate_fwd, ret_w_gate_bwd, ret_decay_fwd, ret_decay_bwd, ret_w_o, w_router, router_bias, moe_w_gate, moe_w_up, moe_w_down):
    b_ctx, n_ctx, d = x_prompt.shape
    b_dec, n_dec, _ = x_sample.shape
    past = cache_attn_k.shape[2]
    n_attn = cache_attn_k.shape[1]
    depth = w_ada.shape[0]
    assert b_dec + 1 <= MOD_ROWS
    lay = Layout(b_ctx, n_ctx, b_dec, n_dec)

    cond = jnp.zeros((MOD_ROWS, d), F32).at[0].set(c_ctx).at[1:1 + b_dec].set(c)
    mods_all = ada_modulation(cond, w_ada, b_ada)

    p = {
        "norm_ffn_g": norm_ffn_g, "final_norm_g": final_norm_g,
        "w_router_t": w_router.T.astype(BF16), "router_bias": router_bias.astype(F32),
        "moe_w_gate": moe_w_gate, "moe_w_up": moe_w_up, "moe_w_down": moe_w_down,
    }
    ret_log_decay = jnp.stack([jax.nn.log_sigmoid(ret_decay_fwd.astype(F32)),
                               jax.nn.log_sigmoid(ret_decay_bwd.astype(F32))], axis=1)
    ck_all = cache_attn_k.reshape(b_dec, n_attn, past, -1)
    cv_all = cache_attn_v.reshape(b_dec, n_attn, past, -1)

    x_parts = [x_prompt.reshape(lay.t_ctx, d), x_sample.reshape(lay.t_dec, d)]
    new_k, new_v, new_sf, new_sb = [], [], [], []
    pending_moe = None

    def first_kernel(*args):
        nonlocal x_parts, pending_moe
        outs = norm_mod_matmul(lay, x_parts, *args, pending_moe=pending_moe)
        if pending_moe is not None:
            *outs, x_joint = outs
            x_parts, pending_moe = [x_joint], None
        return outs

    for i in range(depth):
        mods = mods_all[i]
        j = i // 2
        if i % 2 == 0:
            lam_init = 0.8 - 0.6 * math.exp(-0.3 * i)
            qkw = DA_HEADS * 2 * DA_HEAD_DIM
            vw = DA_HEADS * DA_V_DIM
            cos, sin = _rope_angles(n_dec, DA_HEAD_DIM)
            rope = (jnp.tile(cos, (1, 4)), jnp.concatenate([-sin, sin, -sin, sin], axis=-1))
            q, k, v, k_ctx, v_ctx = first_kernel(
                norm_mix_g[i], mods, da_w_qkv[j],
                [(0, qkw, "rope64", math.log2(math.e) * DA_HEAD_DIM ** -0.5, BF16, "all"),
                 (qkw, qkw, "rope64", 1.0, BF16, "all"),
                 (2 * qkw, vw, "plain", 1.0, BF16, "all"),
                 (qkw, qkw, "plain", 1.0, F32, "ctx64"), (2 * qkw, vw, "plain", 1.0, F32, "ctx")],
                rope)
            lam_params = jnp.stack([da_lambda_q1[j], da_lambda_k1[j], da_lambda_q2[j], da_lambda_k2[j]])
            mix = [diff_attention(q, k, v, 0, b_ctx, n_ctx, lam_params, da_subln_g[j], lam_init,
                                  heads_per_step=DA_HEADS),
                   diff_attention(q, k, v, lay.t_ctx, b_dec, n_dec, lam_params, da_subln_g[j], lam_init,
                                  cache=(ck_all[:, j], cv_all[:, j]))]
            mix_w = da_w_o[j]
            new_k.append(k_ctx.reshape(b_ctx, n_ctx, DA_HEADS, 2, DA_HEAD_DIM))
            new_v.append(v_ctx.reshape(b_ctx, n_ctx, DA_HEADS, DA_V_DIM))
        else:
            kd = ret_w_qkv.shape[2] // 4
            dv = 2 * kd
            w_all = jnp.concatenate([ret_w_qkv[j], ret_w_gate_fwd[j], ret_w_gate_bwd[j]], axis=-1).astype(BF16)
            q, k, v, gates = first_kernel(
                norm_mix_g[i], mods, w_all,
                [(0, kd, "rope256", 1.0, BF16, "all"),
                 (kd, kd, "rope256", (kd // RET_HEADS) ** -0.5, F32, "all"),
                 (2 * kd, dv, "plain", 1.0, BF16, "all"), (2 * kd + dv, 2 * dv, "silu", 1.0, BF16, "all")],
                _rope_angles(n_dec, kd // RET_HEADS))
            ctx_seqs = math.gcd(b_ctx, max(1, n_dec // n_ctx // 2))
            o_ctx, sf, sb = retention(q, k, v, gates, 0, b_ctx, n_ctx, ret_log_decay[j], emit_state=True,
                                      seqs_per_step=ctx_seqs)
            (o_dec,) = retention(q, k, v, gates, lay.t_ctx, b_dec, n_dec, ret_log_decay[j],
                                 state=(state_ret_fwd[:, j:j + 1], state_ret_bwd[:, j:j + 1]))
            mix, mix_w = [o_ctx, o_dec], ret_w_o[j]
            new_sf.append(sf)
            new_sb.append(sb)
        x, *routing = proj_residual_route(lay, x_parts, mix, mix_w, mods, norm_ffn_g[i], p["w_router_t"],
                                          p["router_bias"])
        y_pair, w_col = group_moe(lay, x, routing, p, i)
        if i == depth - 1:
            x_parts = final_combine(lay, x, y_pair, w_col, mods, final_norm_g)
        else:
            x_parts, pending_moe = [x], (y_pair, w_col, mods)

    y_ctx, y_dec = x_parts
    return (y_ctx.reshape(b_ctx, n_ctx, d), y_dec.reshape(b_dec, n_dec, d),
            jnp.stack(new_k, axis=1), jnp.stack(new_v, axis=1),
            jnp.concatenate(new_sf, axis=1), jnp.concatenate(new_sb, axis=1))
```

```python
import functools
import math

import jax
import jax.numpy as jnp
from jax import lax
from jax.experimental import pallas as pl
from jax.experimental.pallas import tpu as pltpu
from jax.experimental.pallas import tpu_sc as plsc

F32 = jnp.float32
BF16 = jnp.bfloat16
I32 = jnp.int32

GRID_W = 64
ROPE_BASE = 10000.0
NORM_EPS = 1e-6
DA_HEADS = 8
DA_HEAD_DIM = 64
DA_V_DIM = 2 * DA_HEAD_DIM
RET_HEADS = 4
RET_CHUNK = 128
N_EXPERTS = 16
N_GROUPS = 4
EXPERTS_PER_GROUP = N_EXPERTS // N_GROUPS
TOP_K = 2
N_MOD = 6
MOD_ROWS = 16
LANES = 128

V7X_VMEM_LIMIT = 56 * 1024 * 1024
V7X_SC_CORES = 2
V7X_SC_SUBCORES = 16
SC_WORKERS = V7X_SC_CORES * V7X_SC_SUBCORES
SC_CHUNK_ROWS = 64

ROW_TILE = 512
EXPERT_TILE = 512
ROUTE_SPLIT = 2


def _cparams(*sem):
    return pltpu.CompilerParams(dimension_semantics=sem, vmem_limit_bytes=V7X_VMEM_LIMIT)


def _nt_dot(a, b):
    return lax.dot_general(a, b, (((1,), (1,)), ((), ())), preferred_element_type=F32)


def _tn_dot(a, b):
    return lax.dot_general(a, b, (((0,), (0,)), ((), ())), preferred_element_type=F32)


class Layout:
    def __init__(self, b_ctx, n_ctx, b_dec, n_dec):
        self.b_ctx, self.n_ctx, self.b_dec, self.n_dec = b_ctx, n_ctx, b_dec, n_dec
        self.t_ctx, self.t_dec = b_ctx * n_ctx, b_dec * n_dec
        self.t = self.t_ctx + self.t_dec
        self.tm = min(ROW_TILE, n_dec, self.t_ctx)
        assert self.t_ctx % self.tm == 0 and n_dec % self.tm == 0
        assert self.t_ctx % n_dec == 0 and self.t_ctx % n_ctx == 0
        self.n_ctx_tiles = self.t_ctx // self.tm
        self.n_tiles = self.t // self.tm

    def mod_row(self, i):
        r = i * self.tm
        return jnp.where(r < self.t_ctx, 0, 1 + (r - self.t_ctx) // self.n_dec)

    def part_tile(self, part, i):
        if part == 0:
            return jnp.minimum(i, self.n_ctx_tiles - 1)
        return jnp.maximum(i - self.n_ctx_tiles, 0)

    def row_specs(self, arrays, width):
        if len(arrays) == 1:
            return [pl.BlockSpec((self.tm, width), lambda i, *_: (i, 0))]
        return [pl.BlockSpec((self.tm, width), lambda i, *_, p=p: (self.part_tile(p, i), 0)) for p in (0, 1)]


def _ada_kernel(c_ref, w_ref, b_ref, o_ref):
    s = jax.nn.silu(c_ref[...]).astype(BF16)
    acc = jnp.dot(s, w_ref[0].astype(BF16), preferred_element_type=F32)
    o_ref[0] = acc + b_ref[0]


def ada_modulation(cond, w_ada, b_ada):
    depth, d, n = w_ada.shape
    tn = 1536
    out = pl.pallas_call(
        _ada_kernel,
        out_shape=jax.ShapeDtypeStruct((depth, MOD_ROWS, n), F32),
        grid=(depth, n // tn),
        in_specs=[
            pl.BlockSpec((MOD_ROWS, d), lambda l, j: (0, 0)),
            pl.BlockSpec((1, d, tn), lambda l, j: (l, 0, j)),
            pl.BlockSpec((1, 1, tn), lambda l, j: (l, 0, j)),
        ],
        out_specs=pl.BlockSpec((1, MOD_ROWS, tn), lambda l, j: (l, 0, j)),
        compiler_params=_cparams("parallel", "parallel"),
        name="ada_modulation",
    )(cond, w_ada, b_ada.reshape(depth, 1, n))
    return out.reshape(depth, MOD_ROWS, N_MOD, d)


def _modulated_norm(x, g, mod_ref, shift_idx):
    ms = jnp.mean(x * x, axis=-1, keepdims=True)
    y = x * lax.rsqrt(ms + NORM_EPS) * g
    return y * (1.0 + mod_ref[0, shift_idx + 1:shift_idx + 2, :]) + mod_ref[0, shift_idx:shift_idx + 1, :]


def _by_part(is_dec, refs, fn):
    if len(refs) == 1:
        fn(refs[0])
        return
    pl.when(jnp.logical_not(is_dec))(lambda: fn(refs[0]))
    pl.when(is_dec)(lambda: fn(refs[1]))


def _rope64(a, cos, sin_signed, first_half):
    partner = jnp.where(first_half, pltpu.roll(a, 96, 1), pltpu.roll(a, 32, 1))
    return a * cos + partner * sin_signed


def _moe_combine(x, y_ref, w_ref, mod_ref):
    w = w_ref[...]

    def expert_out(slot):
        lo, hi = _unpack_bf16_halves(y_ref[slot])
        return jnp.concatenate([lo.astype(F32), hi.astype(F32)], axis=1)

    return x + mod_ref[0, 5:6, :] * (w[:, 0:1] * expert_out(0) + w[:, 1:2] * expert_out(1))


def _nmm_kernel(*refs, n_x, pending_moe, cast_w, outs, tn, n_ctx_tiles):
    x_refs = refs[:n_x]
    if pending_moe:
        y_ref, wcol_ref, prev_mod_ref = refs[n_x:n_x + 3]
        refs = refs[:n_x] + refs[n_x + 3:]
    g_ref, mod_ref, w_ref, cos_ref, sin_ref = refs[n_x:n_x + 5]
    n_out = len(outs) + (1 if pending_moe else 0)
    out_refs = refs[n_x + 5:n_x + 5 + n_out]
    if cast_w:
        w_f32_ref, w_ref = w_ref, refs[n_x + 5 + n_out]

        @pl.when(pl.program_id(0) == 0)
        def _():
            w_ref[...] = w_f32_ref[...].astype(BF16)
    is_dec = pl.program_id(0) >= n_ctx_tiles
    tm = x_refs[0].shape[0]
    segments = sorted({(col0, width) for col0, width, _, _, _ in outs})

    def rotated(a, kind):
        cos, sin = cos_ref[...], sin_ref[...]
        if kind == "rope64":
            lane = lax.broadcasted_iota(I32, (tm, LANES), 1)
            first_half = (lane & 32) == 0
            return [(c * 128, _rope64(a[:, c * 128:(c + 1) * 128], cos, sin, first_half))
                    for c in range(tn // 128)]
        pieces = []
        for c in range(tn // 256):
            x1, x2 = a[:, c * 256:c * 256 + 128], a[:, c * 256 + 128:(c + 1) * 256]
            pieces += [(c * 256, x1 * cos - x2 * sin), (c * 256 + 128, x2 * cos + x1 * sin)]
        return pieces

    def emit(dec):
        x = x_refs[(1 if dec else 0) if n_x == 2 else 0][...]
        if pending_moe:
            x = _moe_combine(x, y_ref, wcol_ref, prev_mod_ref)
            out_refs[-1][...] = x
        h = _modulated_norm(x, g_ref[...], mod_ref, 0).astype(BF16)
        for col0, width in segments:
            sinks = [(o_ref, o) for o_ref, o in zip(out_refs, outs)
                     if (o[0], o[1]) == (col0, width) and not (dec and o[4] != "all")]
            for blk in range(width // tn):
                acc = jnp.dot(h, w_ref[:, col0 + blk * tn:col0 + (blk + 1) * tn], preferred_element_type=F32)
                for o_ref, (_, _, kind, scale, rows) in sinks:
                    a = acc if scale == 1.0 else acc * scale
                    if rows == "ctx64":
                        n64 = width // 64
                        for c in range(tn // 64):
                            o_ref[pl.ds(blk * tn // 64 + c, tm, stride=n64), :] = (
                                a[:, c * 64:(c + 1) * 64].astype(o_ref.dtype))
                    elif kind == "silu":
                        o_ref[:, blk * tn:(blk + 1) * tn] = jax.nn.silu(a).astype(o_ref.dtype)
                    elif kind == "plain" or rows == "ctx" or not dec:
                        o_ref[:, blk * tn:(blk + 1) * tn] = a.astype(o_ref.dtype)
                    else:
                        for off, val in rotated(a, kind):
                            o_ref[:, blk * tn + off:blk * tn + off + 128] = val.astype(o_ref.dtype)

    pl.when(jnp.logical_not(is_dec))(lambda: emit(False))
    pl.when(is_dec)(lambda: emit(True))


def norm_mod_matmul(lay, x_parts, g, mods, w, outputs, rope_tables, pending_moe=None, tn=512):
    d = x_parts[0].shape[1]
    tm = lay.tm
    n_total = w.shape[1]
    outs, out_shapes, out_specs = [], [], []
    for col0, width, kind, scale, dtype, rows in outputs:
        assert width % tn == 0 and col0 % LANES == 0
        outs.append((col0, width, kind, float(scale), rows))
        if rows == "ctx64":
            n64 = width // 64
            out_shapes.append(jax.ShapeDtypeStruct((lay.t_ctx * n64, 64), dtype))
            out_specs.append(pl.BlockSpec((tm * n64, 64), lambda i: (lay.part_tile(0, i), 0)))
            continue
        n_rows = lay.t_ctx if rows == "ctx" else lay.t
        out_shapes.append(jax.ShapeDtypeStruct((n_rows, width), dtype))
        out_specs.append(lay.row_specs([None, None], width)[0] if rows == "ctx" else lay.row_specs([None], width)[0])
    blocks_per_seq = lay.n_dec // tm
    rope_spec = pl.BlockSpec((tm, LANES), lambda i: (lay.part_tile(1, i) % blocks_per_seq, 0))
    mod_spec = pl.BlockSpec((1, N_MOD, d), lambda i: (lay.mod_row(i), 0, 0))
    moe_specs, moe_args = [], []
    if pending_moe is not None:
        assert len(x_parts) == 1
        moe_specs = [pl.BlockSpec((TOP_K, tm, d // 2), lambda i: (0, i, 0)),
                     pl.BlockSpec((tm, LANES), lambda i: (i, 0)), mod_spec]
        moe_args = list(pending_moe)
        out_shapes.append(jax.ShapeDtypeStruct((lay.t, d), F32))
        out_specs.append(pl.BlockSpec((tm, d), lambda i: (i, 0)))
    return pl.pallas_call(
        functools.partial(_nmm_kernel, n_x=len(x_parts), pending_moe=pending_moe is not None,
                          cast_w=w.dtype != BF16, outs=tuple(outs), tn=tn, n_ctx_tiles=lay.n_ctx_tiles),
        out_shape=out_shapes,
        grid=(lay.n_tiles,),
        in_specs=lay.row_specs(x_parts, d) + moe_specs + [
            pl.BlockSpec((1, d), lambda i: (0, 0)),
            mod_spec,
            pl.BlockSpec((d, n_total), lambda i: (0, 0), pipeline_mode=pl.Buffered(1)),
            rope_spec, rope_spec,
        ],
        out_specs=out_specs,
        scratch_shapes=[pltpu.VMEM(w.shape, BF16)] if w.dtype != BF16 else [],
        compiler_params=_cparams("arbitrary"),
        name="norm_mod_matmul",
    )(*x_parts, *moe_args, g.reshape(1, d), mods, w, *rope_tables)


def _diff_lambda(lam_ref, lam_init):
    lp = lam_ref[...]
    return (jnp.exp(jnp.sum(lp[0:1] * lp[1:2], axis=-1, keepdims=True))
            - jnp.exp(jnp.sum(lp[2:3] * lp[3:4], axis=-1, keepdims=True)) + lam_init)


ONES_ROWS = 16


def _scores_t(k, q, comp):
    lane = lax.broadcasted_iota(I32, q.shape, 1)
    return _nt_dot(k, jnp.where((lane < DA_HEAD_DIM) == (comp == 0), q, jnp.zeros_like(q)))


def _softmax_values_t(scores, vts):
    dv = vts[0].shape[0] - ONES_ROWS
    maxes = [jnp.max(s, axis=0, keepdims=True) for s in scores]
    exps = [jnp.exp2(s - m).astype(BF16) for s, m in zip(scores, maxes)]
    accs = [jnp.dot(vt, e, preferred_element_type=F32) for vt, e in zip(vts, exps)]
    return [acc[0:dv] / acc[dv:dv + 1] for acc in accs]


def _diff_finish_t(parts, lam, g_col, lam_init):
    o = parts[0] - lam * parts[1]
    ms = jnp.mean(o * o, axis=0, keepdims=True)
    return jnp.transpose((o * lax.rsqrt(ms + NORM_EPS) * g_col) * (1.0 - lam_init))


def _diff_attn_kernel(lam_ref, q_ref, k_ref, v_ref, g_ref, o_ref, *, lam_init, heads_per_step):
    hw = 2 * DA_HEAD_DIM
    lam = _diff_lambda(lam_ref, lam_init)
    ones = jnp.ones((ONES_ROWS, k_ref.shape[0]), BF16)
    heads = [slice(hh * hw, (hh + 1) * hw) for hh in range(heads_per_step)]
    scores = [[_scores_t(k_ref[:, cols], q_ref[:, cols], comp) for comp in range(2)] for cols in heads]
    vts = [jnp.concatenate([jnp.transpose(v_ref[:, cols].astype(F32)).astype(BF16), ones], axis=0) for cols in heads]
    parts = _softmax_values_t([s for pair in scores for s in pair], [vt for vt in vts for _ in range(2)])
    for hh, cols in enumerate(heads):
        o_ref[:, cols] = _diff_finish_t(parts[2 * hh:2 * hh + 2], lam, g_ref[...], lam_init).astype(o_ref.dtype)


def _diff_attn_cached_kernel(lam_ref, q_ref, k_ref, v_ref, ck_ref, cv_ref, g_ref, o_ref, k_scr, vt_scr, s_scr, *,
                             lam_init, sub_rows):
    seq_len, hw = q_ref.shape
    n_sub = seq_len // sub_rows
    assert n_sub % 2 == 0
    lam = _diff_lambda(lam_ref, lam_init)
    k_scr[0:seq_len, :] = k_ref[...]
    past = vt_scr.shape[1] - seq_len
    head = pl.program_id(1)
    for comp in range(2):
        rows = pl.ds(head * 2 + comp, past, stride=2 * DA_HEADS)
        k_scr[seq_len:, comp * DA_HEAD_DIM:(comp + 1) * DA_HEAD_DIM] = ck_ref[0, rows, :].astype(BF16)
    vt_scr[0:hw, 0:seq_len] = jnp.transpose(v_ref[...])
    vt_scr[0:hw, seq_len:] = jnp.transpose(cv_ref[0, pl.ds(head, past, stride=DA_HEADS), :]).astype(BF16)
    vt_scr[hw:, :] = jnp.ones((ONES_ROWS, vt_scr.shape[1]), BF16)

    def rows_of(t):
        return pl.ds(pl.multiple_of(t * sub_rows, sub_rows), sub_rows)

    def scores(t, slot, comp):
        s_scr[slot, comp] = _scores_t(k_scr[...], q_ref[rows_of(t), :], comp)

    def stage(t_next, slot_next, t, slot):
        for comp in range(2):
            scores(t_next, slot_next, comp)
        parts = _softmax_values_t([s_scr[slot, 0], s_scr[slot, 1]], [vt_scr[...]] * 2)
        o_ref[rows_of(t), :] = _diff_finish_t(parts, lam, g_ref[...], lam_init).astype(o_ref.dtype)

    scores(0, 0, 0)
    scores(0, 0, 1)

    def body(i2, carry):
        t = 2 * i2
        stage(t + 1, 1, t, 0)
        stage(jnp.minimum(t + 2, n_sub - 1), 0, t + 1, 1)
        return carry

    lax.fori_loop(0, n_sub // 2, body, 0, unroll=2 if n_sub % 4 == 0 else 1)


def diff_attention(q, k, v, row0, batch, seq_len, lam_params, subln_g, lam_init, cache=None, heads_per_step=1,
                   sub_rows=256):
    width = q.shape[1]
    hw = 2 * DA_HEAD_DIM
    bw = heads_per_step * hw
    assert row0 % seq_len == 0 and DA_HEADS % heads_per_step == 0
    s0 = row0 // seq_len
    seq_spec = pl.BlockSpec((seq_len, bw), lambda b, h: (s0 + b, h))
    in_specs = [pl.BlockSpec((4, DA_HEAD_DIM), lambda b, h: (0, 0)), seq_spec, seq_spec, seq_spec]
    args = [lam_params, q, k, v]
    scratch = []
    if cache is None:
        body = functools.partial(_diff_attn_kernel, lam_init=lam_init, heads_per_step=heads_per_step)
    else:
        assert heads_per_step == 1
        ck, cv = cache
        past = cv.shape[1] // DA_HEADS
        in_specs += [pl.BlockSpec((1,) + ck.shape[1:], lambda b, h: (b, 0, 0)),
                     pl.BlockSpec((1,) + cv.shape[1:], lambda b, h: (b, 0, 0))]
        args += [ck, cv]
        sub_rows = min(sub_rows, seq_len // 2)
        scratch = [pltpu.VMEM((seq_len + past, hw), BF16), pltpu.VMEM((hw + ONES_ROWS, seq_len + past), BF16),
                   pltpu.VMEM((2, 2, seq_len + past, sub_rows), F32)]
        body = functools.partial(_diff_attn_cached_kernel, lam_init=lam_init, sub_rows=sub_rows)
    in_specs.append(pl.BlockSpec((hw, 1), lambda b, h: (0, 0)))
    args.append(subln_g.reshape(hw, 1))
    return pl.pallas_call(
        body,
        out_shape=jax.ShapeDtypeStruct((batch * seq_len, width), BF16),
        grid=(batch, DA_HEADS // heads_per_step),
        in_specs=in_specs,
        out_specs=pl.BlockSpec((seq_len, bw), lambda b, h: (b, h)),
        scratch_shapes=scratch,
        compiler_params=_cparams("parallel", "parallel"),
        name="diff_attention",
    )(*args)


def _retention_kernel(*refs, seq_len, seqs, has_state, emit_state):
    lg_ref, q_ref, k_ref, v_ref, gf_ref, gb_ref = refs[:6]
    pos = 6
    if has_state:
        s0_refs = refs[6:8]
        pos = 8
    o_ref = refs[pos]
    pos += 1
    if emit_state:
        s_out_refs = refs[pos:pos + 2]
        pos += 2
    s_scr, o_scr = refs[pos:pos + 2]

    h = pl.program_id(1)
    c_len = RET_CHUNK
    n_chunks = seq_len // c_len
    assert n_chunks % 2 == 0
    row = lax.broadcasted_iota(I32, (c_len, c_len), 0)
    colm = lax.broadcasted_iota(I32, (c_len, c_len), 1)
    rel = (row - colm).astype(F32)
    posv = lax.broadcasted_iota(I32, (c_len, 1), 0).astype(F32)

    consts = []
    for backward in (False, True):
        lg = lg_ref[1 if backward else 0, h]
        if backward:
            intra = jnp.where(rel <= 0, jnp.exp(-rel * lg), 0.0)
            q_decay = jnp.exp((c_len - posv) * lg)
            k_decay = jnp.exp(posv * lg)
        else:
            intra = jnp.where(rel >= 0, jnp.exp(rel * lg), 0.0)
            q_decay = jnp.exp((posv + 1.0) * lg)
            k_decay = jnp.exp((c_len - 1.0 - posv) * lg)
        consts.append((intra, q_decay, k_decay, jnp.exp(jnp.zeros((1, 1), F32) + c_len * lg)))
        direction = 1 if backward else 0
        for sq in range(seqs):
            if has_state:
                s_scr[sq, direction] = s0_refs[direction][sq, 0, 0]
            else:
                s_scr[sq, direction] = jnp.zeros(s_scr.shape[2:], F32)

    def body(ci, first_touch):
        chains = [(sq, direction, pl.ds(pl.multiple_of(sq * seq_len + c * c_len, c_len), c_len))
                  for sq in range(seqs) for direction, c in ((0, ci), (1, n_chunks - 1 - ci))]
        qs = [q_ref[rows, :].astype(BF16) for _, _, rows in chains]
        ks = [k_ref[rows, :].astype(F32) for _, _, rows in chains]
        vs = [v_ref[rows, :].astype(BF16) for _, _, rows in chains]
        intras = [(_nt_dot(qb, kf.astype(BF16)) * consts[d][0]).astype(BF16)
                  for (_, d, _), qb, kf in zip(chains, qs, ks)]
        states = [s_scr[sq, d] for sq, d, _ in chains]
        outs = [jnp.dot(a, vb, preferred_element_type=F32)
                + jnp.dot(qb, s.astype(BF16), preferred_element_type=F32) * consts[d][1]
                for (_, d, _), a, qb, vb, s in zip(chains, intras, qs, vs, states)]
        for (sq, d, _), kf, vb, s in zip(chains, ks, vs, states):
            s_scr[sq, d] = consts[d][3] * s + _tn_dot((kf * consts[d][2]).astype(BF16), vb)
        centred = [o - jnp.mean(o, axis=-1, keepdims=True) for o in outs]
        scales = [lax.rsqrt(jnp.mean(oc * oc, axis=-1, keepdims=True) + NORM_EPS) for oc in centred]
        for (_, d, rows), oc, scale in zip(chains, centred, scales):
            gated = oc * scale * (gb_ref if d else gf_ref)[rows, :].astype(F32)
            if first_touch:
                o_scr[rows, :] = gated
            else:
                o_ref[rows, :] = (o_scr[rows, :] + gated).astype(o_ref.dtype)

    unroll = 8 if n_chunks % 16 == 0 else 1
    lax.fori_loop(0, n_chunks // 2, lambda ci, c: (body(ci, True), c)[1], 0, unroll=unroll)
    lax.fori_loop(n_chunks // 2, n_chunks, lambda ci, c: (body(ci, False), c)[1], 0, unroll=unroll)
    if emit_state:
        for d in range(2):
            for sq in range(seqs):
                s_out_refs[d][sq, 0, 0] = s_scr[sq, d]


def retention(q, k, v, gates, row0, batch, seq_len, log_decay, state=None, emit_state=False, seqs_per_step=1):
    dk = q.shape[1] // RET_HEADS
    dv = v.shape[1] // RET_HEADS
    seqs = seqs_per_step
    rows = seqs * seq_len
    assert row0 % rows == 0 and batch % seqs == 0
    s0 = row0 // rows
    in_specs = [
        pl.BlockSpec(memory_space=pltpu.SMEM),
        pl.BlockSpec((rows, dk), lambda b, h: (s0 + b, h)),
        pl.BlockSpec((rows, dk), lambda b, h: (s0 + b, h)),
        pl.BlockSpec((rows, dv), lambda b, h: (s0 + b, h)),
        pl.BlockSpec((rows, dv), lambda b, h: (s0 + b, h)),
        pl.BlockSpec((rows, dv), lambda b, h: (s0 + b, RET_HEADS + h)),
    ]
    args = [log_decay, q, k, v, gates, gates]
    state_spec = pl.BlockSpec((seqs, 1, 1, dk, dv), lambda b, h: (b, 0, h, 0, 0))
    if state is not None:
        in_specs += [state_spec, state_spec]
        args += list(state)
    out_shapes = [jax.ShapeDtypeStruct((batch * seq_len, RET_HEADS * dv), BF16)]
    out_specs = [pl.BlockSpec((rows, dv), lambda b, h: (b, h))]
    if emit_state:
        out_shapes += [jax.ShapeDtypeStruct((batch, 1, RET_HEADS, dk, dv), F32)] * 2
        out_specs += [state_spec, state_spec]
    return pl.pallas_call(
        functools.partial(_retention_kernel, seq_len=seq_len, seqs=seqs, has_state=state is not None,
                          emit_state=emit_state),
        out_shape=out_shapes,
        grid=(batch // seqs, RET_HEADS),
        in_specs=in_specs,
        out_specs=out_specs,
        scratch_shapes=[pltpu.VMEM((seqs, 2, dk, dv), F32), pltpu.VMEM((rows, dv), F32)],
        compiler_params=_cparams("parallel", "parallel"),
        name="retention",
    )(*args)


def _pack_bf16_halves(x):
    half = x.shape[1] // 2
    lo = lax.bitcast_convert_type(x[:, :half].astype(F32), jnp.uint32)
    hi = lax.bitcast_convert_type(x[:, half:].astype(F32), jnp.uint32)
    return (hi & jnp.uint32(0xFFFF0000)) | (lo >> 16)


def _unpack_bf16_halves(p):
    lo = lax.bitcast_convert_type(p << 16, F32).astype(BF16)
    hi = lax.bitcast_convert_type(p & jnp.uint32(0xFFFF0000), F32).astype(BF16)
    return lo, hi


def _first_max_onehot(vals):
    m = vals[0]
    for v in vals[1:]:
        m = jnp.maximum(m, v)
    onehot, taken = [], None
    for v in vals:
        hit = v == m
        if taken is None:
            onehot.append(hit)
            taken = hit
        else:
            onehot.append(hit & jnp.logical_not(taken))
            taken = taken | hit
    return m, onehot


def _pick(onehot, vals):
    out = vals[-1]
    for oh, v in zip(onehot[-2::-1], vals[-2::-1]):
        out = jnp.where(oh, v, out)
    return out


def _route_rows(x, g_ref, mod_ref, wr_ref, bias_ref, tri_ref, h_ref, idx_ref, wcol_ref, rank_ref, cnt_ref,
                ind_scr, wrow_scr):
    hb = _modulated_norm(x, g_ref[...], mod_ref, 3).astype(BF16)
    h_ref[...] = _pack_bf16_halves(hb)
    logits = _nt_dot(wr_ref[...], hb)
    s = jax.nn.sigmoid(logits)
    sel = s + bias_ref[...]
    neg_inf = jnp.full_like(sel[0:1], -jnp.inf)
    sel_rows = [sel[e:e + 1] for e in range(N_EXPERTS)]
    s_rows = [s[e:e + 1] for e in range(N_EXPERTS)]

    def top2(vals):
        m1, oh1 = _first_max_onehot(vals)
        rest = [jnp.where(o, neg_inf, v) for o, v in zip(oh1, vals)]
        m2, oh2 = _first_max_onehot(rest)
        return m1, m2, oh1, oh2

    grp_scores = []
    for g in range(N_GROUPS):
        m1, m2, _, _ = top2(sel_rows[g * EXPERTS_PER_GROUP:(g + 1) * EXPERTS_PER_GROUP])
        grp_scores.append(m1 + m2)
    _, in_grp = _first_max_onehot(grp_scores)
    cand_sel = [_pick(in_grp, [sel_rows[g * EXPERTS_PER_GROUP + k] for g in range(N_GROUPS)])
                for k in range(EXPERTS_PER_GROUP)]
    cand_s = [_pick(in_grp, [s_rows[g * EXPERTS_PER_GROUP + k] for g in range(N_GROUPS)])
              for k in range(EXPERTS_PER_GROUP)]
    _, _, oh1, oh2 = top2(cand_sel)
    w1 = _pick(oh1, cand_s)
    w2 = _pick(oh2, cand_s)
    denom = w1 + w2
    wrow_scr[...] = jnp.zeros_like(wrow_scr)
    wrow_scr[0:1, :] = w1 / denom
    wrow_scr[1:2, :] = w2 / denom
    wcol_ref[...] = jnp.transpose(wrow_scr[...])

    ints = [jnp.full(w1.shape, k, I32) for k in range(EXPERTS_PER_GROUP)]
    grp = _pick(in_grp, ints) * EXPERTS_PER_GROUP
    idx_ref[0:1, :] = grp + _pick(oh1, ints)
    idx_ref[1:2, :] = grp + _pick(oh2, ints)

    one, zero = jnp.ones_like(w1), jnp.zeros_like(w1)
    for g in range(N_GROUPS):
        for k in range(EXPERTS_PER_GROUP):
            e = g * EXPERTS_PER_GROUP + k
            ind_scr[e:e + 1, :] = jnp.where(in_grp[g] & (oh1[k] | oh2[k]), one, zero)
    ind = ind_scr[...].astype(BF16)
    ranks = jnp.dot(ind, tri_ref[...], preferred_element_type=F32)
    cnt_ref[0] = jnp.dot(ind, jnp.ones((ind.shape[1], LANES), BF16), preferred_element_type=F32)
    for slot, oh in ((0, oh1), (1, oh2)):
        r = zero
        for g in range(N_GROUPS):
            for k in range(EXPERTS_PER_GROUP):
                e = g * EXPERTS_PER_GROUP + k
                r = r + jnp.where(in_grp[g] & oh[k], ranks[e:e + 1], zero)
        rank_ref[slot:slot + 1, :] = r.astype(I32)


def _proj_route_kernel(*refs, n_x, n_a, n_ctx_tiles):
    x_refs, a_refs = refs[:n_x], refs[n_x:n_x + n_a]
    w_ref, mod_ref, g_ref, wr_ref, bias_ref, tri_ref, o_ref = refs[n_x + n_a:n_x + n_a + 7]
    route_out_refs = refs[n_x + n_a + 7:n_x + n_a + 12]
    ind_scr, wrow_scr, w_scr = refs[n_x + n_a + 12:]
    is_dec = pl.program_id(0) >= n_ctx_tiles

    @pl.when(pl.program_id(0) == 0)
    def _():
        w_scr[...] = w_ref[...].astype(BF16)
    h_ref, idx_ref, wcol_ref, rank_ref, cnt_ref = route_out_refs
    sub = o_ref.shape[0] // ROUTE_SPLIT

    def body(dec):
        x_ref = x_refs[(1 if dec else 0) if n_x == 2 else 0]
        a_ref = a_refs[(1 if dec else 0) if n_a == 2 else 0]
        parts = [slice(part * sub, (part + 1) * sub) for part in range(ROUTE_SPLIT)]
        for rows in parts:
            y = jnp.dot(a_ref[rows, :].astype(BF16), w_scr[...], preferred_element_type=F32)
            o_ref[rows, :] = x_ref[rows, :] + mod_ref[0, 2:3, :] * y
        for part, rows in enumerate(parts):
            _route_rows(o_ref[rows, :], g_ref, mod_ref, wr_ref, bias_ref, tri_ref, h_ref.at[rows, :],
                        idx_ref.at[:, rows], wcol_ref.at[rows, :], rank_ref.at[:, rows], cnt_ref.at[part:part + 1],
                        ind_scr, wrow_scr)

    if n_x == 1 and n_a == 1:
        body(False)
    else:
        pl.when(jnp.logical_not(is_dec))(lambda: body(False))
        pl.when(is_dec)(lambda: body(True))


def proj_residual_route(lay, x_parts, a_parts, w, mods, ffn_g, w_router_t, router_bias):
    d = x_parts[0].shape[1]
    kdim = a_parts[0].shape[1]
    tm, t = lay.tm, lay.t
    sub = tm // ROUTE_SPLIT
    tri = jnp.triu(jnp.ones((sub, sub), BF16), k=1)
    pair = jax.ShapeDtypeStruct((TOP_K, t), I32)
    pair_spec = pl.BlockSpec((TOP_K, tm), lambda i: (0, i))
    const = lambda shape: pl.BlockSpec(shape, lambda i: (0,) * len(shape))
    return pl.pallas_call(
        functools.partial(_proj_route_kernel, n_x=len(x_parts), n_a=len(a_parts), n_ctx_tiles=lay.n_ctx_tiles),
        out_shape=[jax.ShapeDtypeStruct((t, d), F32), jax.ShapeDtypeStruct((t, d // 2), jnp.uint32), pair,
                   jax.ShapeDtypeStruct((t, LANES), F32), pair,
                   jax.ShapeDtypeStruct((lay.n_tiles * ROUTE_SPLIT, N_EXPERTS, LANES), F32)],
        grid=(lay.n_tiles,),
        in_specs=lay.row_specs(x_parts, d) + lay.row_specs(a_parts, kdim) + [
            pl.BlockSpec((kdim, d), lambda i: (0, 0), pipeline_mode=pl.Buffered(1)),
            pl.BlockSpec((1, N_MOD, d), lambda i: (lay.mod_row(i), 0, 0)),
            const((1, d)), const((N_EXPERTS, d)), const((N_EXPERTS, 1)), const((sub, sub)),
        ],
        out_specs=[pl.BlockSpec((tm, d), lambda i: (i, 0)), pl.BlockSpec((tm, d // 2), lambda i: (i, 0)), pair_spec,
                   pl.BlockSpec((tm, LANES), lambda i: (i, 0)), pair_spec,
                   pl.BlockSpec((ROUTE_SPLIT, N_EXPERTS, LANES), lambda i: (i, 0, 0))],
        scratch_shapes=[pltpu.VMEM((N_EXPERTS, sub), F32), pltpu.VMEM((LANES, sub), F32),
                        pltpu.VMEM((kdim, d), BF16)],
        compiler_params=_cparams("arbitrary"),
        name="proj_residual_route",
    )(*x_parts, *a_parts, w, mods, ffn_g.reshape(1, d), w_router_t, router_bias.reshape(N_EXPERTS, 1), tri)


def dispatch_plan(idx, rank, cnt, n_sorted):
    t = idx.shape[1]
    n_tiles, tm = cnt.shape[0], t // cnt.shape[0]
    cnt_tile = cnt[:, :, 0].astype(I32)
    total = jnp.sum(cnt_tile, axis=0)
    padded = ((total + EXPERT_TILE - 1) // EXPERT_TILE) * EXPERT_TILE
    end = jnp.cumsum(padded)
    start = end - padded
    base = start[None, :] + jnp.cumsum(cnt_tile, axis=0) - cnt_tile
    idx3 = idx.reshape(TOP_K, n_tiles, tm)
    pos = rank.reshape(TOP_K, n_tiles, tm)
    for e in range(N_EXPERTS):
        pos = pos + jnp.where(idx3 == e, base[None, :, e, None], 0)
    tile_row = jnp.arange(n_sorted // EXPERT_TILE, dtype=I32) * EXPERT_TILE
    tile_expert = jnp.minimum(jnp.sum(end[None, :] <= tile_row[:, None], axis=1), N_EXPERTS - 1).astype(I32)
    n_valid = (end[-1] // EXPERT_TILE).astype(I32).reshape(1)
    return pos.reshape(TOP_K * t).astype(I32), tile_expert, n_valid


def _sc_mesh():
    return plsc.VectorSubcoreMesh(core_axis_name="c", subcore_axis_name="s")


def _sc_worker_id():
    return lax.axis_index("s") * V7X_SC_CORES + lax.axis_index("c")


def _sc_two_stage_pipeline(n_chunks, fetch_idx, load, store):
    assert n_chunks % 2 == 0
    fetch_idx(0, 0)
    load(0, 0).start()

    @pl.loop(0, n_chunks // 2)
    def _(p):
        j = 2 * p

        @pl.when(p > 0)
        def _():
            store(j - 1, 1).wait()

        fetch_idx(j + 1, 1)
        load(j + 1, 1).start()
        load(j, 0).wait()
        store(j, 0).start()
        store(j, 0).wait()

        @pl.when(j + 2 < n_chunks)
        def _():
            fetch_idx(j + 2, 0)
            load(j + 2, 0).start()

        load(j + 1, 1).wait()
        store(j + 1, 1).start()

    store(n_chunks - 1, 1).wait()


_SC_SCRATCH = lambda chunk, d, dtype: [
    pltpu.VMEM((chunk,), I32), pltpu.VMEM((chunk,), I32),
    pltpu.VMEM((chunk, d), dtype), pltpu.VMEM((chunk, d), dtype),
    pltpu.SemaphoreType.DMA, pltpu.SemaphoreType.DMA, pltpu.SemaphoreType.DMA, pltpu.SemaphoreType.DMA]


def sc_scatter_rows(src, pos, n_out_rows):
    t, d = src.shape
    n_idx = pos.shape[0]
    per_worker = n_idx // SC_WORKERS
    chunk = SC_CHUNK_ROWS
    assert n_idx % (SC_WORKERS * chunk * 2) == 0 and t % chunk == 0

    @functools.partial(
        pl.kernel, mesh=_sc_mesh(), out_type=jax.ShapeDtypeStruct((n_out_rows, d), src.dtype),
        scratch_types=_SC_SCRATCH(chunk, d, src.dtype), name="sc_scatter_rows")
    def scatter(src_hbm, pos_hbm, out_hbm, idx_a, idx_b, rows_a, rows_b, lsem_a, lsem_b, ssem_a, ssem_b):
        base = _sc_worker_id() * per_worker
        idx, rows, lsem, ssem = (idx_a, idx_b), (rows_a, rows_b), (lsem_a, lsem_b), (ssem_a, ssem_b)

        def fetch_idx(j, b):
            pltpu.sync_copy(pos_hbm.at[pl.ds(base + j * chunk, chunk)], idx[b])

        def load(j, b):
            return pltpu.make_async_copy(src_hbm.at[pl.ds(lax.rem(base + j * chunk, t), chunk)], rows[b], lsem[b])

        def store(j, b):
            return pltpu.make_async_copy(rows[b], out_hbm.at[idx[b]], ssem[b])

        _sc_two_stage_pipeline(per_worker // chunk, fetch_idx, load, store)

    return scatter(src, pos)


def sc_gather_rows(table, idx):
    _, d = table.shape
    n_idx = idx.shape[0]
    per_worker = n_idx // SC_WORKERS
    chunk = SC_CHUNK_ROWS
    assert n_idx % (SC_WORKERS * chunk * 2) == 0

    @functools.partial(
        pl.kernel, mesh=_sc_mesh(), out_type=jax.ShapeDtypeStruct((n_idx, d), table.dtype),
        scratch_types=_SC_SCRATCH(chunk, d, table.dtype), name="sc_gather_rows")
    def gather(table_hbm, idx_hbm, out_hbm, idx_a, idx_b, rows_a, rows_b, lsem_a, lsem_b, ssem_a, ssem_b):
        base = _sc_worker_id() * per_worker
        idx, rows, lsem, ssem = (idx_a, idx_b), (rows_a, rows_b), (lsem_a, lsem_b), (ssem_a, ssem_b)

        def fetch_idx(j, b):
            pltpu.sync_copy(idx_hbm.at[pl.ds(base + j * chunk, chunk)], idx[b])

        def load(j, b):
            return pltpu.make_async_copy(table_hbm.at[idx[b]], rows[b], lsem[b])

        def store(j, b):
            return pltpu.make_async_copy(rows[b], out_hbm.at[pl.ds(base + j * chunk, chunk)], ssem[b])

        _sc_two_stage_pipeline(per_worker // chunk, fetch_idx, load, store)

    return gather(table, idx)


def _experts_kernel(te_ref, nv_ref, x_ref, wg_ref, wu_ref, wd_ref, y_ref, wg_scr, wu_scr, wd_scr):
    i = pl.program_id(0)

    @pl.when(i < nv_ref[0])
    def _():
        @pl.when((i == 0) | (te_ref[i] != te_ref[jnp.maximum(i - 1, 0)]))
        def _():
            wg_scr[...] = wg_ref[0, 0].astype(BF16)
            wu_scr[...] = wu_ref[0, 0].astype(BF16)
            wd_scr[...] = wd_ref[0, 0].astype(BF16)

        x_lo, x_hi = _unpack_bf16_halves(x_ref[...])
        half = x_lo.shape[1]

        def in_proj(w_scr):
            return (jnp.dot(x_lo, w_scr[0:half, :], preferred_element_type=F32)
                    + jnp.dot(x_hi, w_scr[half:, :], preferred_element_type=F32))

        a = jax.nn.silu(in_proj(wg_scr)) * in_proj(wu_scr)
        y = jnp.dot(a.astype(BF16), wd_scr[...], preferred_element_type=F32)
        y_ref[...] = _pack_bf16_halves(y.astype(BF16))


def grouped_experts(xs, tile_expert, n_valid, wg, wu, wd, layer):
    n_rows = xs.shape[0]
    d, de = wg.shape[-2:]
    tm = EXPERT_TILE
    row_map = lambda i, te, nv: (jnp.minimum(i, nv[0] - 1), 0)
    grid_spec = pltpu.PrefetchScalarGridSpec(
        num_scalar_prefetch=2,
        grid=(n_rows // tm,),
        in_specs=[
            pl.BlockSpec((tm, d // 2), row_map),
            pl.BlockSpec((1, 1, d, de), lambda i, te, nv: (layer, te[i], 0, 0)),
            pl.BlockSpec((1, 1, d, de), lambda i, te, nv: (layer, te[i], 0, 0)),
            pl.BlockSpec((1, 1, de, d), lambda i, te, nv: (layer, te[i], 0, 0)),
        ],
        out_specs=pl.BlockSpec((tm, d // 2), row_map),
        scratch_shapes=[pltpu.VMEM((d, de), BF16), pltpu.VMEM((d, de), BF16), pltpu.VMEM((de, d), BF16)],
    )
    return pl.pallas_call(
        _experts_kernel,
        out_shape=jax.ShapeDtypeStruct((n_rows, d // 2), jnp.uint32),
        grid_spec=grid_spec,
        compiler_params=_cparams("arbitrary"),
        name="grouped_experts",
    )(tile_expert, n_valid, xs, wg, wu, wd)


def _final_combine_kernel(x_ref, y_ref, w_ref, mod_ref, fg_ref, o_ctx_ref, o_dec_ref, *, n_ctx_tiles):
    out = _moe_combine(x_ref[...], y_ref, w_ref, mod_ref)
    ms = jnp.mean(out * out, axis=-1, keepdims=True)
    out = out * lax.rsqrt(ms + NORM_EPS) * fg_ref[...]

    def store(o_ref):
        o_ref[...] = out
    _by_part(pl.program_id(0) >= n_ctx_tiles, (o_ctx_ref, o_dec_ref), store)


def final_combine(lay, x, y_pair, w_col, mods, final_g):
    d = x.shape[1]
    tm = lay.tm
    return pl.pallas_call(
        functools.partial(_final_combine_kernel, n_ctx_tiles=lay.n_ctx_tiles),
        out_shape=[jax.ShapeDtypeStruct((lay.t_ctx, d), F32), jax.ShapeDtypeStruct((lay.t_dec, d), F32)],
        grid=(lay.n_tiles,),
        in_specs=[
            pl.BlockSpec((tm, d), lambda i: (i, 0)),
            pl.BlockSpec((TOP_K, tm, d // 2), lambda i: (0, i, 0)),
            pl.BlockSpec((tm, LANES), lambda i: (i, 0)),
            pl.BlockSpec((1, N_MOD, d), lambda i: (lay.mod_row(i), 0, 0)),
            pl.BlockSpec((1, d), lambda i: (0, 0)),
        ],
        out_specs=lay.row_specs([None, None], d),
        compiler_params=_cparams("arbitrary"),
        name="final_combine",
    )(x, y_pair, w_col, mods, final_g.reshape(1, d))


def group_moe(lay, x, routing, p, layer):
    t, d = x.shape
    h, idx, w_col, rank, cnt = routing
    n_sorted = TOP_K * t + N_EXPERTS * (EXPERT_TILE - 1)
    n_sorted = -(-n_sorted // EXPERT_TILE) * EXPERT_TILE
    pos, tile_expert, n_valid = dispatch_plan(idx, rank, cnt, n_sorted)
    xs = sc_scatter_rows(h, pos, n_sorted)
    ys = grouped_experts(xs, tile_expert, n_valid, p["moe_w_gate"], p["moe_w_up"], p["moe_w_down"], layer)
    return sc_gather_rows(ys, pos).reshape(TOP_K, t, d // 2), w_col


def _rope_angles(n, d):
    n_rows = n // GRID_W
    row = jnp.repeat(jnp.arange(n_rows), GRID_W).astype(F32)
    col = jnp.tile(jnp.arange(GRID_W), n_rows).astype(F32)
    nf = d // 4
    freqs = jnp.power(ROPE_BASE, -jnp.arange(nf, dtype=F32) / nf)
    ang = jnp.concatenate([row[:, None] * freqs, col[:, None] * freqs], axis=-1)
    return jnp.cos(ang), jnp.sin(ang)


def kernel(x_prompt, x_sample, cache_attn_k, cache_attn_v, state_ret_fwd, state_ret_bwd, c, c_ctx, w_ada, b_ada, norm_mix_g, norm_ffn_g, final_norm_g, da_w_qkv, da_lambda_q1, da_lambda_k1, da_lambda_q2, da_lambda_k2, da_subln_g, da_w_o, ret_w_qkv, ret_w_gate_fwd, ret_w_gate_bwd, ret_decay_fwd, ret_decay_bwd, ret_w_o, w_router, router_bias, moe_w_gate, moe_w_up, moe_w_down):
    b_ctx, n_ctx, d = x_prompt.shape
    b_dec, n_dec, _ = x_sample.shape
    past = cache_attn_k.shape[2]
    n_attn = cache_attn_k.shape[1]
    depth = w_ada.shape[0]
    assert b_dec + 1 <= MOD_ROWS
    lay = Layout(b_ctx, n_ctx, b_dec, n_dec)

    cond = jnp.zeros((MOD_ROWS, d), F32).at[0].set(c_ctx).at[1:1 + b_dec].set(c)
    mods_all = ada_modulation(cond, w_ada, b_ada)

    p = {
        "w_router_t": w_router.T.astype(BF16), "router_bias": router_bias.astype(F32),
        "moe_w_gate": moe_w_gate, "moe_w_up": moe_w_up, "moe_w_down": moe_w_down,
    }
    ret_log_decay = jnp.stack([jax.nn.log_sigmoid(ret_decay_fwd.astype(F32)),
                               jax.nn.log_sigmoid(ret_decay_bwd.astype(F32))], axis=1)
    ck_all = cache_attn_k.reshape(b_dec, n_attn, past * DA_HEADS * 2, DA_HEAD_DIM)
    cv_all = cache_attn_v.reshape(b_dec, n_attn, past * DA_HEADS, DA_V_DIM)

    x_parts = [x_prompt.reshape(lay.t_ctx, d), x_sample.reshape(lay.t_dec, d)]
    new_k, new_v, new_sf, new_sb = [], [], [], []
    pending_moe = None

    def first_kernel(*args):
        nonlocal x_parts, pending_moe
        outs = norm_mod_matmul(lay, x_parts, *args, pending_moe=pending_moe)
        if pending_moe is not None:
            *outs, x_joint = outs
            x_parts, pending_moe = [x_joint], None
        return outs

    for i in range(depth):
        mods = mods_all[i]
        j = i // 2
        if i % 2 == 0:
            lam_init = 0.8 - 0.6 * math.exp(-0.3 * i)
            qkw = DA_HEADS * 2 * DA_HEAD_DIM
            vw = DA_HEADS * DA_V_DIM
            cos, sin = _rope_angles(n_dec, DA_HEAD_DIM)
            rope = (jnp.tile(cos, (1, 4)), jnp.concatenate([-sin, sin, -sin, sin], axis=-1))
            q, k, v, k_ctx, v_ctx = first_kernel(
                norm_mix_g[i], mods, da_w_qkv[j],
                [(0, qkw, "rope64", math.log2(math.e) * DA_HEAD_DIM ** -0.5, BF16, "all"),
                 (qkw, qkw, "rope64", 1.0, BF16, "all"),
                 (2 * qkw, vw, "plain", 1.0, BF16, "all"),
                 (qkw, qkw, "plain", 1.0, F32, "ctx64"), (2 * qkw, vw, "plain", 1.0, F32, "ctx")],
                rope)
            lam_params = jnp.stack([da_lambda_q1[j], da_lambda_k1[j], da_lambda_q2[j], da_lambda_k2[j]])
            mix = [diff_attention(q, k, v, 0, b_ctx, n_ctx, lam_params, da_subln_g[j], lam_init,
                                  heads_per_step=DA_HEADS),
                   diff_attention(q, k, v, lay.t_ctx, b_dec, n_dec, lam_params, da_subln_g[j], lam_init,
                                  cache=(ck_all[:, j], cv_all[:, j]))]
            mix_w = da_w_o[j]
            new_k.append(k_ctx.reshape(b_ctx, n_ctx, DA_HEADS, 2, DA_HEAD_DIM))
            new_v.append(v_ctx.reshape(b_ctx, n_ctx, DA_HEADS, DA_V_DIM))
        else:
            kd = ret_w_qkv.shape[2] // 4
            dv = 2 * kd
            w_all = jnp.concatenate([ret_w_qkv[j], ret_w_gate_fwd[j], ret_w_gate_bwd[j]], axis=-1).astype(BF16)
            q, k, v, gates = first_kernel(
                norm_mix_g[i], mods, w_all,
                [(0, kd, "rope256", 1.0, BF16, "all"),
                 (kd, kd, "rope256", (kd // RET_HEADS) ** -0.5, F32, "all"),
                 (2 * kd, dv, "plain", 1.0, BF16, "all"), (2 * kd + dv, 2 * dv, "silu", 1.0, BF16, "all")],
                _rope_angles(n_dec, kd // RET_HEADS))
            ctx_seqs = math.gcd(b_ctx, max(1, n_dec // n_ctx // 2))
            o_ctx, sf, sb = retention(q, k, v, gates, 0, b_ctx, n_ctx, ret_log_decay[j], emit_state=True,
                                      seqs_per_step=ctx_seqs)
            (o_dec,) = retention(q, k, v, gates, lay.t_ctx, b_dec, n_dec, ret_log_decay[j],
                                 state=(state_ret_fwd[:, j:j + 1], state_ret_bwd[:, j:j + 1]))
            mix, mix_w = [o_ctx, o_dec], ret_w_o[j]
            new_sf.append(sf)
            new_sb.append(sb)
        x, *routing = proj_residual_route(lay, x_parts, mix, mix_w, mods, norm_ffn_g[i], p["w_router_t"],
                                          p["router_bias"])
        y_pair, w_col = group_moe(lay, x, routing, p, i)
        if i == depth - 1:
            x_parts = final_combine(lay, x, y_pair, w_col, mods, final_norm_g)
        else:
            x_parts, pending_moe = [x], (y_pair, w_col, mods)

    y_ctx, y_dec = x_parts
    return (y_ctx.reshape(b_ctx, n_ctx, d), y_dec.reshape(b_dec, n_dec, d),
            jnp.stack(new_k, axis=1), jnp.stack(new_v, axis=1),
            jnp.concatenate(new_sf, axis=1), jnp.concatenate(new_sb, axis=1))
```

```python
import functools
import math

import jax
import jax.numpy as jnp
from jax import lax
from jax.experimental import pallas as pl
from jax.experimental.pallas import tpu as pltpu
from jax.experimental.pallas import tpu_sc as plsc

F32 = jnp.float32
BF16 = jnp.bfloat16
I32 = jnp.int32

GRID_W = 64
ROPE_BASE = 10000.0
NORM_EPS = 1e-6
DA_HEADS = 8
DA_HEAD_DIM = 64
DA_V_DIM = 2 * DA_HEAD_DIM
RET_HEADS = 4
RET_CHUNK = 128
N_EXPERTS = 16
N_GROUPS = 4
EXPERTS_PER_GROUP = N_EXPERTS // N_GROUPS
TOP_K = 2
N_MOD = 6
MOD_ROWS = 16
LANES = 128

V7X_VMEM_LIMIT = 56 * 1024 * 1024
V7X_SC_CORES = 2
V7X_SC_SUBCORES = 16
SC_WORKERS = V7X_SC_CORES * V7X_SC_SUBCORES
SC_CHUNK_ROWS = 64

ROW_TILE = 512
EXPERT_TILE = 512
ROUTE_SPLIT = 2


def _cparams(*sem):
    return pltpu.CompilerParams(dimension_semantics=sem, vmem_limit_bytes=V7X_VMEM_LIMIT)


def _nt_dot(a, b):
    return lax.dot_general(a, b, (((1,), (1,)), ((), ())), preferred_element_type=F32)


def _tn_dot(a, b):
    return lax.dot_general(a, b, (((0,), (0,)), ((), ())), preferred_element_type=F32)


class Layout:
    def __init__(self, b_ctx, n_ctx, b_dec, n_dec):
        self.b_ctx, self.n_ctx, self.b_dec, self.n_dec = b_ctx, n_ctx, b_dec, n_dec
        self.t_ctx, self.t_dec = b_ctx * n_ctx, b_dec * n_dec
        self.t = self.t_ctx + self.t_dec
        self.tm = min(ROW_TILE, n_dec, self.t_ctx)
        assert self.t_ctx % self.tm == 0 and n_dec % self.tm == 0
        assert self.t_ctx % n_dec == 0 and self.t_ctx % n_ctx == 0
        self.n_ctx_tiles = self.t_ctx // self.tm
        self.n_tiles = self.t // self.tm

    def mod_row(self, i):
        r = i * self.tm
        return jnp.where(r < self.t_ctx, 0, 1 + (r - self.t_ctx) // self.n_dec)

    def part_tile(self, part, i):
        if part == 0:
            return jnp.minimum(i, self.n_ctx_tiles - 1)
        return jnp.maximum(i - self.n_ctx_tiles, 0)

    def row_specs(self, arrays, width):
        if len(arrays) == 1:
            return [pl.BlockSpec((self.tm, width), lambda i, *_: (i, 0))]
        return [pl.BlockSpec((self.tm, width), lambda i, *_, p=p: (self.part_tile(p, i), 0)) for p in (0, 1)]


def _ada_kernel(c_ref, w_ref, b_ref, o_ref):
    s = jax.nn.silu(c_ref[...]).astype(BF16)
    acc = jnp.dot(s, w_ref[0].astype(BF16), preferred_element_type=F32)
    o_ref[0] = acc + b_ref[0]


def ada_modulation(cond, w_ada, b_ada):
    depth, d, n = w_ada.shape
    tn = 1536
    out = pl.pallas_call(
        _ada_kernel,
        out_shape=jax.ShapeDtypeStruct((depth, MOD_ROWS, n), F32),
        grid=(depth, n // tn),
        in_specs=[
            pl.BlockSpec((MOD_ROWS, d), lambda l, j: (0, 0)),
            pl.BlockSpec((1, d, tn), lambda l, j: (l, 0, j)),
            pl.BlockSpec((1, 1, tn), lambda l, j: (l, 0, j)),
        ],
        out_specs=pl.BlockSpec((1, MOD_ROWS, tn), lambda l, j: (l, 0, j)),
        compiler_params=_cparams("parallel", "parallel"),
        name="ada_modulation",
    )(cond, w_ada, b_ada.reshape(depth, 1, n))
    return out.reshape(depth, MOD_ROWS, N_MOD, d)


def _modulated_norm(x, g, mod_ref, shift_idx):
    ms = jnp.mean(x * x, axis=-1, keepdims=True)
    y = x * lax.rsqrt(ms + NORM_EPS) * g
    return y * (1.0 + mod_ref[0, shift_idx + 1:shift_idx + 2, :]) + mod_ref[0, shift_idx:shift_idx + 1, :]


def _rope64(a, cos, sin_signed, first_half):
    partner = jnp.where(first_half, pltpu.roll(a, 96, 1), pltpu.roll(a, 32, 1))
    return a * cos + partner * sin_signed


def _moe_combine(x, y_ref, w_ref, mod_ref):
    w = w_ref[...]

    def expert_out(slot):
        lo, hi = _unpack_bf16_halves(y_ref[slot])
        return jnp.concatenate([lo.astype(F32), hi.astype(F32)], axis=1)

    return x + mod_ref[0, 5:6, :] * (w[:, 0:1] * expert_out(0) + w[:, 1:2] * expert_out(1))


def _nmm_kernel(*refs, n_x, pending_moe, cast_w, outs, tn, n_ctx_tiles):
    x_refs = refs[:n_x]
    if pending_moe:
        y_ref, wcol_ref, prev_mod_ref = refs[n_x:n_x + 3]
        refs = refs[:n_x] + refs[n_x + 3:]
    g_ref, mod_ref, w_ref, cos_ref, sin_ref = refs[n_x:n_x + 5]
    n_out = len(outs) + (1 if pending_moe else 0)
    out_refs = refs[n_x + 5:n_x + 5 + n_out]
    if cast_w:
        w_f32_ref, w_ref = w_ref, refs[n_x + 5 + n_out]

        @pl.when(pl.program_id(0) == 0)
        def _():
            w_ref[...] = w_f32_ref[...].astype(BF16)
    is_dec = pl.program_id(0) >= n_ctx_tiles
    tm = x_refs[0].shape[0]
    segments = sorted({(col0, width) for col0, width, _, _, _ in outs})

    def rotated(a, kind):
        cos, sin = cos_ref[...], sin_ref[...]
        if kind == "rope64":
            lane = lax.broadcasted_iota(I32, (tm, LANES), 1)
            first_half = (lane & 32) == 0
            return [(c * 128, _rope64(a[:, c * 128:(c + 1) * 128], cos, sin, first_half))
                    for c in range(tn // 128)]
        pieces = []
        for c in range(tn // 256):
            x1, x2 = a[:, c * 256:c * 256 + 128], a[:, c * 256 + 128:(c + 1) * 256]
            pieces += [(c * 256, x1 * cos - x2 * sin), (c * 256 + 128, x2 * cos + x1 * sin)]
        return pieces

    def emit(dec):
        x = x_refs[(1 if dec else 0) if n_x == 2 else 0][...]
        if pending_moe:
            x = _moe_combine(x, y_ref, wcol_ref, prev_mod_ref)
            out_refs[-1][...] = x
        h = _modulated_norm(x, g_ref[...], mod_ref, 0).astype(BF16)
        for col0, width in segments:
            sinks = [(o_ref, o) for o_ref, o in zip(out_refs, outs)
                     if (o[0], o[1]) == (col0, width) and not (dec and o[4] != "all")]
            for blk in range(width // tn):
                acc = jnp.dot(h, w_ref[:, col0 + blk * tn:col0 + (blk + 1) * tn], preferred_element_type=F32)
                for o_ref, (_, _, kind, scale, rows) in sinks:
                    a = acc if scale == 1.0 else acc * scale
                    if rows == "ctx64":
                        n64 = width // 64
                        for c in range(tn // 64):
                            o_ref[pl.ds(blk * tn // 64 + c, tm, stride=n64), :] = (
                                a[:, c * 64:(c + 1) * 64].astype(o_ref.dtype))
                    elif kind == "silu":
                        o_ref[:, blk * tn:(blk + 1) * tn] = jax.nn.silu(a).astype(o_ref.dtype)
                    elif kind == "plain" or rows == "ctx" or not dec:
                        o_ref[:, blk * tn:(blk + 1) * tn] = a.astype(o_ref.dtype)
                    else:
                        for off, val in rotated(a, kind):
                            o_ref[:, blk * tn + off:blk * tn + off + 128] = val.astype(o_ref.dtype)

    pl.when(jnp.logical_not(is_dec))(lambda: emit(False))
    pl.when(is_dec)(lambda: emit(True))


def norm_mod_matmul(lay, x_parts, g, mods, w, outputs, rope_tables, pending_moe=None, tn=512):
    d = x_parts[0].shape[1]
    tm = lay.tm
    n_total = w.shape[1]
    outs, out_shapes, out_specs = [], [], []
    for col0, width, kind, scale, dtype, rows in outputs:
        assert width % tn == 0 and col0 % LANES == 0
        outs.append((col0, width, kind, float(scale), rows))
        if rows == "ctx64":
            n64 = width // 64
            out_shapes.append(jax.ShapeDtypeStruct((lay.t_ctx * n64, 64), dtype))
            out_specs.append(pl.BlockSpec((tm * n64, 64), lambda i: (lay.part_tile(0, i), 0)))
            continue
        n_rows = lay.t_ctx if rows == "ctx" else lay.t
        out_shapes.append(jax.ShapeDtypeStruct((n_rows, width), dtype))
        out_specs.append(lay.row_specs([None, None], width)[0] if rows == "ctx" else lay.row_specs([None], width)[0])
    blocks_per_seq = lay.n_dec // tm
    rope_spec = pl.BlockSpec((tm, LANES), lambda i: (lay.part_tile(1, i) % blocks_per_seq, 0))
    mod_spec = pl.BlockSpec((1, N_MOD, d), lambda i: (lay.mod_row(i), 0, 0))
    moe_specs, moe_args = [], []
    if pending_moe is not None:
        assert len(x_parts) == 1
        moe_specs = [pl.BlockSpec((TOP_K, tm, d // 2), lambda i: (0, i, 0)),
                     pl.BlockSpec((tm, LANES), lambda i: (i, 0)), mod_spec]
        moe_args = list(pending_moe)
        out_shapes.append(jax.ShapeDtypeStruct((lay.t, d), F32))
        out_specs.append(pl.BlockSpec((tm, d), lambda i: (i, 0)))
    return pl.pallas_call(
        functools.partial(_nmm_kernel, n_x=len(x_parts), pending_moe=pending_moe is not None,
                          cast_w=w.dtype != BF16, outs=tuple(outs), tn=tn, n_ctx_tiles=lay.n_ctx_tiles),
        out_shape=out_shapes,
        grid=(lay.n_tiles,),
        in_specs=lay.row_specs(x_parts, d) + moe_specs + [
            pl.BlockSpec((1, d), lambda i: (0, 0)),
            mod_spec,
            pl.BlockSpec((d, n_total), lambda i: (0, 0), pipeline_mode=pl.Buffered(1)),
            rope_spec, rope_spec,
        ],
        out_specs=out_specs,
        scratch_shapes=[pltpu.VMEM(w.shape, BF16)] if w.dtype != BF16 else [],
        compiler_params=_cparams("arbitrary"),
        name="norm_mod_matmul",
    )(*x_parts, *moe_args, g.reshape(1, d), mods, w, *rope_tables)


def _diff_lambda(lam_ref, lam_init):
    lp = lam_ref[...]
    return (jnp.exp(jnp.sum(lp[0:1] * lp[1:2], axis=-1, keepdims=True))
            - jnp.exp(jnp.sum(lp[2:3] * lp[3:4], axis=-1, keepdims=True)) + lam_init)


ONES_ROWS = 16


def _scores_t(k, q, comp):
    lane = lax.broadcasted_iota(I32, q.shape, 1)
    return _nt_dot(k, jnp.where((lane < DA_HEAD_DIM) == (comp == 0), q, jnp.zeros_like(q)))


def _softmax_values_t(scores, vts):
    dv = vts[0].shape[0] - ONES_ROWS
    maxes = [jnp.max(s, axis=0, keepdims=True) for s in scores]
    exps = [jnp.exp2(s - m).astype(BF16) for s, m in zip(scores, maxes)]
    accs = [jnp.dot(vt, e, preferred_element_type=F32) for vt, e in zip(vts, exps)]
    return [acc[0:dv] / acc[dv:dv + 1] for acc in accs]


def _diff_finish_t(parts, lam, g_col, lam_init):
    o = parts[0] - lam * parts[1]
    ms = jnp.mean(o * o, axis=0, keepdims=True)
    return jnp.transpose((o * lax.rsqrt(ms + NORM_EPS) * g_col) * (1.0 - lam_init))


def _diff_attn_kernel(lam_ref, q_ref, k_ref, v_ref, g_ref, o_ref, *, lam_init, heads_per_step):
    hw = 2 * DA_HEAD_DIM
    lam = _diff_lambda(lam_ref, lam_init)
    ones = jnp.ones((ONES_ROWS, k_ref.shape[0]), BF16)
    heads = [slice(hh * hw, (hh + 1) * hw) for hh in range(heads_per_step)]
    scores = [[_scores_t(k_ref[:, cols], q_ref[:, cols], comp) for comp in range(2)] for cols in heads]
    vts = [jnp.concatenate([jnp.transpose(v_ref[:, cols].astype(F32)).astype(BF16), ones], axis=0) for cols in heads]
    parts = _softmax_values_t([s for pair in scores for s in pair], [vt for vt in vts for _ in range(2)])
    for hh, cols in enumerate(heads):
        o_ref[:, cols] = _diff_finish_t(parts[2 * hh:2 * hh + 2], lam, g_ref[...], lam_init).astype(o_ref.dtype)


def _diff_attn_cached_kernel(lam_ref, q_ref, k_ref, v_ref, ck_ref, cv_ref, g_ref, o_ref, k_scr, vt_scr, s_scr, *,
                             lam_init, sub_rows):
    seq_len, hw = q_ref.shape
    n_sub = seq_len // sub_rows
    assert n_sub % 2 == 0
    lam = _diff_lambda(lam_ref, lam_init)
    k_scr[0:seq_len, :] = k_ref[...]
    k_scr[seq_len:, :] = ck_ref[0].astype(BF16)
    vt_scr[0:hw, 0:seq_len] = jnp.transpose(v_ref[...])
    vt_scr[0:hw, seq_len:] = jnp.transpose(cv_ref[0]).astype(BF16)
    vt_scr[hw:, :] = jnp.ones((ONES_ROWS, vt_scr.shape[1]), BF16)

    def rows_of(t):
        return pl.ds(pl.multiple_of(t * sub_rows, sub_rows), sub_rows)

    def scores(t, slot, comp):
        s_scr[slot, comp] = _scores_t(k_scr[...], q_ref[rows_of(t), :], comp)

    def stage(t_next, slot_next, t, slot):
        for comp in range(2):
            scores(t_next, slot_next, comp)
        parts = _softmax_values_t([s_scr[slot, 0], s_scr[slot, 1]], [vt_scr[...]] * 2)
        o_ref[rows_of(t), :] = _diff_finish_t(parts, lam, g_ref[...], lam_init).astype(o_ref.dtype)

    scores(0, 0, 0)
    scores(0, 0, 1)

    def body(i2, carry):
        t = 2 * i2
        stage(t + 1, 1, t, 0)
        stage(jnp.minimum(t + 2, n_sub - 1), 0, t + 1, 1)
        return carry

    lax.fori_loop(0, n_sub // 2, body, 0, unroll=2 if n_sub % 4 == 0 else 1)


def diff_attention(q, k, v, row0, batch, seq_len, lam_params, subln_g, lam_init, cache=None, heads_per_step=1,
                   sub_rows=256):
    width = q.shape[1]
    hw = 2 * DA_HEAD_DIM
    bw = heads_per_step * hw
    assert row0 % seq_len == 0 and DA_HEADS % heads_per_step == 0
    s0 = row0 // seq_len
    seq_spec = pl.BlockSpec((seq_len, bw), lambda b, h: (s0 + b, h))
    in_specs = [pl.BlockSpec((4, DA_HEAD_DIM), lambda b, h: (0, 0)), seq_spec, seq_spec, seq_spec]
    args = [lam_params, q, k, v]
    scratch = []
    if cache is None:
        body = functools.partial(_diff_attn_kernel, lam_init=lam_init, heads_per_step=heads_per_step)
    else:
        assert heads_per_step == 1
        past = cache[0].shape[1]
        cache_spec = pl.BlockSpec((1, past, hw), lambda b, h: (b, 0, h))
        in_specs += [cache_spec, cache_spec]
        args += list(cache)
        sub_rows = min(sub_rows, seq_len // 2)
        scratch = [pltpu.VMEM((seq_len + past, hw), BF16), pltpu.VMEM((hw + ONES_ROWS, seq_len + past), BF16),
                   pltpu.VMEM((2, 2, seq_len + past, sub_rows), F32)]
        body = functools.partial(_diff_attn_cached_kernel, lam_init=lam_init, sub_rows=sub_rows)
    in_specs.append(pl.BlockSpec((hw, 1), lambda b, h: (0, 0)))
    args.append(subln_g.reshape(hw, 1))
    return pl.pallas_call(
        body,
        out_shape=jax.ShapeDtypeStruct((batch * seq_len, width), BF16),
        grid=(batch, DA_HEADS // heads_per_step),
        in_specs=in_specs,
        out_specs=pl.BlockSpec((seq_len, bw), lambda b, h: (b, h)),
        scratch_shapes=scratch,
        compiler_params=_cparams("parallel", "parallel"),
        name="diff_attention",
    )(*args)


def _retention_kernel(*refs, seq_len, seqs, has_state, emit_state):
    lg_ref, q_ref, k_ref, v_ref, gf_ref, gb_ref = refs[:6]
    pos = 6
    if has_state:
        s0_refs = refs[6:8]
        pos = 8
    o_ref = refs[pos]
    pos += 1
    if emit_state:
        s_out_refs = refs[pos:pos + 2]
        pos += 2
    s_scr, o_scr = refs[pos:pos + 2]

    h = pl.program_id(1)
    c_len = RET_CHUNK
    n_chunks = seq_len // c_len
    assert n_chunks % 2 == 0
    row = lax.broadcasted_iota(I32, (c_len, c_len), 0)
    colm = lax.broadcasted_iota(I32, (c_len, c_len), 1)
    rel = (row - colm).astype(F32)
    posv = lax.broadcasted_iota(I32, (c_len, 1), 0).astype(F32)

    consts = []
    for backward in (False, True):
        lg = lg_ref[1 if backward else 0, h]
        if backward:
            intra = jnp.where(rel <= 0, jnp.exp(-rel * lg), 0.0)
            q_decay = jnp.exp((c_len - posv) * lg)
            k_decay = jnp.exp(posv * lg)
        else:
            intra = jnp.where(rel >= 0, jnp.exp(rel * lg), 0.0)
            q_decay = jnp.exp((posv + 1.0) * lg)
            k_decay = jnp.exp((c_len - 1.0 - posv) * lg)
        consts.append((intra, q_decay, k_decay, jnp.exp(jnp.zeros((1, 1), F32) + c_len * lg)))
        direction = 1 if backward else 0
        for sq in range(seqs):
            if has_state:
                s_scr[sq, direction] = s0_refs[direction][sq, 0, 0]
            else:
                s_scr[sq, direction] = jnp.zeros(s_scr.shape[2:], F32)

    def body(ci, first_touch):
        chains = [(sq, direction, pl.ds(pl.multiple_of(sq * seq_len + c * c_len, c_len), c_len))
                  for sq in range(seqs) for direction, c in ((0, ci), (1, n_chunks - 1 - ci))]
        qs = [q_ref[rows, :].astype(BF16) for _, _, rows in chains]
        ks = [k_ref[rows, :].astype(F32) for _, _, rows in chains]
        vs = [v_ref[rows, :].astype(BF16) for _, _, rows in chains]
        intras = [(_nt_dot(qb, kf.astype(BF16)) * consts[d][0]).astype(BF16)
                  for (_, d, _), qb, kf in zip(chains, qs, ks)]
        states = [s_scr[sq, d] for sq, d, _ in chains]
        outs = [jnp.dot(a, vb, preferred_element_type=F32)
                + jnp.dot(qb, s.astype(BF16), preferred_element_type=F32) * consts[d][1]
                for (_, d, _), a, qb, vb, s in zip(chains, intras, qs, vs, states)]
        for (sq, d, _), kf, vb, s in zip(chains, ks, vs, states):
            s_scr[sq, d] = consts[d][3] * s + _tn_dot((kf * consts[d][2]).astype(BF16), vb)
        centred = [o - jnp.mean(o, axis=-1, keepdims=True) for o in outs]
        scales = [lax.rsqrt(jnp.mean(oc * oc, axis=-1, keepdims=True) + NORM_EPS) for oc in centred]
        for (_, d, rows), oc, scale in zip(chains, centred, scales):
            gated = oc * scale * (gb_ref if d else gf_ref)[rows, :].astype(F32)
            if first_touch:
                o_scr[rows, :] = gated
            else:
                o_ref[rows, :] = (o_scr[rows, :] + gated).astype(o_ref.dtype)

    unroll = 8 if n_chunks % 16 == 0 else 1
    lax.fori_loop(0, n_chunks // 2, lambda ci, c: (body(ci, True), c)[1], 0, unroll=unroll)
    lax.fori_loop(n_chunks // 2, n_chunks, lambda ci, c: (body(ci, False), c)[1], 0, unroll=unroll)
    if emit_state:
        for d in range(2):
            for sq in range(seqs):
                s_out_refs[d][sq, 0, 0] = s_scr[sq, d]


def retention(q, k, v, gates, row0, batch, seq_len, log_decay, state=None, emit_state=False, seqs_per_step=1):
    dk = q.shape[1] // RET_HEADS
    dv = v.shape[1] // RET_HEADS
    seqs = seqs_per_step
    rows = seqs * seq_len
    assert row0 % rows == 0 and batch % seqs == 0
    s0 = row0 // rows
    in_specs = [
        pl.BlockSpec(memory_space=pltpu.SMEM),
        pl.BlockSpec((rows, dk), lambda b, h: (s0 + b, h)),
        pl.BlockSpec((rows, dk), lambda b, h: (s0 + b, h)),
        pl.BlockSpec((rows, dv), lambda b, h: (s0 + b, h)),
        pl.BlockSpec((rows, dv), lambda b, h: (s0 + b, h)),
        pl.BlockSpec((rows, dv), lambda b, h: (s0 + b, RET_HEADS + h)),
    ]
    args = [log_decay, q, k, v, gates, gates]
    state_spec = pl.BlockSpec((seqs, 1, 1, dk, dv), lambda b, h: (b, 0, h, 0, 0))
    if state is not None:
        in_specs += [state_spec, state_spec]
        args += list(state)
    out_shapes = [jax.ShapeDtypeStruct((batch * seq_len, RET_HEADS * dv), BF16)]
    out_specs = [pl.BlockSpec((rows, dv), lambda b, h: (b, h))]
    if emit_state:
        out_shapes += [jax.ShapeDtypeStruct((batch, 1, RET_HEADS, dk, dv), F32)] * 2
        out_specs += [state_spec, state_spec]
    return pl.pallas_call(
        functools.partial(_retention_kernel, seq_len=seq_len, seqs=seqs, has_state=state is not None,
                          emit_state=emit_state),
        out_shape=out_shapes,
        grid=(batch // seqs, RET_HEADS),
        in_specs=in_specs,
        out_specs=out_specs,
        scratch_shapes=[pltpu.VMEM((seqs, 2, dk, dv), F32), pltpu.VMEM((rows, dv), F32)],
        compiler_params=_cparams("parallel", "parallel"),
        name="retention",
    )(*args)


def _pack_bf16_halves(x):
    half = x.shape[1] // 2
    lo = lax.bitcast_convert_type(x[:, :half].astype(F32), jnp.uint32)
    hi = lax.bitcast_convert_type(x[:, half:].astype(F32), jnp.uint32)
    return (hi & jnp.uint32(0xFFFF0000)) | (lo >> 16)


def _unpack_bf16_halves(p):
    lo = lax.bitcast_convert_type(p << 16, F32).astype(BF16)
    hi = lax.bitcast_convert_type(p & jnp.uint32(0xFFFF0000), F32).astype(BF16)
    return lo, hi


def _first_max_onehot(vals):
    m = vals[0]
    for v in vals[1:]:
        m = jnp.maximum(m, v)
    onehot, taken = [], None
    for v in vals:
        hit = v == m
        if taken is None:
            onehot.append(hit)
            taken = hit
        else:
            onehot.append(hit & jnp.logical_not(taken))
            taken = taken | hit
    return m, onehot


def _pick(onehot, vals):
    out = vals[-1]
    for oh, v in zip(onehot[-2::-1], vals[-2::-1]):
        out = jnp.where(oh, v, out)
    return out


def _route_rows(x, g_ref, mod_ref, wr_ref, bias_ref, tri_ref, h_ref, idx_ref, wcol_ref, rank_ref, cnt_ref,
                ind_scr, wrow_scr):
    hb = _modulated_norm(x, g_ref[...], mod_ref, 3).astype(BF16)
    h_ref[...] = _pack_bf16_halves(hb)
    logits = _nt_dot(wr_ref[...], hb)
    s = jax.nn.sigmoid(logits)
    sel = s + bias_ref[...]
    neg_inf = jnp.full_like(sel[0:1], -jnp.inf)
    sel_rows = [sel[e:e + 1] for e in range(N_EXPERTS)]
    s_rows = [s[e:e + 1] for e in range(N_EXPERTS)]

    def top2(vals):
        m1, oh1 = _first_max_onehot(vals)
        rest = [jnp.where(o, neg_inf, v) for o, v in zip(oh1, vals)]
        m2, oh2 = _first_max_onehot(rest)
        return m1, m2, oh1, oh2

    grp_scores = []
    for g in range(N_GROUPS):
        m1, m2, _, _ = top2(sel_rows[g * EXPERTS_PER_GROUP:(g + 1) * EXPERTS_PER_GROUP])
        grp_scores.append(m1 + m2)
    _, in_grp = _first_max_onehot(grp_scores)
    cand_sel = [_pick(in_grp, [sel_rows[g * EXPERTS_PER_GROUP + k] for g in range(N_GROUPS)])
                for k in range(EXPERTS_PER_GROUP)]
    cand_s = [_pick(in_grp, [s_rows[g * EXPERTS_PER_GROUP + k] for g in range(N_GROUPS)])
              for k in range(EXPERTS_PER_GROUP)]
    _, _, oh1, oh2 = top2(cand_sel)
    w1 = _pick(oh1, cand_s)
    w2 = _pick(oh2, cand_s)
    denom = w1 + w2
    wrow_scr[...] = jnp.zeros_like(wrow_scr)
    wrow_scr[0:1, :] = w1 / denom
    wrow_scr[1:2, :] = w2 / denom
    wcol_ref[...] = jnp.transpose(wrow_scr[...])

    ints = [jnp.full(w1.shape, k, I32) for k in range(EXPERTS_PER_GROUP)]
    grp = _pick(in_grp, ints) * EXPERTS_PER_GROUP
    idx_ref[0:1, :] = grp + _pick(oh1, ints)
    idx_ref[1:2, :] = grp + _pick(oh2, ints)

    one, zero = jnp.ones_like(w1), jnp.zeros_like(w1)
    for g in range(N_GROUPS):
        for k in range(EXPERTS_PER_GROUP):
            e = g * EXPERTS_PER_GROUP + k
            ind_scr[e:e + 1, :] = jnp.where(in_grp[g] & (oh1[k] | oh2[k]), one, zero)
    ind = ind_scr[...].astype(BF16)
    ranks = jnp.dot(ind, tri_ref[...], preferred_element_type=F32)
    cnt_ref[0] = jnp.dot(ind, jnp.ones((ind.shape[1], LANES), BF16), preferred_element_type=F32)
    for slot, oh in ((0, oh1), (1, oh2)):
        r = zero
        for g in range(N_GROUPS):
            for k in range(EXPERTS_PER_GROUP):
                e = g * EXPERTS_PER_GROUP + k
                r = r + jnp.where(in_grp[g] & oh[k], ranks[e:e + 1], zero)
        rank_ref[slot:slot + 1, :] = r.astype(I32)


def _proj_route_kernel(*refs, n_x, n_a, n_ctx_tiles):
    x_refs, a_refs = refs[:n_x], refs[n_x:n_x + n_a]
    w_ref, mod_ref, g_ref, wr_ref, bias_ref, tri_ref, o_ref = refs[n_x + n_a:n_x + n_a + 7]
    route_out_refs = refs[n_x + n_a + 7:n_x + n_a + 12]
    ind_scr, wrow_scr, w_scr = refs[n_x + n_a + 12:]
    is_dec = pl.program_id(0) >= n_ctx_tiles

    @pl.when(pl.program_id(0) == 0)
    def _():
        w_scr[...] = w_ref[...].astype(BF16)
    h_ref, idx_ref, wcol_ref, rank_ref, cnt_ref = route_out_refs
    sub = o_ref.shape[0] // ROUTE_SPLIT

    def body(dec):
        x_ref = x_refs[(1 if dec else 0) if n_x == 2 else 0]
        a_ref = a_refs[(1 if dec else 0) if n_a == 2 else 0]
        parts = [slice(part * sub, (part + 1) * sub) for part in range(ROUTE_SPLIT)]
        for rows in parts:
            y = jnp.dot(a_ref[rows, :].astype(BF16), w_scr[...], preferred_element_type=F32)
            o_ref[rows, :] = x_ref[rows, :] + mod_ref[0, 2:3, :] * y
        for part, rows in enumerate(parts):
            _route_rows(o_ref[rows, :], g_ref, mod_ref, wr_ref, bias_ref, tri_ref, h_ref.at[rows, :],
                        idx_ref.at[:, rows], wcol_ref.at[rows, :], rank_ref.at[:, rows], cnt_ref.at[part:part + 1],
                        ind_scr, wrow_scr)

    if n_x == 1 and n_a == 1:
        body(False)
    else:
        pl.when(jnp.logical_not(is_dec))(lambda: body(False))
        pl.when(is_dec)(lambda: body(True))


def proj_residual_route(lay, x_parts, a_parts, w, mods, ffn_g, w_router_t, router_bias):
    d = x_parts[0].shape[1]
    kdim = a_parts[0].shape[1]
    tm, t = lay.tm, lay.t
    sub = tm // ROUTE_SPLIT
    tri = jnp.triu(jnp.ones((sub, sub), BF16), k=1)
    pair = jax.ShapeDtypeStruct((TOP_K, t), I32)
    pair_spec = pl.BlockSpec((TOP_K, tm), lambda i: (0, i))
    const = lambda shape: pl.BlockSpec(shape, lambda i: (0,) * len(shape))
    return pl.pallas_call(
        functools.partial(_proj_route_kernel, n_x=len(x_parts), n_a=len(a_parts), n_ctx_tiles=lay.n_ctx_tiles),
        out_shape=[jax.ShapeDtypeStruct((t, d), F32), jax.ShapeDtypeStruct((t, d // 2), jnp.uint32), pair,
                   jax.ShapeDtypeStruct((t, LANES), F32), pair,
                   jax.ShapeDtypeStruct((lay.n_tiles * ROUTE_SPLIT, N_EXPERTS, LANES), F32)],
        grid=(lay.n_tiles,),
        in_specs=lay.row_specs(x_parts, d) + lay.row_specs(a_parts, kdim) + [
            pl.BlockSpec((kdim, d), lambda i: (0, 0), pipeline_mode=pl.Buffered(1)),
            pl.BlockSpec((1, N_MOD, d), lambda i: (lay.mod_row(i), 0, 0)),
            const((1, d)), const((N_EXPERTS, d)), const((N_EXPERTS, 1)), const((sub, sub)),
        ],
        out_specs=[pl.BlockSpec((tm, d), lambda i: (i, 0)), pl.BlockSpec((tm, d // 2), lambda i: (i, 0)), pair_spec,
                   pl.BlockSpec((tm, LANES), lambda i: (i, 0)), pair_spec,
                   pl.BlockSpec((ROUTE_SPLIT, N_EXPERTS, LANES), lambda i: (i, 0, 0))],
        scratch_shapes=[pltpu.VMEM((N_EXPERTS, sub), F32), pltpu.VMEM((LANES, sub), F32),
                        pltpu.VMEM((kdim, d), BF16)],
        compiler_params=_cparams("arbitrary"),
        name="proj_residual_route",
    )(*x_parts, *a_parts, w, mods, ffn_g.reshape(1, d), w_router_t, router_bias.reshape(N_EXPERTS, 1), tri)


def dispatch_plan(idx, rank, cnt, n_sorted):
    t = idx.shape[1]
    n_tiles, tm = cnt.shape[0], t // cnt.shape[0]
    cnt_tile = cnt[:, :, 0].astype(I32)
    total = jnp.sum(cnt_tile, axis=0)
    padded = ((total + EXPERT_TILE - 1) // EXPERT_TILE) * EXPERT_TILE
    end = jnp.cumsum(padded)
    start = end - padded
    base = start[None, :] + jnp.cumsum(cnt_tile, axis=0) - cnt_tile
    idx3 = idx.reshape(TOP_K, n_tiles, tm)
    pos = rank.reshape(TOP_K, n_tiles, tm)
    for e in range(N_EXPERTS):
        pos = pos + jnp.where(idx3 == e, base[None, :, e, None], 0)
    tile_row = jnp.arange(n_sorted // EXPERT_TILE, dtype=I32) * EXPERT_TILE
    tile_expert = jnp.minimum(jnp.sum(end[None, :] <= tile_row[:, None], axis=1), N_EXPERTS - 1).astype(I32)
    n_valid = (end[-1] // EXPERT_TILE).astype(I32).reshape(1)
    return pos.reshape(TOP_K * t).astype(I32), tile_expert, n_valid


def _sc_mesh():
    return plsc.VectorSubcoreMesh(core_axis_name="c", subcore_axis_name="s")


def _sc_worker_id():
    return lax.axis_index("s") * V7X_SC_CORES + lax.axis_index("c")


def _sc_two_stage_pipeline(n_chunks, fetch_idx, load, store):
    assert n_chunks % 2 == 0
    fetch_idx(0, 0)
    load(0, 0).start()

    @pl.loop(0, n_chunks // 2)
    def _(p):
        j = 2 * p

        @pl.when(p > 0)
        def _():
            store(j - 1, 1).wait()

        fetch_idx(j + 1, 1)
        load(j + 1, 1).start()
        load(j, 0).wait()
        store(j, 0).start()
        store(j, 0).wait()

        @pl.when(j + 2 < n_chunks)
        def _():
            fetch_idx(j + 2, 0)
            load(j + 2, 0).start()

        load(j + 1, 1).wait()
        store(j + 1, 1).start()

    store(n_chunks - 1, 1).wait()


_SC_SCRATCH = lambda chunk, d, dtype: [
    pltpu.VMEM((chunk,), I32), pltpu.VMEM((chunk,), I32),
    pltpu.VMEM((chunk, d), dtype), pltpu.VMEM((chunk, d), dtype),
    pltpu.SemaphoreType.DMA, pltpu.SemaphoreType.DMA, pltpu.SemaphoreType.DMA, pltpu.SemaphoreType.DMA]


def sc_scatter_rows(src, pos, n_out_rows):
    t, d = src.shape
    n_idx = pos.shape[0]
    per_worker = n_idx // SC_WORKERS
    chunk = SC_CHUNK_ROWS
    assert n_idx % (SC_WORKERS * chunk * 2) == 0 and t % chunk == 0

    @functools.partial(
        pl.kernel, mesh=_sc_mesh(), out_type=jax.ShapeDtypeStruct((n_out_rows, d), src.dtype),
        scratch_types=_SC_SCRATCH(chunk, d, src.dtype), name="sc_scatter_rows")
    def scatter(src_hbm, pos_hbm, out_hbm, idx_a, idx_b, rows_a, rows_b, lsem_a, lsem_b, ssem_a, ssem_b):
        base = _sc_worker_id() * per_worker
        idx, rows, lsem, ssem = (idx_a, idx_b), (rows_a, rows_b), (lsem_a, lsem_b), (ssem_a, ssem_b)

        def fetch_idx(j, b):
            pltpu.sync_copy(pos_hbm.at[pl.ds(base + j * chunk, chunk)], idx[b])

        def load(j, b):
            return pltpu.make_async_copy(src_hbm.at[pl.ds(lax.rem(base + j * chunk, t), chunk)], rows[b], lsem[b])

        def store(j, b):
            return pltpu.make_async_copy(rows[b], out_hbm.at[idx[b]], ssem[b])

        _sc_two_stage_pipeline(per_worker // chunk, fetch_idx, load, store)

    return scatter(src, pos)


def sc_gather_rows(table, idx):
    _, d = table.shape
    n_idx = idx.shape[0]
    per_worker = n_idx // SC_WORKERS
    chunk = SC_CHUNK_ROWS
    assert n_idx % (SC_WORKERS * chunk * 2) == 0

    @functools.partial(
        pl.kernel, mesh=_sc_mesh(), out_type=jax.ShapeDtypeStruct((n_idx, d), table.dtype),
        scratch_types=_SC_SCRATCH(chunk, d, table.dtype), name="sc_gather_rows")
    def gather(table_hbm, idx_hbm, out_hbm, idx_a, idx_b, rows_a, rows_b, lsem_a, lsem_b, ssem_a, ssem_b):
        base = _sc_worker_id() * per_worker
        idx, rows, lsem, ssem = (idx_a, idx_b), (rows_a, rows_b), (lsem_a, lsem_b), (ssem_a, ssem_b)

        def fetch_idx(j, b):
            pltpu.sync_copy(idx_hbm.at[pl.ds(base + j * chunk, chunk)], idx[b])

        def load(j, b):
            return pltpu.make_async_copy(table_hbm.at[idx[b]], rows[b], lsem[b])

        def store(j, b):
            return pltpu.make_async_copy(rows[b], out_hbm.at[pl.ds(base + j * chunk, chunk)], ssem[b])

        _sc_two_stage_pipeline(per_worker // chunk, fetch_idx, load, store)

    return gather(table, idx)


def _experts_kernel(te_ref, nv_ref, x_ref, wg_ref, wu_ref, wd_ref, y_ref, wg_scr, wu_scr, wd_scr):
    i = pl.program_id(0)

    @pl.when(i < nv_ref[0])
    def _():
        @pl.when((i == 0) | (te_ref[i] != te_ref[jnp.maximum(i - 1, 0)]))
        def _():
            wg_scr[...] = wg_ref[0, 0].astype(BF16)
            wu_scr[...] = wu_ref[0, 0].astype(BF16)
            wd_scr[...] = wd_ref[0, 0].astype(BF16)

        x_lo, x_hi = _unpack_bf16_halves(x_ref[...])
        half = x_lo.shape[1]

        def in_proj(w_scr):
            return (jnp.dot(x_lo, w_scr[0:half, :], preferred_element_type=F32)
                    + jnp.dot(x_hi, w_scr[half:, :], preferred_element_type=F32))

        a = jax.nn.silu(in_proj(wg_scr)) * in_proj(wu_scr)
        y = jnp.dot(a.astype(BF16), wd_scr[...], preferred_element_type=F32)
        y_ref[...] = _pack_bf16_halves(y.astype(BF16))


def grouped_experts(xs, tile_expert, n_valid, wg, wu, wd, layer):
    n_rows = xs.shape[0]
    d, de = wg.shape[-2:]
    tm = EXPERT_TILE
    row_map = lambda i, te, nv: (jnp.minimum(i, nv[0] - 1), 0)
    grid_spec = pltpu.PrefetchScalarGridSpec(
        num_scalar_prefetch=2,
        grid=(n_rows // tm,),
        in_specs=[
            pl.BlockSpec((tm, d // 2), row_map),
            pl.BlockSpec((1, 1, d, de), lambda i, te, nv: (layer, te[i], 0, 0)),
            pl.BlockSpec((1, 1, d, de), lambda i, te, nv: (layer, te[i], 0, 0)),
            pl.BlockSpec((1, 1, de, d), lambda i, te, nv: (layer, te[i], 0, 0)),
        ],
        out_specs=pl.BlockSpec((tm, d // 2), row_map),
        scratch_shapes=[pltpu.VMEM((d, de), BF16), pltpu.VMEM((d, de), BF16), pltpu.VMEM((de, d), BF16)],
    )
    return pl.pallas_call(
        _experts_kernel,
        out_shape=jax.ShapeDtypeStruct((n_rows, d // 2), jnp.uint32),
        grid_spec=grid_spec,
        compiler_params=_cparams("arbitrary"),
        name="grouped_experts",
    )(tile_expert, n_valid, xs, wg, wu, wd)


def _final_combine_kernel(x_ref, y_ref, w_ref, mod_ref, fg_ref, o_ref):
    out = _moe_combine(x_ref[...], y_ref, w_ref, mod_ref)
    ms = jnp.mean(out * out, axis=-1, keepdims=True)
    o_ref[...] = out * lax.rsqrt(ms + NORM_EPS) * fg_ref[...]


def final_combine(lay, x, y_parts, w_col, mods, final_g):
    d = x.shape[1]
    tm = lay.tm
    outs = []
    for tile0, y_pair in zip((0, lay.n_ctx_tiles), y_parts):
        rows = y_pair.shape[1]
        outs.append(pl.pallas_call(
            _final_combine_kernel,
            out_shape=jax.ShapeDtypeStruct((rows, d), F32),
            grid=(rows // tm,),
            in_specs=[
                pl.BlockSpec((tm, d), lambda i, tile0=tile0: (tile0 + i, 0)),
                pl.BlockSpec((TOP_K, tm, d // 2), lambda i: (0, i, 0)),
                pl.BlockSpec((tm, LANES), lambda i, tile0=tile0: (tile0 + i, 0)),
                pl.BlockSpec((1, N_MOD, d), lambda i, tile0=tile0: (lay.mod_row(tile0 + i), 0, 0)),
                pl.BlockSpec((1, d), lambda i: (0, 0)),
            ],
            out_specs=pl.BlockSpec((tm, d), lambda i: (i, 0)),
            compiler_params=_cparams("parallel"),
            name="final_combine",
        )(x, y_pair, w_col, mods, final_g.reshape(1, d)))
    return outs


def group_moe(lay, x, routing, p, layer, gather_by_part):
    t, d = x.shape
    h, idx, w_col, rank, cnt = routing
    n_sorted = TOP_K * t + N_EXPERTS * (EXPERT_TILE - 1)
    n_sorted = -(-n_sorted // EXPERT_TILE) * EXPERT_TILE
    pos, tile_expert, n_valid = dispatch_plan(idx, rank, cnt, n_sorted)
    xs = sc_scatter_rows(h, pos, n_sorted)
    ys = grouped_experts(xs, tile_expert, n_valid, p["moe_w_gate"], p["moe_w_up"], p["moe_w_down"], layer)
    if not gather_by_part:
        return sc_gather_rows(ys, pos).reshape(TOP_K, t, d // 2), w_col
    pos = pos.reshape(TOP_K, t)
    return [sc_gather_rows(ys, pos[:, r0:r1].reshape(-1)).reshape(TOP_K, r1 - r0, d // 2)
            for r0, r1 in ((0, lay.t_ctx), (lay.t_ctx, t))], w_col


def _rope_angles(n, d):
    n_rows = n // GRID_W
    row = jnp.repeat(jnp.arange(n_rows), GRID_W).astype(F32)
    col = jnp.tile(jnp.arange(GRID_W), n_rows).astype(F32)
    nf = d // 4
    freqs = jnp.power(ROPE_BASE, -jnp.arange(nf, dtype=F32) / nf)
    ang = jnp.concatenate([row[:, None] * freqs, col[:, None] * freqs], axis=-1)
    return jnp.cos(ang), jnp.sin(ang)


def kernel(x_prompt, x_sample, cache_attn_k, cache_attn_v, state_ret_fwd, state_ret_bwd, c, c_ctx, w_ada, b_ada, norm_mix_g, norm_ffn_g, final_norm_g, da_w_qkv, da_lambda_q1, da_lambda_k1, da_lambda_q2, da_lambda_k2, da_subln_g, da_w_o, ret_w_qkv, ret_w_gate_fwd, ret_w_gate_bwd, ret_decay_fwd, ret_decay_bwd, ret_w_o, w_router, router_bias, moe_w_gate, moe_w_up, moe_w_down):
    b_ctx, n_ctx, d = x_prompt.shape
    b_dec, n_dec, _ = x_sample.shape
    past = cache_attn_k.shape[2]
    n_attn = cache_attn_k.shape[1]
    depth = w_ada.shape[0]
    assert b_dec + 1 <= MOD_ROWS
    lay = Layout(b_ctx, n_ctx, b_dec, n_dec)

    cond = jnp.zeros((MOD_ROWS, d), F32).at[0].set(c_ctx).at[1:1 + b_dec].set(c)
    mods_all = ada_modulation(cond, w_ada, b_ada)

    p = {
        "w_router_t": w_router.T.astype(BF16), "router_bias": router_bias.astype(F32),
        "moe_w_gate": moe_w_gate, "moe_w_up": moe_w_up, "moe_w_down": moe_w_down,
    }
    ret_log_decay = jnp.stack([jax.nn.log_sigmoid(ret_decay_fwd.astype(F32)),
                               jax.nn.log_sigmoid(ret_decay_bwd.astype(F32))], axis=1)
    ck_all = cache_attn_k.reshape(b_dec, n_attn, past, -1)
    cv_all = cache_attn_v.reshape(b_dec, n_attn, past, -1)

    x_parts = [x_prompt.reshape(lay.t_ctx, d), x_sample.reshape(lay.t_dec, d)]
    new_k, new_v, new_sf, new_sb = [], [], [], []
    pending_moe = None

    def first_kernel(*args):
        nonlocal x_parts, pending_moe
        outs = norm_mod_matmul(lay, x_parts, *args, pending_moe=pending_moe)
        if pending_moe is not None:
            *outs, x_joint = outs
            x_parts, pending_moe = [x_joint], None
        return outs

    for i in range(depth):
        mods = mods_all[i]
        j = i // 2
        if i % 2 == 0:
            lam_init = 0.8 - 0.6 * math.exp(-0.3 * i)
            qkw = DA_HEADS * 2 * DA_HEAD_DIM
            vw = DA_HEADS * DA_V_DIM
            cos, sin = _rope_angles(n_dec, DA_HEAD_DIM)
            rope = (jnp.tile(cos, (1, 4)), jnp.concatenate([-sin, sin, -sin, sin], axis=-1))
            q, k, v, k_ctx, v_ctx = first_kernel(
                norm_mix_g[i], mods, da_w_qkv[j],
                [(0, qkw, "rope64", math.log2(math.e) * DA_HEAD_DIM ** -0.5, BF16, "all"),
                 (qkw, qkw, "rope64", 1.0, BF16, "all"),
                 (2 * qkw, vw, "plain", 1.0, BF16, "all"),
                 (qkw, qkw, "plain", 1.0, F32, "ctx64"), (2 * qkw, vw, "plain", 1.0, F32, "ctx")],
                rope)
            lam_params = jnp.stack([da_lambda_q1[j], da_lambda_k1[j], da_lambda_q2[j], da_lambda_k2[j]])
            mix = [diff_attention(q, k, v, 0, b_ctx, n_ctx, lam_params, da_subln_g[j], lam_init,
                                  heads_per_step=DA_HEADS),
                   diff_attention(q, k, v, lay.t_ctx, b_dec, n_dec, lam_params, da_subln_g[j], lam_init,
                                  cache=(ck_all[:, j], cv_all[:, j]))]
            mix_w = da_w_o[j]
            new_k.append(k_ctx.reshape(b_ctx, n_ctx, DA_HEADS, 2, DA_HEAD_DIM))
            new_v.append(v_ctx.reshape(b_ctx, n_ctx, DA_HEADS, DA_V_DIM))
        else:
            kd = ret_w_qkv.shape[2] // 4
            dv = 2 * kd
            w_all = jnp.concatenate([ret_w_qkv[j], ret_w_gate_fwd[j], ret_w_gate_bwd[j]], axis=-1).astype(BF16)
            q, k, v, gates = first_kernel(
                norm_mix_g[i], mods, w_all,
                [(0, kd, "rope256", 1.0, BF16, "all"),
                 (kd, kd, "rope256", (kd // RET_HEADS) ** -0.5, F32, "all"),
                 (2 * kd, dv, "plain", 1.0, BF16, "all"), (2 * kd + dv, 2 * dv, "silu", 1.0, BF16, "all")],
                _rope_angles(n_dec, kd // RET_HEADS))
            ctx_seqs = math.gcd(b_ctx, max(1, n_dec // n_ctx // 2))
            o_ctx, sf, sb = retention(q, k, v, gates, 0, b_ctx, n_ctx, ret_log_decay[j], emit_state=True,
                                      seqs_per_step=ctx_seqs)
            (o_dec,) = retention(q, k, v, gates, lay.t_ctx, b_dec, n_dec, ret_log_decay[j],
                                 state=(state_ret_fwd[:, j:j + 1], state_ret_bwd[:, j:j + 1]))
            mix, mix_w = [o_ctx, o_dec], ret_w_o[j]
            new_sf.append(sf)
            new_sb.append(sb)
        x, *routing = proj_residual_route(lay, x_parts, mix, mix_w, mods, norm_ffn_g[i], p["w_router_t"],
                                          p["router_bias"])
        y_pair, w_col = group_moe(lay, x, routing, p, i, gather_by_part=(i == depth - 1))
        if i == depth - 1:
            x_parts = final_combine(lay, x, y_pair, w_col, mods, final_norm_g)
        else:
            x_parts, pending_moe = [x], (y_pair, w_col, mods)

    y_ctx, y_dec = x_parts
    return (y_ctx.reshape(b_ctx, n_ctx, d), y_dec.reshape(b_dec, n_dec, d),
            jnp.stack(new_k, axis=1), jnp.stack(new_v, axis=1),
            jnp.concatenate(new_sf, axis=1), jnp.concatenate(new_sb, axis=1))
```

```python
import functools
import math

import jax
import jax.numpy as jnp
from jax import lax
from jax.experimental import pallas as pl
from jax.experimental.pallas import tpu as pltpu
from jax.experimental.pallas import tpu_sc as plsc

F32 = jnp.float32
BF16 = jnp.bfloat16
I32 = jnp.int32

GRID_W = 64
ROPE_BASE = 10000.0
NORM_EPS = 1e-6
DA_HEADS = 8
DA_HEAD_DIM = 64
DA_V_DIM = 2 * DA_HEAD_DIM
RET_HEADS = 4
RET_CHUNK = 128
N_EXPERTS = 16
N_GROUPS = 4
EXPERTS_PER_GROUP = N_EXPERTS // N_GROUPS
TOP_K = 2
N_MOD = 6
MOD_ROWS = 16
LANES = 128

V7X_VMEM_LIMIT = 56 * 1024 * 1024
V7X_SC_CORES = 2
V7X_SC_SUBCORES = 16
SC_WORKERS = V7X_SC_CORES * V7X_SC_SUBCORES
SC_CHUNK_ROWS = 64

ROW_TILE = 512
EXPERT_TILE = 512
ROUTE_SPLIT = 2


def _cparams(*sem):
    return pltpu.CompilerParams(dimension_semantics=sem, vmem_limit_bytes=V7X_VMEM_LIMIT)


def _nt_dot(a, b):
    return lax.dot_general(a, b, (((1,), (1,)), ((), ())), preferred_element_type=F32)


def _tn_dot(a, b):
    return lax.dot_general(a, b, (((0,), (0,)), ((), ())), preferred_element_type=F32)


class Layout:
    def __init__(self, b_ctx, n_ctx, b_dec, n_dec):
        self.b_ctx, self.n_ctx, self.b_dec, self.n_dec = b_ctx, n_ctx, b_dec, n_dec
        self.t_ctx, self.t_dec = b_ctx * n_ctx, b_dec * n_dec
        self.t = self.t_ctx + self.t_dec
        self.tm = min(ROW_TILE, n_dec, self.t_ctx)
        assert self.t_ctx % self.tm == 0 and n_dec % self.tm == 0
        assert self.t_ctx % n_dec == 0 and self.t_ctx % n_ctx == 0
        self.n_ctx_tiles = self.t_ctx // self.tm
        self.n_tiles = self.t // self.tm

    def mod_row(self, i):
        r = i * self.tm
        return jnp.where(r < self.t_ctx, 0, 1 + (r - self.t_ctx) // self.n_dec)

    def part_tile(self, part, i):
        if part == 0:
            return jnp.minimum(i, self.n_ctx_tiles - 1)
        return jnp.maximum(i - self.n_ctx_tiles, 0)

    def row_specs(self, arrays, width):
        if len(arrays) == 1:
            return [pl.BlockSpec((self.tm, width), lambda i, *_: (i, 0))]
        return [pl.BlockSpec((self.tm, width), lambda i, *_, p=p: (self.part_tile(p, i), 0)) for p in (0, 1)]


def _ada_kernel(c_ref, w_ref, b_ref, o_ref):
    s = jax.nn.silu(c_ref[...]).astype(BF16)
    acc = jnp.dot(s, w_ref[0].astype(BF16), preferred_element_type=F32)
    o_ref[0] = acc + b_ref[0]


def ada_modulation(cond, w_ada, b_ada):
    depth, d, n = w_ada.shape
    tn = 1536
    out = pl.pallas_call(
        _ada_kernel,
        out_shape=jax.ShapeDtypeStruct((depth, MOD_ROWS, n), F32),
        grid=(depth, n // tn),
        in_specs=[
            pl.BlockSpec((MOD_ROWS, d), lambda l, j: (0, 0)),
            pl.BlockSpec((1, d, tn), lambda l, j: (l, 0, j)),
            pl.BlockSpec((1, 1, tn), lambda l, j: (l, 0, j)),
        ],
        out_specs=pl.BlockSpec((1, MOD_ROWS, tn), lambda l, j: (l, 0, j)),
        compiler_params=_cparams("parallel", "parallel"),
        name="ada_modulation",
    )(cond, w_ada, b_ada.reshape(depth, 1, n))
    return out.reshape(depth, MOD_ROWS, N_MOD, d)


def _modulated_norm(x, g, mod_ref, shift_idx):
    ms = jnp.mean(x * x, axis=-1, keepdims=True)
    y = x * lax.rsqrt(ms + NORM_EPS) * g
    return y * (1.0 + mod_ref[0, shift_idx + 1:shift_idx + 2, :]) + mod_ref[0, shift_idx:shift_idx + 1, :]


def _by_part(is_dec, refs, fn):
    if len(refs) == 1:
        fn(refs[0])
        return
    pl.when(jnp.logical_not(is_dec))(lambda: fn(refs[0]))
    pl.when(is_dec)(lambda: fn(refs[1]))


def _rope64(a, cos, sin_signed, first_half):
    partner = jnp.where(first_half, pltpu.roll(a, 96, 1), pltpu.roll(a, 32, 1))
    return a * cos + partner * sin_signed


def _moe_combine(x, y_ref, w_ref, mod_ref):
    w = w_ref[...]

    def expert_out(slot):
        lo, hi = _unpack_bf16_halves(y_ref[slot])
        return jnp.concatenate([lo.astype(F32), hi.astype(F32)], axis=1)

    return x + mod_ref[0, 5:6, :] * (w[:, 0:1] * expert_out(0) + w[:, 1:2] * expert_out(1))


def _nmm_kernel(*refs, n_x, pending_moe, cast_w, outs, tn, n_ctx_tiles):
    x_refs = refs[:n_x]
    if pending_moe:
        y_ref, wcol_ref, prev_mod_ref = refs[n_x:n_x + 3]
        refs = refs[:n_x] + refs[n_x + 3:]
    g_ref, mod_ref, w_ref, cos_ref, sin_ref = refs[n_x:n_x + 5]
    n_out = len(outs) + (1 if pending_moe else 0)
    out_refs = refs[n_x + 5:n_x + 5 + n_out]
    if cast_w:
        w_f32_ref, w_ref = w_ref, refs[n_x + 5 + n_out]

        @pl.when(pl.program_id(0) == 0)
        def _():
            w_ref[...] = w_f32_ref[...].astype(BF16)
    is_dec = pl.program_id(0) >= n_ctx_tiles
    tm = x_refs[0].shape[0]
    segments = sorted({(col0, width) for col0, width, _, _, _ in outs})

    def rotated(a, kind, rows):
        cos, sin = cos_ref[rows, :], sin_ref[rows, :]
        if kind == "rope64":
            lane = lax.broadcasted_iota(I32, cos.shape, 1)
            first_half = (lane & 32) == 0
            return [(c * 128, _rope64(a[:, c * 128:(c + 1) * 128], cos, sin, first_half))
                    for c in range(tn // 128)]
        pieces = []
        for c in range(tn // 256):
            x1, x2 = a[:, c * 256:c * 256 + 128], a[:, c * 256 + 128:(c + 1) * 256]
            pieces += [(c * 256, x1 * cos - x2 * sin), (c * 256 + 128, x2 * cos + x1 * sin)]
        return pieces

    def emit(dec):
        x_ref = x_refs[(1 if dec else 0) if n_x == 2 else 0]
        for r0 in (0, tm // 2):
            rows = slice(r0, r0 + tm // 2)
            x = x_ref[rows, :]
            if pending_moe:
                x = _moe_combine(x, y_ref.at[:, rows, :], wcol_ref.at[rows, :], prev_mod_ref)
                out_refs[-1][rows, :] = x
            h = _modulated_norm(x, g_ref[...], mod_ref, 0).astype(BF16)
            for col0, width in segments:
                sinks = [(o_ref, o) for o_ref, o in zip(out_refs, outs)
                         if (o[0], o[1]) == (col0, width) and not (dec and o[4] != "all")]
                for blk in range(width // tn):
                    cols = slice(blk * tn, (blk + 1) * tn)
                    acc = jnp.dot(h, w_ref[:, col0 + blk * tn:col0 + (blk + 1) * tn], preferred_element_type=F32)
                    for o_ref, (_, _, kind, scale, out_rows) in sinks:
                        a = acc if scale == 1.0 else acc * scale
                        if out_rows == "ctx64":
                            n64 = width // 64
                            for c in range(tn // 64):
                                o_ref[pl.ds(r0 * n64 + blk * tn // 64 + c, tm // 2, stride=n64), :] = (
                                    a[:, c * 64:(c + 1) * 64].astype(o_ref.dtype))
                        elif kind == "silu":
                            o_ref[rows, cols] = jax.nn.silu(a).astype(o_ref.dtype)
                        elif kind == "plain" or out_rows == "ctx" or not dec:
                            o_ref[rows, cols] = a.astype(o_ref.dtype)
                        else:
                            for off, val in rotated(a, kind, rows):
                                o_ref[rows, blk * tn + off:blk * tn + off + 128] = val.astype(o_ref.dtype)

    pl.when(jnp.logical_not(is_dec))(lambda: emit(False))
    pl.when(is_dec)(lambda: emit(True))


def norm_mod_matmul(lay, x_parts, g, mods, w, outputs, rope_tables, pending_moe=None, tn=512):
    d = x_parts[0].shape[1]
    tm = lay.tm
    n_total = w.shape[1]
    outs, out_shapes, out_specs = [], [], []
    for col0, width, kind, scale, dtype, rows in outputs:
        assert width % tn == 0 and col0 % LANES == 0
        outs.append((col0, width, kind, float(scale), rows))
        if rows == "ctx64":
            n64 = width // 64
            out_shapes.append(jax.ShapeDtypeStruct((lay.t_ctx * n64, 64), dtype))
            out_specs.append(pl.BlockSpec((tm * n64, 64), lambda i: (lay.part_tile(0, i), 0)))
            continue
        n_rows = lay.t_ctx if rows == "ctx" else lay.t
        out_shapes.append(jax.ShapeDtypeStruct((n_rows, width), dtype))
        out_specs.append(lay.row_specs([None, None], width)[0] if rows == "ctx" else lay.row_specs([None], width)[0])
    blocks_per_seq = lay.n_dec // tm
    rope_spec = pl.BlockSpec((tm, LANES), lambda i: (lay.part_tile(1, i) % blocks_per_seq, 0))
    mod_spec = pl.BlockSpec((1, N_MOD, d), lambda i: (lay.mod_row(i), 0, 0))
    moe_specs, moe_args = [], []
    if pending_moe is not None:
        assert len(x_parts) == 1
        moe_specs = [pl.BlockSpec((TOP_K, tm, d // 2), lambda i: (0, i, 0)),
                     pl.BlockSpec((tm, LANES), lambda i: (i, 0)), mod_spec]
        moe_args = list(pending_moe)
        out_shapes.append(jax.ShapeDtypeStruct((lay.t, d), F32))
        out_specs.append(pl.BlockSpec((tm, d), lambda i: (i, 0)))
    return pl.pallas_call(
        functools.partial(_nmm_kernel, n_x=len(x_parts), pending_moe=pending_moe is not None,
                          cast_w=w.dtype != BF16, outs=tuple(outs), tn=tn, n_ctx_tiles=lay.n_ctx_tiles),
        out_shape=out_shapes,
        grid=(lay.n_tiles,),
        in_specs=lay.row_specs(x_parts, d) + moe_specs + [
            pl.BlockSpec((1, d), lambda i: (0, 0)),
            mod_spec,
            pl.BlockSpec((d, n_total), lambda i: (0, 0), pipeline_mode=pl.Buffered(1)),
            rope_spec, rope_spec,
        ],
        out_specs=out_specs,
        scratch_shapes=[pltpu.VMEM(w.shape, BF16)] if w.dtype != BF16 else [],
        compiler_params=_cparams("arbitrary"),
        name="norm_mod_matmul",
    )(*x_parts, *moe_args, g.reshape(1, d), mods, w, *rope_tables)


def _diff_lambda(lam_ref, lam_init):
    lp = lam_ref[...]
    return (jnp.exp(jnp.sum(lp[0:1] * lp[1:2], axis=-1, keepdims=True))
            - jnp.exp(jnp.sum(lp[2:3] * lp[3:4], axis=-1, keepdims=True)) + lam_init)


ONES_ROWS = 16


def _scores_t(k, q, comp):
    lane = lax.broadcasted_iota(I32, q.shape, 1)
    return _nt_dot(k, jnp.where((lane < DA_HEAD_DIM) == (comp == 0), q, jnp.zeros_like(q)))


def _softmax_values_t(scores, vts):
    dv = vts[0].shape[0] - ONES_ROWS
    maxes = [jnp.max(s, axis=0, keepdims=True) for s in scores]
    exps = [jnp.exp2(s - m).astype(BF16) for s, m in zip(scores, maxes)]
    accs = [jnp.dot(vt, e, preferred_element_type=F32) for vt, e in zip(vts, exps)]
    return [acc[0:dv] / acc[dv:dv + 1] for acc in accs]


def _diff_finish_t(parts, lam, g_col, lam_init):
    o = parts[0] - lam * parts[1]
    ms = jnp.mean(o * o, axis=0, keepdims=True)
    return jnp.transpose((o * lax.rsqrt(ms + NORM_EPS) * g_col) * (1.0 - lam_init))


def _diff_attn_kernel(lam_ref, q_ref, k_ref, v_ref, g_ref, o_ref, *, lam_init, heads_per_step):
    hw = 2 * DA_HEAD_DIM
    lam = _diff_lambda(lam_ref, lam_init)
    ones = jnp.ones((ONES_ROWS, k_ref.shape[0]), BF16)
    heads = [slice(hh * hw, (hh + 1) * hw) for hh in range(heads_per_step)]
    scores = [[_scores_t(k_ref[:, cols], q_ref[:, cols], comp) for comp in range(2)] for cols in heads]
    vts = [jnp.concatenate([jnp.transpose(v_ref[:, cols].astype(F32)).astype(BF16), ones], axis=0) for cols in heads]
    parts = _softmax_values_t([s for pair in scores for s in pair], [vt for vt in vts for _ in range(2)])
    for hh, cols in enumerate(heads):
        o_ref[:, cols] = _diff_finish_t(parts[2 * hh:2 * hh + 2], lam, g_ref[...], lam_init).astype(o_ref.dtype)


def _diff_attn_cached_kernel(lam_ref, q_ref, k_ref, v_ref, ck_ref, cv_ref, g_ref, o_ref, k_scr, vt_scr, s_scr, *,
                             lam_init, sub_rows):
    seq_len, hw = q_ref.shape
    n_sub = seq_len // sub_rows
    assert n_sub % 2 == 0
    lam = _diff_lambda(lam_ref, lam_init)
    k_scr[0:seq_len, :] = k_ref[...]
    k_scr[seq_len:, :] = ck_ref[0].astype(BF16)
    vt_scr[0:hw, 0:seq_len] = jnp.transpose(v_ref[...])
    vt_scr[0:hw, seq_len:] = jnp.transpose(cv_ref[0]).astype(BF16)
    vt_scr[hw:, :] = jnp.ones((ONES_ROWS, vt_scr.shape[1]), BF16)

    def rows_of(t):
        return pl.ds(pl.multiple_of(t * sub_rows, sub_rows), sub_rows)

    def scores(t, slot, comp):
        s_scr[slot, comp] = _scores_t(k_scr[...], q_ref[rows_of(t), :], comp)

    def stage(t_next, slot_next, t, slot):
        for comp in range(2):
            scores(t_next, slot_next, comp)
        parts = _softmax_values_t([s_scr[slot, 0], s_scr[slot, 1]], [vt_scr[...]] * 2)
        o_ref[rows_of(t), :] = _diff_finish_t(parts, lam, g_ref[...], lam_init).astype(o_ref.dtype)

    scores(0, 0, 0)
    scores(0, 0, 1)

    def body(i2, carry):
        t = 2 * i2
        stage(t + 1, 1, t, 0)
        stage(jnp.minimum(t + 2, n_sub - 1), 0, t + 1, 1)
        return carry

    lax.fori_loop(0, n_sub // 2, body, 0, unroll=2 if n_sub % 4 == 0 else 1)


def diff_attention(q, k, v, row0, batch, seq_len, lam_params, subln_g, lam_init, cache=None, heads_per_step=1,
                   sub_rows=256):
    width = q.shape[1]
    hw = 2 * DA_HEAD_DIM
    bw = heads_per_step * hw
    assert row0 % seq_len == 0 and DA_HEADS % heads_per_step == 0
    s0 = row0 // seq_len
    seq_spec = pl.BlockSpec((seq_len, bw), lambda b, h: (s0 + b, h))
    in_specs = [pl.BlockSpec((4, DA_HEAD_DIM), lambda b, h: (0, 0)), seq_spec, seq_spec, seq_spec]
    args = [lam_params, q, k, v]
    scratch = []
    if cache is None:
        body = functools.partial(_diff_attn_kernel, lam_init=lam_init, heads_per_step=heads_per_step)
    else:
        assert heads_per_step == 1
        past = cache[0].shape[1]
        cache_spec = pl.BlockSpec((1, past, hw), lambda b, h: (b, 0, h))
        in_specs += [cache_spec, cache_spec]
        args += list(cache)
        sub_rows = min(sub_rows, seq_len // 2)
        scratch = [pltpu.VMEM((seq_len + past, hw), BF16), pltpu.VMEM((hw + ONES_ROWS, seq_len + past), BF16),
                   pltpu.VMEM((2, 2, seq_len + past, sub_rows), F32)]
        body = functools.partial(_diff_attn_cached_kernel, lam_init=lam_init, sub_rows=sub_rows)
    in_specs.append(pl.BlockSpec((hw, 1), lambda b, h: (0, 0)))
    args.append(subln_g.reshape(hw, 1))
    return pl.pallas_call(
        body,
        out_shape=jax.ShapeDtypeStruct((batch * seq_len, width), BF16),
        grid=(batch, DA_HEADS // heads_per_step),
        in_specs=in_specs,
        out_specs=pl.BlockSpec((seq_len, bw), lambda b, h: (b, h)),
        scratch_shapes=scratch,
        compiler_params=_cparams("parallel", "parallel"),
        name="diff_attention",
    )(*args)


def _retention_kernel(*refs, seq_len, seqs, has_state, emit_state):
    lg_ref, q_ref, k_ref, v_ref, gf_ref, gb_ref = refs[:6]
    pos = 6
    if has_state:
        s0_refs = refs[6:8]
        pos = 8
    o_ref = refs[pos]
    pos += 1
    if emit_state:
        s_out_refs = refs[pos:pos + 2]
        pos += 2
    s_scr, o_scr = refs[pos:pos + 2]

    h = pl.program_id(1)
    c_len = RET_CHUNK
    n_chunks = seq_len // c_len
    assert n_chunks % 2 == 0
    row = lax.broadcasted_iota(I32, (c_len, c_len), 0)
    colm = lax.broadcasted_iota(I32, (c_len, c_len), 1)
    rel = (row - colm).astype(F32)
    posv = lax.broadcasted_iota(I32, (c_len, 1), 0).astype(F32)

    consts = []
    for backward in (False, True):
        lg = lg_ref[1 if backward else 0, h]
        if backward:
            intra = jnp.where(rel <= 0, jnp.exp(-rel * lg), 0.0)
            q_decay = jnp.exp((c_len - posv) * lg)
            k_decay = jnp.exp(posv * lg)
        else:
            intra = jnp.where(rel >= 0, jnp.exp(rel * lg), 0.0)
            q_decay = jnp.exp((posv + 1.0) * lg)
            k_decay = jnp.exp((c_len - 1.0 - posv) * lg)
        consts.append((intra, q_decay, k_decay, jnp.exp(jnp.zeros((1, 1), F32) + c_len * lg)))
        direction = 1 if backward else 0
        for sq in range(seqs):
            if has_state:
                s_scr[sq, direction] = s0_refs[direction][sq, 0, 0]
            else:
                s_scr[sq, direction] = jnp.zeros(s_scr.shape[2:], F32)

    def body(ci, first_touch):
        chains = [(sq, direction, pl.ds(pl.multiple_of(sq * seq_len + c * c_len, c_len), c_len))
                  for sq in range(seqs) for direction, c in ((0, ci), (1, n_chunks - 1 - ci))]
        qs = [q_ref[rows, :].astype(BF16) for _, _, rows in chains]
        ks = [k_ref[rows, :].astype(F32) for _, _, rows in chains]
        vs = [v_ref[rows, :].astype(BF16) for _, _, rows in chains]
        intras = [(_nt_dot(qb, kf.astype(BF16)) * consts[d][0]).astype(BF16)
                  for (_, d, _), qb, kf in zip(chains, qs, ks)]
        states = [s_scr[sq, d] for sq, d, _ in chains]
        outs = [jnp.dot(a, vb, preferred_element_type=F32)
                + jnp.dot(qb, s.astype(BF16), preferred_element_type=F32) * consts[d][1]
                for (_, d, _), a, qb, vb, s in zip(chains, intras, qs, vs, states)]
        for (sq, d, _), kf, vb, s in zip(chains, ks, vs, states):
            s_scr[sq, d] = consts[d][3] * s + _tn_dot((kf * consts[d][2]).astype(BF16), vb)
        centred = [o - jnp.mean(o, axis=-1, keepdims=True) for o in outs]
        scales = [lax.rsqrt(jnp.mean(oc * oc, axis=-1, keepdims=True) + NORM_EPS) for oc in centred]
        for (_, d, rows), oc, scale in zip(chains, centred, scales):
            gated = oc * scale * (gb_ref if d else gf_ref)[rows, :].astype(F32)
            if first_touch:
                o_scr[rows, :] = gated
            else:
                o_ref[rows, :] = (o_scr[rows, :] + gated).astype(o_ref.dtype)

    unroll = 8 if n_chunks % 16 == 0 else 1
    lax.fori_loop(0, n_chunks // 2, lambda ci, c: (body(ci, True), c)[1], 0, unroll=unroll)
    lax.fori_loop(n_chunks // 2, n_chunks, lambda ci, c: (body(ci, False), c)[1], 0, unroll=unroll)
    if emit_state:
        for d in range(2):
            for sq in range(seqs):
                s_out_refs[d][sq, 0, 0] = s_scr[sq, d]


def retention(q, k, v, gates, row0, batch, seq_len, log_decay, state=None, emit_state=False, seqs_per_step=1):
    dk = q.shape[1] // RET_HEADS
    dv = v.shape[1] // RET_HEADS
    seqs = seqs_per_step
    rows = seqs * seq_len
    assert row0 % rows == 0 and batch % seqs == 0
    s0 = row0 // rows
    in_specs = [
        pl.BlockSpec(memory_space=pltpu.SMEM),
        pl.BlockSpec((rows, dk), lambda b, h: (s0 + b, h)),
        pl.BlockSpec((rows, dk), lambda b, h: (s0 + b, h)),
        pl.BlockSpec((rows, dv), lambda b, h: (s0 + b, h)),
        pl.BlockSpec((rows, dv), lambda b, h: (s0 + b, h)),
        pl.BlockSpec((rows, dv), lambda b, h: (s0 + b, RET_HEADS + h)),
    ]
    args = [log_decay, q, k, v, gates, gates]
    state_spec = pl.BlockSpec((seqs, 1, 1, dk, dv), lambda b, h: (b, 0, h, 0, 0))
    if state is not None:
        in_specs += [state_spec, state_spec]
        args += list(state)
    out_shapes = [jax.ShapeDtypeStruct((batch * seq_len, RET_HEADS * dv), BF16)]
    out_specs = [pl.BlockSpec((rows, dv), lambda b, h: (b, h))]
    if emit_state:
        out_shapes += [jax.ShapeDtypeStruct((batch, 1, RET_HEADS, dk, dv), F32)] * 2
        out_specs += [state_spec, state_spec]
    return pl.pallas_call(
        functools.partial(_retention_kernel, seq_len=seq_len, seqs=seqs, has_state=state is not None,
                          emit_state=emit_state),
        out_shape=out_shapes,
        grid=(batch // seqs, RET_HEADS),
        in_specs=in_specs,
        out_specs=out_specs,
        scratch_shapes=[pltpu.VMEM((seqs, 2, dk, dv), F32), pltpu.VMEM((rows, dv), F32)],
        compiler_params=_cparams("parallel", "parallel"),
        name="retention",
    )(*args)


def _pack_bf16_halves(x):
    half = x.shape[1] // 2
    lo = lax.bitcast_convert_type(x[:, :half].astype(F32), jnp.uint32)
    hi = lax.bitcast_convert_type(x[:, half:].astype(F32), jnp.uint32)
    return (hi & jnp.uint32(0xFFFF0000)) | (lo >> 16)


def _unpack_bf16_halves(p):
    lo = lax.bitcast_convert_type(p << 16, F32).astype(BF16)
    hi = lax.bitcast_convert_type(p & jnp.uint32(0xFFFF0000), F32).astype(BF16)
    return lo, hi


def _first_max_onehot(vals):
    m = vals[0]
    for v in vals[1:]:
        m = jnp.maximum(m, v)
    onehot, taken = [], None
    for v in vals:
        hit = v == m
        if taken is None:
            onehot.append(hit)
            taken = hit
        else:
            onehot.append(hit & jnp.logical_not(taken))
            taken = taken | hit
    return m, onehot


def _pick(onehot, vals):
    out = vals[-1]
    for oh, v in zip(onehot[-2::-1], vals[-2::-1]):
        out = jnp.where(oh, v, out)
    return out


def _route_rows(x, g_ref, mod_ref, wr_ref, bias_ref, tri_ref, h_ref, idx_ref, wcol_ref, rank_ref, cnt_ref,
                ind_scr, wrow_scr):
    hb = _modulated_norm(x, g_ref[...], mod_ref, 3).astype(BF16)
    h_ref[...] = _pack_bf16_halves(hb)
    logits = _nt_dot(wr_ref[...], hb)
    s = jax.nn.sigmoid(logits)
    sel = s + bias_ref[...]
    neg_inf = jnp.full_like(sel[0:1], -jnp.inf)
    sel_rows = [sel[e:e + 1] for e in range(N_EXPERTS)]
    s_rows = [s[e:e + 1] for e in range(N_EXPERTS)]

    def top2(vals):
        m1, oh1 = _first_max_onehot(vals)
        rest = [jnp.where(o, neg_inf, v) for o, v in zip(oh1, vals)]
        m2, oh2 = _first_max_onehot(rest)
        return m1, m2, oh1, oh2

    grp_scores = []
    for g in range(N_GROUPS):
        m1, m2, _, _ = top2(sel_rows[g * EXPERTS_PER_GROUP:(g + 1) * EXPERTS_PER_GROUP])
        grp_scores.append(m1 + m2)
    _, in_grp = _first_max_onehot(grp_scores)
    cand_sel = [_pick(in_grp, [sel_rows[g * EXPERTS_PER_GROUP + k] for g in range(N_GROUPS)])
                for k in range(EXPERTS_PER_GROUP)]
    cand_s = [_pick(in_grp, [s_rows[g * EXPERTS_PER_GROUP + k] for g in range(N_GROUPS)])
              for k in range(EXPERTS_PER_GROUP)]
    _, _, oh1, oh2 = top2(cand_sel)
    w1 = _pick(oh1, cand_s)
    w2 = _pick(oh2, cand_s)
    denom = w1 + w2
    wrow_scr[...] = jnp.zeros_like(wrow_scr)
    wrow_scr[0:1, :] = w1 / denom
    wrow_scr[1:2, :] = w2 / denom
    wcol_ref[...] = jnp.transpose(wrow_scr[...])

    ints = [jnp.full(w1.shape, k, I32) for k in range(EXPERTS_PER_GROUP)]
    grp = _pick(in_grp, ints) * EXPERTS_PER_GROUP
    idx_ref[0:1, :] = grp + _pick(oh1, ints)
    idx_ref[1:2, :] = grp + _pick(oh2, ints)

    one, zero = jnp.ones_like(w1), jnp.zeros_like(w1)
    for g in range(N_GROUPS):
        for k in range(EXPERTS_PER_GROUP):
            e = g * EXPERTS_PER_GROUP + k
            ind_scr[e:e + 1, :] = jnp.where(in_grp[g] & (oh1[k] | oh2[k]), one, zero)
    ind = ind_scr[...].astype(BF16)
    ranks = jnp.dot(ind, tri_ref[...], preferred_element_type=F32)
    cnt_ref[0] = jnp.dot(ind, jnp.ones((ind.shape[1], LANES), BF16), preferred_element_type=F32)
    for slot, oh in ((0, oh1), (1, oh2)):
        r = zero
        for g in range(N_GROUPS):
            for k in range(EXPERTS_PER_GROUP):
                e = g * EXPERTS_PER_GROUP + k
                r = r + jnp.where(in_grp[g] & oh[k], ranks[e:e + 1], zero)
        rank_ref[slot:slot + 1, :] = r.astype(I32)


def _proj_route_kernel(*refs, n_x, n_a, n_ctx_tiles):
    x_refs, a_refs = refs[:n_x], refs[n_x:n_x + n_a]
    w_ref, mod_ref, g_ref, wr_ref, bias_ref, tri_ref, o_ref = refs[n_x + n_a:n_x + n_a + 7]
    route_out_refs = refs[n_x + n_a + 7:n_x + n_a + 12]
    ind_scr, wrow_scr, w_scr = refs[n_x + n_a + 12:]
    is_dec = pl.program_id(0) >= n_ctx_tiles

    @pl.when(pl.program_id(0) == 0)
    def _():
        w_scr[...] = w_ref[...].astype(BF16)
    h_ref, idx_ref, wcol_ref, rank_ref, cnt_ref = route_out_refs
    sub = o_ref.shape[0] // ROUTE_SPLIT

    def body(dec):
        x_ref = x_refs[(1 if dec else 0) if n_x == 2 else 0]
        a_ref = a_refs[(1 if dec else 0) if n_a == 2 else 0]
        parts = [slice(part * sub, (part + 1) * sub) for part in range(ROUTE_SPLIT)]
        for rows in parts:
            y = jnp.dot(a_ref[rows, :].astype(BF16), w_scr[...], preferred_element_type=F32)
            o_ref[rows, :] = x_ref[rows, :] + mod_ref[0, 2:3, :] * y
        for part, rows in enumerate(parts):
            _route_rows(o_ref[rows, :], g_ref, mod_ref, wr_ref, bias_ref, tri_ref, h_ref.at[rows, :],
                        idx_ref.at[:, rows], wcol_ref.at[rows, :], rank_ref.at[:, rows], cnt_ref.at[part:part + 1],
                        ind_scr, wrow_scr)

    if n_x == 1 and n_a == 1:
        body(False)
    else:
        pl.when(jnp.logical_not(is_dec))(lambda: body(False))
        pl.when(is_dec)(lambda: body(True))


def proj_residual_route(lay, x_parts, a_parts, w, mods, ffn_g, w_router_t, router_bias):
    d = x_parts[0].shape[1]
    kdim = a_parts[0].shape[1]
    tm, t = lay.tm, lay.t
    sub = tm // ROUTE_SPLIT
    tri = jnp.triu(jnp.ones((sub, sub), BF16), k=1)
    pair = jax.ShapeDtypeStruct((TOP_K, t), I32)
    pair_spec = pl.BlockSpec((TOP_K, tm), lambda i: (0, i))
    const = lambda shape: pl.BlockSpec(shape, lambda i: (0,) * len(shape))
    return pl.pallas_call(
        functools.partial(_proj_route_kernel, n_x=len(x_parts), n_a=len(a_parts), n_ctx_tiles=lay.n_ctx_tiles),
        out_shape=[jax.ShapeDtypeStruct((t, d), F32), jax.ShapeDtypeStruct((t, d // 2), jnp.uint32), pair,
                   jax.ShapeDtypeStruct((t, LANES), F32), pair,
                   jax.ShapeDtypeStruct((lay.n_tiles * ROUTE_SPLIT, N_EXPERTS, LANES), F32)],
        grid=(lay.n_tiles,),
        in_specs=lay.row_specs(x_parts, d) + lay.row_specs(a_parts, kdim) + [
            pl.BlockSpec((kdim, d), lambda i: (0, 0), pipeline_mode=pl.Buffered(1)),
            pl.BlockSpec((1, N_MOD, d), lambda i: (lay.mod_row(i), 0, 0)),
            const((1, d)), const((N_EXPERTS, d)), const((N_EXPERTS, 1)), const((sub, sub)),
        ],
        out_specs=[pl.BlockSpec((tm, d), lambda i: (i, 0)), pl.BlockSpec((tm, d // 2), lambda i: (i, 0)), pair_spec,
                   pl.BlockSpec((tm, LANES), lambda i: (i, 0)), pair_spec,
                   pl.BlockSpec((ROUTE_SPLIT, N_EXPERTS, LANES), lambda i: (i, 0, 0))],
        scratch_shapes=[pltpu.VMEM((N_EXPERTS, sub), F32), pltpu.VMEM((LANES, sub), F32),
                        pltpu.VMEM((kdim, d), BF16)],
        compiler_params=_cparams("arbitrary"),
        name="proj_residual_route",
    )(*x_parts, *a_parts, w, mods, ffn_g.reshape(1, d), w_router_t, router_bias.reshape(N_EXPERTS, 1), tri)


def dispatch_plan(idx, rank, cnt, n_sorted):
    t = idx.shape[1]
    n_tiles, tm = cnt.shape[0], t // cnt.shape[0]
    cnt_tile = cnt[:, :, 0].astype(I32)
    total = jnp.sum(cnt_tile, axis=0)
    padded = ((total + EXPERT_TILE - 1) // EXPERT_TILE) * EXPERT_TILE
    end = jnp.cumsum(padded)
    start = end - padded
    base = start[None, :] + jnp.cumsum(cnt_tile, axis=0) - cnt_tile
    idx3 = idx.reshape(TOP_K, n_tiles, tm)
    pos = rank.reshape(TOP_K, n_tiles, tm)
    for e in range(N_EXPERTS):
        pos = pos + jnp.where(idx3 == e, base[None, :, e, None], 0)
    tile_row = jnp.arange(n_sorted // EXPERT_TILE, dtype=I32) * EXPERT_TILE
    tile_expert = jnp.minimum(jnp.sum(end[None, :] <= tile_row[:, None], axis=1), N_EXPERTS - 1).astype(I32)
    n_valid = (end[-1] // EXPERT_TILE).astype(I32).reshape(1)
    return pos.reshape(TOP_K * t).astype(I32), tile_expert, n_valid


def _sc_mesh():
    return plsc.VectorSubcoreMesh(core_axis_name="c", subcore_axis_name="s")


def _sc_worker_id():
    return lax.axis_index("s") * V7X_SC_CORES + lax.axis_index("c")


def _sc_two_stage_pipeline(n_chunks, fetch_idx, load, store):
    assert n_chunks % 2 == 0
    fetch_idx(0, 0)
    load(0, 0).start()

    @pl.loop(0, n_chunks // 2)
    def _(p):
        j = 2 * p

        @pl.when(p > 0)
        def _():
            store(j - 1, 1).wait()

        fetch_idx(j + 1, 1)
        load(j + 1, 1).start()
        load(j, 0).wait()
        store(j, 0).start()
        store(j, 0).wait()

        @pl.when(j + 2 < n_chunks)
        def _():
            fetch_idx(j + 2, 0)
            load(j + 2, 0).start()

        load(j + 1, 1).wait()
        store(j + 1, 1).start()

    store(n_chunks - 1, 1).wait()


_SC_SCRATCH = lambda chunk, d, dtype: [
    pltpu.VMEM((chunk,), I32), pltpu.VMEM((chunk,), I32),
    pltpu.VMEM((chunk, d), dtype), pltpu.VMEM((chunk, d), dtype),
    pltpu.SemaphoreType.DMA, pltpu.SemaphoreType.DMA, pltpu.SemaphoreType.DMA, pltpu.SemaphoreType.DMA]


def sc_scatter_rows(src, pos, n_out_rows):
    t, d = src.shape
    n_idx = pos.shape[0]
    per_worker = n_idx // SC_WORKERS
    chunk = SC_CHUNK_ROWS
    assert n_idx % (SC_WORKERS * chunk * 2) == 0 and t % chunk == 0

    @functools.partial(
        pl.kernel, mesh=_sc_mesh(), out_type=jax.ShapeDtypeStruct((n_out_rows, d), src.dtype),
        scratch_types=_SC_SCRATCH(chunk, d, src.dtype), name="sc_scatter_rows")
    def scatter(src_hbm, pos_hbm, out_hbm, idx_a, idx_b, rows_a, rows_b, lsem_a, lsem_b, ssem_a, ssem_b):
        base = _sc_worker_id() * per_worker
        idx, rows, lsem, ssem = (idx_a, idx_b), (rows_a, rows_b), (lsem_a, lsem_b), (ssem_a, ssem_b)

        def fetch_idx(j, b):
            pltpu.sync_copy(pos_hbm.at[pl.ds(base + j * chunk, chunk)], idx[b])

        def load(j, b):
            return pltpu.make_async_copy(src_hbm.at[pl.ds(lax.rem(base + j * chunk, t), chunk)], rows[b], lsem[b])

        def store(j, b):
            return pltpu.make_async_copy(rows[b], out_hbm.at[idx[b]], ssem[b])

        _sc_two_stage_pipeline(per_worker // chunk, fetch_idx, load, store)

    return scatter(src, pos)


def sc_gather_rows(table, idx):
    _, d = table.shape
    n_idx = idx.shape[0]
    per_worker = n_idx // SC_WORKERS
    chunk = SC_CHUNK_ROWS
    assert n_idx % (SC_WORKERS * chunk * 2) == 0

    @functools.partial(
        pl.kernel, mesh=_sc_mesh(), out_type=jax.ShapeDtypeStruct((n_idx, d), table.dtype),
        scratch_types=_SC_SCRATCH(chunk, d, table.dtype), name="sc_gather_rows")
    def gather(table_hbm, idx_hbm, out_hbm, idx_a, idx_b, rows_a, rows_b, lsem_a, lsem_b, ssem_a, ssem_b):
        base = _sc_worker_id() * per_worker
        idx, rows, lsem, ssem = (idx_a, idx_b), (rows_a, rows_b), (lsem_a, lsem_b), (ssem_a, ssem_b)

        def fetch_idx(j, b):
            pltpu.sync_copy(idx_hbm.at[pl.ds(base + j * chunk, chunk)], idx[b])

        def load(j, b):
            return pltpu.make_async_copy(table_hbm.at[idx[b]], rows[b], lsem[b])

        def store(j, b):
            return pltpu.make_async_copy(rows[b], out_hbm.at[pl.ds(base + j * chunk, chunk)], ssem[b])

        _sc_two_stage_pipeline(per_worker // chunk, fetch_idx, load, store)

    return gather(table, idx)


def _experts_kernel(te_ref, nv_ref, x_ref, wg_ref, wu_ref, wd_ref, y_ref, wg_scr, wu_scr, wd_scr):
    i = pl.program_id(0)

    @pl.when(i < nv_ref[0])
    def _():
        @pl.when((i == 0) | (te_ref[i] != te_ref[jnp.maximum(i - 1, 0)]))
        def _():
            wg_scr[...] = wg_ref[0, 0].astype(BF16)
            wu_scr[...] = wu_ref[0, 0].astype(BF16)
            wd_scr[...] = wd_ref[0, 0].astype(BF16)

        x_lo, x_hi = _unpack_bf16_halves(x_ref[...])
        half = x_lo.shape[1]

        def in_proj(w_scr):
            return (jnp.dot(x_lo, w_scr[0:half, :], preferred_element_type=F32)
                    + jnp.dot(x_hi, w_scr[half:, :], preferred_element_type=F32))

        a = jax.nn.silu(in_proj(wg_scr)) * in_proj(wu_scr)
        y = jnp.dot(a.astype(BF16), wd_scr[...], preferred_element_type=F32)
        y_ref[...] = _pack_bf16_halves(y.astype(BF16))


def grouped_experts(xs, tile_expert, n_valid, wg, wu, wd, layer):
    n_rows = xs.shape[0]
    d, de = wg.shape[-2:]
    tm = EXPERT_TILE
    row_map = lambda i, te, nv: (jnp.minimum(i, nv[0] - 1), 0)
    grid_spec = pltpu.PrefetchScalarGridSpec(
        num_scalar_prefetch=2,
        grid=(n_rows // tm,),
        in_specs=[
            pl.BlockSpec((tm, d // 2), row_map),
            pl.BlockSpec((1, 1, d, de), lambda i, te, nv: (layer, te[i], 0, 0)),
            pl.BlockSpec((1, 1, d, de), lambda i, te, nv: (layer, te[i], 0, 0)),
            pl.BlockSpec((1, 1, de, d), lambda i, te, nv: (layer, te[i], 0, 0)),
        ],
        out_specs=pl.BlockSpec((tm, d // 2), row_map),
        scratch_shapes=[pltpu.VMEM((d, de), BF16), pltpu.VMEM((d, de), BF16), pltpu.VMEM((de, d), BF16)],
    )
    return pl.pallas_call(
        _experts_kernel,
        out_shape=jax.ShapeDtypeStruct((n_rows, d // 2), jnp.uint32),
        grid_spec=grid_spec,
        compiler_params=_cparams("arbitrary"),
        name="grouped_experts",
    )(tile_expert, n_valid, xs, wg, wu, wd)


def _final_combine_kernel(x_ref, y_ref, w_ref, mod_ref, fg_ref, o_ctx_ref, o_dec_ref, *, n_ctx_tiles):
    out = _moe_combine(x_ref[...], y_ref, w_ref, mod_ref)
    ms = jnp.mean(out * out, axis=-1, keepdims=True)
    out = out * lax.rsqrt(ms + NORM_EPS) * fg_ref[...]

    def store(o_ref):
        o_ref[...] = out
    _by_part(pl.program_id(0) >= n_ctx_tiles, (o_ctx_ref, o_dec_ref), store)


def final_combine(lay, x, y_pair, w_col, mods, final_g):
    d = x.shape[1]
    tm = lay.tm
    return pl.pallas_call(
        functools.partial(_final_combine_kernel, n_ctx_tiles=lay.n_ctx_tiles),
        out_shape=[jax.ShapeDtypeStruct((lay.t_ctx, d), F32), jax.ShapeDtypeStruct((lay.t_dec, d), F32)],
        grid=(lay.n_tiles,),
        in_specs=[
            pl.BlockSpec((tm, d), lambda i: (i, 0)),
            pl.BlockSpec((TOP_K, tm, d // 2), lambda i: (0, i, 0)),
            pl.BlockSpec((tm, LANES), lambda i: (i, 0)),
            pl.BlockSpec((1, N_MOD, d), lambda i: (lay.mod_row(i), 0, 0)),
            pl.BlockSpec((1, d), lambda i: (0, 0)),
        ],
        out_specs=lay.row_specs([None, None], d),
        compiler_params=_cparams("arbitrary"),
        name="final_combine",
    )(x, y_pair, w_col, mods, final_g.reshape(1, d))


def group_moe(lay, x, routing, p, layer):
    t, d = x.shape
    h, idx, w_col, rank, cnt = routing
    n_sorted = TOP_K * t + N_EXPERTS * (EXPERT_TILE - 1)
    n_sorted = -(-n_sorted // EXPERT_TILE) * EXPERT_TILE
    pos, tile_expert, n_valid = dispatch_plan(idx, rank, cnt, n_sorted)
    xs = sc_scatter_rows(h, pos, n_sorted)
    ys = grouped_experts(xs, tile_expert, n_valid, p["moe_w_gate"], p["moe_w_up"], p["moe_w_down"], layer)
    return sc_gather_rows(ys, pos).reshape(TOP_K, t, d // 2), w_col


def _rope_angles(n, d):
    n_rows = n // GRID_W
    row = jnp.repeat(jnp.arange(n_rows), GRID_W).astype(F32)
    col = jnp.tile(jnp.arange(GRID_W), n_rows).astype(F32)
    nf = d // 4
    freqs = jnp.power(ROPE_BASE, -jnp.arange(nf, dtype=F32) / nf)
    ang = jnp.concatenate([row[:, None] * freqs, col[:, None] * freqs], axis=-1)
    return jnp.cos(ang), jnp.sin(ang)


def kernel(x_prompt, x_sample, cache_attn_k, cache_attn_v, state_ret_fwd, state_ret_bwd, c, c_ctx, w_ada, b_ada, norm_mix_g, norm_ffn_g, final_norm_g, da_w_qkv, da_lambda_q1, da_lambda_k1, da_lambda_q2, da_lambda_k2, da_subln_g, da_w_o, ret_w_qkv, ret_w_gate_fwd, ret_w_gate_bwd, ret_decay_fwd, ret_decay_bwd, ret_w_o, w_router, router_bias, moe_w_gate, moe_w_up, moe_w_down):
    b_ctx, n_ctx, d = x_prompt.shape
    b_dec, n_dec, _ = x_sample.shape
    past = cache_attn_k.shape[2]
    n_attn = cache_attn_k.shape[1]
    depth = w_ada.shape[0]
    assert b_dec + 1 <= MOD_ROWS
    lay = Layout(b_ctx, n_ctx, b_dec, n_dec)

    cond = jnp.zeros((MOD_ROWS, d), F32).at[0].set(c_ctx).at[1:1 + b_dec].set(c)
    mods_all = ada_modulation(cond, w_ada, b_ada)

    p = {
        "norm_ffn_g": norm_ffn_g, "final_norm_g": final_norm_g,
        "w_router_t": w_router.T.astype(BF16), "router_bias": router_bias.astype(F32),
        "moe_w_gate": moe_w_gate, "moe_w_up": moe_w_up, "moe_w_down": moe_w_down,
    }
    ret_log_decay = jnp.stack([jax.nn.log_sigmoid(ret_decay_fwd.astype(F32)),
                               jax.nn.log_sigmoid(ret_decay_bwd.astype(F32))], axis=1)
    ck_all = cache_attn_k.reshape(b_dec, n_attn, past, -1)
    cv_all = cache_attn_v.reshape(b_dec, n_attn, past, -1)

    x_parts = [x_prompt.reshape(lay.t_ctx, d), x_sample.reshape(lay.t_dec, d)]
    new_k, new_v, new_sf, new_sb = [], [], [], []
    pending_moe = None

    def first_kernel(*args):
        nonlocal x_parts, pending_moe
        outs = norm_mod_matmul(lay, x_parts, *args, pending_moe=pending_moe)
        if pending_moe is not None:
            *outs, x_joint = outs
            x_parts, pending_moe = [x_joint], None
        return outs

    for i in range(depth):
        mods = mods_all[i]
        j = i // 2
        if i % 2 == 0:
            lam_init = 0.8 - 0.6 * math.exp(-0.3 * i)
            qkw = DA_HEADS * 2 * DA_HEAD_DIM
            vw = DA_HEADS * DA_V_DIM
            cos, sin = _rope_angles(n_dec, DA_HEAD_DIM)
            rope = (jnp.tile(cos, (1, 4)), jnp.concatenate([-sin, sin, -sin, sin], axis=-1))
            q, k, v, k_ctx, v_ctx = first_kernel(
                norm_mix_g[i], mods, da_w_qkv[j],
                [(0, qkw, "rope64", math.log2(math.e) * DA_HEAD_DIM ** -0.5, BF16, "all"),
                 (qkw, qkw, "rope64", 1.0, BF16, "all"),
                 (2 * qkw, vw, "plain", 1.0, BF16, "all"),
                 (qkw, qkw, "plain", 1.0, F32, "ctx64"), (2 * qkw, vw, "plain", 1.0, F32, "ctx")],
                rope)
            lam_params = jnp.stack([da_lambda_q1[j], da_lambda_k1[j], da_lambda_q2[j], da_lambda_k2[j]])
            mix = [diff_attention(q, k, v, 0, b_ctx, n_ctx, lam_params, da_subln_g[j], lam_init,
                                  heads_per_step=DA_HEADS),
                   diff_attention(q, k, v, lay.t_ctx, b_dec, n_dec, lam_params, da_subln_g[j], lam_init,
                                  cache=(ck_all[:, j], cv_all[:, j]))]
            mix_w = da_w_o[j]
            new_k.append(k_ctx.reshape(b_ctx, n_ctx, DA_HEADS, 2, DA_HEAD_DIM))
            new_v.append(v_ctx.reshape(b_ctx, n_ctx, DA_HEADS, DA_V_DIM))
        else:
            kd = ret_w_qkv.shape[2] // 4
            dv = 2 * kd
            w_all = jnp.concatenate([ret_w_qkv[j], ret_w_gate_fwd[j], ret_w_gate_bwd[j]], axis=-1).astype(BF16)
            q, k, v, gates = first_kernel(
                norm_mix_g[i], mods, w_all,
                [(0, kd, "rope256", 1.0, BF16, "all"),
                 (kd, kd, "rope256", (kd // RET_HEADS) ** -0.5, F32, "all"),
                 (2 * kd, dv, "plain", 1.0, BF16, "all"), (2 * kd + dv, 2 * dv, "silu", 1.0, BF16, "all")],
                _rope_angles(n_dec, kd // RET_HEADS))
            ctx_seqs = math.gcd(b_ctx, max(1, n_dec // n_ctx // 2))
            o_ctx, sf, sb = retention(q, k, v, gates, 0, b_ctx, n_ctx, ret_log_decay[j], emit_state=True,
                                      seqs_per_step=ctx_seqs)
            (o_dec,) = retention(q, k, v, gates, lay.t_ctx, b_dec, n_dec, ret_log_decay[j],
                                 state=(state_ret_fwd[:, j:j + 1], state_ret_bwd[:, j:j + 1]))
            mix, mix_w = [o_ctx, o_dec], ret_w_o[j]
            new_sf.append(sf)
            new_sb.append(sb)
        x, *routing = proj_residual_route(lay, x_parts, mix, mix_w, mods, norm_ffn_g[i], p["w_router_t"],
                                          p["router_bias"])
        y_pair, w_col = group_moe(lay, x, routing, p, i)
        if i == depth - 1:
            x_parts = final_combine(lay, x, y_pair, w_col, mods, final_norm_g)
        else:
            x_parts, pending_moe = [x], (y_pair, w_col, mods)

    y_ctx, y_dec = x_parts
    return (y_ctx.reshape(b_ctx, n_ctx, d), y_dec.reshape(b_dec, n_dec, d),
            jnp.stack(new_k, axis=1), jnp.stack(new_v, axis=1),
            jnp.concatenate(new_sf, axis=1), jnp.concatenate(new_sb, axis=1))
```

```python
import functools
import math

import jax
import jax.numpy as jnp
from jax import lax
from jax.experimental import pallas as pl
from jax.experimental.pallas import tpu as pltpu
from jax.experimental.pallas import tpu_sc as plsc

F32 = jnp.float32
BF16 = jnp.bfloat16
I32 = jnp.int32

GRID_W = 64
ROPE_BASE = 10000.0
NORM_EPS = 1e-6
DA_HEADS = 8
DA_HEAD_DIM = 64
DA_V_DIM = 2 * DA_HEAD_DIM
RET_HEADS = 4
RET_CHUNK = 128
N_EXPERTS = 16
N_GROUPS = 4
EXPERTS_PER_GROUP = N_EXPERTS // N_GROUPS
TOP_K = 2
N_MOD = 6
MOD_ROWS = 16
LANES = 128

V7X_VMEM_LIMIT = 56 * 1024 * 1024
V7X_SC_CORES = 2
V7X_SC_SUBCORES = 16
SC_WORKERS = V7X_SC_CORES * V7X_SC_SUBCORES
SC_CHUNK_ROWS = 64

ROW_TILE = 512
EXPERT_TILE = 512
ROUTE_SPLIT = 2


def _cparams(*sem):
    return pltpu.CompilerParams(dimension_semantics=sem, vmem_limit_bytes=V7X_VMEM_LIMIT)


def _nt_dot(a, b):
    return lax.dot_general(a, b, (((1,), (1,)), ((), ())), preferred_element_type=F32)


def _tn_dot(a, b):
    return lax.dot_general(a, b, (((0,), (0,)), ((), ())), preferred_element_type=F32)


class Layout:
    def __init__(self, b_ctx, n_ctx, b_dec, n_dec):
        self.b_ctx, self.n_ctx, self.b_dec, self.n_dec = b_ctx, n_ctx, b_dec, n_dec
        self.t_ctx, self.t_dec = b_ctx * n_ctx, b_dec * n_dec
        self.t = self.t_ctx + self.t_dec
        self.tm = min(ROW_TILE, n_dec, self.t_ctx)
        assert self.t_ctx % self.tm == 0 and n_dec % self.tm == 0
        assert self.t_ctx % n_dec == 0 and self.t_ctx % n_ctx == 0
        self.n_ctx_tiles = self.t_ctx // self.tm
        self.n_tiles = self.t // self.tm

    def mod_row(self, i):
        r = i * self.tm
        return jnp.where(r < self.t_ctx, 0, 1 + (r - self.t_ctx) // self.n_dec)

    def part_tile(self, part, i):
        if part == 0:
            return jnp.minimum(i, self.n_ctx_tiles - 1)
        return jnp.maximum(i - self.n_ctx_tiles, 0)

    def row_specs(self, arrays, width):
        if len(arrays) == 1:
            return [pl.BlockSpec((self.tm, width), lambda i, *_: (i, 0))]
        return [pl.BlockSpec((self.tm, width), lambda i, *_, p=p: (self.part_tile(p, i), 0)) for p in (0, 1)]


def _ada_kernel(c_ref, w_ref, b_ref, o_ref):
    s = jax.nn.silu(c_ref[...]).astype(BF16)
    acc = jnp.dot(s, w_ref[0].astype(BF16), preferred_element_type=F32)
    o_ref[0] = acc + b_ref[0]


def ada_modulation(cond, w_ada, b_ada):
    depth, d, n = w_ada.shape
    tn = 1536
    out = pl.pallas_call(
        _ada_kernel,
        out_shape=jax.ShapeDtypeStruct((depth, MOD_ROWS, n), F32),
        grid=(depth, n // tn),
        in_specs=[
            pl.BlockSpec((MOD_ROWS, d), lambda l, j: (0, 0)),
            pl.BlockSpec((1, d, tn), lambda l, j: (l, 0, j)),
            pl.BlockSpec((1, 1, tn), lambda l, j: (l, 0, j)),
        ],
        out_specs=pl.BlockSpec((1, MOD_ROWS, tn), lambda l, j: (l, 0, j)),
        compiler_params=_cparams("parallel", "parallel"),
        name="ada_modulation",
    )(cond, w_ada, b_ada.reshape(depth, 1, n))
    return out.reshape(depth, MOD_ROWS, N_MOD, d)


def _modulated_norm(x, g, mod_ref, shift_idx):
    ms = jnp.mean(x * x, axis=-1, keepdims=True)
    y = x * lax.rsqrt(ms + NORM_EPS) * g
    return y * (1.0 + mod_ref[0, shift_idx + 1:shift_idx + 2, :]) + mod_ref[0, shift_idx:shift_idx + 1, :]


def _by_part(is_dec, refs, fn):
    if len(refs) == 1:
        fn(refs[0])
        return
    pl.when(jnp.logical_not(is_dec))(lambda: fn(refs[0]))
    pl.when(is_dec)(lambda: fn(refs[1]))


def _rope64(a, cos, sin_signed, first_half):
    partner = jnp.where(first_half, pltpu.roll(a, 96, 1), pltpu.roll(a, 32, 1))
    return a * cos + partner * sin_signed


def _moe_combine(x, y_ref, w_ref, mod_ref):
    w = w_ref[...]

    def expert_out(slot):
        lo, hi = _unpack_bf16_halves(y_ref[slot])
        return jnp.concatenate([lo.astype(F32), hi.astype(F32)], axis=1)

    return x + mod_ref[0, 5:6, :] * (w[:, 0:1] * expert_out(0) + w[:, 1:2] * expert_out(1))


def _nmm_kernel(*refs, n_x, pending_moe, cast_w, outs, tn, n_ctx_tiles):
    x_refs = refs[:n_x]
    if pending_moe:
        y_ref, wcol_ref, prev_mod_ref = refs[n_x:n_x + 3]
        refs = refs[:n_x] + refs[n_x + 3:]
    g_ref, mod_ref, w_ref, cos_ref, sin_ref = refs[n_x:n_x + 5]
    n_out = len(outs) + (1 if pending_moe else 0)
    out_refs = refs[n_x + 5:n_x + 5 + n_out]
    if cast_w:
        w_f32_ref, w_ref = w_ref, refs[n_x + 5 + n_out]

        @pl.when(pl.program_id(0) == 0)
        def _():
            w_ref[...] = w_f32_ref[...].astype(BF16)
    is_dec = pl.program_id(0) >= n_ctx_tiles
    tm = x_refs[0].shape[0]
    segments = sorted({(col0, width) for col0, width, _, _, _ in outs})

    def rotated(a, kind, rows):
        cos, sin = cos_ref[rows, :], sin_ref[rows, :]
        if kind == "rope64":
            lane = lax.broadcasted_iota(I32, cos.shape, 1)
            first_half = (lane & 32) == 0
            return [(c * 128, _rope64(a[:, c * 128:(c + 1) * 128], cos, sin, first_half))
                    for c in range(tn // 128)]
        pieces = []
        for c in range(tn // 256):
            x1, x2 = a[:, c * 256:c * 256 + 128], a[:, c * 256 + 128:(c + 1) * 256]
            pieces += [(c * 256, x1 * cos - x2 * sin), (c * 256 + 128, x2 * cos + x1 * sin)]
        return pieces

    def emit(dec):
        x_ref = x_refs[(1 if dec else 0) if n_x == 2 else 0]
        for r0 in (0, tm // 2):
            rows = slice(r0, r0 + tm // 2)
            x = x_ref[rows, :]
            if pending_moe:
                x = _moe_combine(x, y_ref.at[:, rows, :], wcol_ref.at[rows, :], prev_mod_ref)
                out_refs[-1][rows, :] = x
            h = _modulated_norm(x, g_ref[...], mod_ref, 0).astype(BF16)
            for col0, width in segments:
                sinks = [(o_ref, o) for o_ref, o in zip(out_refs, outs)
                         if (o[0], o[1]) == (col0, width) and not (dec and o[4] != "all")]
                for blk in range(width // tn):
                    cols = slice(blk * tn, (blk + 1) * tn)
                    acc = jnp.dot(h, w_ref[:, col0 + blk * tn:col0 + (blk + 1) * tn], preferred_element_type=F32)
                    for o_ref, (_, _, kind, scale, out_rows) in sinks:
                        a = acc if scale == 1.0 else acc * scale
                        if out_rows == "ctx64":
                            n64 = width // 64
                            for c in range(tn // 64):
                                o_ref[pl.ds(r0 * n64 + blk * tn // 64 + c, tm // 2, stride=n64), :] = (
                                    a[:, c * 64:(c + 1) * 64].astype(o_ref.dtype))
                        elif kind == "silu":
                            o_ref[rows, cols] = jax.nn.silu(a).astype(o_ref.dtype)
                        elif kind == "plain" or out_rows == "ctx" or not dec:
                            o_ref[rows, cols] = a.astype(o_ref.dtype)
                        else:
                            for off, val in rotated(a, kind, rows):
                                o_ref[rows, blk * tn + off:blk * tn + off + 128] = val.astype(o_ref.dtype)

    pl.when(jnp.logical_not(is_dec))(lambda: emit(False))
    pl.when(is_dec)(lambda: emit(True))


def norm_mod_matmul(lay, x_parts, g, mods, w, outputs, rope_tables, pending_moe=None, tn=512):
    d = x_parts[0].shape[1]
    tm = lay.tm
    n_total = w.shape[1]
    outs, out_shapes, out_specs = [], [], []
    for col0, width, kind, scale, dtype, rows in outputs:
        assert width % tn == 0 and col0 % LANES == 0
        outs.append((col0, width, kind, float(scale), rows))
        if rows == "ctx64":
            n64 = width // 64
            out_shapes.append(jax.ShapeDtypeStruct((lay.t_ctx * n64, 64), dtype))
            out_specs.append(pl.BlockSpec((tm * n64, 64), lambda i: (lay.part_tile(0, i), 0)))
            continue
        n_rows = lay.t_ctx if rows == "ctx" else lay.t
        out_shapes.append(jax.ShapeDtypeStruct((n_rows, width), dtype))
        out_specs.append(lay.row_specs([None, None], width)[0] if rows == "ctx" else lay.row_specs([None], width)[0])
    blocks_per_seq = lay.n_dec // tm
    rope_spec = pl.BlockSpec((tm, LANES), lambda i: (lay.part_tile(1, i) % blocks_per_seq, 0))
    mod_spec = pl.BlockSpec((1, N_MOD, d), lambda i: (lay.mod_row(i), 0, 0))
    moe_specs, moe_args = [], []
    if pending_moe is not None:
        assert len(x_parts) == 1
        moe_specs = [pl.BlockSpec((TOP_K, tm, d // 2), lambda i: (0, i, 0)),
                     pl.BlockSpec((tm, LANES), lambda i: (i, 0)), mod_spec]
        moe_args = list(pending_moe)
        out_shapes.append(jax.ShapeDtypeStruct((lay.t, d), F32))
        out_specs.append(pl.BlockSpec((tm, d), lambda i: (i, 0)))
    return pl.pallas_call(
        functools.partial(_nmm_kernel, n_x=len(x_parts), pending_moe=pending_moe is not None,
                          cast_w=w.dtype != BF16, outs=tuple(outs), tn=tn, n_ctx_tiles=lay.n_ctx_tiles),
        out_shape=out_shapes,
        grid=(lay.n_tiles,),
        in_specs=lay.row_specs(x_parts, d) + moe_specs + [
            pl.BlockSpec((1, d), lambda i: (0, 0)),
            mod_spec,
            pl.BlockSpec((d, n_total), lambda i: (0, 0), pipeline_mode=pl.Buffered(1)),
            rope_spec, rope_spec,
        ],
        out_specs=out_specs,
        scratch_shapes=[pltpu.VMEM(w.shape, BF16)] if w.dtype != BF16 else [],
        compiler_params=_cparams("arbitrary"),
        name="norm_mod_matmul",
    )(*x_parts, *moe_args, g.reshape(1, d), mods, w, *rope_tables)


def _diff_lambda(lam_ref, lam_init):
    lp = lam_ref[...]
    return (jnp.exp(jnp.sum(lp[0:1] * lp[1:2], axis=-1, keepdims=True))
            - jnp.exp(jnp.sum(lp[2:3] * lp[3:4], axis=-1, keepdims=True)) + lam_init)


ONES_ROWS = 16


def _scores_t(k, q, comp):
    lane = lax.broadcasted_iota(I32, q.shape, 1)
    return _nt_dot(k, jnp.where((lane < DA_HEAD_DIM) == (comp == 0), q, jnp.zeros_like(q)))


def _softmax_values_t(scores, vts):
    dv = vts[0].shape[0] - ONES_ROWS
    maxes = [jnp.max(s, axis=0, keepdims=True) for s in scores]
    exps = [jnp.exp2(s - m).astype(BF16) for s, m in zip(scores, maxes)]
    accs = [jnp.dot(vt, e, preferred_element_type=F32) for vt, e in zip(vts, exps)]
    return [acc[0:dv] / acc[dv:dv + 1] for acc in accs]


def _diff_finish_t(parts, lam, g_col, lam_init):
    o = parts[0] - lam * parts[1]
    ms = jnp.mean(o * o, axis=0, keepdims=True)
    return jnp.transpose((o * lax.rsqrt(ms + NORM_EPS) * g_col) * (1.0 - lam_init))


def _diff_attn_kernel(lam_ref, q_ref, k_ref, v_ref, g_ref, o_ref, *, lam_init, heads_per_step):
    hw = 2 * DA_HEAD_DIM
    lam = _diff_lambda(lam_ref, lam_init)
    ones = jnp.ones((ONES_ROWS, k_ref.shape[0]), BF16)
    heads = [slice(hh * hw, (hh + 1) * hw) for hh in range(heads_per_step)]
    scores = [[_scores_t(k_ref[:, cols], q_ref[:, cols], comp) for comp in range(2)] for cols in heads]
    vts = [jnp.concatenate([jnp.transpose(v_ref[:, cols].astype(F32)).astype(BF16), ones], axis=0) for cols in heads]
    parts = _softmax_values_t([s for pair in scores for s in pair], [vt for vt in vts for _ in range(2)])
    for hh, cols in enumerate(heads):
        o_ref[:, cols] = _diff_finish_t(parts[2 * hh:2 * hh + 2], lam, g_ref[...], lam_init).astype(o_ref.dtype)


def _diff_attn_cached_kernel(lam_ref, q_ref, k_ref, v_ref, ck_ref, cv_ref, g_ref, o_ref, k_scr, vt_scr, s_scr, *,
                             lam_init, sub_rows):
    seq_len, hw = q_ref.shape
    n_sub = seq_len // sub_rows
    assert n_sub % 2 == 0
    lam = _diff_lambda(lam_ref, lam_init)
    k_scr[0:seq_len, :] = k_ref[...]
    k_scr[seq_len:, :] = ck_ref[0].astype(BF16)
    vt_scr[0:hw, 0:seq_len] = jnp.transpose(v_ref[...])
    vt_scr[0:hw, seq_len:] = jnp.transpose(cv_ref[0]).astype(BF16)
    vt_scr[hw:, :] = jnp.ones((ONES_ROWS, vt_scr.shape[1]), BF16)

    def rows_of(t):
        return pl.ds(pl.multiple_of(t * sub_rows, sub_rows), sub_rows)

    def scores(t, slot, comp):
        s_scr[slot, comp] = _scores_t(k_scr[...], q_ref[rows_of(t), :], comp)

    def stage(t_next, slot_next, t, slot):
        for comp in range(2):
            scores(t_next, slot_next, comp)
        parts = _softmax_values_t([s_scr[slot, 0], s_scr[slot, 1]], [vt_scr[...]] * 2)
        o_ref[rows_of(t), :] = _diff_finish_t(parts, lam, g_ref[...], lam_init).astype(o_ref.dtype)

    scores(0, 0, 0)
    scores(0, 0, 1)

    def body(i2, carry):
        t = 2 * i2
        stage(t + 1, 1, t, 0)
        stage(jnp.minimum(t + 2, n_sub - 1), 0, t + 1, 1)
        return carry

    lax.fori_loop(0, n_sub // 2, body, 0, unroll=2 if n_sub % 4 == 0 else 1)


def diff_attention(q, k, v, row0, batch, seq_len, lam_params, subln_g, lam_init, cache=None, heads_per_step=1,
                   sub_rows=256):
    width = q.shape[1]
    hw = 2 * DA_HEAD_DIM
    bw = heads_per_step * hw
    assert row0 % seq_len == 0 and DA_HEADS % heads_per_step == 0
    s0 = row0 // seq_len
    seq_spec = pl.BlockSpec((seq_len, bw), lambda b, h: (s0 + b, h))
    in_specs = [pl.BlockSpec((4, DA_HEAD_DIM), lambda b, h: (0, 0)), seq_spec, seq_spec, seq_spec]
    args = [lam_params, q, k, v]
    scratch = []
    if cache is None:
        body = functools.partial(_diff_attn_kernel, lam_init=lam_init, heads_per_step=heads_per_step)
    else:
        assert heads_per_step == 1
        past = cache[0].shape[1]
        cache_spec = pl.BlockSpec((1, past, hw), lambda b, h: (b, 0, h))
        in_specs += [cache_spec, cache_spec]
        args += list(cache)
        sub_rows = min(sub_rows, seq_len // 2)
        scratch = [pltpu.VMEM((seq_len + past, hw), BF16), pltpu.VMEM((hw + ONES_ROWS, seq_len + past), BF16),
                   pltpu.VMEM((2, 2, seq_len + past, sub_rows), F32)]
        body = functools.partial(_diff_attn_cached_kernel, lam_init=lam_init, sub_rows=sub_rows)
    in_specs.append(pl.BlockSpec((hw, 1), lambda b, h: (0, 0)))
    args.append(subln_g.reshape(hw, 1))
    return pl.pallas_call(
        body,
        out_shape=jax.ShapeDtypeStruct((batch * seq_len, width), BF16),
        grid=(batch, DA_HEADS // heads_per_step),
        in_specs=in_specs,
        out_specs=pl.BlockSpec((seq_len, bw), lambda b, h: (b, h)),
        scratch_shapes=scratch,
        compiler_params=_cparams("parallel", "parallel"),
        name="diff_attention",
    )(*args)


def _retention_kernel(*refs, seq_len, seqs, has_state, emit_state):
    lg_ref, q_ref, k_ref, v_ref, gf_ref, gb_ref = refs[:6]
    pos = 6
    if has_state:
        s0_refs = refs[6:8]
        pos = 8
    o_ref = refs[pos]
    pos += 1
    if emit_state:
        s_out_refs = refs[pos:pos + 2]
        pos += 2
    s_scr, o_scr = refs[pos:pos + 2]

    h = pl.program_id(1)
    c_len = RET_CHUNK
    n_chunks = seq_len // c_len
    assert n_chunks % 2 == 0
    row = lax.broadcasted_iota(I32, (c_len, c_len), 0)
    colm = lax.broadcasted_iota(I32, (c_len, c_len), 1)
    rel = (row - colm).astype(F32)
    posv = lax.broadcasted_iota(I32, (c_len, 1), 0).astype(F32)

    consts = []
    for backward in (False, True):
        lg = lg_ref[1 if backward else 0, h]
        if backward:
            intra = jnp.where(rel <= 0, jnp.exp(-rel * lg), 0.0)
            q_decay = jnp.exp((c_len - posv) * lg)
            k_decay = jnp.exp(posv * lg)
        else:
            intra = jnp.where(rel >= 0, jnp.exp(rel * lg), 0.0)
            q_decay = jnp.exp((posv + 1.0) * lg)
            k_decay = jnp.exp((c_len - 1.0 - posv) * lg)
        consts.append((intra, q_decay, k_decay, jnp.exp(jnp.zeros((1, 1), F32) + c_len * lg)))
        direction = 1 if backward else 0
        for sq in range(seqs):
            if has_state:
                s_scr[sq, direction] = s0_refs[direction][sq, 0, 0]
            else:
                s_scr[sq, direction] = jnp.zeros(s_scr.shape[2:], F32)

    def body(ci, first_touch):
        chains = [(sq, direction, pl.ds(pl.multiple_of(sq * seq_len + c * c_len, c_len), c_len))
                  for sq in range(seqs) for direction, c in ((0, ci), (1, n_chunks - 1 - ci))]
        qs = [q_ref[rows, :].astype(BF16) for _, _, rows in chains]
        ks = [k_ref[rows, :].astype(F32) for _, _, rows in chains]
        vs = [v_ref[rows, :].astype(BF16) for _, _, rows in chains]
        intras = [(_nt_dot(qb, kf.astype(BF16)) * consts[d][0]).astype(BF16)
                  for (_, d, _), qb, kf in zip(chains, qs, ks)]
        states = [s_scr[sq, d] for sq, d, _ in chains]
        outs = [jnp.dot(a, vb, preferred_element_type=F32)
                + jnp.dot(qb, s.astype(BF16), preferred_element_type=F32) * consts[d][1]
                for (_, d, _), a, qb, vb, s in zip(chains, intras, qs, vs, states)]
        for (sq, d, _), kf, vb, s in zip(chains, ks, vs, states):
            s_scr[sq, d] = consts[d][3] * s + _tn_dot((kf * consts[d][2]).astype(BF16), vb)
        centred = [o - jnp.mean(o, axis=-1, keepdims=True) for o in outs]
        scales = [lax.rsqrt(jnp.mean(oc * oc, axis=-1, keepdims=True) + NORM_EPS) for oc in centred]
        for (_, d, rows), oc, scale in zip(chains, centred, scales):
            gated = oc * scale * (gb_ref if d else gf_ref)[rows, :].astype(F32)
            if first_touch:
                o_scr[rows, :] = gated
            else:
                o_ref[rows, :] = (o_scr[rows, :] + gated).astype(o_ref.dtype)

    unroll = 8 if n_chunks % 16 == 0 else 1
    lax.fori_loop(0, n_chunks // 2, lambda ci, c: (body(ci, True), c)[1], 0, unroll=unroll)
    lax.fori_loop(n_chunks // 2, n_chunks, lambda ci, c: (body(ci, False), c)[1], 0, unroll=unroll)
    if emit_state:
        for d in range(2):
            for sq in range(seqs):
                s_out_refs[d][sq, 0, 0] = s_scr[sq, d]


def retention(q, k, v, gates, row0, batch, seq_len, log_decay, state=None, emit_state=False, seqs_per_step=1):
    dk = q.shape[1] // RET_HEADS
    dv = v.shape[1] // RET_HEADS
    seqs = seqs_per_step
    rows = seqs * seq_len
    assert row0 % rows == 0 and batch % seqs == 0
    s0 = row0 // rows
    in_specs = [
        pl.BlockSpec(memory_space=pltpu.SMEM),
        pl.BlockSpec((rows, dk), lambda b, h: (s0 + b, h)),
        pl.BlockSpec((rows, dk), lambda b, h: (s0 + b, h)),
        pl.BlockSpec((rows, dv), lambda b, h: (s0 + b, h)),
        pl.BlockSpec((rows, dv), lambda b, h: (s0 + b, h)),
        pl.BlockSpec((rows, dv), lambda b, h: (s0 + b, RET_HEADS + h)),
    ]
    args = [log_decay, q, k, v, gates, gates]
    state_spec = pl.BlockSpec((seqs, 1, 1, dk, dv), lambda b, h: (b, 0, h, 0, 0))
    if state is not None:
        in_specs += [state_spec, state_spec]
        args += list(state)
    out_shapes = [jax.ShapeDtypeStruct((batch * seq_len, RET_HEADS * dv), BF16)]
    out_specs = [pl.BlockSpec((rows, dv), lambda b, h: (b, h))]
    if emit_state:
        out_shapes += [jax.ShapeDtypeStruct((batch, 1, RET_HEADS, dk, dv), F32)] * 2
        out_specs += [state_spec, state_spec]
    return pl.pallas_call(
        functools.partial(_retention_kernel, seq_len=seq_len, seqs=seqs, has_state=state is not None,
                          emit_state=emit_state),
        out_shape=out_shapes,
        grid=(batch // seqs, RET_HEADS),
        in_specs=in_specs,
        out_specs=out_specs,
        scratch_shapes=[pltpu.VMEM((seqs, 2, dk, dv), F32), pltpu.VMEM((rows, dv), F32)],
        compiler_params=_cparams("parallel", "parallel"),
        name="retention",
    )(*args)


def _pack_bf16_halves(x):
    half = x.shape[1] // 2
    lo = lax.bitcast_convert_type(x[:, :half].astype(F32), jnp.uint32)
    hi = lax.bitcast_convert_type(x[:, half:].astype(F32), jnp.uint32)
    return (hi & jnp.uint32(0xFFFF0000)) | (lo >> 16)


def _unpack_bf16_halves(p):
    lo = lax.bitcast_convert_type(p << 16, F32).astype(BF16)
    hi = lax.bitcast_convert_type(p & jnp.uint32(0xFFFF0000), F32).astype(BF16)
    return lo, hi


def _first_max_onehot(vals):
    m = vals[0]
    for v in vals[1:]:
        m = jnp.maximum(m, v)
    onehot, taken = [], None
    for v in vals:
        hit = v == m
        if taken is None:
            onehot.append(hit)
            taken = hit
        else:
            onehot.append(hit & jnp.logical_not(taken))
            taken = taken | hit
    return m, onehot


def _pick(onehot, vals):
    out = vals[-1]
    for oh, v in zip(onehot[-2::-1], vals[-2::-1]):
        out = jnp.where(oh, v, out)
    return out


def _route_rows(x, g_ref, mod_ref, wr_ref, bias_ref, tri_ref, h_ref, idx_ref, wcol_ref, rank_ref, cnt_ref,
                ind_scr, wrow_scr):
    hb = _modulated_norm(x, g_ref[...], mod_ref, 3).astype(BF16)
    h_ref[...] = _pack_bf16_halves(hb)
    logits = _nt_dot(wr_ref[...], hb)
    s = jax.nn.sigmoid(logits)
    sel = s + bias_ref[...]
    neg_inf = jnp.full_like(sel[0:1], -jnp.inf)
    sel_rows = [sel[e:e + 1] for e in range(N_EXPERTS)]
    s_rows = [s[e:e + 1] for e in range(N_EXPERTS)]

    def top2(vals):
        m1, oh1 = _first_max_onehot(vals)
        rest = [jnp.where(o, neg_inf, v) for o, v in zip(oh1, vals)]
        m2, oh2 = _first_max_onehot(rest)
        return m1, m2, oh1, oh2

    grp_scores = []
    for g in range(N_GROUPS):
        m1, m2, _, _ = top2(sel_rows[g * EXPERTS_PER_GROUP:(g + 1) * EXPERTS_PER_GROUP])
        grp_scores.append(m1 + m2)
    _, in_grp = _first_max_onehot(grp_scores)
    cand_sel = [_pick(in_grp, [sel_rows[g * EXPERTS_PER_GROUP + k] for g in range(N_GROUPS)])
                for k in range(EXPERTS_PER_GROUP)]
    cand_s = [_pick(in_grp, [s_rows[g * EXPERTS_PER_GROUP + k] for g in range(N_GROUPS)])
              for k in range(EXPERTS_PER_GROUP)]
    _, _, oh1, oh2 = top2(cand_sel)
    w1 = _pick(oh1, cand_s)
    w2 = _pick(oh2, cand_s)
    denom = w1 + w2
    wrow_scr[...] = jnp.zeros_like(wrow_scr)
    wrow_scr[0:1, :] = w1 / denom
    wrow_scr[1:2, :] = w2 / denom
    wcol_ref[...] = jnp.transpose(wrow_scr[...])

    ints = [jnp.full(w1.shape, k, I32) for k in range(EXPERTS_PER_GROUP)]
    grp = _pick(in_grp, ints) * EXPERTS_PER_GROUP
    idx_ref[0:1, :] = grp + _pick(oh1, ints)
    idx_ref[1:2, :] = grp + _pick(oh2, ints)

    one, zero = jnp.ones_like(w1), jnp.zeros_like(w1)
    for g in range(N_GROUPS):
        for k in range(EXPERTS_PER_GROUP):
            e = g * EXPERTS_PER_GROUP + k
            ind_scr[e:e + 1, :] = jnp.where(in_grp[g] & (oh1[k] | oh2[k]), one, zero)
    ind = ind_scr[...].astype(BF16)
    ranks = jnp.dot(ind, tri_ref[...], preferred_element_type=F32)
    cnt_ref[0] = jnp.dot(ind, jnp.ones((ind.shape[1], LANES), BF16), preferred_element_type=F32)
    for slot, oh in ((0, oh1), (1, oh2)):
        r = zero
        for g in range(N_GROUPS):
            for k in range(EXPERTS_PER_GROUP):
                e = g * EXPERTS_PER_GROUP + k
                r = r + jnp.where(in_grp[g] & oh[k], ranks[e:e + 1], zero)
        rank_ref[slot:slot + 1, :] = r.astype(I32)


def _proj_route_kernel(*refs, n_x, n_a, n_ctx_tiles):
    x_refs, a_refs = refs[:n_x], refs[n_x:n_x + n_a]
    w_ref, mod_ref, g_ref, wr_ref, bias_ref, tri_ref, o_ref = refs[n_x + n_a:n_x + n_a + 7]
    route_out_refs = refs[n_x + n_a + 7:n_x + n_a + 12]
    ind_scr, wrow_scr, w_scr = refs[n_x + n_a + 12:]
    is_dec = pl.program_id(0) >= n_ctx_tiles

    @pl.when(pl.program_id(0) == 0)
    def _():
        w_scr[...] = w_ref[...].astype(BF16)
    h_ref, idx_ref, wcol_ref, rank_ref, cnt_ref = route_out_refs
    sub = o_ref.shape[0] // ROUTE_SPLIT

    def body(dec):
        x_ref = x_refs[(1 if dec else 0) if n_x == 2 else 0]
        a_ref = a_refs[(1 if dec else 0) if n_a == 2 else 0]
        parts = [slice(part * sub, (part + 1) * sub) for part in range(ROUTE_SPLIT)]
        for rows in parts:
            y = jnp.dot(a_ref[rows, :].astype(BF16), w_scr[...], preferred_element_type=F32)
            o_ref[rows, :] = x_ref[rows, :] + mod_ref[0, 2:3, :] * y
        for part, rows in enumerate(parts):
            _route_rows(o_ref[rows, :], g_ref, mod_ref, wr_ref, bias_ref, tri_ref, h_ref.at[rows, :],
                        idx_ref.at[:, rows], wcol_ref.at[rows, :], rank_ref.at[:, rows], cnt_ref.at[part:part + 1],
                        ind_scr, wrow_scr)

    if n_x == 1 and n_a == 1:
        body(False)
    else:
        pl.when(jnp.logical_not(is_dec))(lambda: body(False))
        pl.when(is_dec)(lambda: body(True))


def proj_residual_route(lay, x_parts, a_parts, w, mods, ffn_g, w_router_t, router_bias):
    d = x_parts[0].shape[1]
    kdim = a_parts[0].shape[1]
    tm, t = lay.tm, lay.t
    sub = tm // ROUTE_SPLIT
    tri = jnp.triu(jnp.ones((sub, sub), BF16), k=1)
    pair = jax.ShapeDtypeStruct((TOP_K, t), I32)
    pair_spec = pl.BlockSpec((TOP_K, tm), lambda i: (0, i))
    const = lambda shape: pl.BlockSpec(shape, lambda i: (0,) * len(shape))
    return pl.pallas_call(
        functools.partial(_proj_route_kernel, n_x=len(x_parts), n_a=len(a_parts), n_ctx_tiles=lay.n_ctx_tiles),
        out_shape=[jax.ShapeDtypeStruct((t, d), F32), jax.ShapeDtypeStruct((t, d // 2), jnp.uint32), pair,
                   jax.ShapeDtypeStruct((t, LANES), F32), pair,
                   jax.ShapeDtypeStruct((lay.n_tiles * ROUTE_SPLIT, N_EXPERTS, LANES), F32)],
        grid=(lay.n_tiles,),
        in_specs=lay.row_specs(x_parts, d) + lay.row_specs(a_parts, kdim) + [
            pl.BlockSpec((kdim, d), lambda i: (0, 0), pipeline_mode=pl.Buffered(1)),
            pl.BlockSpec((1, N_MOD, d), lambda i: (lay.mod_row(i), 0, 0)),
            const((1, d)), const((N_EXPERTS, d)), const((N_EXPERTS, 1)), const((sub, sub)),
        ],
        out_specs=[pl.BlockSpec((tm, d), lambda i: (i, 0)), pl.BlockSpec((tm, d // 2), lambda i: (i, 0)), pair_spec,
                   pl.BlockSpec((tm, LANES), lambda i: (i, 0)), pair_spec,
                   pl.BlockSpec((ROUTE_SPLIT, N_EXPERTS, LANES), lambda i: (i, 0, 0))],
        scratch_shapes=[pltpu.VMEM((N_EXPERTS, sub), F32), pltpu.VMEM((LANES, sub), F32),
                        pltpu.VMEM((kdim, d), BF16)],
        compiler_params=_cparams("arbitrary"),
        name="proj_residual_route",
    )(*x_parts, *a_parts, w, mods, ffn_g.reshape(1, d), w_router_t, router_bias.reshape(N_EXPERTS, 1), tri)


def dispatch_plan(idx, rank, cnt, n_sorted):
    t = idx.shape[1]
    n_tiles, tm = cnt.shape[0], t // cnt.shape[0]
    cnt_tile = cnt[:, :, 0].astype(I32)
    total = jnp.sum(cnt_tile, axis=0)
    padded = ((total + EXPERT_TILE - 1) // EXPERT_TILE) * EXPERT_TILE
    end = jnp.cumsum(padded)
    start = end - padded
    base = start[None, :] + jnp.cumsum(cnt_tile, axis=0) - cnt_tile
    idx3 = idx.reshape(TOP_K, n_tiles, tm)
    pos = rank.reshape(TOP_K, n_tiles, tm)
    for e in range(N_EXPERTS):
        pos = pos + jnp.where(idx3 == e, base[None, :, e, None], 0)
    tile_row = jnp.arange(n_sorted // EXPERT_TILE, dtype=I32) * EXPERT_TILE
    tile_expert = jnp.minimum(jnp.sum(end[None, :] <= tile_row[:, None], axis=1), N_EXPERTS - 1).astype(I32)
    n_valid = (end[-1] // EXPERT_TILE).astype(I32).reshape(1)
    return pos.reshape(TOP_K * t).astype(I32), tile_expert, n_valid


def _sc_mesh():
    return plsc.VectorSubcoreMesh(core_axis_name="c", subcore_axis_name="s")


def _sc_worker_id():
    return lax.axis_index("s") * V7X_SC_CORES + lax.axis_index("c")


def _sc_two_stage_pipeline(n_chunks, fetch_idx, load, store):
    assert n_chunks % 2 == 0
    fetch_idx(0, 0)
    load(0, 0).start()

    @pl.loop(0, n_chunks // 2)
    def _(p):
        j = 2 * p

        @pl.when(p > 0)
        def _():
            store(j - 1, 1).wait()

        fetch_idx(j + 1, 1)
        load(j + 1, 1).start()
        load(j, 0).wait()
        store(j, 0).start()
        store(j, 0).wait()

        @pl.when(j + 2 < n_chunks)
        def _():
            fetch_idx(j + 2, 0)
            load(j + 2, 0).start()

        load(j + 1, 1).wait()
        store(j + 1, 1).start()

    store(n_chunks - 1, 1).wait()


_SC_SCRATCH = lambda chunk, d, dtype: [
    pltpu.VMEM((chunk,), I32), pltpu.VMEM((chunk,), I32),
    pltpu.VMEM((chunk, d), dtype), pltpu.VMEM((chunk, d), dtype),
    pltpu.SemaphoreType.DMA, pltpu.SemaphoreType.DMA, pltpu.SemaphoreType.DMA, pltpu.SemaphoreType.DMA]


def sc_scatter_rows(src, pos, n_out_rows):
    t, d = src.shape
    n_idx = pos.shape[0]
    per_worker = n_idx // SC_WORKERS
    chunk = SC_CHUNK_ROWS
    assert n_idx % (SC_WORKERS * chunk * 2) == 0 and t % chunk == 0

    @functools.partial(
        pl.kernel, mesh=_sc_mesh(), out_type=jax.ShapeDtypeStruct((n_out_rows, d), src.dtype),
        scratch_types=_SC_SCRATCH(chunk, d, src.dtype), name="sc_scatter_rows")
    def scatter(src_hbm, pos_hbm, out_hbm, idx_a, idx_b, rows_a, rows_b, lsem_a, lsem_b, ssem_a, ssem_b):
        base = _sc_worker_id() * per_worker
        idx, rows, lsem, ssem = (idx_a, idx_b), (rows_a, rows_b), (lsem_a, lsem_b), (ssem_a, ssem_b)

        def fetch_idx(j, b):
            pltpu.sync_copy(pos_hbm.at[pl.ds(base + j * chunk, chunk)], idx[b])

        def load(j, b):
            return pltpu.make_async_copy(src_hbm.at[pl.ds(lax.rem(base + j * chunk, t), chunk)], rows[b], lsem[b])

        def store(j, b):
            return pltpu.make_async_copy(rows[b], out_hbm.at[idx[b]], ssem[b])

        _sc_two_stage_pipeline(per_worker // chunk, fetch_idx, load, store)

    return scatter(src, pos)


def sc_gather_rows(table, idx):
    _, d = table.shape
    n_idx = idx.shape[0]
    per_worker = n_idx // SC_WORKERS
    chunk = SC_CHUNK_ROWS
    assert n_idx % (SC_WORKERS * chunk * 2) == 0

    @functools.partial(
        pl.kernel, mesh=_sc_mesh(), out_type=jax.ShapeDtypeStruct((n_idx, d), table.dtype),
        scratch_types=_SC_SCRATCH(chunk, d, table.dtype), name="sc_gather_rows")
    def gather(table_hbm, idx_hbm, out_hbm, idx_a, idx_b, rows_a, rows_b, lsem_a, lsem_b, ssem_a, ssem_b):
        base = _sc_worker_id() * per_worker
        idx, rows, lsem, ssem = (idx_a, idx_b), (rows_a, rows_b), (lsem_a, lsem_b), (ssem_a, ssem_b)

        def fetch_idx(j, b):
            pltpu.sync_copy(idx_hbm.at[pl.ds(base + j * chunk, chunk)], idx[b])

        def load(j, b):
            return pltpu.make_async_copy(table_hbm.at[idx[b]], rows[b], lsem[b])

        def store(j, b):
            return pltpu.make_async_copy(rows[b], out_hbm.at[pl.ds(base + j * chunk, chunk)], ssem[b])

        _sc_two_stage_pipeline(per_worker // chunk, fetch_idx, load, store)

    return gather(table, idx)


def _experts_kernel(te_ref, nv_ref, x_ref, wg_ref, wu_ref, wd_ref, y_ref, wg_scr, wu_scr, wd_scr):
    i = pl.program_id(0)

    @pl.when(i < nv_ref[0])
    def _():
        @pl.when((i == 0) | (te_ref[i] != te_ref[jnp.maximum(i - 1, 0)]))
        def _():
            wg_scr[...] = wg_ref[0, 0].astype(BF16)
            wu_scr[...] = wu_ref[0, 0].astype(BF16)
            wd_scr[...] = wd_ref[0, 0].astype(BF16)

        x_lo, x_hi = _unpack_bf16_halves(x_ref[...])
        half = x_lo.shape[1]

        def in_proj(w_scr):
            return (jnp.dot(x_lo, w_scr[0:half, :], preferred_element_type=F32)
                    + jnp.dot(x_hi, w_scr[half:, :], preferred_element_type=F32))

        a = jax.nn.silu(in_proj(wg_scr)) * in_proj(wu_scr)
        y = jnp.dot(a.astype(BF16), wd_scr[...], preferred_element_type=F32)
        y_ref[...] = _pack_bf16_halves(y.astype(BF16))


def grouped_experts(xs, tile_expert, n_valid, wg, wu, wd, layer):
    n_rows = xs.shape[0]
    d, de = wg.shape[-2:]
    tm = EXPERT_TILE
    row_map = lambda i, te, nv: (jnp.minimum(i, nv[0] - 1), 0)
    grid_spec = pltpu.PrefetchScalarGridSpec(
        num_scalar_prefetch=2,
        grid=(n_rows // tm,),
        in_specs=[
            pl.BlockSpec((tm, d // 2), row_map),
            pl.BlockSpec((1, 1, d, de), lambda i, te, nv: (layer, te[i], 0, 0)),
            pl.BlockSpec((1, 1, d, de), lambda i, te, nv: (layer, te[i], 0, 0)),
            pl.BlockSpec((1, 1, de, d), lambda i, te, nv: (layer, te[i], 0, 0)),
        ],
        out_specs=pl.BlockSpec((tm, d // 2), row_map),
        scratch_shapes=[pltpu.VMEM((d, de), BF16), pltpu.VMEM((d, de), BF16), pltpu.VMEM((de, d), BF16)],
    )
    return pl.pallas_call(
        _experts_kernel,
        out_shape=jax.ShapeDtypeStruct((n_rows, d // 2), jnp.uint32),
        grid_spec=grid_spec,
        compiler_params=_cparams("arbitrary"),
        name="grouped_experts",
    )(tile_expert, n_valid, xs, wg, wu, wd)


def _final_combine_kernel(x_ref, y_ref, w_ref, mod_ref, fg_ref, o_ctx_ref, o_dec_ref, *, n_ctx_tiles):
    out = _moe_combine(x_ref[...], y_ref, w_ref, mod_ref)
    ms = jnp.mean(out * out, axis=-1, keepdims=True)
    out = out * lax.rsqrt(ms + NORM_EPS) * fg_ref[...]

    def store(o_ref):
        o_ref[...] = out
    _by_part(pl.program_id(0) >= n_ctx_tiles, (o_ctx_ref, o_dec_ref), store)


def final_combine(lay, x, y_pair, w_col, mods, final_g):
    d = x.shape[1]
    tm = lay.tm
    return pl.pallas_call(
        functools.partial(_final_combine_kernel, n_ctx_tiles=lay.n_ctx_tiles),
        out_shape=[jax.ShapeDtypeStruct((lay.t_ctx, d), F32), jax.ShapeDtypeStruct((lay.t_dec, d), F32)],
        grid=(lay.n_tiles,),
        in_specs=[
            pl.BlockSpec((tm, d), lambda i: (i, 0)),
            pl.BlockSpec((TOP_K, tm, d // 2), lambda i: (0, i, 0)),
            pl.BlockSpec((tm, LANES), lambda i: (i, 0)),
            pl.BlockSpec((1, N_MOD, d), lambda i: (lay.mod_row(i), 0, 0)),
            pl.BlockSpec((1, d), lambda i: (0, 0)),
        ],
        out_specs=lay.row_specs([None, None], d),
        compiler_params=_cparams("arbitrary"),
        name="final_combine",
    )(x, y_pair, w_col, mods, final_g.reshape(1, d))


def group_moe(lay, x, routing, p, layer):
    t, d = x.shape
    h, idx, w_col, rank, cnt = routing
    n_sorted = TOP_K * t + N_EXPERTS * (EXPERT_TILE - 1)
    n_sorted = -(-n_sorted // EXPERT_TILE) * EXPERT_TILE
    pos, tile_expert, n_valid = dispatch_plan(idx, rank, cnt, n_sorted)
    xs = sc_scatter_rows(h, pos, n_sorted)
    ys = grouped_experts(xs, tile_expert, n_valid, p["moe_w_gate"], p["moe_w_up"], p["moe_w_down"], layer)
    return sc_gather_rows(ys, pos).reshape(TOP_K, t, d // 2), w_col


def _rope_angles(n, d):
    n_rows = n // GRID_W
    row = jnp.repeat(jnp.arange(n_rows), GRID_W).astype(F32)
    col = jnp.tile(jnp.arange(GRID_W), n_rows).astype(F32)
    nf = d // 4
    freqs = jnp.power(ROPE_BASE, -jnp.arange(nf, dtype=F32) / nf)
    ang = jnp.concatenate([row[:, None] * freqs, col[:, None] * freqs], axis=-1)
    return jnp.cos(ang), jnp.sin(ang)


def kernel(x_prompt, x_sample, cache_attn_k, cache_attn_v, state_ret_fwd, state_ret_bwd, c, c_ctx, w_ada, b_ada, norm_mix_g, norm_ffn_g, final_norm_g, da_w_qkv, da_lambda_q1, da_lambda_k1, da_lambda_q2, da_lambda_k2, da_subln_g, da_w_o, ret_w_qkv, ret_w_gate_fwd, ret_w_gate_bwd, ret_decay_fwd, ret_decay_bwd, ret_w_o, w_router, router_bias, moe_w_gate, moe_w_up, moe_w_down):
    b_ctx, n_ctx, d = x_prompt.shape
    b_dec, n_dec, _ = x_sample.shape
    past = cache_attn_k.shape[2]
    n_attn = cache_attn_k.shape[1]
    depth = w_ada.shape[0]
    assert b_dec + 1 <= MOD_ROWS
    lay = Layout(b_ctx, n_ctx, b_dec, n_dec)

    cond = jnp.zeros((MOD_ROWS, d), F32).at[0].set(c_ctx).at[1:1 + b_dec].set(c)
    mods_all = ada_modulation(cond, w_ada, b_ada)

    p = {
        "w_router_t": w_router.T.astype(BF16), "router_bias": router_bias.astype(F32),
        "moe_w_gate": moe_w_gate, "moe_w_up": moe_w_up, "moe_w_down": moe_w_down,
    }
    ret_log_decay = jnp.stack([jax.nn.log_sigmoid(ret_decay_fwd.astype(F32)),
                               jax.nn.log_sigmoid(ret_decay_bwd.astype(F32))], axis=1)
    ck_all = cache_attn_k.reshape(b_dec, n_attn, past, -1)
    cv_all = cache_attn_v.reshape(b_dec, n_attn, past, -1)

    x_parts = [x_prompt.reshape(lay.t_ctx, d), x_sample.reshape(lay.t_dec, d)]
    new_k, new_v, new_sf, new_sb = [], [], [], []
    pending_moe = None

    def first_kernel(*args):
        nonlocal x_parts, pending_moe
        outs = norm_mod_matmul(lay, x_parts, *args, pending_moe=pending_moe)
        if pending_moe is not None:
            *outs, x_joint = outs
            x_parts, pending_moe = [x_joint], None
        return outs

    for i in range(depth):
        mods = mods_all[i]
        j = i // 2
        if i % 2 == 0:
            lam_init = 0.8 - 0.6 * math.exp(-0.3 * i)
            qkw = DA_HEADS * 2 * DA_HEAD_DIM
            vw = DA_HEADS * DA_V_DIM
            cos, sin = _rope_angles(n_dec, DA_HEAD_DIM)
            rope = (jnp.tile(cos, (1, 4)), jnp.concatenate([-sin, sin, -sin, sin], axis=-1))
            q, k, v, k_ctx, v_ctx = first_kernel(
                norm_mix_g[i], mods, da_w_qkv[j],
                [(0, qkw, "rope64", math.log2(math.e) * DA_HEAD_DIM ** -0.5, BF16, "all"),
                 (qkw, qkw, "rope64", 1.0, BF16, "all"),
                 (2 * qkw, vw, "plain", 1.0, BF16, "all"),
                 (qkw, qkw, "plain", 1.0, F32, "ctx64"), (2 * qkw, vw, "plain", 1.0, F32, "ctx")],
                rope)
            lam_params = jnp.stack([da_lambda_q1[j], da_lambda_k1[j], da_lambda_q2[j], da_lambda_k2[j]])
            mix = [diff_attention(q, k, v, 0, b_ctx, n_ctx, lam_params, da_subln_g[j], lam_init,
                                  heads_per_step=DA_HEADS),
                   diff_attention(q, k, v, lay.t_ctx, b_dec, n_dec, lam_params, da_subln_g[j], lam_init,
                                  cache=(ck_all[:, j], cv_all[:, j]))]
            mix_w = da_w_o[j]
            new_k.append(k_ctx.reshape(b_ctx, n_ctx, DA_HEADS, 2, DA_HEAD_DIM))
            new_v.append(v_ctx.reshape(b_ctx, n_ctx, DA_HEADS, DA_V_DIM))
        else:
            kd = ret_w_qkv.shape[2] // 4
            dv = 2 * kd
            w_all = jnp.concatenate([ret_w_qkv[j], ret_w_gate_fwd[j], ret_w_gate_bwd[j]], axis=-1).astype(BF16)
            q, k, v, gates = first_kernel(
                norm_mix_g[i], mods, w_all,
                [(0, kd, "rope256", 1.0, BF16, "all"),
                 (kd, kd, "rope256", (kd // RET_HEADS) ** -0.5, F32, "all"),
                 (2 * kd, dv, "plain", 1.0, BF16, "all"), (2 * kd + dv, 2 * dv, "silu", 1.0, BF16, "all")],
                _rope_angles(n_dec, kd // RET_HEADS))
            ctx_seqs = math.gcd(b_ctx, max(1, n_dec // n_ctx // 2))
            o_ctx, sf, sb = retention(q, k, v, gates, 0, b_ctx, n_ctx, ret_log_decay[j], emit_state=True,
                                      seqs_per_step=ctx_seqs)
            (o_dec,) = retention(q, k, v, gates, lay.t_ctx, b_dec, n_dec, ret_log_decay[j],
                                 state=(state_ret_fwd[:, j:j + 1], state_ret_bwd[:, j:j + 1]))
            mix, mix_w = [o_ctx, o_dec], ret_w_o[j]
            new_sf.append(sf)
            new_sb.append(sb)
        x, *routing = proj_residual_route(lay, x_parts, mix, mix_w, mods, norm_ffn_g[i], p["w_router_t"],
                                          p["router_bias"])
        y_pair, w_col = group_moe(lay, x, routing, p, i)
        if i == depth - 1:
            x_parts = final_combine(lay, x, y_pair, w_col, mods, final_norm_g)
        else:
            x_parts, pending_moe = [x], (y_pair, w_col, mods)

    y_ctx, y_dec = x_parts
    return (y_ctx.reshape(b_ctx, n_ctx, d), y_dec.reshape(b_dec, n_dec, d),
            jnp.stack(new_k, axis=1), jnp.stack(new_v, axis=1),
            jnp.concatenate(new_sf, axis=1), jnp.concatenate(new_sb, axis=1))
```

```python
import functools
import math

import jax
import jax.numpy as jnp
from jax import lax
from jax.experimental import pallas as pl
from jax.experimental.pallas import tpu as pltpu
from jax.experimental.pallas import tpu_sc as plsc

F32 = jnp.float32
BF16 = jnp.bfloat16
I32 = jnp.int32

GRID_W = 64
ROPE_BASE = 10000.0
NORM_EPS = 1e-6
DA_HEADS = 8
DA_HEAD_DIM = 64
DA_V_DIM = 2 * DA_HEAD_DIM
RET_HEADS = 4
RET_CHUNK = 128
N_EXPERTS = 16
N_GROUPS = 4
EXPERTS_PER_GROUP = N_EXPERTS // N_GROUPS
TOP_K = 2
N_MOD = 6
MOD_ROWS = 16
LANES = 128

V7X_VMEM_LIMIT = 56 * 1024 * 1024
V7X_SC_CORES = 2
V7X_SC_SUBCORES = 16
SC_WORKERS = V7X_SC_CORES * V7X_SC_SUBCORES
SC_CHUNK_ROWS = 64

ROW_TILE = 512
EXPERT_TILE = 512
ROUTE_SPLIT = 2


def _cparams(*sem):
    return pltpu.CompilerParams(dimension_semantics=sem, vmem_limit_bytes=V7X_VMEM_LIMIT)


def _nt_dot(a, b):
    return lax.dot_general(a, b, (((1,), (1,)), ((), ())), preferred_element_type=F32)


def _tn_dot(a, b):
    return lax.dot_general(a, b, (((0,), (0,)), ((), ())), preferred_element_type=F32)


class Layout:
    def __init__(self, b_ctx, n_ctx, b_dec, n_dec, row_tile=ROW_TILE):
        self.b_ctx, self.n_ctx, self.b_dec, self.n_dec = b_ctx, n_ctx, b_dec, n_dec
        self.t_ctx, self.t_dec = b_ctx * n_ctx, b_dec * n_dec
        self.t = self.t_ctx + self.t_dec
        self.tm = min(row_tile, n_dec, self.t_ctx)
        assert self.t_ctx % self.tm == 0 and n_dec % self.tm == 0
        assert self.t_ctx % n_dec == 0 and self.t_ctx % n_ctx == 0
        self.n_ctx_tiles = self.t_ctx // self.tm
        self.n_tiles = self.t // self.tm

    def mod_row(self, i):
        r = i * self.tm
        return jnp.where(r < self.t_ctx, 0, 1 + (r - self.t_ctx) // self.n_dec)

    def part_tile(self, part, i):
        if part == 0:
            return jnp.minimum(i, self.n_ctx_tiles - 1)
        return jnp.maximum(i - self.n_ctx_tiles, 0)

    def row_specs(self, arrays, width):
        if len(arrays) == 1:
            return [pl.BlockSpec((self.tm, width), lambda i, *_: (i, 0))]
        return [pl.BlockSpec((self.tm, width), lambda i, *_, p=p: (self.part_tile(p, i), 0)) for p in (0, 1)]


def _ada_kernel(c_ref, w_ref, b_ref, o_ref):
    s = jax.nn.silu(c_ref[...]).astype(BF16)
    acc = jnp.dot(s, w_ref[0].astype(BF16), preferred_element_type=F32)
    o_ref[0] = acc + b_ref[0]


def ada_modulation(cond, w_ada, b_ada):
    depth, d, n = w_ada.shape
    tn = 1536
    out = pl.pallas_call(
        _ada_kernel,
        out_shape=jax.ShapeDtypeStruct((depth, MOD_ROWS, n), F32),
        grid=(depth, n // tn),
        in_specs=[
            pl.BlockSpec((MOD_ROWS, d), lambda l, j: (0, 0)),
            pl.BlockSpec((1, d, tn), lambda l, j: (l, 0, j)),
            pl.BlockSpec((1, 1, tn), lambda l, j: (l, 0, j)),
        ],
        out_specs=pl.BlockSpec((1, MOD_ROWS, tn), lambda l, j: (l, 0, j)),
        compiler_params=_cparams("parallel", "parallel"),
        name="ada_modulation",
    )(cond, w_ada, b_ada.reshape(depth, 1, n))
    return out.reshape(depth, MOD_ROWS, N_MOD, d)


def _modulated_norm(x, g, mod_ref, shift_idx):
    ms = jnp.mean(x * x, axis=-1, keepdims=True)
    y = x * lax.rsqrt(ms + NORM_EPS) * g
    return y * (1.0 + mod_ref[0, shift_idx + 1:shift_idx + 2, :]) + mod_ref[0, shift_idx:shift_idx + 1, :]


def _by_part(is_dec, refs, fn):
    if len(refs) == 1:
        fn(refs[0])
        return
    pl.when(jnp.logical_not(is_dec))(lambda: fn(refs[0]))
    pl.when(is_dec)(lambda: fn(refs[1]))


def _rope64(a, cos, sin_signed, first_half):
    partner = jnp.where(first_half, pltpu.roll(a, 96, 1), pltpu.roll(a, 32, 1))
    return a * cos + partner * sin_signed


def _moe_combine(x, y_ref, w_ref, mod_ref):
    w = w_ref[...]

    def expert_out(slot):
        lo, hi = _unpack_bf16_halves(y_ref[slot])
        return jnp.concatenate([lo.astype(F32), hi.astype(F32)], axis=1)

    return x + mod_ref[0, 5:6, :] * (w[:, 0:1] * expert_out(0) + w[:, 1:2] * expert_out(1))


def _nmm_kernel(*refs, n_x, pending_moe, cast_w, outs, tn, n_ctx_tiles):
    x_refs = refs[:n_x]
    if pending_moe:
        y_ref, wcol_ref, prev_mod_ref = refs[n_x:n_x + 3]
        refs = refs[:n_x] + refs[n_x + 3:]
    g_ref, mod_ref, w_ref, cos_ref, sin_ref = refs[n_x:n_x + 5]
    n_out = len(outs) + (1 if pending_moe else 0)
    out_refs = refs[n_x + 5:n_x + 5 + n_out]
    if cast_w:
        w_f32_ref, w_ref = w_ref, refs[n_x + 5 + n_out]

        @pl.when(pl.program_id(0) == 0)
        def _():
            w_ref[...] = w_f32_ref[...].astype(BF16)
    is_dec = pl.program_id(0) >= n_ctx_tiles
    tm = x_refs[0].shape[0]
    segments = sorted({(col0, width) for col0, width, _, _, _ in outs})

    def rotated(a, kind, rows):
        cos, sin = cos_ref[rows, :], sin_ref[rows, :]
        if kind == "rope64":
            lane = lax.broadcasted_iota(I32, cos.shape, 1)
            first_half = (lane & 32) == 0
            return [(c * 128, _rope64(a[:, c * 128:(c + 1) * 128], cos, sin, first_half))
                    for c in range(tn // 128)]
        pieces = []
        for c in range(tn // 256):
            x1, x2 = a[:, c * 256:c * 256 + 128], a[:, c * 256 + 128:(c + 1) * 256]
            pieces += [(c * 256, x1 * cos - x2 * sin), (c * 256 + 128, x2 * cos + x1 * sin)]
        return pieces

    def emit(dec):
        x_ref = x_refs[(1 if dec else 0) if n_x == 2 else 0]
        for r0 in (0, tm // 2):
            rows = slice(r0, r0 + tm // 2)
            x = x_ref[rows, :]
            if pending_moe:
                x = _moe_combine(x, y_ref.at[:, rows, :], wcol_ref.at[rows, :], prev_mod_ref)
                out_refs[-1][rows, :] = x
            h = _modulated_norm(x, g_ref[...], mod_ref, 0).astype(BF16)
            for col0, width in segments:
                sinks = [(o_ref, o) for o_ref, o in zip(out_refs, outs)
                         if (o[0], o[1]) == (col0, width) and not (dec and o[4] != "all")]
                for blk in range(width // tn):
                    cols = slice(blk * tn, (blk + 1) * tn)
                    acc = jnp.dot(h, w_ref[:, col0 + blk * tn:col0 + (blk + 1) * tn], preferred_element_type=F32)
                    for o_ref, (_, _, kind, scale, out_rows) in sinks:
                        a = acc if scale == 1.0 else acc * scale
                        if out_rows == "ctx64":
                            n64 = width // 64
                            for c in range(tn // 64):
                                o_ref[pl.ds(r0 * n64 + blk * tn // 64 + c, tm // 2, stride=n64), :] = (
                                    a[:, c * 64:(c + 1) * 64].astype(o_ref.dtype))
                        elif kind == "silu":
                            o_ref[rows, cols] = jax.nn.silu(a).astype(o_ref.dtype)
                        elif kind == "plain" or out_rows == "ctx" or not dec:
                            o_ref[rows, cols] = a.astype(o_ref.dtype)
                        else:
                            for off, val in rotated(a, kind, rows):
                                o_ref[rows, blk * tn + off:blk * tn + off + 128] = val.astype(o_ref.dtype)

    pl.when(jnp.logical_not(is_dec))(lambda: emit(False))
    pl.when(is_dec)(lambda: emit(True))


def norm_mod_matmul(lay, x_parts, g, mods, w, outputs, rope_tables, pending_moe=None, tn=512):
    d = x_parts[0].shape[1]
    tm = lay.tm
    n_total = w.shape[1]
    outs, out_shapes, out_specs = [], [], []
    for col0, width, kind, scale, dtype, rows in outputs:
        assert width % tn == 0 and col0 % LANES == 0
        outs.append((col0, width, kind, float(scale), rows))
        if rows == "ctx64":
            n64 = width // 64
            out_shapes.append(jax.ShapeDtypeStruct((lay.t_ctx * n64, 64), dtype))
            out_specs.append(pl.BlockSpec((tm * n64, 64), lambda i: (lay.part_tile(0, i), 0)))
            continue
        n_rows = lay.t_ctx if rows == "ctx" else lay.t
        out_shapes.append(jax.ShapeDtypeStruct((n_rows, width), dtype))
        out_specs.append(lay.row_specs([None, None], width)[0] if rows == "ctx" else lay.row_specs([None], width)[0])
    blocks_per_seq = lay.n_dec // tm
    rope_spec = pl.BlockSpec((tm, LANES), lambda i: (lay.part_tile(1, i) % blocks_per_seq, 0))
    mod_spec = pl.BlockSpec((1, N_MOD, d), lambda i: (lay.mod_row(i), 0, 0))
    moe_specs, moe_args = [], []
    if pending_moe is not None:
        assert len(x_parts) == 1
        moe_specs = [pl.BlockSpec((TOP_K, tm, d // 2), lambda i: (0, i, 0)),
                     pl.BlockSpec((tm, LANES), lambda i: (i, 0)), mod_spec]
        moe_args = list(pending_moe)
        out_shapes.append(jax.ShapeDtypeStruct((lay.t, d), F32))
        out_specs.append(pl.BlockSpec((tm, d), lambda i: (i, 0)))
    return pl.pallas_call(
        functools.partial(_nmm_kernel, n_x=len(x_parts), pending_moe=pending_moe is not None,
                          cast_w=w.dtype != BF16, outs=tuple(outs), tn=tn, n_ctx_tiles=lay.n_ctx_tiles),
        out_shape=out_shapes,
        grid=(lay.n_tiles,),
        in_specs=lay.row_specs(x_parts, d) + moe_specs + [
            pl.BlockSpec((1, d), lambda i: (0, 0)),
            mod_spec,
            pl.BlockSpec((d, n_total), lambda i: (0, 0), pipeline_mode=pl.Buffered(1)),
            rope_spec, rope_spec,
        ],
        out_specs=out_specs,
        scratch_shapes=[pltpu.VMEM(w.shape, BF16)] if w.dtype != BF16 else [],
        compiler_params=_cparams("arbitrary"),
        name="norm_mod_matmul",
    )(*x_parts, *moe_args, g.reshape(1, d), mods, w, *rope_tables)


def _diff_lambda(lam_ref, lam_init):
    lp = lam_ref[...]
    return (jnp.exp(jnp.sum(lp[0:1] * lp[1:2], axis=-1, keepdims=True))
            - jnp.exp(jnp.sum(lp[2:3] * lp[3:4], axis=-1, keepdims=True)) + lam_init)


ONES_ROWS = 16


def _scores_t(k, q, comp):
    lane = lax.broadcasted_iota(I32, q.shape, 1)
    return _nt_dot(k, jnp.where((lane < DA_HEAD_DIM) == (comp == 0), q, jnp.zeros_like(q)))


def _softmax_values_t(scores, vts):
    dv = vts[0].shape[0] - ONES_ROWS
    maxes = [jnp.max(s, axis=0, keepdims=True) for s in scores]
    exps = [jnp.exp2(s - m).astype(BF16) for s, m in zip(scores, maxes)]
    accs = [jnp.dot(vt, e, preferred_element_type=F32) for vt, e in zip(vts, exps)]
    return [acc[0:dv] / acc[dv:dv + 1] for acc in accs]


def _diff_finish_t(parts, lam, g_col, lam_init):
    o = parts[0] - lam * parts[1]
    ms = jnp.mean(o * o, axis=0, keepdims=True)
    return jnp.transpose((o * lax.rsqrt(ms + NORM_EPS) * g_col) * (1.0 - lam_init))


def _diff_attn_kernel(lam_ref, q_ref, k_ref, v_ref, g_ref, o_ref, *, lam_init, heads_per_step):
    hw = 2 * DA_HEAD_DIM
    lam = _diff_lambda(lam_ref, lam_init)
    ones = jnp.ones((ONES_ROWS, k_ref.shape[0]), BF16)
    heads = [slice(hh * hw, (hh + 1) * hw) for hh in range(heads_per_step)]
    scores = [[_scores_t(k_ref[:, cols], q_ref[:, cols], comp) for comp in range(2)] for cols in heads]
    vts = [jnp.concatenate([jnp.transpose(v_ref[:, cols].astype(F32)).astype(BF16), ones], axis=0) for cols in heads]
    parts = _softmax_values_t([s for pair in scores for s in pair], [vt for vt in vts for _ in range(2)])
    for hh, cols in enumerate(heads):
        o_ref[:, cols] = _diff_finish_t(parts[2 * hh:2 * hh + 2], lam, g_ref[...], lam_init).astype(o_ref.dtype)


def _diff_attn_cached_kernel(lam_ref, q_ref, k_ref, v_ref, ck_ref, cv_ref, g_ref, o_ref, k_scr, vt_scr, s_scr, *,
                             lam_init, sub_rows):
    seq_len, hw = q_ref.shape
    n_sub = seq_len // sub_rows
    assert n_sub % 2 == 0
    lam = _diff_lambda(lam_ref, lam_init)
    k_scr[0:seq_len, :] = k_ref[...]
    k_scr[seq_len:, :] = ck_ref[0].astype(BF16)
    vt_scr[0:hw, 0:seq_len] = jnp.transpose(v_ref[...])
    vt_scr[0:hw, seq_len:] = jnp.transpose(cv_ref[0]).astype(BF16)
    vt_scr[hw:, :] = jnp.ones((ONES_ROWS, vt_scr.shape[1]), BF16)

    def rows_of(t):
        return pl.ds(pl.multiple_of(t * sub_rows, sub_rows), sub_rows)

    def scores(t, slot, comp):
        s_scr[slot, comp] = _scores_t(k_scr[...], q_ref[rows_of(t), :], comp)

    def stage(t_next, slot_next, t, slot):
        for comp in range(2):
            scores(t_next, slot_next, comp)
        parts = _softmax_values_t([s_scr[slot, 0], s_scr[slot, 1]], [vt_scr[...]] * 2)
        o_ref[rows_of(t), :] = _diff_finish_t(parts, lam, g_ref[...], lam_init).astype(o_ref.dtype)

    scores(0, 0, 0)
    scores(0, 0, 1)

    def body(i2, carry):
        t = 2 * i2
        stage(t + 1, 1, t, 0)
        stage(jnp.minimum(t + 2, n_sub - 1), 0, t + 1, 1)
        return carry

    lax.fori_loop(0, n_sub // 2, body, 0, unroll=2 if n_sub % 4 == 0 else 1)


def diff_attention(q, k, v, row0, batch, seq_len, lam_params, subln_g, lam_init, cache=None, heads_per_step=1,
                   sub_rows=256):
    width = q.shape[1]
    hw = 2 * DA_HEAD_DIM
    bw = heads_per_step * hw
    assert row0 % seq_len == 0 and DA_HEADS % heads_per_step == 0
    s0 = row0 // seq_len
    seq_spec = pl.BlockSpec((seq_len, bw), lambda b, h: (s0 + b, h))
    in_specs = [pl.BlockSpec((4, DA_HEAD_DIM), lambda b, h: (0, 0)), seq_spec, seq_spec, seq_spec]
    args = [lam_params, q, k, v]
    scratch = []
    if cache is None:
        body = functools.partial(_diff_attn_kernel, lam_init=lam_init, heads_per_step=heads_per_step)
    else:
        assert heads_per_step == 1
        past = cache[0].shape[1]
        cache_spec = pl.BlockSpec((1, past, hw), lambda b, h: (b, 0, h))
        in_specs += [cache_spec, cache_spec]
        args += list(cache)
        sub_rows = min(sub_rows, seq_len // 2)
        scratch = [pltpu.VMEM((seq_len + past, hw), BF16), pltpu.VMEM((hw + ONES_ROWS, seq_len + past), BF16),
                   pltpu.VMEM((2, 2, seq_len + past, sub_rows), F32)]
        body = functools.partial(_diff_attn_cached_kernel, lam_init=lam_init, sub_rows=sub_rows)
    in_specs.append(pl.BlockSpec((hw, 1), lambda b, h: (0, 0)))
    args.append(subln_g.reshape(hw, 1))
    return pl.pallas_call(
        body,
        out_shape=jax.ShapeDtypeStruct((batch * seq_len, width), BF16),
        grid=(batch, DA_HEADS // heads_per_step),
        in_specs=in_specs,
        out_specs=pl.BlockSpec((seq_len, bw), lambda b, h: (b, h)),
        scratch_shapes=scratch,
        compiler_params=_cparams("parallel", "parallel"),
        name="diff_attention",
    )(*args)


def _retention_kernel(*refs, seq_len, seqs, has_state, emit_state):
    lg_ref, q_ref, k_ref, v_ref, gf_ref, gb_ref = refs[:6]
    pos = 6
    if has_state:
        s0_refs = refs[6:8]
        pos = 8
    o_ref = refs[pos]
    pos += 1
    if emit_state:
        s_out_refs = refs[pos:pos + 2]
        pos += 2
    s_scr, o_scr = refs[pos:pos + 2]

    h = pl.program_id(1)
    c_len = RET_CHUNK
    n_chunks = seq_len // c_len
    assert n_chunks % 2 == 0
    row = lax.broadcasted_iota(I32, (c_len, c_len), 0)
    colm = lax.broadcasted_iota(I32, (c_len, c_len), 1)
    rel = (row - colm).astype(F32)
    posv = lax.broadcasted_iota(I32, (c_len, 1), 0).astype(F32)

    consts = []
    for backward in (False, True):
        lg = lg_ref[1 if backward else 0, h]
        if backward:
            intra = jnp.where(rel <= 0, jnp.exp(-rel * lg), 0.0)
            q_decay = jnp.exp((c_len - posv) * lg)
            k_decay = jnp.exp(posv * lg)
        else:
            intra = jnp.where(rel >= 0, jnp.exp(rel * lg), 0.0)
            q_decay = jnp.exp((posv + 1.0) * lg)
            k_decay = jnp.exp((c_len - 1.0 - posv) * lg)
        consts.append((intra, q_decay, k_decay, jnp.exp(jnp.zeros((1, 1), F32) + c_len * lg)))
        direction = 1 if backward else 0
        for sq in range(seqs):
            if has_state:
                s_scr[sq, direction] = s0_refs[direction][sq, 0, 0]
            else:
                s_scr[sq, direction] = jnp.zeros(s_scr.shape[2:], F32)

    def body(ci, first_touch):
        chains = [(sq, direction, pl.ds(pl.multiple_of(sq * seq_len + c * c_len, c_len), c_len))
                  for sq in range(seqs) for direction, c in ((0, ci), (1, n_chunks - 1 - ci))]
        qs = [q_ref[rows, :].astype(BF16) for _, _, rows in chains]
        ks = [k_ref[rows, :].astype(F32) for _, _, rows in chains]
        vs = [v_ref[rows, :].astype(BF16) for _, _, rows in chains]
        intras = [(_nt_dot(qb, kf.astype(BF16)) * consts[d][0]).astype(BF16)
                  for (_, d, _), qb, kf in zip(chains, qs, ks)]
        states = [s_scr[sq, d] for sq, d, _ in chains]
        outs = [jnp.dot(a, vb, preferred_element_type=F32)
                + jnp.dot(qb, s.astype(BF16), preferred_element_type=F32) * consts[d][1]
                for (_, d, _), a, qb, vb, s in zip(chains, intras, qs, vs, states)]
        for (sq, d, _), kf, vb, s in zip(chains, ks, vs, states):
            s_scr[sq, d] = consts[d][3] * s + _tn_dot((kf * consts[d][2]).astype(BF16), vb)
        centred = [o - jnp.mean(o, axis=-1, keepdims=True) for o in outs]
        scales = [lax.rsqrt(jnp.mean(oc * oc, axis=-1, keepdims=True) + NORM_EPS) for oc in centred]
        for (_, d, rows), oc, scale in zip(chains, centred, scales):
            gated = oc * scale * (gb_ref if d else gf_ref)[rows, :].astype(F32)
            if first_touch:
                o_scr[rows, :] = gated
            else:
                o_ref[rows, :] = (o_scr[rows, :] + gated).astype(o_ref.dtype)

    unroll = 8 if n_chunks % 16 == 0 else 1
    lax.fori_loop(0, n_chunks // 2, lambda ci, c: (body(ci, True), c)[1], 0, unroll=unroll)
    lax.fori_loop(n_chunks // 2, n_chunks, lambda ci, c: (body(ci, False), c)[1], 0, unroll=unroll)
    if emit_state:
        for d in range(2):
            for sq in range(seqs):
                s_out_refs[d][sq, 0, 0] = s_scr[sq, d]


def retention(q, k, v, gates, row0, batch, seq_len, log_decay, state=None, emit_state=False, seqs_per_step=1):
    dk = q.shape[1] // RET_HEADS
    dv = v.shape[1] // RET_HEADS
    seqs = seqs_per_step
    rows = seqs * seq_len
    assert row0 % rows == 0 and batch % seqs == 0
    s0 = row0 // rows
    in_specs = [
        pl.BlockSpec(memory_space=pltpu.SMEM),
        pl.BlockSpec((rows, dk), lambda b, h: (s0 + b, h)),
        pl.BlockSpec((rows, dk), lambda b, h: (s0 + b, h)),
        pl.BlockSpec((rows, dv), lambda b, h: (s0 + b, h)),
        pl.BlockSpec((rows, dv), lambda b, h: (s0 + b, h)),
        pl.BlockSpec((rows, dv), lambda b, h: (s0 + b, RET_HEADS + h)),
    ]
    args = [log_decay, q, k, v, gates, gates]
    state_spec = pl.BlockSpec((seqs, 1, 1, dk, dv), lambda b, h: (b, 0, h, 0, 0))
    if state is not None:
        in_specs += [state_spec, state_spec]
        args += list(state)
    out_shapes = [jax.ShapeDtypeStruct((batch * seq_len, RET_HEADS * dv), BF16)]
    out_specs = [pl.BlockSpec((rows, dv), lambda b, h: (b, h))]
    if emit_state:
        out_shapes += [jax.ShapeDtypeStruct((batch, 1, RET_HEADS, dk, dv), F32)] * 2
        out_specs += [state_spec, state_spec]
    return pl.pallas_call(
        functools.partial(_retention_kernel, seq_len=seq_len, seqs=seqs, has_state=state is not None,
                          emit_state=emit_state),
        out_shape=out_shapes,
        grid=(batch // seqs, RET_HEADS),
        in_specs=in_specs,
        out_specs=out_specs,
        scratch_shapes=[pltpu.VMEM((seqs, 2, dk, dv), F32), pltpu.VMEM((rows, dv), F32)],
        compiler_params=_cparams("parallel", "parallel"),
        name="retention",
    )(*args)


def _pack_bf16_halves(x):
    half = x.shape[1] // 2
    lo = lax.bitcast_convert_type(x[:, :half].astype(F32), jnp.uint32)
    hi = lax.bitcast_convert_type(x[:, half:].astype(F32), jnp.uint32)
    return (hi & jnp.uint32(0xFFFF0000)) | (lo >> 16)


def _unpack_bf16_halves(p):
    lo = lax.bitcast_convert_type(p << 16, F32).astype(BF16)
    hi = lax.bitcast_convert_type(p & jnp.uint32(0xFFFF0000), F32).astype(BF16)
    return lo, hi


def _first_max_onehot(vals):
    m = vals[0]
    for v in vals[1:]:
        m = jnp.maximum(m, v)
    onehot, taken = [], None
    for v in vals:
        hit = v == m
        if taken is None:
            onehot.append(hit)
            taken = hit
        else:
            onehot.append(hit & jnp.logical_not(taken))
            taken = taken | hit
    return m, onehot


def _pick(onehot, vals):
    out = vals[-1]
    for oh, v in zip(onehot[-2::-1], vals[-2::-1]):
        out = jnp.where(oh, v, out)
    return out


def _route_rows(x, g_ref, mod_ref, wr_ref, bias_ref, tri_ref, h_ref, idx_ref, wcol_ref, rank_ref, cnt_ref,
                ind_scr, wrow_scr):
    hb = _modulated_norm(x, g_ref[...], mod_ref, 3).astype(BF16)
    h_ref[...] = _pack_bf16_halves(hb)
    logits = _nt_dot(wr_ref[...], hb)
    s = jax.nn.sigmoid(logits)
    sel = s + bias_ref[...]
    neg_inf = jnp.full_like(sel[0:1], -jnp.inf)
    sel_rows = [sel[e:e + 1] for e in range(N_EXPERTS)]
    s_rows = [s[e:e + 1] for e in range(N_EXPERTS)]

    def top2(vals):
        m1, oh1 = _first_max_onehot(vals)
        rest = [jnp.where(o, neg_inf, v) for o, v in zip(oh1, vals)]
        m2, oh2 = _first_max_onehot(rest)
        return m1, m2, oh1, oh2

    grp_scores = []
    for g in range(N_GROUPS):
        m1, m2, _, _ = top2(sel_rows[g * EXPERTS_PER_GROUP:(g + 1) * EXPERTS_PER_GROUP])
        grp_scores.append(m1 + m2)
    _, in_grp = _first_max_onehot(grp_scores)
    cand_sel = [_pick(in_grp, [sel_rows[g * EXPERTS_PER_GROUP + k] for g in range(N_GROUPS)])
                for k in range(EXPERTS_PER_GROUP)]
    cand_s = [_pick(in_grp, [s_rows[g * EXPERTS_PER_GROUP + k] for g in range(N_GROUPS)])
              for k in range(EXPERTS_PER_GROUP)]
    _, _, oh1, oh2 = top2(cand_sel)
    w1 = _pick(oh1, cand_s)
    w2 = _pick(oh2, cand_s)
    denom = w1 + w2
    wrow_scr[...] = jnp.zeros_like(wrow_scr)
    wrow_scr[0:1, :] = w1 / denom
    wrow_scr[1:2, :] = w2 / denom
    wcol_ref[...] = jnp.transpose(wrow_scr[...])

    ints = [jnp.full(w1.shape, k, I32) for k in range(EXPERTS_PER_GROUP)]
    grp = _pick(in_grp, ints) * EXPERTS_PER_GROUP
    idx_ref[0:1, :] = grp + _pick(oh1, ints)
    idx_ref[1:2, :] = grp + _pick(oh2, ints)

    one, zero = jnp.ones_like(w1), jnp.zeros_like(w1)
    for g in range(N_GROUPS):
        for k in range(EXPERTS_PER_GROUP):
            e = g * EXPERTS_PER_GROUP + k
            ind_scr[e:e + 1, :] = jnp.where(in_grp[g] & (oh1[k] | oh2[k]), one, zero)
    ind = ind_scr[...].astype(BF16)
    ranks = jnp.dot(ind, tri_ref[...], preferred_element_type=F32)
    cnt_ref[0] = jnp.dot(ind, jnp.ones((ind.shape[1], LANES), BF16), preferred_element_type=F32)
    for slot, oh in ((0, oh1), (1, oh2)):
        r = zero
        for g in range(N_GROUPS):
            for k in range(EXPERTS_PER_GROUP):
                e = g * EXPERTS_PER_GROUP + k
                r = r + jnp.where(in_grp[g] & oh[k], ranks[e:e + 1], zero)
        rank_ref[slot:slot + 1, :] = r.astype(I32)


def _proj_route_kernel(*refs, n_x, n_a, n_ctx_tiles):
    x_refs, a_refs = refs[:n_x], refs[n_x:n_x + n_a]
    w_ref, mod_ref, g_ref, wr_ref, bias_ref, tri_ref, o_ref = refs[n_x + n_a:n_x + n_a + 7]
    route_out_refs = refs[n_x + n_a + 7:n_x + n_a + 12]
    ind_scr, wrow_scr, w_scr = refs[n_x + n_a + 12:]
    is_dec = pl.program_id(0) >= n_ctx_tiles

    @pl.when(pl.program_id(0) == 0)
    def _():
        w_scr[...] = w_ref[...].astype(BF16)
    h_ref, idx_ref, wcol_ref, rank_ref, cnt_ref = route_out_refs
    sub = o_ref.shape[0] // ROUTE_SPLIT

    def body(dec):
        x_ref = x_refs[(1 if dec else 0) if n_x == 2 else 0]
        a_ref = a_refs[(1 if dec else 0) if n_a == 2 else 0]
        parts = [slice(part * sub, (part + 1) * sub) for part in range(ROUTE_SPLIT)]
        for rows in parts:
            y = jnp.dot(a_ref[rows, :].astype(BF16), w_scr[...], preferred_element_type=F32)
            o_ref[rows, :] = x_ref[rows, :] + mod_ref[0, 2:3, :] * y
        for part, rows in enumerate(parts):
            _route_rows(o_ref[rows, :], g_ref, mod_ref, wr_ref, bias_ref, tri_ref, h_ref.at[rows, :],
                        idx_ref.at[:, rows], wcol_ref.at[rows, :], rank_ref.at[:, rows], cnt_ref.at[part:part + 1],
                        ind_scr, wrow_scr)

    if n_x == 1 and n_a == 1:
        body(False)
    else:
        pl.when(jnp.logical_not(is_dec))(lambda: body(False))
        pl.when(is_dec)(lambda: body(True))


def proj_residual_route(lay, x_parts, a_parts, w, mods, ffn_g, w_router_t, router_bias):
    d = x_parts[0].shape[1]
    kdim = a_parts[0].shape[1]
    tm, t = lay.tm, lay.t
    sub = tm // ROUTE_SPLIT
    tri = jnp.triu(jnp.ones((sub, sub), BF16), k=1)
    pair = jax.ShapeDtypeStruct((TOP_K, t), I32)
    pair_spec = pl.BlockSpec((TOP_K, tm), lambda i: (0, i))
    const = lambda shape: pl.BlockSpec(shape, lambda i: (0,) * len(shape))
    return pl.pallas_call(
        functools.partial(_proj_route_kernel, n_x=len(x_parts), n_a=len(a_parts), n_ctx_tiles=lay.n_ctx_tiles),
        out_shape=[jax.ShapeDtypeStruct((t, d), F32), jax.ShapeDtypeStruct((t, d // 2), jnp.uint32), pair,
                   jax.ShapeDtypeStruct((t, LANES), F32), pair,
                   jax.ShapeDtypeStruct((lay.n_tiles * ROUTE_SPLIT, N_EXPERTS, LANES), F32)],
        grid=(lay.n_tiles,),
        in_specs=lay.row_specs(x_parts, d) + lay.row_specs(a_parts, kdim) + [
            pl.BlockSpec((kdim, d), lambda i: (0, 0), pipeline_mode=pl.Buffered(1)),
            pl.BlockSpec((1, N_MOD, d), lambda i: (lay.mod_row(i), 0, 0)),
            const((1, d)), const((N_EXPERTS, d)), const((N_EXPERTS, 1)), const((sub, sub)),
        ],
        out_specs=[pl.BlockSpec((tm, d), lambda i: (i, 0)), pl.BlockSpec((tm, d // 2), lambda i: (i, 0)), pair_spec,
                   pl.BlockSpec((tm, LANES), lambda i: (i, 0)), pair_spec,
                   pl.BlockSpec((ROUTE_SPLIT, N_EXPERTS, LANES), lambda i: (i, 0, 0))],
        scratch_shapes=[pltpu.VMEM((N_EXPERTS, sub), F32), pltpu.VMEM((LANES, sub), F32),
                        pltpu.VMEM((kdim, d), BF16)],
        compiler_params=_cparams("arbitrary"),
        name="proj_residual_route",
    )(*x_parts, *a_parts, w, mods, ffn_g.reshape(1, d), w_router_t, router_bias.reshape(N_EXPERTS, 1), tri)


def dispatch_plan(idx, rank, cnt, n_sorted):
    t = idx.shape[1]
    n_tiles, tm = cnt.shape[0], t // cnt.shape[0]
    cnt_tile = cnt[:, :, 0].astype(I32)
    total = jnp.sum(cnt_tile, axis=0)
    padded = ((total + EXPERT_TILE - 1) // EXPERT_TILE) * EXPERT_TILE
    end = jnp.cumsum(padded)
    start = end - padded
    base = start[None, :] + jnp.cumsum(cnt_tile, axis=0) - cnt_tile
    idx3 = idx.reshape(TOP_K, n_tiles, tm)
    pos = rank.reshape(TOP_K, n_tiles, tm)
    for e in range(N_EXPERTS):
        pos = pos + jnp.where(idx3 == e, base[None, :, e, None], 0)
    tile_row = jnp.arange(n_sorted // EXPERT_TILE, dtype=I32) * EXPERT_TILE
    tile_expert = jnp.minimum(jnp.sum(end[None, :] <= tile_row[:, None], axis=1), N_EXPERTS - 1).astype(I32)
    n_valid = (end[-1] // EXPERT_TILE).astype(I32).reshape(1)
    return pos.reshape(TOP_K * t).astype(I32), tile_expert, n_valid


def _sc_mesh():
    return plsc.VectorSubcoreMesh(core_axis_name="c", subcore_axis_name="s")


def _sc_worker_id():
    return lax.axis_index("s") * V7X_SC_CORES + lax.axis_index("c")


def _sc_two_stage_pipeline(n_chunks, fetch_idx, load, store):
    assert n_chunks % 2 == 0
    fetch_idx(0, 0)
    load(0, 0).start()

    @pl.loop(0, n_chunks // 2)
    def _(p):
        j = 2 * p

        @pl.when(p > 0)
        def _():
            store(j - 1, 1).wait()

        fetch_idx(j + 1, 1)
        load(j + 1, 1).start()
        load(j, 0).wait()
        store(j, 0).start()
        store(j, 0).wait()

        @pl.when(j + 2 < n_chunks)
        def _():
            fetch_idx(j + 2, 0)
            load(j + 2, 0).start()

        load(j + 1, 1).wait()
        store(j + 1, 1).start()

    store(n_chunks - 1, 1).wait()


_SC_SCRATCH = lambda chunk, d, dtype: [
    pltpu.VMEM((chunk,), I32), pltpu.VMEM((chunk,), I32),
    pltpu.VMEM((chunk, d), dtype), pltpu.VMEM((chunk, d), dtype),
    pltpu.SemaphoreType.DMA, pltpu.SemaphoreType.DMA, pltpu.SemaphoreType.DMA, pltpu.SemaphoreType.DMA]


def sc_scatter_rows(src, pos, n_out_rows):
    t, d = src.shape
    n_idx = pos.shape[0]
    per_worker = n_idx // SC_WORKERS
    chunk = SC_CHUNK_ROWS
    assert n_idx % (SC_WORKERS * chunk * 2) == 0 and t % chunk == 0

    @functools.partial(
        pl.kernel, mesh=_sc_mesh(), out_type=jax.ShapeDtypeStruct((n_out_rows, d), src.dtype),
        scratch_types=_SC_SCRATCH(chunk, d, src.dtype), name="sc_scatter_rows")
    def scatter(src_hbm, pos_hbm, out_hbm, idx_a, idx_b, rows_a, rows_b, lsem_a, lsem_b, ssem_a, ssem_b):
        base = _sc_worker_id() * per_worker
        idx, rows, lsem, ssem = (idx_a, idx_b), (rows_a, rows_b), (lsem_a, lsem_b), (ssem_a, ssem_b)

        def fetch_idx(j, b):
            pltpu.sync_copy(pos_hbm.at[pl.ds(base + j * chunk, chunk)], idx[b])

        def load(j, b):
            return pltpu.make_async_copy(src_hbm.at[pl.ds(lax.rem(base + j * chunk, t), chunk)], rows[b], lsem[b])

        def store(j, b):
            return pltpu.make_async_copy(rows[b], out_hbm.at[idx[b]], ssem[b])

        _sc_two_stage_pipeline(per_worker // chunk, fetch_idx, load, store)

    return scatter(src, pos)


def sc_gather_rows(table, idx):
    _, d = table.shape
    n_idx = idx.shape[0]
    per_worker = n_idx // SC_WORKERS
    chunk = SC_CHUNK_ROWS
    assert n_idx % (SC_WORKERS * chunk * 2) == 0

    @functools.partial(
        pl.kernel, mesh=_sc_mesh(), out_type=jax.ShapeDtypeStruct((n_idx, d), table.dtype),
        scratch_types=_SC_SCRATCH(chunk, d, table.dtype), name="sc_gather_rows")
    def gather(table_hbm, idx_hbm, out_hbm, idx_a, idx_b, rows_a, rows_b, lsem_a, lsem_b, ssem_a, ssem_b):
        base = _sc_worker_id() * per_worker
        idx, rows, lsem, ssem = (idx_a, idx_b), (rows_a, rows_b), (lsem_a, lsem_b), (ssem_a, ssem_b)

        def fetch_idx(j, b):
            pltpu.sync_copy(idx_hbm.at[pl.ds(base + j * chunk, chunk)], idx[b])

        def load(j, b):
            return pltpu.make_async_copy(table_hbm.at[idx[b]], rows[b], lsem[b])

        def store(j, b):
            return pltpu.make_async_copy(rows[b], out_hbm.at[pl.ds(base + j * chunk, chunk)], ssem[b])

        _sc_two_stage_pipeline(per_worker // chunk, fetch_idx, load, store)

    return gather(table, idx)


def _experts_kernel(te_ref, nv_ref, x_ref, wg_ref, wu_ref, wd_ref, y_ref, wg_scr, wu_scr, wd_scr):
    i = pl.program_id(0)

    @pl.when(i < nv_ref[0])
    def _():
        @pl.when((i == 0) | (te_ref[i] != te_ref[jnp.maximum(i - 1, 0)]))
        def _():
            wg_scr[...] = wg_ref[0, 0].astype(BF16)
            wu_scr[...] = wu_ref[0, 0].astype(BF16)
            wd_scr[...] = wd_ref[0, 0].astype(BF16)

        x_lo, x_hi = _unpack_bf16_halves(x_ref[...])
        half = x_lo.shape[1]

        def in_proj(w_scr):
            return (jnp.dot(x_lo, w_scr[0:half, :], preferred_element_type=F32)
                    + jnp.dot(x_hi, w_scr[half:, :], preferred_element_type=F32))

        a = jax.nn.silu(in_proj(wg_scr)) * in_proj(wu_scr)
        y = jnp.dot(a.astype(BF16), wd_scr[...], preferred_element_type=F32)
        y_ref[...] = _pack_bf16_halves(y.astype(BF16))


def grouped_experts(xs, tile_expert, n_valid, wg, wu, wd, layer):
    n_rows = xs.shape[0]
    d, de = wg.shape[-2:]
    tm = EXPERT_TILE
    row_map = lambda i, te, nv: (jnp.minimum(i, nv[0] - 1), 0)
    grid_spec = pltpu.PrefetchScalarGridSpec(
        num_scalar_prefetch=2,
        grid=(n_rows // tm,),
        in_specs=[
            pl.BlockSpec((tm, d // 2), row_map),
            pl.BlockSpec((1, 1, d, de), lambda i, te, nv: (layer, te[i], 0, 0)),
            pl.BlockSpec((1, 1, d, de), lambda i, te, nv: (layer, te[i], 0, 0)),
            pl.BlockSpec((1, 1, de, d), lambda i, te, nv: (layer, te[i], 0, 0)),
        ],
        out_specs=pl.BlockSpec((tm, d // 2), row_map),
        scratch_shapes=[pltpu.VMEM((d, de), BF16), pltpu.VMEM((d, de), BF16), pltpu.VMEM((de, d), BF16)],
    )
    return pl.pallas_call(
        _experts_kernel,
        out_shape=jax.ShapeDtypeStruct((n_rows, d // 2), jnp.uint32),
        grid_spec=grid_spec,
        compiler_params=_cparams("arbitrary"),
        name="grouped_experts",
    )(tile_expert, n_valid, xs, wg, wu, wd)


def _final_combine_kernel(x_ref, y_ref, w_ref, mod_ref, fg_ref, o_ctx_ref, o_dec_ref, *, n_ctx_tiles):
    out = _moe_combine(x_ref[...], y_ref, w_ref, mod_ref)
    ms = jnp.mean(out * out, axis=-1, keepdims=True)
    out = out * lax.rsqrt(ms + NORM_EPS) * fg_ref[...]

    def store(o_ref):
        o_ref[...] = out
    _by_part(pl.program_id(0) >= n_ctx_tiles, (o_ctx_ref, o_dec_ref), store)


def final_combine(lay, x, y_pair, w_col, mods, final_g):
    d = x.shape[1]
    tm = lay.tm
    return pl.pallas_call(
        functools.partial(_final_combine_kernel, n_ctx_tiles=lay.n_ctx_tiles),
        out_shape=[jax.ShapeDtypeStruct((lay.t_ctx, d), F32), jax.ShapeDtypeStruct((lay.t_dec, d), F32)],
        grid=(lay.n_tiles,),
        in_specs=[
            pl.BlockSpec((tm, d), lambda i: (i, 0)),
            pl.BlockSpec((TOP_K, tm, d // 2), lambda i: (0, i, 0)),
            pl.BlockSpec((tm, LANES), lambda i: (i, 0)),
            pl.BlockSpec((1, N_MOD, d), lambda i: (lay.mod_row(i), 0, 0)),
            pl.BlockSpec((1, d), lambda i: (0, 0)),
        ],
        out_specs=lay.row_specs([None, None], d),
        compiler_params=_cparams("arbitrary"),
        name="final_combine",
    )(x, y_pair, w_col, mods, final_g.reshape(1, d))


def group_moe(lay, x, routing, p, layer):
    t, d = x.shape
    h, idx, w_col, rank, cnt = routing
    n_sorted = TOP_K * t + N_EXPERTS * (EXPERT_TILE - 1)
    n_sorted = -(-n_sorted // EXPERT_TILE) * EXPERT_TILE
    pos, tile_expert, n_valid = dispatch_plan(idx, rank, cnt, n_sorted)
    xs = sc_scatter_rows(h, pos, n_sorted)
    ys = grouped_experts(xs, tile_expert, n_valid, p["moe_w_gate"], p["moe_w_up"], p["moe_w_down"], layer)
    return sc_gather_rows(ys, pos).reshape(TOP_K, t, d // 2), w_col


def _rope_angles(n, d):
    n_rows = n // GRID_W
    row = jnp.repeat(jnp.arange(n_rows), GRID_W).astype(F32)
    col = jnp.tile(jnp.arange(GRID_W), n_rows).astype(F32)
    nf = d // 4
    freqs = jnp.power(ROPE_BASE, -jnp.arange(nf, dtype=F32) / nf)
    ang = jnp.concatenate([row[:, None] * freqs, col[:, None] * freqs], axis=-1)
    return jnp.cos(ang), jnp.sin(ang)


def kernel(x_prompt, x_sample, cache_attn_k, cache_attn_v, state_ret_fwd, state_ret_bwd, c, c_ctx, w_ada, b_ada, norm_mix_g, norm_ffn_g, final_norm_g, da_w_qkv, da_lambda_q1, da_lambda_k1, da_lambda_q2, da_lambda_k2, da_subln_g, da_w_o, ret_w_qkv, ret_w_gate_fwd, ret_w_gate_bwd, ret_decay_fwd, ret_decay_bwd, ret_w_o, w_router, router_bias, moe_w_gate, moe_w_up, moe_w_down):
    b_ctx, n_ctx, d = x_prompt.shape
    b_dec, n_dec, _ = x_sample.shape
    past = cache_attn_k.shape[2]
    n_attn = cache_attn_k.shape[1]
    depth = w_ada.shape[0]
    assert b_dec + 1 <= MOD_ROWS
    lay = Layout(b_ctx, n_ctx, b_dec, n_dec)
    lay_proj = Layout(b_ctx, n_ctx, b_dec, n_dec, row_tile=2 * ROW_TILE)

    cond = jnp.zeros((MOD_ROWS, d), F32).at[0].set(c_ctx).at[1:1 + b_dec].set(c)
    mods_all = ada_modulation(cond, w_ada, b_ada)

    p = {
        "w_router_t": w_router.T.astype(BF16), "router_bias": router_bias.astype(F32),
        "moe_w_gate": moe_w_gate, "moe_w_up": moe_w_up, "moe_w_down": moe_w_down,
    }
    ret_log_decay = jnp.stack([jax.nn.log_sigmoid(ret_decay_fwd.astype(F32)),
                               jax.nn.log_sigmoid(ret_decay_bwd.astype(F32))], axis=1)
    ck_all = cache_attn_k.reshape(b_dec, n_attn, past, -1)
    cv_all = cache_attn_v.reshape(b_dec, n_attn, past, -1)

    x_parts = [x_prompt.reshape(lay.t_ctx, d), x_sample.reshape(lay.t_dec, d)]
    new_k, new_v, new_sf, new_sb = [], [], [], []
    pending_moe = None

    def first_kernel(*args):
        nonlocal x_parts, pending_moe
        outs = norm_mod_matmul(lay, x_parts, *args, pending_moe=pending_moe)
        if pending_moe is not None:
            *outs, x_joint = outs
            x_parts, pending_moe = [x_joint], None
        return outs

    for i in range(depth):
        mods = mods_all[i]
        j = i // 2
        if i % 2 == 0:
            lam_init = 0.8 - 0.6 * math.exp(-0.3 * i)
            qkw = DA_HEADS * 2 * DA_HEAD_DIM
            vw = DA_HEADS * DA_V_DIM
            cos, sin = _rope_angles(n_dec, DA_HEAD_DIM)
            rope = (jnp.tile(cos, (1, 4)), jnp.concatenate([-sin, sin, -sin, sin], axis=-1))
            q, k, v, k_ctx, v_ctx = first_kernel(
                norm_mix_g[i], mods, da_w_qkv[j],
                [(0, qkw, "rope64", math.log2(math.e) * DA_HEAD_DIM ** -0.5, BF16, "all"),
                 (qkw, qkw, "rope64", 1.0, BF16, "all"),
                 (2 * qkw, vw, "plain", 1.0, BF16, "all"),
                 (qkw, qkw, "plain", 1.0, F32, "ctx64"), (2 * qkw, vw, "plain", 1.0, F32, "ctx")],
                rope)
            lam_params = jnp.stack([da_lambda_q1[j], da_lambda_k1[j], da_lambda_q2[j], da_lambda_k2[j]])
            mix = [diff_attention(q, k, v, 0, b_ctx, n_ctx, lam_params, da_subln_g[j], lam_init,
                                  heads_per_step=DA_HEADS),
                   diff_attention(q, k, v, lay.t_ctx, b_dec, n_dec, lam_params, da_subln_g[j], lam_init,
                                  cache=(ck_all[:, j], cv_all[:, j]))]
            mix_w = da_w_o[j]
            new_k.append(k_ctx.reshape(b_ctx, n_ctx, DA_HEADS, 2, DA_HEAD_DIM))
            new_v.append(v_ctx.reshape(b_ctx, n_ctx, DA_HEADS, DA_V_DIM))
        else:
            kd = ret_w_qkv.shape[2] // 4
            dv = 2 * kd
            w_all = jnp.concatenate([ret_w_qkv[j], ret_w_gate_fwd[j], ret_w_gate_bwd[j]], axis=-1).astype(BF16)
            q, k, v, gates = first_kernel(
                norm_mix_g[i], mods, w_all,
                [(0, kd, "rope256", 1.0, BF16, "all"),
                 (kd, kd, "rope256", (kd // RET_HEADS) ** -0.5, F32, "all"),
                 (2 * kd, dv, "plain", 1.0, BF16, "all"), (2 * kd + dv, 2 * dv, "silu", 1.0, BF16, "all")],
                _rope_angles(n_dec, kd // RET_HEADS))
            ctx_seqs = math.gcd(b_ctx, max(1, n_dec // n_ctx // 2))
            o_ctx, sf, sb = retention(q, k, v, gates, 0, b_ctx, n_ctx, ret_log_decay[j], emit_state=True,
                                      seqs_per_step=ctx_seqs)
            (o_dec,) = retention(q, k, v, gates, lay.t_ctx, b_dec, n_dec, ret_log_decay[j],
                                 state=(state_ret_fwd[:, j:j + 1], state_ret_bwd[:, j:j + 1]))
            mix, mix_w = [o_ctx, o_dec], ret_w_o[j]
            new_sf.append(sf)
            new_sb.append(sb)
        x, *routing = proj_residual_route(lay_proj, x_parts, mix, mix_w, mods, norm_ffn_g[i], p["w_router_t"],
                                          p["router_bias"])
        y_pair, w_col = group_moe(lay, x, routing, p, i)
        if i == depth - 1:
            x_parts = final_combine(lay, x, y_pair, w_col, mods, final_norm_g)
        else:
            x_parts, pending_moe = [x], (y_pair, w_col, mods)

    y_ctx, y_dec = x_parts
    return (y_ctx.reshape(b_ctx, n_ctx, d), y_dec.reshape(b_dec, n_dec, d),
            jnp.stack(new_k, axis=1), jnp.stack(new_v, axis=1),
            jnp.concatenate(new_sf, axis=1), jnp.concatenate(new_sb, axis=1))
```

```python
import functools
import math

import jax
import jax.numpy as jnp
from jax import lax
from jax.experimental import pallas as pl
from jax.experimental.pallas import tpu as pltpu
from jax.experimental.pallas import tpu_sc as plsc

F32 = jnp.float32
BF16 = jnp.bfloat16
I32 = jnp.int32

GRID_W = 64
ROPE_BASE = 10000.0
NORM_EPS = 1e-6
DA_HEADS = 8
DA_HEAD_DIM = 64
DA_V_DIM = 2 * DA_HEAD_DIM
RET_HEADS = 4
RET_CHUNK = 128
N_EXPERTS = 16
N_GROUPS = 4
EXPERTS_PER_GROUP = N_EXPERTS // N_GROUPS
TOP_K = 2
N_MOD = 6
MOD_ROWS = 16
LANES = 128

V7X_VMEM_LIMIT = 56 * 1024 * 1024
V7X_SC_CORES = 2
V7X_SC_SUBCORES = 16
SC_WORKERS = V7X_SC_CORES * V7X_SC_SUBCORES
SC_CHUNK_ROWS = 64

ROW_TILE = 512
EXPERT_TILE = 512
ROUTE_SPLIT = 2


def _cparams(*sem):
    return pltpu.CompilerParams(dimension_semantics=sem, vmem_limit_bytes=V7X_VMEM_LIMIT)


def _nt_dot(a, b):
    return lax.dot_general(a, b, (((1,), (1,)), ((), ())), preferred_element_type=F32)


def _tn_dot(a, b):
    return lax.dot_general(a, b, (((0,), (0,)), ((), ())), preferred_element_type=F32)


class Layout:
    def __init__(self, b_ctx, n_ctx, b_dec, n_dec, row_tile=ROW_TILE):
        self.b_ctx, self.n_ctx, self.b_dec, self.n_dec = b_ctx, n_ctx, b_dec, n_dec
        self.t_ctx, self.t_dec = b_ctx * n_ctx, b_dec * n_dec
        self.t = self.t_ctx + self.t_dec
        self.tm = min(row_tile, n_dec, self.t_ctx)
        assert self.t_ctx % self.tm == 0 and n_dec % self.tm == 0
        assert self.t_ctx % n_dec == 0 and self.t_ctx % n_ctx == 0
        self.n_ctx_tiles = self.t_ctx // self.tm
        self.n_tiles = self.t // self.tm

    def mod_row(self, i):
        r = i * self.tm
        return jnp.where(r < self.t_ctx, 0, 1 + (r - self.t_ctx) // self.n_dec)

    def part_tile(self, part, i):
        if part == 0:
            return jnp.minimum(i, self.n_ctx_tiles - 1)
        return jnp.maximum(i - self.n_ctx_tiles, 0)

    def row_specs(self, arrays, width):
        if len(arrays) == 1:
            return [pl.BlockSpec((self.tm, width), lambda i, *_: (i, 0))]
        return [pl.BlockSpec((self.tm, width), lambda i, *_, p=p: (self.part_tile(p, i), 0)) for p in (0, 1)]


def _ada_kernel(c_ref, w_ref, b_ref, o_ref):
    s = jax.nn.silu(c_ref[...]).astype(BF16)
    acc = jnp.dot(s, w_ref[0].astype(BF16), preferred_element_type=F32)
    o_ref[0] = acc + b_ref[0]


def ada_modulation(cond, w_ada, b_ada):
    depth, d, n = w_ada.shape
    tn = 1536
    out = pl.pallas_call(
        _ada_kernel,
        out_shape=jax.ShapeDtypeStruct((depth, MOD_ROWS, n), F32),
        grid=(depth, n // tn),
        in_specs=[
            pl.BlockSpec((MOD_ROWS, d), lambda l, j: (0, 0)),
            pl.BlockSpec((1, d, tn), lambda l, j: (l, 0, j)),
            pl.BlockSpec((1, 1, tn), lambda l, j: (l, 0, j)),
        ],
        out_specs=pl.BlockSpec((1, MOD_ROWS, tn), lambda l, j: (l, 0, j)),
        compiler_params=_cparams("parallel", "parallel"),
        name="ada_modulation",
    )(cond, w_ada, b_ada.reshape(depth, 1, n))
    return out.reshape(depth, MOD_ROWS, N_MOD, d)


def _modulated_norm(x, g, mod_ref, shift_idx):
    ms = jnp.mean(x * x, axis=-1, keepdims=True)
    y = x * lax.rsqrt(ms + NORM_EPS) * g
    return y * (1.0 + mod_ref[0, shift_idx + 1:shift_idx + 2, :]) + mod_ref[0, shift_idx:shift_idx + 1, :]


def _by_part(is_dec, refs, fn):
    if len(refs) == 1:
        fn(refs[0])
        return
    pl.when(jnp.logical_not(is_dec))(lambda: fn(refs[0]))
    pl.when(is_dec)(lambda: fn(refs[1]))


def _rope64(a, cos, sin_signed, first_half):
    partner = jnp.where(first_half, pltpu.roll(a, 96, 1), pltpu.roll(a, 32, 1))
    return a * cos + partner * sin_signed


def _moe_combine(x, y_ref, w_ref, mod_ref):
    w = w_ref[...]

    def expert_out(slot):
        lo, hi = _unpack_bf16_halves(y_ref[slot])
        return jnp.concatenate([lo.astype(F32), hi.astype(F32)], axis=1)

    return x + mod_ref[0, 5:6, :] * (w[:, 0:1] * expert_out(0) + w[:, 1:2] * expert_out(1))


def _nmm_kernel(*refs, n_x, pending_moe, cast_w, outs, tn, n_ctx_tiles):
    x_refs = refs[:n_x]
    if pending_moe:
        y_ref, wcol_ref, prev_mod_ref = refs[n_x:n_x + 3]
        refs = refs[:n_x] + refs[n_x + 3:]
    g_ref, mod_ref, w_ref, cos_ref, sin_ref = refs[n_x:n_x + 5]
    n_out = len(outs) + (1 if pending_moe else 0)
    out_refs = refs[n_x + 5:n_x + 5 + n_out]
    if cast_w:
        w_f32_ref, w_ref = w_ref, refs[n_x + 5 + n_out]

        @pl.when(pl.program_id(0) == 0)
        def _():
            w_ref[...] = w_f32_ref[...].astype(BF16)
    is_dec = pl.program_id(0) >= n_ctx_tiles
    tm = x_refs[0].shape[0]
    segments = sorted({(col0, width) for col0, width, _, _, _ in outs})

    def rotated(a, kind, rows):
        cos, sin = cos_ref[rows, :], sin_ref[rows, :]
        if kind == "rope64":
            lane = lax.broadcasted_iota(I32, cos.shape, 1)
            first_half = (lane & 32) == 0
            return [(c * 128, _rope64(a[:, c * 128:(c + 1) * 128], cos, sin, first_half))
                    for c in range(tn // 128)]
        pieces = []
        for c in range(tn // 256):
            x1, x2 = a[:, c * 256:c * 256 + 128], a[:, c * 256 + 128:(c + 1) * 256]
            pieces += [(c * 256, x1 * cos - x2 * sin), (c * 256 + 128, x2 * cos + x1 * sin)]
        return pieces

    def emit(dec):
        x_ref = x_refs[(1 if dec else 0) if n_x == 2 else 0]
        for r0 in (0, tm // 2):
            rows = slice(r0, r0 + tm // 2)
            x = x_ref[rows, :]
            if pending_moe:
                x = _moe_combine(x, y_ref.at[:, rows, :], wcol_ref.at[rows, :], prev_mod_ref)
                out_refs[-1][rows, :] = x
            h = _modulated_norm(x, g_ref[...], mod_ref, 0).astype(BF16)
            for col0, width in segments:
                sinks = [(o_ref, o) for o_ref, o in zip(out_refs, outs)
                         if (o[0], o[1]) == (col0, width) and not (dec and o[4] != "all")]
                for blk in range(width // tn):
                    cols = slice(blk * tn, (blk + 1) * tn)
                    acc = jnp.dot(h, w_ref[:, col0 + blk * tn:col0 + (blk + 1) * tn], preferred_element_type=F32)
                    for o_ref, (_, _, kind, scale, out_rows) in sinks:
                        a = acc if scale == 1.0 else acc * scale
                        if out_rows == "ctx64":
                            n64 = width // 64
                            for c in range(tn // 64):
                                o_ref[pl.ds(r0 * n64 + blk * tn // 64 + c, tm // 2, stride=n64), :] = (
                                    a[:, c * 64:(c + 1) * 64].astype(o_ref.dtype))
                        elif kind == "silu":
                            o_ref[rows, cols] = jax.nn.silu(a).astype(o_ref.dtype)
                        elif kind == "plain" or out_rows == "ctx" or not dec:
                            o_ref[rows, cols] = a.astype(o_ref.dtype)
                        else:
                            for off, val in rotated(a, kind, rows):
                                o_ref[rows, blk * tn + off:blk * tn + off + 128] = val.astype(o_ref.dtype)

    pl.when(jnp.logical_not(is_dec))(lambda: emit(False))
    pl.when(is_dec)(lambda: emit(True))


def norm_mod_matmul(lay, x_parts, g, mods, w, outputs, rope_tables, pending_moe=None, tn=512):
    d = x_parts[0].shape[1]
    tm = lay.tm
    n_total = w.shape[1]
    outs, out_shapes, out_specs = [], [], []
    for col0, width, kind, scale, dtype, rows in outputs:
        assert width % tn == 0 and col0 % LANES == 0
        outs.append((col0, width, kind, float(scale), rows))
        if rows == "ctx64":
            n64 = width // 64
            out_shapes.append(jax.ShapeDtypeStruct((lay.t_ctx * n64, 64), dtype))
            out_specs.append(pl.BlockSpec((tm * n64, 64), lambda i: (lay.part_tile(0, i), 0)))
            continue
        n_rows = lay.t_ctx if rows == "ctx" else lay.t
        out_shapes.append(jax.ShapeDtypeStruct((n_rows, width), dtype))
        out_specs.append(lay.row_specs([None, None], width)[0] if rows == "ctx" else lay.row_specs([None], width)[0])
    blocks_per_seq = lay.n_dec // tm
    rope_spec = pl.BlockSpec((tm, LANES), lambda i: (lay.part_tile(1, i) % blocks_per_seq, 0))
    mod_spec = pl.BlockSpec((1, N_MOD, d), lambda i: (lay.mod_row(i), 0, 0))
    moe_specs, moe_args = [], []
    if pending_moe is not None:
        assert len(x_parts) == 1
        moe_specs = [pl.BlockSpec((TOP_K, tm, d // 2), lambda i: (0, i, 0)),
                     pl.BlockSpec((tm, LANES), lambda i: (i, 0)), mod_spec]
        moe_args = list(pending_moe)
        out_shapes.append(jax.ShapeDtypeStruct((lay.t, d), F32))
        out_specs.append(pl.BlockSpec((tm, d), lambda i: (i, 0)))
    return pl.pallas_call(
        functools.partial(_nmm_kernel, n_x=len(x_parts), pending_moe=pending_moe is not None,
                          cast_w=w.dtype != BF16, outs=tuple(outs), tn=tn, n_ctx_tiles=lay.n_ctx_tiles),
        out_shape=out_shapes,
        grid=(lay.n_tiles,),
        in_specs=lay.row_specs(x_parts, d) + moe_specs + [
            pl.BlockSpec((1, d), lambda i: (0, 0)),
            mod_spec,
            pl.BlockSpec((d, n_total), lambda i: (0, 0), pipeline_mode=pl.Buffered(1)),
            rope_spec, rope_spec,
        ],
        out_specs=out_specs,
        scratch_shapes=[pltpu.VMEM(w.shape, BF16)] if w.dtype != BF16 else [],
        compiler_params=_cparams("arbitrary"),
        name="norm_mod_matmul",
    )(*x_parts, *moe_args, g.reshape(1, d), mods, w, *rope_tables)


def _diff_lambda(lam_ref, lam_init):
    lp = lam_ref[...]
    return (jnp.exp(jnp.sum(lp[0:1] * lp[1:2], axis=-1, keepdims=True))
            - jnp.exp(jnp.sum(lp[2:3] * lp[3:4], axis=-1, keepdims=True)) + lam_init)


ONES_ROWS = 16


def _scores_t(k, q, comp):
    lane = lax.broadcasted_iota(I32, q.shape, 1)
    return _nt_dot(k, jnp.where((lane < DA_HEAD_DIM) == (comp == 0), q, jnp.zeros_like(q)))


def _softmax_values_t(scores, vts):
    dv = vts[0].shape[0] - ONES_ROWS
    maxes = [jnp.max(s, axis=0, keepdims=True) for s in scores]
    exps = [jnp.exp2(s - m).astype(BF16) for s, m in zip(scores, maxes)]
    accs = [jnp.dot(vt, e, preferred_element_type=F32) for vt, e in zip(vts, exps)]
    return [acc[0:dv] / acc[dv:dv + 1] for acc in accs]


def _diff_finish_t(parts, lam, g_col, lam_init):
    o = parts[0] - lam * parts[1]
    ms = jnp.mean(o * o, axis=0, keepdims=True)
    return jnp.transpose((o * lax.rsqrt(ms + NORM_EPS) * g_col) * (1.0 - lam_init))


def _diff_attn_kernel(lam_ref, q_ref, k_ref, v_ref, g_ref, o_ref, *, lam_init, heads_per_step):
    hw = 2 * DA_HEAD_DIM
    lam = _diff_lambda(lam_ref, lam_init)
    ones = jnp.ones((ONES_ROWS, k_ref.shape[0]), BF16)
    heads = [slice(hh * hw, (hh + 1) * hw) for hh in range(heads_per_step)]
    scores = [[_scores_t(k_ref[:, cols], q_ref[:, cols], comp) for comp in range(2)] for cols in heads]
    vts = [jnp.concatenate([jnp.transpose(v_ref[:, cols].astype(F32)).astype(BF16), ones], axis=0) for cols in heads]
    parts = _softmax_values_t([s for pair in scores for s in pair], [vt for vt in vts for _ in range(2)])
    for hh, cols in enumerate(heads):
        o_ref[:, cols] = _diff_finish_t(parts[2 * hh:2 * hh + 2], lam, g_ref[...], lam_init).astype(o_ref.dtype)


def _diff_attn_cached_kernel(lam_ref, q_ref, k_ref, v_ref, ck_ref, cv_ref, g_ref, o_ref, k_scr, vt_scr, s_scr, *,
                             lam_init, sub_rows):
    seq_len, hw = q_ref.shape
    n_sub = seq_len // sub_rows
    assert n_sub % 2 == 0
    lam = _diff_lambda(lam_ref, lam_init)
    k_scr[0:seq_len, :] = k_ref[...]
    k_scr[seq_len:, :] = ck_ref[0].astype(BF16)
    vt_scr[0:hw, 0:seq_len] = jnp.transpose(v_ref[...])
    vt_scr[0:hw, seq_len:] = jnp.transpose(cv_ref[0]).astype(BF16)
    vt_scr[hw:, :] = jnp.ones((ONES_ROWS, vt_scr.shape[1]), BF16)

    def rows_of(t):
        return pl.ds(pl.multiple_of(t * sub_rows, sub_rows), sub_rows)

    def scores(t, slot, comp):
        s_scr[slot, comp] = _scores_t(k_scr[...], q_ref[rows_of(t), :], comp)

    def stage(t_next, slot_next, t, slot):
        for comp in range(2):
            scores(t_next, slot_next, comp)
        parts = _softmax_values_t([s_scr[slot, 0], s_scr[slot, 1]], [vt_scr[...]] * 2)
        o_ref[rows_of(t), :] = _diff_finish_t(parts, lam, g_ref[...], lam_init).astype(o_ref.dtype)

    scores(0, 0, 0)
    scores(0, 0, 1)

    def body(i2, carry):
        t = 2 * i2
        stage(t + 1, 1, t, 0)
        stage(jnp.minimum(t + 2, n_sub - 1), 0, t + 1, 1)
        return carry

    lax.fori_loop(0, n_sub // 2, body, 0, unroll=2 if n_sub % 4 == 0 else 1)


def diff_attention(q, k, v, row0, batch, seq_len, lam_params, subln_g, lam_init, cache=None, heads_per_step=1,
                   sub_rows=256):
    width = q.shape[1]
    hw = 2 * DA_HEAD_DIM
    bw = heads_per_step * hw
    assert row0 % seq_len == 0 and DA_HEADS % heads_per_step == 0
    s0 = row0 // seq_len
    seq_spec = pl.BlockSpec((seq_len, bw), lambda b, h: (s0 + b, h))
    in_specs = [pl.BlockSpec((4, DA_HEAD_DIM), lambda b, h: (0, 0)), seq_spec, seq_spec, seq_spec]
    args = [lam_params, q, k, v]
    scratch = []
    if cache is None:
        body = functools.partial(_diff_attn_kernel, lam_init=lam_init, heads_per_step=heads_per_step)
    else:
        assert heads_per_step == 1
        past = cache[0].shape[1]
        cache_spec = pl.BlockSpec((1, past, hw), lambda b, h: (b, 0, h))
        in_specs += [cache_spec, cache_spec]
        args += list(cache)
        sub_rows = min(sub_rows, seq_len // 2)
        scratch = [pltpu.VMEM((seq_len + past, hw), BF16), pltpu.VMEM((hw + ONES_ROWS, seq_len + past), BF16),
                   pltpu.VMEM((2, 2, seq_len + past, sub_rows), F32)]
        body = functools.partial(_diff_attn_cached_kernel, lam_init=lam_init, sub_rows=sub_rows)
    in_specs.append(pl.BlockSpec((hw, 1), lambda b, h: (0, 0)))
    args.append(subln_g.reshape(hw, 1))
    return pl.pallas_call(
        body,
        out_shape=jax.ShapeDtypeStruct((batch * seq_len, width), BF16),
        grid=(batch, DA_HEADS // heads_per_step),
        in_specs=in_specs,
        out_specs=pl.BlockSpec((seq_len, bw), lambda b, h: (b, h)),
        scratch_shapes=scratch,
        compiler_params=_cparams("parallel", "parallel"),
        name="diff_attention",
    )(*args)


def _retention_kernel(*refs, seq_len, seqs, has_state, emit_state):
    lg_ref, q_ref, k_ref, v_ref, gf_ref, gb_ref = refs[:6]
    pos = 6
    if has_state:
        s0_refs = refs[6:8]
        pos = 8
    o_ref = refs[pos]
    pos += 1
    if emit_state:
        s_out_refs = refs[pos:pos + 2]
        pos += 2
    s_scr, o_scr = refs[pos:pos + 2]

    h = pl.program_id(1)
    c_len = RET_CHUNK
    n_chunks = seq_len // c_len
    assert n_chunks % 2 == 0
    row = lax.broadcasted_iota(I32, (c_len, c_len), 0)
    colm = lax.broadcasted_iota(I32, (c_len, c_len), 1)
    rel = (row - colm).astype(F32)
    posv = lax.broadcasted_iota(I32, (c_len, 1), 0).astype(F32)

    consts = []
    for backward in (False, True):
        lg = lg_ref[1 if backward else 0, h]
        if backward:
            intra = jnp.where(rel <= 0, jnp.exp(-rel * lg), 0.0)
            q_decay = jnp.exp((c_len - posv) * lg)
            k_decay = jnp.exp(posv * lg)
        else:
            intra = jnp.where(rel >= 0, jnp.exp(rel * lg), 0.0)
            q_decay = jnp.exp((posv + 1.0) * lg)
            k_decay = jnp.exp((c_len - 1.0 - posv) * lg)
        consts.append((intra, q_decay, k_decay, jnp.exp(jnp.zeros((1, 1), F32) + c_len * lg)))
        direction = 1 if backward else 0
        for sq in range(seqs):
            if has_state:
                s_scr[sq, direction] = s0_refs[direction][sq, 0, 0]
            else:
                s_scr[sq, direction] = jnp.zeros(s_scr.shape[2:], F32)

    def body(ci, first_touch):
        chains = [(sq, direction, pl.ds(pl.multiple_of(sq * seq_len + c * c_len, c_len), c_len))
                  for sq in range(seqs) for direction, c in ((0, ci), (1, n_chunks - 1 - ci))]
        qs = [q_ref[rows, :].astype(BF16) for _, _, rows in chains]
        ks = [k_ref[rows, :].astype(F32) for _, _, rows in chains]
        vs = [v_ref[rows, :].astype(BF16) for _, _, rows in chains]
        intras = [(_nt_dot(qb, kf.astype(BF16)) * consts[d][0]).astype(BF16)
                  for (_, d, _), qb, kf in zip(chains, qs, ks)]
        states = [s_scr[sq, d] for sq, d, _ in chains]
        outs = [jnp.dot(a, vb, preferred_element_type=F32)
                + jnp.dot(qb, s.astype(BF16), preferred_element_type=F32) * consts[d][1]
                for (_, d, _), a, qb, vb, s in zip(chains, intras, qs, vs, states)]
        for (sq, d, _), kf, vb, s in zip(chains, ks, vs, states):
            s_scr[sq, d] = consts[d][3] * s + _tn_dot((kf * consts[d][2]).astype(BF16), vb)
        centred = [o - jnp.mean(o, axis=-1, keepdims=True) for o in outs]
        scales = [lax.rsqrt(jnp.mean(oc * oc, axis=-1, keepdims=True) + NORM_EPS) for oc in centred]
        for (_, d, rows), oc, scale in zip(chains, centred, scales):
            gated = oc * scale * (gb_ref if d else gf_ref)[rows, :].astype(F32)
            if first_touch:
                o_scr[rows, :] = gated
            else:
                o_ref[rows, :] = (o_scr[rows, :] + gated).astype(o_ref.dtype)

    unroll = 8 if n_chunks % 16 == 0 else 1
    lax.fori_loop(0, n_chunks // 2, lambda ci, c: (body(ci, True), c)[1], 0, unroll=unroll)
    lax.fori_loop(n_chunks // 2, n_chunks, lambda ci, c: (body(ci, False), c)[1], 0, unroll=unroll)
    if emit_state:
        for d in range(2):
            for sq in range(seqs):
                s_out_refs[d][sq, 0, 0] = s_scr[sq, d]


def retention(q, k, v, gates, row0, batch, seq_len, log_decay, state=None, emit_state=False, seqs_per_step=1):
    dk = q.shape[1] // RET_HEADS
    dv = v.shape[1] // RET_HEADS
    seqs = seqs_per_step
    rows = seqs * seq_len
    assert row0 % rows == 0 and batch % seqs == 0
    s0 = row0 // rows
    in_specs = [
        pl.BlockSpec(memory_space=pltpu.SMEM),
        pl.BlockSpec((rows, dk), lambda b, h: (s0 + b, h)),
        pl.BlockSpec((rows, dk), lambda b, h: (s0 + b, h)),
        pl.BlockSpec((rows, dv), lambda b, h: (s0 + b, h)),
        pl.BlockSpec((rows, dv), lambda b, h: (s0 + b, h)),
        pl.BlockSpec((rows, dv), lambda b, h: (s0 + b, RET_HEADS + h)),
    ]
    args = [log_decay, q, k, v, gates, gates]
    state_spec = pl.BlockSpec((seqs, 1, 1, dk, dv), lambda b, h: (b, 0, h, 0, 0))
    if state is not None:
        in_specs += [state_spec, state_spec]
        args += list(state)
    out_shapes = [jax.ShapeDtypeStruct((batch * seq_len, RET_HEADS * dv), BF16)]
    out_specs = [pl.BlockSpec((rows, dv), lambda b, h: (b, h))]
    if emit_state:
        out_shapes += [jax.ShapeDtypeStruct((batch, 1, RET_HEADS, dk, dv), F32)] * 2
        out_specs += [state_spec, state_spec]
    return pl.pallas_call(
        functools.partial(_retention_kernel, seq_len=seq_len, seqs=seqs, has_state=state is not None,
                          emit_state=emit_state),
        out_shape=out_shapes,
        grid=(batch // seqs, RET_HEADS),
        in_specs=in_specs,
        out_specs=out_specs,
        scratch_shapes=[pltpu.VMEM((seqs, 2, dk, dv), F32), pltpu.VMEM((rows, dv), F32)],
        compiler_params=_cparams("parallel", "parallel"),
        name="retention",
    )(*args)


def _pack_bf16_halves(x):
    half = x.shape[1] // 2
    lo = lax.bitcast_convert_type(x[:, :half].astype(F32), jnp.uint32)
    hi = lax.bitcast_convert_type(x[:, half:].astype(F32), jnp.uint32)
    return (hi & jnp.uint32(0xFFFF0000)) | (lo >> 16)


def _unpack_bf16_halves(p):
    lo = lax.bitcast_convert_type(p << 16, F32).astype(BF16)
    hi = lax.bitcast_convert_type(p & jnp.uint32(0xFFFF0000), F32).astype(BF16)
    return lo, hi


def _first_max_onehot(vals):
    m = vals[0]
    for v in vals[1:]:
        m = jnp.maximum(m, v)
    onehot, taken = [], None
    for v in vals:
        hit = v == m
        if taken is None:
            onehot.append(hit)
            taken = hit
        else:
            onehot.append(hit & jnp.logical_not(taken))
            taken = taken | hit
    return m, onehot


def _pick(onehot, vals):
    out = vals[-1]
    for oh, v in zip(onehot[-2::-1], vals[-2::-1]):
        out = jnp.where(oh, v, out)
    return out


def _route_rows(x, g_ref, mod_ref, wr_ref, bias_ref, tri_ref, h_ref, idx_ref, wcol_ref, rank_ref, cnt_ref,
                ind_scr, wrow_scr):
    hb = _modulated_norm(x, g_ref[...], mod_ref, 3).astype(BF16)
    h_ref[...] = _pack_bf16_halves(hb)
    logits = _nt_dot(wr_ref[...], hb)
    s = jax.nn.sigmoid(logits)
    sel = s + bias_ref[...]
    neg_inf = jnp.full_like(sel[0:1], -jnp.inf)
    sel_rows = [sel[e:e + 1] for e in range(N_EXPERTS)]
    s_rows = [s[e:e + 1] for e in range(N_EXPERTS)]

    def top2(vals):
        m1, oh1 = _first_max_onehot(vals)
        rest = [jnp.where(o, neg_inf, v) for o, v in zip(oh1, vals)]
        m2, oh2 = _first_max_onehot(rest)
        return m1, m2, oh1, oh2

    grp_scores = []
    for g in range(N_GROUPS):
        m1, m2, _, _ = top2(sel_rows[g * EXPERTS_PER_GROUP:(g + 1) * EXPERTS_PER_GROUP])
        grp_scores.append(m1 + m2)
    _, in_grp = _first_max_onehot(grp_scores)
    cand_sel = [_pick(in_grp, [sel_rows[g * EXPERTS_PER_GROUP + k] for g in range(N_GROUPS)])
                for k in range(EXPERTS_PER_GROUP)]
    cand_s = [_pick(in_grp, [s_rows[g * EXPERTS_PER_GROUP + k] for g in range(N_GROUPS)])
              for k in range(EXPERTS_PER_GROUP)]
    _, _, oh1, oh2 = top2(cand_sel)
    w1 = _pick(oh1, cand_s)
    w2 = _pick(oh2, cand_s)
    denom = w1 + w2
    wrow_scr[...] = jnp.zeros_like(wrow_scr)
    wrow_scr[0:1, :] = w1 / denom
    wrow_scr[1:2, :] = w2 / denom
    wcol_ref[...] = jnp.transpose(wrow_scr[...])

    ints = [jnp.full(w1.shape, k, I32) for k in range(EXPERTS_PER_GROUP)]
    grp = _pick(in_grp, ints) * EXPERTS_PER_GROUP
    idx_ref[0:1, :] = grp + _pick(oh1, ints)
    idx_ref[1:2, :] = grp + _pick(oh2, ints)

    one, zero = jnp.ones_like(w1), jnp.zeros_like(w1)
    for g in range(N_GROUPS):
        for k in range(EXPERTS_PER_GROUP):
            e = g * EXPERTS_PER_GROUP + k
            ind_scr[e:e + 1, :] = jnp.where(in_grp[g] & (oh1[k] | oh2[k]), one, zero)
    ind = ind_scr[...].astype(BF16)
    ranks = jnp.dot(ind, tri_ref[...], preferred_element_type=F32)
    cnt_ref[0] = jnp.dot(ind, jnp.ones((ind.shape[1], LANES), BF16), preferred_element_type=F32)
    for slot, oh in ((0, oh1), (1, oh2)):
        r = zero
        for g in range(N_GROUPS):
            for k in range(EXPERTS_PER_GROUP):
                e = g * EXPERTS_PER_GROUP + k
                r = r + jnp.where(in_grp[g] & oh[k], ranks[e:e + 1], zero)
        rank_ref[slot:slot + 1, :] = r.astype(I32)


def _proj_route_kernel(*refs, n_x, n_a, n_ctx_tiles):
    x_refs, a_refs = refs[:n_x], refs[n_x:n_x + n_a]
    w_ref, mod_ref, g_ref, wr_ref, bias_ref, tri_ref, o_ref = refs[n_x + n_a:n_x + n_a + 7]
    route_out_refs = refs[n_x + n_a + 7:n_x + n_a + 12]
    ind_scr, wrow_scr, w_scr = refs[n_x + n_a + 12:]
    is_dec = pl.program_id(0) >= n_ctx_tiles

    @pl.when(pl.program_id(0) == 0)
    def _():
        w_scr[...] = w_ref[...].astype(BF16)
    h_ref, idx_ref, wcol_ref, rank_ref, cnt_ref = route_out_refs
    sub = o_ref.shape[0] // ROUTE_SPLIT

    def body(dec):
        x_ref = x_refs[(1 if dec else 0) if n_x == 2 else 0]
        a_ref = a_refs[(1 if dec else 0) if n_a == 2 else 0]
        parts = [slice(part * sub, (part + 1) * sub) for part in range(ROUTE_SPLIT)]
        for rows in parts:
            y = jnp.dot(a_ref[rows, :].astype(BF16), w_scr[...], preferred_element_type=F32)
            o_ref[rows, :] = x_ref[rows, :] + mod_ref[0, 2:3, :] * y
        for part, rows in enumerate(parts):
            _route_rows(o_ref[rows, :], g_ref, mod_ref, wr_ref, bias_ref, tri_ref, h_ref.at[rows, :],
                        idx_ref.at[:, rows], wcol_ref.at[rows, :], rank_ref.at[:, rows], cnt_ref.at[part:part + 1],
                        ind_scr, wrow_scr)

    if n_x == 1 and n_a == 1:
        body(False)
    else:
        pl.when(jnp.logical_not(is_dec))(lambda: body(False))
        pl.when(is_dec)(lambda: body(True))


def proj_residual_route(lay, x_parts, a_parts, w, mods, ffn_g, w_router_t, router_bias):
    d = x_parts[0].shape[1]
    kdim = a_parts[0].shape[1]
    tm, t = lay.tm, lay.t
    sub = tm // ROUTE_SPLIT
    tri = jnp.triu(jnp.ones((sub, sub), BF16), k=1)
    pair = jax.ShapeDtypeStruct((TOP_K, t), I32)
    pair_spec = pl.BlockSpec((TOP_K, tm), lambda i: (0, i))
    const = lambda shape: pl.BlockSpec(shape, lambda i: (0,) * len(shape))
    return pl.pallas_call(
        functools.partial(_proj_route_kernel, n_x=len(x_parts), n_a=len(a_parts), n_ctx_tiles=lay.n_ctx_tiles),
        out_shape=[jax.ShapeDtypeStruct((t, d), F32), jax.ShapeDtypeStruct((t, d // 2), jnp.uint32), pair,
                   jax.ShapeDtypeStruct((t, LANES), F32), pair,
                   jax.ShapeDtypeStruct((lay.n_tiles * ROUTE_SPLIT, N_EXPERTS, LANES), F32)],
        grid=(lay.n_tiles,),
        in_specs=lay.row_specs(x_parts, d) + lay.row_specs(a_parts, kdim) + [
            pl.BlockSpec((kdim, d), lambda i: (0, 0), pipeline_mode=pl.Buffered(1)),
            pl.BlockSpec((1, N_MOD, d), lambda i: (lay.mod_row(i), 0, 0)),
            const((1, d)), const((N_EXPERTS, d)), const((N_EXPERTS, 1)), const((sub, sub)),
        ],
        out_specs=[pl.BlockSpec((tm, d), lambda i: (i, 0)), pl.BlockSpec((tm, d // 2), lambda i: (i, 0)), pair_spec,
                   pl.BlockSpec((tm, LANES), lambda i: (i, 0)), pair_spec,
                   pl.BlockSpec((ROUTE_SPLIT, N_EXPERTS, LANES), lambda i: (i, 0, 0))],
        scratch_shapes=[pltpu.VMEM((N_EXPERTS, sub), F32), pltpu.VMEM((LANES, sub), F32),
                        pltpu.VMEM((kdim, d), BF16)],
        compiler_params=_cparams("arbitrary"),
        name="proj_residual_route",
    )(*x_parts, *a_parts, w, mods, ffn_g.reshape(1, d), w_router_t, router_bias.reshape(N_EXPERTS, 1), tri)


def dispatch_plan(idx, rank, cnt, n_sorted):
    t = idx.shape[1]
    n_tiles, tm = cnt.shape[0], t // cnt.shape[0]
    cnt_tile = cnt[:, :, 0].astype(I32)
    total = jnp.sum(cnt_tile, axis=0)
    padded = ((total + EXPERT_TILE - 1) // EXPERT_TILE) * EXPERT_TILE
    end = jnp.cumsum(padded)
    start = end - padded
    base = start[None, :] + jnp.cumsum(cnt_tile, axis=0) - cnt_tile
    idx3 = idx.reshape(TOP_K, n_tiles, tm)
    pos = rank.reshape(TOP_K, n_tiles, tm)
    for e in range(N_EXPERTS):
        pos = pos + jnp.where(idx3 == e, base[None, :, e, None], 0)
    tile_row = jnp.arange(n_sorted // EXPERT_TILE, dtype=I32) * EXPERT_TILE
    tile_expert = jnp.minimum(jnp.sum(end[None, :] <= tile_row[:, None], axis=1), N_EXPERTS - 1).astype(I32)
    n_valid = (end[-1] // EXPERT_TILE).astype(I32).reshape(1)
    return pos.reshape(TOP_K * t).astype(I32), tile_expert, n_valid


def _sc_mesh():
    return plsc.VectorSubcoreMesh(core_axis_name="c", subcore_axis_name="s")


def _sc_worker_id():
    return lax.axis_index("s") * V7X_SC_CORES + lax.axis_index("c")


def _sc_two_stage_pipeline(n_chunks, fetch_idx, load, store):
    assert n_chunks % 2 == 0
    fetch_idx(0, 0)
    load(0, 0).start()

    @pl.loop(0, n_chunks // 2)
    def _(p):
        j = 2 * p

        @pl.when(p > 0)
        def _():
            store(j - 1, 1).wait()

        fetch_idx(j + 1, 1)
        load(j + 1, 1).start()
        load(j, 0).wait()
        store(j, 0).start()
        store(j, 0).wait()

        @pl.when(j + 2 < n_chunks)
        def _():
            fetch_idx(j + 2, 0)
            load(j + 2, 0).start()

        load(j + 1, 1).wait()
        store(j + 1, 1).start()

    store(n_chunks - 1, 1).wait()


_SC_SCRATCH = lambda chunk, d, dtype: [
    pltpu.VMEM((chunk,), I32), pltpu.VMEM((chunk,), I32),
    pltpu.VMEM((chunk, d), dtype), pltpu.VMEM((chunk, d), dtype),
    pltpu.SemaphoreType.DMA, pltpu.SemaphoreType.DMA, pltpu.SemaphoreType.DMA, pltpu.SemaphoreType.DMA]


def sc_scatter_rows(src, pos, n_out_rows):
    t, d = src.shape
    n_idx = pos.shape[0]
    per_worker = n_idx // SC_WORKERS
    chunk = SC_CHUNK_ROWS
    assert n_idx % (SC_WORKERS * chunk * 2) == 0 and t % chunk == 0

    @functools.partial(
        pl.kernel, mesh=_sc_mesh(), out_type=jax.ShapeDtypeStruct((n_out_rows, d), src.dtype),
        scratch_types=_SC_SCRATCH(chunk, d, src.dtype), name="sc_scatter_rows")
    def scatter(src_hbm, pos_hbm, out_hbm, idx_a, idx_b, rows_a, rows_b, lsem_a, lsem_b, ssem_a, ssem_b):
        base = _sc_worker_id() * per_worker
        idx, rows, lsem, ssem = (idx_a, idx_b), (rows_a, rows_b), (lsem_a, lsem_b), (ssem_a, ssem_b)

        def fetch_idx(j, b):
            pltpu.sync_copy(pos_hbm.at[pl.ds(base + j * chunk, chunk)], idx[b])

        def load(j, b):
            return pltpu.make_async_copy(src_hbm.at[pl.ds(lax.rem(base + j * chunk, t), chunk)], rows[b], lsem[b])

        def store(j, b):
            return pltpu.make_async_copy(rows[b], out_hbm.at[idx[b]], ssem[b])

        _sc_two_stage_pipeline(per_worker // chunk, fetch_idx, load, store)

    return scatter(src, pos)


def sc_gather_rows(table, idx):
    _, d = table.shape
    n_idx = idx.shape[0]
    per_worker = n_idx // SC_WORKERS
    chunk = SC_CHUNK_ROWS
    assert n_idx % (SC_WORKERS * chunk * 2) == 0

    @functools.partial(
        pl.kernel, mesh=_sc_mesh(), out_type=jax.ShapeDtypeStruct((n_idx, d), table.dtype),
        scratch_types=_SC_SCRATCH(chunk, d, table.dtype), name="sc_gather_rows")
    def gather(table_hbm, idx_hbm, out_hbm, idx_a, idx_b, rows_a, rows_b, lsem_a, lsem_b, ssem_a, ssem_b):
        base = _sc_worker_id() * per_worker
        idx, rows, lsem, ssem = (idx_a, idx_b), (rows_a, rows_b), (lsem_a, lsem_b), (ssem_a, ssem_b)

        def fetch_idx(j, b):
            pltpu.sync_copy(idx_hbm.at[pl.ds(base + j * chunk, chunk)], idx[b])

        def load(j, b):
            return pltpu.make_async_copy(table_hbm.at[idx[b]], rows[b], lsem[b])

        def store(j, b):
            return pltpu.make_async_copy(rows[b], out_hbm.at[pl.ds(base + j * chunk, chunk)], ssem[b])

        _sc_two_stage_pipeline(per_worker // chunk, fetch_idx, load, store)

    return gather(table, idx)


def _experts_kernel(te_ref, nv_ref, x_ref, wg_ref, wu_ref, wd_ref, y_ref, wg_scr, wu_scr, wd_scr):
    i = pl.program_id(0)

    @pl.when(i < nv_ref[0])
    def _():
        @pl.when((i == 0) | (te_ref[i] != te_ref[jnp.maximum(i - 1, 0)]))
        def _():
            wg_scr[...] = wg_ref[0, 0].astype(BF16)
            wu_scr[...] = wu_ref[0, 0].astype(BF16)
            wd_scr[...] = wd_ref[0, 0].astype(BF16)

        x_lo, x_hi = _unpack_bf16_halves(x_ref[...])
        half = x_lo.shape[1]

        def in_proj(w_scr):
            return (jnp.dot(x_lo, w_scr[0:half, :], preferred_element_type=F32)
                    + jnp.dot(x_hi, w_scr[half:, :], preferred_element_type=F32))

        a = jax.nn.silu(in_proj(wg_scr)) * in_proj(wu_scr)
        y = jnp.dot(a.astype(BF16), wd_scr[...], preferred_element_type=F32)
        y_ref[...] = _pack_bf16_halves(y.astype(BF16))


def grouped_experts(xs, tile_expert, n_valid, wg, wu, wd, layer):
    n_rows = xs.shape[0]
    d, de = wg.shape[-2:]
    tm = EXPERT_TILE
    row_map = lambda i, te, nv: (jnp.minimum(i, nv[0] - 1), 0)
    grid_spec = pltpu.PrefetchScalarGridSpec(
        num_scalar_prefetch=2,
        grid=(n_rows // tm,),
        in_specs=[
            pl.BlockSpec((tm, d // 2), row_map),
            pl.BlockSpec((1, 1, d, de), lambda i, te, nv: (layer, te[i], 0, 0)),
            pl.BlockSpec((1, 1, d, de), lambda i, te, nv: (layer, te[i], 0, 0)),
            pl.BlockSpec((1, 1, de, d), lambda i, te, nv: (layer, te[i], 0, 0)),
        ],
        out_specs=pl.BlockSpec((tm, d // 2), row_map),
        scratch_shapes=[pltpu.VMEM((d, de), BF16), pltpu.VMEM((d, de), BF16), pltpu.VMEM((de, d), BF16)],
    )
    return pl.pallas_call(
        _experts_kernel,
        out_shape=jax.ShapeDtypeStruct((n_rows, d // 2), jnp.uint32),
        grid_spec=grid_spec,
        compiler_params=_cparams("arbitrary"),
        name="grouped_experts",
    )(tile_expert, n_valid, xs, wg, wu, wd)


def _final_combine_kernel(x_ref, y_ref, w_ref, mod_ref, fg_ref, o_ctx_ref, o_dec_ref, *, n_ctx_tiles):
    out = _moe_combine(x_ref[...], y_ref, w_ref, mod_ref)
    ms = jnp.mean(out * out, axis=-1, keepdims=True)
    out = out * lax.rsqrt(ms + NORM_EPS) * fg_ref[...]

    def store(o_ref):
        o_ref[...] = out
    _by_part(pl.program_id(0) >= n_ctx_tiles, (o_ctx_ref, o_dec_ref), store)


def final_combine(lay, x, y_pair, w_col, mods, final_g):
    d = x.shape[1]
    tm = lay.tm
    return pl.pallas_call(
        functools.partial(_final_combine_kernel, n_ctx_tiles=lay.n_ctx_tiles),
        out_shape=[jax.ShapeDtypeStruct((lay.t_ctx, d), F32), jax.ShapeDtypeStruct((lay.t_dec, d), F32)],
        grid=(lay.n_tiles,),
        in_specs=[
            pl.BlockSpec((tm, d), lambda i: (i, 0)),
            pl.BlockSpec((TOP_K, tm, d // 2), lambda i: (0, i, 0)),
            pl.BlockSpec((tm, LANES), lambda i: (i, 0)),
            pl.BlockSpec((1, N_MOD, d), lambda i: (lay.mod_row(i), 0, 0)),
            pl.BlockSpec((1, d), lambda i: (0, 0)),
        ],
        out_specs=lay.row_specs([None, None], d),
        compiler_params=_cparams("arbitrary"),
        name="final_combine",
    )(x, y_pair, w_col, mods, final_g.reshape(1, d))


def group_moe(lay, x, routing, p, layer):
    t, d = x.shape
    h, idx, w_col, rank, cnt = routing
    n_sorted = TOP_K * t + N_EXPERTS * (EXPERT_TILE - 1)
    n_sorted = -(-n_sorted // EXPERT_TILE) * EXPERT_TILE
    pos, tile_expert, n_valid = dispatch_plan(idx, rank, cnt, n_sorted)
    xs = sc_scatter_rows(h, pos, n_sorted)
    ys = grouped_experts(xs, tile_expert, n_valid, p["moe_w_gate"], p["moe_w_up"], p["moe_w_down"], layer)
    return sc_gather_rows(ys, pos).reshape(TOP_K, t, d // 2), w_col


def _rope_angles(n, d):
    n_rows = n // GRID_W
    row = jnp.repeat(jnp.arange(n_rows), GRID_W).astype(F32)
    col = jnp.tile(jnp.arange(GRID_W), n_rows).astype(F32)
    nf = d // 4
    freqs = jnp.power(ROPE_BASE, -jnp.arange(nf, dtype=F32) / nf)
    ang = jnp.concatenate([row[:, None] * freqs, col[:, None] * freqs], axis=-1)
    return jnp.cos(ang), jnp.sin(ang)


def kernel(x_prompt, x_sample, cache_attn_k, cache_attn_v, state_ret_fwd, state_ret_bwd, c, c_ctx, w_ada, b_ada, norm_mix_g, norm_ffn_g, final_norm_g, da_w_qkv, da_lambda_q1, da_lambda_k1, da_lambda_q2, da_lambda_k2, da_subln_g, da_w_o, ret_w_qkv, ret_w_gate_fwd, ret_w_gate_bwd, ret_decay_fwd, ret_decay_bwd, ret_w_o, w_router, router_bias, moe_w_gate, moe_w_up, moe_w_down):
    b_ctx, n_ctx, d = x_prompt.shape
    b_dec, n_dec, _ = x_sample.shape
    past = cache_attn_k.shape[2]
    n_attn = cache_attn_k.shape[1]
    depth = w_ada.shape[0]
    assert b_dec + 1 <= MOD_ROWS
    lay = Layout(b_ctx, n_ctx, b_dec, n_dec)
    lay_proj = Layout(b_ctx, n_ctx, b_dec, n_dec, row_tile=2 * ROW_TILE)

    cond = jnp.zeros((MOD_ROWS, d), F32).at[0].set(c_ctx).at[1:1 + b_dec].set(c)
    mods_all = ada_modulation(cond, w_ada, b_ada)

    p = {
        "w_router_t": w_router.T.astype(BF16), "router_bias": router_bias.astype(F32),
        "moe_w_gate": moe_w_gate, "moe_w_up": moe_w_up, "moe_w_down": moe_w_down,
    }
    ret_log_decay = jnp.stack([jax.nn.log_sigmoid(ret_decay_fwd.astype(F32)),
                               jax.nn.log_sigmoid(ret_decay_bwd.astype(F32))], axis=1)
    ck_all = cache_attn_k.reshape(b_dec, n_attn, past, -1)
    cv_all = cache_attn_v.reshape(b_dec, n_attn, past, -1)

    x_parts = [x_prompt.reshape(lay.t_ctx, d), x_sample.reshape(lay.t_dec, d)]
    new_k, new_v, new_sf, new_sb = [], [], [], []
    pending_moe = None

    def first_kernel(*args):
        nonlocal x_parts, pending_moe
        outs = norm_mod_matmul(lay, x_parts, *args, pending_moe=pending_moe)
        if pending_moe is not None:
            *outs, x_joint = outs
            x_parts, pending_moe = [x_joint], None
        return outs

    for i in range(depth):
        mods = mods_all[i]
        j = i // 2
        if i % 2 == 0:
            lam_init = 0.8 - 0.6 * math.exp(-0.3 * i)
            qkw = DA_HEADS * 2 * DA_HEAD_DIM
            vw = DA_HEADS * DA_V_DIM
            cos, sin = _rope_angles(n_dec, DA_HEAD_DIM)
            rope = (jnp.tile(cos, (1, 4)), jnp.concatenate([-sin, sin, -sin, sin], axis=-1))
            q, k, v, k_ctx, v_ctx = first_kernel(
                norm_mix_g[i], mods, da_w_qkv[j],
                [(0, qkw, "rope64", math.log2(math.e) * DA_HEAD_DIM ** -0.5, BF16, "all"),
                 (qkw, qkw, "rope64", 1.0, BF16, "all"),
                 (2 * qkw, vw, "plain", 1.0, BF16, "all"),
                 (qkw, qkw, "plain", 1.0, F32, "ctx64"), (2 * qkw, vw, "plain", 1.0, F32, "ctx")],
                rope)
            lam_params = jnp.stack([da_lambda_q1[j], da_lambda_k1[j], da_lambda_q2[j], da_lambda_k2[j]])
            mix = [diff_attention(q, k, v, 0, b_ctx, n_ctx, lam_params, da_subln_g[j], lam_init,
                                  heads_per_step=DA_HEADS),
                   diff_attention(q, k, v, lay.t_ctx, b_dec, n_dec, lam_params, da_subln_g[j], lam_init,
                                  cache=(ck_all[:, j], cv_all[:, j]))]
            mix_w = da_w_o[j]
            new_k.append(k_ctx.reshape(b_ctx, n_ctx, DA_HEADS, 2, DA_HEAD_DIM))
            new_v.append(v_ctx.reshape(b_ctx, n_ctx, DA_HEADS, DA_V_DIM))
        else:
            kd = ret_w_qkv.shape[2] // 4
            dv = 2 * kd
            w_all = jnp.concatenate([ret_w_qkv[j], ret_w_gate_fwd[j], ret_w_gate_bwd[j]], axis=-1).astype(BF16)
            q, k, v, gates = first_kernel(
                norm_mix_g[i], mods, w_all,
                [(0, kd, "rope256", 1.0, BF16, "all"),
                 (kd, kd, "rope256", (kd // RET_HEADS) ** -0.5, F32, "all"),
                 (2 * kd, dv, "plain", 1.0, BF16, "all"), (2 * kd + dv, 2 * dv, "silu", 1.0, BF16, "all")],
                _rope_angles(n_dec, kd // RET_HEADS))
            ctx_seqs = math.gcd(b_ctx, max(1, n_dec // n_ctx // 2))
            o_ctx, sf, sb = retention(q, k, v, gates, 0, b_ctx, n_ctx, ret_log_decay[j], emit_state=True,
                                      seqs_per_step=ctx_seqs)
            (o_dec,) = retention(q, k, v, gates, lay.t_ctx, b_dec, n_dec, ret_log_decay[j],
                                 state=(state_ret_fwd[:, j:j + 1], state_ret_bwd[:, j:j + 1]))
            mix, mix_w = [o_ctx, o_dec], ret_w_o[j]
            new_sf.append(sf)
            new_sb.append(sb)
        x, *routing = proj_residual_route(lay_proj, x_parts, mix, mix_w, mods, norm_ffn_g[i], p["w_router_t"],
                                          p["router_bias"])
        y_pair, w_col = group_moe(lay, x, routing, p, i)
        if i == depth - 1:
            x_parts = final_combine(lay_proj, x, y_pair, w_col, mods, final_norm_g)
        else:
            x_parts, pending_moe = [x], (y_pair, w_col, mods)

    y_ctx, y_dec = x_parts
    return (y_ctx.reshape(b_ctx, n_ctx, d), y_dec.reshape(b_dec, n_dec, d),
            jnp.stack(new_k, axis=1), jnp.stack(new_v, axis=1),
            jnp.concatenate(new_sf, axis=1), jnp.concatenate(new_sb, axis=1))
```

```python
import functools
import math

import jax
import jax.numpy as jnp
from jax import lax
from jax.experimental import pallas as pl
from jax.experimental.pallas import tpu as pltpu
from jax.experimental.pallas import tpu_sc as plsc

F32 = jnp.float32
BF16 = jnp.bfloat16
I32 = jnp.int32

GRID_W = 64
ROPE_BASE = 10000.0
NORM_EPS = 1e-6
DA_HEADS = 8
DA_HEAD_DIM = 64
DA_V_DIM = 2 * DA_HEAD_DIM
RET_HEADS = 4
RET_CHUNK = 128
N_EXPERTS = 16
N_GROUPS = 4
EXPERTS_PER_GROUP = N_EXPERTS // N_GROUPS
TOP_K = 2
N_MOD = 6
MOD_ROWS = 16
LANES = 128

V7X_VMEM_LIMIT = 56 * 1024 * 1024
V7X_SC_CORES = 2
V7X_SC_SUBCORES = 16
SC_WORKERS = V7X_SC_CORES * V7X_SC_SUBCORES
SC_CHUNK_ROWS = 64

ROW_TILE = 512
EXPERT_TILE = 512
ROUTE_SPLIT = 2


def _cparams(*sem):
    return pltpu.CompilerParams(dimension_semantics=sem, vmem_limit_bytes=V7X_VMEM_LIMIT)


def _nt_dot(a, b):
    return lax.dot_general(a, b, (((1,), (1,)), ((), ())), preferred_element_type=F32)


def _tn_dot(a, b):
    return lax.dot_general(a, b, (((0,), (0,)), ((), ())), preferred_element_type=F32)


class Layout:
    def __init__(self, b_ctx, n_ctx, b_dec, n_dec, row_tile=ROW_TILE):
        self.b_ctx, self.n_ctx, self.b_dec, self.n_dec = b_ctx, n_ctx, b_dec, n_dec
        self.t_ctx, self.t_dec = b_ctx * n_ctx, b_dec * n_dec
        self.t = self.t_ctx + self.t_dec
        self.tm = min(row_tile, n_dec, self.t_ctx)
        assert self.t_ctx % self.tm == 0 and n_dec % self.tm == 0
        assert self.t_ctx % n_dec == 0 and self.t_ctx % n_ctx == 0
        self.n_ctx_tiles = self.t_ctx // self.tm
        self.n_tiles = self.t // self.tm

    def mod_row(self, i):
        r = i * self.tm
        return jnp.where(r < self.t_ctx, 0, 1 + (r - self.t_ctx) // self.n_dec)

    def part_tile(self, part, i):
        if part == 0:
            return jnp.minimum(i, self.n_ctx_tiles - 1)
        return jnp.maximum(i - self.n_ctx_tiles, 0)

    def row_specs(self, arrays, width):
        if len(arrays) == 1:
            return [pl.BlockSpec((self.tm, width), lambda i, *_: (i, 0))]
        return [pl.BlockSpec((self.tm, width), lambda i, *_, p=p: (self.part_tile(p, i), 0)) for p in (0, 1)]


def _ada_kernel(c_ref, w_ref, b_ref, o_ref):
    s = jax.nn.silu(c_ref[...]).astype(BF16)
    acc = jnp.dot(s, w_ref[0].astype(BF16), preferred_element_type=F32)
    o_ref[0] = acc + b_ref[0]


def ada_modulation(cond, w_ada, b_ada):
    depth, d, n = w_ada.shape
    tn = 1536
    out = pl.pallas_call(
        _ada_kernel,
        out_shape=jax.ShapeDtypeStruct((depth, MOD_ROWS, n), F32),
        grid=(depth, n // tn),
        in_specs=[
            pl.BlockSpec((MOD_ROWS, d), lambda l, j: (0, 0)),
            pl.BlockSpec((1, d, tn), lambda l, j: (l, 0, j)),
            pl.BlockSpec((1, 1, tn), lambda l, j: (l, 0, j)),
        ],
        out_specs=pl.BlockSpec((1, MOD_ROWS, tn), lambda l, j: (l, 0, j)),
        compiler_params=_cparams("parallel", "parallel"),
        name="ada_modulation",
    )(cond, w_ada, b_ada.reshape(depth, 1, n))
    return out.reshape(depth, MOD_ROWS, N_MOD, d)


def _modulated_norm(x, g, mod_ref, shift_idx):
    ms = jnp.mean(x * x, axis=-1, keepdims=True)
    y = x * lax.rsqrt(ms + NORM_EPS) * g
    return y * (1.0 + mod_ref[0, shift_idx + 1:shift_idx + 2, :]) + mod_ref[0, shift_idx:shift_idx + 1, :]


def _by_part(is_dec, refs, fn):
    if len(refs) == 1:
        fn(refs[0])
        return
    pl.when(jnp.logical_not(is_dec))(lambda: fn(refs[0]))
    pl.when(is_dec)(lambda: fn(refs[1]))


def _rope64(a, cos, sin_signed, first_half):
    partner = jnp.where(first_half, pltpu.roll(a, 96, 1), pltpu.roll(a, 32, 1))
    return a * cos + partner * sin_signed


def _moe_combine(x, y_ref, w_ref, mod_ref):
    w = w_ref[...]

    def expert_out(slot):
        lo, hi = _unpack_bf16_halves(y_ref[slot])
        return jnp.concatenate([lo.astype(F32), hi.astype(F32)], axis=1)

    return x + mod_ref[0, 5:6, :] * (w[:, 0:1] * expert_out(0) + w[:, 1:2] * expert_out(1))


def _nmm_kernel(*refs, n_x, pending_moe, cast_w, outs, tn, n_ctx_tiles):
    x_refs = refs[:n_x]
    if pending_moe:
        y_ref, wcol_ref, prev_mod_ref = refs[n_x:n_x + 3]
        refs = refs[:n_x] + refs[n_x + 3:]
    g_ref, mod_ref, w_ref, cos_ref, sin_ref = refs[n_x:n_x + 5]
    n_out = len(outs) + (1 if pending_moe else 0)
    out_refs = refs[n_x + 5:n_x + 5 + n_out]
    if cast_w:
        w_f32_ref, w_ref = w_ref, refs[n_x + 5 + n_out]

        @pl.when(pl.program_id(0) == 0)
        def _():
            w_ref[...] = w_f32_ref[...].astype(BF16)
    is_dec = pl.program_id(0) >= n_ctx_tiles
    tm = x_refs[0].shape[0]
    segments = sorted({(col0, width) for col0, width, _, _, _ in outs})

    def rotated(a, kind, rows):
        cos, sin = cos_ref[rows, :], sin_ref[rows, :]
        if kind == "rope64":
            lane = lax.broadcasted_iota(I32, cos.shape, 1)
            first_half = (lane & 32) == 0
            return [(c * 128, _rope64(a[:, c * 128:(c + 1) * 128], cos, sin, first_half))
                    for c in range(tn // 128)]
        pieces = []
        for c in range(tn // 256):
            x1, x2 = a[:, c * 256:c * 256 + 128], a[:, c * 256 + 128:(c + 1) * 256]
            pieces += [(c * 256, x1 * cos - x2 * sin), (c * 256 + 128, x2 * cos + x1 * sin)]
        return pieces

    def emit(dec):
        x_ref = x_refs[(1 if dec else 0) if n_x == 2 else 0]
        for r0 in (0, tm // 2):
            rows = slice(r0, r0 + tm // 2)
            x = x_ref[rows, :]
            if pending_moe:
                x = _moe_combine(x, y_ref.at[:, rows, :], wcol_ref.at[rows, :], prev_mod_ref)
                out_refs[-1][rows, :] = x
            h = _modulated_norm(x, g_ref[...], mod_ref, 0).astype(BF16)
            for col0, width in segments:
                sinks = [(o_ref, o) for o_ref, o in zip(out_refs, outs)
                         if (o[0], o[1]) == (col0, width) and not (dec and o[4] != "all")]
                for blk in range(width // tn):
                    cols = slice(blk * tn, (blk + 1) * tn)
                    acc = jnp.dot(h, w_ref[:, col0 + blk * tn:col0 + (blk + 1) * tn], preferred_element_type=F32)
                    for o_ref, (_, _, kind, scale, out_rows) in sinks:
                        a = acc if scale == 1.0 else acc * scale
                        if out_rows == "ctx64":
                            n64 = width // 64
                            for c in range(tn // 64):
                                o_ref[pl.ds(r0 * n64 + blk * tn // 64 + c, tm // 2, stride=n64), :] = (
                                    a[:, c * 64:(c + 1) * 64].astype(o_ref.dtype))
                        elif kind == "silu":
                            o_ref[rows, cols] = jax.nn.silu(a).astype(o_ref.dtype)
                        elif kind == "plain" or out_rows == "ctx" or not dec:
                            o_ref[rows, cols] = a.astype(o_ref.dtype)
                        else:
                            for off, val in rotated(a, kind, rows):
                                o_ref[rows, blk * tn + off:blk * tn + off + 128] = val.astype(o_ref.dtype)

    pl.when(jnp.logical_not(is_dec))(lambda: emit(False))
    pl.when(is_dec)(lambda: emit(True))


def norm_mod_matmul(lay, x_parts, g, mods, w, outputs, rope_tables, pending_moe=None, tn=512):
    d = x_parts[0].shape[1]
    tm = lay.tm
    n_total = w.shape[1]
    outs, out_shapes, out_specs = [], [], []
    for col0, width, kind, scale, dtype, rows in outputs:
        assert width % tn == 0 and col0 % LANES == 0
        outs.append((col0, width, kind, float(scale), rows))
        if rows == "ctx64":
            n64 = width // 64
            out_shapes.append(jax.ShapeDtypeStruct((lay.t_ctx * n64, 64), dtype))
            out_specs.append(pl.BlockSpec((tm * n64, 64), lambda i: (lay.part_tile(0, i), 0)))
            continue
        n_rows = lay.t_ctx if rows == "ctx" else lay.t
        out_shapes.append(jax.ShapeDtypeStruct((n_rows, width), dtype))
        out_specs.append(lay.row_specs([None, None], width)[0] if rows == "ctx" else lay.row_specs([None], width)[0])
    blocks_per_seq = lay.n_dec // tm
    rope_spec = pl.BlockSpec((tm, LANES), lambda i: (lay.part_tile(1, i) % blocks_per_seq, 0))
    mod_spec = pl.BlockSpec((1, N_MOD, d), lambda i: (lay.mod_row(i), 0, 0))
    moe_specs, moe_args = [], []
    if pending_moe is not None:
        assert len(x_parts) == 1
        moe_specs = [pl.BlockSpec((TOP_K, tm, d // 2), lambda i: (0, i, 0)),
                     pl.BlockSpec((tm, LANES), lambda i: (i, 0)), mod_spec]
        moe_args = list(pending_moe)
        out_shapes.append(jax.ShapeDtypeStruct((lay.t, d), F32))
        out_specs.append(pl.BlockSpec((tm, d), lambda i: (i, 0)))
    return pl.pallas_call(
        functools.partial(_nmm_kernel, n_x=len(x_parts), pending_moe=pending_moe is not None,
                          cast_w=w.dtype != BF16, outs=tuple(outs), tn=tn, n_ctx_tiles=lay.n_ctx_tiles),
        out_shape=out_shapes,
        grid=(lay.n_tiles,),
        in_specs=lay.row_specs(x_parts, d) + moe_specs + [
            pl.BlockSpec((1, d), lambda i: (0, 0)),
            mod_spec,
            pl.BlockSpec((d, n_total), lambda i: (0, 0), pipeline_mode=pl.Buffered(1)),
            rope_spec, rope_spec,
        ],
        out_specs=out_specs,
        scratch_shapes=[pltpu.VMEM(w.shape, BF16)] if w.dtype != BF16 else [],
        compiler_params=_cparams("arbitrary"),
        name="norm_mod_matmul",
    )(*x_parts, *moe_args, g.reshape(1, d), mods, w, *rope_tables)


def _diff_lambda(lam_ref, lam_init):
    lp = lam_ref[...]
    return (jnp.exp(jnp.sum(lp[0:1] * lp[1:2], axis=-1, keepdims=True))
            - jnp.exp(jnp.sum(lp[2:3] * lp[3:4], axis=-1, keepdims=True)) + lam_init)


ONES_ROWS = 16
KEY_CHUNK = 512


def _scores_t(k, q, comp):
    lane = lax.broadcasted_iota(I32, q.shape, 1)
    return _nt_dot(k, jnp.where((lane < DA_HEAD_DIM) == (comp == 0), q, jnp.zeros_like(q)))


def _softmax_values_t(scores, vts):
    dv = vts[0].shape[0] - ONES_ROWS
    n_keys = scores[0].shape[0]
    chunk = min(KEY_CHUNK, n_keys)
    chunks = [slice(c, c + chunk) for c in range(0, n_keys, chunk)]
    maxes = []
    for s in scores:
        m = jnp.max(s[chunks[0], :], axis=0, keepdims=True)
        for c in chunks[1:]:
            m = jnp.maximum(m, jnp.max(s[c, :], axis=0, keepdims=True))
        maxes.append(m)
    accs = []
    for s, m, vt in zip(scores, maxes, vts):
        acc = None
        for c in chunks:
            part = jnp.dot(vt[:, c], jnp.exp2(s[c, :] - m).astype(BF16), preferred_element_type=F32)
            acc = part if acc is None else acc + part
        accs.append(acc)
    return [acc[0:dv] / acc[dv:dv + 1] for acc in accs]


def _diff_finish_t(parts, lam, g_col, lam_init):
    o = parts[0] - lam * parts[1]
    ms = jnp.mean(o * o, axis=0, keepdims=True)
    return jnp.transpose((o * lax.rsqrt(ms + NORM_EPS) * g_col) * (1.0 - lam_init))


def _diff_attn_kernel(lam_ref, q_ref, k_ref, v_ref, g_ref, o_ref, *, lam_init, heads_per_step):
    hw = 2 * DA_HEAD_DIM
    lam = _diff_lambda(lam_ref, lam_init)
    ones = jnp.ones((ONES_ROWS, k_ref.shape[0]), BF16)
    heads = [slice(hh * hw, (hh + 1) * hw) for hh in range(heads_per_step)]
    scores = [[_scores_t(k_ref[:, cols], q_ref[:, cols], comp) for comp in range(2)] for cols in heads]
    vts = [jnp.concatenate([jnp.transpose(v_ref[:, cols].astype(F32)).astype(BF16), ones], axis=0) for cols in heads]
    parts = _softmax_values_t([s for pair in scores for s in pair], [vt for vt in vts for _ in range(2)])
    for hh, cols in enumerate(heads):
        o_ref[:, cols] = _diff_finish_t(parts[2 * hh:2 * hh + 2], lam, g_ref[...], lam_init).astype(o_ref.dtype)


def _diff_attn_cached_kernel(lam_ref, q_ref, k_ref, v_ref, ck_ref, cv_ref, g_ref, o_ref, k_scr, vt_scr, s_scr, *,
                             lam_init, sub_rows):
    seq_len, hw = q_ref.shape
    n_sub = seq_len // sub_rows
    assert n_sub % 2 == 0
    lam = _diff_lambda(lam_ref, lam_init)
    k_scr[0:seq_len, :] = k_ref[...]
    k_scr[seq_len:, :] = ck_ref[0].astype(BF16)
    vt_scr[0:hw, 0:seq_len] = jnp.transpose(v_ref[...])
    vt_scr[0:hw, seq_len:] = jnp.transpose(cv_ref[0]).astype(BF16)
    vt_scr[hw:, :] = jnp.ones((ONES_ROWS, vt_scr.shape[1]), BF16)

    def rows_of(t):
        return pl.ds(pl.multiple_of(t * sub_rows, sub_rows), sub_rows)

    def scores(t, slot, comp):
        s_scr[slot, comp] = _scores_t(k_scr[...], q_ref[rows_of(t), :], comp)

    def stage(t_next, slot_next, t, slot):
        for comp in range(2):
            scores(t_next, slot_next, comp)
        parts = _softmax_values_t([s_scr.at[slot, 0], s_scr.at[slot, 1]], [vt_scr] * 2)
        o_ref[rows_of(t), :] = _diff_finish_t(parts, lam, g_ref[...], lam_init).astype(o_ref.dtype)

    scores(0, 0, 0)
    scores(0, 0, 1)

    def body(i2, carry):
        t = 2 * i2
        stage(t + 1, 1, t, 0)
        stage(jnp.minimum(t + 2, n_sub - 1), 0, t + 1, 1)
        return carry

    lax.fori_loop(0, n_sub // 2, body, 0, unroll=2 if n_sub % 4 == 0 else 1)


def diff_attention(q, k, v, row0, batch, seq_len, lam_params, subln_g, lam_init, cache=None, heads_per_step=1,
                   sub_rows=256):
    width = q.shape[1]
    hw = 2 * DA_HEAD_DIM
    bw = heads_per_step * hw
    assert row0 % seq_len == 0 and DA_HEADS % heads_per_step == 0
    s0 = row0 // seq_len
    seq_spec = pl.BlockSpec((seq_len, bw), lambda b, h: (s0 + b, h))
    in_specs = [pl.BlockSpec((4, DA_HEAD_DIM), lambda b, h: (0, 0)), seq_spec, seq_spec, seq_spec]
    args = [lam_params, q, k, v]
    scratch = []
    if cache is None:
        body = functools.partial(_diff_attn_kernel, lam_init=lam_init, heads_per_step=heads_per_step)
    else:
        assert heads_per_step == 1
        past = cache[0].shape[1]
        cache_spec = pl.BlockSpec((1, past, hw), lambda b, h: (b, 0, h))
        in_specs += [cache_spec, cache_spec]
        args += list(cache)
        sub_rows = min(sub_rows, seq_len // 2)
        scratch = [pltpu.VMEM((seq_len + past, hw), BF16), pltpu.VMEM((hw + ONES_ROWS, seq_len + past), BF16),
                   pltpu.VMEM((2, 2, seq_len + past, sub_rows), F32)]
        body = functools.partial(_diff_attn_cached_kernel, lam_init=lam_init, sub_rows=sub_rows)
    in_specs.append(pl.BlockSpec((hw, 1), lambda b, h: (0, 0)))
    args.append(subln_g.reshape(hw, 1))
    return pl.pallas_call(
        body,
        out_shape=jax.ShapeDtypeStruct((batch * seq_len, width), BF16),
        grid=(batch, DA_HEADS // heads_per_step),
        in_specs=in_specs,
        out_specs=pl.BlockSpec((seq_len, bw), lambda b, h: (b, h)),
        scratch_shapes=scratch,
        compiler_params=_cparams("parallel", "parallel"),
        name="diff_attention",
    )(*args)


def _retention_kernel(*refs, seq_len, seqs, has_state, emit_state):
    lg_ref, q_ref, k_ref, v_ref, gf_ref, gb_ref = refs[:6]
    pos = 6
    if has_state:
        s0_refs = refs[6:8]
        pos = 8
    o_ref = refs[pos]
    pos += 1
    if emit_state:
        s_out_refs = refs[pos:pos + 2]
        pos += 2
    s_scr, o_scr = refs[pos:pos + 2]

    h = pl.program_id(1)
    c_len = RET_CHUNK
    n_chunks = seq_len // c_len
    assert n_chunks % 2 == 0
    row = lax.broadcasted_iota(I32, (c_len, c_len), 0)
    colm = lax.broadcasted_iota(I32, (c_len, c_len), 1)
    rel = (row - colm).astype(F32)
    posv = lax.broadcasted_iota(I32, (c_len, 1), 0).astype(F32)

    consts = []
    for backward in (False, True):
        lg = lg_ref[1 if backward else 0, h]
        if backward:
            intra = jnp.where(rel <= 0, jnp.exp(-rel * lg), 0.0)
            q_decay = jnp.exp((c_len - posv) * lg)
            k_decay = jnp.exp(posv * lg)
        else:
            intra = jnp.where(rel >= 0, jnp.exp(rel * lg), 0.0)
            q_decay = jnp.exp((posv + 1.0) * lg)
            k_decay = jnp.exp((c_len - 1.0 - posv) * lg)
        consts.append((intra, q_decay, k_decay, jnp.exp(jnp.zeros((1, 1), F32) + c_len * lg)))
        direction = 1 if backward else 0
        for sq in range(seqs):
            if has_state:
                s_scr[sq, direction] = s0_refs[direction][sq, 0, 0]
            else:
                s_scr[sq, direction] = jnp.zeros(s_scr.shape[2:], F32)

    def body(ci, first_touch):
        chains = [(sq, direction, pl.ds(pl.multiple_of(sq * seq_len + c * c_len, c_len), c_len))
                  for sq in range(seqs) for direction, c in ((0, ci), (1, n_chunks - 1 - ci))]
        qs = [q_ref[rows, :].astype(BF16) for _, _, rows in chains]
        ks = [k_ref[rows, :].astype(F32) for _, _, rows in chains]
        vs = [v_ref[rows, :].astype(BF16) for _, _, rows in chains]
        intras = [(_nt_dot(qb, kf.astype(BF16)) * consts[d][0]).astype(BF16)
                  for (_, d, _), qb, kf in zip(chains, qs, ks)]
        states = [s_scr[sq, d] for sq, d, _ in chains]
        outs = [jnp.dot(a, vb, preferred_element_type=F32)
                + jnp.dot(qb, s.astype(BF16), preferred_element_type=F32) * consts[d][1]
                for (_, d, _), a, qb, vb, s in zip(chains, intras, qs, vs, states)]
        for (sq, d, _), kf, vb, s in zip(chains, ks, vs, states):
            s_scr[sq, d] = consts[d][3] * s + _tn_dot((kf * consts[d][2]).astype(BF16), vb)
        centred = [o - jnp.mean(o, axis=-1, keepdims=True) for o in outs]
        scales = [lax.rsqrt(jnp.mean(oc * oc, axis=-1, keepdims=True) + NORM_EPS) for oc in centred]
        for (_, d, rows), oc, scale in zip(chains, centred, scales):
            gated = oc * scale * (gb_ref if d else gf_ref)[rows, :].astype(F32)
            if first_touch:
                o_scr[rows, :] = gated
            else:
                o_ref[rows, :] = (o_scr[rows, :] + gated).astype(o_ref.dtype)

    unroll = 8 if n_chunks % 16 == 0 else 1
    lax.fori_loop(0, n_chunks // 2, lambda ci, c: (body(ci, True), c)[1], 0, unroll=unroll)
    lax.fori_loop(n_chunks // 2, n_chunks, lambda ci, c: (body(ci, False), c)[1], 0, unroll=unroll)
    if emit_state:
        for d in range(2):
            for sq in range(seqs):
                s_out_refs[d][sq, 0, 0] = s_scr[sq, d]


def retention(q, k, v, gates, row0, batch, seq_len, log_decay, state=None, emit_state=False, seqs_per_step=1):
    dk = q.shape[1] // RET_HEADS
    dv = v.shape[1] // RET_HEADS
    seqs = seqs_per_step
    rows = seqs * seq_len
    assert row0 % rows == 0 and batch % seqs == 0
    s0 = row0 // rows
    in_specs = [
        pl.BlockSpec(memory_space=pltpu.SMEM),
        pl.BlockSpec((rows, dk), lambda b, h: (s0 + b, h)),
        pl.BlockSpec((rows, dk), lambda b, h: (s0 + b, h)),
        pl.BlockSpec((rows, dv), lambda b, h: (s0 + b, h)),
        pl.BlockSpec((rows, dv), lambda b, h: (s0 + b, h)),
        pl.BlockSpec((rows, dv), lambda b, h: (s0 + b, RET_HEADS + h)),
    ]
    args = [log_decay, q, k, v, gates, gates]
    state_spec = pl.BlockSpec((seqs, 1, 1, dk, dv), lambda b, h: (b, 0, h, 0, 0))
    if state is not None:
        in_specs += [state_spec, state_spec]
        args += list(state)
    out_shapes = [jax.ShapeDtypeStruct((batch * seq_len, RET_HEADS * dv), BF16)]
    out_specs = [pl.BlockSpec((rows, dv), lambda b, h: (b, h))]
    if emit_state:
        out_shapes += [jax.ShapeDtypeStruct((batch, 1, RET_HEADS, dk, dv), F32)] * 2
        out_specs += [state_spec, state_spec]
    return pl.pallas_call(
        functools.partial(_retention_kernel, seq_len=seq_len, seqs=seqs, has_state=state is not None,
                          emit_state=emit_state),
        out_shape=out_shapes,
        grid=(batch // seqs, RET_HEADS),
        in_specs=in_specs,
        out_specs=out_specs,
        scratch_shapes=[pltpu.VMEM((seqs, 2, dk, dv), F32), pltpu.VMEM((rows, dv), F32)],
        compiler_params=_cparams("parallel", "parallel"),
        name="retention",
    )(*args)


def _pack_bf16_halves(x):
    half = x.shape[1] // 2
    lo = lax.bitcast_convert_type(x[:, :half].astype(F32), jnp.uint32)
    hi = lax.bitcast_convert_type(x[:, half:].astype(F32), jnp.uint32)
    return (hi & jnp.uint32(0xFFFF0000)) | (lo >> 16)


def _unpack_bf16_halves(p):
    lo = lax.bitcast_convert_type(p << 16, F32).astype(BF16)
    hi = lax.bitcast_convert_type(p & jnp.uint32(0xFFFF0000), F32).astype(BF16)
    return lo, hi


def _first_max_onehot(vals):
    m = vals[0]
    for v in vals[1:]:
        m = jnp.maximum(m, v)
    onehot, taken = [], None
    for v in vals:
        hit = v == m
        if taken is None:
            onehot.append(hit)
            taken = hit
        else:
            onehot.append(hit & jnp.logical_not(taken))
            taken = taken | hit
    return m, onehot


def _pick(onehot, vals):
    out = vals[-1]
    for oh, v in zip(onehot[-2::-1], vals[-2::-1]):
        out = jnp.where(oh, v, out)
    return out


def _route_rows(x, g_ref, mod_ref, wr_ref, bias_ref, tri_ref, h_ref, idx_ref, wcol_ref, rank_ref, cnt_ref,
                ind_scr, wrow_scr):
    hb = _modulated_norm(x, g_ref[...], mod_ref, 3).astype(BF16)
    h_ref[...] = _pack_bf16_halves(hb)
    logits = _nt_dot(wr_ref[...], hb)
    s = jax.nn.sigmoid(logits)
    sel = s + bias_ref[...]
    neg_inf = jnp.full_like(sel[0:1], -jnp.inf)
    sel_rows = [sel[e:e + 1] for e in range(N_EXPERTS)]
    s_rows = [s[e:e + 1] for e in range(N_EXPERTS)]

    def top2(vals):
        m1, oh1 = _first_max_onehot(vals)
        rest = [jnp.where(o, neg_inf, v) for o, v in zip(oh1, vals)]
        m2, oh2 = _first_max_onehot(rest)
        return m1, m2, oh1, oh2

    grp_scores = []
    for g in range(N_GROUPS):
        m1, m2, _, _ = top2(sel_rows[g * EXPERTS_PER_GROUP:(g + 1) * EXPERTS_PER_GROUP])
        grp_scores.append(m1 + m2)
    _, in_grp = _first_max_onehot(grp_scores)
    cand_sel = [_pick(in_grp, [sel_rows[g * EXPERTS_PER_GROUP + k] for g in range(N_GROUPS)])
                for k in range(EXPERTS_PER_GROUP)]
    cand_s = [_pick(in_grp, [s_rows[g * EXPERTS_PER_GROUP + k] for g in range(N_GROUPS)])
              for k in range(EXPERTS_PER_GROUP)]
    _, _, oh1, oh2 = top2(cand_sel)
    w1 = _pick(oh1, cand_s)
    w2 = _pick(oh2, cand_s)
    denom = w1 + w2
    wrow_scr[...] = jnp.zeros_like(wrow_scr)
    wrow_scr[0:1, :] = w1 / denom
    wrow_scr[1:2, :] = w2 / denom
    wcol_ref[...] = jnp.transpose(wrow_scr[...])

    ints = [jnp.full(w1.shape, k, I32) for k in range(EXPERTS_PER_GROUP)]
    grp = _pick(in_grp, ints) * EXPERTS_PER_GROUP
    idx_ref[0:1, :] = grp + _pick(oh1, ints)
    idx_ref[1:2, :] = grp + _pick(oh2, ints)

    one, zero = jnp.ones_like(w1), jnp.zeros_like(w1)
    for g in range(N_GROUPS):
        for k in range(EXPERTS_PER_GROUP):
            e = g * EXPERTS_PER_GROUP + k
            ind_scr[e:e + 1, :] = jnp.where(in_grp[g] & (oh1[k] | oh2[k]), one, zero)
    ind = ind_scr[...].astype(BF16)
    ranks = jnp.dot(ind, tri_ref[...], preferred_element_type=F32)
    cnt_ref[0] = jnp.dot(ind, jnp.ones((ind.shape[1], LANES), BF16), preferred_element_type=F32)
    for slot, oh in ((0, oh1), (1, oh2)):
        r = zero
        for g in range(N_GROUPS):
            for k in range(EXPERTS_PER_GROUP):
                e = g * EXPERTS_PER_GROUP + k
                r = r + jnp.where(in_grp[g] & oh[k], ranks[e:e + 1], zero)
        rank_ref[slot:slot + 1, :] = r.astype(I32)


def _proj_route_kernel(*refs, n_x, n_a, n_ctx_tiles):
    x_refs, a_refs = refs[:n_x], refs[n_x:n_x + n_a]
    w_ref, mod_ref, g_ref, wr_ref, bias_ref, tri_ref, o_ref = refs[n_x + n_a:n_x + n_a + 7]
    route_out_refs = refs[n_x + n_a + 7:n_x + n_a + 12]
    ind_scr, wrow_scr, w_scr = refs[n_x + n_a + 12:]
    is_dec = pl.program_id(0) >= n_ctx_tiles

    @pl.when(pl.program_id(0) == 0)
    def _():
        w_scr[...] = w_ref[...].astype(BF16)
    h_ref, idx_ref, wcol_ref, rank_ref, cnt_ref = route_out_refs
    sub = o_ref.shape[0] // ROUTE_SPLIT

    def body(dec):
        x_ref = x_refs[(1 if dec else 0) if n_x == 2 else 0]
        a_ref = a_refs[(1 if dec else 0) if n_a == 2 else 0]
        parts = [slice(part * sub, (part + 1) * sub) for part in range(ROUTE_SPLIT)]
        for rows in parts:
            y = jnp.dot(a_ref[rows, :].astype(BF16), w_scr[...], preferred_element_type=F32)
            o_ref[rows, :] = x_ref[rows, :] + mod_ref[0, 2:3, :] * y
        for part, rows in enumerate(parts):
            _route_rows(o_ref[rows, :], g_ref, mod_ref, wr_ref, bias_ref, tri_ref, h_ref.at[rows, :],
                        idx_ref.at[:, rows], wcol_ref.at[rows, :], rank_ref.at[:, rows], cnt_ref.at[part:part + 1],
                        ind_scr, wrow_scr)

    if n_x == 1 and n_a == 1:
        body(False)
    else:
        pl.when(jnp.logical_not(is_dec))(lambda: body(False))
        pl.when(is_dec)(lambda: body(True))


def proj_residual_route(lay, x_parts, a_parts, w, mods, ffn_g, w_router_t, router_bias):
    d = x_parts[0].shape[1]
    kdim = a_parts[0].shape[1]
    tm, t = lay.tm, lay.t
    sub = tm // ROUTE_SPLIT
    tri = jnp.triu(jnp.ones((sub, sub), BF16), k=1)
    pair = jax.ShapeDtypeStruct((TOP_K, t), I32)
    pair_spec = pl.BlockSpec((TOP_K, tm), lambda i: (0, i))
    const = lambda shape: pl.BlockSpec(shape, lambda i: (0,) * len(shape))
    return pl.pallas_call(
        functools.partial(_proj_route_kernel, n_x=len(x_parts), n_a=len(a_parts), n_ctx_tiles=lay.n_ctx_tiles),
        out_shape=[jax.ShapeDtypeStruct((t, d), F32), jax.ShapeDtypeStruct((t, d // 2), jnp.uint32), pair,
                   jax.ShapeDtypeStruct((t, LANES), F32), pair,
                   jax.ShapeDtypeStruct((lay.n_tiles * ROUTE_SPLIT, N_EXPERTS, LANES), F32)],
        grid=(lay.n_tiles,),
        in_specs=lay.row_specs(x_parts, d) + lay.row_specs(a_parts, kdim) + [
            pl.BlockSpec((kdim, d), lambda i: (0, 0), pipeline_mode=pl.Buffered(1)),
            pl.BlockSpec((1, N_MOD, d), lambda i: (lay.mod_row(i), 0, 0)),
            const((1, d)), const((N_EXPERTS, d)), const((N_EXPERTS, 1)), const((sub, sub)),
        ],
        out_specs=[pl.BlockSpec((tm, d), lambda i: (i, 0)), pl.BlockSpec((tm, d // 2), lambda i: (i, 0)), pair_spec,
                   pl.BlockSpec((tm, LANES), lambda i: (i, 0)), pair_spec,
                   pl.BlockSpec((ROUTE_SPLIT, N_EXPERTS, LANES), lambda i: (i, 0, 0))],
        scratch_shapes=[pltpu.VMEM((N_EXPERTS, sub), F32), pltpu.VMEM((LANES, sub), F32),
                        pltpu.VMEM((kdim, d), BF16)],
        compiler_params=_cparams("arbitrary"),
        name="proj_residual_route",
    )(*x_parts, *a_parts, w, mods, ffn_g.reshape(1, d), w_router_t, router_bias.reshape(N_EXPERTS, 1), tri)


def dispatch_plan(idx, rank, cnt, n_sorted):
    t = idx.shape[1]
    n_tiles, tm = cnt.shape[0], t // cnt.shape[0]
    cnt_tile = cnt[:, :, 0].astype(I32)
    total = jnp.sum(cnt_tile, axis=0)
    padded = ((total + EXPERT_TILE - 1) // EXPERT_TILE) * EXPERT_TILE
    end = jnp.cumsum(padded)
    start = end - padded
    base = start[None, :] + jnp.cumsum(cnt_tile, axis=0) - cnt_tile
    idx3 = idx.reshape(TOP_K, n_tiles, tm)
    pos = rank.reshape(TOP_K, n_tiles, tm)
    for e in range(N_EXPERTS):
        pos = pos + jnp.where(idx3 == e, base[None, :, e, None], 0)
    tile_row = jnp.arange(n_sorted // EXPERT_TILE, dtype=I32) * EXPERT_TILE
    tile_expert = jnp.minimum(jnp.sum(end[None, :] <= tile_row[:, None], axis=1), N_EXPERTS - 1).astype(I32)
    n_valid = (end[-1] // EXPERT_TILE).astype(I32).reshape(1)
    return pos.reshape(TOP_K * t).astype(I32), tile_expert, n_valid


def _sc_mesh():
    return plsc.VectorSubcoreMesh(core_axis_name="c", subcore_axis_name="s")


def _sc_worker_id():
    return lax.axis_index("s") * V7X_SC_CORES + lax.axis_index("c")


def _sc_two_stage_pipeline(n_chunks, fetch_idx, load, store):
    assert n_chunks % 2 == 0
    fetch_idx(0, 0)
    load(0, 0).start()

    @pl.loop(0, n_chunks // 2)
    def _(p):
        j = 2 * p

        @pl.when(p > 0)
        def _():
            store(j - 1, 1).wait()

        fetch_idx(j + 1, 1)
        load(j + 1, 1).start()
        load(j, 0).wait()
        store(j, 0).start()
        store(j, 0).wait()

        @pl.when(j + 2 < n_chunks)
        def _():
            fetch_idx(j + 2, 0)
            load(j + 2, 0).start()

        load(j + 1, 1).wait()
        store(j + 1, 1).start()

    store(n_chunks - 1, 1).wait()


_SC_SCRATCH = lambda chunk, d, dtype: [
    pltpu.VMEM((chunk,), I32), pltpu.VMEM((chunk,), I32),
    pltpu.VMEM((chunk, d), dtype), pltpu.VMEM((chunk, d), dtype),
    pltpu.SemaphoreType.DMA, pltpu.SemaphoreType.DMA, pltpu.SemaphoreType.DMA, pltpu.SemaphoreType.DMA]


def sc_scatter_rows(src, pos, n_out_rows):
    t, d = src.shape
    n_idx = pos.shape[0]
    per_worker = n_idx // SC_WORKERS
    chunk = SC_CHUNK_ROWS
    assert n_idx % (SC_WORKERS * chunk * 2) == 0 and t % chunk == 0

    @functools.partial(
        pl.kernel, mesh=_sc_mesh(), out_type=jax.ShapeDtypeStruct((n_out_rows, d), src.dtype),
        scratch_types=_SC_SCRATCH(chunk, d, src.dtype), name="sc_scatter_rows")
    def scatter(src_hbm, pos_hbm, out_hbm, idx_a, idx_b, rows_a, rows_b, lsem_a, lsem_b, ssem_a, ssem_b):
        base = _sc_worker_id() * per_worker
        idx, rows, lsem, ssem = (idx_a, idx_b), (rows_a, rows_b), (lsem_a, lsem_b), (ssem_a, ssem_b)

        def fetch_idx(j, b):
            pltpu.sync_copy(pos_hbm.at[pl.ds(base + j * chunk, chunk)], idx[b])

        def load(j, b):
            return pltpu.make_async_copy(src_hbm.at[pl.ds(lax.rem(base + j * chunk, t), chunk)], rows[b], lsem[b])

        def store(j, b):
            return pltpu.make_async_copy(rows[b], out_hbm.at[idx[b]], ssem[b])

        _sc_two_stage_pipeline(per_worker // chunk, fetch_idx, load, store)

    return scatter(src, pos)


def sc_gather_rows(table, idx):
    _, d = table.shape
    n_idx = idx.shape[0]
    per_worker = n_idx // SC_WORKERS
    chunk = SC_CHUNK_ROWS
    assert n_idx % (SC_WORKERS * chunk * 2) == 0

    @functools.partial(
        pl.kernel, mesh=_sc_mesh(), out_type=jax.ShapeDtypeStruct((n_idx, d), table.dtype),
        scratch_types=_SC_SCRATCH(chunk, d, table.dtype), name="sc_gather_rows")
    def gather(table_hbm, idx_hbm, out_hbm, idx_a, idx_b, rows_a, rows_b, lsem_a, lsem_b, ssem_a, ssem_b):
        base = _sc_worker_id() * per_worker
        idx, rows, lsem, ssem = (idx_a, idx_b), (rows_a, rows_b), (lsem_a, lsem_b), (ssem_a, ssem_b)

        def fetch_idx(j, b):
            pltpu.sync_copy(idx_hbm.at[pl.ds(base + j * chunk, chunk)], idx[b])

        def load(j, b):
            return pltpu.make_async_copy(table_hbm.at[idx[b]], rows[b], lsem[b])

        def store(j, b):
            return pltpu.make_async_copy(rows[b], out_hbm.at[pl.ds(base + j * chunk, chunk)], ssem[b])

        _sc_two_stage_pipeline(per_worker // chunk, fetch_idx, load, store)

    return gather(table, idx)


def _experts_kernel(te_ref, nv_ref, x_ref, wg_ref, wu_ref, wd_ref, y_ref, wg_scr, wu_scr, wd_scr):
    i = pl.program_id(0)

    @pl.when(i < nv_ref[0])
    def _():
        @pl.when((i == 0) | (te_ref[i] != te_ref[jnp.maximum(i - 1, 0)]))
        def _():
            wg_scr[...] = wg_ref[0, 0].astype(BF16)
            wu_scr[...] = wu_ref[0, 0].astype(BF16)
            wd_scr[...] = wd_ref[0, 0].astype(BF16)

        x_lo, x_hi = _unpack_bf16_halves(x_ref[...])
        half = x_lo.shape[1]

        def in_proj(w_scr):
            return (jnp.dot(x_lo, w_scr[0:half, :], preferred_element_type=F32)
                    + jnp.dot(x_hi, w_scr[half:, :], preferred_element_type=F32))

        a = jax.nn.silu(in_proj(wg_scr)) * in_proj(wu_scr)
        y = jnp.dot(a.astype(BF16), wd_scr[...], preferred_element_type=F32)
        y_ref[...] = _pack_bf16_halves(y.astype(BF16))


def grouped_experts(xs, tile_expert, n_valid, wg, wu, wd, layer):
    n_rows = xs.shape[0]
    d, de = wg.shape[-2:]
    tm = EXPERT_TILE
    row_map = lambda i, te, nv: (jnp.minimum(i, nv[0] - 1), 0)
    grid_spec = pltpu.PrefetchScalarGridSpec(
        num_scalar_prefetch=2,
        grid=(n_rows // tm,),
        in_specs=[
            pl.BlockSpec((tm, d // 2), row_map),
            pl.BlockSpec((1, 1, d, de), lambda i, te, nv: (layer, te[i], 0, 0)),
            pl.BlockSpec((1, 1, d, de), lambda i, te, nv: (layer, te[i], 0, 0)),
            pl.BlockSpec((1, 1, de, d), lambda i, te, nv: (layer, te[i], 0, 0)),
        ],
        out_specs=pl.BlockSpec((tm, d // 2), row_map),
        scratch_shapes=[pltpu.VMEM((d, de), BF16), pltpu.VMEM((d, de), BF16), pltpu.VMEM((de, d), BF16)],
    )
    return pl.pallas_call(
        _experts_kernel,
        out_shape=jax.ShapeDtypeStruct((n_rows, d // 2), jnp.uint32),
        grid_spec=grid_spec,
        compiler_params=_cparams("arbitrary"),
        name="grouped_experts",
    )(tile_expert, n_valid, xs, wg, wu, wd)


def _final_combine_kernel(x_ref, y_ref, w_ref, mod_ref, fg_ref, o_ctx_ref, o_dec_ref, *, n_ctx_tiles):
    out = _moe_combine(x_ref[...], y_ref, w_ref, mod_ref)
    ms = jnp.mean(out * out, axis=-1, keepdims=True)
    out = out * lax.rsqrt(ms + NORM_EPS) * fg_ref[...]

    def store(o_ref):
        o_ref[...] = out
    _by_part(pl.program_id(0) >= n_ctx_tiles, (o_ctx_ref, o_dec_ref), store)


def final_combine(lay, x, y_pair, w_col, mods, final_g):
    d = x.shape[1]
    tm = lay.tm
    return pl.pallas_call(
        functools.partial(_final_combine_kernel, n_ctx_tiles=lay.n_ctx_tiles),
        out_shape=[jax.ShapeDtypeStruct((lay.t_ctx, d), F32), jax.ShapeDtypeStruct((lay.t_dec, d), F32)],
        grid=(lay.n_tiles,),
        in_specs=[
            pl.BlockSpec((tm, d), lambda i: (i, 0)),
            pl.BlockSpec((TOP_K, tm, d // 2), lambda i: (0, i, 0)),
            pl.BlockSpec((tm, LANES), lambda i: (i, 0)),
            pl.BlockSpec((1, N_MOD, d), lambda i: (lay.mod_row(i), 0, 0)),
            pl.BlockSpec((1, d), lambda i: (0, 0)),
        ],
        out_specs=lay.row_specs([None, None], d),
        compiler_params=_cparams("arbitrary"),
        name="final_combine",
    )(x, y_pair, w_col, mods, final_g.reshape(1, d))


def group_moe(lay, x, routing, p, layer):
    t, d = x.shape
    h, idx, w_col, rank, cnt = routing
    n_sorted = TOP_K * t + N_EXPERTS * (EXPERT_TILE - 1)
    n_sorted = -(-n_sorted // EXPERT_TILE) * EXPERT_TILE
    pos, tile_expert, n_valid = dispatch_plan(idx, rank, cnt, n_sorted)
    xs = sc_scatter_rows(h, pos, n_sorted)
    ys = grouped_experts(xs, tile_expert, n_valid, p["moe_w_gate"], p["moe_w_up"], p["moe_w_down"], layer)
    return sc_gather_rows(ys, pos).reshape(TOP_K, t, d // 2), w_col


def _rope_angles(n, d):
    n_rows = n // GRID_W
    row = jnp.repeat(jnp.arange(n_rows), GRID_W).astype(F32)
    col = jnp.tile(jnp.arange(GRID_W), n_rows).astype(F32)
    nf = d // 4
    freqs = jnp.power(ROPE_BASE, -jnp.arange(nf, dtype=F32) / nf)
    ang = jnp.concatenate([row[:, None] * freqs, col[:, None] * freqs], axis=-1)
    return jnp.cos(ang), jnp.sin(ang)


def kernel(x_prompt, x_sample, cache_attn_k, cache_attn_v, state_ret_fwd, state_ret_bwd, c, c_ctx, w_ada, b_ada, norm_mix_g, norm_ffn_g, final_norm_g, da_w_qkv, da_lambda_q1, da_lambda_k1, da_lambda_q2, da_lambda_k2, da_subln_g, da_w_o, ret_w_qkv, ret_w_gate_fwd, ret_w_gate_bwd, ret_decay_fwd, ret_decay_bwd, ret_w_o, w_router, router_bias, moe_w_gate, moe_w_up, moe_w_down):
    b_ctx, n_ctx, d = x_prompt.shape
    b_dec, n_dec, _ = x_sample.shape
    past = cache_attn_k.shape[2]
    n_attn = cache_attn_k.shape[1]
    depth = w_ada.shape[0]
    assert b_dec + 1 <= MOD_ROWS
    lay = Layout(b_ctx, n_ctx, b_dec, n_dec)
    lay_proj = Layout(b_ctx, n_ctx, b_dec, n_dec, row_tile=2 * ROW_TILE)

    cond = jnp.zeros((MOD_ROWS, d), F32).at[0].set(c_ctx).at[1:1 + b_dec].set(c)
    mods_all = ada_modulation(cond, w_ada, b_ada)

    p = {
        "w_router_t": w_router.T.astype(BF16), "router_bias": router_bias.astype(F32),
        "moe_w_gate": moe_w_gate, "moe_w_up": moe_w_up, "moe_w_down": moe_w_down,
    }
    ret_log_decay = jnp.stack([jax.nn.log_sigmoid(ret_decay_fwd.astype(F32)),
                               jax.nn.log_sigmoid(ret_decay_bwd.astype(F32))], axis=1)
    ck_all = cache_attn_k.reshape(b_dec, n_attn, past, -1)
    cv_all = cache_attn_v.reshape(b_dec, n_attn, past, -1)

    x_parts = [x_prompt.reshape(lay.t_ctx, d), x_sample.reshape(lay.t_dec, d)]
    new_k, new_v, new_sf, new_sb = [], [], [], []
    pending_moe = None

    def first_kernel(*args):
        nonlocal x_parts, pending_moe
        outs = norm_mod_matmul(lay, x_parts, *args, pending_moe=pending_moe)
        if pending_moe is not None:
            *outs, x_joint = outs
            x_parts, pending_moe = [x_joint], None
        return outs

    for i in range(depth):
        mods = mods_all[i]
        j = i // 2
        if i % 2 == 0:
            lam_init = 0.8 - 0.6 * math.exp(-0.3 * i)
            qkw = DA_HEADS * 2 * DA_HEAD_DIM
            vw = DA_HEADS * DA_V_DIM
            cos, sin = _rope_angles(n_dec, DA_HEAD_DIM)
            rope = (jnp.tile(cos, (1, 4)), jnp.concatenate([-sin, sin, -sin, sin], axis=-1))
            q, k, v, k_ctx, v_ctx = first_kernel(
                norm_mix_g[i], mods, da_w_qkv[j],
                [(0, qkw, "rope64", math.log2(math.e) * DA_HEAD_DIM ** -0.5, BF16, "all"),
                 (qkw, qkw, "rope64", 1.0, BF16, "all"),
                 (2 * qkw, vw, "plain", 1.0, BF16, "all"),
                 (qkw, qkw, "plain", 1.0, F32, "ctx64"), (2 * qkw, vw, "plain", 1.0, F32, "ctx")],
                rope)
            lam_params = jnp.stack([da_lambda_q1[j], da_lambda_k1[j], da_lambda_q2[j], da_lambda_k2[j]])
            mix = [diff_attention(q, k, v, 0, b_ctx, n_ctx, lam_params, da_subln_g[j], lam_init,
                                  heads_per_step=DA_HEADS),
                   diff_attention(q, k, v, lay.t_ctx, b_dec, n_dec, lam_params, da_subln_g[j], lam_init,
                                  cache=(ck_all[:, j], cv_all[:, j]))]
            mix_w = da_w_o[j]
            new_k.append(k_ctx.reshape(b_ctx, n_ctx, DA_HEADS, 2, DA_HEAD_DIM))
            new_v.append(v_ctx.reshape(b_ctx, n_ctx, DA_HEADS, DA_V_DIM))
        else:
            kd = ret_w_qkv.shape[2] // 4
            dv = 2 * kd
            w_all = jnp.concatenate([ret_w_qkv[j], ret_w_gate_fwd[j], ret_w_gate_bwd[j]], axis=-1).astype(BF16)
            q, k, v, gates = first_kernel(
                norm_mix_g[i], mods, w_all,
                [(0, kd, "rope256", 1.0, BF16, "all"),
                 (kd, kd, "rope256", (kd // RET_HEADS) ** -0.5, F32, "all"),
                 (2 * kd, dv, "plain", 1.0, BF16, "all"), (2 * kd + dv, 2 * dv, "silu", 1.0, BF16, "all")],
                _rope_angles(n_dec, kd // RET_HEADS))
            ctx_seqs = math.gcd(b_ctx, max(1, n_dec // n_ctx // 2))
            o_ctx, sf, sb = retention(q, k, v, gates, 0, b_ctx, n_ctx, ret_log_decay[j], emit_state=True,
                                      seqs_per_step=ctx_seqs)
            (o_dec,) = retention(q, k, v, gates, lay.t_ctx, b_dec, n_dec, ret_log_decay[j],
                                 state=(state_ret_fwd[:, j:j + 1], state_ret_bwd[:, j:j + 1]))
            mix, mix_w = [o_ctx, o_dec], ret_w_o[j]
            new_sf.append(sf)
            new_sb.append(sb)
        x, *routing = proj_residual_route(lay_proj, x_parts, mix, mix_w, mods, norm_ffn_g[i], p["w_router_t"],
                                          p["router_bias"])
        y_pair, w_col = group_moe(lay, x, routing, p, i)
        if i == depth - 1:
            x_parts = final_combine(lay_proj, x, y_pair, w_col, mods, final_norm_g)
        else:
            x_parts, pending_moe = [x], (y_pair, w_col, mods)

    y_ctx, y_dec = x_parts
    return (y_ctx.reshape(b_ctx, n_ctx, d), y_dec.reshape(b_dec, n_dec, d),
            jnp.stack(new_k, axis=1), jnp.stack(new_v, axis=1),
            jnp.concatenate(new_sf, axis=1), jnp.concatenate(new_sb, axis=1))
```
